```python
import jax, jax.numpy as jnp
from jax import lax
import numpy as np

D_MODEL = 2048
BATCH = 8
SEQ = 4096
DEPTH = 4

CHUNK = 64
N_META = 16
N_MIXERS = 2
N_MLA = (DEPTH + N_MIXERS - 1) // N_MIXERS
N_LRU = DEPTH // N_MIXERS

MLA_HEADS = 16
Q_LORA = 512
KV_LORA = 512
QK_NOPE = 128
QK_ROPE = 64
V_HEAD = 128
ROPE_THETA = 10000.0
Q_BLOCK = 128

D_RNN = D_MODEL
RNN_BLOCKS = 16
RNN_BW = D_RNN // RNN_BLOCKS
CONV_W = 4
LRU_C = 8.0

D_FF = -(-8 * D_MODEL // (3 * 256)) * 256

RMS_EPS = 1e-6
NEG_BIG = -1e30

kernel_name = 'hybrid_mla_rglru_streaming_trunk'


def rms_norm(x, g):
    xf = x.astype(jnp.float32)
    y = xf * lax.rsqrt(jnp.mean(xf * xf, axis=-1, keepdims=True) + RMS_EPS)
    return (y * g.astype(jnp.float32)).astype(x.dtype)


def chunk_ids(n):
    pos = jnp.arange(n)
    return jnp.where(pos < N_META, 0, 1 + (pos - N_META) // CHUNK)


def apply_rope(x, cos, sin):
    xf = x.astype(jnp.float32)
    x1, x2 = jnp.split(xf, 2, axis=-1)
    out = jnp.concatenate([x1 * cos - x2 * sin, x2 * cos + x1 * sin], axis=-1)
    return out.astype(x.dtype)


def mla_mixer(h, w_in, q_norm, kv_norm, w_uq, w_ukv, w_o):
    B, T, _ = h.shape
    proj = h @ w_in
    c_q, c_kv, k_rope = jnp.split(proj, [Q_LORA, Q_LORA + KV_LORA], axis=-1)
    c_q = rms_norm(c_q, q_norm)
    c_kv = rms_norm(c_kv, kv_norm)
    q = (c_q @ w_uq).reshape(B, T, MLA_HEADS, QK_NOPE + QK_ROPE)
    q_nope, q_rope = jnp.split(q, [QK_NOPE], axis=-1)
    kv = (c_kv @ w_ukv).reshape(B, T, MLA_HEADS, QK_NOPE + V_HEAD)
    k_nope, v = jnp.split(kv, [QK_NOPE], axis=-1)

    pos = jnp.arange(T, dtype=jnp.float32)
    inv_freq = ROPE_THETA ** (-jnp.arange(0, QK_ROPE, 2, dtype=jnp.float32) / QK_ROPE)
    ang = pos[:, None] * inv_freq[None, :]
    cos, sin = jnp.cos(ang), jnp.sin(ang)
    q_rope = apply_rope(q_rope, cos[:, None, :], sin[:, None, :])
    k_rope = apply_rope(k_rope, cos, sin)

    scale = (QK_NOPE + QK_ROPE) ** -0.5
    n_blk = -(-T // Q_BLOCK)
    t_pad = n_blk * Q_BLOCK
    pad = ((0, 0), (0, t_pad - T), (0, 0), (0, 0))
    q_nope = jnp.pad(q_nope, pad)
    q_rope = jnp.pad(q_rope, pad)
    k_chunk = chunk_ids(T)
    q_chunk = chunk_ids(t_pad)

    def attend_block(i):
        s = i * Q_BLOCK
        qn = lax.dynamic_slice_in_dim(q_nope, s, Q_BLOCK, axis=1)
        qr = lax.dynamic_slice_in_dim(q_rope, s, Q_BLOCK, axis=1)
        qc = lax.dynamic_slice_in_dim(q_chunk, s, Q_BLOCK)
        scores = (jnp.einsum('bqhd,bkhd->bhqk', qn, k_nope)
                  + jnp.einsum('bqhd,bkd->bhqk', qr, k_rope)).astype(jnp.float32) * scale
        mask = k_chunk[None, :] <= qc[:, None]
        scores = jnp.where(mask[None, None], scores, NEG_BIG)
        p = jax.nn.softmax(scores, axis=-1).astype(v.dtype)
        return jnp.einsum('bhqk,bkhd->bqhd', p, v)

    out = lax.map(attend_block, jnp.arange(n_blk))
    out = jnp.moveaxis(out, 0, 1).reshape(B, t_pad, MLA_HEADS * V_HEAD)[:, :T]
    return out @ w_o


def rglru_mixer(h, w_in, conv_w, conv_b, w_ga, b_ga, w_gx, b_gx, lam, w_o):
    B, T, _ = h.shape
    xb, yb = jnp.split(h @ w_in, 2, axis=-1)
    yb = jax.nn.gelu(yb, approximate=True)
    xb = lax.conv_general_dilated(
        xb, conv_w[:, None, :], window_strides=(1,), padding=[(CONV_W - 1, 0)],
        dimension_numbers=('NWC', 'WIO', 'NWC'), feature_group_count=D_RNN) + conv_b
    xg = xb.reshape(B, T, RNN_BLOCKS, RNN_BW)
    r = jax.nn.sigmoid(jnp.einsum('btnc,ncd->btnd', xg, w_ga) + b_ga).reshape(B, T, D_RNN)
    i = jax.nn.sigmoid(jnp.einsum('btnc,ncd->btnd', xg, w_gx) + b_gx).reshape(B, T, D_RNN)
    log_a = -LRU_C * r.astype(jnp.float32) * jax.nn.softplus(-lam.astype(jnp.float32))
    a = jnp.exp(log_a)
    b = jnp.sqrt(-jnp.expm1(2.0 * log_a)) * (i * xb).astype(jnp.float32)

    def combine(lhs, rhs):
        a1, b1 = lhs
        a2, b2 = rhs
        return a1 * a2, a2 * b1 + b2

    _, hs = lax.associative_scan(combine, (a, b), axis=1)
    return (hs.astype(h.dtype) * yb) @ w_o


def swiglu(h, w_gu, w_down):
    g, u = jnp.split(h @ w_gu, 2, axis=-1)
    return (jax.nn.silu(g) * u) @ w_down


def _fwd_setup_inputs(seed: int = 0) -> dict:
    key = jax.random.key(seed)
    ks = jax.random.split(key, 24)
    f32 = jnp.float32

    def dense(k, shape, fan_in):
        return jax.random.normal(k, shape, f32) * (fan_in ** -0.5)

    def gain(k, shape):
        return 1.0 + 0.02 * jax.random.normal(k, shape, f32)

    x = jax.random.normal(ks[0], (BATCH, SEQ, D_MODEL), f32)
    meta_tokens = jax.random.normal(ks[1], (N_META, D_MODEL), f32)
    norm_mix = gain(ks[2], (DEPTH, D_MODEL))
    norm_ffn = gain(ks[3], (DEPTH, D_MODEL))
    norm_final = gain(ks[4], (D_MODEL,))

    mla_w_in = dense(ks[5], (N_MLA, D_MODEL, Q_LORA + KV_LORA + QK_ROPE), D_MODEL)
    mla_q_norm = gain(ks[6], (N_MLA, Q_LORA))
    mla_kv_norm = gain(ks[7], (N_MLA, KV_LORA))
    mla_w_uq = dense(ks[8], (N_MLA, Q_LORA, MLA_HEADS * (QK_NOPE + QK_ROPE)), Q_LORA)
    mla_w_ukv = dense(ks[9], (N_MLA, KV_LORA, MLA_HEADS * (QK_NOPE + V_HEAD)), KV_LORA)
    mla_w_o = dense(ks[10], (N_MLA, MLA_HEADS * V_HEAD, D_MODEL), MLA_HEADS * V_HEAD)

    lru_w_in = dense(ks[11], (N_LRU, D_MODEL, 2 * D_RNN), D_MODEL)
    lru_conv_w = dense(ks[12], (N_LRU, CONV_W, D_RNN), CONV_W)
    lru_conv_b = 0.01 * jax.random.normal(ks[13], (N_LRU, D_RNN), f32)
    lru_w_gate_a = dense(ks[14], (N_LRU, RNN_BLOCKS, RNN_BW, RNN_BW), RNN_BW)
    lru_b_gate_a = 0.01 * jax.random.normal(ks[15], (N_LRU, RNN_BLOCKS, RNN_BW), f32)
    lru_w_gate_x = dense(ks[16], (N_LRU, RNN_BLOCKS, RNN_BW, RNN_BW), RNN_BW)
    lru_b_gate_x = 0.01 * jax.random.normal(ks[17], (N_LRU, RNN_BLOCKS, RNN_BW), f32)
    a_c = jax.random.uniform(ks[18], (N_LRU, D_RNN), f32, 0.9, 0.999)
    a0 = a_c ** (1.0 / LRU_C)
    lru_lambda = jnp.log(a0) - jnp.log1p(-a0)
    lru_w_o = dense(ks[19], (N_LRU, D_RNN, D_MODEL), D_RNN)

    ffn_w_gu = dense(ks[20], (DEPTH, D_MODEL, 2 * D_FF), D_MODEL)
    ffn_w_down = dense(ks[21], (DEPTH, D_FF, D_MODEL), D_FF)

    return {
        'x': x, 'meta_tokens': meta_tokens,
        'norm_mix': norm_mix, 'norm_ffn': norm_ffn, 'norm_final': norm_final,
        'mla_w_in': mla_w_in, 'mla_q_norm': mla_q_norm, 'mla_kv_norm': mla_kv_norm,
        'mla_w_uq': mla_w_uq, 'mla_w_ukv': mla_w_ukv, 'mla_w_o': mla_w_o,
        'lru_w_in': lru_w_in, 'lru_conv_w': lru_conv_w, 'lru_conv_b': lru_conv_b,
        'lru_w_gate_a': lru_w_gate_a, 'lru_b_gate_a': lru_b_gate_a,
        'lru_w_gate_x': lru_w_gate_x, 'lru_b_gate_x': lru_b_gate_x,
        'lru_lambda': lru_lambda, 'lru_w_o': lru_w_o,
        'ffn_w_gu': ffn_w_gu, 'ffn_w_down': ffn_w_down,
    }


def _fwd_reference(x, meta_tokens, norm_mix, norm_ffn, norm_final,
              mla_w_in, mla_q_norm, mla_kv_norm, mla_w_uq, mla_w_ukv, mla_w_o,
              lru_w_in, lru_conv_w, lru_conv_b, lru_w_gate_a, lru_b_gate_a,
              lru_w_gate_x, lru_b_gate_x, lru_lambda, lru_w_o,
              ffn_w_gu, ffn_w_down):
    B = x.shape[0]
    meta = jnp.broadcast_to(meta_tokens.astype(x.dtype)[None], (B, N_META, D_MODEL))
    h = jnp.concatenate([meta, x], axis=1)
    for layer in range(DEPTH):
        j = layer // N_MIXERS
        hn = rms_norm(h, norm_mix[layer])
        if layer % N_MIXERS == 0:
            mix = mla_mixer(hn, mla_w_in[j], mla_q_norm[j], mla_kv_norm[j],
                            mla_w_uq[j], mla_w_ukv[j], mla_w_o[j])
        else:
            mix = rglru_mixer(hn, lru_w_in[j], lru_conv_w[j], lru_conv_b[j],
                              lru_w_gate_a[j], lru_b_gate_a[j],
                              lru_w_gate_x[j], lru_b_gate_x[j],
                              lru_lambda[j], lru_w_o[j])
        h = h + mix
        h = h + swiglu(rms_norm(h, norm_ffn[layer]), ffn_w_gu[layer], ffn_w_down[layer])
    h = rms_norm(h, norm_final)
    return h[:, N_META:]


import jax as _jax
import jax.numpy as _jnp

TWIN_FORMAT = 'train_step'
FWD_PARAMS = ['x', 'meta_tokens', 'norm_mix', 'norm_ffn', 'norm_final', 'mla_w_in', 'mla_q_norm', 'mla_kv_norm', 'mla_w_uq', 'mla_w_ukv', 'mla_w_o', 'lru_w_in', 'lru_conv_w', 'lru_conv_b', 'lru_w_gate_a', 'lru_b_gate_a', 'lru_w_gate_x', 'lru_b_gate_x', 'lru_lambda', 'lru_w_o', 'ffn_w_gu', 'ffn_w_down']
TWIN_WEIGHTS = ['meta_tokens', 'norm_mix', 'norm_ffn', 'norm_final', 'mla_w_in', 'mla_q_norm', 'mla_kv_norm', 'mla_w_uq', 'mla_w_ukv', 'mla_w_o', 'lru_w_in', 'lru_conv_w', 'lru_conv_b', 'lru_w_gate_a', 'lru_b_gate_a', 'lru_w_gate_x', 'lru_b_gate_x', 'lru_lambda', 'lru_w_o', 'ffn_w_gu', 'ffn_w_down']
TWIN_DIFF_INPUT = 'x'
TWIN_INPUTS = ['x', 'meta_tokens', 'norm_mix', 'norm_ffn', 'norm_final', 'mla_w_in', 'mla_q_norm', 'mla_kv_norm', 'mla_w_uq', 'mla_w_ukv', 'mla_w_o', 'lru_w_in', 'lru_conv_w', 'lru_conv_b', 'lru_w_gate_a', 'lru_b_gate_a', 'lru_w_gate_x', 'lru_b_gate_x', 'lru_lambda', 'lru_w_o', 'ffn_w_gu', 'ffn_w_down', 'loss_target', 'm_meta_tokens', 'm_norm_mix', 'm_norm_ffn', 'm_norm_final', 'm_mla_w_in', 'm_mla_q_norm', 'm_mla_kv_norm', 'm_mla_w_uq', 'm_mla_w_ukv', 'm_mla_w_o', 'm_lru_w_in', 'm_lru_conv_w', 'm_lru_conv_b', 'm_lru_w_gate_a', 'm_lru_b_gate_a', 'm_lru_w_gate_x', 'm_lru_b_gate_x', 'm_lru_lambda', 'm_lru_w_o', 'm_ffn_w_gu', 'm_ffn_w_down', 'v_meta_tokens', 'v_norm_mix', 'v_norm_ffn', 'v_norm_final', 'v_mla_w_in', 'v_mla_q_norm', 'v_mla_kv_norm', 'v_mla_w_uq', 'v_mla_w_ukv', 'v_mla_w_o', 'v_lru_w_in', 'v_lru_conv_w', 'v_lru_conv_b', 'v_lru_w_gate_a', 'v_lru_b_gate_a', 'v_lru_w_gate_x', 'v_lru_b_gate_x', 'v_lru_lambda', 'v_lru_w_o', 'v_ffn_w_gu', 'v_ffn_w_down']
TWIN_OUTPUTS = ['loss', 'grad_x', 'grad_meta_tokens', 'grad_norm_mix', 'grad_norm_ffn', 'grad_norm_final', 'grad_mla_w_in', 'grad_mla_q_norm', 'grad_mla_kv_norm', 'grad_mla_w_uq', 'grad_mla_w_ukv', 'grad_mla_w_o', 'grad_lru_w_in', 'grad_lru_conv_w', 'grad_lru_conv_b', 'grad_lru_w_gate_a', 'grad_lru_b_gate_a', 'grad_lru_w_gate_x', 'grad_lru_b_gate_x', 'grad_lru_lambda', 'grad_lru_w_o', 'grad_ffn_w_gu', 'grad_ffn_w_down', 'delta_meta_tokens', 'delta_norm_mix', 'delta_norm_ffn', 'delta_norm_final', 'delta_mla_w_in', 'delta_mla_q_norm', 'delta_mla_kv_norm', 'delta_mla_w_uq', 'delta_mla_w_ukv', 'delta_mla_w_o', 'delta_lru_w_in', 'delta_lru_conv_w', 'delta_lru_conv_b', 'delta_lru_w_gate_a', 'delta_lru_b_gate_a', 'delta_lru_w_gate_x', 'delta_lru_b_gate_x', 'delta_lru_lambda', 'delta_lru_w_o', 'delta_ffn_w_gu', 'delta_ffn_w_down', 'new_m_meta_tokens', 'new_m_norm_mix', 'new_m_norm_ffn', 'new_m_norm_final', 'new_m_mla_w_in', 'new_m_mla_q_norm', 'new_m_mla_kv_norm', 'new_m_mla_w_uq', 'new_m_mla_w_ukv', 'new_m_mla_w_o', 'new_m_lru_w_in', 'new_m_lru_conv_w', 'new_m_lru_conv_b', 'new_m_lru_w_gate_a', 'new_m_lru_b_gate_a', 'new_m_lru_w_gate_x', 'new_m_lru_b_gate_x', 'new_m_lru_lambda', 'new_m_lru_w_o', 'new_m_ffn_w_gu', 'new_m_ffn_w_down', 'new_v_meta_tokens', 'new_v_norm_mix', 'new_v_norm_ffn', 'new_v_norm_final', 'new_v_mla_w_in', 'new_v_mla_q_norm', 'new_v_mla_kv_norm', 'new_v_mla_w_uq', 'new_v_mla_w_ukv', 'new_v_mla_w_o', 'new_v_lru_w_in', 'new_v_lru_conv_w', 'new_v_lru_conv_b', 'new_v_lru_w_gate_a', 'new_v_lru_b_gate_a', 'new_v_lru_w_gate_x', 'new_v_lru_b_gate_x', 'new_v_lru_lambda', 'new_v_lru_w_o', 'new_v_ffn_w_gu', 'new_v_ffn_w_down']
TWIN_LEAF_KINDS = {'loss': 'loss', 'grad_x': 'grad_x', 'grad_meta_tokens': 'grad_w', 'grad_norm_mix': 'grad_w', 'grad_norm_ffn': 'grad_w', 'grad_norm_final': 'grad_w', 'grad_mla_w_in': 'grad_w', 'grad_mla_q_norm': 'grad_w', 'grad_mla_kv_norm': 'grad_w', 'grad_mla_w_uq': 'grad_w', 'grad_mla_w_ukv': 'grad_w', 'grad_mla_w_o': 'grad_w', 'grad_lru_w_in': 'grad_w', 'grad_lru_conv_w': 'grad_w', 'grad_lru_conv_b': 'grad_w', 'grad_lru_w_gate_a': 'grad_w', 'grad_lru_b_gate_a': 'grad_w', 'grad_lru_w_gate_x': 'grad_w', 'grad_lru_b_gate_x': 'grad_w', 'grad_lru_lambda': 'grad_w', 'grad_lru_w_o': 'grad_w', 'grad_ffn_w_gu': 'grad_w', 'grad_ffn_w_down': 'grad_w', 'delta_meta_tokens': 'delta_w', 'delta_norm_mix': 'delta_w', 'delta_norm_ffn': 'delta_w', 'delta_norm_final': 'delta_w', 'delta_mla_w_in': 'delta_w', 'delta_mla_q_norm': 'delta_w', 'delta_mla_kv_norm': 'delta_w', 'delta_mla_w_uq': 'delta_w', 'delta_mla_w_ukv': 'delta_w', 'delta_mla_w_o': 'delta_w', 'delta_lru_w_in': 'delta_w', 'delta_lru_conv_w': 'delta_w', 'delta_lru_conv_b': 'delta_w', 'delta_lru_w_gate_a': 'delta_w', 'delta_lru_b_gate_a': 'delta_w', 'delta_lru_w_gate_x': 'delta_w', 'delta_lru_b_gate_x': 'delta_w', 'delta_lru_lambda': 'delta_w', 'delta_lru_w_o': 'delta_w', 'delta_ffn_w_gu': 'delta_w', 'delta_ffn_w_down': 'delta_w', 'new_m_meta_tokens': 'new_m', 'new_m_norm_mix': 'new_m', 'new_m_norm_ffn': 'new_m', 'new_m_norm_final': 'new_m', 'new_m_mla_w_in': 'new_m', 'new_m_mla_q_norm': 'new_m', 'new_m_mla_kv_norm': 'new_m', 'new_m_mla_w_uq': 'new_m', 'new_m_mla_w_ukv': 'new_m', 'new_m_mla_w_o': 'new_m', 'new_m_lru_w_in': 'new_m', 'new_m_lru_conv_w': 'new_m', 'new_m_lru_conv_b': 'new_m', 'new_m_lru_w_gate_a': 'new_m', 'new_m_lru_b_gate_a': 'new_m', 'new_m_lru_w_gate_x': 'new_m', 'new_m_lru_b_gate_x': 'new_m', 'new_m_lru_lambda': 'new_m', 'new_m_lru_w_o': 'new_m', 'new_m_ffn_w_gu': 'new_m', 'new_m_ffn_w_down': 'new_m', 'new_v_meta_tokens': 'new_v', 'new_v_norm_mix': 'new_v', 'new_v_norm_ffn': 'new_v', 'new_v_norm_final': 'new_v', 'new_v_mla_w_in': 'new_v', 'new_v_mla_q_norm': 'new_v', 'new_v_mla_kv_norm': 'new_v', 'new_v_mla_w_uq': 'new_v', 'new_v_mla_w_ukv': 'new_v', 'new_v_mla_w_o': 'new_v', 'new_v_lru_w_in': 'new_v', 'new_v_lru_conv_w': 'new_v', 'new_v_lru_conv_b': 'new_v', 'new_v_lru_w_gate_a': 'new_v', 'new_v_lru_b_gate_a': 'new_v', 'new_v_lru_w_gate_x': 'new_v', 'new_v_lru_b_gate_x': 'new_v', 'new_v_lru_lambda': 'new_v', 'new_v_lru_w_o': 'new_v', 'new_v_ffn_w_gu': 'new_v', 'new_v_ffn_w_down': 'new_v'}


def _forward(args):
    return _fwd_reference(*[args[k] for k in FWD_PARAMS])


def _output_shape():
    def fwd():
        inp = _fwd_setup_inputs(0)
        return _fwd_reference(*[inp[k] for k in FWD_PARAMS])
    out = _jax.eval_shape(fwd)
    return out.shape, out.dtype

N_MICROBATCH = 1
ADAM_LR = 0.001
ADAM_B1 = 0.9
ADAM_B2 = 0.999
ADAM_EPS = 1e-08
ADAM_WD = 0.01
ADAM_STEP = 10
PER_EXAMPLE_BATCH_AXIS = {'x': 0, 'loss_target': 0}
SHARED_INPUTS = []
_WEIGHT_DTYPES = {'meta_tokens': _jnp.float32, 'norm_mix': _jnp.float32, 'norm_ffn': _jnp.float32, 'norm_final': _jnp.float32, 'mla_w_in': _jnp.float32, 'mla_q_norm': _jnp.float32, 'mla_kv_norm': _jnp.float32, 'mla_w_uq': _jnp.float32, 'mla_w_ukv': _jnp.float32, 'mla_w_o': _jnp.float32, 'lru_w_in': _jnp.float32, 'lru_conv_w': _jnp.float32, 'lru_conv_b': _jnp.float32, 'lru_w_gate_a': _jnp.float32, 'lru_b_gate_a': _jnp.float32, 'lru_w_gate_x': _jnp.float32, 'lru_b_gate_x': _jnp.float32, 'lru_lambda': _jnp.float32, 'lru_w_o': _jnp.float32, 'ffn_w_gu': _jnp.float32, 'ffn_w_down': _jnp.float32}
MOMENT_SCALE = {'meta_tokens': 6.163891e-03, 'norm_mix': 4.347834e-02, 'norm_ffn': 6.522877e-02, 'norm_final': 1.600641e+01, 'mla_w_in': 4.666571e-02, 'mla_q_norm': 3.029621e-02, 'mla_kv_norm': 6.014902e-02, 'mla_w_uq': 1.235767e-02, 'mla_w_ukv': 2.075204e-02, 'mla_w_o': 2.652047e-02, 'lru_w_in': 4.013388e-02, 'lru_conv_w': 4.445179e-02, 'lru_conv_b': 4.705751e-01, 'lru_w_gate_a': 1.282581e-02, 'lru_b_gate_a': 1.131160e-02, 'lru_w_gate_x': 2.322197e-02, 'lru_b_gate_x': 1.692582e-02, 'lru_lambda': 2.288154e-02, 'lru_w_o': 4.356401e-02, 'ffn_w_gu': 2.821535e-02, 'ffn_w_down': 4.609322e-02}


def _to_microbatches(a, axis):
    t = _jnp.moveaxis(a, axis, 0)
    t = t.reshape((N_MICROBATCH, t.shape[0] // N_MICROBATCH) + t.shape[1:])
    return _jnp.moveaxis(t, 1, axis + 1)


def setup_inputs(seed: int = 0) -> dict:
    inp = _fwd_setup_inputs(seed)
    key = _jax.random.fold_in(_jax.random.key(seed), 7919)
    shape, _ = _output_shape()
    out = dict(inp)
    out["loss_target"] = _jax.random.normal(_jax.random.fold_in(key, 0), shape, _jnp.float32)
    for i, name in enumerate(TWIN_WEIGHTS):
        w = inp[name].astype(_jnp.float32)
        if MOMENT_SCALE is None:
            s = _jnp.sqrt(_jnp.mean(_jnp.square(w)) + 1e-30)
        else:
            s = MOMENT_SCALE[name]
        km, kv = _jax.random.split(_jax.random.fold_in(key, i + 1))
        out[name] = w
        out["m_" + name] = s * _jax.random.normal(km, w.shape, _jnp.float32)
        out["v_" + name] = (s * s) * _jax.random.uniform(kv, w.shape, _jnp.float32, 0.5, 1.5)
    if N_MICROBATCH > 1:
        for name, axis in PER_EXAMPLE_BATCH_AXIS.items():
            out[name] = _to_microbatches(out[name], axis)
    return {'x': out['x'], 'meta_tokens': out['meta_tokens'], 'norm_mix': out['norm_mix'], 'norm_ffn': out['norm_ffn'], 'norm_final': out['norm_final'], 'mla_w_in': out['mla_w_in'], 'mla_q_norm': out['mla_q_norm'], 'mla_kv_norm': out['mla_kv_norm'], 'mla_w_uq': out['mla_w_uq'], 'mla_w_ukv': out['mla_w_ukv'], 'mla_w_o': out['mla_w_o'], 'lru_w_in': out['lru_w_in'], 'lru_conv_w': out['lru_conv_w'], 'lru_conv_b': out['lru_conv_b'], 'lru_w_gate_a': out['lru_w_gate_a'], 'lru_b_gate_a': out['lru_b_gate_a'], 'lru_w_gate_x': out['lru_w_gate_x'], 'lru_b_gate_x': out['lru_b_gate_x'], 'lru_lambda': out['lru_lambda'], 'lru_w_o': out['lru_w_o'], 'ffn_w_gu': out['ffn_w_gu'], 'ffn_w_down': out['ffn_w_down'], 'loss_target': out['loss_target'], 'm_meta_tokens': out['m_meta_tokens'], 'm_norm_mix': out['m_norm_mix'], 'm_norm_ffn': out['m_norm_ffn'], 'm_norm_final': out['m_norm_final'], 'm_mla_w_in': out['m_mla_w_in'], 'm_mla_q_norm': out['m_mla_q_norm'], 'm_mla_kv_norm': out['m_mla_kv_norm'], 'm_mla_w_uq': out['m_mla_w_uq'], 'm_mla_w_ukv': out['m_mla_w_ukv'], 'm_mla_w_o': out['m_mla_w_o'], 'm_lru_w_in': out['m_lru_w_in'], 'm_lru_conv_w': out['m_lru_conv_w'], 'm_lru_conv_b': out['m_lru_conv_b'], 'm_lru_w_gate_a': out['m_lru_w_gate_a'], 'm_lru_b_gate_a': out['m_lru_b_gate_a'], 'm_lru_w_gate_x': out['m_lru_w_gate_x'], 'm_lru_b_gate_x': out['m_lru_b_gate_x'], 'm_lru_lambda': out['m_lru_lambda'], 'm_lru_w_o': out['m_lru_w_o'], 'm_ffn_w_gu': out['m_ffn_w_gu'], 'm_ffn_w_down': out['m_ffn_w_down'], 'v_meta_tokens': out['v_meta_tokens'], 'v_norm_mix': out['v_norm_mix'], 'v_norm_ffn': out['v_norm_ffn'], 'v_norm_final': out['v_norm_final'], 'v_mla_w_in': out['v_mla_w_in'], 'v_mla_q_norm': out['v_mla_q_norm'], 'v_mla_kv_norm': out['v_mla_kv_norm'], 'v_mla_w_uq': out['v_mla_w_uq'], 'v_mla_w_ukv': out['v_mla_w_ukv'], 'v_mla_w_o': out['v_mla_w_o'], 'v_lru_w_in': out['v_lru_w_in'], 'v_lru_conv_w': out['v_lru_conv_w'], 'v_lru_conv_b': out['v_lru_conv_b'], 'v_lru_w_gate_a': out['v_lru_w_gate_a'], 'v_lru_b_gate_a': out['v_lru_b_gate_a'], 'v_lru_w_gate_x': out['v_lru_w_gate_x'], 'v_lru_b_gate_x': out['v_lru_b_gate_x'], 'v_lru_lambda': out['v_lru_lambda'], 'v_lru_w_o': out['v_lru_w_o'], 'v_ffn_w_gu': out['v_ffn_w_gu'], 'v_ffn_w_down': out['v_ffn_w_down']}


def _loss(weights, diff, rest, loss_target):
    with _jax.named_scope("forward"):
        args = {**rest, TWIN_DIFF_INPUT: diff, **{k: w.astype(_WEIGHT_DTYPES[k]) for k, w in weights.items()}}
        y = _forward(args)
    with _jax.named_scope("loss_head"):
        err = _jnp.square(y.astype(_jnp.float32) - loss_target)
        return 0.5 * _jnp.sum(_jnp.mean(err, axis=-1)) if err.ndim else 0.5 * err


def _adamw(w, g, m, v):
    m = ADAM_B1 * m + (1.0 - ADAM_B1) * g
    v = ADAM_B2 * v + (1.0 - ADAM_B2) * _jnp.square(g)
    m_hat = m / (1.0 - ADAM_B1 ** ADAM_STEP)
    v_hat = v / (1.0 - ADAM_B2 ** ADAM_STEP)
    delta = -ADAM_LR * (m_hat / (_jnp.sqrt(v_hat) + ADAM_EPS) + ADAM_WD * w)
    return delta, m, v


def reference(x, meta_tokens, norm_mix, norm_ffn, norm_final, mla_w_in, mla_q_norm, mla_kv_norm, mla_w_uq, mla_w_ukv, mla_w_o, lru_w_in, lru_conv_w, lru_conv_b, lru_w_gate_a, lru_b_gate_a, lru_w_gate_x, lru_b_gate_x, lru_lambda, lru_w_o, ffn_w_gu, ffn_w_down, loss_target, m_meta_tokens, m_norm_mix, m_norm_ffn, m_norm_final, m_mla_w_in, m_mla_q_norm, m_mla_kv_norm, m_mla_w_uq, m_mla_w_ukv, m_mla_w_o, m_lru_w_in, m_lru_conv_w, m_lru_conv_b, m_lru_w_gate_a, m_lru_b_gate_a, m_lru_w_gate_x, m_lru_b_gate_x, m_lru_lambda, m_lru_w_o, m_ffn_w_gu, m_ffn_w_down, v_meta_tokens, v_norm_mix, v_norm_ffn, v_norm_final, v_mla_w_in, v_mla_q_norm, v_mla_kv_norm, v_mla_w_uq, v_mla_w_ukv, v_mla_w_o, v_lru_w_in, v_lru_conv_w, v_lru_conv_b, v_lru_w_gate_a, v_lru_b_gate_a, v_lru_w_gate_x, v_lru_b_gate_x, v_lru_lambda, v_lru_w_o, v_ffn_w_gu, v_ffn_w_down):
    given = dict(x=x, meta_tokens=meta_tokens, norm_mix=norm_mix, norm_ffn=norm_ffn, norm_final=norm_final, mla_w_in=mla_w_in, mla_q_norm=mla_q_norm, mla_kv_norm=mla_kv_norm, mla_w_uq=mla_w_uq, mla_w_ukv=mla_w_ukv, mla_w_o=mla_w_o, lru_w_in=lru_w_in, lru_conv_w=lru_conv_w, lru_conv_b=lru_conv_b, lru_w_gate_a=lru_w_gate_a, lru_b_gate_a=lru_b_gate_a, lru_w_gate_x=lru_w_gate_x, lru_b_gate_x=lru_b_gate_x, lru_lambda=lru_lambda, lru_w_o=lru_w_o, ffn_w_gu=ffn_w_gu, ffn_w_down=ffn_w_down, loss_target=loss_target, m_meta_tokens=m_meta_tokens, m_norm_mix=m_norm_mix, m_norm_ffn=m_norm_ffn, m_norm_final=m_norm_final, m_mla_w_in=m_mla_w_in, m_mla_q_norm=m_mla_q_norm, m_mla_kv_norm=m_mla_kv_norm, m_mla_w_uq=m_mla_w_uq, m_mla_w_ukv=m_mla_w_ukv, m_mla_w_o=m_mla_w_o, m_lru_w_in=m_lru_w_in, m_lru_conv_w=m_lru_conv_w, m_lru_conv_b=m_lru_conv_b, m_lru_w_gate_a=m_lru_w_gate_a, m_lru_b_gate_a=m_lru_b_gate_a, m_lru_w_gate_x=m_lru_w_gate_x, m_lru_b_gate_x=m_lru_b_gate_x, m_lru_lambda=m_lru_lambda, m_lru_w_o=m_lru_w_o, m_ffn_w_gu=m_ffn_w_gu, m_ffn_w_down=m_ffn_w_down, v_meta_tokens=v_meta_tokens, v_norm_mix=v_norm_mix, v_norm_ffn=v_norm_ffn, v_norm_final=v_norm_final, v_mla_w_in=v_mla_w_in, v_mla_q_norm=v_mla_q_norm, v_mla_kv_norm=v_mla_kv_norm, v_mla_w_uq=v_mla_w_uq, v_mla_w_ukv=v_mla_w_ukv, v_mla_w_o=v_mla_w_o, v_lru_w_in=v_lru_w_in, v_lru_conv_w=v_lru_conv_w, v_lru_conv_b=v_lru_conv_b, v_lru_w_gate_a=v_lru_w_gate_a, v_lru_b_gate_a=v_lru_b_gate_a, v_lru_w_gate_x=v_lru_w_gate_x, v_lru_b_gate_x=v_lru_b_gate_x, v_lru_lambda=v_lru_lambda, v_lru_w_o=v_lru_w_o, v_ffn_w_gu=v_ffn_w_gu, v_ffn_w_down=v_ffn_w_down)
    weights = {n: given[n] for n in TWIN_WEIGHTS}
    shared = {n: given[n] for n in SHARED_INPUTS}
    per_example = {n: given[n] for n in ['x']}
    grad_fn = _jax.value_and_grad(_loss, argnums=(0, 1))

    def one_microbatch(ex, loss_target):
        ex = dict(ex)
        diff = ex.pop(TWIN_DIFF_INPUT)
        return grad_fn(weights, diff, {**shared, **ex}, loss_target)

    if N_MICROBATCH == 1:
        loss, (grad_w, grad_x) = one_microbatch(per_example, given["loss_target"])
    else:
        def body(carry, xs):
            loss_sum, grad_sum = carry
            l_k, (gw_k, gx_k) = one_microbatch(xs[0], xs[1])
            with _jax.named_scope("update"):
                return (loss_sum + l_k, _jax.tree.map(_jnp.add, grad_sum, gw_k)), gx_k

        init = (_jnp.zeros((), _jnp.float32), _jax.tree.map(_jnp.zeros_like, weights))
        (loss, grad_w), grad_x = _jax.lax.scan(body, init, (per_example, given["loss_target"]))
    with _jax.named_scope("update"):
        delta_w, new_m, new_v = {}, {}, {}
        for n in TWIN_WEIGHTS:
            delta_w[n], new_m[n], new_v[n] = _adamw(weights[n], grad_w[n], given["m_" + n], given["v_" + n])
    return (loss, grad_x, *[grad_w[n] for n in TWIN_WEIGHTS], *[delta_w[n] for n in TWIN_WEIGHTS],
            *[new_m[n] for n in TWIN_WEIGHTS], *[new_v[n] for n in TWIN_WEIGHTS])
```

```python
import functools
import math

import jax
import jax.numpy as jnp
from jax import lax
from jax.experimental import pallas as pl
from jax.experimental.pallas import tpu as pltpu

F32 = jnp.float32
BF16 = jnp.bfloat16
MESH = pl.DeviceIdType.MESH

N_META = 16
CHUNK = 64
QK_NOPE = 128
QK_ROPE = 64
V_HEAD = 128
HEAD_W = 256
ROPE_THETA = 10000.0
LRU_C = 8.0
RMS_EPS = 1e-6
NEG_BIG = -1e30
ADAM_LR, ADAM_B1, ADAM_B2, ADAM_EPS, ADAM_WD, ADAM_STEP = 0.001, 0.9, 0.999, 1e-08, 0.01, 10

LANES = 128
SUBLANES = 8
VMEM_LIMIT_BYTES = 52 * 1024 * 1024
N_DEV = 8
CHUNK_SHIFT = N_META + CHUNK - N_META - N_META


def _params(dims):
    return pltpu.CompilerParams(dimension_semantics=dims, vmem_limit_bytes=VMEM_LIMIT_BYTES)


def _pick(n, candidates):
    for c in candidates:
        if c <= n and n % c == 0:
            return c
    return n


def _sigmoid(z):
    return 1.0 / (1.0 + jnp.exp(-z))


def _gelu(x):
    c = math.sqrt(2.0 / math.pi)
    return 0.5 * x * (1.0 + jnp.tanh(c * (x + 0.044715 * x * x * x)))


def _gelu_grad(x):
    c = math.sqrt(2.0 / math.pi)
    th = jnp.tanh(c * (x + 0.044715 * x * x * x))
    return 0.5 * (1.0 + th) + 0.5 * x * (1.0 - th * th) * c * (1.0 + 3.0 * 0.044715 * x * x)


def _neg_expm1(x):
    poly = -x * (1.0 + x * (1.0 / 2.0) * (1.0 + x * (1.0 / 3.0) * (1.0 + x * (1.0 / 4.0) * (
        1.0 + x * (1.0 / 5.0) * (1.0 + x * (1.0 / 6.0) * (1.0 + x * (1.0 / 7.0)))))))
    return jnp.where(x > -0.25, poly, 1.0 - jnp.exp(x))


def _softplus_neg(lam):
    e = jnp.exp(-jnp.abs(lam))
    log1p = jnp.where(e > 1e-4, jnp.log(1.0 + e), e * (1.0 - e * (0.5 - e * (1.0 / 3.0))))
    return jnp.maximum(-lam, 0.0) + log1p


def _rot_half(x):
    lane = lax.broadcasted_iota(jnp.int32, x.shape, 1)
    first = (lane % QK_ROPE) < (QK_ROPE // 2)
    return jnp.where(first, -pltpu.roll(x, LANES - QK_ROPE // 2, 1), pltpu.roll(x, QK_ROPE // 2, 1))


def _rope(x, cos, sin):
    return x * cos + _rot_half(x) * sin


def _unrope(g, cos, sin):
    return g * cos - _rot_half(g) * sin


def _visible(row0, nrows, col0, ncols, n_keys):
    row = row0 + lax.broadcasted_iota(jnp.int32, (nrows, ncols), 0)
    col = col0 + lax.broadcasted_iota(jnp.int32, (nrows, ncols), 1)
    return (((col + CHUNK_SHIFT) >> 6) <= ((row + CHUNK_SHIFT) >> 6)) & (col < n_keys)


def _key_limit(i, tq, t_all):
    return jnp.minimum((((tq * (i + 1) + CHUNK_SHIFT - 1) // CHUNK) + 1) * CHUNK - CHUNK_SHIFT, t_all)


def _first_query_tile(j, tq, tk):
    y = (j * tk + CHUNK_SHIFT) // CHUNK
    return jnp.maximum((CHUNK * y - (CHUNK_SHIFT - 1) + tq - 1) // tq - 1, 0)


def _mm_nn(a, b, *, name, out_dtype, tm, tn, tk, b_blocked=False, res=None, epilogue=None, extras=()):
    m_all, k_all = a.shape
    if b_blocked:
        g_all, kb, nb = b.shape
        n_all = g_all * nb
        assert nb % tn == 0
        r = nb // tn
        b_spec = pl.BlockSpec((None, tk, tn), lambda j, i, k: (j // r, k, j % r))
    else:
        kb, n_all = b.shape
        b_spec = pl.BlockSpec((tk, tn), lambda j, i, k: (k, j))
    assert kb == k_all and m_all % tm == 0 and n_all % tn == 0 and k_all % tk == 0
    nm, nn, nk = m_all // tm, n_all // tn, k_all // tk
    in_specs = [pl.BlockSpec((tm, tk), lambda j, i, k: (i, k)), b_spec]
    operands = [a, b]
    has_res = res is not None
    if has_res:
        in_specs.append(pl.BlockSpec((tm, tn), lambda j, i, k: (i, j)))
        operands.append(res)
    for e in extras:
        in_specs.append(pl.BlockSpec((tm, e.shape[1]), lambda j, i, k: (i, 0)))
        operands.append(e)
    n_ex = len(extras)

    def body(*refs):
        a_ref, b_ref = refs[0], refs[1]
        pos = 2
        res_ref = None
        if has_res:
            res_ref = refs[pos]
            pos += 1
        ex_refs = refs[pos:pos + n_ex]
        pos += n_ex
        o_ref = refs[pos]
        acc_ref = refs[pos + 1] if nk > 1 else None

        def finish(acc):
            if has_res:
                acc = acc + res_ref[...]
            if epilogue is not None:
                acc = epilogue(acc, *ex_refs)
            o_ref[...] = acc.astype(o_ref.dtype)

        prod = jnp.dot(a_ref[...], b_ref[...], preferred_element_type=F32)
        if nk == 1:
            finish(prod)
        else:
            k = pl.program_id(2)

            @pl.when(k == 0)
            def _():
                acc_ref[...] = prod

            @pl.when(k > 0)
            def _():
                acc_ref[...] += prod

            @pl.when(k == nk - 1)
            def _():
                finish(acc_ref[...])

    return pl.pallas_call(
        body, name=name, grid=(nn, nm, nk), in_specs=in_specs,
        out_specs=pl.BlockSpec((tm, tn), lambda j, i, k: (i, j)),
        out_shape=jax.ShapeDtypeStruct((m_all, n_all), out_dtype),
        scratch_shapes=[pltpu.VMEM((tm, tn), F32)] if nk > 1 else [],
        compiler_params=_params(("parallel", "parallel", "arbitrary")),
    )(*operands)


def _mm_nt(a, b, *, name, out_dtype, tm, tn, tk, b_blocked=False):
    m_all, k_all = a.shape
    if b_blocked:
        g_all, n_all, nb = b.shape
        assert g_all * nb == k_all and nb % tk == 0
        r = nb // tk
        b_spec = pl.BlockSpec((None, tn, tk), lambda j, i, k: (k // r, j, k % r))
    else:
        n_all, kb = b.shape
        assert kb == k_all
        b_spec = pl.BlockSpec((tn, tk), lambda j, i, k: (j, k))
    assert m_all % tm == 0 and n_all % tn == 0 and k_all % tk == 0
    nm, nn, nk = m_all // tm, n_all // tn, k_all // tk

    def body(a_ref, b_ref, o_ref, *scratch):
        prod = lax.dot_general(a_ref[...], b_ref[...], (((1,), (1,)), ((), ())), preferred_element_type=F32)
        if nk == 1:
            o_ref[...] = prod.astype(o_ref.dtype)
        else:
            acc_ref = scratch[0]
            k = pl.program_id(2)

            @pl.when(k == 0)
            def _():
                acc_ref[...] = prod

            @pl.when(k > 0)
            def _():
                acc_ref[...] += prod

            @pl.when(k == nk - 1)
            def _():
                o_ref[...] = acc_ref[...].astype(o_ref.dtype)

    return pl.pallas_call(
        body, name=name, grid=(nn, nm, nk),
        in_specs=[pl.BlockSpec((tm, tk), lambda j, i, k: (i, k)), b_spec],
        out_specs=pl.BlockSpec((tm, tn), lambda j, i, k: (i, j)),
        out_shape=jax.ShapeDtypeStruct((m_all, n_all), out_dtype),
        scratch_shapes=[pltpu.VMEM((tm, tn), F32)] if nk > 1 else [],
        compiler_params=_params(("parallel", "parallel", "arbitrary")),
    )(a, b)


def _mm_tn(a, b, *, name, out_dtype, tm, tn, tk, out_block=None):
    t_all, m_all = a.shape
    tb, n_all = b.shape
    assert tb == t_all and m_all % tm == 0 and n_all % tn == 0 and t_all % tk == 0
    nm, nn, nk = m_all // tm, n_all // tn, t_all // tk
    if out_block is None:
        out_shape = jax.ShapeDtypeStruct((m_all, n_all), out_dtype)
        out_spec = pl.BlockSpec((tm, tn), lambda i, j, k: (i, j))
    else:
        assert out_block % tn == 0 and n_all % out_block == 0
        r = out_block // tn
        out_shape = jax.ShapeDtypeStruct((n_all // out_block, m_all, out_block), out_dtype)
        out_spec = pl.BlockSpec((None, tm, tn), lambda i, j, k: (j // r, i, j % r))

    def body(a_ref, b_ref, o_ref, *scratch):
        prod = lax.dot_general(a_ref[...], b_ref[...], (((0,), (0,)), ((), ())), preferred_element_type=F32)
        if nk == 1:
            o_ref[...] = prod.astype(o_ref.dtype)
        else:
            acc_ref = scratch[0]
            k = pl.program_id(2)

            @pl.when(k == 0)
            def _():
                acc_ref[...] = prod

            @pl.when(k > 0)
            def _():
                acc_ref[...] += prod

            @pl.when(k == nk - 1)
            def _():
                o_ref[...] = acc_ref[...].astype(o_ref.dtype)

    return pl.pallas_call(
        body, name=name, grid=(nm, nn, nk),
        in_specs=[pl.BlockSpec((tk, tm), lambda i, j, k: (k, i)), pl.BlockSpec((tk, tn), lambda i, j, k: (k, j))],
        out_specs=out_spec, out_shape=out_shape,
        scratch_shapes=[pltpu.VMEM((tm, tn), F32)] if nk > 1 else [],
        compiler_params=_params(("parallel", "parallel", "arbitrary")),
    )(a, b)


def _rmsnorm_fwd(x, g, *, name, tm):
    t_all, d = x.shape

    def body(x_ref, g_ref, o_ref):
        xv = x_ref[...]
        rstd = lax.rsqrt(jnp.mean(xv * xv, axis=-1, keepdims=True) + RMS_EPS)
        o_ref[...] = (xv * rstd * g_ref[...]).astype(o_ref.dtype)

    return pl.pallas_call(
        body, name=name, grid=(t_all // tm,),
        in_specs=[pl.BlockSpec((tm, d), lambda i: (i, 0)), pl.BlockSpec((1, d), lambda i: (0, 0))],
        out_specs=pl.BlockSpec((tm, d), lambda i: (i, 0)),
        out_shape=jax.ShapeDtypeStruct((t_all, d), BF16),
        compiler_params=_params(("parallel",)),
    )(x, g)


def _rms_bwd_math(dy, xv, g):
    rstd = lax.rsqrt(jnp.mean(xv * xv, axis=-1, keepdims=True) + RMS_EPS)
    xhat = xv * rstd
    dxh = dy * g
    dx = rstd * (dxh - xhat * jnp.mean(dxh * xhat, axis=-1, keepdims=True))
    return dx, jnp.sum(dy * xhat, axis=0, keepdims=True)


def _rmsnorm_bwd(dy, x, g, res, *, name, tm):
    t_all, d = x.shape

    def body(dy_ref, x_ref, g_ref, res_ref, dx_ref, dxb_ref, dg_ref):
        dx, dg = _rms_bwd_math(dy_ref[...], x_ref[...], g_ref[...])
        tot = res_ref[...] + dx
        dx_ref[...] = tot
        dxb_ref[...] = tot.astype(BF16)

        @pl.when(pl.program_id(0) == 0)
        def _():
            dg_ref[...] = dg

        @pl.when(pl.program_id(0) > 0)
        def _():
            dg_ref[...] += dg

    row = pl.BlockSpec((tm, d), lambda i: (i, 0))
    vec = pl.BlockSpec((1, d), lambda i: (0, 0))
    return pl.pallas_call(
        body, name=name, grid=(t_all // tm,),
        in_specs=[row, row, vec, row], out_specs=[row, row, vec],
        out_shape=[jax.ShapeDtypeStruct((t_all, d), F32), jax.ShapeDtypeStruct((t_all, d), BF16),
                   jax.ShapeDtypeStruct((1, d), F32)],
        compiler_params=_params(("arbitrary",)),
    )(dy, x, g, res)


def _loss_head(h, target, g, *, name, tm, n_real):
    t_all, d = h.shape

    def body(h_ref, t_ref, g_ref, loss_ref, dx_ref, dxb_ref, dg_ref):
        i = pl.program_id(0)
        xv = h_ref[...]
        gv = g_ref[...]
        rstd = lax.rsqrt(jnp.mean(xv * xv, axis=-1, keepdims=True) + RMS_EPS)
        y = xv * rstd * gv
        row = i * tm + lax.broadcasted_iota(jnp.int32, (tm, 1), 0)
        valid = (row >= N_META) & (row < n_real)
        err = jnp.where(valid, y - t_ref[...], 0.0)
        part = 0.5 * jnp.sum(jnp.mean(err * err, axis=-1, keepdims=True), axis=0, keepdims=True)
        dx, dg = _rms_bwd_math(err * (1.0 / d), xv, gv)
        dx_ref[...] = dx
        dxb_ref[...] = dx.astype(BF16)

        @pl.when(i == 0)
        def _():
            dg_ref[...] = dg
            loss_ref[...] = jnp.broadcast_to(part, loss_ref.shape)

        @pl.when(i > 0)
        def _():
            dg_ref[...] += dg
            loss_ref[...] += jnp.broadcast_to(part, loss_ref.shape)

    row = pl.BlockSpec((tm, d), lambda i: (i, 0))
    vec = pl.BlockSpec((1, d), lambda i: (0, 0))
    return pl.pallas_call(
        body, name=name, grid=(t_all // tm,),
        in_specs=[row, row, vec],
        out_specs=[pl.BlockSpec((1, LANES), lambda i: (0, 0)), row, row, vec],
        out_shape=[jax.ShapeDtypeStruct((1, LANES), F32), jax.ShapeDtypeStruct((t_all, d), F32),
                   jax.ShapeDtypeStruct((t_all, d), BF16), jax.ShapeDtypeStruct((1, d), F32)],
        compiler_params=_params(("arbitrary",)),
    )(h, target, g)


def _swiglu_fwd(gu, *, name, tm, tc):
    t_all, f2 = gu.shape
    f = f2 // 2
    nc = f // tc

    def body(g_ref, u_ref, o_ref):
        gv = g_ref[...]
        o_ref[...] = (gv * _sigmoid(gv) * u_ref[...]).astype(o_ref.dtype)

    return pl.pallas_call(
        body, name=name, grid=(t_all // tm, nc),
        in_specs=[pl.BlockSpec((tm, tc), lambda i, j: (i, j)), pl.BlockSpec((tm, tc), lambda i, j: (i, nc + j))],
        out_specs=pl.BlockSpec((tm, tc), lambda i, j: (i, j)),
        out_shape=jax.ShapeDtypeStruct((t_all, f), BF16),
        compiler_params=_params(("parallel", "parallel")),
    )(gu, gu)


def _swiglu_bwd(dact, gu, *, name, tm, tc):
    t_all, f2 = gu.shape
    f = f2 // 2
    nc = f // tc

    def body(d_ref, g_ref, u_ref, o_ref):
        jj = pl.program_id(1)
        gv = g_ref[...]
        sg = _sigmoid(gv)
        dv = d_ref[...]

        @pl.when(jj < nc)
        def _():
            o_ref[...] = (dv * u_ref[...] * (sg * (1.0 + gv * (1.0 - sg)))).astype(o_ref.dtype)

        @pl.when(jj >= nc)
        def _():
            o_ref[...] = (dv * gv * sg).astype(o_ref.dtype)

    return pl.pallas_call(
        body, name=name, grid=(t_all // tm, 2 * nc),
        in_specs=[pl.BlockSpec((tm, tc), lambda i, j: (i, j % nc)), pl.BlockSpec((tm, tc), lambda i, j: (i, j % nc)),
                  pl.BlockSpec((tm, tc), lambda i, j: (i, nc + j % nc))],
        out_specs=pl.BlockSpec((tm, tc), lambda i, j: (i, j)),
        out_shape=jax.ShapeDtypeStruct((t_all, f2), BF16),
        compiler_params=_params(("parallel", "parallel")),
    )(dact, gu, gu)


def _mla_prep_fwd(proj, qn, kvn, cos, sin, *, name, tm, lq, lkv):
    t_all, w = proj.shape

    def body(p_ref, qn_ref, kvn_ref, cos_ref, sin_ref, cq_ref, ckv_ref, kr_ref):
        pv = p_ref[...]
        xq = pv[:, :lq]
        xkv = pv[:, lq:lq + lkv]
        cq_ref[...] = (xq * lax.rsqrt(jnp.mean(xq * xq, axis=-1, keepdims=True) + RMS_EPS) * qn_ref[...]).astype(BF16)
        ckv_ref[...] = (xkv * lax.rsqrt(jnp.mean(xkv * xkv, axis=-1, keepdims=True) + RMS_EPS) * kvn_ref[...]).astype(BF16)
        kr_ref[...] = _rope(pv[:, lq + lkv:], cos_ref[...], sin_ref[...]).astype(BF16)

    def row(width):
        return pl.BlockSpec((tm, width), lambda i: (i, 0))

    def vec(width):
        return pl.BlockSpec((1, width), lambda i: (0, 0))

    return pl.pallas_call(
        body, name=name, grid=(t_all // tm,),
        in_specs=[row(w), vec(lq), vec(lkv), row(LANES), row(LANES)],
        out_specs=[row(lq), row(lkv), row(LANES)],
        out_shape=[jax.ShapeDtypeStruct((t_all, lq), BF16), jax.ShapeDtypeStruct((t_all, lkv), BF16),
                   jax.ShapeDtypeStruct((t_all, LANES), BF16)],
        compiler_params=_params(("parallel",)),
    )(proj, qn, kvn, cos, sin)


def _mla_prep_bwd(dcq, dckv, dkr_h, proj, qn, kvn, cos, sin, *, name, tm, lq, lkv):
    t_all, w = proj.shape
    n_heads = dkr_h.shape[0]

    def body(dcq_ref, dckv_ref, dkr_ref, p_ref, qn_ref, kvn_ref, cos_ref, sin_ref, dp_ref, dqn_ref, dkvn_ref):
        pv = p_ref[...]
        dxq, dqn = _rms_bwd_math(dcq_ref[...], pv[:, :lq], qn_ref[...])
        dxkv, dkvn = _rms_bwd_math(dckv_ref[...], pv[:, lq:lq + lkv], kvn_ref[...])
        dkr = dkr_ref[0]
        for hh in range(1, n_heads):
            dkr = dkr + dkr_ref[hh]
        dkr = _unrope(dkr, cos_ref[...], sin_ref[...])
        dp_ref[...] = jnp.concatenate([dxq, dxkv, dkr], axis=1).astype(BF16)

        @pl.when(pl.program_id(0) == 0)
        def _():
            dqn_ref[...] = dqn
            dkvn_ref[...] = dkvn

        @pl.when(pl.program_id(0) > 0)
        def _():
            dqn_ref[...] += dqn
            dkvn_ref[...] += dkvn

    def row(width):
        return pl.BlockSpec((tm, width), lambda i: (i, 0))

    def vec(width):
        return pl.BlockSpec((1, width), lambda i: (0, 0))

    return pl.pallas_call(
        body, name=name, grid=(t_all // tm,),
        in_specs=[row(lq), row(lkv), pl.BlockSpec((n_heads, tm, LANES), lambda i: (0, i, 0)), row(w),
                  vec(lq), vec(lkv), row(LANES), row(LANES)],
        out_specs=[row(w), vec(lq), vec(lkv)],
        out_shape=[jax.ShapeDtypeStruct((t_all, w), BF16), jax.ShapeDtypeStruct((1, lq), F32),
                   jax.ShapeDtypeStruct((1, lkv), F32)],
        compiler_params=_params(("arbitrary",)),
    )(dcq, dckv, dkr_h, proj, qn, kvn, cos, sin)


def _rope_q_epilogue(acc, cos_ref, sin_ref):
    parts = []
    for g in range(acc.shape[1] // LANES):
        blk = acc[:, g * LANES:(g + 1) * LANES]
        parts.append(_rope(blk, cos_ref[...], sin_ref[...]) if g % 2 == 1 else blk)
    return jnp.concatenate(parts, axis=1)


def _scores(q_ref, kv_ref, kr_ref, scale):
    k = jnp.concatenate([kv_ref[:, :QK_NOPE], kr_ref[...]], axis=1)
    s = lax.dot_general(q_ref[...], k, (((1,), (1,)), ((), ())), preferred_element_type=F32) * scale
    return s, k


def _attn_fwd(q, kv, kr, *, name, n_heads, tq, tk, scale, n_keys):
    t_all = q.shape[0]
    nq, nk = t_all // tq, t_all // tk

    def last_kv(i):
        return (_key_limit(i, tq, t_all) - 1) // tk

    def body(q_ref, kv_ref, kr_ref, o_ref, lse_ref, m_scr, l_scr, acc_scr):
        i, j = pl.program_id(1), pl.program_id(2)

        @pl.when(j == 0)
        def _():
            m_scr[...] = jnp.full(m_scr.shape, NEG_BIG, F32)
            l_scr[...] = jnp.zeros(l_scr.shape, F32)
            acc_scr[...] = jnp.zeros(acc_scr.shape, F32)

        @pl.when(j <= last_kv(i))
        def _():
            s, _ = _scores(q_ref, kv_ref, kr_ref, scale)
            s = jnp.where(_visible(i * tq, tq, j * tk, tk, n_keys), s, NEG_BIG)
            m_prev = m_scr[...]
            m_new = jnp.maximum(m_prev, jnp.max(s, axis=-1, keepdims=True))
            alpha = jnp.exp(m_prev - m_new)
            p = jnp.exp(s - m_new)
            l_scr[...] = alpha * l_scr[...] + jnp.sum(p, axis=-1, keepdims=True)
            acc_scr[...] = alpha * acc_scr[...] + jnp.dot(p.astype(BF16), kv_ref[:, QK_NOPE:],
                                                          preferred_element_type=F32)
            m_scr[...] = m_new

        @pl.when(j == nk - 1)
        def _():
            o_ref[...] = (acc_scr[...] / l_scr[...]).astype(o_ref.dtype)
            lse_ref[...] = jnp.broadcast_to(m_scr[...] + jnp.log(l_scr[...]), lse_ref.shape)

    return pl.pallas_call(
        body, name=name, grid=(n_heads, nq, nk),
        in_specs=[pl.BlockSpec((tq, HEAD_W), lambda h, i, j: (i, h)),
                  pl.BlockSpec((tk, HEAD_W), lambda h, i, j: (jnp.minimum(j, last_kv(i)), h)),
                  pl.BlockSpec((tk, LANES), lambda h, i, j: (jnp.minimum(j, last_kv(i)), 0))],
        out_specs=[pl.BlockSpec((tq, V_HEAD), lambda h, i, j: (i, h)),
                   pl.BlockSpec((None, tq, LANES), lambda h, i, j: (h, i, 0))],
        out_shape=[jax.ShapeDtypeStruct((t_all, n_heads * V_HEAD), BF16),
                   jax.ShapeDtypeStruct((n_heads, t_all, LANES), F32)],
        scratch_shapes=[pltpu.VMEM((tq, 1), F32), pltpu.VMEM((tq, 1), F32), pltpu.VMEM((tq, V_HEAD), F32)],
        compiler_params=_params(("parallel", "parallel", "arbitrary")),
    )(q, kv, kr)


def _attn_bwd_dq(q, kv, kr, o, lse, do, cos, sin, *, name, n_heads, tq, tk, scale, n_keys):
    t_all = q.shape[0]
    nq, nk = t_all // tq, t_all // tk

    def last_kv(i):
        return (_key_limit(i, tq, t_all) - 1) // tk

    def body(q_ref, kv_ref, kr_ref, o_ref, lse_ref, do_ref, cos_ref, sin_ref, dq_ref, acc_scr, delta_scr):
        i, j = pl.program_id(1), pl.program_id(2)

        @pl.when(j == 0)
        def _():
            acc_scr[...] = jnp.zeros(acc_scr.shape, F32)
            delta_scr[...] = jnp.sum(do_ref[...].astype(F32) * o_ref[...].astype(F32), axis=-1, keepdims=True)

        @pl.when(j <= last_kv(i))
        def _():
            s, k = _scores(q_ref, kv_ref, kr_ref, scale)
            s = jnp.where(_visible(i * tq, tq, j * tk, tk, n_keys), s, NEG_BIG)
            p = jnp.exp(s - lse_ref[:, :1])
            dp = lax.dot_general(do_ref[...], kv_ref[:, QK_NOPE:], (((1,), (1,)), ((), ())), preferred_element_type=F32)
            ds = (p * (dp - delta_scr[...]) * scale).astype(BF16)
            acc_scr[...] += jnp.dot(ds, k, preferred_element_type=F32)

        @pl.when(j == nk - 1)
        def _():
            dq = acc_scr[...]
            dq_ref[...] = jnp.concatenate(
                [dq[:, :QK_NOPE], _unrope(dq[:, QK_NOPE:], cos_ref[...], sin_ref[...])], axis=1).astype(dq_ref.dtype)

    def qrow(width):
        return pl.BlockSpec((tq, width), lambda h, i, j: (i, h))

    return pl.pallas_call(
        body, name=name, grid=(n_heads, nq, nk),
        in_specs=[qrow(HEAD_W),
                  pl.BlockSpec((tk, HEAD_W), lambda h, i, j: (jnp.minimum(j, last_kv(i)), h)),
                  pl.BlockSpec((tk, LANES), lambda h, i, j: (jnp.minimum(j, last_kv(i)), 0)),
                  qrow(V_HEAD),
                  pl.BlockSpec((None, tq, LANES), lambda h, i, j: (h, i, 0)),
                  qrow(V_HEAD),
                  pl.BlockSpec((tq, LANES), lambda h, i, j: (i, 0)),
                  pl.BlockSpec((tq, LANES), lambda h, i, j: (i, 0))],
        out_specs=qrow(HEAD_W),
        out_shape=jax.ShapeDtypeStruct((t_all, n_heads * HEAD_W), BF16),
        scratch_shapes=[pltpu.VMEM((tq, HEAD_W), F32), pltpu.VMEM((tq, 1), F32)],
        compiler_params=_params(("parallel", "parallel", "arbitrary")),
    )(q, kv, kr, o, lse, do, cos, sin)


def _attn_bwd_dkv(q, kv, kr, o, lse, do, *, name, n_heads, tq, tk, scale, n_keys):
    t_all = q.shape[0]
    nq, nk = t_all // tq, t_all // tk

    def first_q(j):
        return _first_query_tile(j, tq, tk)

    def body(q_ref, kv_ref, kr_ref, o_ref, lse_ref, do_ref, dkv_ref, dkr_ref, dk_scr, dv_scr):
        j, i = pl.program_id(1), pl.program_id(2)

        @pl.when(i == 0)
        def _():
            dk_scr[...] = jnp.zeros(dk_scr.shape, F32)
            dv_scr[...] = jnp.zeros(dv_scr.shape, F32)

        @pl.when(i >= first_q(j))
        def _():
            s, _ = _scores(q_ref, kv_ref, kr_ref, scale)
            s = jnp.where(_visible(i * tq, tq, j * tk, tk, n_keys), s, NEG_BIG)
            p = jnp.exp(s - lse_ref[:, :1])
            dov = do_ref[...]
            delta = jnp.sum(dov.astype(F32) * o_ref[...].astype(F32), axis=-1, keepdims=True)
            dv_scr[...] += lax.dot_general(p.astype(BF16), dov, (((0,), (0,)), ((), ())), preferred_element_type=F32)
            dp = lax.dot_general(dov, kv_ref[:, QK_NOPE:], (((1,), (1,)), ((), ())), preferred_element_type=F32)
            ds = (p * (dp - delta) * scale).astype(BF16)
            dk_scr[...] += lax.dot_general(ds, q_ref[...], (((0,), (0,)), ((), ())), preferred_element_type=F32)

        @pl.when(i == nq - 1)
        def _():
            dk = dk_scr[...]
            dkv_ref[...] = jnp.concatenate([dk[:, :QK_NOPE], dv_scr[...]], axis=1).astype(dkv_ref.dtype)
            dkr_ref[...] = dk[:, QK_NOPE:]

    def qrow(width):
        return pl.BlockSpec((tq, width), lambda h, j, i: (jnp.maximum(i, first_q(j)), h))

    return pl.pallas_call(
        body, name=name, grid=(n_heads, nk, nq),
        in_specs=[qrow(HEAD_W),
                  pl.BlockSpec((tk, HEAD_W), lambda h, j, i: (j, h)),
                  pl.BlockSpec((tk, LANES), lambda h, j, i: (j, 0)),
                  qrow(V_HEAD),
                  pl.BlockSpec((None, tq, LANES), lambda h, j, i: (h, jnp.maximum(i, first_q(j)), 0)),
                  qrow(V_HEAD)],
        out_specs=[pl.BlockSpec((tk, HEAD_W), lambda h, j, i: (j, h)),
                   pl.BlockSpec((None, tk, LANES), lambda h, j, i: (h, j, 0))],
        out_shape=[jax.ShapeDtypeStruct((t_all, n_heads * HEAD_W), BF16),
                   jax.ShapeDtypeStruct((n_heads, t_all, LANES), F32)],
        scratch_shapes=[pltpu.VMEM((tk, HEAD_W), F32), pltpu.VMEM((tk, V_HEAD), F32)],
        compiler_params=_params(("parallel", "parallel", "arbitrary")),
    )(q, kv, kr, o, lse, do)


LRU_ROWS = 128


def _shifted_back(ref, t0, rows, shift_max):
    main = ref[pl.ds(t0, rows), :]
    prev = ref[pl.ds(pl.multiple_of(jnp.maximum(t0 - SUBLANES, 0), SUBLANES), SUBLANES), :]
    prev = jnp.where(t0 > 0, prev, 0.0)
    ext = jnp.concatenate([prev, main], axis=0)
    return [main] + [pltpu.roll(ext, s, 0)[SUBLANES:, :] for s in range(1, shift_max + 1)]


def _conv_fwd(xp_ref, t0, rows, cw, cb):
    sh = _shifted_back(xp_ref, t0, rows, 3)
    out = cb + cw[3:4, :] * sh[0]
    for k in range(3):
        out = out + cw[k:k + 1, :] * sh[3 - k]
    return out, sh


def _lru_gates(xb, wga, bga, wgx, bgx, sp):
    xbb = xb.astype(BF16)
    r = _sigmoid(jnp.dot(xbb, wga, preferred_element_type=F32) + bga)
    ig = _sigmoid(jnp.dot(xbb, wgx, preferred_element_type=F32) + bgx)
    la = -LRU_C * r * sp
    a = jnp.exp(la)
    s = jnp.sqrt(_neg_expm1(2.0 * la))
    return xbb, r, ig, a, s


def _lru_fwd(xy, conv_w, conv_b, wga, bga, wgx, bgx, lam, *, name):
    t_all = xy.shape[0]
    dr = xy.shape[1] // 2
    c = LANES
    nblk = dr // c
    rows = LRU_ROWS
    nt = t_all // rows

    def body(xp_ref, yp_ref, cw_ref, cb_ref, wga_ref, bga_ref, wgx_ref, bgx_ref, lam_ref, hs_ref, hsy_ref, a_scr, b_scr):
        cw, cb = cw_ref[...], cb_ref[...]
        sp = _softplus_neg(lam_ref[...])

        def gates(t, carry):
            t0 = pl.multiple_of(t * rows, rows)
            xb, _ = _conv_fwd(xp_ref, t0, rows, cw, cb)
            _, _, ig, a, s = _lru_gates(xb, wga_ref[0], bga_ref[...], wgx_ref[0], bgx_ref[...], sp)
            a_scr[pl.ds(t0, rows), :] = a
            b_scr[pl.ds(t0, rows), :] = s * (ig * xb)
            return carry

        lax.fori_loop(0, nt, gates, 0)

        def scan(g, h):
            t0 = pl.multiple_of(g * SUBLANES, SUBLANES)
            av, bv = a_scr[pl.ds(t0, SUBLANES), :], b_scr[pl.ds(t0, SUBLANES), :]
            out = []
            for rr in range(SUBLANES):
                h = av[rr:rr + 1, :] * h + bv[rr:rr + 1, :]
                out.append(h)
            hs_ref[pl.ds(t0, SUBLANES), :] = jnp.concatenate(out, axis=0)
            return h

        lax.fori_loop(0, t_all // SUBLANES, scan, jnp.zeros((1, c), F32))

        def gate_out(t, carry):
            t0 = pl.multiple_of(t * rows, rows)
            hsy_ref[pl.ds(t0, rows), :] = (hs_ref[pl.ds(t0, rows), :] * _gelu(yp_ref[pl.ds(t0, rows), :])).astype(BF16)
            return carry

        lax.fori_loop(0, nt, gate_out, 0)

    col = pl.BlockSpec((t_all, c), lambda b: (0, b))
    vec = pl.BlockSpec((1, c), lambda b: (0, b))
    wsp = pl.BlockSpec((1, c, c), lambda b: (b, 0, 0))
    return pl.pallas_call(
        body, name=name, grid=(nblk,),
        in_specs=[col, pl.BlockSpec((t_all, c), lambda b: (0, nblk + b)), pl.BlockSpec((4, c), lambda b: (0, b)), vec,
                  wsp, vec, wsp, vec, vec],
        out_specs=[col, col],
        out_shape=[jax.ShapeDtypeStruct((t_all, dr), F32), jax.ShapeDtypeStruct((t_all, dr), BF16)],
        scratch_shapes=[pltpu.VMEM((t_all, c), F32), pltpu.VMEM((t_all, c), F32)],
        compiler_params=_params(("parallel",)),
    )(xy, xy, conv_w, conv_b, wga, bga, wgx, bgx, lam)


def _lru_bwd(xy, hs, dhsy, conv_w, conv_b, wga, bga, wgx, bgx, lam, *, name):
    t_all = xy.shape[0]
    dr = xy.shape[1] // 2
    c = LANES
    nblk = dr // c
    rows = LRU_ROWS
    nt = t_all // rows
    ng = t_all // SUBLANES

    def body(xp_ref, yp_ref, hs_ref, dh_ref, cw_ref, cb_ref, wga_ref, bga_ref, wgx_ref, bgx_ref, lam_ref,
             dxp_ref, dyp_ref, dcw_ref, dcb_ref, dwga_ref, dbga_ref, dwgx_ref, dbgx_ref, dlam_ref,
             xb_scr, r_scr, i_scr, a_scr, lam_scr, da_scr):
        cw, cb = cw_ref[...], cb_ref[...]
        lamv = lam_ref[...]
        sp = _softplus_neg(lamv)
        sig_neg = 1.0 / (1.0 + jnp.exp(lamv))
        wga_v, wgx_v = wga_ref[0], wgx_ref[0]

        def recompute(t, carry):
            t0 = pl.multiple_of(t * rows, rows)
            xb, _ = _conv_fwd(xp_ref, t0, rows, cw, cb)
            _, r, ig, a, _ = _lru_gates(xb, wga_v, bga_ref[...], wgx_v, bgx_ref[...], sp)
            xb_scr[pl.ds(t0, rows), :] = xb
            r_scr[pl.ds(t0, rows), :] = r
            i_scr[pl.ds(t0, rows), :] = ig
            a_scr[pl.ds(t0, rows), :] = a
            return carry

        lax.fori_loop(0, nt, recompute, 0)

        def rscan(gi, carry):
            lam_next, a_next = carry
            g = ng - 1 - gi
            t0 = pl.multiple_of(g * SUBLANES, SUBLANES)
            av = a_scr[pl.ds(t0, SUBLANES), :]
            hv = hs_ref[pl.ds(t0, SUBLANES), :]
            hprev = hs_ref[pl.ds(pl.multiple_of(jnp.maximum(t0 - SUBLANES, 0), SUBLANES), SUBLANES), :]
            hlast = jnp.where(t0 > 0, hprev[SUBLANES - 1:, :], 0.0)
            dhs = dh_ref[pl.ds(t0, SUBLANES), :] * _gelu(yp_ref[pl.ds(t0, SUBLANES), :])
            lams, das = [None] * SUBLANES, [None] * SUBLANES
            for rr in range(SUBLANES - 1, -1, -1):
                lam_t = dhs[rr:rr + 1, :] + a_next * lam_next
                h_before = hv[rr - 1:rr, :] if rr > 0 else hlast
                lams[rr] = lam_t
                das[rr] = lam_t * h_before
                lam_next, a_next = lam_t, av[rr:rr + 1, :]
            lam_scr[pl.ds(t0, SUBLANES), :] = jnp.concatenate(lams, axis=0)
            da_scr[pl.ds(t0, SUBLANES), :] = jnp.concatenate(das, axis=0)
            return lam_next, a_next

        lax.fori_loop(0, ng, rscan, (jnp.zeros((1, c), F32), jnp.zeros((1, c), F32)))

        dwga_ref[...] = jnp.zeros(dwga_ref.shape, F32)
        dwgx_ref[...] = jnp.zeros(dwgx_ref.shape, F32)

        def elementwise(t, carry):
            dbga, dbgx, dlam = carry
            t0 = pl.multiple_of(t * rows, rows)
            xb = xb_scr[pl.ds(t0, rows), :]
            r = r_scr[pl.ds(t0, rows), :]
            ig = i_scr[pl.ds(t0, rows), :]
            a = a_scr[pl.ds(t0, rows), :]
            db = lam_scr[pl.ds(t0, rows), :]
            da = da_scr[pl.ds(t0, rows), :]
            la = -LRU_C * r * sp
            s = jnp.sqrt(_neg_expm1(2.0 * la))
            d_ixb = db * s
            dla = da * a - (db * ig * xb) * (a * a / s)
            dzr = dla * (-LRU_C * sp) * r * (1.0 - r)
            dzi = d_ixb * xb * ig * (1.0 - ig)
            dzr_b, dzi_b = dzr.astype(BF16), dzi.astype(BF16)
            xbb = xb.astype(BF16)
            dwga_ref[0] += lax.dot_general(xbb, dzr_b, (((0,), (0,)), ((), ())), preferred_element_type=F32)
            dwgx_ref[0] += lax.dot_general(xbb, dzi_b, (((0,), (0,)), ((), ())), preferred_element_type=F32)
            dxb = (d_ixb * ig
                   + lax.dot_general(dzr_b, wga_v, (((1,), (1,)), ((), ())), preferred_element_type=F32)
                   + lax.dot_general(dzi_b, wgx_v, (((1,), (1,)), ((), ())), preferred_element_type=F32))
            xb_scr[pl.ds(t0, rows), :] = dxb
            yp = yp_ref[pl.ds(t0, rows), :]
            dyp_ref[pl.ds(t0, rows), :] = (dh_ref[pl.ds(t0, rows), :] * hs_ref[pl.ds(t0, rows), :] * _gelu_grad(yp)).astype(BF16)
            return (dbga + jnp.sum(dzr, axis=0, keepdims=True), dbgx + jnp.sum(dzi, axis=0, keepdims=True),
                    dlam + jnp.sum(dla * r, axis=0, keepdims=True))

        zero = jnp.zeros((1, c), F32)
        dbga, dbgx, dlam = lax.fori_loop(0, nt, elementwise, (zero, zero, zero))
        dbga_ref[...] = dbga
        dbgx_ref[...] = dbgx
        dlam_ref[...] = dlam * (LRU_C * sig_neg)

        def conv_bwd(t, carry):
            dcw, dcb = carry
            t0 = pl.multiple_of(t * rows, rows)
            main = xb_scr[pl.ds(t0, rows), :]
            nxt = xb_scr[pl.ds(pl.multiple_of(jnp.minimum(t0 + rows, t_all - SUBLANES), SUBLANES), SUBLANES), :]
            nxt = jnp.where(t0 + rows < t_all, nxt, 0.0)
            ext = jnp.concatenate([main, nxt], axis=0)
            dxp = cw[3:4, :] * main
            for k in range(3):
                dxp = dxp + cw[k:k + 1, :] * pltpu.roll(ext, rows + SUBLANES - (3 - k), 0)[:rows, :]
            dxp_ref[pl.ds(t0, rows), :] = dxp.astype(BF16)
            sh = _shifted_back(xp_ref, t0, rows, 3)
            new = [jnp.sum(main * sh[3 - k], axis=0, keepdims=True) for k in range(4)]
            return dcw + jnp.concatenate(new, axis=0), dcb + jnp.sum(main, axis=0, keepdims=True)

        dcw, dcb = lax.fori_loop(0, nt, conv_bwd, (jnp.zeros((4, c), F32), zero))
        dcw_ref[...] = dcw
        dcb_ref[...] = dcb

    col = pl.BlockSpec((t_all, c), lambda b: (0, b))
    col2 = pl.BlockSpec((t_all, c), lambda b: (0, nblk + b))
    vec = pl.BlockSpec((1, c), lambda b: (0, b))
    tap = pl.BlockSpec((4, c), lambda b: (0, b))
    wsp = pl.BlockSpec((1, c, c), lambda b: (b, 0, 0))
    vshape = jax.ShapeDtypeStruct((1, dr), F32)
    wshape = jax.ShapeDtypeStruct((nblk, c, c), F32)
    dxp, dyp, dcw, dcb, dwga, dbga, dwgx, dbgx, dlam = pl.pallas_call(
        body, name=name, grid=(nblk,),
        in_specs=[col, col2, col, col, tap, vec, wsp, vec, wsp, vec, vec],
        out_specs=[col, col, tap, vec, wsp, vec, wsp, vec, vec],
        out_shape=[jax.ShapeDtypeStruct((t_all, dr), BF16), jax.ShapeDtypeStruct((t_all, dr), BF16),
                   jax.ShapeDtypeStruct((4, dr), F32), vshape, wshape, vshape, wshape, vshape, vshape],
        scratch_shapes=[pltpu.VMEM((t_all, c), F32)] * 6,
        compiler_params=_params(("parallel",)),
    )(xy, xy, hs, dhsy, conv_w, conv_b, wga, bga, wgx, bgx, lam)
    return jnp.concatenate([dxp, dyp], axis=1), dcw, dcb, dwga, dbga, dwgx, dbgx, dlam


def _mesh_pos():
    return lax.axis_index("x"), lax.axis_index("y"), lax.axis_index("c")


def _all_gather(shards, *, name):
    n = len(shards)

    def body(*refs):
        ins, outs = refs[:n], refs[n:2 * n]
        send_sems, recv_sems, local_sems = refs[2 * n:]
        x, y, c = _mesh_pos()
        me, sibling = (x, y, c), (x, y, 1 - c)
        chips = [(1 - x, y), (x, 1 - y), (1 - x, 1 - y)]

        def slot(p):
            return 4 * p[0] + 2 * p[1] + p[2]

        def copy(a, k, block, to, src=None):
            dst = outs[a].at[slot(block)]
            return pltpu.make_async_remote_copy(
                src_ref=dst if src is None else src, dst_ref=dst, send_sem=send_sems.at[a, k],
                recv_sem=recv_sems.at[a, k], device_id=to, device_id_type=MESH)

        mine = [pltpu.make_async_copy(ins[a], outs[a].at[slot(me)], local_sems.at[a]) for a in range(n)]
        for cp in mine:
            cp.start()
        first = []
        for a in range(n):
            first.append(copy(a, 0, me, sibling, src=ins[a]))
            first += [copy(a, 1 + j, me, (*chip, c), src=ins[a]) for j, chip in enumerate(chips)]
        for cp in first:
            cp.start()
        passed = []
        for a in range(n):
            for j, chip in enumerate(chips):
                copy(a, 1 + j, (*chip, c), me).wait_recv()
                fwd = copy(a, 4 + j, (*chip, c), sibling)
                fwd.start()
                passed.append(fwd)
        for a in range(n):
            copy(a, 0, sibling, me).wait_recv()
            for j, chip in enumerate(chips):
                copy(a, 4 + j, (*chip, 1 - c), me).wait_recv()
        for cp in first + passed:
            cp.wait_send()
        for cp in mine:
            cp.wait()

    any_spec = pl.BlockSpec(memory_space=pl.ANY)
    return pl.pallas_call(
        body, name=name,
        in_specs=[any_spec] * n, out_specs=[any_spec] * n,
        out_shape=[jax.ShapeDtypeStruct((N_DEV,) + s.shape, s.dtype) for s in shards],
        scratch_shapes=[pltpu.SemaphoreType.DMA((n, 7)), pltpu.SemaphoreType.DMA((n, 7)), pltpu.SemaphoreType.DMA((n,))],
    )(*shards)


def _rs_sibling(grads, *, name):
    n = len(grads)

    def body(*refs):
        ins, outs = refs[:n], refs[n:2 * n]
        send_sems, recv_sems = refs[2 * n:]
        x, y, c = _mesh_pos()
        copies = []
        for a in range(n):
            for j in range(4):
                copies.append(pltpu.make_async_remote_copy(
                    src_ref=ins[a].at[2 * j + (1 - c)], dst_ref=outs[a].at[j], send_sem=send_sems.at[a, j],
                    recv_sem=recv_sems.at[a, j], device_id=(x, y, 1 - c), device_id_type=MESH))
        for cp in copies:
            cp.start()
        for cp in copies:
            cp.wait()

    any_spec = pl.BlockSpec(memory_space=pl.ANY)
    return pl.pallas_call(
        body, name=name, in_specs=[any_spec] * n, out_specs=[any_spec] * n,
        out_shape=[jax.ShapeDtypeStruct((4,) + g.shape[1:], g.dtype) for g in grads],
        scratch_shapes=[pltpu.SemaphoreType.DMA((n, 4)), pltpu.SemaphoreType.DMA((n, 4))],
    )(*grads)


def _rs_chips(parts, *, name):
    n = len(parts)

    def body(*refs):
        ins, outs = refs[:n], refs[n:2 * n]
        send_sems, recv_sems, local_sems = refs[2 * n:]
        x, y, c = _mesh_pos()
        copies, local = [], []
        for a in range(n):
            for k in (1, 2, 3):
                px = 1 - x if k & 2 else x
                py = 1 - y if k & 1 else y
                copies.append(pltpu.make_async_remote_copy(
                    src_ref=ins[a].at[2 * px + py], dst_ref=outs[a].at[k - 1], send_sem=send_sems.at[a, k - 1],
                    recv_sem=recv_sems.at[a, k - 1], device_id=(px, py, c), device_id_type=MESH))
            local.append(pltpu.make_async_copy(ins[a].at[2 * x + y], outs[a].at[3], local_sems.at[a]))
        for cp in copies + local:
            cp.start()
        for cp in copies + local:
            cp.wait()

    any_spec = pl.BlockSpec(memory_space=pl.ANY)
    return pl.pallas_call(
        body, name=name, in_specs=[any_spec] * n, out_specs=[any_spec] * n,
        out_shape=[jax.ShapeDtypeStruct(p.shape, p.dtype) for p in parts],
        scratch_shapes=[pltpu.SemaphoreType.DMA((n, 3)), pltpu.SemaphoreType.DMA((n, 3)), pltpu.SemaphoreType.DMA((n,))],
    )(*parts)


def _pair_add(grads, landed, core, *, name, tr):
    _, r_all, c_all = grads.shape

    def body(core_ref, g_ref, l_ref, o_ref):
        o_ref[...] = (g_ref[...].astype(F32) + l_ref[...].astype(F32)).astype(o_ref.dtype)

    return pl.pallas_call(
        body, name=name,
        grid_spec=pltpu.PrefetchScalarGridSpec(
            num_scalar_prefetch=1, grid=(4, r_all // tr),
            in_specs=[pl.BlockSpec((None, tr, c_all), lambda j, i, core_ref: (2 * j + core_ref[0], i, 0)),
                      pl.BlockSpec((None, tr, c_all), lambda j, i, core_ref: (j, i, 0))],
            out_specs=pl.BlockSpec((None, tr, c_all), lambda j, i, core_ref: (j, i, 0))),
        out_shape=jax.ShapeDtypeStruct((4, r_all, c_all), grads.dtype),
        compiler_params=_params(("parallel", "parallel")),
    )(core, grads, landed)


def _adamw_math(w, g, m, v):
    m2 = ADAM_B1 * m + (1.0 - ADAM_B1) * g
    v2 = ADAM_B2 * v + (1.0 - ADAM_B2) * (g * g)
    m_hat = m2 / (1.0 - ADAM_B1 ** ADAM_STEP)
    v_hat = v2 / (1.0 - ADAM_B2 ** ADAM_STEP)
    delta = -ADAM_LR * (m_hat / (jnp.sqrt(v_hat) + ADAM_EPS) + ADAM_WD * w)
    return delta, m2, v2


def _adamw(w, m, v, terms, order, *, name, tr, col_block=None):
    r_all, c_all = w.shape
    n_slots = terms.shape[0]

    def body(*refs):
        if col_block is not None:
            refs = refs[1:]
        w_ref, m_ref, v_ref, t_ref, g_ref, d_ref, m2_ref, v2_ref = refs
        g = t_ref[order[0]].astype(F32)
        for s in order[1:]:
            g = g + t_ref[s].astype(F32)
        delta, m2, v2 = _adamw_math(w_ref[...], g, m_ref[...], v_ref[...])
        g_ref[...] = g
        d_ref[...] = delta
        m2_ref[...] = m2
        v2_ref[...] = v2

    shape = jax.ShapeDtypeStruct((r_all, c_all), F32)
    if col_block is None:
        row = pl.BlockSpec((tr, c_all), lambda i: (i, 0))
        return pl.pallas_call(
            body, name=name, grid=(r_all // tr,),
            in_specs=[row, row, row, pl.BlockSpec((n_slots, tr, c_all), lambda i: (0, i, 0))],
            out_specs=[row] * 4, out_shape=[shape] * 4, compiler_params=_params(("parallel",)),
        )(w, m, v, terms)
    row = pl.BlockSpec((tr, c_all), lambda i, blk: (i, 0))
    return pl.pallas_call(
        body, name=name,
        grid_spec=pltpu.PrefetchScalarGridSpec(
            num_scalar_prefetch=1, grid=(r_all // tr,),
            in_specs=[row, row, row, pl.BlockSpec((n_slots, tr, c_all), lambda i, blk: (0, i, blk[0]))],
            out_specs=[row] * 4),
        out_shape=[shape] * 4, compiler_params=_params(("parallel",)),
    )(col_block, w, m, v, terms)


def _rope_tables(t_all):
    pos = jnp.arange(t_all, dtype=F32)
    inv_freq = ROPE_THETA ** (-jnp.arange(0, QK_ROPE, 2, dtype=F32) / QK_ROPE)
    ang = pos[:, None] * inv_freq[None, :]
    cos, sin = jnp.cos(ang), jnp.sin(ang)
    return jnp.tile(cos, (1, LANES // (QK_ROPE // 2))), jnp.tile(sin, (1, LANES // (QK_ROPE // 2)))


def _adam_row_tile(r_all, c_all):
    target = max(SUBLANES, (512 * 1024) // (4 * c_all))
    return _pick(r_all, [t for t in (512, 256, 128, 64, 32, 16, 8) if t <= target])


_DIVS = (2048, 1024, 512, 256, 128)


def _rows_natural(wg):
    return wg.reshape(wg.shape[0] * wg.shape[1], wg.shape[2])


def _mla_layer_fwd(tag, h, g_mix, ws, qn, kvn, cos, sin, *, tm, tq, n_heads, scale, n_keys):
    w_in, w_uq, w_ukv, w_o = _rows_natural(ws[0]), ws[1], ws[2], _rows_natural(ws[3])
    d = h.shape[1]
    lq, lkv = qn.shape[1], kvn.shape[1]
    hn = _rmsnorm_fwd(h, g_mix, name=f"norm_mix{tag}", tm=tm)
    proj = _mm_nn(hn, w_in, name=f"mla_in{tag}", out_dtype=F32, tm=tm, tn=w_in.shape[1], tk=_pick(d, _DIVS))
    cq, ckv, kr = _mla_prep_fwd(proj, qn, kvn, cos, sin, name=f"mla_prep{tag}", tm=tm, lq=lq, lkv=lkv)
    q = _mm_nn(cq, w_uq, name=f"mla_q{tag}", out_dtype=BF16, tm=tm, tn=w_uq.shape[2], tk=lq, b_blocked=True,
               epilogue=_rope_q_epilogue, extras=(cos, sin))
    kv = _mm_nn(ckv, w_ukv, name=f"mla_kv{tag}", out_dtype=BF16, tm=tm, tn=w_ukv.shape[2], tk=lkv, b_blocked=True)
    o, lse = _attn_fwd(q, kv, kr, name=f"attn_fwd{tag}", n_heads=n_heads, tq=tq, tk=tq, scale=scale, n_keys=n_keys)
    h_mid = _mm_nn(o, w_o, name=f"mla_o{tag}", out_dtype=F32, tm=tm, tn=_pick(d, _DIVS[1:]), tk=_pick(o.shape[1], _DIVS), res=h)
    return h_mid, (hn, proj, cq, ckv, kr, q, kv, o, lse)


def _mla_layer_bwd(tag, dh, dh_b, h_in, saved, g_mix, ws, qn, kvn, cos, sin, *, tm, tq, n_heads, scale, n_keys):
    hn, proj, cq, ckv, kr, q, kv, o, lse = saved
    w_in, w_uq, w_ukv, w_o = _rows_natural(ws[0]), ws[1], ws[2], _rows_natural(ws[3])
    d = h_in.shape[1]
    lq, lkv = qn.shape[1], kvn.shape[1]
    ov = o.shape[1]
    tn_d, tk_d = _pick(d, _DIVS[1:]), _pick(d, _DIVS)
    do = _mm_nt(dh_b, w_o, name=f"mla_do{tag}", out_dtype=BF16, tm=tm, tn=_pick(ov, _DIVS[1:]), tk=tk_d)
    dw_o = _mm_tn(o, dh_b, name=f"mla_dwo{tag}", out_dtype=BF16, tm=_pick(ov, _DIVS[1:]), tn=tn_d, tk=tm)
    dq = _attn_bwd_dq(q, kv, kr, o, lse, do, cos, sin, name=f"attn_dq{tag}", n_heads=n_heads, tq=tq, tk=tq, scale=scale, n_keys=n_keys)
    dkv, dkr_h = _attn_bwd_dkv(q, kv, kr, o, lse, do, name=f"attn_dkv{tag}", n_heads=n_heads, tq=tq, tk=tq, scale=scale, n_keys=n_keys)
    hw, kw = w_uq.shape[2], w_ukv.shape[2]
    dw_uq = _mm_tn(cq, dq, name=f"mla_dwuq{tag}", out_dtype=BF16, tm=lq, tn=hw, tk=tm, out_block=hw)
    dcq = _mm_nt(dq, w_uq, name=f"mla_dcq{tag}", out_dtype=F32, tm=tm, tn=lq, tk=hw, b_blocked=True)
    dw_ukv = _mm_tn(ckv, dkv, name=f"mla_dwukv{tag}", out_dtype=BF16, tm=lkv, tn=kw, tk=tm, out_block=kw)
    dckv = _mm_nt(dkv, w_ukv, name=f"mla_dckv{tag}", out_dtype=F32, tm=tm, tn=lkv, tk=kw, b_blocked=True)
    dproj, dqn, dkvn = _mla_prep_bwd(dcq, dckv, dkr_h, proj, qn, kvn, cos, sin, name=f"mla_prep_bwd{tag}", tm=tm, lq=lq, lkv=lkv)
    wc = w_in.shape[1]
    dw_in = _mm_tn(hn, dproj, name=f"mla_dwin{tag}", out_dtype=BF16, tm=tn_d, tn=wc, tk=tm)
    dhn = _mm_nt(dproj, w_in, name=f"mla_dhn{tag}", out_dtype=F32, tm=tm, tn=tn_d, tk=wc)
    dh, dh_b, dg = _rmsnorm_bwd(dhn, h_in, g_mix, dh, name=f"norm_mix_bwd{tag}", tm=tm)
    return dh, dh_b, dg, dqn, dkvn, [dw_in.reshape(N_DEV, -1, wc), dw_uq, dw_ukv, dw_o.reshape(N_DEV, -1, d)]


def _lru_layer_fwd(tag, h, g_mix, ws, small, *, tm):
    w_lin, w_lo = ws[0], _rows_natural(ws[1])
    d = h.shape[1]
    dr = w_lo.shape[0]
    hn = _rmsnorm_fwd(h, g_mix, name=f"norm_mix{tag}", tm=tm)
    xy = _mm_nn(hn, w_lin, name=f"lru_in{tag}", out_dtype=F32, tm=tm, tn=w_lin.shape[2], tk=_pick(d, _DIVS), b_blocked=True)
    hs, hsy = _lru_fwd(xy, *small, name=f"lru_fwd{tag}")
    h_mid = _mm_nn(hsy, w_lo, name=f"lru_o{tag}", out_dtype=F32, tm=tm, tn=_pick(d, _DIVS[1:]), tk=_pick(dr, _DIVS), res=h)
    return h_mid, (hn, xy, hs, hsy)


def _lru_layer_bwd(tag, dh, dh_b, h_in, saved, g_mix, ws, small, *, tm):
    hn, xy, hs, hsy = saved
    w_lin, w_lo = ws[0], _rows_natural(ws[1])
    d = h_in.shape[1]
    dr = w_lo.shape[0]
    tn_d, tk_d = _pick(d, _DIVS[1:]), _pick(d, _DIVS)
    dhsy = _mm_nt(dh_b, w_lo, name=f"lru_dhsy{tag}", out_dtype=F32, tm=tm, tn=_pick(dr, _DIVS[1:]), tk=tk_d)
    dw_lo = _mm_tn(hsy, dh_b, name=f"lru_dwo{tag}", out_dtype=BF16, tm=_pick(dr, _DIVS[1:]), tn=tn_d, tk=tm)
    dxy, *dsmall = _lru_bwd(xy, hs, dhsy, *small, name=f"lru_bwd{tag}")
    lw = w_lin.shape[2]
    dw_lin = _mm_tn(hn, dxy, name=f"lru_dwin{tag}", out_dtype=BF16, tm=tn_d, tn=lw, tk=tm, out_block=lw)
    dhn = _mm_nt(dxy, w_lin, name=f"lru_dhn{tag}", out_dtype=F32, tm=tm, tn=tn_d, tk=lw, b_blocked=True)
    dh, dh_b, dg = _rmsnorm_bwd(dhn, h_in, g_mix, dh, name=f"norm_mix_bwd{tag}", tm=tm)
    return dh, dh_b, dg, tuple(dsmall), [dw_lin, dw_lo.reshape(N_DEV, -1, d)]


def _ffn_layer_fwd(tag, h_mid, g_ffn, ws, *, tm):
    w_gu, w_down = ws[0], _rows_natural(ws[1])
    d = h_mid.shape[1]
    f_all = w_down.shape[0]
    fk = _pick(f_all, (1408,) + _DIVS[1:])
    hn2 = _rmsnorm_fwd(h_mid, g_ffn, name=f"norm_ffn{tag}", tm=tm)
    gu = _mm_nn(hn2, w_gu, name=f"ffn_gu{tag}", out_dtype=F32, tm=tm, tn=w_gu.shape[2], tk=_pick(d, _DIVS), b_blocked=True)
    act = _swiglu_fwd(gu, name=f"swiglu_fwd{tag}", tm=tm, tc=fk)
    h_out = _mm_nn(act, w_down, name=f"ffn_down{tag}", out_dtype=F32, tm=tm, tn=_pick(d, _DIVS[1:]), tk=fk, res=h_mid)
    return h_out, (hn2, gu, act)


def _ffn_layer_bwd(tag, dh, dh_b, h_mid, saved, g_ffn, ws, *, tm):
    hn2, gu, act = saved
    w_gu, w_down = ws[0], _rows_natural(ws[1])
    d = h_mid.shape[1]
    f_all = w_down.shape[0]
    f_local = w_gu.shape[2]
    fk = _pick(f_all, (1408,) + _DIVS[1:])
    tn_d, tk_d = _pick(d, _DIVS[1:]), _pick(d, _DIVS)
    dact = _mm_nt(dh_b, w_down, name=f"ffn_dact{tag}", out_dtype=F32, tm=tm, tn=fk, tk=tk_d)
    dgu = _swiglu_bwd(dact, gu, name=f"swiglu_bwd{tag}", tm=tm, tc=fk)
    dw_down = _mm_tn(act, dh_b, name=f"ffn_dwdown{tag}", out_dtype=BF16, tm=fk, tn=tn_d, tk=tm)
    dhn2 = _mm_nt(dgu, w_gu, name=f"ffn_dhn{tag}", out_dtype=F32, tm=tm, tn=tn_d, tk=f_local, b_blocked=True)
    dw_gu = _mm_tn(hn2, dgu, name=f"ffn_dwgu{tag}", out_dtype=BF16, tm=tn_d, tn=f_local, tk=tm, out_block=f_local)
    dh, dh_b, dg = _rmsnorm_bwd(dhn2, h_mid, g_ffn, dh, name=f"norm_ffn_bwd{tag}", tm=tm)
    return dh, dh_b, dg, [dw_gu, dw_down.reshape(N_DEV, -1, d)]


def kernel(x, meta_tokens, norm_mix, norm_ffn, norm_final, mla_w_in, mla_q_norm, mla_kv_norm, mla_w_uq, mla_w_ukv, mla_w_o, lru_w_in, lru_conv_w, lru_conv_b, lru_w_gate_a, lru_b_gate_a, lru_w_gate_x, lru_b_gate_x, lru_lambda, lru_w_o, ffn_w_gu, ffn_w_down, loss_target, m_meta_tokens, m_norm_mix, m_norm_ffn, m_norm_final, m_mla_w_in, m_mla_q_norm, m_mla_kv_norm, m_mla_w_uq, m_mla_w_ukv, m_mla_w_o, m_lru_w_in, m_lru_conv_w, m_lru_conv_b, m_lru_w_gate_a, m_lru_b_gate_a, m_lru_w_gate_x, m_lru_b_gate_x, m_lru_lambda, m_lru_w_o, m_ffn_w_gu, m_ffn_w_down, v_meta_tokens, v_norm_mix, v_norm_ffn, v_norm_final, v_mla_w_in, v_mla_q_norm, v_mla_kv_norm, v_mla_w_uq, v_mla_w_ukv, v_mla_w_o, v_lru_w_in, v_lru_conv_w, v_lru_conv_b, v_lru_w_gate_a, v_lru_b_gate_a, v_lru_w_gate_x, v_lru_b_gate_x, v_lru_lambda, v_lru_w_o, v_ffn_w_gu, v_ffn_w_down):
    seq, d = x.shape[1], x.shape[2]
    n_real = N_META + seq
    t_all = -(-n_real // LANES) * LANES
    tm = _pick(t_all, (384, 256, 128))
    tq = tm if tm <= 384 and t_all > 512 else LANES
    depth = norm_mix.shape[0]
    n_mla, n_lru = mla_w_in.shape[0], lru_w_in.shape[0]
    lq, lkv = mla_q_norm.shape[1], mla_kv_norm.shape[1]
    w_in_cols = lq + lkv + LANES
    heads_local = mla_w_uq.shape[2] // (QK_NOPE + QK_ROPE)
    n_heads = heads_local * N_DEV
    dr = lru_w_gate_a.shape[1] * lru_w_gate_a.shape[2]
    f_local = ffn_w_gu.shape[2]
    f_all = f_local * N_DEV // 2
    scale = (QK_NOPE + QK_ROPE) ** -0.5
    cx, cy, cc = _mesh_pos()
    core = jnp.reshape(cc, (1,)).astype(jnp.int32)
    my_slot = jnp.reshape(4 * cx + 2 * cy + cc, (1,)).astype(jnp.int32)

    def pad_cols(w, cols):
        return jnp.pad(w, ((0, 0), (0, cols - w.shape[1])))

    def pad_heads(w):
        k_all = w.shape[0]
        w3 = w.reshape(k_all, heads_local, QK_NOPE + QK_ROPE)
        return jnp.pad(w3, ((0, 0), (0, 0), (0, HEAD_W - QK_NOPE - QK_ROPE))).reshape(k_all, heads_local * HEAD_W)

    def unpad_heads(w):
        k_all = w.shape[0]
        return w.reshape(k_all, heads_local, HEAD_W)[:, :, :QK_NOPE + QK_ROPE].reshape(k_all, -1)

    small_rows = N_META + n_lru * 4 + 2 * n_lru
    small_pad = -(-small_rows // SUBLANES) * SUBLANES

    def pack_small(meta, conv_w, conv_b, lam):
        rows = jnp.concatenate([meta, conv_w.reshape(n_lru * 4, -1), conv_b, lam], axis=0)
        return jnp.pad(rows, ((0, small_pad - small_rows), (0, 0)))

    def unpack_small(p):
        o1 = N_META + n_lru * 4
        return (p[:N_META], p[N_META:o1].reshape(n_lru, 4, -1), p[o1:o1 + n_lru], p[o1 + n_lru:o1 + 2 * n_lru])

    mla_shards, lru_shards, ffn_shards = [], [], []
    for j in range(n_mla):
        mla_shards.append([pad_cols(mla_w_in[j], w_in_cols).astype(BF16), pad_heads(mla_w_uq[j]).astype(BF16),
                           mla_w_ukv[j].astype(BF16), mla_w_o[j].astype(BF16)])
    for j in range(n_lru):
        lru_shards.append([lru_w_in[j].astype(BF16), lru_w_o[j].astype(BF16)])
    for layer in range(depth):
        ffn_shards.append([ffn_w_gu[layer].astype(BF16), ffn_w_down[layer].astype(BF16)])

    small_full = _all_gather([pack_small(meta_tokens, lru_conv_w, lru_conv_b, lru_lambda)], name="ag_small")[0]
    small_full = jnp.transpose(small_full, (1, 0, 2)).reshape(small_pad, -1)
    meta_full, conv_w_full, conv_b_full, lam_full = unpack_small(small_full)

    gathered = []
    for layer in range(depth):
        j = layer // 2
        shards = (mla_shards[j] if layer % 2 == 0 else lru_shards[j]) + ffn_shards[layer]
        gathered.append(_all_gather(shards, name=f"ag_layer{layer}"))

    def rows_natural(wg):
        return wg.reshape(wg.shape[0] * wg.shape[1], wg.shape[2])

    cos, sin = _rope_tables(t_all)
    zeros_tail = jnp.zeros((t_all - n_real, d), F32)
    h = jnp.concatenate([meta_full, x[0], zeros_tail], axis=0)
    target = jnp.concatenate([jnp.zeros((N_META, d), F32), loss_target[0], zeros_tail], axis=0)

    attn_kw = dict(tm=tm, tq=tq, n_heads=n_heads, scale=scale, n_keys=n_real)

    def lru_small(j):
        return (conv_w_full[j], conv_b_full[j][None, :], lru_w_gate_a[j].astype(BF16), lru_b_gate_a[j].reshape(1, dr),
                lru_w_gate_x[j].astype(BF16), lru_b_gate_x[j].reshape(1, dr), lam_full[j][None, :])

    saved = []
    for layer in range(depth):
        j = layer // 2
        g_mix = norm_mix[layer][None, :]
        g_ffn = norm_ffn[layer][None, :]
        ws = gathered[layer]
        if layer % 2 == 0:
            h_mid, mix_saved = _mla_layer_fwd(layer, h, g_mix, ws[:4], mla_q_norm[j][None, :], mla_kv_norm[j][None, :], cos, sin,
                                              **attn_kw)
        else:
            h_mid, mix_saved = _lru_layer_fwd(layer, h, g_mix, ws[:2], lru_small(j), tm=tm)
        h_out, ffn_saved = _ffn_layer_fwd(layer, h_mid, g_ffn, ws[-2:], tm=tm)
        saved.append((h, h_mid, mix_saved, ffn_saved))
        h = h_out

    loss_part, dh, dh_b, dg_final = _loss_head(h, target, norm_final[None, :], name="loss_head", tm=tm, n_real=n_real)
    loss = lax.psum(loss_part[0, 0], ("x", "y", "c"))

    big_grads = [None] * depth
    d_norm_mix, d_norm_ffn = [None] * depth, [None] * depth
    d_qn, d_kvn = [None] * n_mla, [None] * n_mla
    d_small = {k: [None] * n_lru for k in ("cw", "cb", "wga", "bga", "wgx", "bgx", "lam")}
    for layer in reversed(range(depth)):
        j = layer // 2
        h_in, h_mid, mix_saved, ffn_saved = saved[layer]
        ws = gathered[layer]
        dh, dh_b, d_norm_ffn[layer], ffn_g = _ffn_layer_bwd(layer, dh, dh_b, h_mid, ffn_saved, norm_ffn[layer][None, :], ws[-2:], tm=tm)
        g_mix = norm_mix[layer][None, :]
        if layer % 2 == 0:
            dh, dh_b, d_norm_mix[layer], d_qn[j], d_kvn[j], mix_g = _mla_layer_bwd(
                layer, dh, dh_b, h_in, mix_saved, g_mix, ws[:4], mla_q_norm[j][None, :], mla_kv_norm[j][None, :], cos, sin, **attn_kw)
        else:
            dh, dh_b, d_norm_mix[layer], dsmall, mix_g = _lru_layer_bwd(layer, dh, dh_b, h_in, mix_saved, g_mix, ws[:2],
                                                                        lru_small(j), tm=tm)
            for key, val in zip(("cw", "cb", "wga", "bga", "wgx", "bgx", "lam"), dsmall):
                d_small[key][j] = val
        big_grads[layer] = mix_g + ffn_g

    grad_x = dh[N_META:n_real][None]

    reduced = []
    for layer in range(depth):
        landed = _rs_sibling(big_grads[layer], name=f"rs_sibling{layer}")
        parts = [_pair_add(g, l, core, name=f"rs_add{layer}_{a}", tr=_adam_row_tile(g.shape[1], g.shape[2]))
                 for a, (g, l) in enumerate(zip(big_grads[layer], landed))]
        reduced.append(_rs_chips(parts, name=f"rs_chips{layer}"))

    own_first = (3, 0, 1, 2)

    def adam_sharded(terms, w, m, v, tag):
        r_all, c_all = terms.shape[1], terms.shape[2]
        return _adamw(w.reshape(r_all, c_all), m.reshape(r_all, c_all), v.reshape(r_all, c_all), terms, own_first,
                      name=f"adamw_{tag}", tr=_adam_row_tile(r_all, c_all))

    def per_layer(fn, n):
        outs = [fn(i) for i in range(n)]
        return [jnp.stack([o[k] for o in outs], axis=0) for k in range(4)]

    res = {}
    res["mla_w_in"] = per_layer(lambda j: [t[:, :lq + lkv + QK_ROPE] for t in adam_sharded(
        reduced[2 * j][0], pad_cols(mla_w_in[j], w_in_cols), pad_cols(m_mla_w_in[j], w_in_cols),
        pad_cols(v_mla_w_in[j], w_in_cols), f"mla_w_in{j}")], n_mla)
    res["mla_w_uq"] = per_layer(lambda j: [unpad_heads(t) for t in adam_sharded(
        reduced[2 * j][1], pad_heads(mla_w_uq[j]), pad_heads(m_mla_w_uq[j]), pad_heads(v_mla_w_uq[j]), f"mla_w_uq{j}")], n_mla)
    res["mla_w_ukv"] = per_layer(lambda j: adam_sharded(reduced[2 * j][2], mla_w_ukv[j], m_mla_w_ukv[j], v_mla_w_ukv[j],
                                                         f"mla_w_ukv{j}"), n_mla)
    res["mla_w_o"] = per_layer(lambda j: adam_sharded(reduced[2 * j][3], mla_w_o[j], m_mla_w_o[j], v_mla_w_o[j],
                                                       f"mla_w_o{j}"), n_mla)
    res["lru_w_in"] = per_layer(lambda j: adam_sharded(reduced[2 * j + 1][0], lru_w_in[j], m_lru_w_in[j], v_lru_w_in[j],
                                                        f"lru_w_in{j}"), n_lru)
    res["lru_w_o"] = per_layer(lambda j: adam_sharded(reduced[2 * j + 1][1], lru_w_o[j], m_lru_w_o[j], v_lru_w_o[j],
                                                       f"lru_w_o{j}"), n_lru)
    res["ffn_w_gu"] = per_layer(lambda l: adam_sharded(reduced[l][-2], ffn_w_gu[l], m_ffn_w_gu[l], v_ffn_w_gu[l],
                                                        f"ffn_w_gu{l}"), depth)
    res["ffn_w_down"] = per_layer(lambda l: adam_sharded(reduced[l][-1], ffn_w_down[l], m_ffn_w_down[l], v_ffn_w_down[l],
                                                          f"ffn_w_down{l}"), depth)

    d_meta = dh[:N_META]
    small_grad = pack_small(d_meta, jnp.stack(d_small["cw"], axis=0), jnp.concatenate(d_small["cb"], axis=0),
                            jnp.concatenate(d_small["lam"], axis=0))
    rep_grads = [
        jnp.concatenate(d_norm_mix, axis=0), jnp.concatenate(d_norm_ffn, axis=0), dg_final,
        jnp.concatenate(d_qn, axis=0), jnp.concatenate(d_kvn, axis=0),
        jnp.stack(d_small["wga"], axis=0).reshape(-1, LANES), jnp.concatenate(d_small["bga"], axis=0),
        jnp.stack(d_small["wgx"], axis=0).reshape(-1, LANES), jnp.concatenate(d_small["bgx"], axis=0),
    ]
    rep_grads = [jnp.pad(g, ((0, -g.shape[0] % SUBLANES), (0, 0))) for g in rep_grads]
    all_small = _all_gather([small_grad] + rep_grads, name="ag_small_grads")
    slot_order = tuple(range(N_DEV))

    def adam_rep(terms, w, m, v, tag):
        r_pad, c_all = terms.shape[1], terms.shape[2]

        def prep(t):
            t2 = t.reshape(-1, c_all)
            return jnp.pad(t2, ((0, r_pad - t2.shape[0]), (0, 0)))

        outs = _adamw(prep(w), prep(m), prep(v), terms, slot_order, name=f"adamw_{tag}", tr=_adam_row_tile(r_pad, c_all))
        n_rows = w.size // c_all
        return [o[:n_rows].reshape(w.shape) for o in outs]

    small_w = pack_small(meta_tokens, lru_conv_w, lru_conv_b, lru_lambda)
    small_m = pack_small(m_meta_tokens, m_lru_conv_w, m_lru_conv_b, m_lru_lambda)
    small_v = pack_small(v_meta_tokens, v_lru_conv_w, v_lru_conv_b, v_lru_lambda)
    small_out = _adamw(small_w, small_m, small_v, all_small[0], slot_order, name="adamw_small", tr=small_pad, col_block=my_slot)
    small_out = [unpack_small(o) for o in small_out]
    for idx, key in enumerate(("meta_tokens", "lru_conv_w", "lru_conv_b", "lru_lambda")):
        res[key] = [small_out[k][idx] for k in range(4)]

    res["norm_mix"] = adam_rep(all_small[1], norm_mix, m_norm_mix, v_norm_mix, "norm_mix")
    res["norm_ffn"] = adam_rep(all_small[2], norm_ffn, m_norm_ffn, v_norm_ffn, "norm_ffn")
    res["norm_final"] = adam_rep(all_small[3], norm_final, m_norm_final, v_norm_final, "norm_final")
    res["mla_q_norm"] = adam_rep(all_small[4], mla_q_norm, m_mla_q_norm, v_mla_q_norm, "mla_q_norm")
    res["mla_kv_norm"] = adam_rep(all_small[5], mla_kv_norm, m_mla_kv_norm, v_mla_kv_norm, "mla_kv_norm")
    res["lru_w_gate_a"] = adam_rep(all_small[6], lru_w_gate_a, m_lru_w_gate_a, v_lru_w_gate_a, "lru_w_gate_a")
    res["lru_b_gate_a"] = adam_rep(all_small[7], lru_b_gate_a, m_lru_b_gate_a, v_lru_b_gate_a, "lru_b_gate_a")
    res["lru_w_gate_x"] = adam_rep(all_small[8], lru_w_gate_x, m_lru_w_gate_x, v_lru_w_gate_x, "lru_w_gate_x")
    res["lru_b_gate_x"] = adam_rep(all_small[9], lru_b_gate_x, m_lru_b_gate_x, v_lru_b_gate_x, "lru_b_gate_x")

    names = ["meta_tokens", "norm_mix", "norm_ffn", "norm_final", "mla_w_in", "mla_q_norm", "mla_kv_norm", "mla_w_uq",
             "mla_w_ukv", "mla_w_o", "lru_w_in", "lru_conv_w", "lru_conv_b", "lru_w_gate_a", "lru_b_gate_a", "lru_w_gate_x",
             "lru_b_gate_x", "lru_lambda", "lru_w_o", "ffn_w_gu", "ffn_w_down"]
    shapes = dict(meta_tokens=meta_tokens, norm_mix=norm_mix, norm_ffn=norm_ffn, norm_final=norm_final, mla_w_in=mla_w_in,
                  mla_q_norm=mla_q_norm, mla_kv_norm=mla_kv_norm, mla_w_uq=mla_w_uq, mla_w_ukv=mla_w_ukv, mla_w_o=mla_w_o,
                  lru_w_in=lru_w_in, lru_conv_w=lru_conv_w, lru_conv_b=lru_conv_b, lru_w_gate_a=lru_w_gate_a,
                  lru_b_gate_a=lru_b_gate_a, lru_w_gate_x=lru_w_gate_x, lru_b_gate_x=lru_b_gate_x, lru_lambda=lru_lambda,
                  lru_w_o=lru_w_o, ffn_w_gu=ffn_w_gu, ffn_w_down=ffn_w_down)
    outs = [loss, grad_x]
    for k in range(4):
        outs += [res[nm][k].reshape(shapes[nm].shape) for nm in names]
    return tuple(outs)
```

```python
import math

import jax
import jax.numpy as jnp
from jax import lax
from jax.experimental import pallas as pl
from jax.experimental.pallas import tpu as pltpu

F32 = jnp.float32
BF16 = jnp.bfloat16
MESH = pl.DeviceIdType.MESH

N_META = 16
CHUNK = 64
QK_NOPE = 128
QK_ROPE = 64
V_HEAD = 128
HEAD_W = 256
ROPE_THETA = 10000.0
LRU_C = 8.0
RMS_EPS = 1e-6
NEG_BIG = -1e30
ADAM_LR, ADAM_B1, ADAM_B2, ADAM_EPS, ADAM_WD, ADAM_STEP = 0.001, 0.9, 0.999, 1e-08, 0.01, 10

LANES = 128
SUBLANES = 8
VMEM_LIMIT_BYTES = 52 * 1024 * 1024
N_DEV = 8

_NT = (((1,), (1,)), ((), ()))
_TN = (((0,), (0,)), ((), ()))
_DIVS = (2048, 1024, 512, 256, 128)
_ROW_TILES = (1408, 1024, 512, 256, 128)


def _params(dims):
    return pltpu.CompilerParams(dimension_semantics=dims, vmem_limit_bytes=VMEM_LIMIT_BYTES)


def _pick(n, candidates):
    for c in candidates:
        if c <= n and n % c == 0:
            return c
    return n


def _sigmoid(z):
    return 1.0 / (1.0 + jnp.exp(-z))


def _gelu(x):
    c = math.sqrt(2.0 / math.pi)
    return 0.5 * x * (1.0 + jnp.tanh(c * (x + 0.044715 * x * x * x)))


def _gelu_grad(x):
    c = math.sqrt(2.0 / math.pi)
    th = jnp.tanh(c * (x + 0.044715 * x * x * x))
    return 0.5 * (1.0 + th) + 0.5 * x * (1.0 - th * th) * c * (1.0 + 3.0 * 0.044715 * x * x)


def _neg_expm1(x):
    poly = -x * (1.0 + x * (1.0 / 2.0) * (1.0 + x * (1.0 / 3.0) * (1.0 + x * (1.0 / 4.0) * (
        1.0 + x * (1.0 / 5.0) * (1.0 + x * (1.0 / 6.0) * (1.0 + x * (1.0 / 7.0)))))))
    return jnp.where(x > -0.25, poly, 1.0 - jnp.exp(x))


def _softplus_neg(lam):
    e = jnp.exp(-jnp.abs(lam))
    log1p = jnp.where(e > 1e-4, jnp.log(1.0 + e), e * (1.0 - e * (0.5 - e * (1.0 / 3.0))))
    return jnp.maximum(-lam, 0.0) + log1p


def _rot_half(x):
    lane = lax.broadcasted_iota(jnp.int32, x.shape, 1)
    first = (lane % QK_ROPE) < (QK_ROPE // 2)
    return jnp.where(first, -pltpu.roll(x, LANES - QK_ROPE // 2, 1), pltpu.roll(x, QK_ROPE // 2, 1))


def _rope(x, cos, sin):
    return x * cos + _rot_half(x) * sin


def _unrope(g, cos, sin):
    return g * cos - _rot_half(g) * sin


def _mm_nn(a, b, *, name, out_dtype, tm, tn, tk, b_blocked=False, res=None, epilogue=None, extras=()):
    m_all, k_all = a.shape
    if b_blocked:
        g_all, kb, nb = b.shape
        n_all = g_all * nb
        assert nb % tn == 0
        r = nb // tn
        b_spec = pl.BlockSpec((None, tk, tn), lambda j, i, k: (j // r, k, j % r))
    else:
        kb, n_all = b.shape
        b_spec = pl.BlockSpec((tk, tn), lambda j, i, k: (k, j))
    assert kb == k_all and m_all % tm == 0 and n_all % tn == 0 and k_all % tk == 0
    nm, nn, nk = m_all // tm, n_all // tn, k_all // tk
    in_specs = [pl.BlockSpec((tm, tk), lambda j, i, k: (i, k)), b_spec]
    operands = [a, b]
    has_res = res is not None
    if has_res:
        in_specs.append(pl.BlockSpec((tm, tn), lambda j, i, k: (i, j)))
        operands.append(res)
    for e in extras:
        in_specs.append(pl.BlockSpec((tm, e.shape[1]), lambda j, i, k: (i, 0)))
        operands.append(e)
    n_ex = len(extras)

    def body(*refs):
        a_ref, b_ref = refs[0], refs[1]
        pos = 2
        res_ref = None
        if has_res:
            res_ref = refs[pos]
            pos += 1
        ex_refs = refs[pos:pos + n_ex]
        pos += n_ex
        o_ref = refs[pos]
        acc_ref = refs[pos + 1] if nk > 1 else None

        def finish(acc):
            if has_res:
                acc = acc + res_ref[...]
            if epilogue is not None:
                acc = epilogue(acc, *ex_refs)
            o_ref[...] = acc.astype(o_ref.dtype)

        prod = jnp.dot(a_ref[...], b_ref[...], preferred_element_type=F32)
        if nk == 1:
            finish(prod)
        else:
            k = pl.program_id(2)

            @pl.when(k == 0)
            def _():
                acc_ref[...] = prod

            @pl.when(k > 0)
            def _():
                acc_ref[...] += prod

            @pl.when(k == nk - 1)
            def _():
                finish(acc_ref[...])

    return pl.pallas_call(
        body, name=name, grid=(nn, nm, nk), in_specs=in_specs,
        out_specs=pl.BlockSpec((tm, tn), lambda j, i, k: (i, j)),
        out_shape=jax.ShapeDtypeStruct((m_all, n_all), out_dtype),
        scratch_shapes=[pltpu.VMEM((tm, tn), F32)] if nk > 1 else [],
        compiler_params=_params(("parallel", "parallel", "arbitrary")),
    )(*operands)


def _mm_nt(a, b, *, name, out_dtype, tm, tn, tk, b_blocked=False):
    m_all, k_all = a.shape
    if b_blocked:
        g_all, n_all, nb = b.shape
        assert g_all * nb == k_all and nb % tk == 0
        r = nb // tk
        b_spec = pl.BlockSpec((None, tn, tk), lambda j, i, k: (k // r, j, k % r))
    else:
        n_all, kb = b.shape
        assert kb == k_all
        b_spec = pl.BlockSpec((tn, tk), lambda j, i, k: (j, k))
    assert m_all % tm == 0 and n_all % tn == 0 and k_all % tk == 0
    nm, nn, nk = m_all // tm, n_all // tn, k_all // tk

    def body(a_ref, b_ref, o_ref, *scratch):
        prod = lax.dot_general(a_ref[...], b_ref[...], _NT, preferred_element_type=F32)
        if nk == 1:
            o_ref[...] = prod.astype(o_ref.dtype)
        else:
            acc_ref = scratch[0]
            k = pl.program_id(2)

            @pl.when(k == 0)
            def _():
                acc_ref[...] = prod

            @pl.when(k > 0)
            def _():
                acc_ref[...] += prod

            @pl.when(k == nk - 1)
            def _():
                o_ref[...] = acc_ref[...].astype(o_ref.dtype)

    return pl.pallas_call(
        body, name=name, grid=(nn, nm, nk),
        in_specs=[pl.BlockSpec((tm, tk), lambda j, i, k: (i, k)), b_spec],
        out_specs=pl.BlockSpec((tm, tn), lambda j, i, k: (i, j)),
        out_shape=jax.ShapeDtypeStruct((m_all, n_all), out_dtype),
        scratch_shapes=[pltpu.VMEM((tm, tn), F32)] if nk > 1 else [],
        compiler_params=_params(("parallel", "parallel", "arbitrary")),
    )(a, b)


def _mm_tn(a, b, *, name, out_dtype, tm, tn, tk, out_block=None):
    t_all, m_all = a.shape
    tb, n_all = b.shape
    assert tb == t_all and m_all % tm == 0 and n_all % tn == 0 and t_all % tk == 0
    nm, nn, nk = m_all // tm, n_all // tn, t_all // tk
    if out_block is None:
        out_shape = jax.ShapeDtypeStruct((m_all, n_all), out_dtype)
        out_spec = pl.BlockSpec((tm, tn), lambda i, j, k: (i, j))
    else:
        assert out_block % tn == 0 and n_all % out_block == 0
        r = out_block // tn
        out_shape = jax.ShapeDtypeStruct((n_all // out_block, m_all, out_block), out_dtype)
        out_spec = pl.BlockSpec((None, tm, tn), lambda i, j, k: (j // r, i, j % r))

    def body(a_ref, b_ref, o_ref, *scratch):
        prod = lax.dot_general(a_ref[...], b_ref[...], _TN, preferred_element_type=F32)
        if nk == 1:
            o_ref[...] = prod.astype(o_ref.dtype)
        else:
            acc_ref = scratch[0]
            k = pl.program_id(2)

            @pl.when(k == 0)
            def _():
                acc_ref[...] = prod

            @pl.when(k > 0)
            def _():
                acc_ref[...] += prod

            @pl.when(k == nk - 1)
            def _():
                o_ref[...] = acc_ref[...].astype(o_ref.dtype)

    return pl.pallas_call(
        body, name=name, grid=(nm, nn, nk),
        in_specs=[pl.BlockSpec((tk, tm), lambda i, j, k: (k, i)), pl.BlockSpec((tk, tn), lambda i, j, k: (k, j))],
        out_specs=out_spec, out_shape=out_shape,
        scratch_shapes=[pltpu.VMEM((tm, tn), F32)] if nk > 1 else [],
        compiler_params=_params(("parallel", "parallel", "arbitrary")),
    )(a, b)


def _rmsnorm_fwd(x, g, *, name, tm):
    t_all, d = x.shape

    def body(x_ref, g_ref, o_ref):
        xv = x_ref[...]
        rstd = lax.rsqrt(jnp.mean(xv * xv, axis=-1, keepdims=True) + RMS_EPS)
        o_ref[...] = (xv * rstd * g_ref[...]).astype(o_ref.dtype)

    return pl.pallas_call(
        body, name=name, grid=(t_all // tm,),
        in_specs=[pl.BlockSpec((tm, d), lambda i: (i, 0)), pl.BlockSpec((1, d), lambda i: (0, 0))],
        out_specs=pl.BlockSpec((tm, d), lambda i: (i, 0)),
        out_shape=jax.ShapeDtypeStruct((t_all, d), BF16),
        compiler_params=_params(("parallel",)),
    )(x, g)


def _rms_bwd_math(dy, xv, g):
    rstd = lax.rsqrt(jnp.mean(xv * xv, axis=-1, keepdims=True) + RMS_EPS)
    xhat = xv * rstd
    dxh = dy * g
    dx = rstd * (dxh - xhat * jnp.mean(dxh * xhat, axis=-1, keepdims=True))
    return dx, jnp.sum(dy * xhat, axis=0, keepdims=True)


def _rmsnorm_bwd(dy, x, g, res, *, name, tm):
    t_all, d = x.shape

    def body(dy_ref, x_ref, g_ref, res_ref, dx_ref, dxb_ref, dg_ref):
        dx, dg = _rms_bwd_math(dy_ref[...], x_ref[...], g_ref[...])
        tot = res_ref[...] + dx
        dx_ref[...] = tot
        dxb_ref[...] = tot.astype(BF16)

        @pl.when(pl.program_id(0) == 0)
        def _():
            dg_ref[...] = dg

        @pl.when(pl.program_id(0) > 0)
        def _():
            dg_ref[...] += dg

    row = pl.BlockSpec((tm, d), lambda i: (i, 0))
    vec = pl.BlockSpec((1, d), lambda i: (0, 0))
    return pl.pallas_call(
        body, name=name, grid=(t_all // tm,),
        in_specs=[row, row, vec, row], out_specs=[row, row, vec],
        out_shape=[jax.ShapeDtypeStruct((t_all, d), F32), jax.ShapeDtypeStruct((t_all, d), BF16),
                   jax.ShapeDtypeStruct((1, d), F32)],
        compiler_params=_params(("arbitrary",)),
    )(dy, x, g, res)


def _loss_head(h, target, g, *, name, tm, n_real):
    t_all, d = h.shape

    def body(h_ref, t_ref, g_ref, loss_ref, dx_ref, dxb_ref, dg_ref):
        i = pl.program_id(0)
        xv = h_ref[...]
        gv = g_ref[...]
        rstd = lax.rsqrt(jnp.mean(xv * xv, axis=-1, keepdims=True) + RMS_EPS)
        y = xv * rstd * gv
        row = i * tm + lax.broadcasted_iota(jnp.int32, (tm, 1), 0)
        valid = (row >= N_META) & (row < n_real)
        err = jnp.where(valid, y - t_ref[...], 0.0)
        part = 0.5 * jnp.sum(jnp.mean(err * err, axis=-1, keepdims=True), axis=0, keepdims=True)
        dx, dg = _rms_bwd_math(err * (1.0 / d), xv, gv)
        dx_ref[...] = dx
        dxb_ref[...] = dx.astype(BF16)

        @pl.when(i == 0)
        def _():
            dg_ref[...] = dg
            loss_ref[...] = jnp.broadcast_to(part, loss_ref.shape)

        @pl.when(i > 0)
        def _():
            dg_ref[...] += dg
            loss_ref[...] += jnp.broadcast_to(part, loss_ref.shape)

    row = pl.BlockSpec((tm, d), lambda i: (i, 0))
    vec = pl.BlockSpec((1, d), lambda i: (0, 0))
    return pl.pallas_call(
        body, name=name, grid=(t_all // tm,),
        in_specs=[row, row, vec],
        out_specs=[pl.BlockSpec((1, LANES), lambda i: (0, 0)), row, row, vec],
        out_shape=[jax.ShapeDtypeStruct((1, LANES), F32), jax.ShapeDtypeStruct((t_all, d), F32),
                   jax.ShapeDtypeStruct((t_all, d), BF16), jax.ShapeDtypeStruct((1, d), F32)],
        compiler_params=_params(("arbitrary",)),
    )(h, target, g)


def _swiglu_fwd(gu, *, name, tm, tc):
    t_all, f2 = gu.shape
    f = f2 // 2
    nc = f // tc

    def body(g_ref, u_ref, o_ref):
        gv = g_ref[...]
        o_ref[...] = (gv * _sigmoid(gv) * u_ref[...]).astype(o_ref.dtype)

    return pl.pallas_call(
        body, name=name, grid=(t_all // tm, nc),
        in_specs=[pl.BlockSpec((tm, tc), lambda i, j: (i, j)), pl.BlockSpec((tm, tc), lambda i, j: (i, nc + j))],
        out_specs=pl.BlockSpec((tm, tc), lambda i, j: (i, j)),
        out_shape=jax.ShapeDtypeStruct((t_all, f), BF16),
        compiler_params=_params(("parallel", "parallel")),
    )(gu, gu)


def _swiglu_bwd(dact, gu, *, name, tm, tc):
    t_all, f2 = gu.shape
    f = f2 // 2
    nc = f // tc

    def body(d_ref, g_ref, u_ref, o_ref):
        jj = pl.program_id(1)
        gv = g_ref[...]
        sg = _sigmoid(gv)
        dv = d_ref[...]

        @pl.when(jj < nc)
        def _():
            o_ref[...] = (dv * u_ref[...] * (sg * (1.0 + gv * (1.0 - sg)))).astype(o_ref.dtype)

        @pl.when(jj >= nc)
        def _():
            o_ref[...] = (dv * gv * sg).astype(o_ref.dtype)

    return pl.pallas_call(
        body, name=name, grid=(t_all // tm, 2 * nc),
        in_specs=[pl.BlockSpec((tm, tc), lambda i, j: (i, j % nc)), pl.BlockSpec((tm, tc), lambda i, j: (i, j % nc)),
                  pl.BlockSpec((tm, tc), lambda i, j: (i, nc + j % nc))],
        out_specs=pl.BlockSpec((tm, tc), lambda i, j: (i, j)),
        out_shape=jax.ShapeDtypeStruct((t_all, f2), BF16),
        compiler_params=_params(("parallel", "parallel")),
    )(dact, gu, gu)


def _mla_prep_fwd(proj, qn, kvn, cos, sin, *, name, tm, lq, lkv):
    t_all, w = proj.shape

    def body(p_ref, qn_ref, kvn_ref, cos_ref, sin_ref, cq_ref, ckv_ref, kr_ref):
        pv = p_ref[...]
        xq = pv[:, :lq]
        xkv = pv[:, lq:lq + lkv]
        cq_ref[...] = (xq * lax.rsqrt(jnp.mean(xq * xq, axis=-1, keepdims=True) + RMS_EPS) * qn_ref[...]).astype(BF16)
        ckv_ref[...] = (xkv * lax.rsqrt(jnp.mean(xkv * xkv, axis=-1, keepdims=True) + RMS_EPS) * kvn_ref[...]).astype(BF16)
        kr_ref[...] = _rope(pv[:, lq + lkv:], cos_ref[...], sin_ref[...]).astype(BF16)

    def row(width):
        return pl.BlockSpec((tm, width), lambda i: (i, 0))

    def vec(width):
        return pl.BlockSpec((1, width), lambda i: (0, 0))

    return pl.pallas_call(
        body, name=name, grid=(t_all // tm,),
        in_specs=[row(w), vec(lq), vec(lkv), row(LANES), row(LANES)],
        out_specs=[row(lq), row(lkv), row(LANES)],
        out_shape=[jax.ShapeDtypeStruct((t_all, lq), BF16), jax.ShapeDtypeStruct((t_all, lkv), BF16),
                   jax.ShapeDtypeStruct((t_all, LANES), BF16)],
        compiler_params=_params(("parallel",)),
    )(proj, qn, kvn, cos, sin)


def _mla_prep_bwd(dcq, dckv, dkr_h, proj, qn, kvn, cos, sin, *, name, tm, lq, lkv):
    t_all, w = proj.shape
    n_heads = dkr_h.shape[0]

    def body(dcq_ref, dckv_ref, dkr_ref, p_ref, qn_ref, kvn_ref, cos_ref, sin_ref, dp_ref, dqn_ref, dkvn_ref):
        pv = p_ref[...]
        dxq, dqn = _rms_bwd_math(dcq_ref[...], pv[:, :lq], qn_ref[...])
        dxkv, dkvn = _rms_bwd_math(dckv_ref[...], pv[:, lq:lq + lkv], kvn_ref[...])
        dkr = dkr_ref[0]
        for hh in range(1, n_heads):
            dkr = dkr + dkr_ref[hh]
        dkr = _unrope(dkr, cos_ref[...], sin_ref[...])
        dp_ref[...] = jnp.concatenate([dxq, dxkv, dkr], axis=1).astype(BF16)

        @pl.when(pl.program_id(0) == 0)
        def _():
            dqn_ref[...] = dqn
            dkvn_ref[...] = dkvn

        @pl.when(pl.program_id(0) > 0)
        def _():
            dqn_ref[...] += dqn
            dkvn_ref[...] += dkvn

    def row(width):
        return pl.BlockSpec((tm, width), lambda i: (i, 0))

    def vec(width):
        return pl.BlockSpec((1, width), lambda i: (0, 0))

    return pl.pallas_call(
        body, name=name, grid=(t_all // tm,),
        in_specs=[row(lq), row(lkv), pl.BlockSpec((n_heads, tm, LANES), lambda i: (0, i, 0)), row(w),
                  vec(lq), vec(lkv), row(LANES), row(LANES)],
        out_specs=[row(w), vec(lq), vec(lkv)],
        out_shape=[jax.ShapeDtypeStruct((t_all, w), BF16), jax.ShapeDtypeStruct((1, lq), F32),
                   jax.ShapeDtypeStruct((1, lkv), F32)],
        compiler_params=_params(("arbitrary",)),
    )(dcq, dckv, dkr_h, proj, qn, kvn, cos, sin)


def _rope_q_epilogue(acc, cos_ref, sin_ref):
    parts = []
    for g in range(acc.shape[1] // LANES):
        blk = acc[:, g * LANES:(g + 1) * LANES]
        parts.append(_rope(blk, cos_ref[...], sin_ref[...]) if g % 2 == 1 else blk)
    return jnp.concatenate(parts, axis=1)


def _chunk_causal(rows, cols):
    r = lax.broadcasted_iota(jnp.int32, (rows, cols), 0)
    c = lax.broadcasted_iota(jnp.int32, (rows, cols), 1)
    return (c >> 6) <= (r >> 6)


def _meta_keys(rows, cols):
    return lax.broadcasted_iota(jnp.int32, (rows, cols), 1) < N_META


def _attn_fwd(q, kv, kr, *, name, n_heads, tq, n_real, scale):
    t_all = q.shape[0]
    nq = (n_real - N_META) // tq
    assert N_META + nq * tq == n_real and tq % CHUNK == 0 and t_all >= LANES
    n_pad = t_all - n_real

    def body(q_ref, kv_ref, kr_ref, o_ref, lse_ref, k_scr, m_scr, l_scr, acc_scr):
        k_scr[:, :QK_NOPE] = kv_ref[:, :QK_NOPE]
        k_scr[:, QK_NOPE:] = kr_ref[...]
        if n_pad:
            o_ref[pl.ds(n_real, n_pad), :] = jnp.zeros((n_pad, V_HEAD), o_ref.dtype)
            lse_ref[pl.ds(n_real, n_pad), :] = jnp.zeros((n_pad, LANES), F32)

        def scores(qt, c0, width):
            return lax.dot_general(qt, k_scr[pl.ds(c0, width), :], _NT, preferred_element_type=F32) * scale

        def values(c0, width):
            return kv_ref[pl.ds(c0, width), QK_NOPE:]

        s = jnp.where(_meta_keys(LANES, LANES), scores(q_ref[pl.ds(0, LANES), :], 0, LANES), NEG_BIG)
        m = jnp.max(s, axis=-1, keepdims=True)
        p = jnp.exp(s - m)
        l = jnp.sum(p, axis=-1, keepdims=True)
        o_meta = jnp.dot(p.astype(BF16), values(0, LANES), preferred_element_type=F32) / l
        o_ref[pl.ds(0, N_META), :] = o_meta[:N_META].astype(o_ref.dtype)
        lse_ref[pl.ds(0, N_META), :] = jnp.broadcast_to((m + jnp.log(l))[:N_META], (N_META, LANES))

        def accumulate(s, c0, width):
            m_prev = m_scr[...]
            m_new = jnp.maximum(m_prev, jnp.max(s, axis=-1, keepdims=True))
            alpha = jnp.exp(m_prev - m_new)
            p = jnp.exp(s - m_new)
            l_scr[...] = alpha * l_scr[...] + jnp.sum(p, axis=-1, keepdims=True)
            acc_scr[...] = alpha * acc_scr[...] + jnp.dot(p.astype(BF16), values(c0, width), preferred_element_type=F32)
            m_scr[...] = m_new

        def q_tile(i, carry):
            r0 = pl.multiple_of(N_META + i * tq, N_META)
            qt = q_ref[pl.ds(r0, tq), :]
            s = jnp.where(_meta_keys(tq, LANES), scores(qt, 0, LANES), NEG_BIG)
            m = jnp.max(s, axis=-1, keepdims=True)
            p = jnp.exp(s - m)
            m_scr[...] = m
            l_scr[...] = jnp.sum(p, axis=-1, keepdims=True)
            acc_scr[...] = jnp.dot(p.astype(BF16), values(0, LANES), preferred_element_type=F32)

            def full_block(j, c):
                c0 = pl.multiple_of(N_META + j * tq, N_META)
                accumulate(scores(qt, c0, tq), c0, tq)
                return c

            lax.fori_loop(0, i, full_block, 0)
            accumulate(jnp.where(_chunk_causal(tq, tq), scores(qt, r0, tq), NEG_BIG), r0, tq)
            o_ref[pl.ds(r0, tq), :] = (acc_scr[...] / l_scr[...]).astype(o_ref.dtype)
            lse_ref[pl.ds(r0, tq), :] = jnp.broadcast_to(m_scr[...] + jnp.log(l_scr[...]), (tq, LANES))
            return carry

        lax.fori_loop(0, nq, q_tile, 0)

    def head(width):
        return pl.BlockSpec((t_all, width), lambda h: (0, h))

    return pl.pallas_call(
        body, name=name, grid=(n_heads,),
        in_specs=[head(HEAD_W), head(HEAD_W), pl.BlockSpec((t_all, LANES), lambda h: (0, 0))],
        out_specs=[head(V_HEAD), pl.BlockSpec((None, t_all, LANES), lambda h: (h, 0, 0))],
        out_shape=[jax.ShapeDtypeStruct((t_all, n_heads * V_HEAD), BF16),
                   jax.ShapeDtypeStruct((n_heads, t_all, LANES), F32)],
        scratch_shapes=[pltpu.VMEM((t_all, HEAD_W), BF16), pltpu.VMEM((tq, 1), F32), pltpu.VMEM((tq, 1), F32),
                        pltpu.VMEM((tq, V_HEAD), F32)],
        compiler_params=_params(("parallel",)),
    )(q, kv, kr)


def _attn_bwd(q, kv, kr, o, lse, do, cos, sin, *, name, n_heads, tq, n_real, scale):
    t_all = q.shape[0]
    nq = (n_real - N_META) // tq
    assert N_META + nq * tq == n_real and tq % CHUNK == 0 and t_all >= LANES
    n_pad = t_all - n_real

    def body(q_ref, kv_ref, kr_ref, o_ref, lse_ref, do_ref, cos_ref, sin_ref, dq_ref, dkv_ref, dkr_ref,
             k_scr, dk_scr, dv_scr, dq_scr):
        k_scr[:, :QK_NOPE] = kv_ref[:, :QK_NOPE]
        k_scr[:, QK_NOPE:] = kr_ref[...]
        dk_scr[...] = jnp.zeros(dk_scr.shape, F32)
        dv_scr[...] = jnp.zeros(dv_scr.shape, F32)
        if n_pad:
            dq_ref[pl.ds(n_real, n_pad), :] = jnp.zeros((n_pad, HEAD_W), dq_ref.dtype)

        def block(qt, dot, lse_t, delta, c0, width, mask):
            kb = k_scr[pl.ds(c0, width), :]
            s = lax.dot_general(qt, kb, _NT, preferred_element_type=F32) * scale
            p = jnp.exp(s - lse_t)
            if mask is not None:
                p = jnp.where(mask, p, 0.0)
            dp = lax.dot_general(dot, kv_ref[pl.ds(c0, width), QK_NOPE:], _NT, preferred_element_type=F32)
            ds = (p * (dp - delta) * scale).astype(BF16)
            dv_scr[pl.ds(c0, width), :] += lax.dot_general(p.astype(BF16), dot, _TN, preferred_element_type=F32)
            dk_scr[pl.ds(c0, width), :] += lax.dot_general(ds, qt, _TN, preferred_element_type=F32)
            return jnp.dot(ds, kb, preferred_element_type=F32)

        def write_dq(r0, rows, dq):
            cs, sn = cos_ref[pl.ds(r0, rows), :], sin_ref[pl.ds(r0, rows), :]
            dq_ref[pl.ds(r0, rows), :] = jnp.concatenate(
                [dq[:, :QK_NOPE], _unrope(dq[:, QK_NOPE:], cs, sn)], axis=1).astype(dq_ref.dtype)

        rows_m = lax.broadcasted_iota(jnp.int32, (LANES, LANES), 0) < N_META
        dot = do_ref[pl.ds(0, LANES), :]
        delta = jnp.sum(dot.astype(F32) * o_ref[pl.ds(0, LANES), :].astype(F32), axis=-1, keepdims=True)
        dq = block(q_ref[pl.ds(0, LANES), :], dot, lse_ref[pl.ds(0, LANES), :1], delta, 0, LANES,
                   _meta_keys(LANES, LANES) & rows_m)
        write_dq(0, N_META, dq[:N_META])

        def q_tile(i, carry):
            r0 = pl.multiple_of(N_META + i * tq, N_META)
            qt = q_ref[pl.ds(r0, tq), :]
            dot = do_ref[pl.ds(r0, tq), :]
            lse_t = lse_ref[pl.ds(r0, tq), :1]
            delta = jnp.sum(dot.astype(F32) * o_ref[pl.ds(r0, tq), :].astype(F32), axis=-1, keepdims=True)
            dq_scr[...] = block(qt, dot, lse_t, delta, 0, LANES, _meta_keys(tq, LANES))

            def full_block(j, c):
                c0 = pl.multiple_of(N_META + j * tq, N_META)
                dq_scr[...] += block(qt, dot, lse_t, delta, c0, tq, None)
                return c

            lax.fori_loop(0, i, full_block, 0)
            dq_scr[...] += block(qt, dot, lse_t, delta, r0, tq, _chunk_causal(tq, tq))
            write_dq(r0, tq, dq_scr[...])
            return carry

        lax.fori_loop(0, nq, q_tile, 0)
        dk = dk_scr[...]
        dkv_ref[...] = jnp.concatenate([dk[:, :QK_NOPE], dv_scr[...]], axis=1).astype(dkv_ref.dtype)
        dkr_ref[...] = dk[:, QK_NOPE:]

    def head(width):
        return pl.BlockSpec((t_all, width), lambda h: (0, h))

    table = pl.BlockSpec((t_all, LANES), lambda h: (0, 0))
    per_head = pl.BlockSpec((None, t_all, LANES), lambda h: (h, 0, 0))
    return pl.pallas_call(
        body, name=name, grid=(n_heads,),
        in_specs=[head(HEAD_W), head(HEAD_W), table, head(V_HEAD), per_head, head(V_HEAD), table, table],
        out_specs=[head(HEAD_W), head(HEAD_W), per_head],
        out_shape=[jax.ShapeDtypeStruct((t_all, n_heads * HEAD_W), BF16), jax.ShapeDtypeStruct((t_all, n_heads * HEAD_W), BF16),
                   jax.ShapeDtypeStruct((n_heads, t_all, LANES), F32)],
        scratch_shapes=[pltpu.VMEM((t_all, HEAD_W), BF16), pltpu.VMEM((t_all, HEAD_W), F32), pltpu.VMEM((t_all, V_HEAD), F32),
                        pltpu.VMEM((tq, HEAD_W), F32)],
        compiler_params=_params(("parallel",)),
    )(q, kv, kr, o, lse, do, cos, sin)


LRU_ROWS = 128


def _shifted_back(ref, t0, rows, shift_max):
    main = ref[pl.ds(t0, rows), :]
    prev = ref[pl.ds(pl.multiple_of(jnp.maximum(t0 - SUBLANES, 0), SUBLANES), SUBLANES), :]
    prev = jnp.where(t0 > 0, prev, 0.0)
    ext = jnp.concatenate([prev, main], axis=0)
    return [main] + [pltpu.roll(ext, s, 0)[SUBLANES:, :] for s in range(1, shift_max + 1)]


def _shifted_ahead(ref, t0, rows, t_all, shift_max):
    main = ref[pl.ds(t0, rows), :]
    nxt = ref[pl.ds(pl.multiple_of(jnp.minimum(t0 + rows, t_all - SUBLANES), SUBLANES), SUBLANES), :]
    nxt = jnp.where(t0 + rows < t_all, nxt, 0.0)
    ext = jnp.concatenate([main, nxt], axis=0)
    return [main] + [pltpu.roll(ext, rows + SUBLANES - s, 0)[:rows, :] for s in range(1, shift_max + 1)]


def _conv_fwd(xp_ref, t0, rows, cw, cb):
    sh = _shifted_back(xp_ref, t0, rows, 3)
    out = cb + cw[3:4, :] * sh[0]
    for k in range(3):
        out = out + cw[k:k + 1, :] * sh[3 - k]
    return out, sh


def _lru_gates(xb, wga, bga, wgx, bgx, sp):
    xbb = xb.astype(BF16)
    r = _sigmoid(jnp.dot(xbb, wga, preferred_element_type=F32) + bga)
    ig = _sigmoid(jnp.dot(xbb, wgx, preferred_element_type=F32) + bgx)
    la = -LRU_C * r * sp
    a = jnp.exp(la)
    s = jnp.sqrt(_neg_expm1(2.0 * la))
    return xbb, r, ig, a, s


def _scan_tile(a, b, reverse):
    rows = a.shape[0]
    ridx = lax.broadcasted_iota(jnp.int32, a.shape, 0)
    s = 1
    while s < rows:
        if reverse:
            keep = ridx < rows - s
            a_sh, b_sh = pltpu.roll(a, rows - s, 0), pltpu.roll(b, rows - s, 0)
        else:
            keep = ridx >= s
            a_sh, b_sh = pltpu.roll(a, s, 0), pltpu.roll(b, s, 0)
        b = jnp.where(keep, a * b_sh + b, b)
        a = jnp.where(keep, a * a_sh, a)
        s *= 2
    return a, b


def _lru_fwd(xy, conv_w, conv_b, wga, bga, wgx, bgx, lam, *, name):
    t_all = xy.shape[0]
    dr = xy.shape[1] // 2
    c = LANES
    nblk = dr // c
    rows = LRU_ROWS
    nt = t_all // rows

    def body(xp_ref, yp_ref, cw_ref, cb_ref, wga_ref, bga_ref, wgx_ref, bgx_ref, lam_ref, hs_ref, hsy_ref):
        cw, cb = cw_ref[...], cb_ref[...]
        sp = _softplus_neg(lam_ref[...])

        def tile(t, h_in):
            t0 = pl.multiple_of(t * rows, rows)
            xb, _ = _conv_fwd(xp_ref, t0, rows, cw, cb)
            _, _, ig, a, s = _lru_gates(xb, wga_ref[0], bga_ref[...], wgx_ref[0], bgx_ref[...], sp)
            cum_a, h0 = _scan_tile(a, s * (ig * xb), reverse=False)
            hs = cum_a * h_in + h0
            hs_ref[pl.ds(t0, rows), :] = hs
            hsy_ref[pl.ds(t0, rows), :] = (hs * _gelu(yp_ref[pl.ds(t0, rows), :])).astype(BF16)
            return hs[rows - 1:, :]

        lax.fori_loop(0, nt, tile, jnp.zeros((1, c), F32))

    col = pl.BlockSpec((t_all, c), lambda b: (0, b))
    vec = pl.BlockSpec((1, c), lambda b: (0, b))
    wsp = pl.BlockSpec((1, c, c), lambda b: (b, 0, 0))
    return pl.pallas_call(
        body, name=name, grid=(nblk,),
        in_specs=[col, pl.BlockSpec((t_all, c), lambda b: (0, nblk + b)), pl.BlockSpec((4, c), lambda b: (0, b)), vec,
                  wsp, vec, wsp, vec, vec],
        out_specs=[col, col],
        out_shape=[jax.ShapeDtypeStruct((t_all, dr), F32), jax.ShapeDtypeStruct((t_all, dr), BF16)],
        compiler_params=_params(("parallel",)),
    )(xy, xy, conv_w, conv_b, wga, bga, wgx, bgx, lam)


def _lru_bwd(xy, hs, dhsy, conv_w, conv_b, wga, bga, wgx, bgx, lam, *, name):
    t_all = xy.shape[0]
    dr = xy.shape[1] // 2
    c = LANES
    nblk = dr // c
    rows = LRU_ROWS
    nt = t_all // rows

    def body(xp_ref, yp_ref, hs_ref, dh_ref, cw_ref, cb_ref, wga_ref, bga_ref, wgx_ref, bgx_ref, lam_ref,
             dxp_ref, dyp_ref, dcw_ref, dcb_ref, dwga_ref, dbga_ref, dwgx_ref, dbgx_ref, dlam_ref,
             xb_scr, r_scr, i_scr, a_scr):
        cw, cb = cw_ref[...], cb_ref[...]
        lamv = lam_ref[...]
        sp = _softplus_neg(lamv)
        sig_neg = 1.0 / (1.0 + jnp.exp(lamv))
        wga_v, wgx_v = wga_ref[0], wgx_ref[0]

        def recompute(t, carry):
            t0 = pl.multiple_of(t * rows, rows)
            xb, _ = _conv_fwd(xp_ref, t0, rows, cw, cb)
            _, r, ig, a, _ = _lru_gates(xb, wga_v, bga_ref[...], wgx_v, bgx_ref[...], sp)
            xb_scr[pl.ds(t0, rows), :] = xb
            r_scr[pl.ds(t0, rows), :] = r
            i_scr[pl.ds(t0, rows), :] = ig
            a_scr[pl.ds(t0, rows), :] = a
            return carry

        lax.fori_loop(0, nt, recompute, 0)
        dwga_ref[...] = jnp.zeros(dwga_ref.shape, F32)
        dwgx_ref[...] = jnp.zeros(dwgx_ref.shape, F32)

        def tile(ti, carry):
            lam_in, dbga, dbgx, dlam, dcw, dcb = carry
            t = nt - 1 - ti
            t0 = pl.multiple_of(t * rows, rows)
            a_now, a_next = _shifted_ahead(a_scr, t0, rows, t_all, 1)
            yp = yp_ref[pl.ds(t0, rows), :]
            dhy = dh_ref[pl.ds(t0, rows), :]
            cum_a, lam0 = _scan_tile(a_next, dhy * _gelu(yp), reverse=True)
            lam_t = cum_a * lam_in + lam0
            hs_now, hs_prev = _shifted_back(hs_ref, t0, rows, 1)
            da = lam_t * hs_prev
            xb = xb_scr[pl.ds(t0, rows), :]
            r = r_scr[pl.ds(t0, rows), :]
            ig = i_scr[pl.ds(t0, rows), :]
            la = -LRU_C * r * sp
            s = jnp.sqrt(_neg_expm1(2.0 * la))
            d_ixb = lam_t * s
            dla = da * a_now - (lam_t * ig * xb) * (a_now * a_now / s)
            dzr = dla * (-LRU_C * sp) * r * (1.0 - r)
            dzi = d_ixb * xb * ig * (1.0 - ig)
            dzr_b, dzi_b = dzr.astype(BF16), dzi.astype(BF16)
            xbb = xb.astype(BF16)
            dwga_ref[0] += lax.dot_general(xbb, dzr_b, _TN, preferred_element_type=F32)
            dwgx_ref[0] += lax.dot_general(xbb, dzi_b, _TN, preferred_element_type=F32)
            dxb = (d_ixb * ig + lax.dot_general(dzr_b, wga_v, _NT, preferred_element_type=F32)
                   + lax.dot_general(dzi_b, wgx_v, _NT, preferred_element_type=F32))
            xb_scr[pl.ds(t0, rows), :] = dxb
            dyp_ref[pl.ds(t0, rows), :] = (dhy * hs_now * _gelu_grad(yp)).astype(BF16)
            ahead = _shifted_ahead(xb_scr, t0, rows, t_all, 3)
            dxp = cw[3:4, :] * ahead[0]
            for k in range(3):
                dxp = dxp + cw[k:k + 1, :] * ahead[3 - k]
            dxp_ref[pl.ds(t0, rows), :] = dxp.astype(BF16)
            back = _shifted_back(xp_ref, t0, rows, 3)
            dcw_t = jnp.concatenate([jnp.sum(dxb * back[3 - k], axis=0, keepdims=True) for k in range(4)], axis=0)
            return (lam_t[:1, :], dbga + jnp.sum(dzr, axis=0, keepdims=True), dbgx + jnp.sum(dzi, axis=0, keepdims=True),
                    dlam + jnp.sum(dla * r, axis=0, keepdims=True), dcw + dcw_t, dcb + jnp.sum(dxb, axis=0, keepdims=True))

        zero = jnp.zeros((1, c), F32)
        _, dbga, dbgx, dlam, dcw, dcb = lax.fori_loop(0, nt, tile, (zero, zero, zero, zero, jnp.zeros((4, c), F32), zero))
        dbga_ref[...] = dbga
        dbgx_ref[...] = dbgx
        dlam_ref[...] = dlam * (LRU_C * sig_neg)
        dcw_ref[...] = dcw
        dcb_ref[...] = dcb

    col = pl.BlockSpec((t_all, c), lambda b: (0, b))
    col2 = pl.BlockSpec((t_all, c), lambda b: (0, nblk + b))
    vec = pl.BlockSpec((1, c), lambda b: (0, b))
    tap = pl.BlockSpec((4, c), lambda b: (0, b))
    wsp = pl.BlockSpec((1, c, c), lambda b: (b, 0, 0))
    vshape = jax.ShapeDtypeStruct((1, dr), F32)
    wshape = jax.ShapeDtypeStruct((nblk, c, c), F32)
    dxp, dyp, dcw, dcb, dwga, dbga, dwgx, dbgx, dlam = pl.pallas_call(
        body, name=name, grid=(nblk,),
        in_specs=[col, col2, col, col, tap, vec, wsp, vec, wsp, vec, vec],
        out_specs=[col, col, tap, vec, wsp, vec, wsp, vec, vec],
        out_shape=[jax.ShapeDtypeStruct((t_all, dr), BF16), jax.ShapeDtypeStruct((t_all, dr), BF16),
                   jax.ShapeDtypeStruct((4, dr), F32), vshape, wshape, vshape, wshape, vshape, vshape],
        scratch_shapes=[pltpu.VMEM((t_all, c), F32)] * 4,
        compiler_params=_params(("parallel",)),
    )(xy, xy, hs, dhsy, conv_w, conv_b, wga, bga, wgx, bgx, lam)
    return jnp.concatenate([dxp, dyp], axis=1), dcw, dcb, dwga, dbga, dwgx, dbgx, dlam


def _mesh_pos():
    return lax.axis_index("x"), lax.axis_index("y"), lax.axis_index("c")


def _all_gather(shards, *, name):
    n = len(shards)

    def body(*refs):
        ins, outs = refs[:n], refs[n:2 * n]
        send_sems, recv_sems, local_sems = refs[2 * n:]
        x, y, c = _mesh_pos()
        me, sibling = (x, y, c), (x, y, 1 - c)
        chips = [(1 - x, y), (x, 1 - y), (1 - x, 1 - y)]

        def slot(p):
            return 4 * p[0] + 2 * p[1] + p[2]

        def copy(a, k, block, to, src=None):
            dst = outs[a].at[slot(block)]
            return pltpu.make_async_remote_copy(
                src_ref=dst if src is None else src, dst_ref=dst, send_sem=send_sems.at[a, k],
                recv_sem=recv_sems.at[a, k], device_id=to, device_id_type=MESH)

        mine = [pltpu.make_async_copy(ins[a], outs[a].at[slot(me)], local_sems.at[a]) for a in range(n)]
        for cp in mine:
            cp.start()
        first = []
        for a in range(n):
            first.append(copy(a, 0, me, sibling, src=ins[a]))
            first += [copy(a, 1 + j, me, (*chip, c), src=ins[a]) for j, chip in enumerate(chips)]
        for cp in first:
            cp.start()
        passed = []
        for a in range(n):
            for j, chip in enumerate(chips):
                copy(a, 1 + j, (*chip, c), me).wait_recv()
                fwd = copy(a, 4 + j, (*chip, c), sibling)
                fwd.start()
                passed.append(fwd)
        for a in range(n):
            copy(a, 0, sibling, me).wait_recv()
            for j, chip in enumerate(chips):
                copy(a, 4 + j, (*chip, 1 - c), me).wait_recv()
        for cp in first + passed:
            cp.wait_send()
        for cp in mine:
            cp.wait()

    any_spec = pl.BlockSpec(memory_space=pl.ANY)
    return pl.pallas_call(
        body, name=name,
        in_specs=[any_spec] * n, out_specs=[any_spec] * n,
        out_shape=[jax.ShapeDtypeStruct((N_DEV,) + s.shape, s.dtype) for s in shards],
        scratch_shapes=[pltpu.SemaphoreType.DMA((n, 7)), pltpu.SemaphoreType.DMA((n, 7)), pltpu.SemaphoreType.DMA((n,))],
    )(*shards)


def _rs_sibling(grads, *, name):
    n = len(grads)

    def body(*refs):
        ins, outs = refs[:n], refs[n:2 * n]
        send_sems, recv_sems = refs[2 * n:]
        x, y, c = _mesh_pos()
        copies = []
        for a in range(n):
            for j in range(4):
                copies.append(pltpu.make_async_remote_copy(
                    src_ref=ins[a].at[2 * j + (1 - c)], dst_ref=outs[a].at[j], send_sem=send_sems.at[a, j],
                    recv_sem=recv_sems.at[a, j], device_id=(x, y, 1 - c), device_id_type=MESH))
        for cp in copies:
            cp.start()
        for cp in copies:
            cp.wait()

    any_spec = pl.BlockSpec(memory_space=pl.ANY)
    return pl.pallas_call(
        body, name=name, in_specs=[any_spec] * n, out_specs=[any_spec] * n,
        out_shape=[jax.ShapeDtypeStruct((4,) + g.shape[1:], g.dtype) for g in grads],
        scratch_shapes=[pltpu.SemaphoreType.DMA((n, 4)), pltpu.SemaphoreType.DMA((n, 4))],
    )(*grads)


def _rs_chips(parts, *, name):
    n = len(parts)

    def body(*refs):
        ins, outs = refs[:n], refs[n:2 * n]
        send_sems, recv_sems, local_sems = refs[2 * n:]
        x, y, c = _mesh_pos()
        copies, local = [], []
        for a in range(n):
            for k in (1, 2, 3):
                px = 1 - x if k & 2 else x
                py = 1 - y if k & 1 else y
                copies.append(pltpu.make_async_remote_copy(
                    src_ref=ins[a].at[2 * px + py], dst_ref=outs[a].at[k - 1], send_sem=send_sems.at[a, k - 1],
                    recv_sem=recv_sems.at[a, k - 1], device_id=(px, py, c), device_id_type=MESH))
            local.append(pltpu.make_async_copy(ins[a].at[2 * x + y], outs[a].at[3], local_sems.at[a]))
        for cp in copies + local:
            cp.start()
        for cp in copies + local:
            cp.wait()

    any_spec = pl.BlockSpec(memory_space=pl.ANY)
    return pl.pallas_call(
        body, name=name, in_specs=[any_spec] * n, out_specs=[any_spec] * n,
        out_shape=[jax.ShapeDtypeStruct(p.shape, p.dtype) for p in parts],
        scratch_shapes=[pltpu.SemaphoreType.DMA((n, 3)), pltpu.SemaphoreType.DMA((n, 3)), pltpu.SemaphoreType.DMA((n,))],
    )(*parts)


def _pair_add(grads, landed, core, *, name, tr):
    _, r_all, c_all = grads.shape

    def body(core_ref, g_ref, l_ref, o_ref):
        o_ref[...] = (g_ref[...].astype(F32) + l_ref[...].astype(F32)).astype(o_ref.dtype)

    return pl.pallas_call(
        body, name=name,
        grid_spec=pltpu.PrefetchScalarGridSpec(
            num_scalar_prefetch=1, grid=(4, r_all // tr),
            in_specs=[pl.BlockSpec((None, tr, c_all), lambda j, i, core_ref: (2 * j + core_ref[0], i, 0)),
                      pl.BlockSpec((None, tr, c_all), lambda j, i, core_ref: (j, i, 0))],
            out_specs=pl.BlockSpec((None, tr, c_all), lambda j, i, core_ref: (j, i, 0))),
        out_shape=jax.ShapeDtypeStruct((4, r_all, c_all), grads.dtype),
        compiler_params=_params(("parallel", "parallel")),
    )(core, grads, landed)


def _adamw_math(w, g, m, v):
    m2 = ADAM_B1 * m + (1.0 - ADAM_B1) * g
    v2 = ADAM_B2 * v + (1.0 - ADAM_B2) * (g * g)
    m_hat = m2 / (1.0 - ADAM_B1 ** ADAM_STEP)
    v_hat = v2 / (1.0 - ADAM_B2 ** ADAM_STEP)
    delta = -ADAM_LR * (m_hat / (jnp.sqrt(v_hat) + ADAM_EPS) + ADAM_WD * w)
    return delta, m2, v2


def _adamw(w, m, v, terms, order, *, name, tr, col_block=None):
    r_all, c_all = w.shape
    n_slots = terms.shape[0]

    def body(*refs):
        if col_block is not None:
            refs = refs[1:]
        w_ref, m_ref, v_ref, t_ref, g_ref, d_ref, m2_ref, v2_ref = refs
        g = t_ref[order[0]].astype(F32)
        for s in order[1:]:
            g = g + t_ref[s].astype(F32)
        delta, m2, v2 = _adamw_math(w_ref[...], g, m_ref[...], v_ref[...])
        g_ref[...] = g
        d_ref[...] = delta
        m2_ref[...] = m2
        v2_ref[...] = v2

    shape = jax.ShapeDtypeStruct((r_all, c_all), F32)
    if col_block is None:
        row = pl.BlockSpec((tr, c_all), lambda i: (i, 0))
        return pl.pallas_call(
            body, name=name, grid=(r_all // tr,),
            in_specs=[row, row, row, pl.BlockSpec((n_slots, tr, c_all), lambda i: (0, i, 0))],
            out_specs=[row] * 4, out_shape=[shape] * 4, compiler_params=_params(("parallel",)),
        )(w, m, v, terms)
    row = pl.BlockSpec((tr, c_all), lambda i, blk: (i, 0))
    return pl.pallas_call(
        body, name=name,
        grid_spec=pltpu.PrefetchScalarGridSpec(
            num_scalar_prefetch=1, grid=(r_all // tr,),
            in_specs=[row, row, row, pl.BlockSpec((n_slots, tr, c_all), lambda i, blk: (0, i, blk[0]))],
            out_specs=[row] * 4),
        out_shape=[shape] * 4, compiler_params=_params(("parallel",)),
    )(col_block, w, m, v, terms)


def _rope_tables(t_all):
    pos = jnp.arange(t_all, dtype=F32)
    inv_freq = ROPE_THETA ** (-jnp.arange(0, QK_ROPE, 2, dtype=F32) / QK_ROPE)
    ang = pos[:, None] * inv_freq[None, :]
    cos, sin = jnp.cos(ang), jnp.sin(ang)
    return jnp.tile(cos, (1, LANES // (QK_ROPE // 2))), jnp.tile(sin, (1, LANES // (QK_ROPE // 2)))


def _adam_row_tile(r_all, c_all):
    target = max(SUBLANES, (512 * 1024) // (4 * c_all))
    return _pick(r_all, [t for t in (512, 256, 128, 64, 32, 16, 8) if t <= target])


def _rows_natural(wg):
    return wg.reshape(wg.shape[0] * wg.shape[1], wg.shape[2])


def _mla_layer_fwd(tag, h, g_mix, ws, qn, kvn, cos, sin, *, tm, tq, n_heads, scale, n_real):
    w_in, w_uq, w_ukv, w_o = _rows_natural(ws[0]), ws[1], ws[2], _rows_natural(ws[3])
    t_all, d = h.shape
    lq, lkv = qn.shape[1], kvn.shape[1]
    tmb = _pick(t_all, _ROW_TILES)
    hn = _rmsnorm_fwd(h, g_mix, name=f"norm_mix{tag}", tm=tm)
    proj = _mm_nn(hn, w_in, name=f"mla_in{tag}", out_dtype=F32, tm=tmb, tn=w_in.shape[1], tk=_pick(d, _DIVS))
    cq, ckv, kr = _mla_prep_fwd(proj, qn, kvn, cos, sin, name=f"mla_prep{tag}", tm=tm, lq=lq, lkv=lkv)
    q = _mm_nn(cq, w_uq, name=f"mla_q{tag}", out_dtype=BF16, tm=tmb, tn=w_uq.shape[2], tk=lq, b_blocked=True,
               epilogue=_rope_q_epilogue, extras=(cos, sin))
    kv = _mm_nn(ckv, w_ukv, name=f"mla_kv{tag}", out_dtype=BF16, tm=tmb, tn=w_ukv.shape[2], tk=lkv, b_blocked=True)
    o, lse = _attn_fwd(q, kv, kr, name=f"attn_fwd{tag}", n_heads=n_heads, tq=tq, n_real=n_real, scale=scale)
    h_mid = _mm_nn(o, w_o, name=f"mla_o{tag}", out_dtype=F32, tm=tmb, tn=_pick(d, _DIVS[2:]), tk=_pick(o.shape[1], _DIVS), res=h)
    return h_mid, (hn, proj, cq, ckv, kr, q, kv, o, lse)


def _mla_layer_bwd(tag, dh, dh_b, h_in, saved, g_mix, ws, qn, kvn, cos, sin, *, tm, tq, n_heads, scale, n_real):
    hn, proj, cq, ckv, kr, q, kv, o, lse = saved
    w_in, w_uq, w_ukv, w_o = _rows_natural(ws[0]), ws[1], ws[2], _rows_natural(ws[3])
    t_all, d = h_in.shape
    lq, lkv = qn.shape[1], kvn.shape[1]
    ov = o.shape[1]
    tmb = _pick(t_all, _ROW_TILES)
    tn_d, tk_d = _pick(d, _DIVS[1:]), _pick(d, _DIVS)
    do = _mm_nt(dh_b, w_o, name=f"mla_do{tag}", out_dtype=BF16, tm=tmb, tn=_pick(ov, _DIVS[1:]), tk=tk_d)
    dw_o = _mm_tn(o, dh_b, name=f"mla_dwo{tag}", out_dtype=BF16, tm=_pick(ov, _DIVS[1:]), tn=tn_d, tk=tmb)
    dq, dkv, dkr_h = _attn_bwd(q, kv, kr, o, lse, do, cos, sin, name=f"attn_bwd{tag}", n_heads=n_heads, tq=tq, n_real=n_real,
                               scale=scale)
    hw, kw = w_uq.shape[2], w_ukv.shape[2]
    dw_uq = _mm_tn(cq, dq, name=f"mla_dwuq{tag}", out_dtype=BF16, tm=lq, tn=hw, tk=tmb, out_block=hw)
    dcq = _mm_nt(dq, w_uq, name=f"mla_dcq{tag}", out_dtype=F32, tm=tmb, tn=lq, tk=hw, b_blocked=True)
    dw_ukv = _mm_tn(ckv, dkv, name=f"mla_dwukv{tag}", out_dtype=BF16, tm=lkv, tn=kw, tk=tmb, out_block=kw)
    dckv = _mm_nt(dkv, w_ukv, name=f"mla_dckv{tag}", out_dtype=F32, tm=tmb, tn=lkv, tk=kw, b_blocked=True)
    dproj, dqn, dkvn = _mla_prep_bwd(dcq, dckv, dkr_h, proj, qn, kvn, cos, sin, name=f"mla_prep_bwd{tag}", tm=tm, lq=lq, lkv=lkv)
    wc = w_in.shape[1]
    dw_in = _mm_tn(hn, dproj, name=f"mla_dwin{tag}", out_dtype=BF16, tm=tn_d, tn=wc, tk=tmb)
    dhn = _mm_nt(dproj, w_in, name=f"mla_dhn{tag}", out_dtype=F32, tm=tmb, tn=tn_d, tk=wc)
    dh, dh_b, dg = _rmsnorm_bwd(dhn, h_in, g_mix, dh, name=f"norm_mix_bwd{tag}", tm=tm)
    return dh, dh_b, dg, dqn, dkvn, [dw_in.reshape(N_DEV, -1, wc), dw_uq, dw_ukv, dw_o.reshape(N_DEV, -1, d)]


def _lru_layer_fwd(tag, h, g_mix, ws, small, *, tm):
    w_lin, w_lo = ws[0], _rows_natural(ws[1])
    t_all, d = h.shape
    dr = w_lo.shape[0]
    tmb = _pick(t_all, _ROW_TILES)
    hn = _rmsnorm_fwd(h, g_mix, name=f"norm_mix{tag}", tm=tm)
    xy = _mm_nn(hn, w_lin, name=f"lru_in{tag}", out_dtype=F32, tm=tmb, tn=w_lin.shape[2], tk=_pick(d, _DIVS), b_blocked=True)
    hs, hsy = _lru_fwd(xy, *small, name=f"lru_fwd{tag}")
    h_mid = _mm_nn(hsy, w_lo, name=f"lru_o{tag}", out_dtype=F32, tm=tmb, tn=_pick(d, _DIVS[2:]), tk=_pick(dr, _DIVS), res=h)
    return h_mid, (hn, xy, hs, hsy)


def _lru_layer_bwd(tag, dh, dh_b, h_in, saved, g_mix, ws, small, *, tm):
    hn, xy, hs, hsy = saved
    w_lin, w_lo = ws[0], _rows_natural(ws[1])
    t_all, d = h_in.shape
    dr = w_lo.shape[0]
    tmb = _pick(t_all, _ROW_TILES)
    tn_d, tk_d = _pick(d, _DIVS[1:]), _pick(d, _DIVS)
    dhsy = _mm_nt(dh_b, w_lo, name=f"lru_dhsy{tag}", out_dtype=F32, tm=tmb, tn=_pick(dr, _DIVS[1:]), tk=tk_d)
    dw_lo = _mm_tn(hsy, dh_b, name=f"lru_dwo{tag}", out_dtype=BF16, tm=_pick(dr, _DIVS[1:]), tn=tn_d, tk=tmb)
    dxy, *dsmall = _lru_bwd(xy, hs, dhsy, *small, name=f"lru_bwd{tag}")
    lw = w_lin.shape[2]
    dw_lin = _mm_tn(hn, dxy, name=f"lru_dwin{tag}", out_dtype=BF16, tm=tn_d, tn=lw, tk=tmb, out_block=lw)
    dhn = _mm_nt(dxy, w_lin, name=f"lru_dhn{tag}", out_dtype=F32, tm=tmb, tn=tn_d, tk=lw, b_blocked=True)
    dh, dh_b, dg = _rmsnorm_bwd(dhn, h_in, g_mix, dh, name=f"norm_mix_bwd{tag}", tm=tm)
    return dh, dh_b, dg, tuple(dsmall), [dw_lin, dw_lo.reshape(N_DEV, -1, d)]


def _ffn_layer_fwd(tag, h_mid, g_ffn, ws, *, tm):
    w_gu, w_down = ws[0], _rows_natural(ws[1])
    t_all, d = h_mid.shape
    f_all = w_down.shape[0]
    tmb = _pick(t_all, _ROW_TILES)
    fk = _pick(f_all, (1408,) + _DIVS[1:])
    hn2 = _rmsnorm_fwd(h_mid, g_ffn, name=f"norm_ffn{tag}", tm=tm)
    gu = _mm_nn(hn2, w_gu, name=f"ffn_gu{tag}", out_dtype=F32, tm=tmb, tn=w_gu.shape[2], tk=_pick(d, _DIVS), b_blocked=True)
    act = _swiglu_fwd(gu, name=f"swiglu_fwd{tag}", tm=tm, tc=fk)
    h_out = _mm_nn(act, w_down, name=f"ffn_down{tag}", out_dtype=F32, tm=tmb, tn=_pick(d, _DIVS[1:]), tk=fk, res=h_mid)
    return h_out, (hn2, gu, act)


def _ffn_layer_bwd(tag, dh, dh_b, h_mid, saved, g_ffn, ws, *, tm):
    hn2, gu, act = saved
    w_gu, w_down = ws[0], _rows_natural(ws[1])
    t_all, d = h_mid.shape
    f_all = w_down.shape[0]
    f_local = w_gu.shape[2]
    tmb = _pick(t_all, _ROW_TILES)
    fk = _pick(f_all, (1408,) + _DIVS[1:])
    tn_d, tk_d = _pick(d, _DIVS[1:]), _pick(d, _DIVS)
    dact = _mm_nt(dh_b, w_down, name=f"ffn_dact{tag}", out_dtype=F32, tm=tmb, tn=fk, tk=tk_d)
    dgu = _swiglu_bwd(dact, gu, name=f"swiglu_bwd{tag}", tm=tm, tc=fk)
    dw_down = _mm_tn(act, dh_b, name=f"ffn_dwdown{tag}", out_dtype=BF16, tm=fk, tn=tn_d, tk=tmb)
    dhn2 = _mm_nt(dgu, w_gu, name=f"ffn_dhn{tag}", out_dtype=F32, tm=tmb, tn=tn_d, tk=f_local, b_blocked=True)
    dw_gu = _mm_tn(hn2, dgu, name=f"ffn_dwgu{tag}", out_dtype=BF16, tm=tn_d, tn=f_local, tk=tmb, out_block=f_local)
    dh, dh_b, dg = _rmsnorm_bwd(dhn2, h_mid, g_ffn, dh, name=f"norm_ffn_bwd{tag}", tm=tm)
    return dh, dh_b, dg, [dw_gu, dw_down.reshape(N_DEV, -1, d)]


def kernel(x, meta_tokens, norm_mix, norm_ffn, norm_final, mla_w_in, mla_q_norm, mla_kv_norm, mla_w_uq, mla_w_ukv, mla_w_o, lru_w_in, lru_conv_w, lru_conv_b, lru_w_gate_a, lru_b_gate_a, lru_w_gate_x, lru_b_gate_x, lru_lambda, lru_w_o, ffn_w_gu, ffn_w_down, loss_target, m_meta_tokens, m_norm_mix, m_norm_ffn, m_norm_final, m_mla_w_in, m_mla_q_norm, m_mla_kv_norm, m_mla_w_uq, m_mla_w_ukv, m_mla_w_o, m_lru_w_in, m_lru_conv_w, m_lru_conv_b, m_lru_w_gate_a, m_lru_b_gate_a, m_lru_w_gate_x, m_lru_b_gate_x, m_lru_lambda, m_lru_w_o, m_ffn_w_gu, m_ffn_w_down, v_meta_tokens, v_norm_mix, v_norm_ffn, v_norm_final, v_mla_w_in, v_mla_q_norm, v_mla_kv_norm, v_mla_w_uq, v_mla_w_ukv, v_mla_w_o, v_lru_w_in, v_lru_conv_w, v_lru_conv_b, v_lru_w_gate_a, v_lru_b_gate_a, v_lru_w_gate_x, v_lru_b_gate_x, v_lru_lambda, v_lru_w_o, v_ffn_w_gu, v_ffn_w_down):
    seq, d = x.shape[1], x.shape[2]
    assert seq % CHUNK == 0
    n_real = N_META + seq
    t_all = -(-n_real // LANES) * LANES
    tm = _pick(t_all, (384, 256, 128))
    tq = _pick(seq, (512, 256, 128, 64))
    depth = norm_mix.shape[0]
    n_mla, n_lru = mla_w_in.shape[0], lru_w_in.shape[0]
    lq, lkv = mla_q_norm.shape[1], mla_kv_norm.shape[1]
    w_in_cols = lq + lkv + LANES
    heads_local = mla_w_uq.shape[2] // (QK_NOPE + QK_ROPE)
    n_heads = heads_local * N_DEV
    dr = lru_w_gate_a.shape[1] * lru_w_gate_a.shape[2]
    scale = (QK_NOPE + QK_ROPE) ** -0.5
    cx, cy, cc = _mesh_pos()
    core = jnp.reshape(cc, (1,)).astype(jnp.int32)
    my_slot = jnp.reshape(4 * cx + 2 * cy + cc, (1,)).astype(jnp.int32)

    def pad_cols(w, cols):
        return jnp.pad(w, ((0, 0), (0, cols - w.shape[1])))

    def pad_heads(w):
        k_all = w.shape[0]
        w3 = w.reshape(k_all, heads_local, QK_NOPE + QK_ROPE)
        return jnp.pad(w3, ((0, 0), (0, 0), (0, HEAD_W - QK_NOPE - QK_ROPE))).reshape(k_all, heads_local * HEAD_W)

    def unpad_heads(w):
        k_all = w.shape[0]
        return w.reshape(k_all, heads_local, HEAD_W)[:, :, :QK_NOPE + QK_ROPE].reshape(k_all, -1)

    small_rows = N_META + n_lru * 4 + 2 * n_lru
    small_pad = -(-small_rows // SUBLANES) * SUBLANES

    def pack_small(meta, conv_w, conv_b, lam):
        rows = jnp.concatenate([meta, conv_w.reshape(n_lru * 4, -1), conv_b, lam], axis=0)
        return jnp.pad(rows, ((0, small_pad - small_rows), (0, 0)))

    def unpack_small(p):
        o1 = N_META + n_lru * 4
        return (p[:N_META], p[N_META:o1].reshape(n_lru, 4, -1), p[o1:o1 + n_lru], p[o1 + n_lru:o1 + 2 * n_lru])

    mla_shards, lru_shards, ffn_shards = [], [], []
    for j in range(n_mla):
        mla_shards.append([pad_cols(mla_w_in[j], w_in_cols).astype(BF16), pad_heads(mla_w_uq[j]).astype(BF16),
                           mla_w_ukv[j].astype(BF16), mla_w_o[j].astype(BF16)])
    for j in range(n_lru):
        lru_shards.append([lru_w_in[j].astype(BF16), lru_w_o[j].astype(BF16)])
    for layer in range(depth):
        ffn_shards.append([ffn_w_gu[layer].astype(BF16), ffn_w_down[layer].astype(BF16)])

    small_full = _all_gather([pack_small(meta_tokens, lru_conv_w, lru_conv_b, lru_lambda)], name="ag_small")[0]
    small_full = jnp.transpose(small_full, (1, 0, 2)).reshape(small_pad, -1)
    meta_full, conv_w_full, conv_b_full, lam_full = unpack_small(small_full)

    gathered = []
    for layer in range(depth):
        j = layer // 2
        shards = (mla_shards[j] if layer % 2 == 0 else lru_shards[j]) + ffn_shards[layer]
        gathered.append(_all_gather(shards, name=f"ag_layer{layer}"))

    cos, sin = _rope_tables(t_all)
    zeros_tail = jnp.zeros((t_all - n_real, d), F32)
    h = jnp.concatenate([meta_full, x[0], zeros_tail], axis=0)
    target = jnp.concatenate([jnp.zeros((N_META, d), F32), loss_target[0], zeros_tail], axis=0)

    attn_kw = dict(tm=tm, tq=tq, n_heads=n_heads, scale=scale, n_real=n_real)

    def lru_small(j):
        return (conv_w_full[j], conv_b_full[j][None, :], lru_w_gate_a[j].astype(BF16), lru_b_gate_a[j].reshape(1, dr),
                lru_w_gate_x[j].astype(BF16), lru_b_gate_x[j].reshape(1, dr), lam_full[j][None, :])

    saved = []
    for layer in range(depth):
        j = layer // 2
        g_mix = norm_mix[layer][None, :]
        g_ffn = norm_ffn[layer][None, :]
        ws = gathered[layer]
        if layer % 2 == 0:
            h_mid, mix_saved = _mla_layer_fwd(layer, h, g_mix, ws[:4], mla_q_norm[j][None, :], mla_kv_norm[j][None, :], cos, sin,
                                              **attn_kw)
        else:
            h_mid, mix_saved = _lru_layer_fwd(layer, h, g_mix, ws[:2], lru_small(j), tm=tm)
        h_out, ffn_saved = _ffn_layer_fwd(layer, h_mid, g_ffn, ws[-2:], tm=tm)
        saved.append((h, h_mid, mix_saved, ffn_saved))
        h = h_out

    loss_part, dh, dh_b, dg_final = _loss_head(h, target, norm_final[None, :], name="loss_head", tm=tm, n_real=n_real)
    loss = lax.psum(loss_part[0, 0], ("x", "y", "c"))

    big_grads = [None] * depth
    d_norm_mix, d_norm_ffn = [None] * depth, [None] * depth
    d_qn, d_kvn = [None] * n_mla, [None] * n_mla
    d_small = {k: [None] * n_lru for k in ("cw", "cb", "wga", "bga", "wgx", "bgx", "lam")}
    for layer in reversed(range(depth)):
        j = layer // 2
        h_in, h_mid, mix_saved, ffn_saved = saved[layer]
        ws = gathered[layer]
        dh, dh_b, d_norm_ffn[layer], ffn_g = _ffn_layer_bwd(layer, dh, dh_b, h_mid, ffn_saved, norm_ffn[layer][None, :], ws[-2:], tm=tm)
        g_mix = norm_mix[layer][None, :]
        if layer % 2 == 0:
            dh, dh_b, d_norm_mix[layer], d_qn[j], d_kvn[j], mix_g = _mla_layer_bwd(
                layer, dh, dh_b, h_in, mix_saved, g_mix, ws[:4], mla_q_norm[j][None, :], mla_kv_norm[j][None, :], cos, sin, **attn_kw)
        else:
            dh, dh_b, d_norm_mix[layer], dsmall, mix_g = _lru_layer_bwd(layer, dh, dh_b, h_in, mix_saved, g_mix, ws[:2],
                                                                        lru_small(j), tm=tm)
            for key, val in zip(("cw", "cb", "wga", "bga", "wgx", "bgx", "lam"), dsmall):
                d_small[key][j] = val
        big_grads[layer] = mix_g + ffn_g

    grad_x = dh[N_META:n_real][None]

    reduced = []
    for layer in range(depth):
        landed = _rs_sibling(big_grads[layer], name=f"rs_sibling{layer}")
        parts = [_pair_add(g, l, core, name=f"rs_add{layer}_{a}", tr=_adam_row_tile(g.shape[1], g.shape[2]))
                 for a, (g, l) in enumerate(zip(big_grads[layer], landed))]
        reduced.append(_rs_chips(parts, name=f"rs_chips{layer}"))

    own_first = (3, 0, 1, 2)

    def adam_sharded(terms, w, m, v, tag):
        r_all, c_all = terms.shape[1], terms.shape[2]
        return _adamw(w.reshape(r_all, c_all), m.reshape(r_all, c_all), v.reshape(r_all, c_all), terms, own_first,
                      name=f"adamw_{tag}", tr=_adam_row_tile(r_all, c_all))

    def per_layer(fn, n):
        outs = [fn(i) for i in range(n)]
        return [jnp.stack([o[k] for o in outs], axis=0) for k in range(4)]

    res = {}
    res["mla_w_in"] = per_layer(lambda j: [t[:, :lq + lkv + QK_ROPE] for t in adam_sharded(
        reduced[2 * j][0], pad_cols(mla_w_in[j], w_in_cols), pad_cols(m_mla_w_in[j], w_in_cols),
        pad_cols(v_mla_w_in[j], w_in_cols), f"mla_w_in{j}")], n_mla)
    res["mla_w_uq"] = per_layer(lambda j: [unpad_heads(t) for t in adam_sharded(
        reduced[2 * j][1], pad_heads(mla_w_uq[j]), pad_heads(m_mla_w_uq[j]), pad_heads(v_mla_w_uq[j]), f"mla_w_uq{j}")], n_mla)
    res["mla_w_ukv"] = per_layer(lambda j: adam_sharded(reduced[2 * j][2], mla_w_ukv[j], m_mla_w_ukv[j], v_mla_w_ukv[j],
                                                         f"mla_w_ukv{j}"), n_mla)
    res["mla_w_o"] = per_layer(lambda j: adam_sharded(reduced[2 * j][3], mla_w_o[j], m_mla_w_o[j], v_mla_w_o[j],
                                                       f"mla_w_o{j}"), n_mla)
    res["lru_w_in"] = per_layer(lambda j: adam_sharded(reduced[2 * j + 1][0], lru_w_in[j], m_lru_w_in[j], v_lru_w_in[j],
                                                        f"lru_w_in{j}"), n_lru)
    res["lru_w_o"] = per_layer(lambda j: adam_sharded(reduced[2 * j + 1][1], lru_w_o[j], m_lru_w_o[j], v_lru_w_o[j],
                                                       f"lru_w_o{j}"), n_lru)
    res["ffn_w_gu"] = per_layer(lambda l: adam_sharded(reduced[l][-2], ffn_w_gu[l], m_ffn_w_gu[l], v_ffn_w_gu[l],
                                                        f"ffn_w_gu{l}"), depth)
    res["ffn_w_down"] = per_layer(lambda l: adam_sharded(reduced[l][-1], ffn_w_down[l], m_ffn_w_down[l], v_ffn_w_down[l],
                                                          f"ffn_w_down{l}"), depth)

    d_meta = dh[:N_META]
    small_grad = pack_small(d_meta, jnp.stack(d_small["cw"], axis=0), jnp.concatenate(d_small["cb"], axis=0),
                            jnp.concatenate(d_small["lam"], axis=0))
    rep_grads = [
        jnp.concatenate(d_norm_mix, axis=0), jnp.concatenate(d_norm_ffn, axis=0), dg_final,
        jnp.concatenate(d_qn, axis=0), jnp.concatenate(d_kvn, axis=0),
        jnp.stack(d_small["wga"], axis=0).reshape(-1, LANES), jnp.concatenate(d_small["bga"], axis=0),
        jnp.stack(d_small["wgx"], axis=0).reshape(-1, LANES), jnp.concatenate(d_small["bgx"], axis=0),
    ]
    rep_grads = [jnp.pad(g, ((0, -g.shape[0] % SUBLANES), (0, 0))) for g in rep_grads]
    all_small = _all_gather([small_grad] + rep_grads, name="ag_small_grads")
    slot_order = tuple(range(N_DEV))

    def adam_rep(terms, w, m, v, tag):
        r_pad, c_all = terms.shape[1], terms.shape[2]

        def prep(t):
            t2 = t.reshape(-1, c_all)
            return jnp.pad(t2, ((0, r_pad - t2.shape[0]), (0, 0)))

        outs = _adamw(prep(w), prep(m), prep(v), terms, slot_order, name=f"adamw_{tag}", tr=_adam_row_tile(r_pad, c_all))
        n_rows = w.size // c_all
        return [o[:n_rows].reshape(w.shape) for o in outs]

    small_w = pack_small(meta_tokens, lru_conv_w, lru_conv_b, lru_lambda)
    small_m = pack_small(m_meta_tokens, m_lru_conv_w, m_lru_conv_b, m_lru_lambda)
    small_v = pack_small(v_meta_tokens, v_lru_conv_w, v_lru_conv_b, v_lru_lambda)
    small_out = _adamw(small_w, small_m, small_v, all_small[0], slot_order, name="adamw_small", tr=small_pad, col_block=my_slot)
    small_out = [unpack_small(o) for o in small_out]
    for idx, key in enumerate(("meta_tokens", "lru_conv_w", "lru_conv_b", "lru_lambda")):
        res[key] = [small_out[k][idx] for k in range(4)]

    res["norm_mix"] = adam_rep(all_small[1], norm_mix, m_norm_mix, v_norm_mix, "norm_mix")
    res["norm_ffn"] = adam_rep(all_small[2], norm_ffn, m_norm_ffn, v_norm_ffn, "norm_ffn")
    res["norm_final"] = adam_rep(all_small[3], norm_final, m_norm_final, v_norm_final, "norm_final")
    res["mla_q_norm"] = adam_rep(all_small[4], mla_q_norm, m_mla_q_norm, v_mla_q_norm, "mla_q_norm")
    res["mla_kv_norm"] = adam_rep(all_small[5], mla_kv_norm, m_mla_kv_norm, v_mla_kv_norm, "mla_kv_norm")
    res["lru_w_gate_a"] = adam_rep(all_small[6], lru_w_gate_a, m_lru_w_gate_a, v_lru_w_gate_a, "lru_w_gate_a")
    res["lru_b_gate_a"] = adam_rep(all_small[7], lru_b_gate_a, m_lru_b_gate_a, v_lru_b_gate_a, "lru_b_gate_a")
    res["lru_w_gate_x"] = adam_rep(all_small[8], lru_w_gate_x, m_lru_w_gate_x, v_lru_w_gate_x, "lru_w_gate_x")
    res["lru_b_gate_x"] = adam_rep(all_small[9], lru_b_gate_x, m_lru_b_gate_x, v_lru_b_gate_x, "lru_b_gate_x")

    names = ["meta_tokens", "norm_mix", "norm_ffn", "norm_final", "mla_w_in", "mla_q_norm", "mla_kv_norm", "mla_w_uq",
             "mla_w_ukv", "mla_w_o", "lru_w_in", "lru_conv_w", "lru_conv_b", "lru_w_gate_a", "lru_b_gate_a", "lru_w_gate_x",
             "lru_b_gate_x", "lru_lambda", "lru_w_o", "ffn_w_gu", "ffn_w_down"]
    shapes = dict(meta_tokens=meta_tokens, norm_mix=norm_mix, norm_ffn=norm_ffn, norm_final=norm_final, mla_w_in=mla_w_in,
                  mla_q_norm=mla_q_norm, mla_kv_norm=mla_kv_norm, mla_w_uq=mla_w_uq, mla_w_ukv=mla_w_ukv, mla_w_o=mla_w_o,
                  lru_w_in=lru_w_in, lru_conv_w=lru_conv_w, lru_conv_b=lru_conv_b, lru_w_gate_a=lru_w_gate_a,
                  lru_b_gate_a=lru_b_gate_a, lru_w_gate_x=lru_w_gate_x, lru_b_gate_x=lru_b_gate_x, lru_lambda=lru_lambda,
                  lru_w_o=lru_w_o, ffn_w_gu=ffn_w_gu, ffn_w_down=ffn_w_down)
    outs = [loss, grad_x]
    for k in range(4):
        outs += [res[nm][k].reshape(shapes[nm].shape) for nm in names]
    return tuple(outs)
```

```python
import math

import jax
import jax.numpy as jnp
from jax import lax
from jax.experimental import pallas as pl
from jax.experimental.pallas import tpu as pltpu

F32 = jnp.float32
BF16 = jnp.bfloat16
MESH = pl.DeviceIdType.MESH

N_META = 16
CHUNK = 64
QK_NOPE = 128
QK_ROPE = 64
V_HEAD = 128
HEAD_W = 256
ROPE_THETA = 10000.0
LRU_C = 8.0
RMS_EPS = 1e-6
NEG_BIG = -1e30
ADAM_LR, ADAM_B1, ADAM_B2, ADAM_EPS, ADAM_WD, ADAM_STEP = 0.001, 0.9, 0.999, 1e-08, 0.01, 10

LANES = 128
SUBLANES = 8
VMEM_LIMIT_BYTES = 52 * 1024 * 1024
N_DEV = 8

_NT = (((1,), (1,)), ((), ()))
_TN = (((0,), (0,)), ((), ()))
_DIVS = (2048, 1024, 512, 256, 128)
_ROW_TILES = (1408, 1024, 512, 256, 128)


def _params(dims):
    return pltpu.CompilerParams(dimension_semantics=dims, vmem_limit_bytes=VMEM_LIMIT_BYTES)


def _pick(n, candidates):
    for c in candidates:
        if c <= n and n % c == 0:
            return c
    return n


def _sigmoid(z):
    return 1.0 / (1.0 + jnp.exp(-z))


def _gelu(x):
    c = math.sqrt(2.0 / math.pi)
    return 0.5 * x * (1.0 + jnp.tanh(c * (x + 0.044715 * x * x * x)))


def _gelu_grad(x):
    c = math.sqrt(2.0 / math.pi)
    th = jnp.tanh(c * (x + 0.044715 * x * x * x))
    return 0.5 * (1.0 + th) + 0.5 * x * (1.0 - th * th) * c * (1.0 + 3.0 * 0.044715 * x * x)


def _neg_expm1(x):
    poly = -x * (1.0 + x * (1.0 / 2.0) * (1.0 + x * (1.0 / 3.0) * (1.0 + x * (1.0 / 4.0) * (
        1.0 + x * (1.0 / 5.0) * (1.0 + x * (1.0 / 6.0) * (1.0 + x * (1.0 / 7.0)))))))
    return jnp.where(x > -0.25, poly, 1.0 - jnp.exp(x))


def _softplus_neg(lam):
    e = jnp.exp(-jnp.abs(lam))
    log1p = jnp.where(e > 1e-4, jnp.log(1.0 + e), e * (1.0 - e * (0.5 - e * (1.0 / 3.0))))
    return jnp.maximum(-lam, 0.0) + log1p


def _rot_half(x):
    lane = lax.broadcasted_iota(jnp.int32, x.shape, 1)
    first = (lane % QK_ROPE) < (QK_ROPE // 2)
    return jnp.where(first, -pltpu.roll(x, LANES - QK_ROPE // 2, 1), pltpu.roll(x, QK_ROPE // 2, 1))


def _rope(x, cos, sin):
    return x * cos + _rot_half(x) * sin


def _unrope(g, cos, sin):
    return g * cos - _rot_half(g) * sin


def _mm_nn(a, b, *, name, out_dtype, tm, tn, tk, b_blocked=False, res=None, epilogue=None, extras=()):
    m_all, k_all = a.shape
    if b_blocked:
        g_all, kb, nb = b.shape
        n_all = g_all * nb
        assert nb % tn == 0
        r = nb // tn
        b_spec = pl.BlockSpec((None, tk, tn), lambda j, i, k: (j // r, k, j % r))
    else:
        kb, n_all = b.shape
        b_spec = pl.BlockSpec((tk, tn), lambda j, i, k: (k, j))
    assert kb == k_all and m_all % tm == 0 and n_all % tn == 0 and k_all % tk == 0
    nm, nn, nk = m_all // tm, n_all // tn, k_all // tk
    in_specs = [pl.BlockSpec((tm, tk), lambda j, i, k: (i, k)), b_spec]
    operands = [a, b]
    has_res = res is not None
    if has_res:
        in_specs.append(pl.BlockSpec((tm, tn), lambda j, i, k: (i, j)))
        operands.append(res)
    for e in extras:
        in_specs.append(pl.BlockSpec((tm, e.shape[1]), lambda j, i, k: (i, 0)))
        operands.append(e)
    n_ex = len(extras)

    def body(*refs):
        a_ref, b_ref = refs[0], refs[1]
        pos = 2
        res_ref = None
        if has_res:
            res_ref = refs[pos]
            pos += 1
        ex_refs = refs[pos:pos + n_ex]
        pos += n_ex
        o_ref = refs[pos]
        acc_ref = refs[pos + 1] if nk > 1 else None

        def finish(acc):
            if has_res:
                acc = acc + res_ref[...]
            if epilogue is not None:
                acc = epilogue(acc, *ex_refs)
            o_ref[...] = acc.astype(o_ref.dtype)

        prod = jnp.dot(a_ref[...], b_ref[...], preferred_element_type=F32)
        if nk == 1:
            finish(prod)
        else:
            k = pl.program_id(2)

            @pl.when(k == 0)
            def _():
                acc_ref[...] = prod

            @pl.when(k > 0)
            def _():
                acc_ref[...] += prod

            @pl.when(k == nk - 1)
            def _():
                finish(acc_ref[...])

    return pl.pallas_call(
        body, name=name, grid=(nn, nm, nk), in_specs=in_specs,
        out_specs=pl.BlockSpec((tm, tn), lambda j, i, k: (i, j)),
        out_shape=jax.ShapeDtypeStruct((m_all, n_all), out_dtype),
        scratch_shapes=[pltpu.VMEM((tm, tn), F32)] if nk > 1 else [],
        compiler_params=_params(("parallel", "parallel", "arbitrary")),
    )(*operands)


def _mm_nt(a, b, *, name, out_dtype, tm, tn, tk, b_blocked=False):
    m_all, k_all = a.shape
    if b_blocked:
        g_all, n_all, nb = b.shape
        assert g_all * nb == k_all and nb % tk == 0
        r = nb // tk
        b_spec = pl.BlockSpec((None, tn, tk), lambda j, i, k: (k // r, j, k % r))
    else:
        n_all, kb = b.shape
        assert kb == k_all
        b_spec = pl.BlockSpec((tn, tk), lambda j, i, k: (j, k))
    assert m_all % tm == 0 and n_all % tn == 0 and k_all % tk == 0
    nm, nn, nk = m_all // tm, n_all // tn, k_all // tk

    def body(a_ref, b_ref, o_ref, *scratch):
        prod = lax.dot_general(a_ref[...], b_ref[...], _NT, preferred_element_type=F32)
        if nk == 1:
            o_ref[...] = prod.astype(o_ref.dtype)
        else:
            acc_ref = scratch[0]
            k = pl.program_id(2)

            @pl.when(k == 0)
            def _():
                acc_ref[...] = prod

            @pl.when(k > 0)
            def _():
                acc_ref[...] += prod

            @pl.when(k == nk - 1)
            def _():
                o_ref[...] = acc_ref[...].astype(o_ref.dtype)

    return pl.pallas_call(
        body, name=name, grid=(nn, nm, nk),
        in_specs=[pl.BlockSpec((tm, tk), lambda j, i, k: (i, k)), b_spec],
        out_specs=pl.BlockSpec((tm, tn), lambda j, i, k: (i, j)),
        out_shape=jax.ShapeDtypeStruct((m_all, n_all), out_dtype),
        scratch_shapes=[pltpu.VMEM((tm, tn), F32)] if nk > 1 else [],
        compiler_params=_params(("parallel", "parallel", "arbitrary")),
    )(a, b)


def _mm_tn(a, b, *, name, out_dtype, tm, tn, tk, out_block=None):
    t_all, m_all = a.shape
    tb, n_all = b.shape
    assert tb == t_all and m_all % tm == 0 and n_all % tn == 0 and t_all % tk == 0
    nm, nn, nk = m_all // tm, n_all // tn, t_all // tk
    if out_block is None:
        out_shape = jax.ShapeDtypeStruct((m_all, n_all), out_dtype)
        out_spec = pl.BlockSpec((tm, tn), lambda i, j, k: (i, j))
    else:
        assert out_block % tn == 0 and n_all % out_block == 0
        r = out_block // tn
        out_shape = jax.ShapeDtypeStruct((n_all // out_block, m_all, out_block), out_dtype)
        out_spec = pl.BlockSpec((None, tm, tn), lambda i, j, k: (j // r, i, j % r))

    def body(a_ref, b_ref, o_ref, *scratch):
        prod = lax.dot_general(a_ref[...], b_ref[...], _TN, preferred_element_type=F32)
        if nk == 1:
            o_ref[...] = prod.astype(o_ref.dtype)
        else:
            acc_ref = scratch[0]
            k = pl.program_id(2)

            @pl.when(k == 0)
            def _():
                acc_ref[...] = prod

            @pl.when(k > 0)
            def _():
                acc_ref[...] += prod

            @pl.when(k == nk - 1)
            def _():
                o_ref[...] = acc_ref[...].astype(o_ref.dtype)

    return pl.pallas_call(
        body, name=name, grid=(nm, nn, nk),
        in_specs=[pl.BlockSpec((tk, tm), lambda i, j, k: (k, i)), pl.BlockSpec((tk, tn), lambda i, j, k: (k, j))],
        out_specs=out_spec, out_shape=out_shape,
        scratch_shapes=[pltpu.VMEM((tm, tn), F32)] if nk > 1 else [],
        compiler_params=_params(("parallel", "parallel", "arbitrary")),
    )(a, b)


def _rmsnorm_fwd(x, g, *, name, tm):
    t_all, d = x.shape

    def body(x_ref, g_ref, o_ref):
        xv = x_ref[...]
        rstd = lax.rsqrt(jnp.mean(xv * xv, axis=-1, keepdims=True) + RMS_EPS)
        o_ref[...] = (xv * rstd * g_ref[...]).astype(o_ref.dtype)

    return pl.pallas_call(
        body, name=name, grid=(t_all // tm,),
        in_specs=[pl.BlockSpec((tm, d), lambda i: (i, 0)), pl.BlockSpec((1, d), lambda i: (0, 0))],
        out_specs=pl.BlockSpec((tm, d), lambda i: (i, 0)),
        out_shape=jax.ShapeDtypeStruct((t_all, d), BF16),
        compiler_params=_params(("parallel",)),
    )(x, g)


def _rms_bwd_math(dy, xv, g):
    rstd = lax.rsqrt(jnp.mean(xv * xv, axis=-1, keepdims=True) + RMS_EPS)
    xhat = xv * rstd
    dxh = dy * g
    dx = rstd * (dxh - xhat * jnp.mean(dxh * xhat, axis=-1, keepdims=True))
    return dx, jnp.sum(dy * xhat, axis=0, keepdims=True)


def _rmsnorm_bwd(dy, x, g, res, *, name, tm):
    t_all, d = x.shape

    def body(dy_ref, x_ref, g_ref, res_ref, dx_ref, dxb_ref, dg_ref):
        dx, dg = _rms_bwd_math(dy_ref[...], x_ref[...], g_ref[...])
        tot = res_ref[...] + dx
        dx_ref[...] = tot
        dxb_ref[...] = tot.astype(BF16)

        @pl.when(pl.program_id(0) == 0)
        def _():
            dg_ref[...] = dg

        @pl.when(pl.program_id(0) > 0)
        def _():
            dg_ref[...] += dg

    row = pl.BlockSpec((tm, d), lambda i: (i, 0))
    vec = pl.BlockSpec((1, d), lambda i: (0, 0))
    return pl.pallas_call(
        body, name=name, grid=(t_all // tm,),
        in_specs=[row, row, vec, row], out_specs=[row, row, vec],
        out_shape=[jax.ShapeDtypeStruct((t_all, d), F32), jax.ShapeDtypeStruct((t_all, d), BF16),
                   jax.ShapeDtypeStruct((1, d), F32)],
        compiler_params=_params(("arbitrary",)),
    )(dy, x, g, res)


def _loss_head(h, target, g, *, name, tm, n_real):
    t_all, d = h.shape

    def body(h_ref, t_ref, g_ref, loss_ref, dx_ref, dxb_ref, dg_ref):
        i = pl.program_id(0)
        xv = h_ref[...]
        gv = g_ref[...]
        rstd = lax.rsqrt(jnp.mean(xv * xv, axis=-1, keepdims=True) + RMS_EPS)
        y = xv * rstd * gv
        row = i * tm + lax.broadcasted_iota(jnp.int32, (tm, 1), 0)
        valid = (row >= N_META) & (row < n_real)
        err = jnp.where(valid, y - t_ref[...], 0.0)
        part = 0.5 * jnp.sum(jnp.mean(err * err, axis=-1, keepdims=True), axis=0, keepdims=True)
        dx, dg = _rms_bwd_math(err * (1.0 / d), xv, gv)
        dx_ref[...] = dx
        dxb_ref[...] = dx.astype(BF16)

        @pl.when(i == 0)
        def _():
            dg_ref[...] = dg
            loss_ref[...] = jnp.broadcast_to(part, loss_ref.shape)

        @pl.when(i > 0)
        def _():
            dg_ref[...] += dg
            loss_ref[...] += jnp.broadcast_to(part, loss_ref.shape)

    row = pl.BlockSpec((tm, d), lambda i: (i, 0))
    vec = pl.BlockSpec((1, d), lambda i: (0, 0))
    return pl.pallas_call(
        body, name=name, grid=(t_all // tm,),
        in_specs=[row, row, vec],
        out_specs=[pl.BlockSpec((1, LANES), lambda i: (0, 0)), row, row, vec],
        out_shape=[jax.ShapeDtypeStruct((1, LANES), F32), jax.ShapeDtypeStruct((t_all, d), F32),
                   jax.ShapeDtypeStruct((t_all, d), BF16), jax.ShapeDtypeStruct((1, d), F32)],
        compiler_params=_params(("arbitrary",)),
    )(h, target, g)


def _swiglu_fwd(gu, *, name, tm, tc):
    t_all, f2 = gu.shape
    f = f2 // 2
    nc = f // tc

    def body(g_ref, u_ref, o_ref):
        gv = g_ref[...]
        o_ref[...] = (gv * _sigmoid(gv) * u_ref[...]).astype(o_ref.dtype)

    return pl.pallas_call(
        body, name=name, grid=(t_all // tm, nc),
        in_specs=[pl.BlockSpec((tm, tc), lambda i, j: (i, j)), pl.BlockSpec((tm, tc), lambda i, j: (i, nc + j))],
        out_specs=pl.BlockSpec((tm, tc), lambda i, j: (i, j)),
        out_shape=jax.ShapeDtypeStruct((t_all, f), BF16),
        compiler_params=_params(("parallel", "parallel")),
    )(gu, gu)


def _swiglu_bwd(dact, gu, *, name, tm, tc):
    t_all, f2 = gu.shape
    f = f2 // 2
    nc = f // tc

    def body(d_ref, g_ref, u_ref, o_ref):
        jj = pl.program_id(1)
        gv = g_ref[...]
        sg = _sigmoid(gv)
        dv = d_ref[...]

        @pl.when(jj < nc)
        def _():
            o_ref[...] = (dv * u_ref[...] * (sg * (1.0 + gv * (1.0 - sg)))).astype(o_ref.dtype)

        @pl.when(jj >= nc)
        def _():
            o_ref[...] = (dv * gv * sg).astype(o_ref.dtype)

    return pl.pallas_call(
        body, name=name, grid=(t_all // tm, 2 * nc),
        in_specs=[pl.BlockSpec((tm, tc), lambda i, j: (i, j % nc)), pl.BlockSpec((tm, tc), lambda i, j: (i, j % nc)),
                  pl.BlockSpec((tm, tc), lambda i, j: (i, nc + j % nc))],
        out_specs=pl.BlockSpec((tm, tc), lambda i, j: (i, j)),
        out_shape=jax.ShapeDtypeStruct((t_all, f2), BF16),
        compiler_params=_params(("parallel", "parallel")),
    )(dact, gu, gu)


def _mla_prep_fwd(proj, qn, kvn, cos, sin, *, name, tm, lq, lkv):
    t_all, w = proj.shape

    def body(p_ref, qn_ref, kvn_ref, cos_ref, sin_ref, cq_ref, ckv_ref, kr_ref):
        pv = p_ref[...]
        xq = pv[:, :lq]
        xkv = pv[:, lq:lq + lkv]
        cq_ref[...] = (xq * lax.rsqrt(jnp.mean(xq * xq, axis=-1, keepdims=True) + RMS_EPS) * qn_ref[...]).astype(BF16)
        ckv_ref[...] = (xkv * lax.rsqrt(jnp.mean(xkv * xkv, axis=-1, keepdims=True) + RMS_EPS) * kvn_ref[...]).astype(BF16)
        kr_ref[...] = _rope(pv[:, lq + lkv:], cos_ref[...], sin_ref[...]).astype(BF16)

    def row(width):
        return pl.BlockSpec((tm, width), lambda i: (i, 0))

    def vec(width):
        return pl.BlockSpec((1, width), lambda i: (0, 0))

    return pl.pallas_call(
        body, name=name, grid=(t_all // tm,),
        in_specs=[row(w), vec(lq), vec(lkv), row(LANES), row(LANES)],
        out_specs=[row(lq), row(lkv), row(LANES)],
        out_shape=[jax.ShapeDtypeStruct((t_all, lq), BF16), jax.ShapeDtypeStruct((t_all, lkv), BF16),
                   jax.ShapeDtypeStruct((t_all, LANES), BF16)],
        compiler_params=_params(("parallel",)),
    )(proj, qn, kvn, cos, sin)


def _mla_prep_bwd(dcq, dckv, dkr_h, proj, qn, kvn, cos, sin, *, name, tm, lq, lkv):
    t_all, w = proj.shape
    n_heads = dkr_h.shape[0]

    def body(dcq_ref, dckv_ref, dkr_ref, p_ref, qn_ref, kvn_ref, cos_ref, sin_ref, dp_ref, dqn_ref, dkvn_ref):
        pv = p_ref[...]
        dxq, dqn = _rms_bwd_math(dcq_ref[...], pv[:, :lq], qn_ref[...])
        dxkv, dkvn = _rms_bwd_math(dckv_ref[...], pv[:, lq:lq + lkv], kvn_ref[...])
        dkr = dkr_ref[0]
        for hh in range(1, n_heads):
            dkr = dkr + dkr_ref[hh]
        dkr = _unrope(dkr, cos_ref[...], sin_ref[...])
        dp_ref[...] = jnp.concatenate([dxq, dxkv, dkr], axis=1).astype(BF16)

        @pl.when(pl.program_id(0) == 0)
        def _():
            dqn_ref[...] = dqn
            dkvn_ref[...] = dkvn

        @pl.when(pl.program_id(0) > 0)
        def _():
            dqn_ref[...] += dqn
            dkvn_ref[...] += dkvn

    def row(width):
        return pl.BlockSpec((tm, width), lambda i: (i, 0))

    def vec(width):
        return pl.BlockSpec((1, width), lambda i: (0, 0))

    return pl.pallas_call(
        body, name=name, grid=(t_all // tm,),
        in_specs=[row(lq), row(lkv), pl.BlockSpec((n_heads, tm, LANES), lambda i: (0, i, 0)), row(w),
                  vec(lq), vec(lkv), row(LANES), row(LANES)],
        out_specs=[row(w), vec(lq), vec(lkv)],
        out_shape=[jax.ShapeDtypeStruct((t_all, w), BF16), jax.ShapeDtypeStruct((1, lq), F32),
                   jax.ShapeDtypeStruct((1, lkv), F32)],
        compiler_params=_params(("arbitrary",)),
    )(dcq, dckv, dkr_h, proj, qn, kvn, cos, sin)


def _rope_q_epilogue(acc, cos_ref, sin_ref):
    parts = []
    for g in range(acc.shape[1] // LANES):
        blk = acc[:, g * LANES:(g + 1) * LANES]
        parts.append(_rope(blk, cos_ref[...], sin_ref[...]) if g % 2 == 1 else blk)
    return jnp.concatenate(parts, axis=1)


def _chunk_causal(rows, cols):
    r = lax.broadcasted_iota(jnp.int32, (rows, cols), 0)
    c = lax.broadcasted_iota(jnp.int32, (rows, cols), 1)
    return (c >> 6) <= (r >> 6)


def _meta_keys(rows, cols):
    return lax.broadcasted_iota(jnp.int32, (rows, cols), 1) < N_META


def _attn_fwd(q, kv, kr, *, name, n_heads, tq, n_real, scale):
    t_all = q.shape[0]
    nq = (n_real - N_META) // tq
    assert N_META + nq * tq == n_real and tq % CHUNK == 0 and t_all >= LANES
    n_pad = t_all - n_real

    def body(q_ref, kv_ref, kr_ref, o_ref, lse_ref, k_scr, m_scr, l_scr, acc_scr):
        k_scr[:, :QK_NOPE] = kv_ref[:, :QK_NOPE]
        k_scr[:, QK_NOPE:] = kr_ref[...]
        if n_pad:
            o_ref[pl.ds(n_real, n_pad), :] = jnp.zeros((n_pad, V_HEAD), o_ref.dtype)
            lse_ref[pl.ds(n_real, n_pad), :] = jnp.zeros((n_pad, LANES), F32)

        def scores(qt, c0, width):
            return lax.dot_general(qt, k_scr[pl.ds(c0, width), :], _NT, preferred_element_type=F32) * scale

        def values(c0, width):
            return kv_ref[pl.ds(c0, width), QK_NOPE:]

        s = jnp.where(_meta_keys(LANES, LANES), scores(q_ref[pl.ds(0, LANES), :], 0, LANES), NEG_BIG)
        m = jnp.max(s, axis=-1, keepdims=True)
        p = jnp.exp(s - m)
        l = jnp.sum(p, axis=-1, keepdims=True)
        o_meta = jnp.dot(p.astype(BF16), values(0, LANES), preferred_element_type=F32) / l
        o_ref[pl.ds(0, N_META), :] = o_meta[:N_META].astype(o_ref.dtype)
        lse_ref[pl.ds(0, N_META), :] = jnp.broadcast_to((m + jnp.log(l))[:N_META], (N_META, LANES))

        def accumulate(s, c0, width):
            m_prev = m_scr[...]
            m_new = jnp.maximum(m_prev, jnp.max(s, axis=-1, keepdims=True))
            alpha = jnp.exp(m_prev - m_new)
            p = jnp.exp(s - m_new)
            l_scr[...] = alpha * l_scr[...] + jnp.sum(p, axis=-1, keepdims=True)
            acc_scr[...] = alpha * acc_scr[...] + jnp.dot(p.astype(BF16), values(c0, width), preferred_element_type=F32)
            m_scr[...] = m_new

        def q_tile(i, carry):
            r0 = pl.multiple_of(N_META + i * tq, N_META)
            qt = q_ref[pl.ds(r0, tq), :]
            s = jnp.where(_meta_keys(tq, LANES), scores(qt, 0, LANES), NEG_BIG)
            m = jnp.max(s, axis=-1, keepdims=True)
            p = jnp.exp(s - m)
            m_scr[...] = m
            l_scr[...] = jnp.sum(p, axis=-1, keepdims=True)
            acc_scr[...] = jnp.dot(p.astype(BF16), values(0, LANES), preferred_element_type=F32)

            def full_block(j, c):
                c0 = pl.multiple_of(N_META + j * tq, N_META)
                accumulate(scores(qt, c0, tq), c0, tq)
                return c

            lax.fori_loop(0, i, full_block, 0)
            accumulate(jnp.where(_chunk_causal(tq, tq), scores(qt, r0, tq), NEG_BIG), r0, tq)
            o_ref[pl.ds(r0, tq), :] = (acc_scr[...] / l_scr[...]).astype(o_ref.dtype)
            lse_ref[pl.ds(r0, tq), :] = jnp.broadcast_to(m_scr[...] + jnp.log(l_scr[...]), (tq, LANES))
            return carry

        lax.fori_loop(0, nq, q_tile, 0)

    def head(width):
        return pl.BlockSpec((t_all, width), lambda h: (0, h))

    return pl.pallas_call(
        body, name=name, grid=(n_heads,),
        in_specs=[head(HEAD_W), head(HEAD_W), pl.BlockSpec((t_all, LANES), lambda h: (0, 0))],
        out_specs=[head(V_HEAD), pl.BlockSpec((None, t_all, LANES), lambda h: (h, 0, 0))],
        out_shape=[jax.ShapeDtypeStruct((t_all, n_heads * V_HEAD), BF16),
                   jax.ShapeDtypeStruct((n_heads, t_all, LANES), F32)],
        scratch_shapes=[pltpu.VMEM((t_all, HEAD_W), BF16), pltpu.VMEM((tq, 1), F32), pltpu.VMEM((tq, 1), F32),
                        pltpu.VMEM((tq, V_HEAD), F32)],
        compiler_params=_params(("parallel",)),
    )(q, kv, kr)


def _attn_bwd(q, kv, kr, o, lse, do, cos, sin, *, name, n_heads, tq, n_real, scale):
    t_all = q.shape[0]
    nq = (n_real - N_META) // tq
    assert N_META + nq * tq == n_real and tq % CHUNK == 0 and t_all >= LANES
    n_pad = t_all - n_real

    def body(q_ref, kv_ref, kr_ref, o_ref, lse_ref, do_ref, cos_ref, sin_ref, dq_ref, dkv_ref, dkr_ref,
             k_scr, dk_scr, dv_scr, dq_scr):
        k_scr[:, :QK_NOPE] = kv_ref[:, :QK_NOPE]
        k_scr[:, QK_NOPE:] = kr_ref[...]
        dk_scr[...] = jnp.zeros(dk_scr.shape, F32)
        dv_scr[...] = jnp.zeros(dv_scr.shape, F32)
        if n_pad:
            dq_ref[pl.ds(n_real, n_pad), :] = jnp.zeros((n_pad, HEAD_W), dq_ref.dtype)

        def block(qt, dot, lse_t, delta, c0, width, mask):
            kb = k_scr[pl.ds(c0, width), :]
            s = lax.dot_general(qt, kb, _NT, preferred_element_type=F32) * scale
            p = jnp.exp(s - lse_t)
            if mask is not None:
                p = jnp.where(mask, p, 0.0)
            dp = lax.dot_general(dot, kv_ref[pl.ds(c0, width), QK_NOPE:], _NT, preferred_element_type=F32)
            ds = (p * (dp - delta) * scale).astype(BF16)
            dv_scr[pl.ds(c0, width), :] += lax.dot_general(p.astype(BF16), dot, _TN, preferred_element_type=F32)
            dk_scr[pl.ds(c0, width), :] += lax.dot_general(ds, qt, _TN, preferred_element_type=F32)
            return jnp.dot(ds, kb, preferred_element_type=F32)

        def write_dq(r0, rows, dq):
            cs, sn = cos_ref[pl.ds(r0, rows), :], sin_ref[pl.ds(r0, rows), :]
            dq_ref[pl.ds(r0, rows), :] = jnp.concatenate(
                [dq[:, :QK_NOPE], _unrope(dq[:, QK_NOPE:], cs, sn)], axis=1).astype(dq_ref.dtype)

        rows_m = lax.broadcasted_iota(jnp.int32, (LANES, LANES), 0) < N_META
        dot = do_ref[pl.ds(0, LANES), :]
        delta = jnp.sum(dot.astype(F32) * o_ref[pl.ds(0, LANES), :].astype(F32), axis=-1, keepdims=True)
        dq = block(q_ref[pl.ds(0, LANES), :], dot, lse_ref[pl.ds(0, LANES), :1], delta, 0, LANES,
                   _meta_keys(LANES, LANES) & rows_m)
        write_dq(0, N_META, dq[:N_META])

        def q_tile(i, carry):
            r0 = pl.multiple_of(N_META + i * tq, N_META)
            qt = q_ref[pl.ds(r0, tq), :]
            dot = do_ref[pl.ds(r0, tq), :]
            lse_t = lse_ref[pl.ds(r0, tq), :1]
            delta = jnp.sum(dot.astype(F32) * o_ref[pl.ds(r0, tq), :].astype(F32), axis=-1, keepdims=True)
            dq_scr[...] = block(qt, dot, lse_t, delta, 0, LANES, _meta_keys(tq, LANES))

            def full_block(j, c):
                c0 = pl.multiple_of(N_META + j * tq, N_META)
                dq_scr[...] += block(qt, dot, lse_t, delta, c0, tq, None)
                return c

            lax.fori_loop(0, i, full_block, 0)
            dq_scr[...] += block(qt, dot, lse_t, delta, r0, tq, _chunk_causal(tq, tq))
            write_dq(r0, tq, dq_scr[...])
            return carry

        lax.fori_loop(0, nq, q_tile, 0)
        dk = dk_scr[...]
        dkv_ref[...] = jnp.concatenate([dk[:, :QK_NOPE], dv_scr[...]], axis=1).astype(dkv_ref.dtype)
        dkr_ref[...] = dk[:, QK_NOPE:]

    def head(width):
        return pl.BlockSpec((t_all, width), lambda h: (0, h))

    table = pl.BlockSpec((t_all, LANES), lambda h: (0, 0))
    per_head = pl.BlockSpec((None, t_all, LANES), lambda h: (h, 0, 0))
    return pl.pallas_call(
        body, name=name, grid=(n_heads,),
        in_specs=[head(HEAD_W), head(HEAD_W), table, head(V_HEAD), per_head, head(V_HEAD), table, table],
        out_specs=[head(HEAD_W), head(HEAD_W), per_head],
        out_shape=[jax.ShapeDtypeStruct((t_all, n_heads * HEAD_W), BF16), jax.ShapeDtypeStruct((t_all, n_heads * HEAD_W), BF16),
                   jax.ShapeDtypeStruct((n_heads, t_all, LANES), F32)],
        scratch_shapes=[pltpu.VMEM((t_all, HEAD_W), BF16), pltpu.VMEM((t_all, HEAD_W), F32), pltpu.VMEM((t_all, V_HEAD), F32),
                        pltpu.VMEM((tq, HEAD_W), F32)],
        compiler_params=_params(("parallel",)),
    )(q, kv, kr, o, lse, do, cos, sin)


LRU_ROWS = 128


def _shifted_back(ref, t0, rows, shift_max):
    main = ref[pl.ds(t0, rows), :]
    prev = ref[pl.ds(pl.multiple_of(jnp.maximum(t0 - SUBLANES, 0), SUBLANES), SUBLANES), :]
    prev = jnp.where(t0 > 0, prev, 0.0)
    ext = jnp.concatenate([prev, main], axis=0)
    return [main] + [pltpu.roll(ext, s, 0)[SUBLANES:, :] for s in range(1, shift_max + 1)]


def _shifted_ahead(ref, t0, rows, t_all, shift_max):
    main = ref[pl.ds(t0, rows), :]
    nxt = ref[pl.ds(pl.multiple_of(jnp.minimum(t0 + rows, t_all - SUBLANES), SUBLANES), SUBLANES), :]
    nxt = jnp.where(t0 + rows < t_all, nxt, 0.0)
    ext = jnp.concatenate([main, nxt], axis=0)
    return [main] + [pltpu.roll(ext, rows + SUBLANES - s, 0)[:rows, :] for s in range(1, shift_max + 1)]


def _conv_fwd(xp_ref, t0, rows, cw, cb):
    sh = _shifted_back(xp_ref, t0, rows, 3)
    out = cb + cw[3:4, :] * sh[0]
    for k in range(3):
        out = out + cw[k:k + 1, :] * sh[3 - k]
    return out, sh


def _lru_gates(xb, wga, bga, wgx, bgx, sp):
    xbb = xb.astype(BF16)
    r = _sigmoid(jnp.dot(xbb, wga, preferred_element_type=F32) + bga)
    ig = _sigmoid(jnp.dot(xbb, wgx, preferred_element_type=F32) + bgx)
    la = -LRU_C * r * sp
    a = jnp.exp(la)
    s = jnp.sqrt(_neg_expm1(2.0 * la))
    return xbb, r, ig, a, s


def _scan_tile(a, b, reverse):
    rows = a.shape[0]
    ridx = lax.broadcasted_iota(jnp.int32, a.shape, 0)
    s = 1
    while s < rows:
        if reverse:
            keep = ridx < rows - s
            a_sh, b_sh = pltpu.roll(a, rows - s, 0), pltpu.roll(b, rows - s, 0)
        else:
            keep = ridx >= s
            a_sh, b_sh = pltpu.roll(a, s, 0), pltpu.roll(b, s, 0)
        b = jnp.where(keep, a * b_sh + b, b)
        a = jnp.where(keep, a * a_sh, a)
        s *= 2
    return a, b


def _lru_fwd(xy, conv_w, conv_b, wga, bga, wgx, bgx, lam, *, name):
    t_all = xy.shape[0]
    dr = xy.shape[1] // 2
    c = LANES
    nblk = dr // c
    rows = LRU_ROWS
    nt = t_all // rows

    def body(xp_ref, yp_ref, cw_ref, cb_ref, wga_ref, bga_ref, wgx_ref, bgx_ref, lam_ref, hs_ref, hsy_ref):
        cw, cb = cw_ref[...], cb_ref[...]
        sp = _softplus_neg(lam_ref[...])

        def tile(t, h_in):
            t0 = pl.multiple_of(t * rows, rows)
            xb, _ = _conv_fwd(xp_ref, t0, rows, cw, cb)
            _, _, ig, a, s = _lru_gates(xb, wga_ref[0], bga_ref[...], wgx_ref[0], bgx_ref[...], sp)
            cum_a, h0 = _scan_tile(a, s * (ig * xb), reverse=False)
            hs = cum_a * h_in + h0
            hs_ref[pl.ds(t0, rows), :] = hs
            hsy_ref[pl.ds(t0, rows), :] = (hs * _gelu(yp_ref[pl.ds(t0, rows), :])).astype(BF16)
            return hs[rows - 1:, :]

        lax.fori_loop(0, nt, tile, jnp.zeros((1, c), F32))

    col = pl.BlockSpec((t_all, c), lambda b: (0, b))
    vec = pl.BlockSpec((1, c), lambda b: (0, b))
    wsp = pl.BlockSpec((1, c, c), lambda b: (b, 0, 0))
    return pl.pallas_call(
        body, name=name, grid=(nblk,),
        in_specs=[col, pl.BlockSpec((t_all, c), lambda b: (0, nblk + b)), pl.BlockSpec((4, c), lambda b: (0, b)), vec,
                  wsp, vec, wsp, vec, vec],
        out_specs=[col, col],
        out_shape=[jax.ShapeDtypeStruct((t_all, dr), F32), jax.ShapeDtypeStruct((t_all, dr), BF16)],
        compiler_params=_params(("parallel",)),
    )(xy, xy, conv_w, conv_b, wga, bga, wgx, bgx, lam)


def _lru_bwd(xy, hs, dhsy, conv_w, conv_b, wga, bga, wgx, bgx, lam, *, name):
    t_all = xy.shape[0]
    dr = xy.shape[1] // 2
    c = LANES
    nblk = dr // c
    rows = LRU_ROWS
    nt = t_all // rows

    def body(xp_ref, yp_ref, hs_ref, dh_ref, cw_ref, cb_ref, wga_ref, bga_ref, wgx_ref, bgx_ref, lam_ref,
             dxp_ref, dyp_ref, dcw_ref, dcb_ref, dwga_ref, dbga_ref, dwgx_ref, dbgx_ref, dlam_ref,
             xb_scr, r_scr, i_scr, a_scr):
        cw, cb = cw_ref[...], cb_ref[...]
        lamv = lam_ref[...]
        sp = _softplus_neg(lamv)
        sig_neg = 1.0 / (1.0 + jnp.exp(lamv))
        wga_v, wgx_v = wga_ref[0], wgx_ref[0]

        def recompute(t, carry):
            t0 = pl.multiple_of(t * rows, rows)
            xb, _ = _conv_fwd(xp_ref, t0, rows, cw, cb)
            _, r, ig, a, _ = _lru_gates(xb, wga_v, bga_ref[...], wgx_v, bgx_ref[...], sp)
            xb_scr[pl.ds(t0, rows), :] = xb
            r_scr[pl.ds(t0, rows), :] = r
            i_scr[pl.ds(t0, rows), :] = ig
            a_scr[pl.ds(t0, rows), :] = a
            return carry

        lax.fori_loop(0, nt, recompute, 0)
        dwga_ref[...] = jnp.zeros(dwga_ref.shape, F32)
        dwgx_ref[...] = jnp.zeros(dwgx_ref.shape, F32)

        def tile(ti, carry):
            lam_in, dbga, dbgx, dlam, dcw, dcb = carry
            t = nt - 1 - ti
            t0 = pl.multiple_of(t * rows, rows)
            a_now, a_next = _shifted_ahead(a_scr, t0, rows, t_all, 1)
            yp = yp_ref[pl.ds(t0, rows), :]
            dhy = dh_ref[pl.ds(t0, rows), :]
            cum_a, lam0 = _scan_tile(a_next, dhy * _gelu(yp), reverse=True)
            lam_t = cum_a * lam_in + lam0
            hs_now, hs_prev = _shifted_back(hs_ref, t0, rows, 1)
            da = lam_t * hs_prev
            xb = xb_scr[pl.ds(t0, rows), :]
            r = r_scr[pl.ds(t0, rows), :]
            ig = i_scr[pl.ds(t0, rows), :]
            la = -LRU_C * r * sp
            s = jnp.sqrt(_neg_expm1(2.0 * la))
            d_ixb = lam_t * s
            dla = da * a_now - (lam_t * ig * xb) * (a_now * a_now / s)
            dzr = dla * (-LRU_C * sp) * r * (1.0 - r)
            dzi = d_ixb * xb * ig * (1.0 - ig)
            dzr_b, dzi_b = dzr.astype(BF16), dzi.astype(BF16)
            xbb = xb.astype(BF16)
            dwga_ref[0] += lax.dot_general(xbb, dzr_b, _TN, preferred_element_type=F32)
            dwgx_ref[0] += lax.dot_general(xbb, dzi_b, _TN, preferred_element_type=F32)
            dxb = (d_ixb * ig + lax.dot_general(dzr_b, wga_v, _NT, preferred_element_type=F32)
                   + lax.dot_general(dzi_b, wgx_v, _NT, preferred_element_type=F32))
            xb_scr[pl.ds(t0, rows), :] = dxb
            dyp_ref[pl.ds(t0, rows), :] = (dhy * hs_now * _gelu_grad(yp)).astype(BF16)
            ahead = _shifted_ahead(xb_scr, t0, rows, t_all, 3)
            dxp = cw[3:4, :] * ahead[0]
            for k in range(3):
                dxp = dxp + cw[k:k + 1, :] * ahead[3 - k]
            dxp_ref[pl.ds(t0, rows), :] = dxp.astype(BF16)
            back = _shifted_back(xp_ref, t0, rows, 3)
            dcw_t = jnp.concatenate([jnp.sum(dxb * back[3 - k], axis=0, keepdims=True) for k in range(4)], axis=0)
            return (lam_t[:1, :], dbga + jnp.sum(dzr, axis=0, keepdims=True), dbgx + jnp.sum(dzi, axis=0, keepdims=True),
                    dlam + jnp.sum(dla * r, axis=0, keepdims=True), dcw + dcw_t, dcb + jnp.sum(dxb, axis=0, keepdims=True))

        zero = jnp.zeros((1, c), F32)
        _, dbga, dbgx, dlam, dcw, dcb = lax.fori_loop(0, nt, tile, (zero, zero, zero, zero, jnp.zeros((4, c), F32), zero))
        dbga_ref[...] = dbga
        dbgx_ref[...] = dbgx
        dlam_ref[...] = dlam * (LRU_C * sig_neg)
        dcw_ref[...] = dcw
        dcb_ref[...] = dcb

    col = pl.BlockSpec((t_all, c), lambda b: (0, b))
    col2 = pl.BlockSpec((t_all, c), lambda b: (0, nblk + b))
    vec = pl.BlockSpec((1, c), lambda b: (0, b))
    tap = pl.BlockSpec((4, c), lambda b: (0, b))
    wsp = pl.BlockSpec((1, c, c), lambda b: (b, 0, 0))
    vshape = jax.ShapeDtypeStruct((1, dr), F32)
    wshape = jax.ShapeDtypeStruct((nblk, c, c), F32)
    dxp, dyp, dcw, dcb, dwga, dbga, dwgx, dbgx, dlam = pl.pallas_call(
        body, name=name, grid=(nblk,),
        in_specs=[col, col2, col, col, tap, vec, wsp, vec, wsp, vec, vec],
        out_specs=[col, col, tap, vec, wsp, vec, wsp, vec, vec],
        out_shape=[jax.ShapeDtypeStruct((t_all, dr), BF16), jax.ShapeDtypeStruct((t_all, dr), BF16),
                   jax.ShapeDtypeStruct((4, dr), F32), vshape, wshape, vshape, wshape, vshape, vshape],
        scratch_shapes=[pltpu.VMEM((t_all, c), F32)] * 4,
        compiler_params=_params(("parallel",)),
    )(xy, xy, hs, dhsy, conv_w, conv_b, wga, bga, wgx, bgx, lam)
    return jnp.concatenate([dxp, dyp], axis=1), dcw, dcb, dwga, dbga, dwgx, dbgx, dlam


def _mesh_pos():
    return lax.axis_index("x"), lax.axis_index("y"), lax.axis_index("c")


def _all_gather(shards, *, name):
    n = len(shards)

    def body(*refs):
        ins, outs = refs[:n], refs[n:2 * n]
        send_sems, recv_sems, local_sems = refs[2 * n:]
        x, y, c = _mesh_pos()
        me, sibling = (x, y, c), (x, y, 1 - c)
        chips = [(1 - x, y), (x, 1 - y), (1 - x, 1 - y)]

        def slot(p):
            return 4 * p[0] + 2 * p[1] + p[2]

        def copy(a, k, block, to, src=None):
            dst = outs[a].at[slot(block)]
            return pltpu.make_async_remote_copy(
                src_ref=dst if src is None else src, dst_ref=dst, send_sem=send_sems.at[a, k],
                recv_sem=recv_sems.at[a, k], device_id=to, device_id_type=MESH)

        mine = [pltpu.make_async_copy(ins[a], outs[a].at[slot(me)], local_sems.at[a]) for a in range(n)]
        for cp in mine:
            cp.start()
        first = []
        for a in range(n):
            first.append(copy(a, 0, me, sibling, src=ins[a]))
            first += [copy(a, 1 + j, me, (*chip, c), src=ins[a]) for j, chip in enumerate(chips)]
        for cp in first:
            cp.start()
        passed = []
        for a in range(n):
            for j, chip in enumerate(chips):
                copy(a, 1 + j, (*chip, c), me).wait_recv()
                fwd = copy(a, 4 + j, (*chip, c), sibling)
                fwd.start()
                passed.append(fwd)
        for a in range(n):
            copy(a, 0, sibling, me).wait_recv()
            for j, chip in enumerate(chips):
                copy(a, 4 + j, (*chip, 1 - c), me).wait_recv()
        for cp in first + passed:
            cp.wait_send()
        for cp in mine:
            cp.wait()

    any_spec = pl.BlockSpec(memory_space=pl.ANY)
    return pl.pallas_call(
        body, name=name,
        in_specs=[any_spec] * n, out_specs=[any_spec] * n,
        out_shape=[jax.ShapeDtypeStruct((N_DEV,) + s.shape, s.dtype) for s in shards],
        scratch_shapes=[pltpu.SemaphoreType.DMA((n, 7)), pltpu.SemaphoreType.DMA((n, 7)), pltpu.SemaphoreType.DMA((n,))],
    )(*shards)


def _rs_sibling(grads, *, name):
    n = len(grads)

    def body(*refs):
        ins, outs = refs[:n], refs[n:2 * n]
        send_sems, recv_sems = refs[2 * n:]
        x, y, c = _mesh_pos()
        copies = []
        for a in range(n):
            for j in range(4):
                copies.append(pltpu.make_async_remote_copy(
                    src_ref=ins[a].at[2 * j + (1 - c)], dst_ref=outs[a].at[j], send_sem=send_sems.at[a, j],
                    recv_sem=recv_sems.at[a, j], device_id=(x, y, 1 - c), device_id_type=MESH))
        for cp in copies:
            cp.start()
        for cp in copies:
            cp.wait()

    any_spec = pl.BlockSpec(memory_space=pl.ANY)
    return pl.pallas_call(
        body, name=name, in_specs=[any_spec] * n, out_specs=[any_spec] * n,
        out_shape=[jax.ShapeDtypeStruct((4,) + g.shape[1:], g.dtype) for g in grads],
        scratch_shapes=[pltpu.SemaphoreType.DMA((n, 4)), pltpu.SemaphoreType.DMA((n, 4))],
    )(*grads)


def _rs_chips(parts, *, name):
    n = len(parts)

    def body(*refs):
        ins, outs = refs[:n], refs[n:2 * n]
        send_sems, recv_sems, local_sems = refs[2 * n:]
        x, y, c = _mesh_pos()
        copies, local = [], []
        for a in range(n):
            for k in (1, 2, 3):
                px = 1 - x if k & 2 else x
                py = 1 - y if k & 1 else y
                copies.append(pltpu.make_async_remote_copy(
                    src_ref=ins[a].at[2 * px + py], dst_ref=outs[a].at[k - 1], send_sem=send_sems.at[a, k - 1],
                    recv_sem=recv_sems.at[a, k - 1], device_id=(px, py, c), device_id_type=MESH))
            local.append(pltpu.make_async_copy(ins[a].at[2 * x + y], outs[a].at[3], local_sems.at[a]))
        for cp in copies + local:
            cp.start()
        for cp in copies + local:
            cp.wait()

    any_spec = pl.BlockSpec(memory_space=pl.ANY)
    return pl.pallas_call(
        body, name=name, in_specs=[any_spec] * n, out_specs=[any_spec] * n,
        out_shape=[jax.ShapeDtypeStruct(p.shape, p.dtype) for p in parts],
        scratch_shapes=[pltpu.SemaphoreType.DMA((n, 3)), pltpu.SemaphoreType.DMA((n, 3)), pltpu.SemaphoreType.DMA((n,))],
    )(*parts)


_HBM = pl.BlockSpec(memory_space=pltpu.HBM)
_SEM = pl.BlockSpec(memory_space=pltpu.SEMAPHORE)
_ANY = pl.BlockSpec(memory_space=pl.ANY)
_EFFECT = pltpu.SideEffectType.DATAFLOW_SIDE_EFFECTING


def _slot(p):
    return 4 * p[0] + 2 * p[1] + p[2]


def _remote(src, dst, send, recv, idx, to):
    return pltpu.make_async_remote_copy(src_ref=src, dst_ref=dst, send_sem=send.at[idx], recv_sem=recv.at[idx],
                                        device_id=to, device_id_type=MESH)


def _ag_plan_own(a, src, land, send, recv):
    x, y, c = _mesh_pos()
    dst = land.at[_slot((x, y, c))]
    targets = [(x, y, 1 - c), (1 - x, y, c), (x, 1 - y, c), (1 - x, 1 - y, c)]
    return [_remote(src, dst, send, recv, 4 * a + k, to) for k, to in enumerate(targets)]


def _ag_plan_pass(a, src, land, send, recv):
    x, y, c = _mesh_pos()
    blocks = [land.at[_slot((px, py, c))] for px, py in ((1 - x, y), (x, 1 - y), (1 - x, 1 - y))]
    return [_remote(blk, blk, send, recv, 3 * a + k, (x, y, 1 - c)) for k, blk in enumerate(blocks)]


def _rs_plan_sibling(a, src, land, send, recv):
    x, y, c = _mesh_pos()
    return [_remote(src.at[2 * j + (1 - c)], land.at[j], send, recv, 4 * a + j, (x, y, 1 - c)) for j in range(4)]


def _rs_plan_chips(a, src, land, send, recv):
    x, y, c = _mesh_pos()
    out = []
    for k in (1, 2, 3):
        px = 1 - x if k & 2 else x
        py = 1 - y if k & 1 else y
        out.append(_remote(src.at[2 * px + py], land.at[k - 1], send, recv, 3 * a + k - 1, (px, py, c)))
    return out


def _in_hbm(a):
    return pltpu.with_memory_space_constraint(a, pltpu.HBM)


def _exchange_start(srcs, lands, plan, n_k, *, name):
    ns, n = len(srcs), len(lands)

    def body(*refs):
        src_refs, land_refs = refs[:ns], refs[ns:ns + n]
        send, recv = refs[ns + n], refs[ns + n + 1]
        token = refs[-1]
        for a in range(n):
            for cp in plan(a, src_refs[a] if ns else None, land_refs[a], send, recv):
                cp.start()
        token[...] = jnp.zeros(token.shape, token.dtype)

    bufs = list(srcs) + list(lands)
    outs = pl.pallas_call(
        body, name=name,
        out_shape=(pltpu.SemaphoreType.DMA((n * n_k,)), pltpu.SemaphoreType.DMA((n * n_k,)),
                   *[pltpu.HBM(b.shape, b.dtype) for b in bufs], jax.ShapeDtypeStruct((SUBLANES, LANES), F32)),
        in_specs=[_HBM] * (ns + n),
        out_specs=(_SEM, _SEM, *[_HBM] * (ns + n), pl.BlockSpec(memory_space=pltpu.VMEM)),
        input_output_aliases={i: 2 + i for i in range(ns + n)},
        compiler_params=pltpu.CompilerParams(has_side_effects=_EFFECT),
    )(*[_in_hbm(b) for b in bufs])
    return outs[0], outs[1], list(outs[2:2 + ns]), list(outs[2 + ns:2 + ns + n]), outs[-1]


def _exchange_wait(started, plan, after, *, name):
    send, recv, srcs, lands, _ = started
    ns, n = len(srcs), len(lands)

    def body(*refs):
        src_refs, land_refs = refs[:ns], refs[ns:ns + n]
        send_ref, recv_ref = refs[ns + n], refs[ns + n + 1]
        for a in range(n):
            for cp in plan(a, src_refs[a] if ns else None, land_refs[a], send_ref, recv_ref):
                cp.wait_send()
                cp.wait_recv()

    bufs = list(srcs) + list(lands)
    outs = pl.pallas_call(
        body, name=name,
        out_shape=tuple(pltpu.HBM(b.shape, b.dtype) for b in bufs),
        in_specs=[_HBM] * (ns + n) + [_SEM, _SEM, _ANY],
        out_specs=tuple([_HBM] * (ns + n)),
        input_output_aliases={i: i for i in range(ns + n)},
        compiler_params=pltpu.CompilerParams(has_side_effects=_EFFECT),
    )(*bufs, send, recv, after)
    return list(outs[:ns]), list(outs[ns:])


def _pair_add(grads, landed, core, *, name, tr):
    _, r_all, c_all = grads.shape

    def body(core_ref, g_ref, l_ref, o_ref):
        o_ref[...] = (g_ref[...].astype(F32) + l_ref[...].astype(F32)).astype(o_ref.dtype)

    return pl.pallas_call(
        body, name=name,
        grid_spec=pltpu.PrefetchScalarGridSpec(
            num_scalar_prefetch=1, grid=(4, r_all // tr),
            in_specs=[pl.BlockSpec((None, tr, c_all), lambda j, i, core_ref: (2 * j + core_ref[0], i, 0)),
                      pl.BlockSpec((None, tr, c_all), lambda j, i, core_ref: (j, i, 0))],
            out_specs=pl.BlockSpec((None, tr, c_all), lambda j, i, core_ref: (j, i, 0))),
        out_shape=jax.ShapeDtypeStruct((4, r_all, c_all), grads.dtype),
        compiler_params=_params(("parallel", "parallel")),
    )(core, grads, landed)


def _adamw_math(w, g, m, v):
    m2 = ADAM_B1 * m + (1.0 - ADAM_B1) * g
    v2 = ADAM_B2 * v + (1.0 - ADAM_B2) * (g * g)
    m_hat = m2 / (1.0 - ADAM_B1 ** ADAM_STEP)
    v_hat = v2 / (1.0 - ADAM_B2 ** ADAM_STEP)
    delta = -ADAM_LR * (m_hat / (jnp.sqrt(v_hat) + ADAM_EPS) + ADAM_WD * w)
    return delta, m2, v2


def _adamw(w, m, v, terms, order, *, name, tr, col_block=None, own=None):
    r_all, c_all = w.shape
    n_slots = terms.shape[0]

    def body(*refs):
        if col_block is not None or own is not None:
            refs = refs[1:]
        own_ref = None
        if own is not None:
            own_ref, refs = refs[0], refs[1:]
        w_ref, m_ref, v_ref, t_ref, g_ref, d_ref, m2_ref, v2_ref = refs
        if own_ref is not None:
            g = own_ref[...].astype(F32) + t_ref[order[0]].astype(F32)
        else:
            g = t_ref[order[0]].astype(F32)
        for s in order[1:]:
            g = g + t_ref[s].astype(F32)
        delta, m2, v2 = _adamw_math(w_ref[...], g, m_ref[...], v_ref[...])
        g_ref[...] = g
        d_ref[...] = delta
        m2_ref[...] = m2
        v2_ref[...] = v2

    shape = jax.ShapeDtypeStruct((r_all, c_all), F32)
    if own is not None:
        row = pl.BlockSpec((tr, c_all), lambda i, idx: (i, 0))
        return pl.pallas_call(
            body, name=name,
            grid_spec=pltpu.PrefetchScalarGridSpec(
                num_scalar_prefetch=1, grid=(r_all // tr,),
                in_specs=[pl.BlockSpec((None, tr, c_all), lambda i, idx: (idx[0], i, 0)), row, row, row,
                          pl.BlockSpec((n_slots, tr, c_all), lambda i, idx: (0, i, 0))],
                out_specs=[row] * 4),
            out_shape=[shape] * 4, compiler_params=_params(("parallel",)),
        )(own[1], own[0], w, m, v, terms)
    if col_block is None:
        row = pl.BlockSpec((tr, c_all), lambda i: (i, 0))
        return pl.pallas_call(
            body, name=name, grid=(r_all // tr,),
            in_specs=[row, row, row, pl.BlockSpec((n_slots, tr, c_all), lambda i: (0, i, 0))],
            out_specs=[row] * 4, out_shape=[shape] * 4, compiler_params=_params(("parallel",)),
        )(w, m, v, terms)
    row = pl.BlockSpec((tr, c_all), lambda i, blk: (i, 0))
    return pl.pallas_call(
        body, name=name,
        grid_spec=pltpu.PrefetchScalarGridSpec(
            num_scalar_prefetch=1, grid=(r_all // tr,),
            in_specs=[row, row, row, pl.BlockSpec((n_slots, tr, c_all), lambda i, blk: (0, i, blk[0]))],
            out_specs=[row] * 4),
        out_shape=[shape] * 4, compiler_params=_params(("parallel",)),
    )(col_block, w, m, v, terms)


def _rope_tables(t_all):
    pos = jnp.arange(t_all, dtype=F32)
    inv_freq = ROPE_THETA ** (-jnp.arange(0, QK_ROPE, 2, dtype=F32) / QK_ROPE)
    ang = pos[:, None] * inv_freq[None, :]
    cos, sin = jnp.cos(ang), jnp.sin(ang)
    return jnp.tile(cos, (1, LANES // (QK_ROPE // 2))), jnp.tile(sin, (1, LANES // (QK_ROPE // 2)))


def _adam_row_tile(r_all, c_all):
    target = max(SUBLANES, (512 * 1024) // (4 * c_all))
    return _pick(r_all, [t for t in (512, 256, 128, 64, 32, 16, 8) if t <= target])


def _rows_natural(wg):
    return wg.reshape(wg.shape[0] * wg.shape[1], wg.shape[2])


def _mla_layer_fwd(tag, h, g_mix, ws, qn, kvn, cos, sin, *, tm, tq, n_heads, scale, n_real):
    w_in, w_uq, w_ukv, w_o = _rows_natural(ws[0]), ws[1], ws[2], _rows_natural(ws[3])
    t_all, d = h.shape
    lq, lkv = qn.shape[1], kvn.shape[1]
    tmb = _pick(t_all, _ROW_TILES)
    hn = _rmsnorm_fwd(h, g_mix, name=f"norm_mix{tag}", tm=tm)
    proj = _mm_nn(hn, w_in, name=f"mla_in{tag}", out_dtype=F32, tm=tmb, tn=w_in.shape[1], tk=_pick(d, _DIVS))
    cq, ckv, kr = _mla_prep_fwd(proj, qn, kvn, cos, sin, name=f"mla_prep{tag}", tm=tm, lq=lq, lkv=lkv)
    q = _mm_nn(cq, w_uq, name=f"mla_q{tag}", out_dtype=BF16, tm=tmb, tn=w_uq.shape[2], tk=lq, b_blocked=True,
               epilogue=_rope_q_epilogue, extras=(cos, sin))
    kv = _mm_nn(ckv, w_ukv, name=f"mla_kv{tag}", out_dtype=BF16, tm=tmb, tn=w_ukv.shape[2], tk=lkv, b_blocked=True)
    o, lse = _attn_fwd(q, kv, kr, name=f"attn_fwd{tag}", n_heads=n_heads, tq=tq, n_real=n_real, scale=scale)
    h_mid = _mm_nn(o, w_o, name=f"mla_o{tag}", out_dtype=F32, tm=tmb, tn=_pick(d, _DIVS[2:]), tk=_pick(o.shape[1], _DIVS), res=h)
    return h_mid, (hn, proj, cq, ckv, kr, q, kv, o, lse)


def _mla_layer_bwd(tag, dh, dh_b, h_in, saved, g_mix, ws, qn, kvn, cos, sin, *, tm, tq, n_heads, scale, n_real):
    hn, proj, cq, ckv, kr, q, kv, o, lse = saved
    w_in, w_uq, w_ukv, w_o = _rows_natural(ws[0]), ws[1], ws[2], _rows_natural(ws[3])
    t_all, d = h_in.shape
    lq, lkv = qn.shape[1], kvn.shape[1]
    ov = o.shape[1]
    tmb = _pick(t_all, _ROW_TILES)
    tn_d, tk_d = _pick(d, _DIVS[1:]), _pick(d, _DIVS)
    do = _mm_nt(dh_b, w_o, name=f"mla_do{tag}", out_dtype=BF16, tm=tmb, tn=_pick(ov, _DIVS[1:]), tk=tk_d)
    dw_o = _mm_tn(o, dh_b, name=f"mla_dwo{tag}", out_dtype=BF16, tm=_pick(ov, _DIVS[1:]), tn=tn_d, tk=tmb)
    dq, dkv, dkr_h = _attn_bwd(q, kv, kr, o, lse, do, cos, sin, name=f"attn_bwd{tag}", n_heads=n_heads, tq=tq, n_real=n_real,
                               scale=scale)
    hw, kw = w_uq.shape[2], w_ukv.shape[2]
    dw_uq = _mm_tn(cq, dq, name=f"mla_dwuq{tag}", out_dtype=BF16, tm=lq, tn=hw, tk=tmb, out_block=hw)
    dcq = _mm_nt(dq, w_uq, name=f"mla_dcq{tag}", out_dtype=F32, tm=tmb, tn=lq, tk=hw, b_blocked=True)
    dw_ukv = _mm_tn(ckv, dkv, name=f"mla_dwukv{tag}", out_dtype=BF16, tm=lkv, tn=kw, tk=tmb, out_block=kw)
    dckv = _mm_nt(dkv, w_ukv, name=f"mla_dckv{tag}", out_dtype=F32, tm=tmb, tn=lkv, tk=kw, b_blocked=True)
    dproj, dqn, dkvn = _mla_prep_bwd(dcq, dckv, dkr_h, proj, qn, kvn, cos, sin, name=f"mla_prep_bwd{tag}", tm=tm, lq=lq, lkv=lkv)
    wc = w_in.shape[1]
    dw_in = _mm_tn(hn, dproj, name=f"mla_dwin{tag}", out_dtype=BF16, tm=tn_d, tn=wc, tk=tmb)
    dhn = _mm_nt(dproj, w_in, name=f"mla_dhn{tag}", out_dtype=F32, tm=tmb, tn=tn_d, tk=wc)
    dh, dh_b, dg = _rmsnorm_bwd(dhn, h_in, g_mix, dh, name=f"norm_mix_bwd{tag}", tm=tm)
    return dh, dh_b, dg, dqn, dkvn, [dw_in.reshape(N_DEV, -1, wc), dw_uq, dw_ukv, dw_o.reshape(N_DEV, -1, d)]


def _lru_layer_fwd(tag, h, g_mix, ws, small, *, tm):
    w_lin, w_lo = ws[0], _rows_natural(ws[1])
    t_all, d = h.shape
    dr = w_lo.shape[0]
    tmb = _pick(t_all, _ROW_TILES)
    hn = _rmsnorm_fwd(h, g_mix, name=f"norm_mix{tag}", tm=tm)
    xy = _mm_nn(hn, w_lin, name=f"lru_in{tag}", out_dtype=F32, tm=tmb, tn=w_lin.shape[2], tk=_pick(d, _DIVS), b_blocked=True)
    hs, hsy = _lru_fwd(xy, *small, name=f"lru_fwd{tag}")
    h_mid = _mm_nn(hsy, w_lo, name=f"lru_o{tag}", out_dtype=F32, tm=tmb, tn=_pick(d, _DIVS[2:]), tk=_pick(dr, _DIVS), res=h)
    return h_mid, (hn, xy, hs, hsy)


def _lru_layer_bwd(tag, dh, dh_b, h_in, saved, g_mix, ws, small, *, tm):
    hn, xy, hs, hsy = saved
    w_lin, w_lo = ws[0], _rows_natural(ws[1])
    t_all, d = h_in.shape
    dr = w_lo.shape[0]
    tmb = _pick(t_all, _ROW_TILES)
    tn_d, tk_d = _pick(d, _DIVS[1:]), _pick(d, _DIVS)
    dhsy = _mm_nt(dh_b, w_lo, name=f"lru_dhsy{tag}", out_dtype=F32, tm=tmb, tn=_pick(dr, _DIVS[1:]), tk=tk_d)
    dw_lo = _mm_tn(hsy, dh_b, name=f"lru_dwo{tag}", out_dtype=BF16, tm=_pick(dr, _DIVS[1:]), tn=tn_d, tk=tmb)
    dxy, *dsmall = _lru_bwd(xy, hs, dhsy, *small, name=f"lru_bwd{tag}")
    lw = w_lin.shape[2]
    dw_lin = _mm_tn(hn, dxy, name=f"lru_dwin{tag}", out_dtype=BF16, tm=tn_d, tn=lw, tk=tmb, out_block=lw)
    dhn = _mm_nt(dxy, w_lin, name=f"lru_dhn{tag}", out_dtype=F32, tm=tmb, tn=tn_d, tk=lw, b_blocked=True)
    dh, dh_b, dg = _rmsnorm_bwd(dhn, h_in, g_mix, dh, name=f"norm_mix_bwd{tag}", tm=tm)
    return dh, dh_b, dg, tuple(dsmall), [dw_lin, dw_lo.reshape(N_DEV, -1, d)]


def _ffn_layer_fwd(tag, h_mid, g_ffn, ws, *, tm):
    w_gu, w_down = ws[0], _rows_natural(ws[1])
    t_all, d = h_mid.shape
    f_all = w_down.shape[0]
    tmb = _pick(t_all, _ROW_TILES)
    fk = _pick(f_all, (1408,) + _DIVS[1:])
    hn2 = _rmsnorm_fwd(h_mid, g_ffn, name=f"norm_ffn{tag}", tm=tm)
    gu = _mm_nn(hn2, w_gu, name=f"ffn_gu{tag}", out_dtype=F32, tm=tmb, tn=w_gu.shape[2], tk=_pick(d, _DIVS), b_blocked=True)
    act = _swiglu_fwd(gu, name=f"swiglu_fwd{tag}", tm=tm, tc=fk)
    h_out = _mm_nn(act, w_down, name=f"ffn_down{tag}", out_dtype=F32, tm=tmb, tn=_pick(d, _DIVS[1:]), tk=fk, res=h_mid)
    return h_out, (hn2, gu, act)


def _ffn_layer_bwd(tag, dh, dh_b, h_mid, saved, g_ffn, ws, *, tm):
    hn2, gu, act = saved
    w_gu, w_down = ws[0], _rows_natural(ws[1])
    t_all, d = h_mid.shape
    f_all = w_down.shape[0]
    f_local = w_gu.shape[2]
    tmb = _pick(t_all, _ROW_TILES)
    fk = _pick(f_all, (1408,) + _DIVS[1:])
    tn_d, tk_d = _pick(d, _DIVS[1:]), _pick(d, _DIVS)
    dact = _mm_nt(dh_b, w_down, name=f"ffn_dact{tag}", out_dtype=F32, tm=tmb, tn=fk, tk=tk_d)
    dgu = _swiglu_bwd(dact, gu, name=f"swiglu_bwd{tag}", tm=tm, tc=fk)
    dw_down = _mm_tn(act, dh_b, name=f"ffn_dwdown{tag}", out_dtype=BF16, tm=fk, tn=tn_d, tk=tmb)
    dhn2 = _mm_nt(dgu, w_gu, name=f"ffn_dhn{tag}", out_dtype=F32, tm=tmb, tn=tn_d, tk=f_local, b_blocked=True)
    dw_gu = _mm_tn(hn2, dgu, name=f"ffn_dwgu{tag}", out_dtype=BF16, tm=tn_d, tn=f_local, tk=tmb, out_block=f_local)
    dh, dh_b, dg = _rmsnorm_bwd(dhn2, h_mid, g_ffn, dh, name=f"norm_ffn_bwd{tag}", tm=tm)
    return dh, dh_b, dg, [dw_gu, dw_down.reshape(N_DEV, -1, d)]


def kernel(x, meta_tokens, norm_mix, norm_ffn, norm_final, mla_w_in, mla_q_norm, mla_kv_norm, mla_w_uq, mla_w_ukv, mla_w_o, lru_w_in, lru_conv_w, lru_conv_b, lru_w_gate_a, lru_b_gate_a, lru_w_gate_x, lru_b_gate_x, lru_lambda, lru_w_o, ffn_w_gu, ffn_w_down, loss_target, m_meta_tokens, m_norm_mix, m_norm_ffn, m_norm_final, m_mla_w_in, m_mla_q_norm, m_mla_kv_norm, m_mla_w_uq, m_mla_w_ukv, m_mla_w_o, m_lru_w_in, m_lru_conv_w, m_lru_conv_b, m_lru_w_gate_a, m_lru_b_gate_a, m_lru_w_gate_x, m_lru_b_gate_x, m_lru_lambda, m_lru_w_o, m_ffn_w_gu, m_ffn_w_down, v_meta_tokens, v_norm_mix, v_norm_ffn, v_norm_final, v_mla_w_in, v_mla_q_norm, v_mla_kv_norm, v_mla_w_uq, v_mla_w_ukv, v_mla_w_o, v_lru_w_in, v_lru_conv_w, v_lru_conv_b, v_lru_w_gate_a, v_lru_b_gate_a, v_lru_w_gate_x, v_lru_b_gate_x, v_lru_lambda, v_lru_w_o, v_ffn_w_gu, v_ffn_w_down):
    seq, d = x.shape[1], x.shape[2]
    assert seq % CHUNK == 0
    n_real = N_META + seq
    t_all = -(-n_real // LANES) * LANES
    tm = _pick(t_all, (384, 256, 128))
    tq = _pick(seq, (512, 256, 128, 64))
    depth = norm_mix.shape[0]
    n_mla, n_lru = mla_w_in.shape[0], lru_w_in.shape[0]
    lq, lkv = mla_q_norm.shape[1], mla_kv_norm.shape[1]
    w_in_cols = lq + lkv + LANES
    heads_local = mla_w_uq.shape[2] // (QK_NOPE + QK_ROPE)
    n_heads = heads_local * N_DEV
    dr = lru_w_gate_a.shape[1] * lru_w_gate_a.shape[2]
    scale = (QK_NOPE + QK_ROPE) ** -0.5
    cx, cy, cc = _mesh_pos()
    core = jnp.reshape(cc, (1,)).astype(jnp.int32)
    my_slot = jnp.reshape(4 * cx + 2 * cy + cc, (1,)).astype(jnp.int32)

    def pad_cols(w, cols):
        return jnp.pad(w, ((0, 0), (0, cols - w.shape[1])))

    def pad_heads(w):
        k_all = w.shape[0]
        w3 = w.reshape(k_all, heads_local, QK_NOPE + QK_ROPE)
        return jnp.pad(w3, ((0, 0), (0, 0), (0, HEAD_W - QK_NOPE - QK_ROPE))).reshape(k_all, heads_local * HEAD_W)

    def unpad_heads(w):
        k_all = w.shape[0]
        return w.reshape(k_all, heads_local, HEAD_W)[:, :, :QK_NOPE + QK_ROPE].reshape(k_all, -1)

    small_rows = N_META + n_lru * 4 + 2 * n_lru
    small_pad = -(-small_rows // SUBLANES) * SUBLANES

    def pack_small(meta, conv_w, conv_b, lam):
        rows = jnp.concatenate([meta, conv_w.reshape(n_lru * 4, -1), conv_b, lam], axis=0)
        return jnp.pad(rows, ((0, small_pad - small_rows), (0, 0)))

    def unpack_small(p):
        o1 = N_META + n_lru * 4
        return (p[:N_META], p[N_META:o1].reshape(n_lru, 4, -1), p[o1:o1 + n_lru], p[o1 + n_lru:o1 + 2 * n_lru])

    mla_shards, lru_shards, ffn_shards = [], [], []
    for j in range(n_mla):
        mla_shards.append([pad_cols(mla_w_in[j], w_in_cols).astype(BF16), pad_heads(mla_w_uq[j]).astype(BF16),
                           mla_w_ukv[j].astype(BF16), mla_w_o[j].astype(BF16)])
    for j in range(n_lru):
        lru_shards.append([lru_w_in[j].astype(BF16), lru_w_o[j].astype(BF16)])
    for layer in range(depth):
        ffn_shards.append([ffn_w_gu[layer].astype(BF16), ffn_w_down[layer].astype(BF16)])

    small_full = _all_gather([pack_small(meta_tokens, lru_conv_w, lru_conv_b, lru_lambda)], name="ag_small")[0]
    small_full = jnp.transpose(small_full, (1, 0, 2)).reshape(small_pad, -1)
    meta_full, conv_w_full, conv_b_full, lam_full = unpack_small(small_full)

    n_sub = 2 * depth
    groups = []
    for layer in range(depth):
        groups += [mla_shards[layer // 2] if layer % 2 == 0 else lru_shards[layer // 2], ffn_shards[layer]]
    slot_idx = 4 * cx + 2 * cy + cc
    ag_own = []
    for gi, shards in enumerate(groups):
        lands = [lax.dynamic_update_slice(lax.empty((N_DEV,) + s.shape, s.dtype), s[None], (slot_idx, 0, 0)) for s in shards]
        ag_own.append(_exchange_start(shards, lands, _ag_plan_own, 4, name=f"ag{gi}_start"))
    ag_pass = [None] * n_sub
    weights = [None] * n_sub

    def ag_landed(gi, after):
        _, lands = _exchange_wait(ag_own[gi], _ag_plan_own, after, name=f"ag{gi}_wait")
        ag_pass[gi] = _exchange_start([], lands, _ag_plan_pass, 3, name=f"ag{gi}_pass")
        return ag_pass[gi][4][0, 0]

    def ag_done(gi, after):
        _, weights[gi] = _exchange_wait(ag_pass[gi], _ag_plan_pass, after, name=f"ag{gi}_pass_wait")

    cos, sin = _rope_tables(t_all)
    zeros_tail = jnp.zeros((t_all - n_real, d), F32)
    started = ag_own[0][4][0, 0]
    for st in ag_own[1:]:
        started = started + st[4][0, 0]
    h = jnp.concatenate([meta_full + started, x[0], zeros_tail], axis=0)
    target = jnp.concatenate([jnp.zeros((N_META, d), F32), loss_target[0], zeros_tail], axis=0)

    attn_kw = dict(tm=tm, tq=tq, n_heads=n_heads, scale=scale, n_real=n_real)

    def lru_small(j):
        return (conv_w_full[j], conv_b_full[j][None, :], lru_w_gate_a[j].astype(BF16), lru_b_gate_a[j].reshape(1, dr),
                lru_w_gate_x[j].astype(BF16), lru_b_gate_x[j].reshape(1, dr), lam_full[j][None, :])

    def before_sublayer(k, act):
        tok = ag_landed(k, act) if k <= 1 else 0.0
        ag_done(k, act)
        if 1 <= k < n_sub - 1:
            tok = tok + ag_landed(k + 1, act)
        return tok

    saved = []
    for layer in range(depth):
        j = layer // 2
        g_mix = norm_mix[layer][None, :] + before_sublayer(2 * layer, h)
        if layer % 2 == 0:
            h_mid, mix_saved = _mla_layer_fwd(layer, h, g_mix, weights[2 * layer], mla_q_norm[j][None, :],
                                              mla_kv_norm[j][None, :], cos, sin, **attn_kw)
        else:
            h_mid, mix_saved = _lru_layer_fwd(layer, h, g_mix, weights[2 * layer], lru_small(j), tm=tm)
        g_ffn = norm_ffn[layer][None, :] + before_sublayer(2 * layer + 1, h_mid)
        h_out, ffn_saved = _ffn_layer_fwd(layer, h_mid, g_ffn, weights[2 * layer + 1], tm=tm)
        saved.append((h, h_mid, mix_saved, ffn_saved))
        h = h_out

    loss_part, dh, dh_b, dg_final = _loss_head(h, target, norm_final[None, :], name="loss_head", tm=tm, n_real=n_real)
    loss = lax.psum(loss_part[0, 0], ("x", "y", "c"))

    rs_sib, rs_chip, reduced = [None] * n_sub, [None] * n_sub, [None] * n_sub
    chip_idx = jnp.reshape(2 * cx + cy, (1,)).astype(jnp.int32)

    def rs_begin(k, grads):
        lands = [lax.empty((4,) + g.shape[1:], g.dtype) for g in grads]
        rs_sib[k] = _exchange_start(grads, lands, _rs_plan_sibling, 4, name=f"rs{k}_start")
        return rs_sib[k][4][0, 0]

    def rs_middle(k, after):
        grads, landed = _exchange_wait(rs_sib[k], _rs_plan_sibling, after, name=f"rs{k}_wait")
        parts = [_pair_add(g, l, core, name=f"rs{k}_add{a}", tr=_adam_row_tile(g.shape[1], g.shape[2]))
                 for a, (g, l) in enumerate(zip(grads, landed))]
        lands = [lax.empty((3,) + p.shape[1:], p.dtype) for p in parts]
        rs_chip[k] = _exchange_start(parts, lands, _rs_plan_chips, 3, name=f"rs{k}_chips")
        return rs_chip[k][4][0, 0]

    def rs_end(k, after):
        reduced[k] = _exchange_wait(rs_chip[k], _rs_plan_chips, after, name=f"rs{k}_chips_wait")

    d_norm_mix, d_norm_ffn = [None] * depth, [None] * depth
    d_qn, d_kvn = [None] * n_mla, [None] * n_mla
    d_small = {k: [None] * n_lru for k in ("cw", "cb", "wga", "bga", "wgx", "bgx", "lam")}
    tok, waiting = 0.0, None
    for layer in reversed(range(depth)):
        j = layer // 2
        h_in, h_mid, mix_saved, ffn_saved = saved[layer]
        dh, dh_b, d_norm_ffn[layer], ffn_g = _ffn_layer_bwd(layer, dh, dh_b, h_mid, ffn_saved, norm_ffn[layer][None, :] + tok,
                                                            weights[2 * layer + 1], tm=tm)
        tok = rs_begin(2 * layer + 1, ffn_g)
        if waiting is not None:
            tok = tok + rs_middle(waiting, dh)
        waiting = 2 * layer + 1
        g_mix = norm_mix[layer][None, :] + tok
        if layer % 2 == 0:
            dh, dh_b, d_norm_mix[layer], d_qn[j], d_kvn[j], mix_g = _mla_layer_bwd(
                layer, dh, dh_b, h_in, mix_saved, g_mix, weights[2 * layer], mla_q_norm[j][None, :], mla_kv_norm[j][None, :],
                cos, sin, **attn_kw)
        else:
            dh, dh_b, d_norm_mix[layer], dsmall, mix_g = _lru_layer_bwd(layer, dh, dh_b, h_in, mix_saved, g_mix,
                                                                        weights[2 * layer], lru_small(j), tm=tm)
            for key, val in zip(("cw", "cb", "wga", "bga", "wgx", "bgx", "lam"), dsmall):
                d_small[key][j] = val
        tok = rs_begin(2 * layer, mix_g) + rs_middle(waiting, dh)
        waiting = 2 * layer
    rs_middle(waiting, dh)

    grad_x = dh[N_META:n_real][None]

    def adam_sharded(k, a, w, m, v, tag):
        parts, landed = reduced[k]
        r_all, c_all = landed[a].shape[1], landed[a].shape[2]
        return _adamw(w.reshape(r_all, c_all), m.reshape(r_all, c_all), v.reshape(r_all, c_all), landed[a], (0, 1, 2),
                      name=f"adamw_{tag}", tr=_adam_row_tile(r_all, c_all), own=(parts[a], chip_idx))

    def stacked(outs):
        return [jnp.stack([o[k] for o in outs], axis=0) for k in range(4)]

    per = {nm: [None] * (depth if nm.startswith("ffn") else n_mla) for nm in
           ("mla_w_in", "mla_w_uq", "mla_w_ukv", "mla_w_o", "lru_w_in", "lru_w_o", "ffn_w_gu", "ffn_w_down")}
    after = dh
    for k in reversed(range(n_sub)):
        rs_end(k, after)
        layer, j = k // 2, k // 4
        if k % 2 == 1:
            per["ffn_w_gu"][layer] = adam_sharded(k, 0, ffn_w_gu[layer], m_ffn_w_gu[layer], v_ffn_w_gu[layer], f"ffn_w_gu{layer}")
            per["ffn_w_down"][layer] = adam_sharded(k, 1, ffn_w_down[layer], m_ffn_w_down[layer], v_ffn_w_down[layer],
                                                    f"ffn_w_down{layer}")
            after = per["ffn_w_down"][layer][0]
        elif layer % 2 == 0:
            per["mla_w_in"][j] = [t[:, :lq + lkv + QK_ROPE] for t in adam_sharded(
                k, 0, pad_cols(mla_w_in[j], w_in_cols), pad_cols(m_mla_w_in[j], w_in_cols), pad_cols(v_mla_w_in[j], w_in_cols),
                f"mla_w_in{j}")]
            per["mla_w_uq"][j] = [unpad_heads(t) for t in adam_sharded(
                k, 1, pad_heads(mla_w_uq[j]), pad_heads(m_mla_w_uq[j]), pad_heads(v_mla_w_uq[j]), f"mla_w_uq{j}")]
            per["mla_w_ukv"][j] = adam_sharded(k, 2, mla_w_ukv[j], m_mla_w_ukv[j], v_mla_w_ukv[j], f"mla_w_ukv{j}")
            per["mla_w_o"][j] = adam_sharded(k, 3, mla_w_o[j], m_mla_w_o[j], v_mla_w_o[j], f"mla_w_o{j}")
            after = per["mla_w_o"][j][0]
        else:
            per["lru_w_in"][j] = adam_sharded(k, 0, lru_w_in[j], m_lru_w_in[j], v_lru_w_in[j], f"lru_w_in{j}")
            per["lru_w_o"][j] = adam_sharded(k, 1, lru_w_o[j], m_lru_w_o[j], v_lru_w_o[j], f"lru_w_o{j}")
            after = per["lru_w_o"][j][0]
    res = {nm: stacked(outs) for nm, outs in per.items()}

    d_meta = dh[:N_META]
    small_grad = pack_small(d_meta, jnp.stack(d_small["cw"], axis=0), jnp.concatenate(d_small["cb"], axis=0),
                            jnp.concatenate(d_small["lam"], axis=0))
    rep_grads = [
        jnp.concatenate(d_norm_mix, axis=0), jnp.concatenate(d_norm_ffn, axis=0), dg_final,
        jnp.concatenate(d_qn, axis=0), jnp.concatenate(d_kvn, axis=0),
        jnp.stack(d_small["wga"], axis=0).reshape(-1, LANES), jnp.concatenate(d_small["bga"], axis=0),
        jnp.stack(d_small["wgx"], axis=0).reshape(-1, LANES), jnp.concatenate(d_small["bgx"], axis=0),
    ]
    rep_grads = [jnp.pad(g, ((0, -g.shape[0] % SUBLANES), (0, 0))) for g in rep_grads]
    all_small = _all_gather([small_grad] + rep_grads, name="ag_small_grads")
    slot_order = tuple(range(N_DEV))

    def adam_rep(terms, w, m, v, tag):
        r_pad, c_all = terms.shape[1], terms.shape[2]

        def prep(t):
            t2 = t.reshape(-1, c_all)
            return jnp.pad(t2, ((0, r_pad - t2.shape[0]), (0, 0)))

        outs = _adamw(prep(w), prep(m), prep(v), terms, slot_order, name=f"adamw_{tag}", tr=_adam_row_tile(r_pad, c_all))
        n_rows = w.size // c_all
        return [o[:n_rows].reshape(w.shape) for o in outs]

    small_w = pack_small(meta_tokens, lru_conv_w, lru_conv_b, lru_lambda)
    small_m = pack_small(m_meta_tokens, m_lru_conv_w, m_lru_conv_b, m_lru_lambda)
    small_v = pack_small(v_meta_tokens, v_lru_conv_w, v_lru_conv_b, v_lru_lambda)
    small_out = _adamw(small_w, small_m, small_v, all_small[0], slot_order, name="adamw_small", tr=small_pad, col_block=my_slot)
    small_out = [unpack_small(o) for o in small_out]
    for idx, key in enumerate(("meta_tokens", "lru_conv_w", "lru_conv_b", "lru_lambda")):
        res[key] = [small_out[k][idx] for k in range(4)]

    res["norm_mix"] = adam_rep(all_small[1], norm_mix, m_norm_mix, v_norm_mix, "norm_mix")
    res["norm_ffn"] = adam_rep(all_small[2], norm_ffn, m_norm_ffn, v_norm_ffn, "norm_ffn")
    res["norm_final"] = adam_rep(all_small[3], norm_final, m_norm_final, v_norm_final, "norm_final")
    res["mla_q_norm"] = adam_rep(all_small[4], mla_q_norm, m_mla_q_norm, v_mla_q_norm, "mla_q_norm")
    res["mla_kv_norm"] = adam_rep(all_small[5], mla_kv_norm, m_mla_kv_norm, v_mla_kv_norm, "mla_kv_norm")
    res["lru_w_gate_a"] = adam_rep(all_small[6], lru_w_gate_a, m_lru_w_gate_a, v_lru_w_gate_a, "lru_w_gate_a")
    res["lru_b_gate_a"] = adam_rep(all_small[7], lru_b_gate_a, m_lru_b_gate_a, v_lru_b_gate_a, "lru_b_gate_a")
    res["lru_w_gate_x"] = adam_rep(all_small[8], lru_w_gate_x, m_lru_w_gate_x, v_lru_w_gate_x, "lru_w_gate_x")
    res["lru_b_gate_x"] = adam_rep(all_small[9], lru_b_gate_x, m_lru_b_gate_x, v_lru_b_gate_x, "lru_b_gate_x")

    names = ["meta_tokens", "norm_mix", "norm_ffn", "norm_final", "mla_w_in", "mla_q_norm", "mla_kv_norm", "mla_w_uq",
             "mla_w_ukv", "mla_w_o", "lru_w_in", "lru_conv_w", "lru_conv_b", "lru_w_gate_a", "lru_b_gate_a", "lru_w_gate_x",
             "lru_b_gate_x", "lru_lambda", "lru_w_o", "ffn_w_gu", "ffn_w_down"]
    shapes = dict(meta_tokens=meta_tokens, norm_mix=norm_mix, norm_ffn=norm_ffn, norm_final=norm_final, mla_w_in=mla_w_in,
                  mla_q_norm=mla_q_norm, mla_kv_norm=mla_kv_norm, mla_w_uq=mla_w_uq, mla_w_ukv=mla_w_ukv, mla_w_o=mla_w_o,
                  lru_w_in=lru_w_in, lru_conv_w=lru_conv_w, lru_conv_b=lru_conv_b, lru_w_gate_a=lru_w_gate_a,
                  lru_b_gate_a=lru_b_gate_a, lru_w_gate_x=lru_w_gate_x, lru_b_gate_x=lru_b_gate_x, lru_lambda=lru_lambda,
                  lru_w_o=lru_w_o, ffn_w_gu=ffn_w_gu, ffn_w_down=ffn_w_down)
    outs = [loss, grad_x]
    for k in range(4):
        outs += [res[nm][k].reshape(shapes[nm].shape) for nm in names]
    return tuple(outs)
```

```python
import math

import jax
import jax.numpy as jnp
from jax import lax
from jax.experimental import pallas as pl
from jax.experimental.pallas import tpu as pltpu

F32 = jnp.float32
BF16 = jnp.bfloat16
MESH = pl.DeviceIdType.MESH

N_META = 16
CHUNK = 64
QK_NOPE = 128
QK_ROPE = 64
V_HEAD = 128
HEAD_W = 256
ROPE_THETA = 10000.0
LRU_C = 8.0
RMS_EPS = 1e-6
NEG_BIG = -1e30
ADAM_LR, ADAM_B1, ADAM_B2, ADAM_EPS, ADAM_WD, ADAM_STEP = 0.001, 0.9, 0.999, 1e-08, 0.01, 10

LANES = 128
SUBLANES = 8
VMEM_LIMIT_BYTES = 52 * 1024 * 1024
N_DEV = 8

_NT = (((1,), (1,)), ((), ()))
_TN = (((0,), (0,)), ((), ()))
_DIVS = (2048, 1024, 512, 256, 128)
_ROW_TILES = (1408, 1024, 512, 256, 128)


def _params(dims):
    return pltpu.CompilerParams(dimension_semantics=dims, vmem_limit_bytes=VMEM_LIMIT_BYTES)


def _pick(n, candidates):
    for c in candidates:
        if c <= n and n % c == 0:
            return c
    return n


def _sigmoid(z):
    return 1.0 / (1.0 + jnp.exp(-z))


def _gelu(x):
    c = math.sqrt(2.0 / math.pi)
    return 0.5 * x * (1.0 + jnp.tanh(c * (x + 0.044715 * x * x * x)))


def _gelu_grad(x):
    c = math.sqrt(2.0 / math.pi)
    th = jnp.tanh(c * (x + 0.044715 * x * x * x))
    return 0.5 * (1.0 + th) + 0.5 * x * (1.0 - th * th) * c * (1.0 + 3.0 * 0.044715 * x * x)


def _neg_expm1(x):
    poly = -x * (1.0 + x * (1.0 / 2.0) * (1.0 + x * (1.0 / 3.0) * (1.0 + x * (1.0 / 4.0) * (
        1.0 + x * (1.0 / 5.0) * (1.0 + x * (1.0 / 6.0) * (1.0 + x * (1.0 / 7.0)))))))
    return jnp.where(x > -0.25, poly, 1.0 - jnp.exp(x))


def _softplus_neg(lam):
    e = jnp.exp(-jnp.abs(lam))
    log1p = jnp.where(e > 1e-4, jnp.log(1.0 + e), e * (1.0 - e * (0.5 - e * (1.0 / 3.0))))
    return jnp.maximum(-lam, 0.0) + log1p


def _rot_half(x):
    lane = lax.broadcasted_iota(jnp.int32, x.shape, 1)
    first = (lane % QK_ROPE) < (QK_ROPE // 2)
    return jnp.where(first, -pltpu.roll(x, LANES - QK_ROPE // 2, 1), pltpu.roll(x, QK_ROPE // 2, 1))


def _rope(x, cos, sin):
    return x * cos + _rot_half(x) * sin


def _unrope(g, cos, sin):
    return g * cos - _rot_half(g) * sin


def _mm_nn(a, b, *, name, out_dtype, tm, tn, tk, b_blocked=False, res=None, epilogue=None, extras=()):
    m_all, k_all = a.shape
    if b_blocked:
        g_all, kb, nb = b.shape
        n_all = g_all * nb
        assert nb % tn == 0
        r = nb // tn
        b_spec = pl.BlockSpec((None, tk, tn), lambda j, i, k: (j // r, k, j % r))
    else:
        kb, n_all = b.shape
        b_spec = pl.BlockSpec((tk, tn), lambda j, i, k: (k, j))
    assert kb == k_all and m_all % tm == 0 and n_all % tn == 0 and k_all % tk == 0
    nm, nn, nk = m_all // tm, n_all // tn, k_all // tk
    in_specs = [pl.BlockSpec((tm, tk), lambda j, i, k: (i, k)), b_spec]
    operands = [a, b]
    has_res = res is not None
    if has_res:
        in_specs.append(pl.BlockSpec((tm, tn), lambda j, i, k: (i, j)))
        operands.append(res)
    for e in extras:
        in_specs.append(pl.BlockSpec((tm, e.shape[1]), lambda j, i, k: (i, 0)))
        operands.append(e)
    n_ex = len(extras)

    def body(*refs):
        a_ref, b_ref = refs[0], refs[1]
        pos = 2
        res_ref = None
        if has_res:
            res_ref = refs[pos]
            pos += 1
        ex_refs = refs[pos:pos + n_ex]
        pos += n_ex
        o_ref = refs[pos]
        acc_ref = refs[pos + 1] if nk > 1 else None

        def finish(acc):
            if has_res:
                acc = acc + res_ref[...]
            if epilogue is not None:
                acc = epilogue(acc, *ex_refs)
            o_ref[...] = acc.astype(o_ref.dtype)

        prod = jnp.dot(a_ref[...], b_ref[...], preferred_element_type=F32)
        if nk == 1:
            finish(prod)
        else:
            k = pl.program_id(2)

            @pl.when(k == 0)
            def _():
                acc_ref[...] = prod

            @pl.when(k > 0)
            def _():
                acc_ref[...] += prod

            @pl.when(k == nk - 1)
            def _():
                finish(acc_ref[...])

    return pl.pallas_call(
        body, name=name, grid=(nn, nm, nk), in_specs=in_specs,
        out_specs=pl.BlockSpec((tm, tn), lambda j, i, k: (i, j)),
        out_shape=jax.ShapeDtypeStruct((m_all, n_all), out_dtype),
        scratch_shapes=[pltpu.VMEM((tm, tn), F32)] if nk > 1 else [],
        compiler_params=_params(("parallel", "parallel", "arbitrary")),
    )(*operands)


def _mm_nt(a, b, *, name, out_dtype, tm, tn, tk, b_blocked=False):
    m_all, k_all = a.shape
    if b_blocked:
        g_all, n_all, nb = b.shape
        assert g_all * nb == k_all and nb % tk == 0
        r = nb // tk
        b_spec = pl.BlockSpec((None, tn, tk), lambda j, i, k: (k // r, j, k % r))
    else:
        n_all, kb = b.shape
        assert kb == k_all
        b_spec = pl.BlockSpec((tn, tk), lambda j, i, k: (j, k))
    assert m_all % tm == 0 and n_all % tn == 0 and k_all % tk == 0
    nm, nn, nk = m_all // tm, n_all // tn, k_all // tk

    def body(a_ref, b_ref, o_ref, *scratch):
        prod = lax.dot_general(a_ref[...], b_ref[...], _NT, preferred_element_type=F32)
        if nk == 1:
            o_ref[...] = prod.astype(o_ref.dtype)
        else:
            acc_ref = scratch[0]
            k = pl.program_id(2)

            @pl.when(k == 0)
            def _():
                acc_ref[...] = prod

            @pl.when(k > 0)
            def _():
                acc_ref[...] += prod

            @pl.when(k == nk - 1)
            def _():
                o_ref[...] = acc_ref[...].astype(o_ref.dtype)

    return pl.pallas_call(
        body, name=name, grid=(nn, nm, nk),
        in_specs=[pl.BlockSpec((tm, tk), lambda j, i, k: (i, k)), b_spec],
        out_specs=pl.BlockSpec((tm, tn), lambda j, i, k: (i, j)),
        out_shape=jax.ShapeDtypeStruct((m_all, n_all), out_dtype),
        scratch_shapes=[pltpu.VMEM((tm, tn), F32)] if nk > 1 else [],
        compiler_params=_params(("parallel", "parallel", "arbitrary")),
    )(a, b)


def _mm_tn(a, b, *, name, out_dtype, tm, tn, tk, out_block=None):
    t_all, m_all = a.shape
    tb, n_all = b.shape
    assert tb == t_all and m_all % tm == 0 and n_all % tn == 0 and t_all % tk == 0
    nm, nn, nk = m_all // tm, n_all // tn, t_all // tk
    if out_block is None:
        out_shape = jax.ShapeDtypeStruct((m_all, n_all), out_dtype)
        out_spec = pl.BlockSpec((tm, tn), lambda i, j, k: (i, j))
    else:
        assert out_block % tn == 0 and n_all % out_block == 0
        r = out_block // tn
        out_shape = jax.ShapeDtypeStruct((n_all // out_block, m_all, out_block), out_dtype)
        out_spec = pl.BlockSpec((None, tm, tn), lambda i, j, k: (j // r, i, j % r))

    def body(a_ref, b_ref, o_ref, *scratch):
        prod = lax.dot_general(a_ref[...], b_ref[...], _TN, preferred_element_type=F32)
        if nk == 1:
            o_ref[...] = prod.astype(o_ref.dtype)
        else:
            acc_ref = scratch[0]
            k = pl.program_id(2)

            @pl.when(k == 0)
            def _():
                acc_ref[...] = prod

            @pl.when(k > 0)
            def _():
                acc_ref[...] += prod

            @pl.when(k == nk - 1)
            def _():
                o_ref[...] = acc_ref[...].astype(o_ref.dtype)

    return pl.pallas_call(
        body, name=name, grid=(nm, nn, nk),
        in_specs=[pl.BlockSpec((tk, tm), lambda i, j, k: (k, i)), pl.BlockSpec((tk, tn), lambda i, j, k: (k, j))],
        out_specs=out_spec, out_shape=out_shape,
        scratch_shapes=[pltpu.VMEM((tm, tn), F32)] if nk > 1 else [],
        compiler_params=_params(("parallel", "parallel", "arbitrary")),
    )(a, b)


def _rmsnorm_fwd(x, g, *, name, tm):
    t_all, d = x.shape

    def body(x_ref, g_ref, o_ref):
        xv = x_ref[...]
        rstd = lax.rsqrt(jnp.mean(xv * xv, axis=-1, keepdims=True) + RMS_EPS)
        o_ref[...] = (xv * rstd * g_ref[...]).astype(o_ref.dtype)

    return pl.pallas_call(
        body, name=name, grid=(t_all // tm,),
        in_specs=[pl.BlockSpec((tm, d), lambda i: (i, 0)), pl.BlockSpec((1, d), lambda i: (0, 0))],
        out_specs=pl.BlockSpec((tm, d), lambda i: (i, 0)),
        out_shape=jax.ShapeDtypeStruct((t_all, d), BF16),
        compiler_params=_params(("parallel",)),
    )(x, g)


def _rms_bwd_math(dy, xv, g):
    rstd = lax.rsqrt(jnp.mean(xv * xv, axis=-1, keepdims=True) + RMS_EPS)
    xhat = xv * rstd
    dxh = dy * g
    dx = rstd * (dxh - xhat * jnp.mean(dxh * xhat, axis=-1, keepdims=True))
    return dx, jnp.sum(dy * xhat, axis=0, keepdims=True)


def _rmsnorm_bwd(dy, x, g, res, *, name, tm):
    t_all, d = x.shape

    def body(dy_ref, x_ref, g_ref, res_ref, dx_ref, dxb_ref, dg_ref):
        dx, dg = _rms_bwd_math(dy_ref[...], x_ref[...], g_ref[...])
        tot = res_ref[...] + dx
        dx_ref[...] = tot
        dxb_ref[...] = tot.astype(BF16)

        @pl.when(pl.program_id(0) == 0)
        def _():
            dg_ref[...] = dg

        @pl.when(pl.program_id(0) > 0)
        def _():
            dg_ref[...] += dg

    row = pl.BlockSpec((tm, d), lambda i: (i, 0))
    vec = pl.BlockSpec((1, d), lambda i: (0, 0))
    return pl.pallas_call(
        body, name=name, grid=(t_all // tm,),
        in_specs=[row, row, vec, row], out_specs=[row, row, vec],
        out_shape=[jax.ShapeDtypeStruct((t_all, d), F32), jax.ShapeDtypeStruct((t_all, d), BF16),
                   jax.ShapeDtypeStruct((1, d), F32)],
        compiler_params=_params(("arbitrary",)),
    )(dy, x, g, res)


def _loss_head(h, target, g, *, name, tm, n_real):
    t_all, d = h.shape

    def body(h_ref, t_ref, g_ref, loss_ref, dx_ref, dxb_ref, dg_ref):
        i = pl.program_id(0)
        xv = h_ref[...]
        gv = g_ref[...]
        rstd = lax.rsqrt(jnp.mean(xv * xv, axis=-1, keepdims=True) + RMS_EPS)
        y = xv * rstd * gv
        row = i * tm + lax.broadcasted_iota(jnp.int32, (tm, 1), 0)
        valid = (row >= N_META) & (row < n_real)
        err = jnp.where(valid, y - t_ref[...], 0.0)
        part = 0.5 * jnp.sum(jnp.mean(err * err, axis=-1, keepdims=True), axis=0, keepdims=True)
        dx, dg = _rms_bwd_math(err * (1.0 / d), xv, gv)
        dx_ref[...] = dx
        dxb_ref[...] = dx.astype(BF16)

        @pl.when(i == 0)
        def _():
            dg_ref[...] = dg
            loss_ref[...] = jnp.broadcast_to(part, loss_ref.shape)

        @pl.when(i > 0)
        def _():
            dg_ref[...] += dg
            loss_ref[...] += jnp.broadcast_to(part, loss_ref.shape)

    row = pl.BlockSpec((tm, d), lambda i: (i, 0))
    vec = pl.BlockSpec((1, d), lambda i: (0, 0))
    return pl.pallas_call(
        body, name=name, grid=(t_all // tm,),
        in_specs=[row, row, vec],
        out_specs=[pl.BlockSpec((1, LANES), lambda i: (0, 0)), row, row, vec],
        out_shape=[jax.ShapeDtypeStruct((1, LANES), F32), jax.ShapeDtypeStruct((t_all, d), F32),
                   jax.ShapeDtypeStruct((t_all, d), BF16), jax.ShapeDtypeStruct((1, d), F32)],
        compiler_params=_params(("arbitrary",)),
    )(h, target, g)


def _swiglu_fwd(gu, *, name, tm, tc):
    t_all, f2 = gu.shape
    f = f2 // 2
    nc = f // tc

    def body(g_ref, u_ref, o_ref):
        gv = g_ref[...]
        o_ref[...] = (gv * _sigmoid(gv) * u_ref[...]).astype(o_ref.dtype)

    return pl.pallas_call(
        body, name=name, grid=(t_all // tm, nc),
        in_specs=[pl.BlockSpec((tm, tc), lambda i, j: (i, j)), pl.BlockSpec((tm, tc), lambda i, j: (i, nc + j))],
        out_specs=pl.BlockSpec((tm, tc), lambda i, j: (i, j)),
        out_shape=jax.ShapeDtypeStruct((t_all, f), BF16),
        compiler_params=_params(("parallel", "parallel")),
    )(gu, gu)


def _swiglu_bwd(dact, gu, *, name, tm, tc):
    t_all, f2 = gu.shape
    f = f2 // 2
    nc = f // tc

    def body(d_ref, g_ref, u_ref, o_ref):
        jj = pl.program_id(1)
        gv = g_ref[...]
        sg = _sigmoid(gv)
        dv = d_ref[...]

        @pl.when(jj < nc)
        def _():
            o_ref[...] = (dv * u_ref[...] * (sg * (1.0 + gv * (1.0 - sg)))).astype(o_ref.dtype)

        @pl.when(jj >= nc)
        def _():
            o_ref[...] = (dv * gv * sg).astype(o_ref.dtype)

    return pl.pallas_call(
        body, name=name, grid=(t_all // tm, 2 * nc),
        in_specs=[pl.BlockSpec((tm, tc), lambda i, j: (i, j % nc)), pl.BlockSpec((tm, tc), lambda i, j: (i, j % nc)),
                  pl.BlockSpec((tm, tc), lambda i, j: (i, nc + j % nc))],
        out_specs=pl.BlockSpec((tm, tc), lambda i, j: (i, j)),
        out_shape=jax.ShapeDtypeStruct((t_all, f2), BF16),
        compiler_params=_params(("parallel", "parallel")),
    )(dact, gu, gu)


def _mla_prep_fwd(proj, qn, kvn, cos, sin, *, name, tm, lq, lkv):
    t_all, w = proj.shape

    def body(p_ref, qn_ref, kvn_ref, cos_ref, sin_ref, cq_ref, ckv_ref, kr_ref):
        pv = p_ref[...]
        xq = pv[:, :lq]
        xkv = pv[:, lq:lq + lkv]
        cq_ref[...] = (xq * lax.rsqrt(jnp.mean(xq * xq, axis=-1, keepdims=True) + RMS_EPS) * qn_ref[...]).astype(BF16)
        ckv_ref[...] = (xkv * lax.rsqrt(jnp.mean(xkv * xkv, axis=-1, keepdims=True) + RMS_EPS) * kvn_ref[...]).astype(BF16)
        kr_ref[...] = _rope(pv[:, lq + lkv:], cos_ref[...], sin_ref[...]).astype(BF16)

    def row(width):
        return pl.BlockSpec((tm, width), lambda i: (i, 0))

    def vec(width):
        return pl.BlockSpec((1, width), lambda i: (0, 0))

    return pl.pallas_call(
        body, name=name, grid=(t_all // tm,),
        in_specs=[row(w), vec(lq), vec(lkv), row(LANES), row(LANES)],
        out_specs=[row(lq), row(lkv), row(LANES)],
        out_shape=[jax.ShapeDtypeStruct((t_all, lq), BF16), jax.ShapeDtypeStruct((t_all, lkv), BF16),
                   jax.ShapeDtypeStruct((t_all, LANES), BF16)],
        compiler_params=_params(("parallel",)),
    )(proj, qn, kvn, cos, sin)


def _mla_prep_bwd(dcq, dckv, dkr_h, proj, qn, kvn, cos, sin, *, name, tm, lq, lkv):
    t_all, w = proj.shape
    n_heads = dkr_h.shape[0]

    def body(dcq_ref, dckv_ref, dkr_ref, p_ref, qn_ref, kvn_ref, cos_ref, sin_ref, dp_ref, dqn_ref, dkvn_ref):
        pv = p_ref[...]
        dxq, dqn = _rms_bwd_math(dcq_ref[...], pv[:, :lq], qn_ref[...])
        dxkv, dkvn = _rms_bwd_math(dckv_ref[...], pv[:, lq:lq + lkv], kvn_ref[...])
        dkr = dkr_ref[0]
        for hh in range(1, n_heads):
            dkr = dkr + dkr_ref[hh]
        dkr = _unrope(dkr, cos_ref[...], sin_ref[...])
        dp_ref[...] = jnp.concatenate([dxq, dxkv, dkr], axis=1).astype(BF16)

        @pl.when(pl.program_id(0) == 0)
        def _():
            dqn_ref[...] = dqn
            dkvn_ref[...] = dkvn

        @pl.when(pl.program_id(0) > 0)
        def _():
            dqn_ref[...] += dqn
            dkvn_ref[...] += dkvn

    def row(width):
        return pl.BlockSpec((tm, width), lambda i: (i, 0))

    def vec(width):
        return pl.BlockSpec((1, width), lambda i: (0, 0))

    return pl.pallas_call(
        body, name=name, grid=(t_all // tm,),
        in_specs=[row(lq), row(lkv), pl.BlockSpec((n_heads, tm, LANES), lambda i: (0, i, 0)), row(w),
                  vec(lq), vec(lkv), row(LANES), row(LANES)],
        out_specs=[row(w), vec(lq), vec(lkv)],
        out_shape=[jax.ShapeDtypeStruct((t_all, w), BF16), jax.ShapeDtypeStruct((1, lq), F32),
                   jax.ShapeDtypeStruct((1, lkv), F32)],
        compiler_params=_params(("arbitrary",)),
    )(dcq, dckv, dkr_h, proj, qn, kvn, cos, sin)


def _rope_q_epilogue(acc, cos_ref, sin_ref):
    parts = []
    for g in range(acc.shape[1] // LANES):
        blk = acc[:, g * LANES:(g + 1) * LANES]
        parts.append(_rope(blk, cos_ref[...], sin_ref[...]) if g % 2 == 1 else blk)
    return jnp.concatenate(parts, axis=1)


def _chunk_causal(rows, cols, row0=0):
    r = row0 + lax.broadcasted_iota(jnp.int32, (rows, cols), 0)
    c = lax.broadcasted_iota(jnp.int32, (rows, cols), 1)
    return (c >> 6) <= (r >> 6)


def _meta_keys(rows, cols):
    return lax.broadcasted_iota(jnp.int32, (rows, cols), 1) < N_META


def _attn_fwd(q, kv, kr, *, name, n_heads, tq, n_real, scale):
    t_all = q.shape[0]
    nq = (n_real - N_META) // tq
    assert N_META + nq * tq == n_real and tq % CHUNK == 0 and t_all >= LANES
    n_pad = t_all - n_real
    sub = tq // 2 if (tq // 2) % CHUNK == 0 else tq

    def body(q_ref, kv_ref, kr_ref, o_ref, lse_ref, k_scr, m_scr, l_scr, acc_scr):
        k_scr[:, :QK_NOPE] = kv_ref[:, :QK_NOPE]
        k_scr[:, QK_NOPE:] = kr_ref[...]
        if n_pad:
            o_ref[pl.ds(n_real, n_pad), :] = jnp.zeros((n_pad, V_HEAD), o_ref.dtype)
            lse_ref[pl.ds(n_real, n_pad), :] = jnp.zeros((n_pad, LANES), F32)

        def scores(qt, c0, width):
            return lax.dot_general(qt, k_scr[pl.ds(c0, width), :], _NT, preferred_element_type=F32) * scale

        def values(c0, width):
            return kv_ref[pl.ds(c0, width), QK_NOPE:]

        s = jnp.where(_meta_keys(LANES, LANES), scores(q_ref[pl.ds(0, LANES), :], 0, LANES), NEG_BIG)
        m = jnp.max(s, axis=-1, keepdims=True)
        p = jnp.exp(s - m)
        l = jnp.sum(p, axis=-1, keepdims=True)
        o_meta = jnp.dot(p.astype(BF16), values(0, LANES), preferred_element_type=F32) / l
        o_ref[pl.ds(0, N_META), :] = o_meta[:N_META].astype(o_ref.dtype)
        lse_ref[pl.ds(0, N_META), :] = jnp.broadcast_to((m + jnp.log(l))[:N_META], (N_META, LANES))

        parts = [(u * sub, sub) for u in range(tq // sub)]

        def accumulate(u0, s, c0, width):
            rows = pl.ds(u0, s.shape[0])
            m_prev = m_scr[rows, :]
            m_new = jnp.maximum(m_prev, jnp.max(s, axis=-1, keepdims=True))
            alpha = jnp.exp(m_prev - m_new)
            p = jnp.exp(s - m_new)
            l_scr[rows, :] = alpha * l_scr[rows, :] + jnp.sum(p, axis=-1, keepdims=True)
            acc_scr[rows, :] = alpha * acc_scr[rows, :] + jnp.dot(p.astype(BF16), values(c0, width), preferred_element_type=F32)
            m_scr[rows, :] = m_new

        def q_tile(i, carry):
            r0 = pl.multiple_of(N_META + i * tq, N_META)
            qts = [q_ref[pl.ds(r0 + u0, rows), :] for u0, rows in parts]
            for (u0, rows), qt in zip(parts, qts):
                s = jnp.where(_meta_keys(rows, LANES), scores(qt, 0, LANES), NEG_BIG)
                m = jnp.max(s, axis=-1, keepdims=True)
                p = jnp.exp(s - m)
                m_scr[pl.ds(u0, rows), :] = m
                l_scr[pl.ds(u0, rows), :] = jnp.sum(p, axis=-1, keepdims=True)
                acc_scr[pl.ds(u0, rows), :] = jnp.dot(p.astype(BF16), values(0, LANES), preferred_element_type=F32)

            def full_block(j, c):
                c0 = pl.multiple_of(N_META + j * tq, N_META)
                for (u0, _), qt in zip(parts, qts):
                    accumulate(u0, scores(qt, c0, tq), c0, tq)
                return c

            lax.fori_loop(0, i, full_block, 0)
            for (u0, rows), qt in zip(parts, qts):
                width = u0 + rows
                accumulate(u0, jnp.where(_chunk_causal(rows, width, u0), scores(qt, r0, width), NEG_BIG), r0, width)
            o_ref[pl.ds(r0, tq), :] = (acc_scr[...] / l_scr[...]).astype(o_ref.dtype)
            lse_ref[pl.ds(r0, tq), :] = jnp.broadcast_to(m_scr[...] + jnp.log(l_scr[...]), (tq, LANES))
            return carry

        lax.fori_loop(0, nq, q_tile, 0)

    def head(width):
        return pl.BlockSpec((t_all, width), lambda h: (0, h))

    return pl.pallas_call(
        body, name=name, grid=(n_heads,),
        in_specs=[head(HEAD_W), head(HEAD_W), pl.BlockSpec((t_all, LANES), lambda h: (0, 0))],
        out_specs=[head(V_HEAD), pl.BlockSpec((None, t_all, LANES), lambda h: (h, 0, 0))],
        out_shape=[jax.ShapeDtypeStruct((t_all, n_heads * V_HEAD), BF16),
                   jax.ShapeDtypeStruct((n_heads, t_all, LANES), F32)],
        scratch_shapes=[pltpu.VMEM((t_all, HEAD_W), BF16), pltpu.VMEM((tq, 1), F32), pltpu.VMEM((tq, 1), F32),
                        pltpu.VMEM((tq, V_HEAD), F32)],
        compiler_params=_params(("parallel",)),
    )(q, kv, kr)


def _attn_bwd(q, kv, kr, o, lse, do, cos, sin, *, name, n_heads, tq, n_real, scale):
    t_all = q.shape[0]
    nq = (n_real - N_META) // tq
    assert N_META + nq * tq == n_real and tq % CHUNK == 0 and t_all >= LANES
    n_pad = t_all - n_real

    def body(q_ref, kv_ref, kr_ref, o_ref, lse_ref, do_ref, cos_ref, sin_ref, dq_ref, dkv_ref, dkr_ref,
             k_scr, dk_scr, dv_scr, dq_scr):
        k_scr[:, :QK_NOPE] = kv_ref[:, :QK_NOPE]
        k_scr[:, QK_NOPE:] = kr_ref[...]
        dk_scr[...] = jnp.zeros(dk_scr.shape, F32)
        dv_scr[...] = jnp.zeros(dv_scr.shape, F32)
        if n_pad:
            dq_ref[pl.ds(n_real, n_pad), :] = jnp.zeros((n_pad, HEAD_W), dq_ref.dtype)

        def block(qt, dot, lse_t, delta, c0, width, mask):
            kb = k_scr[pl.ds(c0, width), :]
            s = lax.dot_general(qt, kb, _NT, preferred_element_type=F32) * scale
            p = jnp.exp(s - lse_t)
            if mask is not None:
                p = jnp.where(mask, p, 0.0)
            dp = lax.dot_general(dot, kv_ref[pl.ds(c0, width), QK_NOPE:], _NT, preferred_element_type=F32)
            ds = (p * (dp - delta) * scale).astype(BF16)
            dv_scr[pl.ds(c0, width), :] += lax.dot_general(p.astype(BF16), dot, _TN, preferred_element_type=F32)
            dk_scr[pl.ds(c0, width), :] += lax.dot_general(ds, qt, _TN, preferred_element_type=F32)
            return jnp.dot(ds, kb, preferred_element_type=F32)

        def write_dq(r0, rows, dq):
            cs, sn = cos_ref[pl.ds(r0, rows), :], sin_ref[pl.ds(r0, rows), :]
            dq_ref[pl.ds(r0, rows), :] = jnp.concatenate(
                [dq[:, :QK_NOPE], _unrope(dq[:, QK_NOPE:], cs, sn)], axis=1).astype(dq_ref.dtype)

        rows_m = lax.broadcasted_iota(jnp.int32, (LANES, LANES), 0) < N_META
        dot = do_ref[pl.ds(0, LANES), :]
        delta = jnp.sum(dot.astype(F32) * o_ref[pl.ds(0, LANES), :].astype(F32), axis=-1, keepdims=True)
        dq = block(q_ref[pl.ds(0, LANES), :], dot, lse_ref[pl.ds(0, LANES), :1], delta, 0, LANES,
                   _meta_keys(LANES, LANES) & rows_m)
        write_dq(0, N_META, dq[:N_META])

        def q_tile(i, carry):
            r0 = pl.multiple_of(N_META + i * tq, N_META)
            qt = q_ref[pl.ds(r0, tq), :]
            dot = do_ref[pl.ds(r0, tq), :]
            lse_t = lse_ref[pl.ds(r0, tq), :1]
            delta = jnp.sum(dot.astype(F32) * o_ref[pl.ds(r0, tq), :].astype(F32), axis=-1, keepdims=True)
            dq_scr[...] = block(qt, dot, lse_t, delta, 0, LANES, _meta_keys(tq, LANES))

            def full_block(j, c):
                c0 = pl.multiple_of(N_META + j * tq, N_META)
                dq_scr[...] += block(qt, dot, lse_t, delta, c0, tq, None)
                return c

            lax.fori_loop(0, i, full_block, 0)
            dq_scr[...] += block(qt, dot, lse_t, delta, r0, tq, _chunk_causal(tq, tq))
            write_dq(r0, tq, dq_scr[...])
            return carry

        lax.fori_loop(0, nq, q_tile, 0)
        dk = dk_scr[...]
        dkv_ref[...] = jnp.concatenate([dk[:, :QK_NOPE], dv_scr[...]], axis=1).astype(dkv_ref.dtype)
        dkr_ref[...] = dk[:, QK_NOPE:]

    def head(width):
        return pl.BlockSpec((t_all, width), lambda h: (0, h))

    table = pl.BlockSpec((t_all, LANES), lambda h: (0, 0))
    per_head = pl.BlockSpec((None, t_all, LANES), lambda h: (h, 0, 0))
    return pl.pallas_call(
        body, name=name, grid=(n_heads,),
        in_specs=[head(HEAD_W), head(HEAD_W), table, head(V_HEAD), per_head, head(V_HEAD), table, table],
        out_specs=[head(HEAD_W), head(HEAD_W), per_head],
        out_shape=[jax.ShapeDtypeStruct((t_all, n_heads * HEAD_W), BF16), jax.ShapeDtypeStruct((t_all, n_heads * HEAD_W), BF16),
                   jax.ShapeDtypeStruct((n_heads, t_all, LANES), F32)],
        scratch_shapes=[pltpu.VMEM((t_all, HEAD_W), BF16), pltpu.VMEM((t_all, HEAD_W), F32), pltpu.VMEM((t_all, V_HEAD), F32),
                        pltpu.VMEM((tq, HEAD_W), F32)],
        compiler_params=_params(("parallel",)),
    )(q, kv, kr, o, lse, do, cos, sin)


LRU_ROWS = 128


def _shifted_back(ref, t0, rows, shift_max):
    main = ref[pl.ds(t0, rows), :]
    prev = ref[pl.ds(pl.multiple_of(jnp.maximum(t0 - SUBLANES, 0), SUBLANES), SUBLANES), :]
    prev = jnp.where(t0 > 0, prev, 0.0)
    ext = jnp.concatenate([prev, main], axis=0)
    return [main] + [pltpu.roll(ext, s, 0)[SUBLANES:, :] for s in range(1, shift_max + 1)]


def _shifted_ahead(ref, t0, rows, t_all, shift_max):
    main = ref[pl.ds(t0, rows), :]
    nxt = ref[pl.ds(pl.multiple_of(jnp.minimum(t0 + rows, t_all - SUBLANES), SUBLANES), SUBLANES), :]
    nxt = jnp.where(t0 + rows < t_all, nxt, 0.0)
    ext = jnp.concatenate([main, nxt], axis=0)
    return [main] + [pltpu.roll(ext, rows + SUBLANES - s, 0)[:rows, :] for s in range(1, shift_max + 1)]


def _conv_fwd(xp_ref, t0, rows, cw, cb):
    sh = _shifted_back(xp_ref, t0, rows, 3)
    out = cb + cw[3:4, :] * sh[0]
    for k in range(3):
        out = out + cw[k:k + 1, :] * sh[3 - k]
    return out, sh


def _lru_gates(xb, wga, bga, wgx, bgx, sp):
    xbb = xb.astype(BF16)
    r = _sigmoid(jnp.dot(xbb, wga, preferred_element_type=F32) + bga)
    ig = _sigmoid(jnp.dot(xbb, wgx, preferred_element_type=F32) + bgx)
    la = -LRU_C * r * sp
    a = jnp.exp(la)
    s = jnp.sqrt(_neg_expm1(2.0 * la))
    return xbb, r, ig, a, s


def _scan_tile(a, b, reverse):
    rows = a.shape[0]
    ridx = lax.broadcasted_iota(jnp.int32, a.shape, 0)
    s = 1
    while s < rows:
        if reverse:
            keep = ridx < rows - s
            a_sh, b_sh = pltpu.roll(a, rows - s, 0), pltpu.roll(b, rows - s, 0)
        else:
            keep = ridx >= s
            a_sh, b_sh = pltpu.roll(a, s, 0), pltpu.roll(b, s, 0)
        b = jnp.where(keep, a * b_sh + b, b)
        a = jnp.where(keep, a * a_sh, a)
        s *= 2
    return a, b


def _lru_fwd(xy, conv_w, conv_b, wga, bga, wgx, bgx, lam, *, name):
    t_all = xy.shape[0]
    dr = xy.shape[1] // 2
    c = LANES
    nblk = dr // c
    rows = LRU_ROWS
    nt = t_all // rows

    def body(xp_ref, yp_ref, cw_ref, cb_ref, wga_ref, bga_ref, wgx_ref, bgx_ref, lam_ref, hs_ref, hsy_ref):
        cw, cb = cw_ref[...], cb_ref[...]
        sp = _softplus_neg(lam_ref[...])

        def tile(t, h_in):
            t0 = pl.multiple_of(t * rows, rows)
            xb, _ = _conv_fwd(xp_ref, t0, rows, cw, cb)
            _, _, ig, a, s = _lru_gates(xb, wga_ref[0], bga_ref[...], wgx_ref[0], bgx_ref[...], sp)
            cum_a, h0 = _scan_tile(a, s * (ig * xb), reverse=False)
            hs = cum_a * h_in + h0
            hs_ref[pl.ds(t0, rows), :] = hs
            hsy_ref[pl.ds(t0, rows), :] = (hs * _gelu(yp_ref[pl.ds(t0, rows), :])).astype(BF16)
            return hs[rows - 1:, :]

        lax.fori_loop(0, nt, tile, jnp.zeros((1, c), F32))

    col = pl.BlockSpec((t_all, c), lambda b: (0, b))
    vec = pl.BlockSpec((1, c), lambda b: (0, b))
    wsp = pl.BlockSpec((1, c, c), lambda b: (b, 0, 0))
    return pl.pallas_call(
        body, name=name, grid=(nblk,),
        in_specs=[col, pl.BlockSpec((t_all, c), lambda b: (0, nblk + b)), pl.BlockSpec((4, c), lambda b: (0, b)), vec,
                  wsp, vec, wsp, vec, vec],
        out_specs=[col, col],
        out_shape=[jax.ShapeDtypeStruct((t_all, dr), F32), jax.ShapeDtypeStruct((t_all, dr), BF16)],
        compiler_params=_params(("parallel",)),
    )(xy, xy, conv_w, conv_b, wga, bga, wgx, bgx, lam)


def _lru_bwd(xy, hs, dhsy, conv_w, conv_b, wga, bga, wgx, bgx, lam, *, name):
    t_all = xy.shape[0]
    dr = xy.shape[1] // 2
    c = LANES
    nblk = dr // c
    rows = LRU_ROWS
    nt = t_all // rows

    def body(xp_ref, yp_ref, hs_ref, dh_ref, cw_ref, cb_ref, wga_ref, bga_ref, wgx_ref, bgx_ref, lam_ref,
             dxp_ref, dyp_ref, dcw_ref, dcb_ref, dwga_ref, dbga_ref, dwgx_ref, dbgx_ref, dlam_ref,
             xb_scr, r_scr, i_scr, a_scr):
        cw, cb = cw_ref[...], cb_ref[...]
        lamv = lam_ref[...]
        sp = _softplus_neg(lamv)
        sig_neg = 1.0 / (1.0 + jnp.exp(lamv))
        wga_v, wgx_v = wga_ref[0], wgx_ref[0]

        def recompute(t, carry):
            t0 = pl.multiple_of(t * rows, rows)
            xb, _ = _conv_fwd(xp_ref, t0, rows, cw, cb)
            _, r, ig, a, _ = _lru_gates(xb, wga_v, bga_ref[...], wgx_v, bgx_ref[...], sp)
            xb_scr[pl.ds(t0, rows), :] = xb
            r_scr[pl.ds(t0, rows), :] = r
            i_scr[pl.ds(t0, rows), :] = ig
            a_scr[pl.ds(t0, rows), :] = a
            return carry

        lax.fori_loop(0, nt, recompute, 0)
        dwga_ref[...] = jnp.zeros(dwga_ref.shape, F32)
        dwgx_ref[...] = jnp.zeros(dwgx_ref.shape, F32)

        def tile(ti, carry):
            lam_in, dbga, dbgx, dlam, dcw, dcb = carry
            t = nt - 1 - ti
            t0 = pl.multiple_of(t * rows, rows)
            a_now, a_next = _shifted_ahead(a_scr, t0, rows, t_all, 1)
            yp = yp_ref[pl.ds(t0, rows), :]
            dhy = dh_ref[pl.ds(t0, rows), :]
            cum_a, lam0 = _scan_tile(a_next, dhy * _gelu(yp), reverse=True)
            lam_t = cum_a * lam_in + lam0
            hs_now, hs_prev = _shifted_back(hs_ref, t0, rows, 1)
            da = lam_t * hs_prev
            xb = xb_scr[pl.ds(t0, rows), :]
            r = r_scr[pl.ds(t0, rows), :]
            ig = i_scr[pl.ds(t0, rows), :]
            la = -LRU_C * r * sp
            s = jnp.sqrt(_neg_expm1(2.0 * la))
            d_ixb = lam_t * s
            dla = da * a_now - (lam_t * ig * xb) * (a_now * a_now / s)
            dzr = dla * (-LRU_C * sp) * r * (1.0 - r)
            dzi = d_ixb * xb * ig * (1.0 - ig)
            dzr_b, dzi_b = dzr.astype(BF16), dzi.astype(BF16)
            xbb = xb.astype(BF16)
            dwga_ref[0] += lax.dot_general(xbb, dzr_b, _TN, preferred_element_type=F32)
            dwgx_ref[0] += lax.dot_general(xbb, dzi_b, _TN, preferred_element_type=F32)
            dxb = (d_ixb * ig + lax.dot_general(dzr_b, wga_v, _NT, preferred_element_type=F32)
                   + lax.dot_general(dzi_b, wgx_v, _NT, preferred_element_type=F32))
            xb_scr[pl.ds(t0, rows), :] = dxb
            dyp_ref[pl.ds(t0, rows), :] = (dhy * hs_now * _gelu_grad(yp)).astype(BF16)
            ahead = _shifted_ahead(xb_scr, t0, rows, t_all, 3)
            dxp = cw[3:4, :] * ahead[0]
            for k in range(3):
                dxp = dxp + cw[k:k + 1, :] * ahead[3 - k]
            dxp_ref[pl.ds(t0, rows), :] = dxp.astype(BF16)
            back = _shifted_back(xp_ref, t0, rows, 3)
            dcw_t = jnp.concatenate([jnp.sum(dxb * back[3 - k], axis=0, keepdims=True) for k in range(4)], axis=0)
            return (lam_t[:1, :], dbga + jnp.sum(dzr, axis=0, keepdims=True), dbgx + jnp.sum(dzi, axis=0, keepdims=True),
                    dlam + jnp.sum(dla * r, axis=0, keepdims=True), dcw + dcw_t, dcb + jnp.sum(dxb, axis=0, keepdims=True))

        zero = jnp.zeros((1, c), F32)
        _, dbga, dbgx, dlam, dcw, dcb = lax.fori_loop(0, nt, tile, (zero, zero, zero, zero, jnp.zeros((4, c), F32), zero))
        dbga_ref[...] = dbga
        dbgx_ref[...] = dbgx
        dlam_ref[...] = dlam * (LRU_C * sig_neg)
        dcw_ref[...] = dcw
        dcb_ref[...] = dcb

    col = pl.BlockSpec((t_all, c), lambda b: (0, b))
    col2 = pl.BlockSpec((t_all, c), lambda b: (0, nblk + b))
    vec = pl.BlockSpec((1, c), lambda b: (0, b))
    tap = pl.BlockSpec((4, c), lambda b: (0, b))
    wsp = pl.BlockSpec((1, c, c), lambda b: (b, 0, 0))
    vshape = jax.ShapeDtypeStruct((1, dr), F32)
    wshape = jax.ShapeDtypeStruct((nblk, c, c), F32)
    dxp, dyp, dcw, dcb, dwga, dbga, dwgx, dbgx, dlam = pl.pallas_call(
        body, name=name, grid=(nblk,),
        in_specs=[col, col2, col, col, tap, vec, wsp, vec, wsp, vec, vec],
        out_specs=[col, col, tap, vec, wsp, vec, wsp, vec, vec],
        out_shape=[jax.ShapeDtypeStruct((t_all, dr), BF16), jax.ShapeDtypeStruct((t_all, dr), BF16),
                   jax.ShapeDtypeStruct((4, dr), F32), vshape, wshape, vshape, wshape, vshape, vshape],
        scratch_shapes=[pltpu.VMEM((t_all, c), F32)] * 4,
        compiler_params=_params(("parallel",)),
    )(xy, xy, hs, dhsy, conv_w, conv_b, wga, bga, wgx, bgx, lam)
    return jnp.concatenate([dxp, dyp], axis=1), dcw, dcb, dwga, dbga, dwgx, dbgx, dlam


def _mesh_pos():
    return lax.axis_index("x"), lax.axis_index("y"), lax.axis_index("c")


def _all_gather(shards, *, name):
    n = len(shards)

    def body(*refs):
        ins, outs, token = refs[:n], refs[n:2 * n], refs[2 * n]
        send_sems, recv_sems, local_sems = refs[2 * n + 1:]
        token[...] = jnp.zeros(token.shape, token.dtype)
        x, y, c = _mesh_pos()
        me, sibling = (x, y, c), (x, y, 1 - c)
        chips = [(1 - x, y), (x, 1 - y), (1 - x, 1 - y)]
        slot = _slot

        def copy(a, k, block, to, src=None):
            dst = outs[a].at[slot(block)]
            return pltpu.make_async_remote_copy(
                src_ref=dst if src is None else src, dst_ref=dst, send_sem=send_sems.at[a, k],
                recv_sem=recv_sems.at[a, k], device_id=to, device_id_type=MESH)

        mine = [pltpu.make_async_copy(ins[a], outs[a].at[slot(me)], local_sems.at[a]) for a in range(n)]
        for cp in mine:
            cp.start()
        first = []
        for a in range(n):
            first.append(copy(a, 0, me, sibling, src=ins[a]))
            first += [copy(a, 1 + j, me, (*chip, c), src=ins[a]) for j, chip in enumerate(chips)]
        for cp in first:
            cp.start()
        passed = []
        for a in range(n):
            for j, chip in enumerate(chips):
                copy(a, 1 + j, (*chip, c), me).wait_recv()
                fwd = copy(a, 4 + j, (*chip, c), sibling)
                fwd.start()
                passed.append(fwd)
        for a in range(n):
            copy(a, 0, sibling, me).wait_recv()
            for j, chip in enumerate(chips):
                copy(a, 4 + j, (*chip, 1 - c), me).wait_recv()
        for cp in first + passed:
            cp.wait_send()
        for cp in mine:
            cp.wait()

    any_spec = pl.BlockSpec(memory_space=pl.ANY)
    outs = pl.pallas_call(
        body, name=name,
        in_specs=[any_spec] * n, out_specs=[any_spec] * n + [pl.BlockSpec(memory_space=pltpu.VMEM)],
        out_shape=[jax.ShapeDtypeStruct((N_DEV,) + s.shape, s.dtype) for s in shards]
        + [jax.ShapeDtypeStruct((SUBLANES, LANES), F32)],
        scratch_shapes=[pltpu.SemaphoreType.DMA((n, 7)), pltpu.SemaphoreType.DMA((n, 7)), pltpu.SemaphoreType.DMA((n,))],
    )(*shards)
    return list(outs[:n]), outs[n][0, 0]


_HBM = pl.BlockSpec(memory_space=pltpu.HBM)
_SEM = pl.BlockSpec(memory_space=pltpu.SEMAPHORE)
_ANY = pl.BlockSpec(memory_space=pl.ANY)
_EFFECT = pltpu.SideEffectType.DATAFLOW_SIDE_EFFECTING


def _slot(p):
    return 4 * p[0] + 2 * p[1] + p[2]


def _remote(src, dst, send, recv, idx, to):
    return pltpu.make_async_remote_copy(src_ref=src, dst_ref=dst, send_sem=send.at[idx], recv_sem=recv.at[idx],
                                        device_id=to, device_id_type=MESH)


def _ag_plan_own(a, src, land, send, recv):
    x, y, c = _mesh_pos()
    dst = land.at[_slot((x, y, c))]
    targets = [(x, y, 1 - c), (1 - x, y, c), (x, 1 - y, c), (1 - x, 1 - y, c)]
    return [_remote(src, dst, send, recv, 4 * a + k, to) for k, to in enumerate(targets)]


def _ag_plan_pass(a, src, land, send, recv):
    x, y, c = _mesh_pos()
    blocks = [land.at[_slot((px, py, c))] for px, py in ((1 - x, y), (x, 1 - y), (1 - x, 1 - y))]
    return [_remote(blk, blk, send, recv, 3 * a + k, (x, y, 1 - c)) for k, blk in enumerate(blocks)]


def _rs_plan_sibling(a, src, land, send, recv):
    x, y, c = _mesh_pos()
    return [_remote(src.at[2 * j + (1 - c)], land.at[j], send, recv, 4 * a + j, (x, y, 1 - c)) for j in range(4)]


def _rs_plan_chips(a, src, land, send, recv):
    x, y, c = _mesh_pos()
    out = []
    for k in (1, 2, 3):
        px = 1 - x if k & 2 else x
        py = 1 - y if k & 1 else y
        out.append(_remote(src.at[2 * px + py], land.at[k - 1], send, recv, 3 * a + k - 1, (px, py, c)))
    return out


def _in_hbm(a):
    return pltpu.with_memory_space_constraint(a, pltpu.HBM)


def _exchange_start(srcs, lands, plan, n_k, *, name):
    ns, n = len(srcs), len(lands)

    def body(*refs):
        src_refs, land_refs = refs[:ns], refs[ns:ns + n]
        send, recv = refs[ns + n], refs[ns + n + 1]
        token = refs[-1]
        for a in range(n):
            for cp in plan(a, src_refs[a] if ns else None, land_refs[a], send, recv):
                cp.start()
        token[...] = jnp.zeros(token.shape, token.dtype)

    bufs = list(srcs) + list(lands)
    outs = pl.pallas_call(
        body, name=name,
        out_shape=(pltpu.SemaphoreType.DMA((n * n_k,)), pltpu.SemaphoreType.DMA((n * n_k,)),
                   *[pltpu.HBM(b.shape, b.dtype) for b in bufs], jax.ShapeDtypeStruct((SUBLANES, LANES), F32)),
        in_specs=[_HBM] * (ns + n),
        out_specs=(_SEM, _SEM, *[_HBM] * (ns + n), pl.BlockSpec(memory_space=pltpu.VMEM)),
        input_output_aliases={i: 2 + i for i in range(ns + n)},
        compiler_params=pltpu.CompilerParams(has_side_effects=_EFFECT),
    )(*[_in_hbm(b) for b in bufs])
    return outs[0], outs[1], list(outs[2:2 + ns]), list(outs[2 + ns:2 + ns + n]), outs[-1]


def _exchange_wait(started, plan, after, *, name):
    send, recv, srcs, lands, _ = started
    ns, n = len(srcs), len(lands)

    def body(*refs):
        src_refs, land_refs = refs[:ns], refs[ns:ns + n]
        send_ref, recv_ref = refs[ns + n], refs[ns + n + 1]
        for a in range(n):
            for cp in plan(a, src_refs[a] if ns else None, land_refs[a], send_ref, recv_ref):
                cp.wait_send()
                cp.wait_recv()

    bufs = list(srcs) + list(lands)
    outs = pl.pallas_call(
        body, name=name,
        out_shape=tuple(pltpu.HBM(b.shape, b.dtype) for b in bufs),
        in_specs=[_HBM] * (ns + n) + [_SEM, _SEM, _ANY],
        out_specs=tuple([_HBM] * (ns + n)),
        input_output_aliases={i: i for i in range(ns + n)},
        compiler_params=pltpu.CompilerParams(has_side_effects=_EFFECT),
    )(*bufs, send, recv, after)
    return list(outs[:ns]), list(outs[ns:])


def _pair_add(grads, landed, core, *, name, tr):
    _, r_all, c_all = grads.shape

    def body(core_ref, g_ref, l_ref, o_ref):
        o_ref[...] = (g_ref[...].astype(F32) + l_ref[...].astype(F32)).astype(o_ref.dtype)

    return pl.pallas_call(
        body, name=name,
        grid_spec=pltpu.PrefetchScalarGridSpec(
            num_scalar_prefetch=1, grid=(4, r_all // tr),
            in_specs=[pl.BlockSpec((None, tr, c_all), lambda j, i, core_ref: (2 * j + core_ref[0], i, 0)),
                      pl.BlockSpec((None, tr, c_all), lambda j, i, core_ref: (j, i, 0))],
            out_specs=pl.BlockSpec((None, tr, c_all), lambda j, i, core_ref: (j, i, 0))),
        out_shape=jax.ShapeDtypeStruct((4, r_all, c_all), grads.dtype),
        compiler_params=_params(("parallel", "parallel")),
    )(core, grads, landed)


def _adamw_math(w, g, m, v):
    m2 = ADAM_B1 * m + (1.0 - ADAM_B1) * g
    v2 = ADAM_B2 * v + (1.0 - ADAM_B2) * (g * g)
    m_hat = m2 / (1.0 - ADAM_B1 ** ADAM_STEP)
    v_hat = v2 / (1.0 - ADAM_B2 ** ADAM_STEP)
    delta = -ADAM_LR * (m_hat / (jnp.sqrt(v_hat) + ADAM_EPS) + ADAM_WD * w)
    return delta, m2, v2


def _adamw(w, m, v, terms, order, *, name, tr, col_block=None, own=None):
    r_all, c_all = w.shape
    n_slots = terms.shape[0]

    def body(*refs):
        if col_block is not None or own is not None:
            refs = refs[1:]
        own_ref = None
        if own is not None:
            own_ref, refs = refs[0], refs[1:]
        w_ref, m_ref, v_ref, t_ref, g_ref, d_ref, m2_ref, v2_ref = refs
        if own_ref is not None:
            g = own_ref[...].astype(F32) + t_ref[order[0]].astype(F32)
        else:
            g = t_ref[order[0]].astype(F32)
        for s in order[1:]:
            g = g + t_ref[s].astype(F32)
        delta, m2, v2 = _adamw_math(w_ref[...], g, m_ref[...], v_ref[...])
        g_ref[...] = g
        d_ref[...] = delta
        m2_ref[...] = m2
        v2_ref[...] = v2

    shape = jax.ShapeDtypeStruct((r_all, c_all), F32)
    if own is not None:
        row = pl.BlockSpec((tr, c_all), lambda i, idx: (i, 0))
        return pl.pallas_call(
            body, name=name,
            grid_spec=pltpu.PrefetchScalarGridSpec(
                num_scalar_prefetch=1, grid=(r_all // tr,),
                in_specs=[pl.BlockSpec((None, tr, c_all), lambda i, idx: (idx[0], i, 0)), row, row, row,
                          pl.BlockSpec((n_slots, tr, c_all), lambda i, idx: (0, i, 0))],
                out_specs=[row] * 4),
            out_shape=[shape] * 4, compiler_params=_params(("parallel",)),
        )(own[1], own[0], w, m, v, terms)
    if col_block is None:
        row = pl.BlockSpec((tr, c_all), lambda i: (i, 0))
        return pl.pallas_call(
            body, name=name, grid=(r_all // tr,),
            in_specs=[row, row, row, pl.BlockSpec((n_slots, tr, c_all), lambda i: (0, i, 0))],
            out_specs=[row] * 4, out_shape=[shape] * 4, compiler_params=_params(("parallel",)),
        )(w, m, v, terms)
    row = pl.BlockSpec((tr, c_all), lambda i, blk: (i, 0))
    return pl.pallas_call(
        body, name=name,
        grid_spec=pltpu.PrefetchScalarGridSpec(
            num_scalar_prefetch=1, grid=(r_all // tr,),
            in_specs=[row, row, row, pl.BlockSpec((n_slots, tr, c_all), lambda i, blk: (0, i, blk[0]))],
            out_specs=[row] * 4),
        out_shape=[shape] * 4, compiler_params=_params(("parallel",)),
    )(col_block, w, m, v, terms)


def _rope_tables(t_all):
    pos = jnp.arange(t_all, dtype=F32)
    inv_freq = ROPE_THETA ** (-jnp.arange(0, QK_ROPE, 2, dtype=F32) / QK_ROPE)
    ang = pos[:, None] * inv_freq[None, :]
    cos, sin = jnp.cos(ang), jnp.sin(ang)
    return jnp.tile(cos, (1, LANES // (QK_ROPE // 2))), jnp.tile(sin, (1, LANES // (QK_ROPE // 2)))


def _adam_row_tile(r_all, c_all, block_bytes=512 * 1024):
    target = max(SUBLANES, block_bytes // (4 * c_all))
    return _pick(r_all, [t for t in (1024, 704, 512, 352, 256, 176, 128, 64, 32, 16, 8) if t <= target])


def _rows_natural(wg):
    return wg.reshape(wg.shape[0] * wg.shape[1], wg.shape[2])


def _mla_layer_fwd(tag, h, g_mix, ws, qn, kvn, cos, sin, *, tm, tq, n_heads, scale, n_real):
    w_in, w_uq, w_ukv, w_o = _rows_natural(ws[0]), ws[1], ws[2], _rows_natural(ws[3])
    t_all, d = h.shape
    lq, lkv = qn.shape[1], kvn.shape[1]
    tmb = _pick(t_all, _ROW_TILES)
    hn = _rmsnorm_fwd(h, g_mix, name=f"norm_mix{tag}", tm=tm)
    proj = _mm_nn(hn, w_in, name=f"mla_in{tag}", out_dtype=F32, tm=tmb, tn=w_in.shape[1], tk=_pick(d, _DIVS))
    cq, ckv, kr = _mla_prep_fwd(proj, qn, kvn, cos, sin, name=f"mla_prep{tag}", tm=tm, lq=lq, lkv=lkv)
    q = _mm_nn(cq, w_uq, name=f"mla_q{tag}", out_dtype=BF16, tm=tmb, tn=w_uq.shape[2], tk=lq, b_blocked=True,
               epilogue=_rope_q_epilogue, extras=(cos, sin))
    kv = _mm_nn(ckv, w_ukv, name=f"mla_kv{tag}", out_dtype=BF16, tm=tmb, tn=w_ukv.shape[2], tk=lkv, b_blocked=True)
    o, lse = _attn_fwd(q, kv, kr, name=f"attn_fwd{tag}", n_heads=n_heads, tq=tq, n_real=n_real, scale=scale)
    h_mid = _mm_nn(o, w_o, name=f"mla_o{tag}", out_dtype=F32, tm=tmb, tn=_pick(d, _DIVS[2:]), tk=_pick(o.shape[1], _DIVS), res=h)
    return h_mid, (hn, proj, cq, ckv, kr, q, kv, o, lse)


def _mla_layer_bwd(tag, dh, dh_b, h_in, saved, g_mix, ws, qn, kvn, cos, sin, *, tm, tq, n_heads, scale, n_real):
    hn, proj, cq, ckv, kr, q, kv, o, lse = saved
    w_in, w_uq, w_ukv, w_o = _rows_natural(ws[0]), ws[1], ws[2], _rows_natural(ws[3])
    t_all, d = h_in.shape
    lq, lkv = qn.shape[1], kvn.shape[1]
    ov = o.shape[1]
    tmb = _pick(t_all, _ROW_TILES)
    tn_d, tk_d = _pick(d, _DIVS[1:]), _pick(d, _DIVS)
    do = _mm_nt(dh_b, w_o, name=f"mla_do{tag}", out_dtype=BF16, tm=tmb, tn=_pick(ov, _DIVS[1:]), tk=tk_d)
    dw_o = _mm_tn(o, dh_b, name=f"mla_dwo{tag}", out_dtype=BF16, tm=_pick(ov, _DIVS[1:]), tn=tn_d, tk=tmb)
    dq, dkv, dkr_h = _attn_bwd(q, kv, kr, o, lse, do, cos, sin, name=f"attn_bwd{tag}", n_heads=n_heads, tq=tq, n_real=n_real,
                               scale=scale)
    hw, kw = w_uq.shape[2], w_ukv.shape[2]
    dw_uq = _mm_tn(cq, dq, name=f"mla_dwuq{tag}", out_dtype=BF16, tm=lq, tn=hw, tk=tmb, out_block=hw)
    dcq = _mm_nt(dq, w_uq, name=f"mla_dcq{tag}", out_dtype=F32, tm=tmb, tn=lq, tk=hw, b_blocked=True)
    dw_ukv = _mm_tn(ckv, dkv, name=f"mla_dwukv{tag}", out_dtype=BF16, tm=lkv, tn=kw, tk=tmb, out_block=kw)
    dckv = _mm_nt(dkv, w_ukv, name=f"mla_dckv{tag}", out_dtype=F32, tm=tmb, tn=lkv, tk=kw, b_blocked=True)
    dproj, dqn, dkvn = _mla_prep_bwd(dcq, dckv, dkr_h, proj, qn, kvn, cos, sin, name=f"mla_prep_bwd{tag}", tm=tm, lq=lq, lkv=lkv)
    wc = w_in.shape[1]
    dw_in = _mm_tn(hn, dproj, name=f"mla_dwin{tag}", out_dtype=BF16, tm=tn_d, tn=wc, tk=tmb)
    dhn = _mm_nt(dproj, w_in, name=f"mla_dhn{tag}", out_dtype=F32, tm=tmb, tn=tn_d, tk=wc)
    dh, dh_b, dg = _rmsnorm_bwd(dhn, h_in, g_mix, dh, name=f"norm_mix_bwd{tag}", tm=tm)
    return dh, dh_b, dg, dqn, dkvn, [dw_in.reshape(N_DEV, -1, wc), dw_uq, dw_ukv, dw_o.reshape(N_DEV, -1, d)]


def _lru_layer_fwd(tag, h, g_mix, ws, small, *, tm):
    w_lin, w_lo = ws[0], _rows_natural(ws[1])
    t_all, d = h.shape
    dr = w_lo.shape[0]
    tmb = _pick(t_all, _ROW_TILES)
    hn = _rmsnorm_fwd(h, g_mix, name=f"norm_mix{tag}", tm=tm)
    xy = _mm_nn(hn, w_lin, name=f"lru_in{tag}", out_dtype=F32, tm=tmb, tn=w_lin.shape[2], tk=_pick(d, _DIVS), b_blocked=True)
    hs, hsy = _lru_fwd(xy, *small, name=f"lru_fwd{tag}")
    h_mid = _mm_nn(hsy, w_lo, name=f"lru_o{tag}", out_dtype=F32, tm=tmb, tn=_pick(d, _DIVS[2:]), tk=_pick(dr, _DIVS), res=h)
    return h_mid, (hn, xy, hs, hsy)


def _lru_layer_bwd(tag, dh, dh_b, h_in, saved, g_mix, ws, small, *, tm):
    hn, xy, hs, hsy = saved
    w_lin, w_lo = ws[0], _rows_natural(ws[1])
    t_all, d = h_in.shape
    dr = w_lo.shape[0]
    tmb = _pick(t_all, _ROW_TILES)
    tn_d, tk_d = _pick(d, _DIVS[1:]), _pick(d, _DIVS)
    dhsy = _mm_nt(dh_b, w_lo, name=f"lru_dhsy{tag}", out_dtype=F32, tm=tmb, tn=_pick(dr, _DIVS[1:]), tk=tk_d)
    dw_lo = _mm_tn(hsy, dh_b, name=f"lru_dwo{tag}", out_dtype=BF16, tm=_pick(dr, _DIVS[1:]), tn=tn_d, tk=tmb)
    dxy, *dsmall = _lru_bwd(xy, hs, dhsy, *small, name=f"lru_bwd{tag}")
    lw = w_lin.shape[2]
    dw_lin = _mm_tn(hn, dxy, name=f"lru_dwin{tag}", out_dtype=BF16, tm=tn_d, tn=lw, tk=tmb, out_block=lw)
    dhn = _mm_nt(dxy, w_lin, name=f"lru_dhn{tag}", out_dtype=F32, tm=tmb, tn=tn_d, tk=lw, b_blocked=True)
    dh, dh_b, dg = _rmsnorm_bwd(dhn, h_in, g_mix, dh, name=f"norm_mix_bwd{tag}", tm=tm)
    return dh, dh_b, dg, tuple(dsmall), [dw_lin, dw_lo.reshape(N_DEV, -1, d)]


def _ffn_layer_fwd(tag, h_mid, g_ffn, ws, *, tm):
    w_gu, w_down = ws[0], _rows_natural(ws[1])
    t_all, d = h_mid.shape
    f_all = w_down.shape[0]
    tmb = _pick(t_all, _ROW_TILES)
    fk = _pick(f_all, (1408,) + _DIVS[1:])
    hn2 = _rmsnorm_fwd(h_mid, g_ffn, name=f"norm_ffn{tag}", tm=tm)
    gu = _mm_nn(hn2, w_gu, name=f"ffn_gu{tag}", out_dtype=F32, tm=tmb, tn=w_gu.shape[2], tk=_pick(d, _DIVS), b_blocked=True)
    act = _swiglu_fwd(gu, name=f"swiglu_fwd{tag}", tm=tm, tc=fk)
    h_out = _mm_nn(act, w_down, name=f"ffn_down{tag}", out_dtype=F32, tm=tmb, tn=_pick(d, _DIVS[1:]), tk=fk, res=h_mid)
    return h_out, (hn2, gu, act)


def _ffn_layer_bwd(tag, dh, dh_b, h_mid, saved, g_ffn, ws, *, tm):
    hn2, gu, act = saved
    w_gu, w_down = ws[0], _rows_natural(ws[1])
    t_all, d = h_mid.shape
    f_all = w_down.shape[0]
    f_local = w_gu.shape[2]
    tmb = _pick(t_all, _ROW_TILES)
    fk = _pick(f_all, (1408,) + _DIVS[1:])
    tn_d, tk_d = _pick(d, _DIVS[1:]), _pick(d, _DIVS)
    dact = _mm_nt(dh_b, w_down, name=f"ffn_dact{tag}", out_dtype=F32, tm=tmb, tn=fk, tk=tk_d)
    dgu = _swiglu_bwd(dact, gu, name=f"swiglu_bwd{tag}", tm=tm, tc=fk)
    dw_down = _mm_tn(act, dh_b, name=f"ffn_dwdown{tag}", out_dtype=BF16, tm=fk, tn=tn_d, tk=tmb)
    dhn2 = _mm_nt(dgu, w_gu, name=f"ffn_dhn{tag}", out_dtype=F32, tm=tmb, tn=tn_d, tk=f_local, b_blocked=True)
    dw_gu = _mm_tn(hn2, dgu, name=f"ffn_dwgu{tag}", out_dtype=BF16, tm=tn_d, tn=f_local, tk=tmb, out_block=f_local)
    dh, dh_b, dg = _rmsnorm_bwd(dhn2, h_mid, g_ffn, dh, name=f"norm_ffn_bwd{tag}", tm=tm)
    return dh, dh_b, dg, [dw_gu, dw_down.reshape(N_DEV, -1, d)]


def kernel(x, meta_tokens, norm_mix, norm_ffn, norm_final, mla_w_in, mla_q_norm, mla_kv_norm, mla_w_uq, mla_w_ukv, mla_w_o, lru_w_in, lru_conv_w, lru_conv_b, lru_w_gate_a, lru_b_gate_a, lru_w_gate_x, lru_b_gate_x, lru_lambda, lru_w_o, ffn_w_gu, ffn_w_down, loss_target, m_meta_tokens, m_norm_mix, m_norm_ffn, m_norm_final, m_mla_w_in, m_mla_q_norm, m_mla_kv_norm, m_mla_w_uq, m_mla_w_ukv, m_mla_w_o, m_lru_w_in, m_lru_conv_w, m_lru_conv_b, m_lru_w_gate_a, m_lru_b_gate_a, m_lru_w_gate_x, m_lru_b_gate_x, m_lru_lambda, m_lru_w_o, m_ffn_w_gu, m_ffn_w_down, v_meta_tokens, v_norm_mix, v_norm_ffn, v_norm_final, v_mla_w_in, v_mla_q_norm, v_mla_kv_norm, v_mla_w_uq, v_mla_w_ukv, v_mla_w_o, v_lru_w_in, v_lru_conv_w, v_lru_conv_b, v_lru_w_gate_a, v_lru_b_gate_a, v_lru_w_gate_x, v_lru_b_gate_x, v_lru_lambda, v_lru_w_o, v_ffn_w_gu, v_ffn_w_down):
    seq, d = x.shape[1], x.shape[2]
    assert seq % CHUNK == 0
    n_real = N_META + seq
    t_all = -(-n_real // LANES) * LANES
    tm = _pick(t_all, (384, 256, 128))
    tq = _pick(seq, (512, 256, 128, 64))
    depth = norm_mix.shape[0]
    n_mla, n_lru = mla_w_in.shape[0], lru_w_in.shape[0]
    lq, lkv = mla_q_norm.shape[1], mla_kv_norm.shape[1]
    w_in_cols = lq + lkv + LANES
    heads_local = mla_w_uq.shape[2] // (QK_NOPE + QK_ROPE)
    n_heads = heads_local * N_DEV
    dr = lru_w_gate_a.shape[1] * lru_w_gate_a.shape[2]
    scale = (QK_NOPE + QK_ROPE) ** -0.5
    cx, cy, cc = _mesh_pos()
    core = jnp.reshape(cc, (1,)).astype(jnp.int32)
    my_slot = jnp.reshape(4 * cx + 2 * cy + cc, (1,)).astype(jnp.int32)

    def pad_cols(w, cols):
        return jnp.pad(w, ((0, 0), (0, cols - w.shape[1])))

    def pad_heads(w):
        k_all = w.shape[0]
        w3 = w.reshape(k_all, heads_local, QK_NOPE + QK_ROPE)
        return jnp.pad(w3, ((0, 0), (0, 0), (0, HEAD_W - QK_NOPE - QK_ROPE))).reshape(k_all, heads_local * HEAD_W)

    def unpad_heads(w):
        k_all = w.shape[0]
        return w.reshape(k_all, heads_local, HEAD_W)[:, :, :QK_NOPE + QK_ROPE].reshape(k_all, -1)

    small_rows = N_META + n_lru * 4 + 2 * n_lru
    small_pad = -(-small_rows // SUBLANES) * SUBLANES

    def pack_small(meta, conv_w, conv_b, lam):
        rows = jnp.concatenate([meta, conv_w.reshape(n_lru * 4, -1), conv_b, lam], axis=0)
        return jnp.pad(rows, ((0, small_pad - small_rows), (0, 0)))

    def unpack_small(p):
        o1 = N_META + n_lru * 4
        return (p[:N_META], p[N_META:o1].reshape(n_lru, 4, -1), p[o1:o1 + n_lru], p[o1 + n_lru:o1 + 2 * n_lru])

    (small_full,), small_done = _all_gather([pack_small(meta_tokens, lru_conv_w, lru_conv_b, lru_lambda)], name="ag_small")
    small_full = jnp.transpose(small_full, (1, 0, 2)).reshape(small_pad, -1)
    meta_full, conv_w_full, conv_b_full, lam_full = unpack_small(small_full)

    def wire(w):
        return (w + small_done).astype(BF16)

    mla_shards, lru_shards, ffn_shards = [], [], []
    for j in range(n_mla):
        mla_shards.append([wire(pad_cols(mla_w_in[j], w_in_cols)), wire(pad_heads(mla_w_uq[j])), wire(mla_w_ukv[j]),
                           wire(mla_w_o[j])])
    for j in range(n_lru):
        lru_shards.append([wire(lru_w_in[j]), wire(lru_w_o[j])])
    for layer in range(depth):
        ffn_shards.append([wire(ffn_w_gu[layer]), wire(ffn_w_down[layer])])

    n_sub = 2 * depth
    groups = []
    for layer in range(depth):
        groups += [mla_shards[layer // 2] if layer % 2 == 0 else lru_shards[layer // 2], ffn_shards[layer]]
    slot_idx = 4 * cx + 2 * cy + cc
    ag_own = []
    for gi, shards in enumerate(groups):
        lands = [lax.dynamic_update_slice(lax.empty((N_DEV,) + s.shape, s.dtype), s[None], (slot_idx, 0, 0)) for s in shards]
        ag_own.append(_exchange_start(shards, lands, _ag_plan_own, 4, name=f"ag{gi}_start"))
    ag_pass = [None] * n_sub
    weights = [None] * n_sub

    def ag_landed(gi, after):
        _, lands = _exchange_wait(ag_own[gi], _ag_plan_own, after, name=f"ag{gi}_wait")
        ag_pass[gi] = _exchange_start([], lands, _ag_plan_pass, 3, name=f"ag{gi}_pass")
        return ag_pass[gi][4][0, 0]

    def ag_done(gi, after):
        _, weights[gi] = _exchange_wait(ag_pass[gi], _ag_plan_pass, after, name=f"ag{gi}_pass_wait")

    cos, sin = _rope_tables(t_all)
    zeros_tail = jnp.zeros((t_all - n_real, d), F32)
    started = ag_own[0][4][0, 0]
    for st in ag_own[1:]:
        started = started + st[4][0, 0]
    h = jnp.concatenate([meta_full + started, x[0], zeros_tail], axis=0)
    target = jnp.concatenate([jnp.zeros((N_META, d), F32), loss_target[0], zeros_tail], axis=0)

    attn_kw = dict(tm=tm, tq=tq, n_heads=n_heads, scale=scale, n_real=n_real)

    def lru_small(j):
        return (conv_w_full[j], conv_b_full[j][None, :], lru_w_gate_a[j].astype(BF16), lru_b_gate_a[j].reshape(1, dr),
                lru_w_gate_x[j].astype(BF16), lru_b_gate_x[j].reshape(1, dr), lam_full[j][None, :])

    def before_sublayer(k, act):
        tok = ag_landed(k, act) if k <= 1 else 0.0
        ag_done(k, act)
        if 1 <= k < n_sub - 1:
            tok = tok + ag_landed(k + 1, act)
        return tok

    saved = []
    for layer in range(depth):
        j = layer // 2
        g_mix = norm_mix[layer][None, :] + before_sublayer(2 * layer, h)
        if layer % 2 == 0:
            h_mid, mix_saved = _mla_layer_fwd(layer, h, g_mix, weights[2 * layer], mla_q_norm[j][None, :],
                                              mla_kv_norm[j][None, :], cos, sin, **attn_kw)
        else:
            h_mid, mix_saved = _lru_layer_fwd(layer, h, g_mix, weights[2 * layer], lru_small(j), tm=tm)
        g_ffn = norm_ffn[layer][None, :] + before_sublayer(2 * layer + 1, h_mid)
        h_out, ffn_saved = _ffn_layer_fwd(layer, h_mid, g_ffn, weights[2 * layer + 1], tm=tm)
        saved.append((h, h_mid, mix_saved, ffn_saved))
        h = h_out

    loss_part, dh, dh_b, dg_final = _loss_head(h, target, norm_final[None, :], name="loss_head", tm=tm, n_real=n_real)
    loss = lax.psum(loss_part[0, 0], ("x", "y", "c"))

    rs_sib, rs_chip, reduced = [None] * n_sub, [None] * n_sub, [None] * n_sub
    chip_idx = jnp.reshape(2 * cx + cy, (1,)).astype(jnp.int32)

    def rs_begin(k, grads):
        lands = [lax.empty((4,) + g.shape[1:], g.dtype) for g in grads]
        rs_sib[k] = _exchange_start(grads, lands, _rs_plan_sibling, 4, name=f"rs{k}_start")
        return rs_sib[k][4][0, 0]

    def rs_middle(k, after):
        grads, landed = _exchange_wait(rs_sib[k], _rs_plan_sibling, after, name=f"rs{k}_wait")
        parts = [_pair_add(g, l, core, name=f"rs{k}_add{a}", tr=_adam_row_tile(g.shape[1], g.shape[2], 4 * 1024 * 1024))
                 for a, (g, l) in enumerate(zip(grads, landed))]
        lands = [lax.empty((3,) + p.shape[1:], p.dtype) for p in parts]
        rs_chip[k] = _exchange_start(parts, lands, _rs_plan_chips, 3, name=f"rs{k}_chips")
        return rs_chip[k][4][0, 0]

    def rs_end(k, after):
        reduced[k] = _exchange_wait(rs_chip[k], _rs_plan_chips, after, name=f"rs{k}_chips_wait")

    d_norm_mix, d_norm_ffn = [None] * depth, [None] * depth
    d_qn, d_kvn = [None] * n_mla, [None] * n_mla
    d_small = {k: [None] * n_lru for k in ("cw", "cb", "wga", "bga", "wgx", "bgx", "lam")}
    tok, waiting = 0.0, None
    for layer in reversed(range(depth)):
        j = layer // 2
        h_in, h_mid, mix_saved, ffn_saved = saved[layer]
        dh, dh_b, d_norm_ffn[layer], ffn_g = _ffn_layer_bwd(layer, dh, dh_b, h_mid, ffn_saved, norm_ffn[layer][None, :] + tok,
                                                            weights[2 * layer + 1], tm=tm)
        tok = rs_begin(2 * layer + 1, ffn_g)
        if waiting is not None:
            tok = tok + rs_middle(waiting, dh)
        waiting = 2 * layer + 1
        if layer == 0:
            tok = tok + rs_middle(waiting, dh)
            waiting = None
        g_mix = norm_mix[layer][None, :] + tok
        if layer % 2 == 0:
            dh, dh_b, d_norm_mix[layer], d_qn[j], d_kvn[j], mix_g = _mla_layer_bwd(
                layer, dh, dh_b, h_in, mix_saved, g_mix, weights[2 * layer], mla_q_norm[j][None, :], mla_kv_norm[j][None, :],
                cos, sin, **attn_kw)
        else:
            dh, dh_b, d_norm_mix[layer], dsmall, mix_g = _lru_layer_bwd(layer, dh, dh_b, h_in, mix_saved, g_mix,
                                                                        weights[2 * layer], lru_small(j), tm=tm)
            for key, val in zip(("cw", "cb", "wga", "bga", "wgx", "bgx", "lam"), dsmall):
                d_small[key][j] = val
        tok = rs_begin(2 * layer, mix_g)
        if waiting is not None:
            tok = tok + rs_middle(waiting, dh)
        waiting = 2 * layer
    rs_middle(waiting, dh)

    grad_x = dh[N_META:n_real][None]

    def adam_sharded(k, a, w, m, v, tag):
        parts, landed = reduced[k]
        r_all, c_all = landed[a].shape[1], landed[a].shape[2]
        return _adamw(w.reshape(r_all, c_all), m.reshape(r_all, c_all), v.reshape(r_all, c_all), landed[a], (0, 1, 2),
                      name=f"adamw_{tag}", tr=_adam_row_tile(r_all, c_all, 2 * 1024 * 1024), own=(parts[a], chip_idx))

    def stacked(outs):
        return [jnp.stack([o[k] for o in outs], axis=0) for k in range(4)]

    per = {nm: [None] * (depth if nm.startswith("ffn") else n_mla) for nm in
           ("mla_w_in", "mla_w_uq", "mla_w_ukv", "mla_w_o", "lru_w_in", "lru_w_o", "ffn_w_gu", "ffn_w_down")}
    after = dh
    for k in reversed(range(n_sub)):
        rs_end(k, after)
        layer, j = k // 2, k // 4
        if k % 2 == 1:
            per["ffn_w_gu"][layer] = adam_sharded(k, 0, ffn_w_gu[layer], m_ffn_w_gu[layer], v_ffn_w_gu[layer], f"ffn_w_gu{layer}")
            per["ffn_w_down"][layer] = adam_sharded(k, 1, ffn_w_down[layer], m_ffn_w_down[layer], v_ffn_w_down[layer],
                                                    f"ffn_w_down{layer}")
            after = per["ffn_w_down"][layer][0]
        elif layer % 2 == 0:
            per["mla_w_in"][j] = [t[:, :lq + lkv + QK_ROPE] for t in adam_sharded(
                k, 0, pad_cols(mla_w_in[j], w_in_cols), pad_cols(m_mla_w_in[j], w_in_cols), pad_cols(v_mla_w_in[j], w_in_cols),
                f"mla_w_in{j}")]
            per["mla_w_uq"][j] = [unpad_heads(t) for t in adam_sharded(
                k, 1, pad_heads(mla_w_uq[j]), pad_heads(m_mla_w_uq[j]), pad_heads(v_mla_w_uq[j]), f"mla_w_uq{j}")]
            per["mla_w_ukv"][j] = adam_sharded(k, 2, mla_w_ukv[j], m_mla_w_ukv[j], v_mla_w_ukv[j], f"mla_w_ukv{j}")
            per["mla_w_o"][j] = adam_sharded(k, 3, mla_w_o[j], m_mla_w_o[j], v_mla_w_o[j], f"mla_w_o{j}")
            after = per["mla_w_o"][j][0]
        else:
            per["lru_w_in"][j] = adam_sharded(k, 0, lru_w_in[j], m_lru_w_in[j], v_lru_w_in[j], f"lru_w_in{j}")
            per["lru_w_o"][j] = adam_sharded(k, 1, lru_w_o[j], m_lru_w_o[j], v_lru_w_o[j], f"lru_w_o{j}")
            after = per["lru_w_o"][j][0]
    res = {nm: stacked(outs) for nm, outs in per.items()}

    d_meta = dh[:N_META]
    small_grad = pack_small(d_meta, jnp.stack(d_small["cw"], axis=0), jnp.concatenate(d_small["cb"], axis=0),
                            jnp.concatenate(d_small["lam"], axis=0))
    rep_grads = [
        jnp.concatenate(d_norm_mix, axis=0), jnp.concatenate(d_norm_ffn, axis=0), dg_final,
        jnp.concatenate(d_qn, axis=0), jnp.concatenate(d_kvn, axis=0),
        jnp.stack(d_small["wga"], axis=0).reshape(-1, LANES), jnp.concatenate(d_small["bga"], axis=0),
        jnp.stack(d_small["wgx"], axis=0).reshape(-1, LANES), jnp.concatenate(d_small["bgx"], axis=0),
    ]
    rep_grads = [jnp.pad(g, ((0, -g.shape[0] % SUBLANES), (0, 0))) for g in rep_grads]
    all_small, _ = _all_gather([small_grad] + rep_grads, name="ag_small_grads")
    slot_order = tuple(range(N_DEV))

    def adam_rep(terms, w, m, v, tag):
        r_pad, c_all = terms.shape[1], terms.shape[2]

        def prep(t):
            t2 = t.reshape(-1, c_all)
            return jnp.pad(t2, ((0, r_pad - t2.shape[0]), (0, 0)))

        outs = _adamw(prep(w), prep(m), prep(v), terms, slot_order, name=f"adamw_{tag}", tr=_adam_row_tile(r_pad, c_all))
        n_rows = w.size // c_all
        return [o[:n_rows].reshape(w.shape) for o in outs]

    small_w = pack_small(meta_tokens, lru_conv_w, lru_conv_b, lru_lambda)
    small_m = pack_small(m_meta_tokens, m_lru_conv_w, m_lru_conv_b, m_lru_lambda)
    small_v = pack_small(v_meta_tokens, v_lru_conv_w, v_lru_conv_b, v_lru_lambda)
    small_out = _adamw(small_w, small_m, small_v, all_small[0], slot_order, name="adamw_small", tr=small_pad, col_block=my_slot)
    small_out = [unpack_small(o) for o in small_out]
    for idx, key in enumerate(("meta_tokens", "lru_conv_w", "lru_conv_b", "lru_lambda")):
        res[key] = [small_out[k][idx] for k in range(4)]

    res["norm_mix"] = adam_rep(all_small[1], norm_mix, m_norm_mix, v_norm_mix, "norm_mix")
    res["norm_ffn"] = adam_rep(all_small[2], norm_ffn, m_norm_ffn, v_norm_ffn, "norm_ffn")
    res["norm_final"] = adam_rep(all_small[3], norm_final, m_norm_final, v_norm_final, "norm_final")
    res["mla_q_norm"] = adam_rep(all_small[4], mla_q_norm, m_mla_q_norm, v_mla_q_norm, "mla_q_norm")
    res["mla_kv_norm"] = adam_rep(all_small[5], mla_kv_norm, m_mla_kv_norm, v_mla_kv_norm, "mla_kv_norm")
    res["lru_w_gate_a"] = adam_rep(all_small[6], lru_w_gate_a, m_lru_w_gate_a, v_lru_w_gate_a, "lru_w_gate_a")
    res["lru_b_gate_a"] = adam_rep(all_small[7], lru_b_gate_a, m_lru_b_gate_a, v_lru_b_gate_a, "lru_b_gate_a")
    res["lru_w_gate_x"] = adam_rep(all_small[8], lru_w_gate_x, m_lru_w_gate_x, v_lru_w_gate_x, "lru_w_gate_x")
    res["lru_b_gate_x"] = adam_rep(all_small[9], lru_b_gate_x, m_lru_b_gate_x, v_lru_b_gate_x, "lru_b_gate_x")

    names = ["meta_tokens", "norm_mix", "norm_ffn", "norm_final", "mla_w_in", "mla_q_norm", "mla_kv_norm", "mla_w_uq",
             "mla_w_ukv", "mla_w_o", "lru_w_in", "lru_conv_w", "lru_conv_b", "lru_w_gate_a", "lru_b_gate_a", "lru_w_gate_x",
             "lru_b_gate_x", "lru_lambda", "lru_w_o", "ffn_w_gu", "ffn_w_down"]
    shapes = dict(meta_tokens=meta_tokens, norm_mix=norm_mix, norm_ffn=norm_ffn, norm_final=norm_final, mla_w_in=mla_w_in,
                  mla_q_norm=mla_q_norm, mla_kv_norm=mla_kv_norm, mla_w_uq=mla_w_uq, mla_w_ukv=mla_w_ukv, mla_w_o=mla_w_o,
                  lru_w_in=lru_w_in, lru_conv_w=lru_conv_w, lru_conv_b=lru_conv_b, lru_w_gate_a=lru_w_gate_a,
                  lru_b_gate_a=lru_b_gate_a, lru_w_gate_x=lru_w_gate_x, lru_b_gate_x=lru_b_gate_x, lru_lambda=lru_lambda,
                  lru_w_o=lru_w_o, ffn_w_gu=ffn_w_gu, ffn_w_down=ffn_w_down)
    outs = [loss, grad_x]
    for k in range(4):
        outs += [res[nm][k].reshape(shapes[nm].shape) for nm in names]
    return tuple(outs)
```

```python
import math

import jax
import jax.numpy as jnp
from jax import lax
from jax.experimental import pallas as pl
from jax.experimental.pallas import tpu as pltpu

F32 = jnp.float32
BF16 = jnp.bfloat16
MESH = pl.DeviceIdType.MESH

N_META = 16
CHUNK = 64
QK_NOPE = 128
QK_ROPE = 64
V_HEAD = 128
HEAD_W = 256
ROPE_THETA = 10000.0
LRU_C = 8.0
RMS_EPS = 1e-6
NEG_BIG = -1e30
ADAM_LR, ADAM_B1, ADAM_B2, ADAM_EPS, ADAM_WD, ADAM_STEP = 0.001, 0.9, 0.999, 1e-08, 0.01, 10

LANES = 128
SUBLANES = 8
VMEM_LIMIT_BYTES = 52 * 1024 * 1024
N_DEV = 8

_NT = (((1,), (1,)), ((), ()))
_TN = (((0,), (0,)), ((), ()))
_DIVS = (2048, 1024, 512, 256, 128)
_ROW_TILES = (1408, 1024, 512, 256, 128)


def _params(dims):
    return pltpu.CompilerParams(dimension_semantics=dims, vmem_limit_bytes=VMEM_LIMIT_BYTES)


def _pick(n, candidates):
    for c in candidates:
        if c <= n and n % c == 0:
            return c
    return n


def _sigmoid(z):
    return 1.0 / (1.0 + jnp.exp(-z))


def _gelu(x):
    c = math.sqrt(2.0 / math.pi)
    return 0.5 * x * (1.0 + jnp.tanh(c * (x + 0.044715 * x * x * x)))


def _gelu_grad(x):
    c = math.sqrt(2.0 / math.pi)
    th = jnp.tanh(c * (x + 0.044715 * x * x * x))
    return 0.5 * (1.0 + th) + 0.5 * x * (1.0 - th * th) * c * (1.0 + 3.0 * 0.044715 * x * x)


def _neg_expm1(x):
    poly = -x * (1.0 + x * (1.0 / 2.0) * (1.0 + x * (1.0 / 3.0) * (1.0 + x * (1.0 / 4.0) * (
        1.0 + x * (1.0 / 5.0) * (1.0 + x * (1.0 / 6.0) * (1.0 + x * (1.0 / 7.0)))))))
    return jnp.where(x > -0.25, poly, 1.0 - jnp.exp(x))


def _softplus_neg(lam):
    e = jnp.exp(-jnp.abs(lam))
    log1p = jnp.where(e > 1e-4, jnp.log(1.0 + e), e * (1.0 - e * (0.5 - e * (1.0 / 3.0))))
    return jnp.maximum(-lam, 0.0) + log1p


def _rot_half(x):
    lane = lax.broadcasted_iota(jnp.int32, x.shape, 1)
    first = (lane % QK_ROPE) < (QK_ROPE // 2)
    return jnp.where(first, -pltpu.roll(x, LANES - QK_ROPE // 2, 1), pltpu.roll(x, QK_ROPE // 2, 1))


def _rope(x, cos, sin):
    return x * cos + _rot_half(x) * sin


def _unrope(g, cos, sin):
    return g * cos - _rot_half(g) * sin


def _mm_nn(a, b, *, name, out_dtype, tm, tn, tk, b_blocked=False, res=None, epilogue=None, extras=()):
    m_all, k_all = a.shape
    if b_blocked:
        g_all, kb, nb = b.shape
        n_all = g_all * nb
        assert nb % tn == 0
        r = nb // tn
        b_spec = pl.BlockSpec((None, tk, tn), lambda j, i, k: (j // r, k, j % r))
    else:
        kb, n_all = b.shape
        b_spec = pl.BlockSpec((tk, tn), lambda j, i, k: (k, j))
    assert kb == k_all and m_all % tm == 0 and n_all % tn == 0 and k_all % tk == 0
    nm, nn, nk = m_all // tm, n_all // tn, k_all // tk
    in_specs = [pl.BlockSpec((tm, tk), lambda j, i, k: (i, k)), b_spec]
    operands = [a, b]
    has_res = res is not None
    if has_res:
        in_specs.append(pl.BlockSpec((tm, tn), lambda j, i, k: (i, j)))
        operands.append(res)
    for e in extras:
        in_specs.append(pl.BlockSpec((tm, e.shape[1]), lambda j, i, k: (i, 0)))
        operands.append(e)
    n_ex = len(extras)

    def body(*refs):
        a_ref, b_ref = refs[0], refs[1]
        pos = 2
        res_ref = None
        if has_res:
            res_ref = refs[pos]
            pos += 1
        ex_refs = refs[pos:pos + n_ex]
        pos += n_ex
        o_ref = refs[pos]
        acc_ref = refs[pos + 1] if nk > 1 else None

        def finish(acc):
            if has_res:
                acc = acc + res_ref[...]
            if epilogue is not None:
                acc = epilogue(acc, *ex_refs)
            o_ref[...] = acc.astype(o_ref.dtype)

        prod = jnp.dot(a_ref[...], b_ref[...], preferred_element_type=F32)
        if nk == 1:
            finish(prod)
        else:
            k = pl.program_id(2)

            @pl.when(k == 0)
            def _():
                acc_ref[...] = prod

            @pl.when(k > 0)
            def _():
                acc_ref[...] += prod

            @pl.when(k == nk - 1)
            def _():
                finish(acc_ref[...])

    return pl.pallas_call(
        body, name=name, grid=(nn, nm, nk), in_specs=in_specs,
        out_specs=pl.BlockSpec((tm, tn), lambda j, i, k: (i, j)),
        out_shape=jax.ShapeDtypeStruct((m_all, n_all), out_dtype),
        scratch_shapes=[pltpu.VMEM((tm, tn), F32)] if nk > 1 else [],
        compiler_params=_params(("parallel", "parallel", "arbitrary")),
    )(*operands)


def _mm_nt(a, b, *, name, out_dtype, tm, tn, tk, b_blocked=False):
    if a.ndim == 3:
        n_planes, m_all, kp = a.shape
        assert kp % tk == 0
        rp = kp // tk
        k_all = n_planes * kp
        a_spec = pl.BlockSpec((None, tm, tk), lambda j, i, k: (k // rp, i, k % rp))
    else:
        m_all, k_all = a.shape
        a_spec = pl.BlockSpec((tm, tk), lambda j, i, k: (i, k))
    if b_blocked:
        g_all, n_all, nb = b.shape
        assert g_all * nb == k_all and nb % tk == 0
        r = nb // tk
        b_spec = pl.BlockSpec((None, tn, tk), lambda j, i, k: (k // r, j, k % r))
    else:
        n_all, kb = b.shape
        assert kb == k_all
        b_spec = pl.BlockSpec((tn, tk), lambda j, i, k: (j, k))
    assert m_all % tm == 0 and n_all % tn == 0 and k_all % tk == 0
    nm, nn, nk = m_all // tm, n_all // tn, k_all // tk

    def body(a_ref, b_ref, o_ref, *scratch):
        prod = lax.dot_general(a_ref[...], b_ref[...], _NT, preferred_element_type=F32)
        if nk == 1:
            o_ref[...] = prod.astype(o_ref.dtype)
        else:
            acc_ref = scratch[0]
            k = pl.program_id(2)

            @pl.when(k == 0)
            def _():
                acc_ref[...] = prod

            @pl.when(k > 0)
            def _():
                acc_ref[...] += prod

            @pl.when(k == nk - 1)
            def _():
                o_ref[...] = acc_ref[...].astype(o_ref.dtype)

    return pl.pallas_call(
        body, name=name, grid=(nn, nm, nk),
        in_specs=[a_spec, b_spec],
        out_specs=pl.BlockSpec((tm, tn), lambda j, i, k: (i, j)),
        out_shape=jax.ShapeDtypeStruct((m_all, n_all), out_dtype),
        scratch_shapes=[pltpu.VMEM((tm, tn), F32)] if nk > 1 else [],
        compiler_params=_params(("parallel", "parallel", "arbitrary")),
    )(a, b)


def _mm_tn(a, b, *, name, out_dtype, tm, tn, tk, out_block=None):
    t_all, m_all = a.shape
    if b.ndim == 3:
        n_planes, tb, n_p = b.shape
        assert n_p % tn == 0
        rq = n_p // tn
        n_all = n_planes * n_p
        b_spec = pl.BlockSpec((None, tk, tn), lambda i, j, k: (j // rq, k, j % rq))
    else:
        tb, n_all = b.shape
        b_spec = pl.BlockSpec((tk, tn), lambda i, j, k: (k, j))
    assert tb == t_all and m_all % tm == 0 and n_all % tn == 0 and t_all % tk == 0
    nm, nn, nk = m_all // tm, n_all // tn, t_all // tk
    if out_block is None:
        out_shape = jax.ShapeDtypeStruct((m_all, n_all), out_dtype)
        out_spec = pl.BlockSpec((tm, tn), lambda i, j, k: (i, j))
    else:
        assert out_block % tn == 0 and n_all % out_block == 0
        r = out_block // tn
        out_shape = jax.ShapeDtypeStruct((n_all // out_block, m_all, out_block), out_dtype)
        out_spec = pl.BlockSpec((None, tm, tn), lambda i, j, k: (j // r, i, j % r))

    def body(a_ref, b_ref, o_ref, *scratch):
        prod = lax.dot_general(a_ref[...], b_ref[...], _TN, preferred_element_type=F32)
        if nk == 1:
            o_ref[...] = prod.astype(o_ref.dtype)
        else:
            acc_ref = scratch[0]
            k = pl.program_id(2)

            @pl.when(k == 0)
            def _():
                acc_ref[...] = prod

            @pl.when(k > 0)
            def _():
                acc_ref[...] += prod

            @pl.when(k == nk - 1)
            def _():
                o_ref[...] = acc_ref[...].astype(o_ref.dtype)

    return pl.pallas_call(
        body, name=name, grid=(nm, nn, nk),
        in_specs=[pl.BlockSpec((tk, tm), lambda i, j, k: (k, i)), b_spec],
        out_specs=out_spec, out_shape=out_shape,
        scratch_shapes=[pltpu.VMEM((tm, tn), F32)] if nk > 1 else [],
        compiler_params=_params(("parallel", "parallel", "arbitrary")),
    )(a, b)


def _rmsnorm_fwd(x, g, *, name, tm):
    t_all, d = x.shape

    def body(x_ref, g_ref, o_ref):
        xv = x_ref[...]
        rstd = lax.rsqrt(jnp.mean(xv * xv, axis=-1, keepdims=True) + RMS_EPS)
        o_ref[...] = (xv * rstd * g_ref[...]).astype(o_ref.dtype)

    return pl.pallas_call(
        body, name=name, grid=(t_all // tm,),
        in_specs=[pl.BlockSpec((tm, d), lambda i: (i, 0)), pl.BlockSpec((1, d), lambda i: (0, 0))],
        out_specs=pl.BlockSpec((tm, d), lambda i: (i, 0)),
        out_shape=jax.ShapeDtypeStruct((t_all, d), BF16),
        compiler_params=_params(("parallel",)),
    )(x, g)


def _rms_bwd_math(dy, xv, g):
    rstd = lax.rsqrt(jnp.mean(xv * xv, axis=-1, keepdims=True) + RMS_EPS)
    xhat = xv * rstd
    dxh = dy * g
    dx = rstd * (dxh - xhat * jnp.mean(dxh * xhat, axis=-1, keepdims=True))
    return dx, jnp.sum(dy * xhat, axis=0, keepdims=True)


def _rmsnorm_bwd(dy, x, g, res, *, name, tm):
    t_all, d = x.shape

    def body(dy_ref, x_ref, g_ref, res_ref, dx_ref, dxb_ref, dg_ref):
        dx, dg = _rms_bwd_math(dy_ref[...], x_ref[...], g_ref[...])
        tot = res_ref[...] + dx
        dx_ref[...] = tot
        dxb_ref[...] = tot.astype(BF16)

        @pl.when(pl.program_id(0) == 0)
        def _():
            dg_ref[...] = dg

        @pl.when(pl.program_id(0) > 0)
        def _():
            dg_ref[...] += dg

    row = pl.BlockSpec((tm, d), lambda i: (i, 0))
    vec = pl.BlockSpec((1, d), lambda i: (0, 0))
    return pl.pallas_call(
        body, name=name, grid=(t_all // tm,),
        in_specs=[row, row, vec, row], out_specs=[row, row, vec],
        out_shape=[jax.ShapeDtypeStruct((t_all, d), F32), jax.ShapeDtypeStruct((t_all, d), BF16),
                   jax.ShapeDtypeStruct((1, d), F32)],
        compiler_params=_params(("arbitrary",)),
    )(dy, x, g, res)


def _loss_head(h, target, g, *, name, tm, n_real):
    t_all, d = h.shape

    def body(h_ref, t_ref, g_ref, loss_ref, dx_ref, dxb_ref, dg_ref):
        i = pl.program_id(0)
        xv = h_ref[...]
        gv = g_ref[...]
        rstd = lax.rsqrt(jnp.mean(xv * xv, axis=-1, keepdims=True) + RMS_EPS)
        y = xv * rstd * gv
        row = i * tm + lax.broadcasted_iota(jnp.int32, (tm, 1), 0)
        valid = (row >= N_META) & (row < n_real)
        err = jnp.where(valid, y - t_ref[...], 0.0)
        part = 0.5 * jnp.sum(jnp.mean(err * err, axis=-1, keepdims=True), axis=0, keepdims=True)
        dx, dg = _rms_bwd_math(err * (1.0 / d), xv, gv)
        dx_ref[...] = dx
        dxb_ref[...] = dx.astype(BF16)

        @pl.when(i == 0)
        def _():
            dg_ref[...] = dg
            loss_ref[...] = jnp.broadcast_to(part, loss_ref.shape)

        @pl.when(i > 0)
        def _():
            dg_ref[...] += dg
            loss_ref[...] += jnp.broadcast_to(part, loss_ref.shape)

    row = pl.BlockSpec((tm, d), lambda i: (i, 0))
    vec = pl.BlockSpec((1, d), lambda i: (0, 0))
    return pl.pallas_call(
        body, name=name, grid=(t_all // tm,),
        in_specs=[row, row, vec],
        out_specs=[pl.BlockSpec((1, LANES), lambda i: (0, 0)), row, row, vec],
        out_shape=[jax.ShapeDtypeStruct((1, LANES), F32), jax.ShapeDtypeStruct((t_all, d), F32),
                   jax.ShapeDtypeStruct((t_all, d), BF16), jax.ShapeDtypeStruct((1, d), F32)],
        compiler_params=_params(("arbitrary",)),
    )(h, target, g)


def _ffn_up(x, w_gu, *, name, tm):
    t_all, d = x.shape
    g_all, kb, nb = w_gu.shape
    half = g_all // 2
    f = half * nb
    assert kb == d and t_all % tm == 0

    def body(x_ref, wg_ref, wu_ref, gu_ref, act_ref):
        xv = x_ref[...]
        gv = jnp.dot(xv, wg_ref[...], preferred_element_type=F32)
        uv = jnp.dot(xv, wu_ref[...], preferred_element_type=F32)
        gu_ref[0] = gv
        gu_ref[1] = uv
        act_ref[...] = (gv * _sigmoid(gv) * uv).astype(act_ref.dtype)

    return pl.pallas_call(
        body, name=name, grid=(half, t_all // tm),
        in_specs=[pl.BlockSpec((tm, d), lambda j, i: (i, 0)), pl.BlockSpec((None, d, nb), lambda j, i: (j, 0, 0)),
                  pl.BlockSpec((None, d, nb), lambda j, i: (j + half, 0, 0))],
        out_specs=[pl.BlockSpec((2, tm, nb), lambda j, i: (0, i, j)), pl.BlockSpec((tm, nb), lambda j, i: (i, j))],
        out_shape=[jax.ShapeDtypeStruct((2, t_all, f), F32), jax.ShapeDtypeStruct((t_all, f), BF16)],
        compiler_params=_params(("parallel", "parallel")),
    )(x, w_gu, w_gu)


def _ffn_dact(dy, w_down, gu, *, name, tm, tn):
    t_all, d = dy.shape
    f = w_down.shape[0]
    assert t_all % tm == 0 and f % tn == 0

    def body(dy_ref, w_ref, gu_ref, o_ref):
        dact = lax.dot_general(dy_ref[...], w_ref[...], _NT, preferred_element_type=F32)
        gv, uv = gu_ref[0], gu_ref[1]
        sg = _sigmoid(gv)
        o_ref[0] = (dact * uv * (sg * (1.0 + gv * (1.0 - sg)))).astype(o_ref.dtype)
        o_ref[1] = (dact * gv * sg).astype(o_ref.dtype)

    return pl.pallas_call(
        body, name=name, grid=(f // tn, t_all // tm),
        in_specs=[pl.BlockSpec((tm, d), lambda j, i: (i, 0)), pl.BlockSpec((tn, d), lambda j, i: (j, 0)),
                  pl.BlockSpec((2, tm, tn), lambda j, i: (0, i, j))],
        out_specs=pl.BlockSpec((2, tm, tn), lambda j, i: (0, i, j)),
        out_shape=jax.ShapeDtypeStruct((2, t_all, f), BF16),
        compiler_params=_params(("parallel", "parallel")),
    )(dy, w_down, gu)


def _mla_prep_fwd(proj, qn, kvn, cos, sin, *, name, tm, lq, lkv):
    t_all, w = proj.shape

    def body(p_ref, qn_ref, kvn_ref, cos_ref, sin_ref, cq_ref, ckv_ref, kr_ref):
        pv = p_ref[...]
        xq = pv[:, :lq]
        xkv = pv[:, lq:lq + lkv]
        cq_ref[...] = (xq * lax.rsqrt(jnp.mean(xq * xq, axis=-1, keepdims=True) + RMS_EPS) * qn_ref[...]).astype(BF16)
        ckv_ref[...] = (xkv * lax.rsqrt(jnp.mean(xkv * xkv, axis=-1, keepdims=True) + RMS_EPS) * kvn_ref[...]).astype(BF16)
        kr_ref[...] = _rope(pv[:, lq + lkv:], cos_ref[...], sin_ref[...]).astype(BF16)

    def row(width):
        return pl.BlockSpec((tm, width), lambda i: (i, 0))

    def vec(width):
        return pl.BlockSpec((1, width), lambda i: (0, 0))

    return pl.pallas_call(
        body, name=name, grid=(t_all // tm,),
        in_specs=[row(w), vec(lq), vec(lkv), row(LANES), row(LANES)],
        out_specs=[row(lq), row(lkv), row(LANES)],
        out_shape=[jax.ShapeDtypeStruct((t_all, lq), BF16), jax.ShapeDtypeStruct((t_all, lkv), BF16),
                   jax.ShapeDtypeStruct((t_all, LANES), BF16)],
        compiler_params=_params(("parallel",)),
    )(proj, qn, kvn, cos, sin)


def _mla_prep_bwd(dcq, dckv, dkr_h, proj, qn, kvn, cos, sin, *, name, tm, lq, lkv):
    t_all, w = proj.shape
    n_heads = dkr_h.shape[0]

    def body(dcq_ref, dckv_ref, dkr_ref, p_ref, qn_ref, kvn_ref, cos_ref, sin_ref, dp_ref, dqn_ref, dkvn_ref):
        pv = p_ref[...]
        dxq, dqn = _rms_bwd_math(dcq_ref[...], pv[:, :lq], qn_ref[...])
        dxkv, dkvn = _rms_bwd_math(dckv_ref[...], pv[:, lq:lq + lkv], kvn_ref[...])
        dkr = dkr_ref[0]
        for hh in range(1, n_heads):
            dkr = dkr + dkr_ref[hh]
        dkr = _unrope(dkr, cos_ref[...], sin_ref[...])
        dp_ref[...] = jnp.concatenate([dxq, dxkv, dkr], axis=1).astype(BF16)

        @pl.when(pl.program_id(0) == 0)
        def _():
            dqn_ref[...] = dqn
            dkvn_ref[...] = dkvn

        @pl.when(pl.program_id(0) > 0)
        def _():
            dqn_ref[...] += dqn
            dkvn_ref[...] += dkvn

    def row(width):
        return pl.BlockSpec((tm, width), lambda i: (i, 0))

    def vec(width):
        return pl.BlockSpec((1, width), lambda i: (0, 0))

    return pl.pallas_call(
        body, name=name, grid=(t_all // tm,),
        in_specs=[row(lq), row(lkv), pl.BlockSpec((n_heads, tm, LANES), lambda i: (0, i, 0)), row(w),
                  vec(lq), vec(lkv), row(LANES), row(LANES)],
        out_specs=[row(w), vec(lq), vec(lkv)],
        out_shape=[jax.ShapeDtypeStruct((t_all, w), BF16), jax.ShapeDtypeStruct((1, lq), F32),
                   jax.ShapeDtypeStruct((1, lkv), F32)],
        compiler_params=_params(("arbitrary",)),
    )(dcq, dckv, dkr_h, proj, qn, kvn, cos, sin)


def _rope_q_epilogue(acc, cos_ref, sin_ref):
    parts = []
    for g in range(acc.shape[1] // LANES):
        blk = acc[:, g * LANES:(g + 1) * LANES]
        parts.append(_rope(blk, cos_ref[...], sin_ref[...]) if g % 2 == 1 else blk)
    return jnp.concatenate(parts, axis=1)


def _chunk_causal(rows, cols, row0=0):
    r = row0 + lax.broadcasted_iota(jnp.int32, (rows, cols), 0)
    c = lax.broadcasted_iota(jnp.int32, (rows, cols), 1)
    return (c >> 6) <= (r >> 6)


def _meta_keys(rows, cols):
    return lax.broadcasted_iota(jnp.int32, (rows, cols), 1) < N_META


def _attn_fwd(q, kv, kr, *, name, n_heads, tq, n_real, scale):
    t_all = q.shape[0]
    nq = (n_real - N_META) // tq
    assert N_META + nq * tq == n_real and tq % CHUNK == 0 and t_all >= LANES
    n_pad = t_all - n_real
    sub = tq // 2 if (tq // 2) % CHUNK == 0 else tq

    def body(q_ref, kv_ref, kr_ref, o_ref, lse_ref, k_scr, m_scr, l_scr, acc_scr):
        k_scr[:, :QK_NOPE] = kv_ref[:, :QK_NOPE]
        k_scr[:, QK_NOPE:] = kr_ref[...]
        if n_pad:
            o_ref[pl.ds(n_real, n_pad), :] = jnp.zeros((n_pad, V_HEAD), o_ref.dtype)
            lse_ref[pl.ds(n_real, n_pad), :] = jnp.zeros((n_pad, LANES), F32)

        def scores(qt, c0, width):
            return lax.dot_general(qt, k_scr[pl.ds(c0, width), :], _NT, preferred_element_type=F32) * scale

        def values(c0, width):
            return kv_ref[pl.ds(c0, width), QK_NOPE:]

        s = jnp.where(_meta_keys(LANES, LANES), scores(q_ref[pl.ds(0, LANES), :], 0, LANES), NEG_BIG)
        m = jnp.max(s, axis=-1, keepdims=True)
        p = jnp.exp(s - m)
        l = jnp.sum(p, axis=-1, keepdims=True)
        o_meta = jnp.dot(p.astype(BF16), values(0, LANES), preferred_element_type=F32) / l
        o_ref[pl.ds(0, N_META), :] = o_meta[:N_META].astype(o_ref.dtype)
        lse_ref[pl.ds(0, N_META), :] = jnp.broadcast_to((m + jnp.log(l))[:N_META], (N_META, LANES))

        parts = [(u * sub, sub) for u in range(tq // sub)]

        def accumulate(u0, s, c0, width):
            rows = pl.ds(u0, s.shape[0])
            m_prev = m_scr[rows, :]
            m_new = jnp.maximum(m_prev, jnp.max(s, axis=-1, keepdims=True))
            alpha = jnp.exp(m_prev - m_new)
            p = jnp.exp(s - m_new)
            l_scr[rows, :] = alpha * l_scr[rows, :] + jnp.sum(p, axis=-1, keepdims=True)
            acc_scr[rows, :] = alpha * acc_scr[rows, :] + jnp.dot(p.astype(BF16), values(c0, width), preferred_element_type=F32)
            m_scr[rows, :] = m_new

        def q_tile(i, carry):
            r0 = pl.multiple_of(N_META + i * tq, N_META)
            qts = [q_ref[pl.ds(r0 + u0, rows), :] for u0, rows in parts]
            for (u0, rows), qt in zip(parts, qts):
                s = jnp.where(_meta_keys(rows, LANES), scores(qt, 0, LANES), NEG_BIG)
                m = jnp.max(s, axis=-1, keepdims=True)
                p = jnp.exp(s - m)
                m_scr[pl.ds(u0, rows), :] = m
                l_scr[pl.ds(u0, rows), :] = jnp.sum(p, axis=-1, keepdims=True)
                acc_scr[pl.ds(u0, rows), :] = jnp.dot(p.astype(BF16), values(0, LANES), preferred_element_type=F32)

            def full_block(j, c):
                c0 = pl.multiple_of(N_META + j * tq, N_META)
                for (u0, _), qt in zip(parts, qts):
                    accumulate(u0, scores(qt, c0, tq), c0, tq)
                return c

            lax.fori_loop(0, i, full_block, 0)
            for (u0, rows), qt in zip(parts, qts):
                width = u0 + rows
                accumulate(u0, jnp.where(_chunk_causal(rows, width, u0), scores(qt, r0, width), NEG_BIG), r0, width)
            o_ref[pl.ds(r0, tq), :] = (acc_scr[...] / l_scr[...]).astype(o_ref.dtype)
            lse_ref[pl.ds(r0, tq), :] = jnp.broadcast_to(m_scr[...] + jnp.log(l_scr[...]), (tq, LANES))
            return carry

        lax.fori_loop(0, nq, q_tile, 0)

    def head(width):
        return pl.BlockSpec((t_all, width), lambda h: (0, h))

    return pl.pallas_call(
        body, name=name, grid=(n_heads,),
        in_specs=[head(HEAD_W), head(HEAD_W), pl.BlockSpec((t_all, LANES), lambda h: (0, 0))],
        out_specs=[head(V_HEAD), pl.BlockSpec((None, t_all, LANES), lambda h: (h, 0, 0))],
        out_shape=[jax.ShapeDtypeStruct((t_all, n_heads * V_HEAD), BF16),
                   jax.ShapeDtypeStruct((n_heads, t_all, LANES), F32)],
        scratch_shapes=[pltpu.VMEM((t_all, HEAD_W), BF16), pltpu.VMEM((tq, 1), F32), pltpu.VMEM((tq, 1), F32),
                        pltpu.VMEM((tq, V_HEAD), F32)],
        compiler_params=_params(("parallel",)),
    )(q, kv, kr)


def _attn_bwd(q, kv, kr, o, lse, do, cos, sin, *, name, n_heads, tq, n_real, scale):
    t_all = q.shape[0]
    nq = (n_real - N_META) // tq
    assert N_META + nq * tq == n_real and tq % CHUNK == 0 and t_all >= LANES
    n_pad = t_all - n_real

    def body(q_ref, kv_ref, kr_ref, o_ref, lse_ref, do_ref, cos_ref, sin_ref, dq_ref, dkv_ref, dkr_ref,
             k_scr, dk_scr, dv_scr, dq_scr):
        k_scr[:, :QK_NOPE] = kv_ref[:, :QK_NOPE]
        k_scr[:, QK_NOPE:] = kr_ref[...]
        dk_scr[...] = jnp.zeros(dk_scr.shape, F32)
        dv_scr[...] = jnp.zeros(dv_scr.shape, F32)
        if n_pad:
            dq_ref[pl.ds(n_real, n_pad), :] = jnp.zeros((n_pad, HEAD_W), dq_ref.dtype)

        def block(qt, dot, lse_t, delta, c0, width, mask):
            kb = k_scr[pl.ds(c0, width), :]
            s = lax.dot_general(qt, kb, _NT, preferred_element_type=F32) * scale
            p = jnp.exp(s - lse_t)
            if mask is not None:
                p = jnp.where(mask, p, 0.0)
            dp = lax.dot_general(dot, kv_ref[pl.ds(c0, width), QK_NOPE:], _NT, preferred_element_type=F32)
            ds = (p * (dp - delta) * scale).astype(BF16)
            dv_scr[pl.ds(c0, width), :] += lax.dot_general(p.astype(BF16), dot, _TN, preferred_element_type=F32)
            dk_scr[pl.ds(c0, width), :] += lax.dot_general(ds, qt, _TN, preferred_element_type=F32)
            return jnp.dot(ds, kb, preferred_element_type=F32)

        def write_dq(r0, rows, dq):
            cs, sn = cos_ref[pl.ds(r0, rows), :], sin_ref[pl.ds(r0, rows), :]
            dq_ref[pl.ds(r0, rows), :] = jnp.concatenate(
                [dq[:, :QK_NOPE], _unrope(dq[:, QK_NOPE:], cs, sn)], axis=1).astype(dq_ref.dtype)

        rows_m = lax.broadcasted_iota(jnp.int32, (LANES, LANES), 0) < N_META
        dot = do_ref[pl.ds(0, LANES), :]
        delta = jnp.sum(dot.astype(F32) * o_ref[pl.ds(0, LANES), :].astype(F32), axis=-1, keepdims=True)
        dq = block(q_ref[pl.ds(0, LANES), :], dot, lse_ref[pl.ds(0, LANES), :1], delta, 0, LANES,
                   _meta_keys(LANES, LANES) & rows_m)
        write_dq(0, N_META, dq[:N_META])

        def q_tile(i, carry):
            r0 = pl.multiple_of(N_META + i * tq, N_META)
            qt = q_ref[pl.ds(r0, tq), :]
            dot = do_ref[pl.ds(r0, tq), :]
            lse_t = lse_ref[pl.ds(r0, tq), :1]
            delta = jnp.sum(dot.astype(F32) * o_ref[pl.ds(r0, tq), :].astype(F32), axis=-1, keepdims=True)
            dq_scr[...] = block(qt, dot, lse_t, delta, 0, LANES, _meta_keys(tq, LANES))

            def full_block(j, c):
                c0 = pl.multiple_of(N_META + j * tq, N_META)
                dq_scr[...] += block(qt, dot, lse_t, delta, c0, tq, None)
                return c

            lax.fori_loop(0, i, full_block, 0)
            dq_scr[...] += block(qt, dot, lse_t, delta, r0, tq, _chunk_causal(tq, tq))
            write_dq(r0, tq, dq_scr[...])
            return carry

        lax.fori_loop(0, nq, q_tile, 0)
        dk = dk_scr[...]
        dkv_ref[...] = jnp.concatenate([dk[:, :QK_NOPE], dv_scr[...]], axis=1).astype(dkv_ref.dtype)
        dkr_ref[...] = dk[:, QK_NOPE:]

    def head(width):
        return pl.BlockSpec((t_all, width), lambda h: (0, h))

    table = pl.BlockSpec((t_all, LANES), lambda h: (0, 0))
    per_head = pl.BlockSpec((None, t_all, LANES), lambda h: (h, 0, 0))
    return pl.pallas_call(
        body, name=name, grid=(n_heads,),
        in_specs=[head(HEAD_W), head(HEAD_W), table, head(V_HEAD), per_head, head(V_HEAD), table, table],
        out_specs=[head(HEAD_W), head(HEAD_W), per_head],
        out_shape=[jax.ShapeDtypeStruct((t_all, n_heads * HEAD_W), BF16), jax.ShapeDtypeStruct((t_all, n_heads * HEAD_W), BF16),
                   jax.ShapeDtypeStruct((n_heads, t_all, LANES), F32)],
        scratch_shapes=[pltpu.VMEM((t_all, HEAD_W), BF16), pltpu.VMEM((t_all, HEAD_W), F32), pltpu.VMEM((t_all, V_HEAD), F32),
                        pltpu.VMEM((tq, HEAD_W), F32)],
        compiler_params=_params(("parallel",)),
    )(q, kv, kr, o, lse, do, cos, sin)


LRU_ROWS = 128


def _shifted_back(ref, t0, rows, shift_max):
    main = ref[pl.ds(t0, rows), :]
    prev = ref[pl.ds(pl.multiple_of(jnp.maximum(t0 - SUBLANES, 0), SUBLANES), SUBLANES), :]
    prev = jnp.where(t0 > 0, prev, 0.0)
    ext = jnp.concatenate([prev, main], axis=0)
    return [main] + [pltpu.roll(ext, s, 0)[SUBLANES:, :] for s in range(1, shift_max + 1)]


def _shifted_ahead(ref, t0, rows, t_all, shift_max):
    main = ref[pl.ds(t0, rows), :]
    nxt = ref[pl.ds(pl.multiple_of(jnp.minimum(t0 + rows, t_all - SUBLANES), SUBLANES), SUBLANES), :]
    nxt = jnp.where(t0 + rows < t_all, nxt, 0.0)
    ext = jnp.concatenate([main, nxt], axis=0)
    return [main] + [pltpu.roll(ext, rows + SUBLANES - s, 0)[:rows, :] for s in range(1, shift_max + 1)]


def _conv_fwd(xp_ref, t0, rows, cw, cb):
    sh = _shifted_back(xp_ref, t0, rows, 3)
    out = cb + cw[3:4, :] * sh[0]
    for k in range(3):
        out = out + cw[k:k + 1, :] * sh[3 - k]
    return out, sh


def _lru_gates(xb, wga, bga, wgx, bgx, sp):
    xbb = xb.astype(BF16)
    r = _sigmoid(jnp.dot(xbb, wga, preferred_element_type=F32) + bga)
    ig = _sigmoid(jnp.dot(xbb, wgx, preferred_element_type=F32) + bgx)
    la = -LRU_C * r * sp
    a = jnp.exp(la)
    s = jnp.sqrt(_neg_expm1(2.0 * la))
    return xbb, r, ig, a, s


def _scan_tile(a, b, reverse):
    rows = a.shape[0]
    ridx = lax.broadcasted_iota(jnp.int32, a.shape, 0)
    s = 1
    while s < rows:
        if reverse:
            keep = ridx < rows - s
            a_sh, b_sh = pltpu.roll(a, rows - s, 0), pltpu.roll(b, rows - s, 0)
        else:
            keep = ridx >= s
            a_sh, b_sh = pltpu.roll(a, s, 0), pltpu.roll(b, s, 0)
        b = jnp.where(keep, a * b_sh + b, b)
        a = jnp.where(keep, a * a_sh, a)
        s *= 2
    return a, b


def _lru_fwd(xy, conv_w, conv_b, wga, bga, wgx, bgx, lam, *, name):
    t_all = xy.shape[0]
    dr = xy.shape[1] // 2
    c = LANES
    nblk = dr // c
    rows = LRU_ROWS
    nt = t_all // rows

    def body(xp_ref, yp_ref, cw_ref, cb_ref, wga_ref, bga_ref, wgx_ref, bgx_ref, lam_ref, hs_ref, hsy_ref):
        cw, cb = cw_ref[...], cb_ref[...]
        sp = _softplus_neg(lam_ref[...])

        def tile(t, h_in):
            t0 = pl.multiple_of(t * rows, rows)
            xb, _ = _conv_fwd(xp_ref, t0, rows, cw, cb)
            _, _, ig, a, s = _lru_gates(xb, wga_ref[0], bga_ref[...], wgx_ref[0], bgx_ref[...], sp)
            cum_a, h0 = _scan_tile(a, s * (ig * xb), reverse=False)
            hs = cum_a * h_in + h0
            hs_ref[pl.ds(t0, rows), :] = hs
            hsy_ref[pl.ds(t0, rows), :] = (hs * _gelu(yp_ref[pl.ds(t0, rows), :])).astype(BF16)
            return hs[rows - 1:, :]

        lax.fori_loop(0, nt, tile, jnp.zeros((1, c), F32))

    col = pl.BlockSpec((t_all, c), lambda b: (0, b))
    vec = pl.BlockSpec((1, c), lambda b: (0, b))
    wsp = pl.BlockSpec((1, c, c), lambda b: (b, 0, 0))
    return pl.pallas_call(
        body, name=name, grid=(nblk,),
        in_specs=[col, pl.BlockSpec((t_all, c), lambda b: (0, nblk + b)), pl.BlockSpec((4, c), lambda b: (0, b)), vec,
                  wsp, vec, wsp, vec, vec],
        out_specs=[col, col],
        out_shape=[jax.ShapeDtypeStruct((t_all, dr), F32), jax.ShapeDtypeStruct((t_all, dr), BF16)],
        compiler_params=_params(("parallel",)),
    )(xy, xy, conv_w, conv_b, wga, bga, wgx, bgx, lam)


def _lru_bwd(xy, hs, dhsy, conv_w, conv_b, wga, bga, wgx, bgx, lam, *, name):
    t_all = xy.shape[0]
    dr = xy.shape[1] // 2
    c = LANES
    nblk = dr // c
    rows = LRU_ROWS
    nt = t_all // rows

    def body(xp_ref, yp_ref, hs_ref, dh_ref, cw_ref, cb_ref, wga_ref, bga_ref, wgx_ref, bgx_ref, lam_ref,
             dxp_ref, dyp_ref, dcw_ref, dcb_ref, dwga_ref, dbga_ref, dwgx_ref, dbgx_ref, dlam_ref,
             xb_scr, r_scr, i_scr, a_scr):
        cw, cb = cw_ref[...], cb_ref[...]
        lamv = lam_ref[...]
        sp = _softplus_neg(lamv)
        sig_neg = 1.0 / (1.0 + jnp.exp(lamv))
        wga_v, wgx_v = wga_ref[0], wgx_ref[0]

        def recompute(t, carry):
            t0 = pl.multiple_of(t * rows, rows)
            xb, _ = _conv_fwd(xp_ref, t0, rows, cw, cb)
            _, r, ig, a, _ = _lru_gates(xb, wga_v, bga_ref[...], wgx_v, bgx_ref[...], sp)
            xb_scr[pl.ds(t0, rows), :] = xb
            r_scr[pl.ds(t0, rows), :] = r
            i_scr[pl.ds(t0, rows), :] = ig
            a_scr[pl.ds(t0, rows), :] = a
            return carry

        lax.fori_loop(0, nt, recompute, 0)
        dwga_ref[...] = jnp.zeros(dwga_ref.shape, F32)
        dwgx_ref[...] = jnp.zeros(dwgx_ref.shape, F32)

        def tile(ti, carry):
            lam_in, dbga, dbgx, dlam, dcw, dcb = carry
            t = nt - 1 - ti
            t0 = pl.multiple_of(t * rows, rows)
            a_now, a_next = _shifted_ahead(a_scr, t0, rows, t_all, 1)
            yp = yp_ref[pl.ds(t0, rows), :]
            dhy = dh_ref[pl.ds(t0, rows), :]
            cum_a, lam0 = _scan_tile(a_next, dhy * _gelu(yp), reverse=True)
            lam_t = cum_a * lam_in + lam0
            hs_now, hs_prev = _shifted_back(hs_ref, t0, rows, 1)
            da = lam_t * hs_prev
            xb = xb_scr[pl.ds(t0, rows), :]
            r = r_scr[pl.ds(t0, rows), :]
            ig = i_scr[pl.ds(t0, rows), :]
            la = -LRU_C * r * sp
            s = jnp.sqrt(_neg_expm1(2.0 * la))
            d_ixb = lam_t * s
            dla = da * a_now - (lam_t * ig * xb) * (a_now * a_now / s)
            dzr = dla * (-LRU_C * sp) * r * (1.0 - r)
            dzi = d_ixb * xb * ig * (1.0 - ig)
            dzr_b, dzi_b = dzr.astype(BF16), dzi.astype(BF16)
            xbb = xb.astype(BF16)
            dwga_ref[0] += lax.dot_general(xbb, dzr_b, _TN, preferred_element_type=F32)
            dwgx_ref[0] += lax.dot_general(xbb, dzi_b, _TN, preferred_element_type=F32)
            dxb = (d_ixb * ig + lax.dot_general(dzr_b, wga_v, _NT, preferred_element_type=F32)
                   + lax.dot_general(dzi_b, wgx_v, _NT, preferred_element_type=F32))
            xb_scr[pl.ds(t0, rows), :] = dxb
            dyp_ref[pl.ds(t0, rows), :] = (dhy * hs_now * _gelu_grad(yp)).astype(BF16)
            ahead = _shifted_ahead(xb_scr, t0, rows, t_all, 3)
            dxp = cw[3:4, :] * ahead[0]
            for k in range(3):
                dxp = dxp + cw[k:k + 1, :] * ahead[3 - k]
            dxp_ref[pl.ds(t0, rows), :] = dxp.astype(BF16)
            back = _shifted_back(xp_ref, t0, rows, 3)
            dcw_t = jnp.concatenate([jnp.sum(dxb * back[3 - k], axis=0, keepdims=True) for k in range(4)], axis=0)
            return (lam_t[:1, :], dbga + jnp.sum(dzr, axis=0, keepdims=True), dbgx + jnp.sum(dzi, axis=0, keepdims=True),
                    dlam + jnp.sum(dla * r, axis=0, keepdims=True), dcw + dcw_t, dcb + jnp.sum(dxb, axis=0, keepdims=True))

        zero = jnp.zeros((1, c), F32)
        _, dbga, dbgx, dlam, dcw, dcb = lax.fori_loop(0, nt, tile, (zero, zero, zero, zero, jnp.zeros((4, c), F32), zero))
        dbga_ref[...] = dbga
        dbgx_ref[...] = dbgx
        dlam_ref[...] = dlam * (LRU_C * sig_neg)
        dcw_ref[...] = dcw
        dcb_ref[...] = dcb

    col = pl.BlockSpec((t_all, c), lambda b: (0, b))
    col2 = pl.BlockSpec((t_all, c), lambda b: (0, nblk + b))
    vec = pl.BlockSpec((1, c), lambda b: (0, b))
    tap = pl.BlockSpec((4, c), lambda b: (0, b))
    wsp = pl.BlockSpec((1, c, c), lambda b: (b, 0, 0))
    vshape = jax.ShapeDtypeStruct((1, dr), F32)
    wshape = jax.ShapeDtypeStruct((nblk, c, c), F32)
    dxp, dyp, dcw, dcb, dwga, dbga, dwgx, dbgx, dlam = pl.pallas_call(
        body, name=name, grid=(nblk,),
        in_specs=[col, col2, col, col, tap, vec, wsp, vec, wsp, vec, vec],
        out_specs=[col, col, tap, vec, wsp, vec, wsp, vec, vec],
        out_shape=[jax.ShapeDtypeStruct((t_all, dr), BF16), jax.ShapeDtypeStruct((t_all, dr), BF16),
                   jax.ShapeDtypeStruct((4, dr), F32), vshape, wshape, vshape, wshape, vshape, vshape],
        scratch_shapes=[pltpu.VMEM((t_all, c), F32)] * 4,
        compiler_params=_params(("parallel",)),
    )(xy, xy, hs, dhsy, conv_w, conv_b, wga, bga, wgx, bgx, lam)
    return jnp.concatenate([dxp, dyp], axis=1), dcw, dcb, dwga, dbga, dwgx, dbgx, dlam


def _mesh_pos():
    return lax.axis_index("x"), lax.axis_index("y"), lax.axis_index("c")


def _all_gather(shards, *, name):
    n = len(shards)

    def body(*refs):
        ins, outs, token = refs[:n], refs[n:2 * n], refs[2 * n]
        send_sems, recv_sems, local_sems = refs[2 * n + 1:]
        token[...] = jnp.zeros(token.shape, token.dtype)
        x, y, c = _mesh_pos()
        me, sibling = (x, y, c), (x, y, 1 - c)
        chips = [(1 - x, y), (x, 1 - y), (1 - x, 1 - y)]
        slot = _slot

        def copy(a, k, block, to, src=None):
            dst = outs[a].at[slot(block)]
            return pltpu.make_async_remote_copy(
                src_ref=dst if src is None else src, dst_ref=dst, send_sem=send_sems.at[a, k],
                recv_sem=recv_sems.at[a, k], device_id=to, device_id_type=MESH)

        mine = [pltpu.make_async_copy(ins[a], outs[a].at[slot(me)], local_sems.at[a]) for a in range(n)]
        for cp in mine:
            cp.start()
        first = []
        for a in range(n):
            first.append(copy(a, 0, me, sibling, src=ins[a]))
            first += [copy(a, 1 + j, me, (*chip, c), src=ins[a]) for j, chip in enumerate(chips)]
        for cp in first:
            cp.start()
        passed = []
        for a in range(n):
            for j, chip in enumerate(chips):
                copy(a, 1 + j, (*chip, c), me).wait_recv()
                fwd = copy(a, 4 + j, (*chip, c), sibling)
                fwd.start()
                passed.append(fwd)
        for a in range(n):
            copy(a, 0, sibling, me).wait_recv()
            for j, chip in enumerate(chips):
                copy(a, 4 + j, (*chip, 1 - c), me).wait_recv()
        for cp in first + passed:
            cp.wait_send()
        for cp in mine:
            cp.wait()

    any_spec = pl.BlockSpec(memory_space=pl.ANY)
    outs = pl.pallas_call(
        body, name=name,
        in_specs=[any_spec] * n, out_specs=[any_spec] * n + [pl.BlockSpec(memory_space=pltpu.VMEM)],
        out_shape=[jax.ShapeDtypeStruct((N_DEV,) + s.shape, s.dtype) for s in shards]
        + [jax.ShapeDtypeStruct((SUBLANES, LANES), F32)],
        scratch_shapes=[pltpu.SemaphoreType.DMA((n, 7)), pltpu.SemaphoreType.DMA((n, 7)), pltpu.SemaphoreType.DMA((n,))],
    )(*shards)
    return list(outs[:n]), outs[n][0, 0]


_HBM = pl.BlockSpec(memory_space=pltpu.HBM)
_SEM = pl.BlockSpec(memory_space=pltpu.SEMAPHORE)
_ANY = pl.BlockSpec(memory_space=pl.ANY)
_EFFECT = pltpu.SideEffectType.DATAFLOW_SIDE_EFFECTING


def _slot(p):
    return 4 * p[0] + 2 * p[1] + p[2]


def _remote(src, dst, send, recv, idx, to):
    return pltpu.make_async_remote_copy(src_ref=src, dst_ref=dst, send_sem=send.at[idx], recv_sem=recv.at[idx],
                                        device_id=to, device_id_type=MESH)


def _ag_plan_own(a, src, land, send, recv):
    x, y, c = _mesh_pos()
    dst = land.at[_slot((x, y, c))]
    targets = [(x, y, 1 - c), (1 - x, y, c), (x, 1 - y, c), (1 - x, 1 - y, c)]
    return [_remote(src, dst, send, recv, 4 * a + k, to) for k, to in enumerate(targets)]


def _ag_plan_pass(a, src, land, send, recv):
    x, y, c = _mesh_pos()
    blocks = [land.at[_slot((px, py, c))] for px, py in ((1 - x, y), (x, 1 - y), (1 - x, 1 - y))]
    return [_remote(blk, blk, send, recv, 3 * a + k, (x, y, 1 - c)) for k, blk in enumerate(blocks)]


def _rs_plan_sibling(a, src, land, send, recv):
    x, y, c = _mesh_pos()
    return [_remote(src.at[2 * j + (1 - c)], land.at[j], send, recv, 4 * a + j, (x, y, 1 - c)) for j in range(4)]


def _rs_plan_chips(a, src, land, send, recv):
    x, y, c = _mesh_pos()
    out = []
    for k in (1, 2, 3):
        px = 1 - x if k & 2 else x
        py = 1 - y if k & 1 else y
        out.append(_remote(src.at[2 * px + py], land.at[k - 1], send, recv, 3 * a + k - 1, (px, py, c)))
    return out


def _in_hbm(a):
    return pltpu.with_memory_space_constraint(a, pltpu.HBM)


def _exchange_start(srcs, lands, plan, n_k, *, name):
    ns, n = len(srcs), len(lands)

    def body(*refs):
        src_refs, land_refs = refs[:ns], refs[ns:ns + n]
        send, recv = refs[ns + n], refs[ns + n + 1]
        token = refs[-1]
        for a in range(n):
            for cp in plan(a, src_refs[a] if ns else None, land_refs[a], send, recv):
                cp.start()
        token[...] = jnp.zeros(token.shape, token.dtype)

    bufs = list(srcs) + list(lands)
    outs = pl.pallas_call(
        body, name=name,
        out_shape=(pltpu.SemaphoreType.DMA((n * n_k,)), pltpu.SemaphoreType.DMA((n * n_k,)),
                   *[pltpu.HBM(b.shape, b.dtype) for b in bufs], jax.ShapeDtypeStruct((SUBLANES, LANES), F32)),
        in_specs=[_HBM] * (ns + n),
        out_specs=(_SEM, _SEM, *[_HBM] * (ns + n), pl.BlockSpec(memory_space=pltpu.VMEM)),
        input_output_aliases={i: 2 + i for i in range(ns + n)},
        compiler_params=pltpu.CompilerParams(has_side_effects=_EFFECT),
    )(*[_in_hbm(b) for b in bufs])
    return outs[0], outs[1], list(outs[2:2 + ns]), list(outs[2 + ns:2 + ns + n]), outs[-1]


def _exchange_wait(started, plan, after, *, name):
    send, recv, srcs, lands, _ = started
    ns, n = len(srcs), len(lands)

    def body(*refs):
        src_refs, land_refs = refs[:ns], refs[ns:ns + n]
        send_ref, recv_ref = refs[ns + n], refs[ns + n + 1]
        for a in range(n):
            for cp in plan(a, src_refs[a] if ns else None, land_refs[a], send_ref, recv_ref):
                cp.wait_send()
                cp.wait_recv()

    bufs = list(srcs) + list(lands)
    outs = pl.pallas_call(
        body, name=name,
        out_shape=tuple(pltpu.HBM(b.shape, b.dtype) for b in bufs),
        in_specs=[_HBM] * (ns + n) + [_SEM, _SEM, _ANY],
        out_specs=tuple([_HBM] * (ns + n)),
        input_output_aliases={i: i for i in range(ns + n)},
        compiler_params=pltpu.CompilerParams(has_side_effects=_EFFECT),
    )(*bufs, send, recv, after)
    return list(outs[:ns]), list(outs[ns:])


def _pair_add(grads, landed, core, *, name, tr):
    _, r_all, c_all = grads.shape

    def body(core_ref, g_ref, l_ref, o_ref):
        o_ref[...] = (g_ref[...].astype(F32) + l_ref[...].astype(F32)).astype(o_ref.dtype)

    return pl.pallas_call(
        body, name=name,
        grid_spec=pltpu.PrefetchScalarGridSpec(
            num_scalar_prefetch=1, grid=(4, r_all // tr),
            in_specs=[pl.BlockSpec((None, tr, c_all), lambda j, i, core_ref: (2 * j + core_ref[0], i, 0)),
                      pl.BlockSpec((None, tr, c_all), lambda j, i, core_ref: (j, i, 0))],
            out_specs=pl.BlockSpec((None, tr, c_all), lambda j, i, core_ref: (j, i, 0))),
        out_shape=jax.ShapeDtypeStruct((4, r_all, c_all), grads.dtype),
        compiler_params=_params(("parallel", "parallel")),
    )(core, grads, landed)


def _adamw_math(w, g, m, v):
    m2 = ADAM_B1 * m + (1.0 - ADAM_B1) * g
    v2 = ADAM_B2 * v + (1.0 - ADAM_B2) * (g * g)
    m_hat = m2 / (1.0 - ADAM_B1 ** ADAM_STEP)
    v_hat = v2 / (1.0 - ADAM_B2 ** ADAM_STEP)
    delta = -ADAM_LR * (m_hat / (jnp.sqrt(v_hat) + ADAM_EPS) + ADAM_WD * w)
    return delta, m2, v2


def _adamw(w, m, v, terms, order, *, name, tr, col_block=None, own=None, stack=None):
    r_all, c_all = w.shape
    n_slots = terms.shape[0]

    def body(*refs):
        if col_block is not None or own is not None:
            refs = refs[1:]
        own_ref = None
        if own is not None:
            own_ref, refs = refs[0], refs[1:]
        w_ref, m_ref, v_ref, t_ref, g_ref, d_ref, m2_ref, v2_ref = refs
        if own_ref is not None:
            g = own_ref[...].astype(F32) + t_ref[order[0]].astype(F32)
        else:
            g = t_ref[order[0]].astype(F32)
        for s in order[1:]:
            g = g + t_ref[s].astype(F32)
        delta, m2, v2 = _adamw_math(w_ref[...], g, m_ref[...], v_ref[...])
        g_ref[...] = g
        d_ref[...] = delta
        m2_ref[...] = m2
        v2_ref[...] = v2

    shape = jax.ShapeDtypeStruct((r_all, c_all), F32)
    if own is not None:
        layer, n_layers, prev = stack
        row = pl.BlockSpec((tr, c_all), lambda i, idx: (i, 0))
        slab = pl.BlockSpec((None, tr, c_all), lambda i, idx: (layer, i, 0))
        carried = [] if prev is None else list(prev)

        def stacked_body(*refs):
            body(*refs[:6], *refs[6 + len(carried):])

        return pl.pallas_call(
            stacked_body, name=name,
            grid_spec=pltpu.PrefetchScalarGridSpec(
                num_scalar_prefetch=1, grid=(r_all // tr,),
                in_specs=[pl.BlockSpec((None, tr, c_all), lambda i, idx: (idx[0], i, 0)), row, row, row,
                          pl.BlockSpec((n_slots, tr, c_all), lambda i, idx: (0, i, 0))] + [_ANY] * len(carried),
                out_specs=[slab] * 4),
            out_shape=[jax.ShapeDtypeStruct((n_layers, r_all, c_all), F32)] * 4,
            input_output_aliases={6 + k: k for k in range(len(carried))},
            compiler_params=_params(("parallel",)),
        )(own[1], own[0], w, m, v, terms, *carried)
    if col_block is None:
        row = pl.BlockSpec((tr, c_all), lambda i: (i, 0))
        return pl.pallas_call(
            body, name=name, grid=(r_all // tr,),
            in_specs=[row, row, row, pl.BlockSpec((n_slots, tr, c_all), lambda i: (0, i, 0))],
            out_specs=[row] * 4, out_shape=[shape] * 4, compiler_params=_params(("parallel",)),
        )(w, m, v, terms)
    row = pl.BlockSpec((tr, c_all), lambda i, blk: (i, 0))
    return pl.pallas_call(
        body, name=name,
        grid_spec=pltpu.PrefetchScalarGridSpec(
            num_scalar_prefetch=1, grid=(r_all // tr,),
            in_specs=[row, row, row, pl.BlockSpec((n_slots, tr, c_all), lambda i, blk: (0, i, blk[0]))],
            out_specs=[row] * 4),
        out_shape=[shape] * 4, compiler_params=_params(("parallel",)),
    )(col_block, w, m, v, terms)


def _rope_tables(t_all):
    pos = jnp.arange(t_all, dtype=F32)
    inv_freq = ROPE_THETA ** (-jnp.arange(0, QK_ROPE, 2, dtype=F32) / QK_ROPE)
    ang = pos[:, None] * inv_freq[None, :]
    cos, sin = jnp.cos(ang), jnp.sin(ang)
    return jnp.tile(cos, (1, LANES // (QK_ROPE // 2))), jnp.tile(sin, (1, LANES // (QK_ROPE // 2)))


def _adam_row_tile(r_all, c_all, block_bytes=512 * 1024):
    target = max(SUBLANES, block_bytes // (4 * c_all))
    return _pick(r_all, [t for t in (1024, 704, 512, 352, 256, 176, 128, 64, 32, 16, 8) if t <= target])


def _rows_natural(wg):
    return wg.reshape(wg.shape[0] * wg.shape[1], wg.shape[2])


def _mla_layer_fwd(tag, h, g_mix, ws, qn, kvn, cos, sin, *, tm, tq, n_heads, scale, n_real):
    w_in, w_uq, w_ukv, w_o = _rows_natural(ws[0]), ws[1], ws[2], _rows_natural(ws[3])
    t_all, d = h.shape
    lq, lkv = qn.shape[1], kvn.shape[1]
    tmb = _pick(t_all, _ROW_TILES)
    hn = _rmsnorm_fwd(h, g_mix, name=f"norm_mix{tag}", tm=tm)
    proj = _mm_nn(hn, w_in, name=f"mla_in{tag}", out_dtype=F32, tm=tmb, tn=w_in.shape[1], tk=_pick(d, _DIVS))
    cq, ckv, kr = _mla_prep_fwd(proj, qn, kvn, cos, sin, name=f"mla_prep{tag}", tm=tm, lq=lq, lkv=lkv)
    q = _mm_nn(cq, w_uq, name=f"mla_q{tag}", out_dtype=BF16, tm=tmb, tn=w_uq.shape[2], tk=lq, b_blocked=True,
               epilogue=_rope_q_epilogue, extras=(cos, sin))
    kv = _mm_nn(ckv, w_ukv, name=f"mla_kv{tag}", out_dtype=BF16, tm=tmb, tn=w_ukv.shape[2], tk=lkv, b_blocked=True)
    o, lse = _attn_fwd(q, kv, kr, name=f"attn_fwd{tag}", n_heads=n_heads, tq=tq, n_real=n_real, scale=scale)
    h_mid = _mm_nn(o, w_o, name=f"mla_o{tag}", out_dtype=F32, tm=tmb, tn=_pick(d, _DIVS[2:]), tk=_pick(o.shape[1], _DIVS), res=h)
    return h_mid, (hn, proj, cq, ckv, kr, q, kv, o, lse)


def _mla_layer_bwd(tag, dh, dh_b, h_in, saved, g_mix, ws, qn, kvn, cos, sin, *, tm, tq, n_heads, scale, n_real):
    hn, proj, cq, ckv, kr, q, kv, o, lse = saved
    w_in, w_uq, w_ukv, w_o = _rows_natural(ws[0]), ws[1], ws[2], _rows_natural(ws[3])
    t_all, d = h_in.shape
    lq, lkv = qn.shape[1], kvn.shape[1]
    ov = o.shape[1]
    tmb = _pick(t_all, _ROW_TILES)
    tn_d, tk_d = _pick(d, _DIVS[1:]), _pick(d, _DIVS)
    do = _mm_nt(dh_b, w_o, name=f"mla_do{tag}", out_dtype=BF16, tm=tmb, tn=_pick(ov, _DIVS[1:]), tk=tk_d)
    dw_o = _mm_tn(o, dh_b, name=f"mla_dwo{tag}", out_dtype=BF16, tm=_pick(ov, _DIVS[1:]), tn=tn_d, tk=tmb)
    dq, dkv, dkr_h = _attn_bwd(q, kv, kr, o, lse, do, cos, sin, name=f"attn_bwd{tag}", n_heads=n_heads, tq=tq, n_real=n_real,
                               scale=scale)
    hw, kw = w_uq.shape[2], w_ukv.shape[2]
    dw_uq = _mm_tn(cq, dq, name=f"mla_dwuq{tag}", out_dtype=BF16, tm=lq, tn=hw, tk=tmb, out_block=hw)
    dcq = _mm_nt(dq, w_uq, name=f"mla_dcq{tag}", out_dtype=F32, tm=tmb, tn=lq, tk=hw, b_blocked=True)
    dw_ukv = _mm_tn(ckv, dkv, name=f"mla_dwukv{tag}", out_dtype=BF16, tm=lkv, tn=kw, tk=tmb, out_block=kw)
    dckv = _mm_nt(dkv, w_ukv, name=f"mla_dckv{tag}", out_dtype=F32, tm=tmb, tn=lkv, tk=kw, b_blocked=True)
    dproj, dqn, dkvn = _mla_prep_bwd(dcq, dckv, dkr_h, proj, qn, kvn, cos, sin, name=f"mla_prep_bwd{tag}", tm=tm, lq=lq, lkv=lkv)
    wc = w_in.shape[1]
    dw_in = _mm_tn(hn, dproj, name=f"mla_dwin{tag}", out_dtype=BF16, tm=tn_d, tn=wc, tk=tmb)
    dhn = _mm_nt(dproj, w_in, name=f"mla_dhn{tag}", out_dtype=F32, tm=tmb, tn=tn_d, tk=wc)
    dh, dh_b, dg = _rmsnorm_bwd(dhn, h_in, g_mix, dh, name=f"norm_mix_bwd{tag}", tm=tm)
    return dh, dh_b, dg, dqn, dkvn, [dw_in.reshape(N_DEV, -1, wc), dw_uq, dw_ukv, dw_o.reshape(N_DEV, -1, d)]


def _lru_layer_fwd(tag, h, g_mix, ws, small, *, tm):
    w_lin, w_lo = ws[0], _rows_natural(ws[1])
    t_all, d = h.shape
    dr = w_lo.shape[0]
    tmb = _pick(t_all, _ROW_TILES)
    hn = _rmsnorm_fwd(h, g_mix, name=f"norm_mix{tag}", tm=tm)
    xy = _mm_nn(hn, w_lin, name=f"lru_in{tag}", out_dtype=F32, tm=tmb, tn=w_lin.shape[2], tk=_pick(d, _DIVS), b_blocked=True)
    hs, hsy = _lru_fwd(xy, *small, name=f"lru_fwd{tag}")
    h_mid = _mm_nn(hsy, w_lo, name=f"lru_o{tag}", out_dtype=F32, tm=tmb, tn=_pick(d, _DIVS[2:]), tk=_pick(dr, _DIVS), res=h)
    return h_mid, (hn, xy, hs, hsy)


def _lru_layer_bwd(tag, dh, dh_b, h_in, saved, g_mix, ws, small, *, tm):
    hn, xy, hs, hsy = saved
    w_lin, w_lo = ws[0], _rows_natural(ws[1])
    t_all, d = h_in.shape
    dr = w_lo.shape[0]
    tmb = _pick(t_all, _ROW_TILES)
    tn_d, tk_d = _pick(d, _DIVS[1:]), _pick(d, _DIVS)
    dhsy = _mm_nt(dh_b, w_lo, name=f"lru_dhsy{tag}", out_dtype=F32, tm=tmb, tn=_pick(dr, _DIVS[1:]), tk=tk_d)
    dw_lo = _mm_tn(hsy, dh_b, name=f"lru_dwo{tag}", out_dtype=BF16, tm=_pick(dr, _DIVS[1:]), tn=tn_d, tk=tmb)
    dxy, *dsmall = _lru_bwd(xy, hs, dhsy, *small, name=f"lru_bwd{tag}")
    lw = w_lin.shape[2]
    dw_lin = _mm_tn(hn, dxy, name=f"lru_dwin{tag}", out_dtype=BF16, tm=tn_d, tn=lw, tk=tmb, out_block=lw)
    dhn = _mm_nt(dxy, w_lin, name=f"lru_dhn{tag}", out_dtype=F32, tm=tmb, tn=tn_d, tk=lw, b_blocked=True)
    dh, dh_b, dg = _rmsnorm_bwd(dhn, h_in, g_mix, dh, name=f"norm_mix_bwd{tag}", tm=tm)
    return dh, dh_b, dg, tuple(dsmall), [dw_lin, dw_lo.reshape(N_DEV, -1, d)]


def _ffn_layer_fwd(tag, h_mid, g_ffn, ws, *, tm):
    w_gu, w_down = ws[0], _rows_natural(ws[1])
    t_all, d = h_mid.shape
    f_all = w_down.shape[0]
    tmb = _pick(t_all, _ROW_TILES)
    fk = _pick(f_all, (1408,) + _DIVS[1:])
    hn2 = _rmsnorm_fwd(h_mid, g_ffn, name=f"norm_ffn{tag}", tm=tm)
    gu, act = _ffn_up(hn2, w_gu, name=f"ffn_up{tag}", tm=tm)
    h_out = _mm_nn(act, w_down, name=f"ffn_down{tag}", out_dtype=F32, tm=tmb, tn=_pick(d, _DIVS[1:]), tk=fk, res=h_mid)
    return h_out, (hn2, gu, act)


def _ffn_layer_bwd(tag, dh, dh_b, h_mid, saved, g_ffn, ws, *, tm):
    hn2, gu, act = saved
    w_gu, w_down = ws[0], _rows_natural(ws[1])
    t_all, d = h_mid.shape
    f_all = w_down.shape[0]
    f_local = w_gu.shape[2]
    tmb = _pick(t_all, _ROW_TILES)
    fk = _pick(f_all, (1408,) + _DIVS[1:])
    tn_d, tk_d = _pick(d, _DIVS[1:]), _pick(d, _DIVS)
    dgu = _ffn_dact(dh_b, w_down, gu, name=f"ffn_dact{tag}", tm=tm, tn=f_local)
    dw_down = _mm_tn(act, dh_b, name=f"ffn_dwdown{tag}", out_dtype=BF16, tm=fk, tn=tn_d, tk=tmb)
    dhn2 = _mm_nt(dgu, w_gu, name=f"ffn_dhn{tag}", out_dtype=F32, tm=tmb, tn=tn_d, tk=f_local, b_blocked=True)
    dw_gu = _mm_tn(hn2, dgu, name=f"ffn_dwgu{tag}", out_dtype=BF16, tm=tn_d, tn=f_local, tk=tmb, out_block=f_local)
    dh, dh_b, dg = _rmsnorm_bwd(dhn2, h_mid, g_ffn, dh, name=f"norm_ffn_bwd{tag}", tm=tm)
    return dh, dh_b, dg, [dw_gu, dw_down.reshape(N_DEV, -1, d)]


def kernel(x, meta_tokens, norm_mix, norm_ffn, norm_final, mla_w_in, mla_q_norm, mla_kv_norm, mla_w_uq, mla_w_ukv, mla_w_o, lru_w_in, lru_conv_w, lru_conv_b, lru_w_gate_a, lru_b_gate_a, lru_w_gate_x, lru_b_gate_x, lru_lambda, lru_w_o, ffn_w_gu, ffn_w_down, loss_target, m_meta_tokens, m_norm_mix, m_norm_ffn, m_norm_final, m_mla_w_in, m_mla_q_norm, m_mla_kv_norm, m_mla_w_uq, m_mla_w_ukv, m_mla_w_o, m_lru_w_in, m_lru_conv_w, m_lru_conv_b, m_lru_w_gate_a, m_lru_b_gate_a, m_lru_w_gate_x, m_lru_b_gate_x, m_lru_lambda, m_lru_w_o, m_ffn_w_gu, m_ffn_w_down, v_meta_tokens, v_norm_mix, v_norm_ffn, v_norm_final, v_mla_w_in, v_mla_q_norm, v_mla_kv_norm, v_mla_w_uq, v_mla_w_ukv, v_mla_w_o, v_lru_w_in, v_lru_conv_w, v_lru_conv_b, v_lru_w_gate_a, v_lru_b_gate_a, v_lru_w_gate_x, v_lru_b_gate_x, v_lru_lambda, v_lru_w_o, v_ffn_w_gu, v_ffn_w_down):
    seq, d = x.shape[1], x.shape[2]
    assert seq % CHUNK == 0
    n_real = N_META + seq
    t_all = -(-n_real // LANES) * LANES
    tm = _pick(t_all, (384, 256, 128))
    tq = _pick(seq, (512, 256, 128, 64))
    depth = norm_mix.shape[0]
    n_mla, n_lru = mla_w_in.shape[0], lru_w_in.shape[0]
    lq, lkv = mla_q_norm.shape[1], mla_kv_norm.shape[1]
    w_in_cols = lq + lkv + LANES
    heads_local = mla_w_uq.shape[2] // (QK_NOPE + QK_ROPE)
    n_heads = heads_local * N_DEV
    dr = lru_w_gate_a.shape[1] * lru_w_gate_a.shape[2]
    scale = (QK_NOPE + QK_ROPE) ** -0.5
    cx, cy, cc = _mesh_pos()
    core = jnp.reshape(cc, (1,)).astype(jnp.int32)
    my_slot = jnp.reshape(4 * cx + 2 * cy + cc, (1,)).astype(jnp.int32)

    def pad_cols(w, cols):
        return jnp.pad(w, ((0, 0), (0, cols - w.shape[1])))

    def pad_heads(w):
        k_all = w.shape[0]
        w3 = w.reshape(k_all, heads_local, QK_NOPE + QK_ROPE)
        return jnp.pad(w3, ((0, 0), (0, 0), (0, HEAD_W - QK_NOPE - QK_ROPE))).reshape(k_all, heads_local * HEAD_W)

    def unpad_heads(w):
        k_all = w.shape[0]
        return w.reshape(k_all, heads_local, HEAD_W)[:, :, :QK_NOPE + QK_ROPE].reshape(k_all, -1)

    small_rows = N_META + n_lru * 4 + 2 * n_lru
    small_pad = -(-small_rows // SUBLANES) * SUBLANES

    def pack_small(meta, conv_w, conv_b, lam):
        rows = jnp.concatenate([meta, conv_w.reshape(n_lru * 4, -1), conv_b, lam], axis=0)
        return jnp.pad(rows, ((0, small_pad - small_rows), (0, 0)))

    def unpack_small(p):
        o1 = N_META + n_lru * 4
        return (p[:N_META], p[N_META:o1].reshape(n_lru, 4, -1), p[o1:o1 + n_lru], p[o1 + n_lru:o1 + 2 * n_lru])

    (small_full,), small_done = _all_gather([pack_small(meta_tokens, lru_conv_w, lru_conv_b, lru_lambda)], name="ag_small")
    small_full = jnp.transpose(small_full, (1, 0, 2)).reshape(small_pad, -1)
    meta_full, conv_w_full, conv_b_full, lam_full = unpack_small(small_full)

    def wire(w):
        return (w + small_done).astype(BF16)

    mla_shards, lru_shards, ffn_shards = [], [], []
    for j in range(n_mla):
        mla_shards.append([wire(pad_cols(mla_w_in[j], w_in_cols)), wire(pad_heads(mla_w_uq[j])), wire(mla_w_ukv[j]),
                           wire(mla_w_o[j])])
    for j in range(n_lru):
        lru_shards.append([wire(lru_w_in[j]), wire(lru_w_o[j])])
    for layer in range(depth):
        ffn_shards.append([wire(ffn_w_gu[layer]), wire(ffn_w_down[layer])])

    n_sub = 2 * depth
    groups = []
    for layer in range(depth):
        groups += [mla_shards[layer // 2] if layer % 2 == 0 else lru_shards[layer // 2], ffn_shards[layer]]
    slot_idx = 4 * cx + 2 * cy + cc
    ag_own = []
    for gi, shards in enumerate(groups):
        lands = [lax.dynamic_update_slice(lax.empty((N_DEV,) + s.shape, s.dtype), s[None], (slot_idx, 0, 0)) for s in shards]
        ag_own.append(_exchange_start(shards, lands, _ag_plan_own, 4, name=f"ag{gi}_start"))
    ag_pass = [None] * n_sub
    weights = [None] * n_sub

    def ag_landed(gi, after):
        _, lands = _exchange_wait(ag_own[gi], _ag_plan_own, after, name=f"ag{gi}_wait")
        ag_pass[gi] = _exchange_start([], lands, _ag_plan_pass, 3, name=f"ag{gi}_pass")
        return ag_pass[gi][4][0, 0]

    def ag_done(gi, after):
        _, weights[gi] = _exchange_wait(ag_pass[gi], _ag_plan_pass, after, name=f"ag{gi}_pass_wait")

    cos, sin = _rope_tables(t_all)
    zeros_tail = jnp.zeros((t_all - n_real, d), F32)
    started = ag_own[0][4][0, 0]
    for st in ag_own[1:]:
        started = started + st[4][0, 0]
    h = jnp.concatenate([meta_full + started, x[0], zeros_tail], axis=0)
    target = jnp.concatenate([jnp.zeros((N_META, d), F32), loss_target[0], zeros_tail], axis=0)

    attn_kw = dict(tm=tm, tq=tq, n_heads=n_heads, scale=scale, n_real=n_real)

    def lru_small(j):
        return (conv_w_full[j], conv_b_full[j][None, :], lru_w_gate_a[j].astype(BF16), lru_b_gate_a[j].reshape(1, dr),
                lru_w_gate_x[j].astype(BF16), lru_b_gate_x[j].reshape(1, dr), lam_full[j][None, :])

    def before_sublayer(k, act):
        tok = ag_landed(k, act) if k <= 1 else 0.0
        ag_done(k, act)
        if 1 <= k < n_sub - 1:
            tok = tok + ag_landed(k + 1, act)
        return tok

    saved = []
    for layer in range(depth):
        j = layer // 2
        g_mix = norm_mix[layer][None, :] + before_sublayer(2 * layer, h)
        if layer % 2 == 0:
            h_mid, mix_saved = _mla_layer_fwd(layer, h, g_mix, weights[2 * layer], mla_q_norm[j][None, :],
                                              mla_kv_norm[j][None, :], cos, sin, **attn_kw)
        else:
            h_mid, mix_saved = _lru_layer_fwd(layer, h, g_mix, weights[2 * layer], lru_small(j), tm=tm)
        g_ffn = norm_ffn[layer][None, :] + before_sublayer(2 * layer + 1, h_mid)
        h_out, ffn_saved = _ffn_layer_fwd(layer, h_mid, g_ffn, weights[2 * layer + 1], tm=tm)
        saved.append((h, h_mid, mix_saved, ffn_saved))
        h = h_out

    loss_part, dh, dh_b, dg_final = _loss_head(h, target, norm_final[None, :], name="loss_head", tm=tm, n_real=n_real)
    loss = lax.psum(loss_part[0, 0], ("x", "y", "c"))

    rs_sib, rs_chip, reduced = [None] * n_sub, [None] * n_sub, [None] * n_sub
    chip_idx = jnp.reshape(2 * cx + cy, (1,)).astype(jnp.int32)

    def rs_begin(k, grads):
        lands = [lax.empty((4,) + g.shape[1:], g.dtype) for g in grads]
        rs_sib[k] = _exchange_start(grads, lands, _rs_plan_sibling, 4, name=f"rs{k}_start")
        return rs_sib[k][4][0, 0]

    def rs_middle(k, after):
        grads, landed = _exchange_wait(rs_sib[k], _rs_plan_sibling, after, name=f"rs{k}_wait")
        parts = [_pair_add(g, l, core, name=f"rs{k}_add{a}", tr=_adam_row_tile(g.shape[1], g.shape[2], 4 * 1024 * 1024))
                 for a, (g, l) in enumerate(zip(grads, landed))]
        lands = [lax.empty((3,) + p.shape[1:], p.dtype) for p in parts]
        rs_chip[k] = _exchange_start(parts, lands, _rs_plan_chips, 3, name=f"rs{k}_chips")
        return rs_chip[k][4][0, 0]

    def rs_end(k, after):
        reduced[k] = _exchange_wait(rs_chip[k], _rs_plan_chips, after, name=f"rs{k}_chips_wait")

    d_norm_mix, d_norm_ffn = [None] * depth, [None] * depth
    d_qn, d_kvn = [None] * n_mla, [None] * n_mla
    d_small = {k: [None] * n_lru for k in ("cw", "cb", "wga", "bga", "wgx", "bgx", "lam")}
    tok, waiting = 0.0, None
    for layer in reversed(range(depth)):
        j = layer // 2
        h_in, h_mid, mix_saved, ffn_saved = saved[layer]
        dh, dh_b, d_norm_ffn[layer], ffn_g = _ffn_layer_bwd(layer, dh, dh_b, h_mid, ffn_saved, norm_ffn[layer][None, :] + tok,
                                                            weights[2 * layer + 1], tm=tm)
        tok = rs_begin(2 * layer + 1, ffn_g)
        if waiting is not None:
            tok = tok + rs_middle(waiting, dh)
        waiting = 2 * layer + 1
        if layer == 0:
            tok = tok + rs_middle(waiting, dh)
            waiting = None
        g_mix = norm_mix[layer][None, :] + tok
        if layer % 2 == 0:
            dh, dh_b, d_norm_mix[layer], d_qn[j], d_kvn[j], mix_g = _mla_layer_bwd(
                layer, dh, dh_b, h_in, mix_saved, g_mix, weights[2 * layer], mla_q_norm[j][None, :], mla_kv_norm[j][None, :],
                cos, sin, **attn_kw)
        else:
            dh, dh_b, d_norm_mix[layer], dsmall, mix_g = _lru_layer_bwd(layer, dh, dh_b, h_in, mix_saved, g_mix,
                                                                        weights[2 * layer], lru_small(j), tm=tm)
            for key, val in zip(("cw", "cb", "wga", "bga", "wgx", "bgx", "lam"), dsmall):
                d_small[key][j] = val
        tok = rs_begin(2 * layer, mix_g)
        if waiting is not None:
            tok = tok + rs_middle(waiting, dh)
        waiting = 2 * layer
    rs_middle(waiting, dh)

    grad_x = dh[N_META:n_real][None]

    d_meta = dh[:N_META]
    small_grad = pack_small(d_meta, jnp.stack(d_small["cw"], axis=0), jnp.concatenate(d_small["cb"], axis=0),
                            jnp.concatenate(d_small["lam"], axis=0))
    rep_grads = [
        jnp.concatenate(d_norm_mix, axis=0), jnp.concatenate(d_norm_ffn, axis=0), dg_final,
        jnp.concatenate(d_qn, axis=0), jnp.concatenate(d_kvn, axis=0),
        jnp.stack(d_small["wga"], axis=0).reshape(-1, LANES), jnp.concatenate(d_small["bga"], axis=0),
        jnp.stack(d_small["wgx"], axis=0).reshape(-1, LANES), jnp.concatenate(d_small["bgx"], axis=0),
    ]
    small_srcs = [small_grad] + [jnp.pad(g, ((0, -g.shape[0] % SUBLANES), (0, 0))) for g in rep_grads]
    small_lands = [lax.dynamic_update_slice(lax.empty((N_DEV,) + s.shape, s.dtype), s[None], (slot_idx, 0, 0))
                   for s in small_srcs]
    small_own = _exchange_start(small_srcs, small_lands, _ag_plan_own, 4, name="ag_grads_start")

    res = {}

    def adam_sharded(nm, k, a, idx, n_layers, w, m, v):
        parts, landed = reduced[k]
        r_all, c_all = landed[a].shape[1], landed[a].shape[2]
        res[nm] = _adamw(w.reshape(r_all, c_all), m.reshape(r_all, c_all), v.reshape(r_all, c_all), landed[a], (0, 1, 2),
                         name=f"adamw_{nm}{idx}", tr=_adam_row_tile(r_all, c_all, 2 * 1024 * 1024), own=(parts[a], chip_idx),
                         stack=(idx, n_layers, res.get(nm)))

    after = small_own[4]
    for k in reversed(range(n_sub)):
        rs_end(k, after)
        layer, j = k // 2, k // 4
        if k % 2 == 1:
            adam_sharded("ffn_w_gu", k, 0, layer, depth, ffn_w_gu[layer], m_ffn_w_gu[layer], v_ffn_w_gu[layer])
            adam_sharded("ffn_w_down", k, 1, layer, depth, ffn_w_down[layer], m_ffn_w_down[layer], v_ffn_w_down[layer])
            after = res["ffn_w_down"][0]
        elif layer % 2 == 0:
            adam_sharded("mla_w_in", k, 0, j, n_mla, pad_cols(mla_w_in[j], w_in_cols), pad_cols(m_mla_w_in[j], w_in_cols),
                         pad_cols(v_mla_w_in[j], w_in_cols))
            adam_sharded("mla_w_uq", k, 1, j, n_mla, pad_heads(mla_w_uq[j]), pad_heads(m_mla_w_uq[j]), pad_heads(v_mla_w_uq[j]))
            adam_sharded("mla_w_ukv", k, 2, j, n_mla, mla_w_ukv[j], m_mla_w_ukv[j], v_mla_w_ukv[j])
            adam_sharded("mla_w_o", k, 3, j, n_mla, mla_w_o[j], m_mla_w_o[j], v_mla_w_o[j])
            after = res["mla_w_o"][0]
        else:
            adam_sharded("lru_w_in", k, 0, j, n_lru, lru_w_in[j], m_lru_w_in[j], v_lru_w_in[j])
            adam_sharded("lru_w_o", k, 1, j, n_lru, lru_w_o[j], m_lru_w_o[j], v_lru_w_o[j])
            after = res["lru_w_o"][0]
    res["mla_w_in"] = [t[:, :, :lq + lkv + QK_ROPE] for t in res["mla_w_in"]]
    res["mla_w_uq"] = [t.reshape(n_mla, lq, heads_local, HEAD_W)[:, :, :, :QK_NOPE + QK_ROPE].reshape(n_mla, lq, -1)
                       for t in res["mla_w_uq"]]

    _, small_lands = _exchange_wait(small_own, _ag_plan_own, after, name="ag_grads_wait")
    small_pass = _exchange_start([], small_lands, _ag_plan_pass, 3, name="ag_grads_pass")
    _, all_small = _exchange_wait(small_pass, _ag_plan_pass, after, name="ag_grads_pass_wait")
    slot_order = tuple(range(N_DEV))

    def adam_rep(terms, w, m, v, tag):
        r_pad, c_all = terms.shape[1], terms.shape[2]

        def prep(t):
            t2 = t.reshape(-1, c_all)
            return jnp.pad(t2, ((0, r_pad - t2.shape[0]), (0, 0)))

        outs = _adamw(prep(w), prep(m), prep(v), terms, slot_order, name=f"adamw_{tag}", tr=_adam_row_tile(r_pad, c_all))
        n_rows = w.size // c_all
        return [o[:n_rows].reshape(w.shape) for o in outs]

    small_w = pack_small(meta_tokens, lru_conv_w, lru_conv_b, lru_lambda)
    small_m = pack_small(m_meta_tokens, m_lru_conv_w, m_lru_conv_b, m_lru_lambda)
    small_v = pack_small(v_meta_tokens, v_lru_conv_w, v_lru_conv_b, v_lru_lambda)
    small_out = _adamw(small_w, small_m, small_v, all_small[0], slot_order, name="adamw_small", tr=small_pad, col_block=my_slot)
    small_out = [unpack_small(o) for o in small_out]
    for idx, key in enumerate(("meta_tokens", "lru_conv_w", "lru_conv_b", "lru_lambda")):
        res[key] = [small_out[k][idx] for k in range(4)]

    res["norm_mix"] = adam_rep(all_small[1], norm_mix, m_norm_mix, v_norm_mix, "norm_mix")
    res["norm_ffn"] = adam_rep(all_small[2], norm_ffn, m_norm_ffn, v_norm_ffn, "norm_ffn")
    res["norm_final"] = adam_rep(all_small[3], norm_final, m_norm_final, v_norm_final, "norm_final")
    res["mla_q_norm"] = adam_rep(all_small[4], mla_q_norm, m_mla_q_norm, v_mla_q_norm, "mla_q_norm")
    res["mla_kv_norm"] = adam_rep(all_small[5], mla_kv_norm, m_mla_kv_norm, v_mla_kv_norm, "mla_kv_norm")
    res["lru_w_gate_a"] = adam_rep(all_small[6], lru_w_gate_a, m_lru_w_gate_a, v_lru_w_gate_a, "lru_w_gate_a")
    res["lru_b_gate_a"] = adam_rep(all_small[7], lru_b_gate_a, m_lru_b_gate_a, v_lru_b_gate_a, "lru_b_gate_a")
    res["lru_w_gate_x"] = adam_rep(all_small[8], lru_w_gate_x, m_lru_w_gate_x, v_lru_w_gate_x, "lru_w_gate_x")
    res["lru_b_gate_x"] = adam_rep(all_small[9], lru_b_gate_x, m_lru_b_gate_x, v_lru_b_gate_x, "lru_b_gate_x")

    names = ["meta_tokens", "norm_mix", "norm_ffn", "norm_final", "mla_w_in", "mla_q_norm", "mla_kv_norm", "mla_w_uq",
             "mla_w_ukv", "mla_w_o", "lru_w_in", "lru_conv_w", "lru_conv_b", "lru_w_gate_a", "lru_b_gate_a", "lru_w_gate_x",
             "lru_b_gate_x", "lru_lambda", "lru_w_o", "ffn_w_gu", "ffn_w_down"]
    shapes = dict(meta_tokens=meta_tokens, norm_mix=norm_mix, norm_ffn=norm_ffn, norm_final=norm_final, mla_w_in=mla_w_in,
                  mla_q_norm=mla_q_norm, mla_kv_norm=mla_kv_norm, mla_w_uq=mla_w_uq, mla_w_ukv=mla_w_ukv, mla_w_o=mla_w_o,
                  lru_w_in=lru_w_in, lru_conv_w=lru_conv_w, lru_conv_b=lru_conv_b, lru_w_gate_a=lru_w_gate_a,
                  lru_b_gate_a=lru_b_gate_a, lru_w_gate_x=lru_w_gate_x, lru_b_gate_x=lru_b_gate_x, lru_lambda=lru_lambda,
                  lru_w_o=lru_w_o, ffn_w_gu=ffn_w_gu, ffn_w_down=ffn_w_down)
    outs = [loss, grad_x]
    for k in range(4):
        outs += [res[nm][k].reshape(shapes[nm].shape) for nm in names]
    return tuple(outs)
```

```python
import math

import jax
import jax.numpy as jnp
from jax import lax
from jax.experimental import pallas as pl
from jax.experimental.pallas import tpu as pltpu

F32 = jnp.float32
BF16 = jnp.bfloat16
MESH = pl.DeviceIdType.MESH

N_META = 16
CHUNK = 64
QK_NOPE = 128
QK_ROPE = 64
V_HEAD = 128
HEAD_W = 256
ROPE_THETA = 10000.0
LRU_C = 8.0
RMS_EPS = 1e-6
NEG_BIG = -1e30
ADAM_LR, ADAM_B1, ADAM_B2, ADAM_EPS, ADAM_WD, ADAM_STEP = 0.001, 0.9, 0.999, 1e-08, 0.01, 10

LANES = 128
SUBLANES = 8
VMEM_LIMIT_BYTES = 52 * 1024 * 1024
N_DEV = 8

_NT = (((1,), (1,)), ((), ()))
_TN = (((0,), (0,)), ((), ()))
_DIVS = (2048, 1024, 512, 256, 128)
_ROW_TILES = (1408, 1024, 512, 256, 128)


def _params(dims):
    return pltpu.CompilerParams(dimension_semantics=dims, vmem_limit_bytes=VMEM_LIMIT_BYTES)


def _pick(n, candidates):
    for c in candidates:
        if c <= n and n % c == 0:
            return c
    return n


def _sigmoid(z):
    return 1.0 / (1.0 + jnp.exp(-z))


def _gelu(x):
    c = math.sqrt(2.0 / math.pi)
    return 0.5 * x * (1.0 + jnp.tanh(c * (x + 0.044715 * x * x * x)))


def _gelu_grad(x):
    c = math.sqrt(2.0 / math.pi)
    th = jnp.tanh(c * (x + 0.044715 * x * x * x))
    return 0.5 * (1.0 + th) + 0.5 * x * (1.0 - th * th) * c * (1.0 + 3.0 * 0.044715 * x * x)


def _neg_expm1(x):
    poly = -x * (1.0 + x * (1.0 / 2.0) * (1.0 + x * (1.0 / 3.0) * (1.0 + x * (1.0 / 4.0) * (
        1.0 + x * (1.0 / 5.0) * (1.0 + x * (1.0 / 6.0) * (1.0 + x * (1.0 / 7.0)))))))
    return jnp.where(x > -0.25, poly, 1.0 - jnp.exp(x))


def _softplus_neg(lam):
    e = jnp.exp(-jnp.abs(lam))
    log1p = jnp.where(e > 1e-4, jnp.log(1.0 + e), e * (1.0 - e * (0.5 - e * (1.0 / 3.0))))
    return jnp.maximum(-lam, 0.0) + log1p


def _rot_half(x):
    lane = lax.broadcasted_iota(jnp.int32, x.shape, 1)
    first = (lane % QK_ROPE) < (QK_ROPE // 2)
    return jnp.where(first, -pltpu.roll(x, LANES - QK_ROPE // 2, 1), pltpu.roll(x, QK_ROPE // 2, 1))


def _rope(x, cos, sin):
    return x * cos + _rot_half(x) * sin


def _unrope(g, cos, sin):
    return g * cos - _rot_half(g) * sin


def _mm_nn(a, b, *, name, out_dtype, tm, tn, tk, b_blocked=False, res=None, epilogue=None, extras=()):
    m_all, k_all = a.shape
    if b_blocked:
        g_all, kb, nb = b.shape
        n_all = g_all * nb
        assert nb % tn == 0
        r = nb // tn
        b_spec = pl.BlockSpec((None, tk, tn), lambda j, i, k: (j // r, k, j % r))
    else:
        kb, n_all = b.shape
        b_spec = pl.BlockSpec((tk, tn), lambda j, i, k: (k, j))
    assert kb == k_all and m_all % tm == 0 and n_all % tn == 0 and k_all % tk == 0
    nm, nn, nk = m_all // tm, n_all // tn, k_all // tk
    in_specs = [pl.BlockSpec((tm, tk), lambda j, i, k: (i, k)), b_spec]
    operands = [a, b]
    has_res = res is not None
    if has_res:
        in_specs.append(pl.BlockSpec((tm, tn), lambda j, i, k: (i, j)))
        operands.append(res)
    for e in extras:
        in_specs.append(pl.BlockSpec((tm, e.shape[1]), lambda j, i, k: (i, 0)))
        operands.append(e)
    n_ex = len(extras)

    def body(*refs):
        a_ref, b_ref = refs[0], refs[1]
        pos = 2
        res_ref = None
        if has_res:
            res_ref = refs[pos]
            pos += 1
        ex_refs = refs[pos:pos + n_ex]
        pos += n_ex
        o_ref = refs[pos]
        acc_ref = refs[pos + 1] if nk > 1 else None

        def finish(acc):
            if has_res:
                acc = acc + res_ref[...]
            if epilogue is not None:
                acc = epilogue(acc, *ex_refs)
            o_ref[...] = acc.astype(o_ref.dtype)

        prod = jnp.dot(a_ref[...], b_ref[...], preferred_element_type=F32)
        if nk == 1:
            finish(prod)
        else:
            k = pl.program_id(2)

            @pl.when(k == 0)
            def _():
                acc_ref[...] = prod

            @pl.when(k > 0)
            def _():
                acc_ref[...] += prod

            @pl.when(k == nk - 1)
            def _():
                finish(acc_ref[...])

    return pl.pallas_call(
        body, name=name, grid=(nn, nm, nk), in_specs=in_specs,
        out_specs=pl.BlockSpec((tm, tn), lambda j, i, k: (i, j)),
        out_shape=jax.ShapeDtypeStruct((m_all, n_all), out_dtype),
        scratch_shapes=[pltpu.VMEM((tm, tn), F32)] if nk > 1 else [],
        compiler_params=_params(("parallel", "parallel", "arbitrary")),
    )(*operands)


def _mm_nt(a, b, *, name, out_dtype, tm, tn, tk, b_blocked=False):
    if a.ndim == 3:
        n_planes, m_all, kp = a.shape
        k_all = n_planes * kp
    else:
        n_planes, (m_all, k_all) = 0, a.shape
    if b_blocked and tk == k_all and b.shape[0] > 1:
        g_all, n_all, nb = b.shape
        assert g_all * nb == k_all and m_all % tm == 0 and n_all % tn == 0
        per_plane = kp // nb if n_planes else 0

        def whole_body(a_ref, b_ref, o_ref):
            acc = None
            for g in range(g_all):
                a_g = a_ref[g // per_plane, :, (g % per_plane) * nb:(g % per_plane + 1) * nb] if n_planes else a_ref[:, g * nb:(g + 1) * nb]
                prod = lax.dot_general(a_g, b_ref[g], _NT, preferred_element_type=F32)
                acc = prod if acc is None else acc + prod
            o_ref[...] = acc.astype(o_ref.dtype)

        a_whole = (pl.BlockSpec((n_planes, tm, kp), lambda j, i: (0, i, 0)) if n_planes
                   else pl.BlockSpec((tm, k_all), lambda j, i: (i, 0)))
        return pl.pallas_call(
            whole_body, name=name, grid=(n_all // tn, m_all // tm),
            in_specs=[a_whole, pl.BlockSpec((g_all, tn, nb), lambda j, i: (0, j, 0))],
            out_specs=pl.BlockSpec((tm, tn), lambda j, i: (i, j)),
            out_shape=jax.ShapeDtypeStruct((m_all, n_all), out_dtype),
            compiler_params=_params(("parallel", "parallel")),
        )(a, b)
    if n_planes:
        assert kp % tk == 0
        rp = kp // tk
        a_spec = pl.BlockSpec((None, tm, tk), lambda j, i, k: (k // rp, i, k % rp))
    else:
        a_spec = pl.BlockSpec((tm, tk), lambda j, i, k: (i, k))
    if b_blocked:
        g_all, n_all, nb = b.shape
        assert g_all * nb == k_all and nb % tk == 0
        r = nb // tk
        b_spec = pl.BlockSpec((None, tn, tk), lambda j, i, k: (k // r, j, k % r))
    else:
        n_all, kb = b.shape
        assert kb == k_all
        b_spec = pl.BlockSpec((tn, tk), lambda j, i, k: (j, k))
    assert m_all % tm == 0 and n_all % tn == 0 and k_all % tk == 0
    nm, nn, nk = m_all // tm, n_all // tn, k_all // tk

    def body(a_ref, b_ref, o_ref, *scratch):
        prod = lax.dot_general(a_ref[...], b_ref[...], _NT, preferred_element_type=F32)
        if nk == 1:
            o_ref[...] = prod.astype(o_ref.dtype)
        else:
            acc_ref = scratch[0]
            k = pl.program_id(2)

            @pl.when(k == 0)
            def _():
                acc_ref[...] = prod

            @pl.when(k > 0)
            def _():
                acc_ref[...] += prod

            @pl.when(k == nk - 1)
            def _():
                o_ref[...] = acc_ref[...].astype(o_ref.dtype)

    return pl.pallas_call(
        body, name=name, grid=(nn, nm, nk),
        in_specs=[a_spec, b_spec],
        out_specs=pl.BlockSpec((tm, tn), lambda j, i, k: (i, j)),
        out_shape=jax.ShapeDtypeStruct((m_all, n_all), out_dtype),
        scratch_shapes=[pltpu.VMEM((tm, tn), F32)] if nk > 1 else [],
        compiler_params=_params(("parallel", "parallel", "arbitrary")),
    )(a, b)


def _mm_tn(a, b, *, name, out_dtype, tm, tn, tk, out_block=None):
    t_all, m_all = a.shape
    if b.ndim == 3:
        n_planes, tb, n_p = b.shape
        assert n_p % tn == 0
        rq = n_p // tn
        n_all = n_planes * n_p
        b_spec = pl.BlockSpec((None, tk, tn), lambda i, j, k: (j // rq, k, j % rq))
    else:
        tb, n_all = b.shape
        b_spec = pl.BlockSpec((tk, tn), lambda i, j, k: (k, j))
    assert tb == t_all and m_all % tm == 0 and n_all % tn == 0 and t_all % tk == 0
    nm, nn, nk = m_all // tm, n_all // tn, t_all // tk
    if out_block is None:
        out_shape = jax.ShapeDtypeStruct((m_all, n_all), out_dtype)
        out_spec = pl.BlockSpec((tm, tn), lambda i, j, k: (i, j))
    else:
        assert out_block % tn == 0 and n_all % out_block == 0
        r = out_block // tn
        out_shape = jax.ShapeDtypeStruct((n_all // out_block, m_all, out_block), out_dtype)
        out_spec = pl.BlockSpec((None, tm, tn), lambda i, j, k: (j // r, i, j % r))

    def body(a_ref, b_ref, o_ref, *scratch):
        prod = lax.dot_general(a_ref[...], b_ref[...], _TN, preferred_element_type=F32)
        if nk == 1:
            o_ref[...] = prod.astype(o_ref.dtype)
        else:
            acc_ref = scratch[0]
            k = pl.program_id(2)

            @pl.when(k == 0)
            def _():
                acc_ref[...] = prod

            @pl.when(k > 0)
            def _():
                acc_ref[...] += prod

            @pl.when(k == nk - 1)
            def _():
                o_ref[...] = acc_ref[...].astype(o_ref.dtype)

    return pl.pallas_call(
        body, name=name, grid=(nm, nn, nk),
        in_specs=[pl.BlockSpec((tk, tm), lambda i, j, k: (k, i)), b_spec],
        out_specs=out_spec, out_shape=out_shape,
        scratch_shapes=[pltpu.VMEM((tm, tn), F32)] if nk > 1 else [],
        compiler_params=_params(("parallel", "parallel", "arbitrary")),
    )(a, b)


def _rmsnorm_fwd(x, g, *, name, tm):
    t_all, d = x.shape

    def body(x_ref, g_ref, o_ref):
        xv = x_ref[...]
        rstd = lax.rsqrt(jnp.mean(xv * xv, axis=-1, keepdims=True) + RMS_EPS)
        o_ref[...] = (xv * rstd * g_ref[...]).astype(o_ref.dtype)

    return pl.pallas_call(
        body, name=name, grid=(t_all // tm,),
        in_specs=[pl.BlockSpec((tm, d), lambda i: (i, 0)), pl.BlockSpec((1, d), lambda i: (0, 0))],
        out_specs=pl.BlockSpec((tm, d), lambda i: (i, 0)),
        out_shape=jax.ShapeDtypeStruct((t_all, d), BF16),
        compiler_params=_params(("parallel",)),
    )(x, g)


def _rms_bwd_math(dy, xv, g):
    rstd = lax.rsqrt(jnp.mean(xv * xv, axis=-1, keepdims=True) + RMS_EPS)
    xhat = xv * rstd
    dxh = dy * g
    dx = rstd * (dxh - xhat * jnp.mean(dxh * xhat, axis=-1, keepdims=True))
    return dx, jnp.sum(dy * xhat, axis=0, keepdims=True)


def _rmsnorm_bwd(dy, x, g, res, *, name, tm):
    t_all, d = x.shape

    def body(dy_ref, x_ref, g_ref, res_ref, dx_ref, dxb_ref, dg_ref):
        dx, dg = _rms_bwd_math(dy_ref[...], x_ref[...], g_ref[...])
        tot = res_ref[...] + dx
        dx_ref[...] = tot
        dxb_ref[...] = tot.astype(BF16)

        @pl.when(pl.program_id(0) == 0)
        def _():
            dg_ref[...] = dg

        @pl.when(pl.program_id(0) > 0)
        def _():
            dg_ref[...] += dg

    row = pl.BlockSpec((tm, d), lambda i: (i, 0))
    vec = pl.BlockSpec((1, d), lambda i: (0, 0))
    return pl.pallas_call(
        body, name=name, grid=(t_all // tm,),
        in_specs=[row, row, vec, row], out_specs=[row, row, vec],
        out_shape=[jax.ShapeDtypeStruct((t_all, d), F32), jax.ShapeDtypeStruct((t_all, d), BF16),
                   jax.ShapeDtypeStruct((1, d), F32)],
        compiler_params=_params(("arbitrary",)),
    )(dy, x, g, res)


def _loss_head(h, target, g, *, name, tm, n_real):
    t_all, d = h.shape

    def body(h_ref, t_ref, g_ref, loss_ref, dx_ref, dxb_ref, dg_ref):
        i = pl.program_id(0)
        xv = h_ref[...]
        gv = g_ref[...]
        rstd = lax.rsqrt(jnp.mean(xv * xv, axis=-1, keepdims=True) + RMS_EPS)
        y = xv * rstd * gv
        row = i * tm + lax.broadcasted_iota(jnp.int32, (tm, 1), 0)
        valid = (row >= N_META) & (row < n_real)
        err = jnp.where(valid, y - t_ref[...], 0.0)
        part = 0.5 * jnp.sum(jnp.mean(err * err, axis=-1, keepdims=True), axis=0, keepdims=True)
        dx, dg = _rms_bwd_math(err * (1.0 / d), xv, gv)
        dx_ref[...] = dx
        dxb_ref[...] = dx.astype(BF16)

        @pl.when(i == 0)
        def _():
            dg_ref[...] = dg
            loss_ref[...] = jnp.broadcast_to(part, loss_ref.shape)

        @pl.when(i > 0)
        def _():
            dg_ref[...] += dg
            loss_ref[...] += jnp.broadcast_to(part, loss_ref.shape)

    row = pl.BlockSpec((tm, d), lambda i: (i, 0))
    vec = pl.BlockSpec((1, d), lambda i: (0, 0))
    return pl.pallas_call(
        body, name=name, grid=(t_all // tm,),
        in_specs=[row, row, vec],
        out_specs=[pl.BlockSpec((1, LANES), lambda i: (0, 0)), row, row, vec],
        out_shape=[jax.ShapeDtypeStruct((1, LANES), F32), jax.ShapeDtypeStruct((t_all, d), F32),
                   jax.ShapeDtypeStruct((t_all, d), BF16), jax.ShapeDtypeStruct((1, d), F32)],
        compiler_params=_params(("arbitrary",)),
    )(h, target, g)


def _ffn_up(x, w_gu, *, name, tm):
    t_all, d = x.shape
    g_all, kb, nb = w_gu.shape
    half = g_all // 2
    f = half * nb
    assert kb == d and t_all % tm == 0

    def body(x_ref, wg_ref, wu_ref, gu_ref, act_ref):
        xv = x_ref[...]
        gv = jnp.dot(xv, wg_ref[...], preferred_element_type=F32)
        uv = jnp.dot(xv, wu_ref[...], preferred_element_type=F32)
        gu_ref[0] = gv
        gu_ref[1] = uv
        act_ref[...] = (gv * _sigmoid(gv) * uv).astype(act_ref.dtype)

    return pl.pallas_call(
        body, name=name, grid=(half, t_all // tm),
        in_specs=[pl.BlockSpec((tm, d), lambda j, i: (i, 0)), pl.BlockSpec((None, d, nb), lambda j, i: (j, 0, 0)),
                  pl.BlockSpec((None, d, nb), lambda j, i: (j + half, 0, 0))],
        out_specs=[pl.BlockSpec((2, tm, nb), lambda j, i: (0, i, j)), pl.BlockSpec((tm, nb), lambda j, i: (i, j))],
        out_shape=[jax.ShapeDtypeStruct((2, t_all, f), F32), jax.ShapeDtypeStruct((t_all, f), BF16)],
        compiler_params=_params(("parallel", "parallel")),
    )(x, w_gu, w_gu)


def _ffn_dact(dy, w_down, gu, *, name, tm, tn):
    t_all, d = dy.shape
    f = w_down.shape[0]
    assert t_all % tm == 0 and f % tn == 0

    def body(dy_ref, w_ref, gu_ref, o_ref):
        dact = lax.dot_general(dy_ref[...], w_ref[...], _NT, preferred_element_type=F32)
        gv, uv = gu_ref[0], gu_ref[1]
        sg = _sigmoid(gv)
        o_ref[0] = (dact * uv * (sg * (1.0 + gv * (1.0 - sg)))).astype(o_ref.dtype)
        o_ref[1] = (dact * gv * sg).astype(o_ref.dtype)

    return pl.pallas_call(
        body, name=name, grid=(f // tn, t_all // tm),
        in_specs=[pl.BlockSpec((tm, d), lambda j, i: (i, 0)), pl.BlockSpec((tn, d), lambda j, i: (j, 0)),
                  pl.BlockSpec((2, tm, tn), lambda j, i: (0, i, j))],
        out_specs=pl.BlockSpec((2, tm, tn), lambda j, i: (0, i, j)),
        out_shape=jax.ShapeDtypeStruct((2, t_all, f), BF16),
        compiler_params=_params(("parallel", "parallel")),
    )(dy, w_down, gu)


def _mla_prep_fwd(proj, qn, kvn, cos, sin, *, name, tm, lq, lkv):
    t_all, w = proj.shape

    def body(p_ref, qn_ref, kvn_ref, cos_ref, sin_ref, cq_ref, ckv_ref, kr_ref):
        pv = p_ref[...]
        xq = pv[:, :lq]
        xkv = pv[:, lq:lq + lkv]
        cq_ref[...] = (xq * lax.rsqrt(jnp.mean(xq * xq, axis=-1, keepdims=True) + RMS_EPS) * qn_ref[...]).astype(BF16)
        ckv_ref[...] = (xkv * lax.rsqrt(jnp.mean(xkv * xkv, axis=-1, keepdims=True) + RMS_EPS) * kvn_ref[...]).astype(BF16)
        kr_ref[...] = _rope(pv[:, lq + lkv:], cos_ref[...], sin_ref[...]).astype(BF16)

    def row(width):
        return pl.BlockSpec((tm, width), lambda i: (i, 0))

    def vec(width):
        return pl.BlockSpec((1, width), lambda i: (0, 0))

    return pl.pallas_call(
        body, name=name, grid=(t_all // tm,),
        in_specs=[row(w), vec(lq), vec(lkv), row(LANES), row(LANES)],
        out_specs=[row(lq), row(lkv), row(LANES)],
        out_shape=[jax.ShapeDtypeStruct((t_all, lq), BF16), jax.ShapeDtypeStruct((t_all, lkv), BF16),
                   jax.ShapeDtypeStruct((t_all, LANES), BF16)],
        compiler_params=_params(("parallel",)),
    )(proj, qn, kvn, cos, sin)


def _mla_prep_bwd(dcq, dckv, dkr_h, proj, qn, kvn, cos, sin, *, name, tm, lq, lkv):
    t_all, w = proj.shape
    n_heads = dkr_h.shape[0]

    def body(dcq_ref, dckv_ref, dkr_ref, p_ref, qn_ref, kvn_ref, cos_ref, sin_ref, dp_ref, dqn_ref, dkvn_ref):
        pv = p_ref[...]
        dxq, dqn = _rms_bwd_math(dcq_ref[...], pv[:, :lq], qn_ref[...])
        dxkv, dkvn = _rms_bwd_math(dckv_ref[...], pv[:, lq:lq + lkv], kvn_ref[...])
        dkr = dkr_ref[0]
        for hh in range(1, n_heads):
            dkr = dkr + dkr_ref[hh]
        dkr = _unrope(dkr, cos_ref[...], sin_ref[...])
        dp_ref[...] = jnp.concatenate([dxq, dxkv, dkr], axis=1).astype(BF16)

        @pl.when(pl.program_id(0) == 0)
        def _():
            dqn_ref[...] = dqn
            dkvn_ref[...] = dkvn

        @pl.when(pl.program_id(0) > 0)
        def _():
            dqn_ref[...] += dqn
            dkvn_ref[...] += dkvn

    def row(width):
        return pl.BlockSpec((tm, width), lambda i: (i, 0))

    def vec(width):
        return pl.BlockSpec((1, width), lambda i: (0, 0))

    return pl.pallas_call(
        body, name=name, grid=(t_all // tm,),
        in_specs=[row(lq), row(lkv), pl.BlockSpec((n_heads, tm, LANES), lambda i: (0, i, 0)), row(w),
                  vec(lq), vec(lkv), row(LANES), row(LANES)],
        out_specs=[row(w), vec(lq), vec(lkv)],
        out_shape=[jax.ShapeDtypeStruct((t_all, w), BF16), jax.ShapeDtypeStruct((1, lq), F32),
                   jax.ShapeDtypeStruct((1, lkv), F32)],
        compiler_params=_params(("arbitrary",)),
    )(dcq, dckv, dkr_h, proj, qn, kvn, cos, sin)


def _rope_q_epilogue(acc, cos_ref, sin_ref):
    parts = []
    for g in range(acc.shape[1] // LANES):
        blk = acc[:, g * LANES:(g + 1) * LANES]
        parts.append(_rope(blk, cos_ref[...], sin_ref[...]) if g % 2 == 1 else blk)
    return jnp.concatenate(parts, axis=1)


def _chunk_causal(rows, cols, row0=0):
    r = row0 + lax.broadcasted_iota(jnp.int32, (rows, cols), 0)
    c = lax.broadcasted_iota(jnp.int32, (rows, cols), 1)
    return (c >> 6) <= (r >> 6)


def _meta_keys(rows, cols):
    return lax.broadcasted_iota(jnp.int32, (rows, cols), 1) < N_META


def _attn_fwd(q, kv, kr, *, name, n_heads, tq, n_real, scale):
    t_all = q.shape[0]
    nq = (n_real - N_META) // tq
    assert N_META + nq * tq == n_real and tq % CHUNK == 0 and t_all >= LANES
    n_pad = t_all - n_real
    sub = tq // 2 if (tq // 2) % CHUNK == 0 else tq

    def body(q_ref, kv_ref, kr_ref, o_ref, lse_ref, k_scr, m_scr, l_scr, acc_scr):
        k_scr[:, :QK_NOPE] = kv_ref[:, :QK_NOPE]
        k_scr[:, QK_NOPE:] = kr_ref[...]
        if n_pad:
            o_ref[pl.ds(n_real, n_pad), :] = jnp.zeros((n_pad, V_HEAD), o_ref.dtype)
            lse_ref[pl.ds(n_real, n_pad), :] = jnp.zeros((n_pad, LANES), F32)

        def scores(qt, c0, width):
            return lax.dot_general(qt, k_scr[pl.ds(c0, width), :], _NT, preferred_element_type=F32) * scale

        def values(c0, width):
            return kv_ref[pl.ds(c0, width), QK_NOPE:]

        s = jnp.where(_meta_keys(LANES, LANES), scores(q_ref[pl.ds(0, LANES), :], 0, LANES), NEG_BIG)
        m = jnp.max(s, axis=-1, keepdims=True)
        p = jnp.exp(s - m)
        l = jnp.sum(p, axis=-1, keepdims=True)
        o_meta = jnp.dot(p.astype(BF16), values(0, LANES), preferred_element_type=F32) / l
        o_ref[pl.ds(0, N_META), :] = o_meta[:N_META].astype(o_ref.dtype)
        lse_ref[pl.ds(0, N_META), :] = jnp.broadcast_to((m + jnp.log(l))[:N_META], (N_META, LANES))

        parts = [(u * sub, sub) for u in range(tq // sub)]

        def accumulate(u0, s, c0, width):
            rows = pl.ds(u0, s.shape[0])
            m_prev = m_scr[rows, :]
            m_new = jnp.maximum(m_prev, jnp.max(s, axis=-1, keepdims=True))
            alpha = jnp.exp(m_prev - m_new)
            p = jnp.exp(s - m_new)
            l_scr[rows, :] = alpha * l_scr[rows, :] + jnp.sum(p, axis=-1, keepdims=True)
            acc_scr[rows, :] = alpha * acc_scr[rows, :] + jnp.dot(p.astype(BF16), values(c0, width), preferred_element_type=F32)
            m_scr[rows, :] = m_new

        def q_tile(i, carry):
            r0 = pl.multiple_of(N_META + i * tq, N_META)
            qts = [q_ref[pl.ds(r0 + u0, rows), :] for u0, rows in parts]
            for (u0, rows), qt in zip(parts, qts):
                s = jnp.where(_meta_keys(rows, LANES), scores(qt, 0, LANES), NEG_BIG)
                m = jnp.max(s, axis=-1, keepdims=True)
                p = jnp.exp(s - m)
                m_scr[pl.ds(u0, rows), :] = m
                l_scr[pl.ds(u0, rows), :] = jnp.sum(p, axis=-1, keepdims=True)
                acc_scr[pl.ds(u0, rows), :] = jnp.dot(p.astype(BF16), values(0, LANES), preferred_element_type=F32)

            def full_block(j, c):
                c0 = pl.multiple_of(N_META + j * tq, N_META)
                for (u0, _), qt in zip(parts, qts):
                    accumulate(u0, scores(qt, c0, tq), c0, tq)
                return c

            lax.fori_loop(0, i, full_block, 0)
            for (u0, rows), qt in zip(parts, qts):
                width = u0 + rows
                accumulate(u0, jnp.where(_chunk_causal(rows, width, u0), scores(qt, r0, width), NEG_BIG), r0, width)
            o_ref[pl.ds(r0, tq), :] = (acc_scr[...] / l_scr[...]).astype(o_ref.dtype)
            lse_ref[pl.ds(r0, tq), :] = jnp.broadcast_to(m_scr[...] + jnp.log(l_scr[...]), (tq, LANES))
            return carry

        lax.fori_loop(0, nq, q_tile, 0)

    def head(width):
        return pl.BlockSpec((t_all, width), lambda h: (0, h))

    return pl.pallas_call(
        body, name=name, grid=(n_heads,),
        in_specs=[head(HEAD_W), head(HEAD_W), pl.BlockSpec((t_all, LANES), lambda h: (0, 0))],
        out_specs=[head(V_HEAD), pl.BlockSpec((None, t_all, LANES), lambda h: (h, 0, 0))],
        out_shape=[jax.ShapeDtypeStruct((t_all, n_heads * V_HEAD), BF16),
                   jax.ShapeDtypeStruct((n_heads, t_all, LANES), F32)],
        scratch_shapes=[pltpu.VMEM((t_all, HEAD_W), BF16), pltpu.VMEM((tq, 1), F32), pltpu.VMEM((tq, 1), F32),
                        pltpu.VMEM((tq, V_HEAD), F32)],
        compiler_params=_params(("parallel",)),
    )(q, kv, kr)


def _attn_bwd(q, kv, kr, o, lse, do, cos, sin, *, name, n_heads, tq, n_real, scale):
    t_all = q.shape[0]
    nq = (n_real - N_META) // tq
    assert N_META + nq * tq == n_real and tq % CHUNK == 0 and t_all >= LANES
    n_pad = t_all - n_real

    def body(q_ref, kv_ref, kr_ref, o_ref, lse_ref, do_ref, cos_ref, sin_ref, dq_ref, dkv_ref, dkr_ref,
             k_scr, dk_scr, dv_scr, dq_scr):
        k_scr[:, :QK_NOPE] = kv_ref[:, :QK_NOPE]
        k_scr[:, QK_NOPE:] = kr_ref[...]
        dk_scr[...] = jnp.zeros(dk_scr.shape, F32)
        dv_scr[...] = jnp.zeros(dv_scr.shape, F32)
        if n_pad:
            dq_ref[pl.ds(n_real, n_pad), :] = jnp.zeros((n_pad, HEAD_W), dq_ref.dtype)

        def block(qt, dot, lse_t, delta, c0, width, mask):
            kb = k_scr[pl.ds(c0, width), :]
            s = lax.dot_general(qt, kb, _NT, preferred_element_type=F32) * scale
            p = jnp.exp(s - lse_t)
            if mask is not None:
                p = jnp.where(mask, p, 0.0)
            dp = lax.dot_general(dot, kv_ref[pl.ds(c0, width), QK_NOPE:], _NT, preferred_element_type=F32)
            ds = (p * (dp - delta) * scale).astype(BF16)
            dv_scr[pl.ds(c0, width), :] += lax.dot_general(p.astype(BF16), dot, _TN, preferred_element_type=F32)
            dk_scr[pl.ds(c0, width), :] += lax.dot_general(ds, qt, _TN, preferred_element_type=F32)
            return jnp.dot(ds, kb, preferred_element_type=F32)

        def write_dq(r0, rows, dq):
            cs, sn = cos_ref[pl.ds(r0, rows), :], sin_ref[pl.ds(r0, rows), :]
            dq_ref[pl.ds(r0, rows), :] = jnp.concatenate(
                [dq[:, :QK_NOPE], _unrope(dq[:, QK_NOPE:], cs, sn)], axis=1).astype(dq_ref.dtype)

        rows_m = lax.broadcasted_iota(jnp.int32, (LANES, LANES), 0) < N_META
        dot = do_ref[pl.ds(0, LANES), :]
        delta = jnp.sum(dot.astype(F32) * o_ref[pl.ds(0, LANES), :].astype(F32), axis=-1, keepdims=True)
        dq = block(q_ref[pl.ds(0, LANES), :], dot, lse_ref[pl.ds(0, LANES), :1], delta, 0, LANES,
                   _meta_keys(LANES, LANES) & rows_m)
        write_dq(0, N_META, dq[:N_META])

        def q_tile(i, carry):
            r0 = pl.multiple_of(N_META + i * tq, N_META)
            qt = q_ref[pl.ds(r0, tq), :]
            dot = do_ref[pl.ds(r0, tq), :]
            lse_t = lse_ref[pl.ds(r0, tq), :1]
            delta = jnp.sum(dot.astype(F32) * o_ref[pl.ds(r0, tq), :].astype(F32), axis=-1, keepdims=True)
            dq_scr[...] = block(qt, dot, lse_t, delta, 0, LANES, _meta_keys(tq, LANES))

            def full_block(j, c):
                c0 = pl.multiple_of(N_META + j * tq, N_META)
                dq_scr[...] += block(qt, dot, lse_t, delta, c0, tq, None)
                return c

            lax.fori_loop(0, i, full_block, 0)
            dq_scr[...] += block(qt, dot, lse_t, delta, r0, tq, _chunk_causal(tq, tq))
            write_dq(r0, tq, dq_scr[...])
            return carry

        lax.fori_loop(0, nq, q_tile, 0)
        dk = dk_scr[...]
        dkv_ref[...] = jnp.concatenate([dk[:, :QK_NOPE], dv_scr[...]], axis=1).astype(dkv_ref.dtype)
        dkr_ref[...] = dk[:, QK_NOPE:]

    def head(width):
        return pl.BlockSpec((t_all, width), lambda h: (0, h))

    table = pl.BlockSpec((t_all, LANES), lambda h: (0, 0))
    per_head = pl.BlockSpec((None, t_all, LANES), lambda h: (h, 0, 0))
    return pl.pallas_call(
        body, name=name, grid=(n_heads,),
        in_specs=[head(HEAD_W), head(HEAD_W), table, head(V_HEAD), per_head, head(V_HEAD), table, table],
        out_specs=[head(HEAD_W), head(HEAD_W), per_head],
        out_shape=[jax.ShapeDtypeStruct((t_all, n_heads * HEAD_W), BF16), jax.ShapeDtypeStruct((t_all, n_heads * HEAD_W), BF16),
                   jax.ShapeDtypeStruct((n_heads, t_all, LANES), F32)],
        scratch_shapes=[pltpu.VMEM((t_all, HEAD_W), BF16), pltpu.VMEM((t_all, HEAD_W), F32), pltpu.VMEM((t_all, V_HEAD), F32),
                        pltpu.VMEM((tq, HEAD_W), F32)],
        compiler_params=_params(("parallel",)),
    )(q, kv, kr, o, lse, do, cos, sin)


LRU_ROWS = 128


def _shifted_back(ref, t0, rows, shift_max):
    main = ref[pl.ds(t0, rows), :]
    prev = ref[pl.ds(pl.multiple_of(jnp.maximum(t0 - SUBLANES, 0), SUBLANES), SUBLANES), :]
    prev = jnp.where(t0 > 0, prev, 0.0)
    ext = jnp.concatenate([prev, main], axis=0)
    return [main] + [pltpu.roll(ext, s, 0)[SUBLANES:, :] for s in range(1, shift_max + 1)]


def _shifted_ahead(ref, t0, rows, t_all, shift_max):
    main = ref[pl.ds(t0, rows), :]
    nxt = ref[pl.ds(pl.multiple_of(jnp.minimum(t0 + rows, t_all - SUBLANES), SUBLANES), SUBLANES), :]
    nxt = jnp.where(t0 + rows < t_all, nxt, 0.0)
    ext = jnp.concatenate([main, nxt], axis=0)
    return [main] + [pltpu.roll(ext, rows + SUBLANES - s, 0)[:rows, :] for s in range(1, shift_max + 1)]


def _conv_fwd(xp_ref, t0, rows, cw, cb):
    sh = _shifted_back(xp_ref, t0, rows, 3)
    out = cb + cw[3:4, :] * sh[0]
    for k in range(3):
        out = out + cw[k:k + 1, :] * sh[3 - k]
    return out, sh


def _lru_gates(xb, wga, bga, wgx, bgx, sp):
    xbb = xb.astype(BF16)
    r = _sigmoid(jnp.dot(xbb, wga, preferred_element_type=F32) + bga)
    ig = _sigmoid(jnp.dot(xbb, wgx, preferred_element_type=F32) + bgx)
    la = -LRU_C * r * sp
    a = jnp.exp(la)
    s = jnp.sqrt(_neg_expm1(2.0 * la))
    return xbb, r, ig, a, s


def _scan_tile(a, b, reverse):
    rows = a.shape[0]
    ridx = lax.broadcasted_iota(jnp.int32, a.shape, 0)
    s = 1
    while s < rows:
        if reverse:
            keep = ridx < rows - s
            a_sh, b_sh = pltpu.roll(a, rows - s, 0), pltpu.roll(b, rows - s, 0)
        else:
            keep = ridx >= s
            a_sh, b_sh = pltpu.roll(a, s, 0), pltpu.roll(b, s, 0)
        b = jnp.where(keep, a * b_sh + b, b)
        a = jnp.where(keep, a * a_sh, a)
        s *= 2
    return a, b


def _lru_fwd(xy, conv_w, conv_b, wga, bga, wgx, bgx, lam, *, name):
    t_all = xy.shape[0]
    dr = xy.shape[1] // 2
    c = LANES
    nblk = dr // c
    rows = LRU_ROWS
    nt = t_all // rows

    def body(xp_ref, yp_ref, cw_ref, cb_ref, wga_ref, bga_ref, wgx_ref, bgx_ref, lam_ref, hs_ref, hsy_ref):
        cw, cb = cw_ref[...], cb_ref[...]
        sp = _softplus_neg(lam_ref[...])

        def tile(t, h_in):
            t0 = pl.multiple_of(t * rows, rows)
            xb, _ = _conv_fwd(xp_ref, t0, rows, cw, cb)
            _, _, ig, a, s = _lru_gates(xb, wga_ref[0], bga_ref[...], wgx_ref[0], bgx_ref[...], sp)
            cum_a, h0 = _scan_tile(a, s * (ig * xb), reverse=False)
            hs = cum_a * h_in + h0
            hs_ref[pl.ds(t0, rows), :] = hs
            hsy_ref[pl.ds(t0, rows), :] = (hs * _gelu(yp_ref[pl.ds(t0, rows), :])).astype(BF16)
            return hs[rows - 1:, :]

        lax.fori_loop(0, nt, tile, jnp.zeros((1, c), F32))

    col = pl.BlockSpec((t_all, c), lambda b: (0, b))
    vec = pl.BlockSpec((1, c), lambda b: (0, b))
    wsp = pl.BlockSpec((1, c, c), lambda b: (b, 0, 0))
    return pl.pallas_call(
        body, name=name, grid=(nblk,),
        in_specs=[col, pl.BlockSpec((t_all, c), lambda b: (0, nblk + b)), pl.BlockSpec((4, c), lambda b: (0, b)), vec,
                  wsp, vec, wsp, vec, vec],
        out_specs=[col, col],
        out_shape=[jax.ShapeDtypeStruct((t_all, dr), F32), jax.ShapeDtypeStruct((t_all, dr), BF16)],
        compiler_params=_params(("parallel",)),
    )(xy, xy, conv_w, conv_b, wga, bga, wgx, bgx, lam)


def _lru_bwd(xy, hs, dhsy, conv_w, conv_b, wga, bga, wgx, bgx, lam, *, name):
    t_all = xy.shape[0]
    dr = xy.shape[1] // 2
    c = LANES
    nblk = dr // c
    rows = LRU_ROWS
    nt = t_all // rows

    def body(xp_ref, yp_ref, hs_ref, dh_ref, cw_ref, cb_ref, wga_ref, bga_ref, wgx_ref, bgx_ref, lam_ref,
             dxp_ref, dyp_ref, dcw_ref, dcb_ref, dwga_ref, dbga_ref, dwgx_ref, dbgx_ref, dlam_ref,
             xb_scr, r_scr, i_scr, a_scr):
        cw, cb = cw_ref[...], cb_ref[...]
        lamv = lam_ref[...]
        sp = _softplus_neg(lamv)
        sig_neg = 1.0 / (1.0 + jnp.exp(lamv))
        wga_v, wgx_v = wga_ref[0], wgx_ref[0]

        def recompute(t, carry):
            t0 = pl.multiple_of(t * rows, rows)
            xb, _ = _conv_fwd(xp_ref, t0, rows, cw, cb)
            _, r, ig, a, _ = _lru_gates(xb, wga_v, bga_ref[...], wgx_v, bgx_ref[...], sp)
            xb_scr[pl.ds(t0, rows), :] = xb
            r_scr[pl.ds(t0, rows), :] = r
            i_scr[pl.ds(t0, rows), :] = ig
            a_scr[pl.ds(t0, rows), :] = a
            return carry

        lax.fori_loop(0, nt, recompute, 0)
        dwga_ref[...] = jnp.zeros(dwga_ref.shape, F32)
        dwgx_ref[...] = jnp.zeros(dwgx_ref.shape, F32)

        def tile(ti, carry):
            lam_in, dbga, dbgx, dlam, dcw, dcb = carry
            t = nt - 1 - ti
            t0 = pl.multiple_of(t * rows, rows)
            a_now, a_next = _shifted_ahead(a_scr, t0, rows, t_all, 1)
            yp = yp_ref[pl.ds(t0, rows), :]
            dhy = dh_ref[pl.ds(t0, rows), :]
            cum_a, lam0 = _scan_tile(a_next, dhy * _gelu(yp), reverse=True)
            lam_t = cum_a * lam_in + lam0
            hs_now, hs_prev = _shifted_back(hs_ref, t0, rows, 1)
            da = lam_t * hs_prev
            xb = xb_scr[pl.ds(t0, rows), :]
            r = r_scr[pl.ds(t0, rows), :]
            ig = i_scr[pl.ds(t0, rows), :]
            la = -LRU_C * r * sp
            s = jnp.sqrt(_neg_expm1(2.0 * la))
            d_ixb = lam_t * s
            dla = da * a_now - (lam_t * ig * xb) * (a_now * a_now / s)
            dzr = dla * (-LRU_C * sp) * r * (1.0 - r)
            dzi = d_ixb * xb * ig * (1.0 - ig)
            dzr_b, dzi_b = dzr.astype(BF16), dzi.astype(BF16)
            xbb = xb.astype(BF16)
            dwga_ref[0] += lax.dot_general(xbb, dzr_b, _TN, preferred_element_type=F32)
            dwgx_ref[0] += lax.dot_general(xbb, dzi_b, _TN, preferred_element_type=F32)
            dxb = (d_ixb * ig + lax.dot_general(dzr_b, wga_v, _NT, preferred_element_type=F32)
                   + lax.dot_general(dzi_b, wgx_v, _NT, preferred_element_type=F32))
            xb_scr[pl.ds(t0, rows), :] = dxb
            dyp_ref[pl.ds(t0, rows), :] = (dhy * hs_now * _gelu_grad(yp)).astype(BF16)
            ahead = _shifted_ahead(xb_scr, t0, rows, t_all, 3)
            dxp = cw[3:4, :] * ahead[0]
            for k in range(3):
                dxp = dxp + cw[k:k + 1, :] * ahead[3 - k]
            dxp_ref[pl.ds(t0, rows), :] = dxp.astype(BF16)
            back = _shifted_back(xp_ref, t0, rows, 3)
            dcw_t = jnp.concatenate([jnp.sum(dxb * back[3 - k], axis=0, keepdims=True) for k in range(4)], axis=0)
            return (lam_t[:1, :], dbga + jnp.sum(dzr, axis=0, keepdims=True), dbgx + jnp.sum(dzi, axis=0, keepdims=True),
                    dlam + jnp.sum(dla * r, axis=0, keepdims=True), dcw + dcw_t, dcb + jnp.sum(dxb, axis=0, keepdims=True))

        zero = jnp.zeros((1, c), F32)
        _, dbga, dbgx, dlam, dcw, dcb = lax.fori_loop(0, nt, tile, (zero, zero, zero, zero, jnp.zeros((4, c), F32), zero))
        dbga_ref[...] = dbga
        dbgx_ref[...] = dbgx
        dlam_ref[...] = dlam * (LRU_C * sig_neg)
        dcw_ref[...] = dcw
        dcb_ref[...] = dcb

    col = pl.BlockSpec((t_all, c), lambda b: (0, b))
    col2 = pl.BlockSpec((t_all, c), lambda b: (0, nblk + b))
    vec = pl.BlockSpec((1, c), lambda b: (0, b))
    tap = pl.BlockSpec((4, c), lambda b: (0, b))
    wsp = pl.BlockSpec((1, c, c), lambda b: (b, 0, 0))
    vshape = jax.ShapeDtypeStruct((1, dr), F32)
    wshape = jax.ShapeDtypeStruct((nblk, c, c), F32)
    dxp, dyp, dcw, dcb, dwga, dbga, dwgx, dbgx, dlam = pl.pallas_call(
        body, name=name, grid=(nblk,),
        in_specs=[col, col2, col, col, tap, vec, wsp, vec, wsp, vec, vec],
        out_specs=[col, col, tap, vec, wsp, vec, wsp, vec, vec],
        out_shape=[jax.ShapeDtypeStruct((t_all, dr), BF16), jax.ShapeDtypeStruct((t_all, dr), BF16),
                   jax.ShapeDtypeStruct((4, dr), F32), vshape, wshape, vshape, wshape, vshape, vshape],
        scratch_shapes=[pltpu.VMEM((t_all, c), F32)] * 4,
        compiler_params=_params(("parallel",)),
    )(xy, xy, hs, dhsy, conv_w, conv_b, wga, bga, wgx, bgx, lam)
    return jnp.concatenate([dxp, dyp], axis=1), dcw, dcb, dwga, dbga, dwgx, dbgx, dlam


def _mesh_pos():
    return lax.axis_index("x"), lax.axis_index("y"), lax.axis_index("c")


def _all_gather(shards, *, name):
    n = len(shards)

    def body(*refs):
        ins, outs, token = refs[:n], refs[n:2 * n], refs[2 * n]
        send_sems, recv_sems, local_sems = refs[2 * n + 1:]
        token[...] = jnp.zeros(token.shape, token.dtype)
        x, y, c = _mesh_pos()
        me, sibling = (x, y, c), (x, y, 1 - c)
        chips = [(1 - x, y), (x, 1 - y), (1 - x, 1 - y)]
        slot = _slot

        def copy(a, k, block, to, src=None):
            dst = outs[a].at[slot(block)]
            return pltpu.make_async_remote_copy(
                src_ref=dst if src is None else src, dst_ref=dst, send_sem=send_sems.at[a, k],
                recv_sem=recv_sems.at[a, k], device_id=to, device_id_type=MESH)

        mine = [pltpu.make_async_copy(ins[a], outs[a].at[slot(me)], local_sems.at[a]) for a in range(n)]
        for cp in mine:
            cp.start()
        first = []
        for a in range(n):
            first.append(copy(a, 0, me, sibling, src=ins[a]))
            first += [copy(a, 1 + j, me, (*chip, c), src=ins[a]) for j, chip in enumerate(chips)]
        for cp in first:
            cp.start()
        passed = []
        for a in range(n):
            for j, chip in enumerate(chips):
                copy(a, 1 + j, (*chip, c), me).wait_recv()
                fwd = copy(a, 4 + j, (*chip, c), sibling)
                fwd.start()
                passed.append(fwd)
        for a in range(n):
            copy(a, 0, sibling, me).wait_recv()
            for j, chip in enumerate(chips):
                copy(a, 4 + j, (*chip, 1 - c), me).wait_recv()
        for cp in first + passed:
            cp.wait_send()
        for cp in mine:
            cp.wait()

    any_spec = pl.BlockSpec(memory_space=pl.ANY)
    outs = pl.pallas_call(
        body, name=name,
        in_specs=[any_spec] * n, out_specs=[any_spec] * n + [pl.BlockSpec(memory_space=pltpu.VMEM)],
        out_shape=[jax.ShapeDtypeStruct((N_DEV,) + s.shape, s.dtype) for s in shards]
        + [jax.ShapeDtypeStruct((SUBLANES, LANES), F32)],
        scratch_shapes=[pltpu.SemaphoreType.DMA((n, 7)), pltpu.SemaphoreType.DMA((n, 7)), pltpu.SemaphoreType.DMA((n,))],
    )(*shards)
    return list(outs[:n]), outs[n][0, 0]


_HBM = pl.BlockSpec(memory_space=pltpu.HBM)
_SEM = pl.BlockSpec(memory_space=pltpu.SEMAPHORE)
_ANY = pl.BlockSpec(memory_space=pl.ANY)
_EFFECT = pltpu.SideEffectType.DATAFLOW_SIDE_EFFECTING


def _slot(p):
    return 4 * p[0] + 2 * p[1] + p[2]


def _remote(src, dst, send, recv, idx, to):
    return pltpu.make_async_remote_copy(src_ref=src, dst_ref=dst, send_sem=send.at[idx], recv_sem=recv.at[idx],
                                        device_id=to, device_id_type=MESH)


def _ag_plan_own(a, src, land, send, recv):
    x, y, c = _mesh_pos()
    dst = land.at[_slot((x, y, c))]
    targets = [(x, y, 1 - c), (1 - x, y, c), (x, 1 - y, c), (1 - x, 1 - y, c)]
    return [_remote(src, dst, send, recv, 4 * a + k, to) for k, to in enumerate(targets)]


def _ag_plan_pass(a, src, land, send, recv):
    x, y, c = _mesh_pos()
    blocks = [land.at[_slot((px, py, c))] for px, py in ((1 - x, y), (x, 1 - y), (1 - x, 1 - y))]
    return [_remote(blk, blk, send, recv, 3 * a + k, (x, y, 1 - c)) for k, blk in enumerate(blocks)]


def _rs_plan_sibling(a, src, land, send, recv):
    x, y, c = _mesh_pos()
    return [_remote(src.at[2 * j + (1 - c)], land.at[j], send, recv, 4 * a + j, (x, y, 1 - c)) for j in range(4)]


def _rs_plan_chips(a, src, land, send, recv):
    x, y, c = _mesh_pos()
    out = []
    for k in (1, 2, 3):
        px = 1 - x if k & 2 else x
        py = 1 - y if k & 1 else y
        out.append(_remote(src.at[2 * px + py], land.at[k - 1], send, recv, 3 * a + k - 1, (px, py, c)))
    return out


def _in_hbm(a):
    return pltpu.with_memory_space_constraint(a, pltpu.HBM)


def _exchange_start(srcs, lands, plan, n_k, *, name):
    ns, n = len(srcs), len(lands)

    def body(*refs):
        src_refs, land_refs = refs[:ns], refs[ns:ns + n]
        send, recv = refs[ns + n], refs[ns + n + 1]
        token = refs[-1]
        for a in range(n):
            for cp in plan(a, src_refs[a] if ns else None, land_refs[a], send, recv):
                cp.start()
        token[...] = jnp.zeros(token.shape, token.dtype)

    bufs = list(srcs) + list(lands)
    outs = pl.pallas_call(
        body, name=name,
        out_shape=(pltpu.SemaphoreType.DMA((n * n_k,)), pltpu.SemaphoreType.DMA((n * n_k,)),
                   *[pltpu.HBM(b.shape, b.dtype) for b in bufs], jax.ShapeDtypeStruct((SUBLANES, LANES), F32)),
        in_specs=[_HBM] * (ns + n),
        out_specs=(_SEM, _SEM, *[_HBM] * (ns + n), pl.BlockSpec(memory_space=pltpu.VMEM)),
        input_output_aliases={i: 2 + i for i in range(ns + n)},
        compiler_params=pltpu.CompilerParams(has_side_effects=_EFFECT),
    )(*[_in_hbm(b) for b in bufs])
    return outs[0], outs[1], list(outs[2:2 + ns]), list(outs[2 + ns:2 + ns + n]), outs[-1]


def _exchange_wait(started, plan, after, *, name):
    send, recv, srcs, lands, _ = started
    ns, n = len(srcs), len(lands)

    def body(*refs):
        src_refs, land_refs = refs[:ns], refs[ns:ns + n]
        send_ref, recv_ref = refs[ns + n], refs[ns + n + 1]
        for a in range(n):
            for cp in plan(a, src_refs[a] if ns else None, land_refs[a], send_ref, recv_ref):
                cp.wait_send()
                cp.wait_recv()

    bufs = list(srcs) + list(lands)
    outs = pl.pallas_call(
        body, name=name,
        out_shape=tuple(pltpu.HBM(b.shape, b.dtype) for b in bufs),
        in_specs=[_HBM] * (ns + n) + [_SEM, _SEM, _ANY],
        out_specs=tuple([_HBM] * (ns + n)),
        input_output_aliases={i: i for i in range(ns + n)},
        compiler_params=pltpu.CompilerParams(has_side_effects=_EFFECT),
    )(*bufs, send, recv, after)
    return list(outs[:ns]), list(outs[ns:])


def _pair_add(grads, landed, core, *, name, tr):
    _, r_all, c_all = grads.shape

    def body(core_ref, g_ref, l_ref, o_ref):
        o_ref[...] = (g_ref[...].astype(F32) + l_ref[...].astype(F32)).astype(o_ref.dtype)

    return pl.pallas_call(
        body, name=name,
        grid_spec=pltpu.PrefetchScalarGridSpec(
            num_scalar_prefetch=1, grid=(4, r_all // tr),
            in_specs=[pl.BlockSpec((None, tr, c_all), lambda j, i, core_ref: (2 * j + core_ref[0], i, 0)),
                      pl.BlockSpec((None, tr, c_all), lambda j, i, core_ref: (j, i, 0))],
            out_specs=pl.BlockSpec((None, tr, c_all), lambda j, i, core_ref: (j, i, 0))),
        out_shape=jax.ShapeDtypeStruct((4, r_all, c_all), grads.dtype),
        compiler_params=_params(("parallel", "parallel")),
    )(core, grads, landed)


def _adamw_math(w, g, m, v):
    m2 = ADAM_B1 * m + (1.0 - ADAM_B1) * g
    v2 = ADAM_B2 * v + (1.0 - ADAM_B2) * (g * g)
    m_hat = m2 / (1.0 - ADAM_B1 ** ADAM_STEP)
    v_hat = v2 / (1.0 - ADAM_B2 ** ADAM_STEP)
    delta = -ADAM_LR * (m_hat / (jnp.sqrt(v_hat) + ADAM_EPS) + ADAM_WD * w)
    return delta, m2, v2


def _adamw(w, m, v, terms, order, *, name, tr, col_block=None, own=None, stack=None):
    r_all, c_all = w.shape
    n_slots = terms.shape[0]

    def body(*refs):
        if col_block is not None or own is not None:
            refs = refs[1:]
        own_ref = None
        if own is not None:
            own_ref, refs = refs[0], refs[1:]
        w_ref, m_ref, v_ref, t_ref, g_ref, d_ref, m2_ref, v2_ref = refs
        if own_ref is not None:
            g = own_ref[...].astype(F32) + t_ref[order[0]].astype(F32)
        else:
            g = t_ref[order[0]].astype(F32)
        for s in order[1:]:
            g = g + t_ref[s].astype(F32)
        delta, m2, v2 = _adamw_math(w_ref[...], g, m_ref[...], v_ref[...])
        g_ref[...] = g
        d_ref[...] = delta
        m2_ref[...] = m2
        v2_ref[...] = v2

    shape = jax.ShapeDtypeStruct((r_all, c_all), F32)
    if own is not None:
        layer, n_layers, prev = stack
        row = pl.BlockSpec((tr, c_all), lambda i, idx: (i, 0))
        slab = pl.BlockSpec((None, tr, c_all), lambda i, idx: (layer, i, 0))
        carried = [] if prev is None else list(prev)

        def stacked_body(*refs):
            body(*refs[:6], *refs[6 + len(carried):])

        return pl.pallas_call(
            stacked_body, name=name,
            grid_spec=pltpu.PrefetchScalarGridSpec(
                num_scalar_prefetch=1, grid=(r_all // tr,),
                in_specs=[pl.BlockSpec((None, tr, c_all), lambda i, idx: (idx[0], i, 0)), row, row, row,
                          pl.BlockSpec((n_slots, tr, c_all), lambda i, idx: (0, i, 0))] + [_ANY] * len(carried),
                out_specs=[slab] * 4),
            out_shape=[jax.ShapeDtypeStruct((n_layers, r_all, c_all), F32)] * 4,
            input_output_aliases={6 + k: k for k in range(len(carried))},
            compiler_params=_params(("parallel",)),
        )(own[1], own[0], w, m, v, terms, *carried)
    if col_block is None:
        row = pl.BlockSpec((tr, c_all), lambda i: (i, 0))
        return pl.pallas_call(
            body, name=name, grid=(r_all // tr,),
            in_specs=[row, row, row, pl.BlockSpec((n_slots, tr, c_all), lambda i: (0, i, 0))],
            out_specs=[row] * 4, out_shape=[shape] * 4, compiler_params=_params(("parallel",)),
        )(w, m, v, terms)
    row = pl.BlockSpec((tr, c_all), lambda i, blk: (i, 0))
    return pl.pallas_call(
        body, name=name,
        grid_spec=pltpu.PrefetchScalarGridSpec(
            num_scalar_prefetch=1, grid=(r_all // tr,),
            in_specs=[row, row, row, pl.BlockSpec((n_slots, tr, c_all), lambda i, blk: (0, i, blk[0]))],
            out_specs=[row] * 4),
        out_shape=[shape] * 4, compiler_params=_params(("parallel",)),
    )(col_block, w, m, v, terms)


def _rope_tables(t_all):
    pos = jnp.arange(t_all, dtype=F32)
    inv_freq = ROPE_THETA ** (-jnp.arange(0, QK_ROPE, 2, dtype=F32) / QK_ROPE)
    ang = pos[:, None] * inv_freq[None, :]
    cos, sin = jnp.cos(ang), jnp.sin(ang)
    return jnp.tile(cos, (1, LANES // (QK_ROPE // 2))), jnp.tile(sin, (1, LANES // (QK_ROPE // 2)))


def _adam_row_tile(r_all, c_all, block_bytes=512 * 1024):
    target = max(SUBLANES, block_bytes // (4 * c_all))
    return _pick(r_all, [t for t in (1024, 704, 512, 352, 256, 176, 128, 64, 32, 16, 8) if t <= target])


def _rows_natural(wg):
    return wg.reshape(wg.shape[0] * wg.shape[1], wg.shape[2])


def _mla_layer_fwd(tag, h, g_mix, ws, qn, kvn, cos, sin, *, tm, tq, n_heads, scale, n_real):
    w_in, w_uq, w_ukv, w_o = _rows_natural(ws[0]), ws[1], ws[2], _rows_natural(ws[3])
    t_all, d = h.shape
    lq, lkv = qn.shape[1], kvn.shape[1]
    tmb = _pick(t_all, _ROW_TILES)
    hn = _rmsnorm_fwd(h, g_mix, name=f"norm_mix{tag}", tm=tm)
    proj = _mm_nn(hn, w_in, name=f"mla_in{tag}", out_dtype=F32, tm=tmb, tn=w_in.shape[1], tk=_pick(d, _DIVS))
    cq, ckv, kr = _mla_prep_fwd(proj, qn, kvn, cos, sin, name=f"mla_prep{tag}", tm=tm, lq=lq, lkv=lkv)
    q = _mm_nn(cq, w_uq, name=f"mla_q{tag}", out_dtype=BF16, tm=tmb, tn=w_uq.shape[2], tk=lq, b_blocked=True,
               epilogue=_rope_q_epilogue, extras=(cos, sin))
    kv = _mm_nn(ckv, w_ukv, name=f"mla_kv{tag}", out_dtype=BF16, tm=tmb, tn=w_ukv.shape[2], tk=lkv, b_blocked=True)
    o, lse = _attn_fwd(q, kv, kr, name=f"attn_fwd{tag}", n_heads=n_heads, tq=tq, n_real=n_real, scale=scale)
    h_mid = _mm_nn(o, w_o, name=f"mla_o{tag}", out_dtype=F32, tm=tm, tn=d, tk=o.shape[1], res=h)
    return h_mid, (hn, proj, cq, ckv, kr, q, kv, o, lse)


def _mla_layer_bwd(tag, dh, dh_b, h_in, saved, g_mix, ws, qn, kvn, cos, sin, *, tm, tq, n_heads, scale, n_real):
    hn, proj, cq, ckv, kr, q, kv, o, lse = saved
    w_in, w_uq, w_ukv, w_o = _rows_natural(ws[0]), ws[1], ws[2], _rows_natural(ws[3])
    t_all, d = h_in.shape
    lq, lkv = qn.shape[1], kvn.shape[1]
    ov = o.shape[1]
    tmb = _pick(t_all, _ROW_TILES)
    tn_d, tk_d = _pick(d, _DIVS[1:]), _pick(d, _DIVS)
    do = _mm_nt(dh_b, w_o, name=f"mla_do{tag}", out_dtype=BF16, tm=tmb, tn=_pick(ov, _DIVS[1:]), tk=tk_d)
    dw_o = _mm_tn(o, dh_b, name=f"mla_dwo{tag}", out_dtype=BF16, tm=_pick(ov, _DIVS[2:]), tn=tn_d, tk=t_all)
    dq, dkv, dkr_h = _attn_bwd(q, kv, kr, o, lse, do, cos, sin, name=f"attn_bwd{tag}", n_heads=n_heads, tq=tq, n_real=n_real,
                               scale=scale)
    hw, kw = w_uq.shape[2], w_ukv.shape[2]
    dw_uq = _mm_tn(cq, dq, name=f"mla_dwuq{tag}", out_dtype=BF16, tm=lq, tn=hw, tk=t_all, out_block=hw)
    dcq = _mm_nt(dq, w_uq, name=f"mla_dcq{tag}", out_dtype=F32, tm=tm, tn=lq, tk=dq.shape[1], b_blocked=True)
    dw_ukv = _mm_tn(ckv, dkv, name=f"mla_dwukv{tag}", out_dtype=BF16, tm=lkv, tn=kw, tk=t_all, out_block=kw)
    dckv = _mm_nt(dkv, w_ukv, name=f"mla_dckv{tag}", out_dtype=F32, tm=tm, tn=lkv, tk=dkv.shape[1], b_blocked=True)
    dproj, dqn, dkvn = _mla_prep_bwd(dcq, dckv, dkr_h, proj, qn, kvn, cos, sin, name=f"mla_prep_bwd{tag}", tm=tm, lq=lq, lkv=lkv)
    wc = w_in.shape[1]
    dw_in = _mm_tn(hn, dproj, name=f"mla_dwin{tag}", out_dtype=BF16, tm=_pick(d, _DIVS[2:]), tn=wc, tk=t_all)
    dhn = _mm_nt(dproj, w_in, name=f"mla_dhn{tag}", out_dtype=F32, tm=tmb, tn=tn_d, tk=wc)
    dh, dh_b, dg = _rmsnorm_bwd(dhn, h_in, g_mix, dh, name=f"norm_mix_bwd{tag}", tm=tm)
    return dh, dh_b, dg, dqn, dkvn, [dw_in.reshape(N_DEV, -1, wc), dw_uq, dw_ukv, dw_o.reshape(N_DEV, -1, d)]


def _lru_layer_fwd(tag, h, g_mix, ws, small, *, tm):
    w_lin, w_lo = ws[0], _rows_natural(ws[1])
    t_all, d = h.shape
    dr = w_lo.shape[0]
    tmb = _pick(t_all, _ROW_TILES)
    hn = _rmsnorm_fwd(h, g_mix, name=f"norm_mix{tag}", tm=tm)
    xy = _mm_nn(hn, w_lin, name=f"lru_in{tag}", out_dtype=F32, tm=tmb, tn=w_lin.shape[2], tk=_pick(d, _DIVS), b_blocked=True)
    hs, hsy = _lru_fwd(xy, *small, name=f"lru_fwd{tag}")
    h_mid = _mm_nn(hsy, w_lo, name=f"lru_o{tag}", out_dtype=F32, tm=tm, tn=d, tk=dr, res=h)
    return h_mid, (hn, xy, hs, hsy)


def _lru_layer_bwd(tag, dh, dh_b, h_in, saved, g_mix, ws, small, *, tm):
    hn, xy, hs, hsy = saved
    w_lin, w_lo = ws[0], _rows_natural(ws[1])
    t_all, d = h_in.shape
    dr = w_lo.shape[0]
    tmb = _pick(t_all, _ROW_TILES)
    tn_d, tk_d = _pick(d, _DIVS[1:]), _pick(d, _DIVS)
    dhsy = _mm_nt(dh_b, w_lo, name=f"lru_dhsy{tag}", out_dtype=F32, tm=tmb, tn=_pick(dr, _DIVS[1:]), tk=tk_d)
    dw_lo = _mm_tn(hsy, dh_b, name=f"lru_dwo{tag}", out_dtype=BF16, tm=_pick(dr, _DIVS[2:]), tn=tn_d, tk=t_all)
    dxy, *dsmall = _lru_bwd(xy, hs, dhsy, *small, name=f"lru_bwd{tag}")
    lw = w_lin.shape[2]
    dw_lin = _mm_tn(hn, dxy, name=f"lru_dwin{tag}", out_dtype=BF16, tm=tn_d, tn=lw, tk=t_all, out_block=lw)
    dhn = _mm_nt(dxy, w_lin, name=f"lru_dhn{tag}", out_dtype=F32, tm=tm, tn=tn_d, tk=dxy.shape[1], b_blocked=True)
    dh, dh_b, dg = _rmsnorm_bwd(dhn, h_in, g_mix, dh, name=f"norm_mix_bwd{tag}", tm=tm)
    return dh, dh_b, dg, tuple(dsmall), [dw_lin, dw_lo.reshape(N_DEV, -1, d)]


def _ffn_layer_fwd(tag, h_mid, g_ffn, ws, *, tm):
    w_gu, w_down = ws[0], _rows_natural(ws[1])
    t_all, d = h_mid.shape
    f_all = w_down.shape[0]
    tmb = _pick(t_all, _ROW_TILES)
    fk = _pick(f_all, (1408,) + _DIVS[1:])
    hn2 = _rmsnorm_fwd(h_mid, g_ffn, name=f"norm_ffn{tag}", tm=tm)
    gu, act = _ffn_up(hn2, w_gu, name=f"ffn_up{tag}", tm=tm)
    h_out = _mm_nn(act, w_down, name=f"ffn_down{tag}", out_dtype=F32, tm=tm, tn=_pick(d, _DIVS[1:]), tk=f_all, res=h_mid)
    return h_out, (hn2, gu, act)


def _ffn_layer_bwd(tag, dh, dh_b, h_mid, saved, g_ffn, ws, *, tm):
    hn2, gu, act = saved
    w_gu, w_down = ws[0], _rows_natural(ws[1])
    t_all, d = h_mid.shape
    f_all = w_down.shape[0]
    f_local = w_gu.shape[2]
    tmb = _pick(t_all, _ROW_TILES)
    fk = _pick(f_all, (1408,) + _DIVS[1:])
    tn_d, tk_d = _pick(d, _DIVS[1:]), _pick(d, _DIVS)
    dgu = _ffn_dact(dh_b, w_down, gu, name=f"ffn_dact{tag}", tm=tm, tn=f_local)
    dw_down = _mm_tn(act, dh_b, name=f"ffn_dwdown{tag}", out_dtype=BF16, tm=fk, tn=_pick(d, _DIVS[2:]), tk=t_all)
    dhn2 = _mm_nt(dgu, w_gu, name=f"ffn_dhn{tag}", out_dtype=F32, tm=tm, tn=_pick(d, _DIVS[2:]), tk=2 * f_all, b_blocked=True)
    dw_gu = _mm_tn(hn2, dgu, name=f"ffn_dwgu{tag}", out_dtype=BF16, tm=_pick(d, _DIVS[2:]), tn=f_local, tk=t_all, out_block=f_local)
    dh, dh_b, dg = _rmsnorm_bwd(dhn2, h_mid, g_ffn, dh, name=f"norm_ffn_bwd{tag}", tm=tm)
    return dh, dh_b, dg, [dw_gu, dw_down.reshape(N_DEV, -1, d)]


def kernel(x, meta_tokens, norm_mix, norm_ffn, norm_final, mla_w_in, mla_q_norm, mla_kv_norm, mla_w_uq, mla_w_ukv, mla_w_o, lru_w_in, lru_conv_w, lru_conv_b, lru_w_gate_a, lru_b_gate_a, lru_w_gate_x, lru_b_gate_x, lru_lambda, lru_w_o, ffn_w_gu, ffn_w_down, loss_target, m_meta_tokens, m_norm_mix, m_norm_ffn, m_norm_final, m_mla_w_in, m_mla_q_norm, m_mla_kv_norm, m_mla_w_uq, m_mla_w_ukv, m_mla_w_o, m_lru_w_in, m_lru_conv_w, m_lru_conv_b, m_lru_w_gate_a, m_lru_b_gate_a, m_lru_w_gate_x, m_lru_b_gate_x, m_lru_lambda, m_lru_w_o, m_ffn_w_gu, m_ffn_w_down, v_meta_tokens, v_norm_mix, v_norm_ffn, v_norm_final, v_mla_w_in, v_mla_q_norm, v_mla_kv_norm, v_mla_w_uq, v_mla_w_ukv, v_mla_w_o, v_lru_w_in, v_lru_conv_w, v_lru_conv_b, v_lru_w_gate_a, v_lru_b_gate_a, v_lru_w_gate_x, v_lru_b_gate_x, v_lru_lambda, v_lru_w_o, v_ffn_w_gu, v_ffn_w_down):
    seq, d = x.shape[1], x.shape[2]
    assert seq % CHUNK == 0
    n_real = N_META + seq
    t_all = -(-n_real // LANES) * LANES
    tm = _pick(t_all, (384, 256, 128))
    tq = _pick(seq, (512, 256, 128, 64))
    depth = norm_mix.shape[0]
    n_mla, n_lru = mla_w_in.shape[0], lru_w_in.shape[0]
    lq, lkv = mla_q_norm.shape[1], mla_kv_norm.shape[1]
    w_in_cols = lq + lkv + LANES
    heads_local = mla_w_uq.shape[2] // (QK_NOPE + QK_ROPE)
    n_heads = heads_local * N_DEV
    dr = lru_w_gate_a.shape[1] * lru_w_gate_a.shape[2]
    scale = (QK_NOPE + QK_ROPE) ** -0.5
    cx, cy, cc = _mesh_pos()
    core = jnp.reshape(cc, (1,)).astype(jnp.int32)
    my_slot = jnp.reshape(4 * cx + 2 * cy + cc, (1,)).astype(jnp.int32)

    def pad_cols(w, cols):
        return jnp.pad(w, ((0, 0), (0, cols - w.shape[1])))

    def pad_heads(w):
        k_all = w.shape[0]
        w3 = w.reshape(k_all, heads_local, QK_NOPE + QK_ROPE)
        return jnp.pad(w3, ((0, 0), (0, 0), (0, HEAD_W - QK_NOPE - QK_ROPE))).reshape(k_all, heads_local * HEAD_W)

    def unpad_heads(w):
        k_all = w.shape[0]
        return w.reshape(k_all, heads_local, HEAD_W)[:, :, :QK_NOPE + QK_ROPE].reshape(k_all, -1)

    small_rows = N_META + n_lru * 4 + 2 * n_lru
    small_pad = -(-small_rows // SUBLANES) * SUBLANES

    def pack_small(meta, conv_w, conv_b, lam):
        rows = jnp.concatenate([meta, conv_w.reshape(n_lru * 4, -1), conv_b, lam], axis=0)
        return jnp.pad(rows, ((0, small_pad - small_rows), (0, 0)))

    def unpack_small(p):
        o1 = N_META + n_lru * 4
        return (p[:N_META], p[N_META:o1].reshape(n_lru, 4, -1), p[o1:o1 + n_lru], p[o1 + n_lru:o1 + 2 * n_lru])

    (small_full,), small_done = _all_gather([pack_small(meta_tokens, lru_conv_w, lru_conv_b, lru_lambda)], name="ag_small")
    small_full = jnp.transpose(small_full, (1, 0, 2)).reshape(small_pad, -1)
    meta_full, conv_w_full, conv_b_full, lam_full = unpack_small(small_full)

    def wire(w):
        return (w + small_done).astype(BF16)

    mla_shards, lru_shards, ffn_shards = [], [], []
    for j in range(n_mla):
        mla_shards.append([wire(pad_cols(mla_w_in[j], w_in_cols)), wire(pad_heads(mla_w_uq[j])), wire(mla_w_ukv[j]),
                           wire(mla_w_o[j])])
    for j in range(n_lru):
        lru_shards.append([wire(lru_w_in[j]), wire(lru_w_o[j])])
    for layer in range(depth):
        ffn_shards.append([wire(ffn_w_gu[layer]), wire(ffn_w_down[layer])])

    n_sub = 2 * depth
    groups = []
    for layer in range(depth):
        groups += [mla_shards[layer // 2] if layer % 2 == 0 else lru_shards[layer // 2], ffn_shards[layer]]
    slot_idx = 4 * cx + 2 * cy + cc
    ag_own = []
    for gi, shards in enumerate(groups):
        lands = [lax.dynamic_update_slice(lax.empty((N_DEV,) + s.shape, s.dtype), s[None], (slot_idx, 0, 0)) for s in shards]
        ag_own.append(_exchange_start(shards, lands, _ag_plan_own, 4, name=f"ag{gi}_start"))
    ag_pass = [None] * n_sub
    weights = [None] * n_sub

    def ag_landed(gi, after):
        _, lands = _exchange_wait(ag_own[gi], _ag_plan_own, after, name=f"ag{gi}_wait")
        ag_pass[gi] = _exchange_start([], lands, _ag_plan_pass, 3, name=f"ag{gi}_pass")
        return ag_pass[gi][4][0, 0]

    def ag_done(gi, after):
        _, weights[gi] = _exchange_wait(ag_pass[gi], _ag_plan_pass, after, name=f"ag{gi}_pass_wait")

    cos, sin = _rope_tables(t_all)
    zeros_tail = jnp.zeros((t_all - n_real, d), F32)
    started = ag_own[0][4][0, 0]
    for st in ag_own[1:]:
        started = started + st[4][0, 0]
    h = jnp.concatenate([meta_full + started, x[0], zeros_tail], axis=0)
    target = jnp.concatenate([jnp.zeros((N_META, d), F32), loss_target[0], zeros_tail], axis=0)

    attn_kw = dict(tm=tm, tq=tq, n_heads=n_heads, scale=scale, n_real=n_real)

    def lru_small(j):
        return (conv_w_full[j], conv_b_full[j][None, :], lru_w_gate_a[j].astype(BF16), lru_b_gate_a[j].reshape(1, dr),
                lru_w_gate_x[j].astype(BF16), lru_b_gate_x[j].reshape(1, dr), lam_full[j][None, :])

    def before_sublayer(k, act):
        tok = ag_landed(k, act) if k <= 1 else 0.0
        ag_done(k, act)
        if 1 <= k < n_sub - 1:
            tok = tok + ag_landed(k + 1, act)
        return tok

    saved = []
    for layer in range(depth):
        j = layer // 2
        g_mix = norm_mix[layer][None, :] + before_sublayer(2 * layer, h)
        if layer % 2 == 0:
            h_mid, mix_saved = _mla_layer_fwd(layer, h, g_mix, weights[2 * layer], mla_q_norm[j][None, :],
                                              mla_kv_norm[j][None, :], cos, sin, **attn_kw)
        else:
            h_mid, mix_saved = _lru_layer_fwd(layer, h, g_mix, weights[2 * layer], lru_small(j), tm=tm)
        g_ffn = norm_ffn[layer][None, :] + before_sublayer(2 * layer + 1, h_mid)
        h_out, ffn_saved = _ffn_layer_fwd(layer, h_mid, g_ffn, weights[2 * layer + 1], tm=tm)
        saved.append((h, h_mid, mix_saved, ffn_saved))
        h = h_out

    loss_part, dh, dh_b, dg_final = _loss_head(h, target, norm_final[None, :], name="loss_head", tm=tm, n_real=n_real)
    loss = lax.psum(loss_part[0, 0], ("x", "y", "c"))

    rs_sib, rs_chip, reduced = [None] * n_sub, [None] * n_sub, [None] * n_sub
    chip_idx = jnp.reshape(2 * cx + cy, (1,)).astype(jnp.int32)

    def rs_begin(k, grads):
        lands = [lax.empty((4,) + g.shape[1:], g.dtype) for g in grads]
        rs_sib[k] = _exchange_start(grads, lands, _rs_plan_sibling, 4, name=f"rs{k}_start")
        return rs_sib[k][4][0, 0]

    def rs_middle(k, after):
        grads, landed = _exchange_wait(rs_sib[k], _rs_plan_sibling, after, name=f"rs{k}_wait")
        parts = [_pair_add(g, l, core, name=f"rs{k}_add{a}", tr=_adam_row_tile(g.shape[1], g.shape[2], 4 * 1024 * 1024))
                 for a, (g, l) in enumerate(zip(grads, landed))]
        lands = [lax.empty((3,) + p.shape[1:], p.dtype) for p in parts]
        rs_chip[k] = _exchange_start(parts, lands, _rs_plan_chips, 3, name=f"rs{k}_chips")
        return rs_chip[k][4][0, 0]

    def rs_end(k, after):
        reduced[k] = _exchange_wait(rs_chip[k], _rs_plan_chips, after, name=f"rs{k}_chips_wait")

    d_norm_mix, d_norm_ffn = [None] * depth, [None] * depth
    d_qn, d_kvn = [None] * n_mla, [None] * n_mla
    d_small = {k: [None] * n_lru for k in ("cw", "cb", "wga", "bga", "wgx", "bgx", "lam")}
    tok, waiting = 0.0, None
    for layer in reversed(range(depth)):
        j = layer // 2
        h_in, h_mid, mix_saved, ffn_saved = saved[layer]
        dh, dh_b, d_norm_ffn[layer], ffn_g = _ffn_layer_bwd(layer, dh, dh_b, h_mid, ffn_saved, norm_ffn[layer][None, :] + tok,
                                                            weights[2 * layer + 1], tm=tm)
        tok = rs_begin(2 * layer + 1, ffn_g)
        if waiting is not None:
            tok = tok + rs_middle(waiting, dh)
        waiting = 2 * layer + 1
        if layer == 0:
            tok = tok + rs_middle(waiting, dh)
            waiting = None
        g_mix = norm_mix[layer][None, :] + tok
        if layer % 2 == 0:
            dh, dh_b, d_norm_mix[layer], d_qn[j], d_kvn[j], mix_g = _mla_layer_bwd(
                layer, dh, dh_b, h_in, mix_saved, g_mix, weights[2 * layer], mla_q_norm[j][None, :], mla_kv_norm[j][None, :],
                cos, sin, **attn_kw)
        else:
            dh, dh_b, d_norm_mix[layer], dsmall, mix_g = _lru_layer_bwd(layer, dh, dh_b, h_in, mix_saved, g_mix,
                                                                        weights[2 * layer], lru_small(j), tm=tm)
            for key, val in zip(("cw", "cb", "wga", "bga", "wgx", "bgx", "lam"), dsmall):
                d_small[key][j] = val
        tok = rs_begin(2 * layer, mix_g)
        if waiting is not None:
            tok = tok + rs_middle(waiting, dh)
        waiting = 2 * layer
    rs_middle(waiting, dh)

    grad_x = dh[N_META:n_real][None]

    d_meta = dh[:N_META]
    small_grad = pack_small(d_meta, jnp.stack(d_small["cw"], axis=0), jnp.concatenate(d_small["cb"], axis=0),
                            jnp.concatenate(d_small["lam"], axis=0))
    rep_grads = [
        jnp.concatenate(d_norm_mix, axis=0), jnp.concatenate(d_norm_ffn, axis=0), dg_final,
        jnp.concatenate(d_qn, axis=0), jnp.concatenate(d_kvn, axis=0),
        jnp.stack(d_small["wga"], axis=0).reshape(-1, LANES), jnp.concatenate(d_small["bga"], axis=0),
        jnp.stack(d_small["wgx"], axis=0).reshape(-1, LANES), jnp.concatenate(d_small["bgx"], axis=0),
    ]
    small_srcs = [small_grad] + [jnp.pad(g, ((0, -g.shape[0] % SUBLANES), (0, 0))) for g in rep_grads]
    small_lands = [lax.dynamic_update_slice(lax.empty((N_DEV,) + s.shape, s.dtype), s[None], (slot_idx, 0, 0))
                   for s in small_srcs]
    small_own = _exchange_start(small_srcs, small_lands, _ag_plan_own, 4, name="ag_grads_start")

    res = {}

    def adam_sharded(nm, k, a, idx, n_layers, w, m, v):
        parts, landed = reduced[k]
        r_all, c_all = landed[a].shape[1], landed[a].shape[2]
        res[nm] = _adamw(w.reshape(r_all, c_all), m.reshape(r_all, c_all), v.reshape(r_all, c_all), landed[a], (0, 1, 2),
                         name=f"adamw_{nm}{idx}", tr=_adam_row_tile(r_all, c_all, 2 * 1024 * 1024), own=(parts[a], chip_idx),
                         stack=(idx, n_layers, res.get(nm)))

    after = small_own[4]
    for k in reversed(range(n_sub)):
        rs_end(k, after)
        layer, j = k // 2, k // 4
        if k % 2 == 1:
            adam_sharded("ffn_w_gu", k, 0, layer, depth, ffn_w_gu[layer], m_ffn_w_gu[layer], v_ffn_w_gu[layer])
            adam_sharded("ffn_w_down", k, 1, layer, depth, ffn_w_down[layer], m_ffn_w_down[layer], v_ffn_w_down[layer])
            after = res["ffn_w_down"][0]
        elif layer % 2 == 0:
            adam_sharded("mla_w_in", k, 0, j, n_mla, pad_cols(mla_w_in[j], w_in_cols), pad_cols(m_mla_w_in[j], w_in_cols),
                         pad_cols(v_mla_w_in[j], w_in_cols))
            adam_sharded("mla_w_uq", k, 1, j, n_mla, pad_heads(mla_w_uq[j]), pad_heads(m_mla_w_uq[j]), pad_heads(v_mla_w_uq[j]))
            adam_sharded("mla_w_ukv", k, 2, j, n_mla, mla_w_ukv[j], m_mla_w_ukv[j], v_mla_w_ukv[j])
            adam_sharded("mla_w_o", k, 3, j, n_mla, mla_w_o[j], m_mla_w_o[j], v_mla_w_o[j])
            after = res["mla_w_o"][0]
        else:
            adam_sharded("lru_w_in", k, 0, j, n_lru, lru_w_in[j], m_lru_w_in[j], v_lru_w_in[j])
            adam_sharded("lru_w_o", k, 1, j, n_lru, lru_w_o[j], m_lru_w_o[j], v_lru_w_o[j])
            after = res["lru_w_o"][0]
    res["mla_w_in"] = [t[:, :, :lq + lkv + QK_ROPE] for t in res["mla_w_in"]]
    res["mla_w_uq"] = [t.reshape(n_mla, lq, heads_local, HEAD_W)[:, :, :, :QK_NOPE + QK_ROPE].reshape(n_mla, lq, -1)
                       for t in res["mla_w_uq"]]

    _, small_lands = _exchange_wait(small_own, _ag_plan_own, after, name="ag_grads_wait")
    small_pass = _exchange_start([], small_lands, _ag_plan_pass, 3, name="ag_grads_pass")
    _, all_small = _exchange_wait(small_pass, _ag_plan_pass, after, name="ag_grads_pass_wait")
    slot_order = tuple(range(N_DEV))

    def adam_rep(terms, w, m, v, tag):
        r_pad, c_all = terms.shape[1], terms.shape[2]

        def prep(t):
            t2 = t.reshape(-1, c_all)
            return jnp.pad(t2, ((0, r_pad - t2.shape[0]), (0, 0)))

        outs = _adamw(prep(w), prep(m), prep(v), terms, slot_order, name=f"adamw_{tag}", tr=_adam_row_tile(r_pad, c_all))
        n_rows = w.size // c_all
        return [o[:n_rows].reshape(w.shape) for o in outs]

    small_w = pack_small(meta_tokens, lru_conv_w, lru_conv_b, lru_lambda)
    small_m = pack_small(m_meta_tokens, m_lru_conv_w, m_lru_conv_b, m_lru_lambda)
    small_v = pack_small(v_meta_tokens, v_lru_conv_w, v_lru_conv_b, v_lru_lambda)
    small_out = _adamw(small_w, small_m, small_v, all_small[0], slot_order, name="adamw_small", tr=small_pad, col_block=my_slot)
    small_out = [unpack_small(o) for o in small_out]
    for idx, key in enumerate(("meta_tokens", "lru_conv_w", "lru_conv_b", "lru_lambda")):
        res[key] = [small_out[k][idx] for k in range(4)]

    res["norm_mix"] = adam_rep(all_small[1], norm_mix, m_norm_mix, v_norm_mix, "norm_mix")
    res["norm_ffn"] = adam_rep(all_small[2], norm_ffn, m_norm_ffn, v_norm_ffn, "norm_ffn")
    res["norm_final"] = adam_rep(all_small[3], norm_final, m_norm_final, v_norm_final, "norm_final")
    res["mla_q_norm"] = adam_rep(all_small[4], mla_q_norm, m_mla_q_norm, v_mla_q_norm, "mla_q_norm")
    res["mla_kv_norm"] = adam_rep(all_small[5], mla_kv_norm, m_mla_kv_norm, v_mla_kv_norm, "mla_kv_norm")
    res["lru_w_gate_a"] = adam_rep(all_small[6], lru_w_gate_a, m_lru_w_gate_a, v_lru_w_gate_a, "lru_w_gate_a")
    res["lru_b_gate_a"] = adam_rep(all_small[7], lru_b_gate_a, m_lru_b_gate_a, v_lru_b_gate_a, "lru_b_gate_a")
    res["lru_w_gate_x"] = adam_rep(all_small[8], lru_w_gate_x, m_lru_w_gate_x, v_lru_w_gate_x, "lru_w_gate_x")
    res["lru_b_gate_x"] = adam_rep(all_small[9], lru_b_gate_x, m_lru_b_gate_x, v_lru_b_gate_x, "lru_b_gate_x")

    names = ["meta_tokens", "norm_mix", "norm_ffn", "norm_final", "mla_w_in", "mla_q_norm", "mla_kv_norm", "mla_w_uq",
             "mla_w_ukv", "mla_w_o", "lru_w_in", "lru_conv_w", "lru_conv_b", "lru_w_gate_a", "lru_b_gate_a", "lru_w_gate_x",
             "lru_b_gate_x", "lru_lambda", "lru_w_o", "ffn_w_gu", "ffn_w_down"]
    shapes = dict(meta_tokens=meta_tokens, norm_mix=norm_mix, norm_ffn=norm_ffn, norm_final=norm_final, mla_w_in=mla_w_in,
                  mla_q_norm=mla_q_norm, mla_kv_norm=mla_kv_norm, mla_w_uq=mla_w_uq, mla_w_ukv=mla_w_ukv, mla_w_o=mla_w_o,
                  lru_w_in=lru_w_in, lru_conv_w=lru_conv_w, lru_conv_b=lru_conv_b, lru_w_gate_a=lru_w_gate_a,
                  lru_b_gate_a=lru_b_gate_a, lru_w_gate_x=lru_w_gate_x, lru_b_gate_x=lru_b_gate_x, lru_lambda=lru_lambda,
                  lru_w_o=lru_w_o, ffn_w_gu=ffn_w_gu, ffn_w_down=ffn_w_down)
    outs = [loss, grad_x]
    for k in range(4):
        outs += [res[nm][k].reshape(shapes[nm].shape) for nm in names]
    return tuple(outs)
```

```python
import math

import jax
import jax.numpy as jnp
from jax import lax
from jax.experimental import pallas as pl
from jax.experimental.pallas import tpu as pltpu

F32 = jnp.float32
BF16 = jnp.bfloat16
MESH = pl.DeviceIdType.MESH

N_META = 16
CHUNK = 64
QK_NOPE = 128
QK_ROPE = 64
V_HEAD = 128
HEAD_W = 256
ROPE_THETA = 10000.0
LRU_C = 8.0
RMS_EPS = 1e-6
NEG_BIG = -1e30
ADAM_LR, ADAM_B1, ADAM_B2, ADAM_EPS, ADAM_WD, ADAM_STEP = 0.001, 0.9, 0.999, 1e-08, 0.01, 10

LANES = 128
SUBLANES = 8
VMEM_LIMIT_BYTES = 52 * 1024 * 1024
N_DEV = 8

_NT = (((1,), (1,)), ((), ()))
_TN = (((0,), (0,)), ((), ()))
_DIVS = (2048, 1024, 512, 256, 128)
_ROW_TILES = (1408, 1024, 512, 256, 128)


def _params(dims):
    return pltpu.CompilerParams(dimension_semantics=dims, vmem_limit_bytes=VMEM_LIMIT_BYTES)


def _pick(n, candidates):
    for c in candidates:
        if c <= n and n % c == 0:
            return c
    return n


def _sigmoid(z):
    return 1.0 / (1.0 + jnp.exp(-z))


def _gelu(x):
    c = math.sqrt(2.0 / math.pi)
    return 0.5 * x * (1.0 + jnp.tanh(c * (x + 0.044715 * x * x * x)))


def _gelu_grad(x):
    c = math.sqrt(2.0 / math.pi)
    th = jnp.tanh(c * (x + 0.044715 * x * x * x))
    return 0.5 * (1.0 + th) + 0.5 * x * (1.0 - th * th) * c * (1.0 + 3.0 * 0.044715 * x * x)


def _neg_expm1(x):
    poly = -x * (1.0 + x * (1.0 / 2.0) * (1.0 + x * (1.0 / 3.0) * (1.0 + x * (1.0 / 4.0) * (
        1.0 + x * (1.0 / 5.0) * (1.0 + x * (1.0 / 6.0) * (1.0 + x * (1.0 / 7.0)))))))
    return jnp.where(x > -0.25, poly, 1.0 - jnp.exp(x))


def _softplus_neg(lam):
    e = jnp.exp(-jnp.abs(lam))
    log1p = jnp.where(e > 1e-4, jnp.log(1.0 + e), e * (1.0 - e * (0.5 - e * (1.0 / 3.0))))
    return jnp.maximum(-lam, 0.0) + log1p


def _rot_half(x):
    lane = lax.broadcasted_iota(jnp.int32, x.shape, 1)
    first = (lane % QK_ROPE) < (QK_ROPE // 2)
    return jnp.where(first, -pltpu.roll(x, LANES - QK_ROPE // 2, 1), pltpu.roll(x, QK_ROPE // 2, 1))


def _rope(x, cos, sin):
    return x * cos + _rot_half(x) * sin


def _unrope(g, cos, sin):
    return g * cos - _rot_half(g) * sin


def _mm_nn(a, b, *, name, out_dtype, tm, tn, tk, b_blocked=False, res=None, epilogue=None, extras=()):
    m_all, k_all = a.shape
    if b_blocked:
        g_all, kb, nb = b.shape
        n_all = g_all * nb
        assert nb % tn == 0
        r = nb // tn
        b_spec = pl.BlockSpec((None, tk, tn), lambda j, i, k: (j // r, k, j % r))
    else:
        kb, n_all = b.shape
        b_spec = pl.BlockSpec((tk, tn), lambda j, i, k: (k, j))
    assert kb == k_all and m_all % tm == 0 and n_all % tn == 0 and k_all % tk == 0
    nm, nn, nk = m_all // tm, n_all // tn, k_all // tk
    in_specs = [pl.BlockSpec((tm, tk), lambda j, i, k: (i, k)), b_spec]
    operands = [a, b]
    has_res = res is not None
    if has_res:
        in_specs.append(pl.BlockSpec((tm, tn), lambda j, i, k: (i, j)))
        operands.append(res)
    for e in extras:
        in_specs.append(pl.BlockSpec((tm, e.shape[1]), lambda j, i, k: (i, 0)))
        operands.append(e)
    n_ex = len(extras)

    def body(*refs):
        a_ref, b_ref = refs[0], refs[1]
        pos = 2
        res_ref = None
        if has_res:
            res_ref = refs[pos]
            pos += 1
        ex_refs = refs[pos:pos + n_ex]
        pos += n_ex
        o_ref = refs[pos]
        acc_ref = refs[pos + 1] if nk > 1 else None

        def finish(acc):
            if has_res:
                acc = acc + res_ref[...]
            if epilogue is not None:
                acc = epilogue(acc, *ex_refs)
            o_ref[...] = acc.astype(o_ref.dtype)

        prod = jnp.dot(a_ref[...], b_ref[...], preferred_element_type=F32)
        if nk == 1:
            finish(prod)
        else:
            k = pl.program_id(2)

            @pl.when(k == 0)
            def _():
                acc_ref[...] = prod

            @pl.when(k > 0)
            def _():
                acc_ref[...] += prod

            @pl.when(k == nk - 1)
            def _():
                finish(acc_ref[...])

    return pl.pallas_call(
        body, name=name, grid=(nn, nm, nk), in_specs=in_specs,
        out_specs=pl.BlockSpec((tm, tn), lambda j, i, k: (i, j)),
        out_shape=jax.ShapeDtypeStruct((m_all, n_all), out_dtype),
        scratch_shapes=[pltpu.VMEM((tm, tn), F32)] if nk > 1 else [],
        compiler_params=_params(("parallel", "parallel", "arbitrary")),
    )(*operands)


def _mm_nt(a, b, *, name, out_dtype, tm, tn, tk, b_blocked=False):
    if a.ndim == 3:
        n_planes, m_all, kp = a.shape
        k_all = n_planes * kp
    else:
        n_planes, (m_all, k_all) = 0, a.shape
    if b_blocked and tk == k_all and b.shape[0] > 1:
        g_all, n_all, nb = b.shape
        assert g_all * nb == k_all and m_all % tm == 0 and n_all % tn == 0
        per_plane = kp // nb if n_planes else 0

        def whole_body(a_ref, b_ref, o_ref):
            acc = None
            for g in range(g_all):
                a_g = a_ref[g // per_plane, :, (g % per_plane) * nb:(g % per_plane + 1) * nb] if n_planes else a_ref[:, g * nb:(g + 1) * nb]
                prod = lax.dot_general(a_g, b_ref[g], _NT, preferred_element_type=F32)
                acc = prod if acc is None else acc + prod
            o_ref[...] = acc.astype(o_ref.dtype)

        a_whole = (pl.BlockSpec((n_planes, tm, kp), lambda j, i: (0, i, 0)) if n_planes
                   else pl.BlockSpec((tm, k_all), lambda j, i: (i, 0)))
        return pl.pallas_call(
            whole_body, name=name, grid=(n_all // tn, m_all // tm),
            in_specs=[a_whole, pl.BlockSpec((g_all, tn, nb), lambda j, i: (0, j, 0))],
            out_specs=pl.BlockSpec((tm, tn), lambda j, i: (i, j)),
            out_shape=jax.ShapeDtypeStruct((m_all, n_all), out_dtype),
            compiler_params=_params(("parallel", "parallel")),
        )(a, b)
    if n_planes:
        assert kp % tk == 0
        rp = kp // tk
        a_spec = pl.BlockSpec((None, tm, tk), lambda j, i, k: (k // rp, i, k % rp))
    else:
        a_spec = pl.BlockSpec((tm, tk), lambda j, i, k: (i, k))
    if b_blocked:
        g_all, n_all, nb = b.shape
        assert g_all * nb == k_all and nb % tk == 0
        r = nb // tk
        b_spec = pl.BlockSpec((None, tn, tk), lambda j, i, k: (k // r, j, k % r))
    else:
        n_all, kb = b.shape
        assert kb == k_all
        b_spec = pl.BlockSpec((tn, tk), lambda j, i, k: (j, k))
    assert m_all % tm == 0 and n_all % tn == 0 and k_all % tk == 0
    nm, nn, nk = m_all // tm, n_all // tn, k_all // tk

    def body(a_ref, b_ref, o_ref, *scratch):
        prod = lax.dot_general(a_ref[...], b_ref[...], _NT, preferred_element_type=F32)
        if nk == 1:
            o_ref[...] = prod.astype(o_ref.dtype)
        else:
            acc_ref = scratch[0]
            k = pl.program_id(2)

            @pl.when(k == 0)
            def _():
                acc_ref[...] = prod

            @pl.when(k > 0)
            def _():
                acc_ref[...] += prod

            @pl.when(k == nk - 1)
            def _():
                o_ref[...] = acc_ref[...].astype(o_ref.dtype)

    return pl.pallas_call(
        body, name=name, grid=(nn, nm, nk),
        in_specs=[a_spec, b_spec],
        out_specs=pl.BlockSpec((tm, tn), lambda j, i, k: (i, j)),
        out_shape=jax.ShapeDtypeStruct((m_all, n_all), out_dtype),
        scratch_shapes=[pltpu.VMEM((tm, tn), F32)] if nk > 1 else [],
        compiler_params=_params(("parallel", "parallel", "arbitrary")),
    )(a, b)


def _mm_tn(a, b, *, name, out_dtype, tm, tn, tk, out_block=None):
    t_all, m_all = a.shape
    if b.ndim == 3:
        n_planes, tb, n_p = b.shape
        assert n_p % tn == 0
        rq = n_p // tn
        n_all = n_planes * n_p
        b_spec = pl.BlockSpec((None, tk, tn), lambda i, j, k: (j // rq, k, j % rq))
    else:
        tb, n_all = b.shape
        b_spec = pl.BlockSpec((tk, tn), lambda i, j, k: (k, j))
    assert tb == t_all and m_all % tm == 0 and n_all % tn == 0 and t_all % tk == 0
    nm, nn, nk = m_all // tm, n_all // tn, t_all // tk
    if out_block is None:
        out_shape = jax.ShapeDtypeStruct((m_all, n_all), out_dtype)
        out_spec = pl.BlockSpec((tm, tn), lambda i, j, k: (i, j))
    else:
        assert out_block % tn == 0 and n_all % out_block == 0
        r = out_block // tn
        out_shape = jax.ShapeDtypeStruct((n_all // out_block, m_all, out_block), out_dtype)
        out_spec = pl.BlockSpec((None, tm, tn), lambda i, j, k: (j // r, i, j % r))

    def body(a_ref, b_ref, o_ref, *scratch):
        prod = lax.dot_general(a_ref[...], b_ref[...], _TN, preferred_element_type=F32)
        if nk == 1:
            o_ref[...] = prod.astype(o_ref.dtype)
        else:
            acc_ref = scratch[0]
            k = pl.program_id(2)

            @pl.when(k == 0)
            def _():
                acc_ref[...] = prod

            @pl.when(k > 0)
            def _():
                acc_ref[...] += prod

            @pl.when(k == nk - 1)
            def _():
                o_ref[...] = acc_ref[...].astype(o_ref.dtype)

    return pl.pallas_call(
        body, name=name, grid=(nm, nn, nk),
        in_specs=[pl.BlockSpec((tk, tm), lambda i, j, k: (k, i)), b_spec],
        out_specs=out_spec, out_shape=out_shape,
        scratch_shapes=[pltpu.VMEM((tm, tn), F32)] if nk > 1 else [],
        compiler_params=_params(("parallel", "parallel", "arbitrary")),
    )(a, b)


def _rmsnorm_fwd(x, g, *, name, tm):
    t_all, d = x.shape

    def body(x_ref, g_ref, o_ref):
        xv = x_ref[...]
        rstd = lax.rsqrt(jnp.mean(xv * xv, axis=-1, keepdims=True) + RMS_EPS)
        o_ref[...] = (xv * rstd * g_ref[...]).astype(o_ref.dtype)

    return pl.pallas_call(
        body, name=name, grid=(t_all // tm,),
        in_specs=[pl.BlockSpec((tm, d), lambda i: (i, 0)), pl.BlockSpec((1, d), lambda i: (0, 0))],
        out_specs=pl.BlockSpec((tm, d), lambda i: (i, 0)),
        out_shape=jax.ShapeDtypeStruct((t_all, d), BF16),
        compiler_params=_params(("parallel",)),
    )(x, g)


def _rms_bwd_math(dy, xv, g):
    rstd = lax.rsqrt(jnp.mean(xv * xv, axis=-1, keepdims=True) + RMS_EPS)
    xhat = xv * rstd
    dxh = dy * g
    dx = rstd * (dxh - xhat * jnp.mean(dxh * xhat, axis=-1, keepdims=True))
    return dx, jnp.sum(dy * xhat, axis=0, keepdims=True)


def _rmsnorm_bwd(dy, x, g, res, *, name, tm):
    t_all, d = x.shape

    def body(dy_ref, x_ref, g_ref, res_ref, dx_ref, dxb_ref, dg_ref):
        dx, dg = _rms_bwd_math(dy_ref[...], x_ref[...], g_ref[...])
        tot = res_ref[...] + dx
        dx_ref[...] = tot
        dxb_ref[...] = tot.astype(BF16)

        @pl.when(pl.program_id(0) == 0)
        def _():
            dg_ref[...] = dg

        @pl.when(pl.program_id(0) > 0)
        def _():
            dg_ref[...] += dg

    row = pl.BlockSpec((tm, d), lambda i: (i, 0))
    vec = pl.BlockSpec((1, d), lambda i: (0, 0))
    return pl.pallas_call(
        body, name=name, grid=(t_all // tm,),
        in_specs=[row, row, vec, row], out_specs=[row, row, vec],
        out_shape=[jax.ShapeDtypeStruct((t_all, d), F32), jax.ShapeDtypeStruct((t_all, d), BF16),
                   jax.ShapeDtypeStruct((1, d), F32)],
        compiler_params=_params(("arbitrary",)),
    )(dy, x, g, res)


def _loss_head(h, target, g, *, name, tm, n_real):
    t_all, d = h.shape

    def body(h_ref, t_ref, g_ref, loss_ref, dx_ref, dxb_ref, dg_ref):
        i = pl.program_id(0)
        xv = h_ref[...]
        gv = g_ref[...]
        rstd = lax.rsqrt(jnp.mean(xv * xv, axis=-1, keepdims=True) + RMS_EPS)
        y = xv * rstd * gv
        row = i * tm + lax.broadcasted_iota(jnp.int32, (tm, 1), 0)
        valid = (row >= N_META) & (row < n_real)
        err = jnp.where(valid, y - t_ref[...], 0.0)
        part = 0.5 * jnp.sum(jnp.mean(err * err, axis=-1, keepdims=True), axis=0, keepdims=True)
        dx, dg = _rms_bwd_math(err * (1.0 / d), xv, gv)
        dx_ref[...] = dx
        dxb_ref[...] = dx.astype(BF16)

        @pl.when(i == 0)
        def _():
            dg_ref[...] = dg
            loss_ref[...] = jnp.broadcast_to(part, loss_ref.shape)

        @pl.when(i > 0)
        def _():
            dg_ref[...] += dg
            loss_ref[...] += jnp.broadcast_to(part, loss_ref.shape)

    row = pl.BlockSpec((tm, d), lambda i: (i, 0))
    vec = pl.BlockSpec((1, d), lambda i: (0, 0))
    return pl.pallas_call(
        body, name=name, grid=(t_all // tm,),
        in_specs=[row, row, vec],
        out_specs=[pl.BlockSpec((1, LANES), lambda i: (0, 0)), row, row, vec],
        out_shape=[jax.ShapeDtypeStruct((1, LANES), F32), jax.ShapeDtypeStruct((t_all, d), F32),
                   jax.ShapeDtypeStruct((t_all, d), BF16), jax.ShapeDtypeStruct((1, d), F32)],
        compiler_params=_params(("arbitrary",)),
    )(h, target, g)


def _ffn_up(x, w_gu, *, name, tm):
    t_all, d = x.shape
    g_all, kb, nb = w_gu.shape
    half = g_all // 2
    f = half * nb
    assert kb == d and t_all % tm == 0

    def body(x_ref, wg_ref, wu_ref, gu_ref, act_ref):
        xv = x_ref[...]
        gv = jnp.dot(xv, wg_ref[...], preferred_element_type=F32)
        uv = jnp.dot(xv, wu_ref[...], preferred_element_type=F32)
        gu_ref[0] = gv
        gu_ref[1] = uv
        act_ref[...] = (gv * _sigmoid(gv) * uv).astype(act_ref.dtype)

    return pl.pallas_call(
        body, name=name, grid=(half, t_all // tm),
        in_specs=[pl.BlockSpec((tm, d), lambda j, i: (i, 0)), pl.BlockSpec((None, d, nb), lambda j, i: (j, 0, 0)),
                  pl.BlockSpec((None, d, nb), lambda j, i: (j + half, 0, 0))],
        out_specs=[pl.BlockSpec((2, tm, nb), lambda j, i: (0, i, j)), pl.BlockSpec((tm, nb), lambda j, i: (i, j))],
        out_shape=[jax.ShapeDtypeStruct((2, t_all, f), F32), jax.ShapeDtypeStruct((t_all, f), BF16)],
        compiler_params=_params(("parallel", "parallel")),
    )(x, w_gu, w_gu)


def _ffn_dact(dy, w_down, gu, *, name, tm, tn):
    t_all, d = dy.shape
    f = w_down.shape[0]
    assert t_all % tm == 0 and f % tn == 0

    def body(dy_ref, w_ref, gu_ref, o_ref):
        dact = lax.dot_general(dy_ref[...], w_ref[...], _NT, preferred_element_type=F32)
        gv, uv = gu_ref[0], gu_ref[1]
        sg = _sigmoid(gv)
        o_ref[0] = (dact * uv * (sg * (1.0 + gv * (1.0 - sg)))).astype(o_ref.dtype)
        o_ref[1] = (dact * gv * sg).astype(o_ref.dtype)

    return pl.pallas_call(
        body, name=name, grid=(f // tn, t_all // tm),
        in_specs=[pl.BlockSpec((tm, d), lambda j, i: (i, 0)), pl.BlockSpec((tn, d), lambda j, i: (j, 0)),
                  pl.BlockSpec((2, tm, tn), lambda j, i: (0, i, j))],
        out_specs=pl.BlockSpec((2, tm, tn), lambda j, i: (0, i, j)),
        out_shape=jax.ShapeDtypeStruct((2, t_all, f), BF16),
        compiler_params=_params(("parallel", "parallel")),
    )(dy, w_down, gu)


def _mla_prep_fwd(proj, qn, kvn, cos, sin, *, name, tm, lq, lkv):
    t_all, w = proj.shape

    def body(p_ref, qn_ref, kvn_ref, cos_ref, sin_ref, cq_ref, ckv_ref, kr_ref):
        pv = p_ref[...]
        xq = pv[:, :lq]
        xkv = pv[:, lq:lq + lkv]
        cq_ref[...] = (xq * lax.rsqrt(jnp.mean(xq * xq, axis=-1, keepdims=True) + RMS_EPS) * qn_ref[...]).astype(BF16)
        ckv_ref[...] = (xkv * lax.rsqrt(jnp.mean(xkv * xkv, axis=-1, keepdims=True) + RMS_EPS) * kvn_ref[...]).astype(BF16)
        kr_ref[...] = _rope(pv[:, lq + lkv:], cos_ref[...], sin_ref[...]).astype(BF16)

    def row(width):
        return pl.BlockSpec((tm, width), lambda i: (i, 0))

    def vec(width):
        return pl.BlockSpec((1, width), lambda i: (0, 0))

    return pl.pallas_call(
        body, name=name, grid=(t_all // tm,),
        in_specs=[row(w), vec(lq), vec(lkv), row(LANES), row(LANES)],
        out_specs=[row(lq), row(lkv), row(LANES)],
        out_shape=[jax.ShapeDtypeStruct((t_all, lq), BF16), jax.ShapeDtypeStruct((t_all, lkv), BF16),
                   jax.ShapeDtypeStruct((t_all, LANES), BF16)],
        compiler_params=_params(("parallel",)),
    )(proj, qn, kvn, cos, sin)


def _mla_prep_bwd(dcq, dckv, dkr_h, proj, qn, kvn, cos, sin, *, name, tm, lq, lkv):
    t_all, w = proj.shape
    n_heads = dkr_h.shape[0]

    def body(dcq_ref, dckv_ref, dkr_ref, p_ref, qn_ref, kvn_ref, cos_ref, sin_ref, dp_ref, dqn_ref, dkvn_ref):
        pv = p_ref[...]
        dxq, dqn = _rms_bwd_math(dcq_ref[...], pv[:, :lq], qn_ref[...])
        dxkv, dkvn = _rms_bwd_math(dckv_ref[...], pv[:, lq:lq + lkv], kvn_ref[...])
        dkr = dkr_ref[0]
        for hh in range(1, n_heads):
            dkr = dkr + dkr_ref[hh]
        dkr = _unrope(dkr, cos_ref[...], sin_ref[...])
        dp_ref[...] = jnp.concatenate([dxq, dxkv, dkr], axis=1).astype(BF16)

        @pl.when(pl.program_id(0) == 0)
        def _():
            dqn_ref[...] = dqn
            dkvn_ref[...] = dkvn

        @pl.when(pl.program_id(0) > 0)
        def _():
            dqn_ref[...] += dqn
            dkvn_ref[...] += dkvn

    def row(width):
        return pl.BlockSpec((tm, width), lambda i: (i, 0))

    def vec(width):
        return pl.BlockSpec((1, width), lambda i: (0, 0))

    return pl.pallas_call(
        body, name=name, grid=(t_all // tm,),
        in_specs=[row(lq), row(lkv), pl.BlockSpec((n_heads, tm, LANES), lambda i: (0, i, 0)), row(w),
                  vec(lq), vec(lkv), row(LANES), row(LANES)],
        out_specs=[row(w), vec(lq), vec(lkv)],
        out_shape=[jax.ShapeDtypeStruct((t_all, w), BF16), jax.ShapeDtypeStruct((1, lq), F32),
                   jax.ShapeDtypeStruct((1, lkv), F32)],
        compiler_params=_params(("arbitrary",)),
    )(dcq, dckv, dkr_h, proj, qn, kvn, cos, sin)


def _rope_q_epilogue(acc, cos_ref, sin_ref):
    parts = []
    for g in range(acc.shape[1] // LANES):
        blk = acc[:, g * LANES:(g + 1) * LANES]
        parts.append(_rope(blk, cos_ref[...], sin_ref[...]) if g % 2 == 1 else blk)
    return jnp.concatenate(parts, axis=1)


def _chunk_causal(rows, cols, row0=0):
    r = row0 + lax.broadcasted_iota(jnp.int32, (rows, cols), 0)
    c = lax.broadcasted_iota(jnp.int32, (rows, cols), 1)
    return (c >> 6) <= (r >> 6)


def _meta_keys(rows, cols):
    return lax.broadcasted_iota(jnp.int32, (rows, cols), 1) < N_META


def _attn_fwd(q, kv, kr, *, name, n_heads, tq, n_real, scale):
    t_all = q.shape[0]
    nq = (n_real - N_META) // tq
    assert N_META + nq * tq == n_real and tq % CHUNK == 0 and t_all >= LANES
    n_pad = t_all - n_real
    sub = tq // 2 if (tq // 2) % CHUNK == 0 else tq

    def body(q_ref, kv_ref, kr_ref, o_ref, lse_ref, k_scr, m_scr, l_scr, acc_scr):
        k_scr[:, :QK_NOPE] = kv_ref[:, :QK_NOPE]
        k_scr[:, QK_NOPE:] = kr_ref[...]
        if n_pad:
            o_ref[pl.ds(n_real, n_pad), :] = jnp.zeros((n_pad, V_HEAD), o_ref.dtype)
            lse_ref[pl.ds(n_real, n_pad), :] = jnp.zeros((n_pad, LANES), F32)

        def scores(qt, c0, width):
            return lax.dot_general(qt, k_scr[pl.ds(c0, width), :], _NT, preferred_element_type=F32) * scale

        def values(c0, width):
            return kv_ref[pl.ds(c0, width), QK_NOPE:]

        s = jnp.where(_meta_keys(LANES, LANES), scores(q_ref[pl.ds(0, LANES), :], 0, LANES), NEG_BIG)
        m = jnp.max(s, axis=-1, keepdims=True)
        p = jnp.exp(s - m)
        l = jnp.sum(p, axis=-1, keepdims=True)
        o_meta = jnp.dot(p.astype(BF16), values(0, LANES), preferred_element_type=F32) / l
        o_ref[pl.ds(0, N_META), :] = o_meta[:N_META].astype(o_ref.dtype)
        lse_ref[pl.ds(0, N_META), :] = jnp.broadcast_to((m + jnp.log(l))[:N_META], (N_META, LANES))

        parts = [(u * sub, sub) for u in range(tq // sub)]

        def accumulate(u0, s, vals):
            rows = pl.ds(u0, s.shape[0])
            m_prev = m_scr[rows, :]
            m_new = jnp.maximum(m_prev, jnp.max(s, axis=-1, keepdims=True))
            alpha = jnp.exp(m_prev - m_new)
            p = jnp.exp(s - m_new)
            l_scr[rows, :] = alpha * l_scr[rows, :] + jnp.sum(p, axis=-1, keepdims=True)
            acc_scr[rows, :] = alpha * acc_scr[rows, :] + jnp.dot(p.astype(BF16), vals, preferred_element_type=F32)
            m_scr[rows, :] = m_new

        def q_tile(i, carry):
            r0 = pl.multiple_of(N_META + i * tq, N_META)
            qts = [q_ref[pl.ds(r0 + u0, rows), :] for u0, rows in parts]
            m_scr[...] = jnp.full(m_scr.shape, NEG_BIG, F32)
            l_scr[...] = jnp.zeros(l_scr.shape, F32)
            acc_scr[...] = jnp.zeros(acc_scr.shape, F32)

            def full_blocks(j, width):
                c0 = pl.multiple_of(N_META + j * tq, N_META)
                for (u0, _), qt in zip(parts, qts):
                    accumulate(u0, scores(qt, c0, width), values(c0, width))

            def two_blocks(jj, c):
                full_blocks(2 * jj, 2 * tq)
                return c

            lax.fori_loop(0, i // 2, two_blocks, 0)

            @pl.when(i % 2 == 1)
            def _():
                full_blocks(i - 1, tq)

            for (u0, rows), qt in zip(parts, qts):
                width = u0 + rows
                s = jnp.concatenate([jnp.where(_meta_keys(rows, LANES), scores(qt, 0, LANES), NEG_BIG),
                                     jnp.where(_chunk_causal(rows, width, u0), scores(qt, r0, width), NEG_BIG)], axis=1)
                accumulate(u0, s, jnp.concatenate([values(0, LANES), values(r0, width)], axis=0))
            o_ref[pl.ds(r0, tq), :] = (acc_scr[...] / l_scr[...]).astype(o_ref.dtype)
            lse_ref[pl.ds(r0, tq), :] = jnp.broadcast_to(m_scr[...] + jnp.log(l_scr[...]), (tq, LANES))
            return carry

        lax.fori_loop(0, nq, q_tile, 0)

    def head(width):
        return pl.BlockSpec((t_all, width), lambda h: (0, h))

    return pl.pallas_call(
        body, name=name, grid=(n_heads,),
        in_specs=[head(HEAD_W), head(HEAD_W), pl.BlockSpec((t_all, LANES), lambda h: (0, 0))],
        out_specs=[head(V_HEAD), pl.BlockSpec((None, t_all, LANES), lambda h: (h, 0, 0))],
        out_shape=[jax.ShapeDtypeStruct((t_all, n_heads * V_HEAD), BF16),
                   jax.ShapeDtypeStruct((n_heads, t_all, LANES), F32)],
        scratch_shapes=[pltpu.VMEM((t_all, HEAD_W), BF16), pltpu.VMEM((tq, 1), F32), pltpu.VMEM((tq, 1), F32),
                        pltpu.VMEM((tq, V_HEAD), F32)],
        compiler_params=_params(("parallel",)),
    )(q, kv, kr)


def _attn_bwd(q, kv, kr, o, lse, do, cos, sin, *, name, n_heads, tq, n_real, scale):
    t_all = q.shape[0]
    nq = (n_real - N_META) // tq
    assert N_META + nq * tq == n_real and tq % CHUNK == 0 and t_all >= LANES
    n_pad = t_all - n_real

    def body(q_ref, kv_ref, kr_ref, o_ref, lse_ref, do_ref, cos_ref, sin_ref, dq_ref, dkv_ref, dkr_ref,
             k_scr, dk_scr, dv_scr, dq_scr):
        k_scr[:, :QK_NOPE] = kv_ref[:, :QK_NOPE]
        k_scr[:, QK_NOPE:] = kr_ref[...]
        dk_scr[...] = jnp.zeros(dk_scr.shape, F32)
        dv_scr[...] = jnp.zeros(dv_scr.shape, F32)
        if n_pad:
            dq_ref[pl.ds(n_real, n_pad), :] = jnp.zeros((n_pad, HEAD_W), dq_ref.dtype)

        def block(qt, dot, lse_t, delta, c0, width, mask):
            kb = k_scr[pl.ds(c0, width), :]
            s = lax.dot_general(qt, kb, _NT, preferred_element_type=F32) * scale
            p = jnp.exp(s - lse_t)
            if mask is not None:
                p = jnp.where(mask, p, 0.0)
            dp = lax.dot_general(dot, kv_ref[pl.ds(c0, width), QK_NOPE:], _NT, preferred_element_type=F32)
            ds = (p * (dp - delta) * scale).astype(BF16)
            dv_scr[pl.ds(c0, width), :] += lax.dot_general(p.astype(BF16), dot, _TN, preferred_element_type=F32)
            dk_scr[pl.ds(c0, width), :] += lax.dot_general(ds, qt, _TN, preferred_element_type=F32)
            return jnp.dot(ds, kb, preferred_element_type=F32)

        def write_dq(r0, rows, dq):
            cs, sn = cos_ref[pl.ds(r0, rows), :], sin_ref[pl.ds(r0, rows), :]
            dq_ref[pl.ds(r0, rows), :] = jnp.concatenate(
                [dq[:, :QK_NOPE], _unrope(dq[:, QK_NOPE:], cs, sn)], axis=1).astype(dq_ref.dtype)

        rows_m = lax.broadcasted_iota(jnp.int32, (LANES, LANES), 0) < N_META
        dot = do_ref[pl.ds(0, LANES), :]
        delta = jnp.sum(dot.astype(F32) * o_ref[pl.ds(0, LANES), :].astype(F32), axis=-1, keepdims=True)
        dq = block(q_ref[pl.ds(0, LANES), :], dot, lse_ref[pl.ds(0, LANES), :1], delta, 0, LANES,
                   _meta_keys(LANES, LANES) & rows_m)
        write_dq(0, N_META, dq[:N_META])

        def q_tile(i, carry):
            r0 = pl.multiple_of(N_META + i * tq, N_META)
            qt = q_ref[pl.ds(r0, tq), :]
            dot = do_ref[pl.ds(r0, tq), :]
            lse_t = lse_ref[pl.ds(r0, tq), :1]
            delta = jnp.sum(dot.astype(F32) * o_ref[pl.ds(r0, tq), :].astype(F32), axis=-1, keepdims=True)
            dq_scr[...] = block(qt, dot, lse_t, delta, 0, LANES, _meta_keys(tq, LANES))

            def two_blocks(jj, c):
                c0 = pl.multiple_of(N_META + 2 * jj * tq, N_META)
                dq_scr[...] += block(qt, dot, lse_t, delta, c0, 2 * tq, None)
                return c

            lax.fori_loop(0, i // 2, two_blocks, 0)

            @pl.when(i % 2 == 1)
            def _():
                c0 = pl.multiple_of(N_META + (i - 1) * tq, N_META)
                dq_scr[...] += block(qt, dot, lse_t, delta, c0, tq, None)

            dq_scr[...] += block(qt, dot, lse_t, delta, r0, tq, _chunk_causal(tq, tq))
            write_dq(r0, tq, dq_scr[...])
            return carry

        lax.fori_loop(0, nq, q_tile, 0)
        dk = dk_scr[...]
        dkv_ref[...] = jnp.concatenate([dk[:, :QK_NOPE], dv_scr[...]], axis=1).astype(dkv_ref.dtype)
        dkr_ref[...] = dk[:, QK_NOPE:]

    def head(width):
        return pl.BlockSpec((t_all, width), lambda h: (0, h))

    table = pl.BlockSpec((t_all, LANES), lambda h: (0, 0))
    per_head = pl.BlockSpec((None, t_all, LANES), lambda h: (h, 0, 0))
    return pl.pallas_call(
        body, name=name, grid=(n_heads,),
        in_specs=[head(HEAD_W), head(HEAD_W), table, head(V_HEAD), per_head, head(V_HEAD), table, table],
        out_specs=[head(HEAD_W), head(HEAD_W), per_head],
        out_shape=[jax.ShapeDtypeStruct((t_all, n_heads * HEAD_W), BF16), jax.ShapeDtypeStruct((t_all, n_heads * HEAD_W), BF16),
                   jax.ShapeDtypeStruct((n_heads, t_all, LANES), F32)],
        scratch_shapes=[pltpu.VMEM((t_all, HEAD_W), BF16), pltpu.VMEM((t_all, HEAD_W), F32), pltpu.VMEM((t_all, V_HEAD), F32),
                        pltpu.VMEM((tq, HEAD_W), F32)],
        compiler_params=_params(("parallel",)),
    )(q, kv, kr, o, lse, do, cos, sin)


LRU_ROWS = 128


def _shifted_back(ref, t0, rows, shift_max):
    main = ref[pl.ds(t0, rows), :]
    prev = ref[pl.ds(pl.multiple_of(jnp.maximum(t0 - SUBLANES, 0), SUBLANES), SUBLANES), :]
    prev = jnp.where(t0 > 0, prev, 0.0)
    ext = jnp.concatenate([prev, main], axis=0)
    return [main] + [pltpu.roll(ext, s, 0)[SUBLANES:, :] for s in range(1, shift_max + 1)]


def _shifted_ahead(ref, t0, rows, t_all, shift_max):
    main = ref[pl.ds(t0, rows), :]
    nxt = ref[pl.ds(pl.multiple_of(jnp.minimum(t0 + rows, t_all - SUBLANES), SUBLANES), SUBLANES), :]
    nxt = jnp.where(t0 + rows < t_all, nxt, 0.0)
    ext = jnp.concatenate([main, nxt], axis=0)
    return [main] + [pltpu.roll(ext, rows + SUBLANES - s, 0)[:rows, :] for s in range(1, shift_max + 1)]


def _conv_fwd(xp_ref, t0, rows, cw, cb):
    sh = _shifted_back(xp_ref, t0, rows, 3)
    out = cb + cw[3:4, :] * sh[0]
    for k in range(3):
        out = out + cw[k:k + 1, :] * sh[3 - k]
    return out, sh


def _lru_gates(xb, wga, bga, wgx, bgx, sp):
    xbb = xb.astype(BF16)
    r = _sigmoid(jnp.dot(xbb, wga, preferred_element_type=F32) + bga)
    ig = _sigmoid(jnp.dot(xbb, wgx, preferred_element_type=F32) + bgx)
    la = -LRU_C * r * sp
    a = jnp.exp(la)
    s = jnp.sqrt(_neg_expm1(2.0 * la))
    return xbb, r, ig, a, s


def _scan_tile(a, b, reverse):
    rows = a.shape[0]
    ridx = lax.broadcasted_iota(jnp.int32, a.shape, 0)
    s = 1
    while s < rows:
        if reverse:
            keep = ridx < rows - s
            a_sh, b_sh = pltpu.roll(a, rows - s, 0), pltpu.roll(b, rows - s, 0)
        else:
            keep = ridx >= s
            a_sh, b_sh = pltpu.roll(a, s, 0), pltpu.roll(b, s, 0)
        b = jnp.where(keep, a * b_sh + b, b)
        a = jnp.where(keep, a * a_sh, a)
        s *= 2
    return a, b


def _lru_fwd(xy, conv_w, conv_b, wga, bga, wgx, bgx, lam, *, name):
    t_all = xy.shape[0]
    dr = xy.shape[1] // 2
    c = LANES
    nblk = dr // c
    rows = LRU_ROWS
    nt = t_all // rows

    def body(xp_ref, yp_ref, cw_ref, cb_ref, wga_ref, bga_ref, wgx_ref, bgx_ref, lam_ref, hs_ref, hsy_ref):
        cw, cb = cw_ref[...], cb_ref[...]
        sp = _softplus_neg(lam_ref[...])

        def tile(t, h_in):
            t0 = pl.multiple_of(t * rows, rows)
            xb, _ = _conv_fwd(xp_ref, t0, rows, cw, cb)
            _, _, ig, a, s = _lru_gates(xb, wga_ref[0], bga_ref[...], wgx_ref[0], bgx_ref[...], sp)
            cum_a, h0 = _scan_tile(a, s * (ig * xb), reverse=False)
            hs = cum_a * h_in + h0
            hs_ref[pl.ds(t0, rows), :] = hs
            hsy_ref[pl.ds(t0, rows), :] = (hs * _gelu(yp_ref[pl.ds(t0, rows), :])).astype(BF16)
            return hs[rows - 1:, :]

        lax.fori_loop(0, nt, tile, jnp.zeros((1, c), F32))

    col = pl.BlockSpec((t_all, c), lambda b: (0, b))
    vec = pl.BlockSpec((1, c), lambda b: (0, b))
    wsp = pl.BlockSpec((1, c, c), lambda b: (b, 0, 0))
    return pl.pallas_call(
        body, name=name, grid=(nblk,),
        in_specs=[col, pl.BlockSpec((t_all, c), lambda b: (0, nblk + b)), pl.BlockSpec((4, c), lambda b: (0, b)), vec,
                  wsp, vec, wsp, vec, vec],
        out_specs=[col, col],
        out_shape=[jax.ShapeDtypeStruct((t_all, dr), F32), jax.ShapeDtypeStruct((t_all, dr), BF16)],
        compiler_params=_params(("parallel",)),
    )(xy, xy, conv_w, conv_b, wga, bga, wgx, bgx, lam)


def _lru_bwd(xy, hs, dhsy, conv_w, conv_b, wga, bga, wgx, bgx, lam, *, name):
    t_all = xy.shape[0]
    dr = xy.shape[1] // 2
    c = LANES
    nblk = dr // c
    rows = LRU_ROWS
    nt = t_all // rows

    def body(xp_ref, yp_ref, hs_ref, dh_ref, cw_ref, cb_ref, wga_ref, bga_ref, wgx_ref, bgx_ref, lam_ref,
             dxp_ref, dyp_ref, dcw_ref, dcb_ref, dwga_ref, dbga_ref, dwgx_ref, dbgx_ref, dlam_ref,
             xb_scr, r_scr, i_scr, a_scr):
        cw, cb = cw_ref[...], cb_ref[...]
        lamv = lam_ref[...]
        sp = _softplus_neg(lamv)
        sig_neg = 1.0 / (1.0 + jnp.exp(lamv))
        wga_v, wgx_v = wga_ref[0], wgx_ref[0]

        def recompute(t, carry):
            t0 = pl.multiple_of(t * rows, rows)
            xb, _ = _conv_fwd(xp_ref, t0, rows, cw, cb)
            _, r, ig, a, _ = _lru_gates(xb, wga_v, bga_ref[...], wgx_v, bgx_ref[...], sp)
            xb_scr[pl.ds(t0, rows), :] = xb
            r_scr[pl.ds(t0, rows), :] = r
            i_scr[pl.ds(t0, rows), :] = ig
            a_scr[pl.ds(t0, rows), :] = a
            return carry

        lax.fori_loop(0, nt, recompute, 0)
        dwga_ref[...] = jnp.zeros(dwga_ref.shape, F32)
        dwgx_ref[...] = jnp.zeros(dwgx_ref.shape, F32)

        def tile(ti, carry):
            lam_in, dbga, dbgx, dlam, dcw, dcb = carry
            t = nt - 1 - ti
            t0 = pl.multiple_of(t * rows, rows)
            a_now, a_next = _shifted_ahead(a_scr, t0, rows, t_all, 1)
            yp = yp_ref[pl.ds(t0, rows), :]
            dhy = dh_ref[pl.ds(t0, rows), :]
            cum_a, lam0 = _scan_tile(a_next, dhy * _gelu(yp), reverse=True)
            lam_t = cum_a * lam_in + lam0
            hs_now, hs_prev = _shifted_back(hs_ref, t0, rows, 1)
            da = lam_t * hs_prev
            xb = xb_scr[pl.ds(t0, rows), :]
            r = r_scr[pl.ds(t0, rows), :]
            ig = i_scr[pl.ds(t0, rows), :]
            la = -LRU_C * r * sp
            s = jnp.sqrt(_neg_expm1(2.0 * la))
            d_ixb = lam_t * s
            dla = da * a_now - (lam_t * ig * xb) * (a_now * a_now / s)
            dzr = dla * (-LRU_C * sp) * r * (1.0 - r)
            dzi = d_ixb * xb * ig * (1.0 - ig)
            dzr_b, dzi_b = dzr.astype(BF16), dzi.astype(BF16)
            xbb = xb.astype(BF16)
            dwga_ref[0] += lax.dot_general(xbb, dzr_b, _TN, preferred_element_type=F32)
            dwgx_ref[0] += lax.dot_general(xbb, dzi_b, _TN, preferred_element_type=F32)
            dxb = (d_ixb * ig + lax.dot_general(dzr_b, wga_v, _NT, preferred_element_type=F32)
                   + lax.dot_general(dzi_b, wgx_v, _NT, preferred_element_type=F32))
            xb_scr[pl.ds(t0, rows), :] = dxb
            dyp_ref[pl.ds(t0, rows), :] = (dhy * hs_now * _gelu_grad(yp)).astype(BF16)
            ahead = _shifted_ahead(xb_scr, t0, rows, t_all, 3)
            dxp = cw[3:4, :] * ahead[0]
            for k in range(3):
                dxp = dxp + cw[k:k + 1, :] * ahead[3 - k]
            dxp_ref[pl.ds(t0, rows), :] = dxp.astype(BF16)
            back = _shifted_back(xp_ref, t0, rows, 3)
            dcw_t = jnp.concatenate([jnp.sum(dxb * back[3 - k], axis=0, keepdims=True) for k in range(4)], axis=0)
            return (lam_t[:1, :], dbga + jnp.sum(dzr, axis=0, keepdims=True), dbgx + jnp.sum(dzi, axis=0, keepdims=True),
                    dlam + jnp.sum(dla * r, axis=0, keepdims=True), dcw + dcw_t, dcb + jnp.sum(dxb, axis=0, keepdims=True))

        zero = jnp.zeros((1, c), F32)
        _, dbga, dbgx, dlam, dcw, dcb = lax.fori_loop(0, nt, tile, (zero, zero, zero, zero, jnp.zeros((4, c), F32), zero))
        dbga_ref[...] = dbga
        dbgx_ref[...] = dbgx
        dlam_ref[...] = dlam * (LRU_C * sig_neg)
        dcw_ref[...] = dcw
        dcb_ref[...] = dcb

    col = pl.BlockSpec((t_all, c), lambda b: (0, b))
    col2 = pl.BlockSpec((t_all, c), lambda b: (0, nblk + b))
    vec = pl.BlockSpec((1, c), lambda b: (0, b))
    tap = pl.BlockSpec((4, c), lambda b: (0, b))
    wsp = pl.BlockSpec((1, c, c), lambda b: (b, 0, 0))
    vshape = jax.ShapeDtypeStruct((1, dr), F32)
    wshape = jax.ShapeDtypeStruct((nblk, c, c), F32)
    def planes_body(*refs):
        dxy_ref = refs[11]
        body(*refs[:11], dxy_ref.at[0], dxy_ref.at[1], *refs[12:])

    return pl.pallas_call(
        planes_body, name=name, grid=(nblk,),
        in_specs=[col, col2, col, col, tap, vec, wsp, vec, wsp, vec, vec],
        out_specs=[pl.BlockSpec((2, t_all, c), lambda b: (0, 0, b)), tap, vec, wsp, vec, wsp, vec, vec],
        out_shape=[jax.ShapeDtypeStruct((2, t_all, dr), BF16),
                   jax.ShapeDtypeStruct((4, dr), F32), vshape, wshape, vshape, wshape, vshape, vshape],
        scratch_shapes=[pltpu.VMEM((t_all, c), F32)] * 4,
        compiler_params=_params(("parallel",)),
    )(xy, xy, hs, dhsy, conv_w, conv_b, wga, bga, wgx, bgx, lam)


def _mesh_pos():
    return lax.axis_index("x"), lax.axis_index("y"), lax.axis_index("c")


def _all_gather(shards, *, name):
    n = len(shards)

    def body(*refs):
        ins, outs, token = refs[:n], refs[n:2 * n], refs[2 * n]
        send_sems, recv_sems, local_sems = refs[2 * n + 1:]
        token[...] = jnp.zeros(token.shape, token.dtype)
        x, y, c = _mesh_pos()
        me, sibling = (x, y, c), (x, y, 1 - c)
        chips = [(1 - x, y), (x, 1 - y), (1 - x, 1 - y)]
        slot = _slot

        def copy(a, k, block, to, src=None):
            dst = outs[a].at[slot(block)]
            return pltpu.make_async_remote_copy(
                src_ref=dst if src is None else src, dst_ref=dst, send_sem=send_sems.at[a, k],
                recv_sem=recv_sems.at[a, k], device_id=to, device_id_type=MESH)

        mine = [pltpu.make_async_copy(ins[a], outs[a].at[slot(me)], local_sems.at[a]) for a in range(n)]
        for cp in mine:
            cp.start()
        first = []
        for a in range(n):
            first.append(copy(a, 0, me, sibling, src=ins[a]))
            first += [copy(a, 1 + j, me, (*chip, c), src=ins[a]) for j, chip in enumerate(chips)]
        for cp in first:
            cp.start()
        passed = []
        for a in range(n):
            for j, chip in enumerate(chips):
                copy(a, 1 + j, (*chip, c), me).wait_recv()
                fwd = copy(a, 4 + j, (*chip, c), sibling)
                fwd.start()
                passed.append(fwd)
        for a in range(n):
            copy(a, 0, sibling, me).wait_recv()
            for j, chip in enumerate(chips):
                copy(a, 4 + j, (*chip, 1 - c), me).wait_recv()
        for cp in first + passed:
            cp.wait_send()
        for cp in mine:
            cp.wait()

    any_spec = pl.BlockSpec(memory_space=pl.ANY)
    outs = pl.pallas_call(
        body, name=name,
        in_specs=[any_spec] * n, out_specs=[any_spec] * n + [pl.BlockSpec(memory_space=pltpu.VMEM)],
        out_shape=[jax.ShapeDtypeStruct((N_DEV,) + s.shape, s.dtype) for s in shards]
        + [jax.ShapeDtypeStruct((SUBLANES, LANES), F32)],
        scratch_shapes=[pltpu.SemaphoreType.DMA((n, 7)), pltpu.SemaphoreType.DMA((n, 7)), pltpu.SemaphoreType.DMA((n,))],
    )(*shards)
    return list(outs[:n]), outs[n][0, 0]


_HBM = pl.BlockSpec(memory_space=pltpu.HBM)
_SEM = pl.BlockSpec(memory_space=pltpu.SEMAPHORE)
_ANY = pl.BlockSpec(memory_space=pl.ANY)
_EFFECT = pltpu.SideEffectType.DATAFLOW_SIDE_EFFECTING


def _slot(p):
    return 4 * p[0] + 2 * p[1] + p[2]


def _remote(src, dst, send, recv, idx, to):
    return pltpu.make_async_remote_copy(src_ref=src, dst_ref=dst, send_sem=send.at[idx], recv_sem=recv.at[idx],
                                        device_id=to, device_id_type=MESH)


def _ag_plan_own(a, src, land, send, recv):
    x, y, c = _mesh_pos()
    dst = land.at[_slot((x, y, c))]
    targets = [(x, y, 1 - c), (1 - x, y, c), (x, 1 - y, c), (1 - x, 1 - y, c)]
    return [_remote(src, dst, send, recv, 4 * a + k, to) for k, to in enumerate(targets)]


def _ag_plan_pass(a, src, land, send, recv):
    x, y, c = _mesh_pos()
    blocks = [land.at[_slot((px, py, c))] for px, py in ((1 - x, y), (x, 1 - y), (1 - x, 1 - y))]
    return [_remote(blk, blk, send, recv, 3 * a + k, (x, y, 1 - c)) for k, blk in enumerate(blocks)]


def _rs_plan_sibling(a, src, land, send, recv):
    x, y, c = _mesh_pos()
    return [_remote(src.at[2 * j + (1 - c)], land.at[j], send, recv, 4 * a + j, (x, y, 1 - c)) for j in range(4)]


def _rs_plan_chips(a, src, land, send, recv):
    x, y, c = _mesh_pos()
    out = []
    for k in (1, 2, 3):
        px = 1 - x if k & 2 else x
        py = 1 - y if k & 1 else y
        out.append(_remote(src.at[2 * px + py], land.at[k - 1], send, recv, 3 * a + k - 1, (px, py, c)))
    return out


def _in_hbm(a):
    return pltpu.with_memory_space_constraint(a, pltpu.HBM)


def _exchange_start(srcs, lands, plan, n_k, *, name):
    ns, n = len(srcs), len(lands)

    def body(*refs):
        src_refs, land_refs = refs[:ns], refs[ns:ns + n]
        send, recv = refs[ns + n], refs[ns + n + 1]
        token = refs[-1]
        for a in range(n):
            for cp in plan(a, src_refs[a] if ns else None, land_refs[a], send, recv):
                cp.start()
        token[...] = jnp.zeros(token.shape, token.dtype)

    bufs = list(srcs) + list(lands)
    outs = pl.pallas_call(
        body, name=name,
        out_shape=(pltpu.SemaphoreType.DMA((n * n_k,)), pltpu.SemaphoreType.DMA((n * n_k,)),
                   *[pltpu.HBM(b.shape, b.dtype) for b in bufs], jax.ShapeDtypeStruct((SUBLANES, LANES), F32)),
        in_specs=[_HBM] * (ns + n),
        out_specs=(_SEM, _SEM, *[_HBM] * (ns + n), pl.BlockSpec(memory_space=pltpu.VMEM)),
        input_output_aliases={i: 2 + i for i in range(ns + n)},
        compiler_params=pltpu.CompilerParams(has_side_effects=_EFFECT),
    )(*[_in_hbm(b) for b in bufs])
    return outs[0], outs[1], list(outs[2:2 + ns]), list(outs[2 + ns:2 + ns + n]), outs[-1]


def _exchange_wait(started, plan, after, *, name):
    send, recv, srcs, lands, _ = started
    ns, n = len(srcs), len(lands)

    def body(*refs):
        src_refs, land_refs = refs[:ns], refs[ns:ns + n]
        send_ref, recv_ref = refs[ns + n], refs[ns + n + 1]
        for a in range(n):
            for cp in plan(a, src_refs[a] if ns else None, land_refs[a], send_ref, recv_ref):
                cp.wait_send()
                cp.wait_recv()

    bufs = list(srcs) + list(lands)
    outs = pl.pallas_call(
        body, name=name,
        out_shape=tuple(pltpu.HBM(b.shape, b.dtype) for b in bufs),
        in_specs=[_HBM] * (ns + n) + [_SEM, _SEM, _ANY],
        out_specs=tuple([_HBM] * (ns + n)),
        input_output_aliases={i: i for i in range(ns + n)},
        compiler_params=pltpu.CompilerParams(has_side_effects=_EFFECT),
    )(*bufs, send, recv, after)
    return list(outs[:ns]), list(outs[ns:])


def _pair_add(grads, landed, core, *, name, tr):
    _, r_all, c_all = grads.shape

    def body(core_ref, g_ref, l_ref, o_ref):
        o_ref[...] = (g_ref[...].astype(F32) + l_ref[...].astype(F32)).astype(o_ref.dtype)

    return pl.pallas_call(
        body, name=name,
        grid_spec=pltpu.PrefetchScalarGridSpec(
            num_scalar_prefetch=1, grid=(4, r_all // tr),
            in_specs=[pl.BlockSpec((None, tr, c_all), lambda j, i, core_ref: (2 * j + core_ref[0], i, 0)),
                      pl.BlockSpec((None, tr, c_all), lambda j, i, core_ref: (j, i, 0))],
            out_specs=pl.BlockSpec((None, tr, c_all), lambda j, i, core_ref: (j, i, 0))),
        out_shape=jax.ShapeDtypeStruct((4, r_all, c_all), grads.dtype),
        compiler_params=_params(("parallel", "parallel")),
    )(core, grads, landed)


def _adamw_math(w, g, m, v):
    m2 = ADAM_B1 * m + (1.0 - ADAM_B1) * g
    v2 = ADAM_B2 * v + (1.0 - ADAM_B2) * (g * g)
    m_hat = m2 / (1.0 - ADAM_B1 ** ADAM_STEP)
    v_hat = v2 / (1.0 - ADAM_B2 ** ADAM_STEP)
    delta = -ADAM_LR * (m_hat / (jnp.sqrt(v_hat) + ADAM_EPS) + ADAM_WD * w)
    return delta, m2, v2


def _adamw(w, m, v, terms, order, *, name, tr, col_block=None, own=None, stack=None):
    r_all, c_all = w.shape
    n_slots = terms.shape[0]

    def body(*refs):
        if col_block is not None or own is not None:
            refs = refs[1:]
        own_ref = None
        if own is not None:
            own_ref, refs = refs[0], refs[1:]
        w_ref, m_ref, v_ref, t_ref, g_ref, d_ref, m2_ref, v2_ref = refs
        if own_ref is not None:
            g = own_ref[...].astype(F32) + t_ref[order[0]].astype(F32)
        else:
            g = t_ref[order[0]].astype(F32)
        for s in order[1:]:
            g = g + t_ref[s].astype(F32)
        delta, m2, v2 = _adamw_math(w_ref[...], g, m_ref[...], v_ref[...])
        g_ref[...] = g
        d_ref[...] = delta
        m2_ref[...] = m2
        v2_ref[...] = v2

    shape = jax.ShapeDtypeStruct((r_all, c_all), F32)
    if own is not None:
        layer, n_layers, prev = stack
        row = pl.BlockSpec((tr, c_all), lambda i, idx: (i, 0))
        slab = pl.BlockSpec((None, tr, c_all), lambda i, idx: (layer, i, 0))
        carried = [] if prev is None else list(prev)

        def stacked_body(*refs):
            body(*refs[:6], *refs[6 + len(carried):])

        return pl.pallas_call(
            stacked_body, name=name,
            grid_spec=pltpu.PrefetchScalarGridSpec(
                num_scalar_prefetch=1, grid=(r_all // tr,),
                in_specs=[pl.BlockSpec((None, tr, c_all), lambda i, idx: (idx[0], i, 0)), row, row, row,
                          pl.BlockSpec((n_slots, tr, c_all), lambda i, idx: (0, i, 0))] + [_ANY] * len(carried),
                out_specs=[slab] * 4),
            out_shape=[jax.ShapeDtypeStruct((n_layers, r_all, c_all), F32)] * 4,
            input_output_aliases={6 + k: k for k in range(len(carried))},
            compiler_params=_params(("parallel",)),
        )(own[1], own[0], w, m, v, terms, *carried)
    if col_block is None:
        row = pl.BlockSpec((tr, c_all), lambda i: (i, 0))
        return pl.pallas_call(
            body, name=name, grid=(r_all // tr,),
            in_specs=[row, row, row, pl.BlockSpec((n_slots, tr, c_all), lambda i: (0, i, 0))],
            out_specs=[row] * 4, out_shape=[shape] * 4, compiler_params=_params(("parallel",)),
        )(w, m, v, terms)
    row = pl.BlockSpec((tr, c_all), lambda i, blk: (i, 0))
    return pl.pallas_call(
        body, name=name,
        grid_spec=pltpu.PrefetchScalarGridSpec(
            num_scalar_prefetch=1, grid=(r_all // tr,),
            in_specs=[row, row, row, pl.BlockSpec((n_slots, tr, c_all), lambda i, blk: (0, i, blk[0]))],
            out_specs=[row] * 4),
        out_shape=[shape] * 4, compiler_params=_params(("parallel",)),
    )(col_block, w, m, v, terms)


def _rope_tables(t_all):
    pos = jnp.arange(t_all, dtype=F32)
    inv_freq = ROPE_THETA ** (-jnp.arange(0, QK_ROPE, 2, dtype=F32) / QK_ROPE)
    ang = pos[:, None] * inv_freq[None, :]
    cos, sin = jnp.cos(ang), jnp.sin(ang)
    return jnp.tile(cos, (1, LANES // (QK_ROPE // 2))), jnp.tile(sin, (1, LANES // (QK_ROPE // 2)))


def _adam_row_tile(r_all, c_all, block_bytes=512 * 1024):
    target = max(SUBLANES, block_bytes // (4 * c_all))
    return _pick(r_all, [t for t in (1024, 704, 512, 352, 256, 176, 128, 64, 32, 16, 8) if t <= target])


def _rows_natural(wg):
    return wg.reshape(wg.shape[0] * wg.shape[1], wg.shape[2])


def _mla_layer_fwd(tag, h, g_mix, ws, qn, kvn, cos, sin, *, tm, tq, n_heads, scale, n_real):
    w_in, w_uq, w_ukv, w_o = _rows_natural(ws[0]), ws[1], ws[2], _rows_natural(ws[3])
    t_all, d = h.shape
    lq, lkv = qn.shape[1], kvn.shape[1]
    tmb = _pick(t_all, _ROW_TILES)
    hn = _rmsnorm_fwd(h, g_mix, name=f"norm_mix{tag}", tm=tm)
    proj = _mm_nn(hn, w_in, name=f"mla_in{tag}", out_dtype=F32, tm=tmb, tn=w_in.shape[1], tk=_pick(d, _DIVS))
    cq, ckv, kr = _mla_prep_fwd(proj, qn, kvn, cos, sin, name=f"mla_prep{tag}", tm=tm, lq=lq, lkv=lkv)
    q = _mm_nn(cq, w_uq, name=f"mla_q{tag}", out_dtype=BF16, tm=tmb, tn=w_uq.shape[2], tk=lq, b_blocked=True,
               epilogue=_rope_q_epilogue, extras=(cos, sin))
    kv = _mm_nn(ckv, w_ukv, name=f"mla_kv{tag}", out_dtype=BF16, tm=tmb, tn=w_ukv.shape[2], tk=lkv, b_blocked=True)
    o, lse = _attn_fwd(q, kv, kr, name=f"attn_fwd{tag}", n_heads=n_heads, tq=tq, n_real=n_real, scale=scale)
    h_mid = _mm_nn(o, w_o, name=f"mla_o{tag}", out_dtype=F32, tm=tm, tn=d, tk=o.shape[1], res=h)
    return h_mid, (hn, proj, cq, ckv, kr, q, kv, o, lse)


def _mla_layer_bwd(tag, dh, dh_b, h_in, saved, g_mix, ws, qn, kvn, cos, sin, *, tm, tq, n_heads, scale, n_real):
    hn, proj, cq, ckv, kr, q, kv, o, lse = saved
    w_in, w_uq, w_ukv, w_o = _rows_natural(ws[0]), ws[1], ws[2], _rows_natural(ws[3])
    t_all, d = h_in.shape
    lq, lkv = qn.shape[1], kvn.shape[1]
    ov = o.shape[1]
    tmb = _pick(t_all, _ROW_TILES)
    tn_d, tk_d = _pick(d, _DIVS[1:]), _pick(d, _DIVS)
    do = _mm_nt(dh_b, w_o, name=f"mla_do{tag}", out_dtype=BF16, tm=tmb, tn=_pick(ov, _DIVS[1:]), tk=tk_d)
    dw_o = _mm_tn(o, dh_b, name=f"mla_dwo{tag}", out_dtype=BF16, tm=_pick(ov, _DIVS[2:]), tn=tn_d, tk=t_all)
    dq, dkv, dkr_h = _attn_bwd(q, kv, kr, o, lse, do, cos, sin, name=f"attn_bwd{tag}", n_heads=n_heads, tq=tq, n_real=n_real,
                               scale=scale)
    hw, kw = w_uq.shape[2], w_ukv.shape[2]
    dw_uq = _mm_tn(cq, dq, name=f"mla_dwuq{tag}", out_dtype=BF16, tm=lq, tn=hw, tk=t_all, out_block=hw)
    dcq = _mm_nt(dq, w_uq, name=f"mla_dcq{tag}", out_dtype=F32, tm=tm, tn=lq, tk=dq.shape[1], b_blocked=True)
    dw_ukv = _mm_tn(ckv, dkv, name=f"mla_dwukv{tag}", out_dtype=BF16, tm=lkv, tn=kw, tk=t_all, out_block=kw)
    dckv = _mm_nt(dkv, w_ukv, name=f"mla_dckv{tag}", out_dtype=F32, tm=tm, tn=lkv, tk=dkv.shape[1], b_blocked=True)
    dproj, dqn, dkvn = _mla_prep_bwd(dcq, dckv, dkr_h, proj, qn, kvn, cos, sin, name=f"mla_prep_bwd{tag}", tm=tm, lq=lq, lkv=lkv)
    wc = w_in.shape[1]
    dw_in = _mm_tn(hn, dproj, name=f"mla_dwin{tag}", out_dtype=BF16, tm=_pick(d, _DIVS[2:]), tn=wc, tk=t_all)
    dhn = _mm_nt(dproj, w_in, name=f"mla_dhn{tag}", out_dtype=F32, tm=tmb, tn=tn_d, tk=wc)
    dh, dh_b, dg = _rmsnorm_bwd(dhn, h_in, g_mix, dh, name=f"norm_mix_bwd{tag}", tm=tm)
    return dh, dh_b, dg, dqn, dkvn, [dw_in.reshape(N_DEV, -1, wc), dw_uq, dw_ukv, dw_o.reshape(N_DEV, -1, d)]


def _lru_layer_fwd(tag, h, g_mix, ws, small, *, tm):
    w_lin, w_lo = ws[0], _rows_natural(ws[1])
    t_all, d = h.shape
    dr = w_lo.shape[0]
    tmb = _pick(t_all, _ROW_TILES)
    hn = _rmsnorm_fwd(h, g_mix, name=f"norm_mix{tag}", tm=tm)
    xy = _mm_nn(hn, w_lin, name=f"lru_in{tag}", out_dtype=F32, tm=tmb, tn=w_lin.shape[2], tk=_pick(d, _DIVS), b_blocked=True)
    hs, hsy = _lru_fwd(xy, *small, name=f"lru_fwd{tag}")
    h_mid = _mm_nn(hsy, w_lo, name=f"lru_o{tag}", out_dtype=F32, tm=tm, tn=d, tk=dr, res=h)
    return h_mid, (hn, xy, hs, hsy)


def _lru_layer_bwd(tag, dh, dh_b, h_in, saved, g_mix, ws, small, *, tm):
    hn, xy, hs, hsy = saved
    w_lin, w_lo = ws[0], _rows_natural(ws[1])
    t_all, d = h_in.shape
    dr = w_lo.shape[0]
    tmb = _pick(t_all, _ROW_TILES)
    tn_d, tk_d = _pick(d, _DIVS[1:]), _pick(d, _DIVS)
    dhsy = _mm_nt(dh_b, w_lo, name=f"lru_dhsy{tag}", out_dtype=F32, tm=tmb, tn=_pick(dr, _DIVS[1:]), tk=tk_d)
    dw_lo = _mm_tn(hsy, dh_b, name=f"lru_dwo{tag}", out_dtype=BF16, tm=_pick(dr, _DIVS[2:]), tn=tn_d, tk=t_all)
    dxy, *dsmall = _lru_bwd(xy, hs, dhsy, *small, name=f"lru_bwd{tag}")
    lw = w_lin.shape[2]
    dw_lin = _mm_tn(hn, dxy, name=f"lru_dwin{tag}", out_dtype=BF16, tm=tn_d, tn=lw, tk=t_all, out_block=lw)
    dhn = _mm_nt(dxy, w_lin, name=f"lru_dhn{tag}", out_dtype=F32, tm=tm, tn=tn_d, tk=2 * dr, b_blocked=True)
    dh, dh_b, dg = _rmsnorm_bwd(dhn, h_in, g_mix, dh, name=f"norm_mix_bwd{tag}", tm=tm)
    return dh, dh_b, dg, tuple(dsmall), [dw_lin, dw_lo.reshape(N_DEV, -1, d)]


def _ffn_layer_fwd(tag, h_mid, g_ffn, ws, *, tm):
    w_gu, w_down = ws[0], _rows_natural(ws[1])
    t_all, d = h_mid.shape
    f_all = w_down.shape[0]
    tmb = _pick(t_all, _ROW_TILES)
    fk = _pick(f_all, (1408,) + _DIVS[1:])
    hn2 = _rmsnorm_fwd(h_mid, g_ffn, name=f"norm_ffn{tag}", tm=tm)
    gu, act = _ffn_up(hn2, w_gu, name=f"ffn_up{tag}", tm=tm)
    h_out = _mm_nn(act, w_down, name=f"ffn_down{tag}", out_dtype=F32, tm=tm, tn=_pick(d, _DIVS[1:]), tk=f_all, res=h_mid)
    return h_out, (hn2, gu, act)


def _ffn_layer_bwd(tag, dh, dh_b, h_mid, saved, g_ffn, ws, *, tm):
    hn2, gu, act = saved
    w_gu, w_down = ws[0], _rows_natural(ws[1])
    t_all, d = h_mid.shape
    f_all = w_down.shape[0]
    f_local = w_gu.shape[2]
    tmb = _pick(t_all, _ROW_TILES)
    fk = _pick(f_all, (1408,) + _DIVS[1:])
    tn_d, tk_d = _pick(d, _DIVS[1:]), _pick(d, _DIVS)
    dgu = _ffn_dact(dh_b, w_down, gu, name=f"ffn_dact{tag}", tm=tm, tn=f_local)
    dw_down = _mm_tn(act, dh_b, name=f"ffn_dwdown{tag}", out_dtype=BF16, tm=fk, tn=_pick(d, _DIVS[2:]), tk=t_all)
    dhn2 = _mm_nt(dgu, w_gu, name=f"ffn_dhn{tag}", out_dtype=F32, tm=tm, tn=_pick(d, _DIVS[2:]), tk=2 * f_all, b_blocked=True)
    dw_gu = _mm_tn(hn2, dgu, name=f"ffn_dwgu{tag}", out_dtype=BF16, tm=_pick(d, _DIVS[2:]), tn=f_local, tk=t_all, out_block=f_local)
    dh, dh_b, dg = _rmsnorm_bwd(dhn2, h_mid, g_ffn, dh, name=f"norm_ffn_bwd{tag}", tm=tm)
    return dh, dh_b, dg, [dw_gu, dw_down.reshape(N_DEV, -1, d)]


def kernel(x, meta_tokens, norm_mix, norm_ffn, norm_final, mla_w_in, mla_q_norm, mla_kv_norm, mla_w_uq, mla_w_ukv, mla_w_o, lru_w_in, lru_conv_w, lru_conv_b, lru_w_gate_a, lru_b_gate_a, lru_w_gate_x, lru_b_gate_x, lru_lambda, lru_w_o, ffn_w_gu, ffn_w_down, loss_target, m_meta_tokens, m_norm_mix, m_norm_ffn, m_norm_final, m_mla_w_in, m_mla_q_norm, m_mla_kv_norm, m_mla_w_uq, m_mla_w_ukv, m_mla_w_o, m_lru_w_in, m_lru_conv_w, m_lru_conv_b, m_lru_w_gate_a, m_lru_b_gate_a, m_lru_w_gate_x, m_lru_b_gate_x, m_lru_lambda, m_lru_w_o, m_ffn_w_gu, m_ffn_w_down, v_meta_tokens, v_norm_mix, v_norm_ffn, v_norm_final, v_mla_w_in, v_mla_q_norm, v_mla_kv_norm, v_mla_w_uq, v_mla_w_ukv, v_mla_w_o, v_lru_w_in, v_lru_conv_w, v_lru_conv_b, v_lru_w_gate_a, v_lru_b_gate_a, v_lru_w_gate_x, v_lru_b_gate_x, v_lru_lambda, v_lru_w_o, v_ffn_w_gu, v_ffn_w_down):
    seq, d = x.shape[1], x.shape[2]
    assert seq % CHUNK == 0
    n_real = N_META + seq
    t_all = -(-n_real // LANES) * LANES
    tm = _pick(t_all, (384, 256, 128))
    tq = _pick(seq, (512, 256, 128, 64))
    depth = norm_mix.shape[0]
    n_mla, n_lru = mla_w_in.shape[0], lru_w_in.shape[0]
    lq, lkv = mla_q_norm.shape[1], mla_kv_norm.shape[1]
    w_in_cols = lq + lkv + LANES
    heads_local = mla_w_uq.shape[2] // (QK_NOPE + QK_ROPE)
    n_heads = heads_local * N_DEV
    dr = lru_w_gate_a.shape[1] * lru_w_gate_a.shape[2]
    scale = (QK_NOPE + QK_ROPE) ** -0.5
    cx, cy, cc = _mesh_pos()
    core = jnp.reshape(cc, (1,)).astype(jnp.int32)
    my_slot = jnp.reshape(4 * cx + 2 * cy + cc, (1,)).astype(jnp.int32)

    def pad_cols(w, cols):
        return jnp.pad(w, ((0, 0), (0, cols - w.shape[1])))

    def pad_heads(w):
        k_all = w.shape[0]
        w3 = w.reshape(k_all, heads_local, QK_NOPE + QK_ROPE)
        return jnp.pad(w3, ((0, 0), (0, 0), (0, HEAD_W - QK_NOPE - QK_ROPE))).reshape(k_all, heads_local * HEAD_W)

    def unpad_heads(w):
        k_all = w.shape[0]
        return w.reshape(k_all, heads_local, HEAD_W)[:, :, :QK_NOPE + QK_ROPE].reshape(k_all, -1)

    small_rows = N_META + n_lru * 4 + 2 * n_lru
    small_pad = -(-small_rows // SUBLANES) * SUBLANES

    def pack_small(meta, conv_w, conv_b, lam):
        rows = jnp.concatenate([meta, conv_w.reshape(n_lru * 4, -1), conv_b, lam], axis=0)
        return jnp.pad(rows, ((0, small_pad - small_rows), (0, 0)))

    def unpack_small(p):
        o1 = N_META + n_lru * 4
        return (p[:N_META], p[N_META:o1].reshape(n_lru, 4, -1), p[o1:o1 + n_lru], p[o1 + n_lru:o1 + 2 * n_lru])

    (small_full,), small_done = _all_gather([pack_small(meta_tokens, lru_conv_w, lru_conv_b, lru_lambda)], name="ag_small")
    small_full = jnp.transpose(small_full, (1, 0, 2)).reshape(small_pad, -1)
    meta_full, conv_w_full, conv_b_full, lam_full = unpack_small(small_full)

    def wire(w):
        return (w + small_done).astype(BF16)

    mla_shards, lru_shards, ffn_shards = [], [], []
    for j in range(n_mla):
        mla_shards.append([wire(pad_cols(mla_w_in[j], w_in_cols)), wire(pad_heads(mla_w_uq[j])), wire(mla_w_ukv[j]),
                           wire(mla_w_o[j])])
    for j in range(n_lru):
        lru_shards.append([wire(lru_w_in[j]), wire(lru_w_o[j])])
    for layer in range(depth):
        ffn_shards.append([wire(ffn_w_gu[layer]), wire(ffn_w_down[layer])])

    n_sub = 2 * depth
    groups = []
    for layer in range(depth):
        groups += [mla_shards[layer // 2] if layer % 2 == 0 else lru_shards[layer // 2], ffn_shards[layer]]
    slot_idx = 4 * cx + 2 * cy + cc
    ag_own = []
    for gi, shards in enumerate(groups):
        lands = [lax.dynamic_update_slice(lax.empty((N_DEV,) + s.shape, s.dtype), s[None], (slot_idx, 0, 0)) for s in shards]
        ag_own.append(_exchange_start(shards, lands, _ag_plan_own, 4, name=f"ag{gi}_start"))
    ag_pass = [None] * n_sub
    weights = [None] * n_sub

    def ag_landed(gi, after):
        _, lands = _exchange_wait(ag_own[gi], _ag_plan_own, after, name=f"ag{gi}_wait")
        ag_pass[gi] = _exchange_start([], lands, _ag_plan_pass, 3, name=f"ag{gi}_pass")
        return ag_pass[gi][4][0, 0]

    def ag_done(gi, after):
        _, weights[gi] = _exchange_wait(ag_pass[gi], _ag_plan_pass, after, name=f"ag{gi}_pass_wait")

    cos, sin = _rope_tables(t_all)
    zeros_tail = jnp.zeros((t_all - n_real, d), F32)
    started = ag_own[0][4][0, 0]
    for st in ag_own[1:]:
        started = started + st[4][0, 0]
    h = jnp.concatenate([meta_full + started, x[0], zeros_tail], axis=0)
    target = jnp.concatenate([jnp.zeros((N_META, d), F32), loss_target[0], zeros_tail], axis=0)

    attn_kw = dict(tm=tm, tq=tq, n_heads=n_heads, scale=scale, n_real=n_real)

    def lru_small(j):
        return (conv_w_full[j], conv_b_full[j][None, :], lru_w_gate_a[j].astype(BF16), lru_b_gate_a[j].reshape(1, dr),
                lru_w_gate_x[j].astype(BF16), lru_b_gate_x[j].reshape(1, dr), lam_full[j][None, :])

    def before_sublayer(k, act):
        tok = ag_landed(k, act) if k <= 1 else 0.0
        ag_done(k, act)
        if 1 <= k < n_sub - 1:
            tok = tok + ag_landed(k + 1, act)
        return tok

    saved = []
    for layer in range(depth):
        j = layer // 2
        g_mix = norm_mix[layer][None, :] + before_sublayer(2 * layer, h)
        if layer % 2 == 0:
            h_mid, mix_saved = _mla_layer_fwd(layer, h, g_mix, weights[2 * layer], mla_q_norm[j][None, :],
                                              mla_kv_norm[j][None, :], cos, sin, **attn_kw)
        else:
            h_mid, mix_saved = _lru_layer_fwd(layer, h, g_mix, weights[2 * layer], lru_small(j), tm=tm)
        g_ffn = norm_ffn[layer][None, :] + before_sublayer(2 * layer + 1, h_mid)
        h_out, ffn_saved = _ffn_layer_fwd(layer, h_mid, g_ffn, weights[2 * layer + 1], tm=tm)
        saved.append((h, h_mid, mix_saved, ffn_saved))
        h = h_out

    loss_part, dh, dh_b, dg_final = _loss_head(h, target, norm_final[None, :], name="loss_head", tm=tm, n_real=n_real)
    loss = lax.psum(loss_part[0, 0], ("x", "y", "c"))

    rs_sib, rs_chip, reduced = [None] * n_sub, [None] * n_sub, [None] * n_sub
    chip_idx = jnp.reshape(2 * cx + cy, (1,)).astype(jnp.int32)

    def rs_begin(k, grads):
        lands = [lax.empty((4,) + g.shape[1:], g.dtype) for g in grads]
        rs_sib[k] = _exchange_start(grads, lands, _rs_plan_sibling, 4, name=f"rs{k}_start")
        return rs_sib[k][4][0, 0]

    def rs_middle(k, after):
        grads, landed = _exchange_wait(rs_sib[k], _rs_plan_sibling, after, name=f"rs{k}_wait")
        parts = [_pair_add(g, l, core, name=f"rs{k}_add{a}", tr=_adam_row_tile(g.shape[1], g.shape[2], 4 * 1024 * 1024))
                 for a, (g, l) in enumerate(zip(grads, landed))]
        lands = [lax.empty((3,) + p.shape[1:], p.dtype) for p in parts]
        rs_chip[k] = _exchange_start(parts, lands, _rs_plan_chips, 3, name=f"rs{k}_chips")
        return rs_chip[k][4][0, 0]

    def rs_end(k, after):
        reduced[k] = _exchange_wait(rs_chip[k], _rs_plan_chips, after, name=f"rs{k}_chips_wait")

    d_norm_mix, d_norm_ffn = [None] * depth, [None] * depth
    d_qn, d_kvn = [None] * n_mla, [None] * n_mla
    d_small = {k: [None] * n_lru for k in ("cw", "cb", "wga", "bga", "wgx", "bgx", "lam")}
    tok, waiting = 0.0, None
    for layer in reversed(range(depth)):
        j = layer // 2
        h_in, h_mid, mix_saved, ffn_saved = saved[layer]
        dh, dh_b, d_norm_ffn[layer], ffn_g = _ffn_layer_bwd(layer, dh, dh_b, h_mid, ffn_saved, norm_ffn[layer][None, :] + tok,
                                                            weights[2 * layer + 1], tm=tm)
        tok = rs_begin(2 * layer + 1, ffn_g)
        if waiting is not None:
            tok = tok + rs_middle(waiting, dh)
        waiting = 2 * layer + 1
        if layer == 0:
            tok = tok + rs_middle(waiting, dh)
            waiting = None
        g_mix = norm_mix[layer][None, :] + tok
        if layer % 2 == 0:
            dh, dh_b, d_norm_mix[layer], d_qn[j], d_kvn[j], mix_g = _mla_layer_bwd(
                layer, dh, dh_b, h_in, mix_saved, g_mix, weights[2 * layer], mla_q_norm[j][None, :], mla_kv_norm[j][None, :],
                cos, sin, **attn_kw)
        else:
            dh, dh_b, d_norm_mix[layer], dsmall, mix_g = _lru_layer_bwd(layer, dh, dh_b, h_in, mix_saved, g_mix,
                                                                        weights[2 * layer], lru_small(j), tm=tm)
            for key, val in zip(("cw", "cb", "wga", "bga", "wgx", "bgx", "lam"), dsmall):
                d_small[key][j] = val
        tok = rs_begin(2 * layer, mix_g)
        if waiting is not None:
            tok = tok + rs_middle(waiting, dh)
        waiting = 2 * layer
    rs_middle(waiting, dh)

    grad_x = dh[N_META:n_real][None]

    d_meta = dh[:N_META]
    small_grad = pack_small(d_meta, jnp.stack(d_small["cw"], axis=0), jnp.concatenate(d_small["cb"], axis=0),
                            jnp.concatenate(d_small["lam"], axis=0))
    rep_grads = [
        jnp.concatenate(d_norm_mix, axis=0), jnp.concatenate(d_norm_ffn, axis=0), dg_final,
        jnp.concatenate(d_qn, axis=0), jnp.concatenate(d_kvn, axis=0),
        jnp.stack(d_small["wga"], axis=0).reshape(-1, LANES), jnp.concatenate(d_small["bga"], axis=0),
        jnp.stack(d_small["wgx"], axis=0).reshape(-1, LANES), jnp.concatenate(d_small["bgx"], axis=0),
    ]
    small_srcs = [small_grad] + [jnp.pad(g, ((0, -g.shape[0] % SUBLANES), (0, 0))) for g in rep_grads]
    small_lands = [lax.dynamic_update_slice(lax.empty((N_DEV,) + s.shape, s.dtype), s[None], (slot_idx, 0, 0))
                   for s in small_srcs]
    small_own = _exchange_start(small_srcs, small_lands, _ag_plan_own, 4, name="ag_grads_start")

    res = {}

    def adam_sharded(nm, k, a, idx, n_layers, w, m, v):
        parts, landed = reduced[k]
        r_all, c_all = landed[a].shape[1], landed[a].shape[2]
        res[nm] = _adamw(w.reshape(r_all, c_all), m.reshape(r_all, c_all), v.reshape(r_all, c_all), landed[a], (0, 1, 2),
                         name=f"adamw_{nm}{idx}", tr=_adam_row_tile(r_all, c_all, 2 * 1024 * 1024), own=(parts[a], chip_idx),
                         stack=(idx, n_layers, res.get(nm)))

    after = small_own[4]
    for k in reversed(range(n_sub)):
        rs_end(k, after)
        layer, j = k // 2, k // 4
        if k % 2 == 1:
            adam_sharded("ffn_w_gu", k, 0, layer, depth, ffn_w_gu[layer], m_ffn_w_gu[layer], v_ffn_w_gu[layer])
            adam_sharded("ffn_w_down", k, 1, layer, depth, ffn_w_down[layer], m_ffn_w_down[layer], v_ffn_w_down[layer])
            after = res["ffn_w_down"][0]
        elif layer % 2 == 0:
            adam_sharded("mla_w_in", k, 0, j, n_mla, pad_cols(mla_w_in[j], w_in_cols), pad_cols(m_mla_w_in[j], w_in_cols),
                         pad_cols(v_mla_w_in[j], w_in_cols))
            adam_sharded("mla_w_uq", k, 1, j, n_mla, pad_heads(mla_w_uq[j]), pad_heads(m_mla_w_uq[j]), pad_heads(v_mla_w_uq[j]))
            adam_sharded("mla_w_ukv", k, 2, j, n_mla, mla_w_ukv[j], m_mla_w_ukv[j], v_mla_w_ukv[j])
            adam_sharded("mla_w_o", k, 3, j, n_mla, mla_w_o[j], m_mla_w_o[j], v_mla_w_o[j])
            after = res["mla_w_o"][0]
        else:
            adam_sharded("lru_w_in", k, 0, j, n_lru, lru_w_in[j], m_lru_w_in[j], v_lru_w_in[j])
            adam_sharded("lru_w_o", k, 1, j, n_lru, lru_w_o[j], m_lru_w_o[j], v_lru_w_o[j])
            after = res["lru_w_o"][0]
    res["mla_w_in"] = [t[:, :, :lq + lkv + QK_ROPE] for t in res["mla_w_in"]]
    res["mla_w_uq"] = [t.reshape(n_mla, lq, heads_local, HEAD_W)[:, :, :, :QK_NOPE + QK_ROPE].reshape(n_mla, lq, -1)
                       for t in res["mla_w_uq"]]

    _, small_lands = _exchange_wait(small_own, _ag_plan_own, after, name="ag_grads_wait")
    small_pass = _exchange_start([], small_lands, _ag_plan_pass, 3, name="ag_grads_pass")
    _, all_small = _exchange_wait(small_pass, _ag_plan_pass, after, name="ag_grads_pass_wait")
    slot_order = tuple(range(N_DEV))

    def adam_rep(terms, w, m, v, tag):
        r_pad, c_all = terms.shape[1], terms.shape[2]

        def prep(t):
            t2 = t.reshape(-1, c_all)
            return jnp.pad(t2, ((0, r_pad - t2.shape[0]), (0, 0)))

        outs = _adamw(prep(w), prep(m), prep(v), terms, slot_order, name=f"adamw_{tag}", tr=_adam_row_tile(r_pad, c_all))
        n_rows = w.size // c_all
        return [o[:n_rows].reshape(w.shape) for o in outs]

    small_w = pack_small(meta_tokens, lru_conv_w, lru_conv_b, lru_lambda)
    small_m = pack_small(m_meta_tokens, m_lru_conv_w, m_lru_conv_b, m_lru_lambda)
    small_v = pack_small(v_meta_tokens, v_lru_conv_w, v_lru_conv_b, v_lru_lambda)
    small_out = _adamw(small_w, small_m, small_v, all_small[0], slot_order, name="adamw_small", tr=small_pad, col_block=my_slot)
    small_out = [unpack_small(o) for o in small_out]
    for idx, key in enumerate(("meta_tokens", "lru_conv_w", "lru_conv_b", "lru_lambda")):
        res[key] = [small_out[k][idx] for k in range(4)]

    res["norm_mix"] = adam_rep(all_small[1], norm_mix, m_norm_mix, v_norm_mix, "norm_mix")
    res["norm_ffn"] = adam_rep(all_small[2], norm_ffn, m_norm_ffn, v_norm_ffn, "norm_ffn")
    res["norm_final"] = adam_rep(all_small[3], norm_final, m_norm_final, v_norm_final, "norm_final")
    res["mla_q_norm"] = adam_rep(all_small[4], mla_q_norm, m_mla_q_norm, v_mla_q_norm, "mla_q_norm")
    res["mla_kv_norm"] = adam_rep(all_small[5], mla_kv_norm, m_mla_kv_norm, v_mla_kv_norm, "mla_kv_norm")
    res["lru_w_gate_a"] = adam_rep(all_small[6], lru_w_gate_a, m_lru_w_gate_a, v_lru_w_gate_a, "lru_w_gate_a")
    res["lru_b_gate_a"] = adam_rep(all_small[7], lru_b_gate_a, m_lru_b_gate_a, v_lru_b_gate_a, "lru_b_gate_a")
    res["lru_w_gate_x"] = adam_rep(all_small[8], lru_w_gate_x, m_lru_w_gate_x, v_lru_w_gate_x, "lru_w_gate_x")
    res["lru_b_gate_x"] = adam_rep(all_small[9], lru_b_gate_x, m_lru_b_gate_x, v_lru_b_gate_x, "lru_b_gate_x")

    names = ["meta_tokens", "norm_mix", "norm_ffn", "norm_final", "mla_w_in", "mla_q_norm", "mla_kv_norm", "mla_w_uq",
             "mla_w_ukv", "mla_w_o", "lru_w_in", "lru_conv_w", "lru_conv_b", "lru_w_gate_a", "lru_b_gate_a", "lru_w_gate_x",
             "lru_b_gate_x", "lru_lambda", "lru_w_o", "ffn_w_gu", "ffn_w_down"]
    shapes = dict(meta_tokens=meta_tokens, norm_mix=norm_mix, norm_ffn=norm_ffn, norm_final=norm_final, mla_w_in=mla_w_in,
                  mla_q_norm=mla_q_norm, mla_kv_norm=mla_kv_norm, mla_w_uq=mla_w_uq, mla_w_ukv=mla_w_ukv, mla_w_o=mla_w_o,
                  lru_w_in=lru_w_in, lru_conv_w=lru_conv_w, lru_conv_b=lru_conv_b, lru_w_gate_a=lru_w_gate_a,
                  lru_b_gate_a=lru_b_gate_a, lru_w_gate_x=lru_w_gate_x, lru_b_gate_x=lru_b_gate_x, lru_lambda=lru_lambda,
                  lru_w_o=lru_w_o, ffn_w_gu=ffn_w_gu, ffn_w_down=ffn_w_down)
    outs = [loss, grad_x]
    for k in range(4):
        outs += [res[nm][k].reshape(shapes[nm].shape) for nm in names]
    return tuple(outs)
```

```python
import math

import jax
import jax.numpy as jnp
from jax import lax
from jax.experimental import pallas as pl
from jax.experimental.pallas import tpu as pltpu

F32 = jnp.float32
BF16 = jnp.bfloat16
MESH = pl.DeviceIdType.MESH

N_META = 16
CHUNK = 64
QK_NOPE = 128
QK_ROPE = 64
V_HEAD = 128
HEAD_W = 256
ROPE_THETA = 10000.0
LRU_C = 8.0
RMS_EPS = 1e-6
NEG_BIG = -1e30
ADAM_LR, ADAM_B1, ADAM_B2, ADAM_EPS, ADAM_WD, ADAM_STEP = 0.001, 0.9, 0.999, 1e-08, 0.01, 10

LANES = 128
SUBLANES = 8
VMEM_LIMIT_BYTES = 52 * 1024 * 1024
N_DEV = 8

_NT = (((1,), (1,)), ((), ()))
_TN = (((0,), (0,)), ((), ()))
_DIVS = (2048, 1024, 512, 256, 128)
_ROW_TILES = (1408, 1024, 512, 256, 128)


def _params(dims):
    return pltpu.CompilerParams(dimension_semantics=dims, vmem_limit_bytes=VMEM_LIMIT_BYTES)


def _pick(n, candidates):
    for c in candidates:
        if c <= n and n % c == 0:
            return c
    return n


def _sigmoid(z):
    return 1.0 / (1.0 + jnp.exp(-z))


def _gelu(x):
    c = math.sqrt(2.0 / math.pi)
    return 0.5 * x * (1.0 + jnp.tanh(c * (x + 0.044715 * x * x * x)))


def _gelu_grad(x):
    c = math.sqrt(2.0 / math.pi)
    th = jnp.tanh(c * (x + 0.044715 * x * x * x))
    return 0.5 * (1.0 + th) + 0.5 * x * (1.0 - th * th) * c * (1.0 + 3.0 * 0.044715 * x * x)


def _neg_expm1(x):
    poly = -x * (1.0 + x * (1.0 / 2.0) * (1.0 + x * (1.0 / 3.0) * (1.0 + x * (1.0 / 4.0) * (
        1.0 + x * (1.0 / 5.0) * (1.0 + x * (1.0 / 6.0) * (1.0 + x * (1.0 / 7.0)))))))
    return jnp.where(x > -0.25, poly, 1.0 - jnp.exp(x))


def _softplus_neg(lam):
    e = jnp.exp(-jnp.abs(lam))
    log1p = jnp.where(e > 1e-4, jnp.log(1.0 + e), e * (1.0 - e * (0.5 - e * (1.0 / 3.0))))
    return jnp.maximum(-lam, 0.0) + log1p


def _rot_half(x):
    lane = lax.broadcasted_iota(jnp.int32, x.shape, 1)
    first = (lane % QK_ROPE) < (QK_ROPE // 2)
    return jnp.where(first, -pltpu.roll(x, LANES - QK_ROPE // 2, 1), pltpu.roll(x, QK_ROPE // 2, 1))


def _rope(x, cos, sin):
    return x * cos + _rot_half(x) * sin


def _unrope(g, cos, sin):
    return g * cos - _rot_half(g) * sin


def _grid_order(rows_outer):
    if not rows_outer:
        return lambda f: f
    return lambda f: (lambda i, j, k: f(j, i, k))


def _mm_nn(a, b, *, name, out_dtype, tm, tn, tk, b_blocked=False, res=None, epilogue=None, extras=(), rows_outer=False):
    m_all, k_all = a.shape
    om = _grid_order(rows_outer)
    if b_blocked:
        g_all, kb, nb = b.shape
        n_all = g_all * nb
        assert nb % tn == 0
        r = nb // tn
        b_spec = pl.BlockSpec((None, tk, tn), om(lambda j, i, k: (j // r, k, j % r)))
    else:
        kb, n_all = b.shape
        b_spec = pl.BlockSpec((tk, tn), om(lambda j, i, k: (k, j)))
    assert kb == k_all and m_all % tm == 0 and n_all % tn == 0 and k_all % tk == 0
    nm, nn, nk = m_all // tm, n_all // tn, k_all // tk
    in_specs = [pl.BlockSpec((tm, tk), om(lambda j, i, k: (i, k))), b_spec]
    operands = [a, b]
    has_res = res is not None
    if has_res:
        in_specs.append(pl.BlockSpec((tm, tn), om(lambda j, i, k: (i, j))))
        operands.append(res)
    for e in extras:
        in_specs.append(pl.BlockSpec((tm, e.shape[1]), om(lambda j, i, k: (i, 0))))
        operands.append(e)
    n_ex = len(extras)

    def body(*refs):
        a_ref, b_ref = refs[0], refs[1]
        pos = 2
        res_ref = None
        if has_res:
            res_ref = refs[pos]
            pos += 1
        ex_refs = refs[pos:pos + n_ex]
        pos += n_ex
        o_ref = refs[pos]
        acc_ref = refs[pos + 1] if nk > 1 else None

        def finish(acc):
            if has_res:
                acc = acc + res_ref[...]
            if epilogue is not None:
                acc = epilogue(acc, *ex_refs)
            o_ref[...] = acc.astype(o_ref.dtype)

        prod = jnp.dot(a_ref[...], b_ref[...], preferred_element_type=F32)
        if nk == 1:
            finish(prod)
        else:
            k = pl.program_id(2)

            @pl.when(k == 0)
            def _():
                acc_ref[...] = prod

            @pl.when(k > 0)
            def _():
                acc_ref[...] += prod

            @pl.when(k == nk - 1)
            def _():
                finish(acc_ref[...])

    return pl.pallas_call(
        body, name=name, grid=(nm, nn, nk) if rows_outer else (nn, nm, nk), in_specs=in_specs,
        out_specs=pl.BlockSpec((tm, tn), om(lambda j, i, k: (i, j))),
        out_shape=jax.ShapeDtypeStruct((m_all, n_all), out_dtype),
        scratch_shapes=[pltpu.VMEM((tm, tn), F32)] if nk > 1 else [],
        compiler_params=_params(("parallel", "parallel", "arbitrary")),
    )(*operands)


def _mm_nt(a, b, *, name, out_dtype, tm, tn, tk, b_blocked=False):
    if a.ndim == 3:
        n_planes, m_all, kp = a.shape
        k_all = n_planes * kp
    else:
        n_planes, (m_all, k_all) = 0, a.shape
    if b_blocked and tk == k_all and b.shape[0] > 1:
        g_all, n_all, nb = b.shape
        assert g_all * nb == k_all and m_all % tm == 0 and n_all % tn == 0
        per_plane = kp // nb if n_planes else 0

        def whole_body(a_ref, b_ref, o_ref):
            acc = None
            for g in range(g_all):
                a_g = a_ref[g // per_plane, :, (g % per_plane) * nb:(g % per_plane + 1) * nb] if n_planes else a_ref[:, g * nb:(g + 1) * nb]
                prod = lax.dot_general(a_g, b_ref[g], _NT, preferred_element_type=F32)
                acc = prod if acc is None else acc + prod
            o_ref[...] = acc.astype(o_ref.dtype)

        a_whole = (pl.BlockSpec((n_planes, tm, kp), lambda j, i: (0, i, 0)) if n_planes
                   else pl.BlockSpec((tm, k_all), lambda j, i: (i, 0)))
        return pl.pallas_call(
            whole_body, name=name, grid=(n_all // tn, m_all // tm),
            in_specs=[a_whole, pl.BlockSpec((g_all, tn, nb), lambda j, i: (0, j, 0))],
            out_specs=pl.BlockSpec((tm, tn), lambda j, i: (i, j)),
            out_shape=jax.ShapeDtypeStruct((m_all, n_all), out_dtype),
            compiler_params=_params(("parallel", "parallel")),
        )(a, b)
    if n_planes:
        assert kp % tk == 0
        rp = kp // tk
        a_spec = pl.BlockSpec((None, tm, tk), lambda j, i, k: (k // rp, i, k % rp))
    else:
        a_spec = pl.BlockSpec((tm, tk), lambda j, i, k: (i, k))
    if b_blocked:
        g_all, n_all, nb = b.shape
        assert g_all * nb == k_all and nb % tk == 0
        r = nb // tk
        b_spec = pl.BlockSpec((None, tn, tk), lambda j, i, k: (k // r, j, k % r))
    else:
        n_all, kb = b.shape
        assert kb == k_all
        b_spec = pl.BlockSpec((tn, tk), lambda j, i, k: (j, k))
    assert m_all % tm == 0 and n_all % tn == 0 and k_all % tk == 0
    nm, nn, nk = m_all // tm, n_all // tn, k_all // tk

    def body(a_ref, b_ref, o_ref, *scratch):
        prod = lax.dot_general(a_ref[...], b_ref[...], _NT, preferred_element_type=F32)
        if nk == 1:
            o_ref[...] = prod.astype(o_ref.dtype)
        else:
            acc_ref = scratch[0]
            k = pl.program_id(2)

            @pl.when(k == 0)
            def _():
                acc_ref[...] = prod

            @pl.when(k > 0)
            def _():
                acc_ref[...] += prod

            @pl.when(k == nk - 1)
            def _():
                o_ref[...] = acc_ref[...].astype(o_ref.dtype)

    return pl.pallas_call(
        body, name=name, grid=(nn, nm, nk),
        in_specs=[a_spec, b_spec],
        out_specs=pl.BlockSpec((tm, tn), lambda j, i, k: (i, j)),
        out_shape=jax.ShapeDtypeStruct((m_all, n_all), out_dtype),
        scratch_shapes=[pltpu.VMEM((tm, tn), F32)] if nk > 1 else [],
        compiler_params=_params(("parallel", "parallel", "arbitrary")),
    )(a, b)


def _mm_tn(a, b, *, name, out_dtype, tm, tn, tk, out_block=None, cols_outer=False):
    t_all, m_all = a.shape
    om = _grid_order(cols_outer)
    if b.ndim == 3:
        n_planes, tb, n_p = b.shape
        assert n_p % tn == 0
        rq = n_p // tn
        n_all = n_planes * n_p
        b_spec = pl.BlockSpec((None, tk, tn), om(lambda i, j, k: (j // rq, k, j % rq)))
    else:
        tb, n_all = b.shape
        b_spec = pl.BlockSpec((tk, tn), om(lambda i, j, k: (k, j)))
    assert tb == t_all and m_all % tm == 0 and n_all % tn == 0 and t_all % tk == 0
    nm, nn, nk = m_all // tm, n_all // tn, t_all // tk
    if out_block is None:
        out_shape = jax.ShapeDtypeStruct((m_all, n_all), out_dtype)
        out_spec = pl.BlockSpec((tm, tn), om(lambda i, j, k: (i, j)))
    else:
        assert out_block % tn == 0 and n_all % out_block == 0
        r = out_block // tn
        out_shape = jax.ShapeDtypeStruct((n_all // out_block, m_all, out_block), out_dtype)
        out_spec = pl.BlockSpec((None, tm, tn), om(lambda i, j, k: (j // r, i, j % r)))

    def body(a_ref, b_ref, o_ref, *scratch):
        prod = lax.dot_general(a_ref[...], b_ref[...], _TN, preferred_element_type=F32)
        if nk == 1:
            o_ref[...] = prod.astype(o_ref.dtype)
        else:
            acc_ref = scratch[0]
            k = pl.program_id(2)

            @pl.when(k == 0)
            def _():
                acc_ref[...] = prod

            @pl.when(k > 0)
            def _():
                acc_ref[...] += prod

            @pl.when(k == nk - 1)
            def _():
                o_ref[...] = acc_ref[...].astype(o_ref.dtype)

    return pl.pallas_call(
        body, name=name, grid=(nn, nm, nk) if cols_outer else (nm, nn, nk),
        in_specs=[pl.BlockSpec((tk, tm), om(lambda i, j, k: (k, i))), b_spec],
        out_specs=out_spec, out_shape=out_shape,
        scratch_shapes=[pltpu.VMEM((tm, tn), F32)] if nk > 1 else [],
        compiler_params=_params(("parallel", "parallel", "arbitrary")),
    )(a, b)


def _rmsnorm_fwd(x, g, *, name, tm):
    t_all, d = x.shape

    def body(x_ref, g_ref, o_ref):
        xv = x_ref[...]
        rstd = lax.rsqrt(jnp.mean(xv * xv, axis=-1, keepdims=True) + RMS_EPS)
        o_ref[...] = (xv * rstd * g_ref[...]).astype(o_ref.dtype)

    return pl.pallas_call(
        body, name=name, grid=(t_all // tm,),
        in_specs=[pl.BlockSpec((tm, d), lambda i: (i, 0)), pl.BlockSpec((1, d), lambda i: (0, 0))],
        out_specs=pl.BlockSpec((tm, d), lambda i: (i, 0)),
        out_shape=jax.ShapeDtypeStruct((t_all, d), BF16),
        compiler_params=_params(("parallel",)),
    )(x, g)


def _rms_bwd_math(dy, xv, g):
    rstd = lax.rsqrt(jnp.mean(xv * xv, axis=-1, keepdims=True) + RMS_EPS)
    xhat = xv * rstd
    dxh = dy * g
    dx = rstd * (dxh - xhat * jnp.mean(dxh * xhat, axis=-1, keepdims=True))
    return dx, jnp.sum(dy * xhat, axis=0, keepdims=True)


def _rmsnorm_bwd(dy, x, g, res, *, name, tm):
    t_all, d = x.shape

    def body(dy_ref, x_ref, g_ref, res_ref, dx_ref, dxb_ref, dg_ref):
        dx, dg = _rms_bwd_math(dy_ref[...], x_ref[...], g_ref[...])
        tot = res_ref[...] + dx
        dx_ref[...] = tot
        dxb_ref[...] = tot.astype(BF16)

        @pl.when(pl.program_id(0) == 0)
        def _():
            dg_ref[...] = dg

        @pl.when(pl.program_id(0) > 0)
        def _():
            dg_ref[...] += dg

    row = pl.BlockSpec((tm, d), lambda i: (i, 0))
    vec = pl.BlockSpec((1, d), lambda i: (0, 0))
    return pl.pallas_call(
        body, name=name, grid=(t_all // tm,),
        in_specs=[row, row, vec, row], out_specs=[row, row, vec],
        out_shape=[jax.ShapeDtypeStruct((t_all, d), F32), jax.ShapeDtypeStruct((t_all, d), BF16),
                   jax.ShapeDtypeStruct((1, d), F32)],
        compiler_params=_params(("arbitrary",)),
    )(dy, x, g, res)


def _loss_head(h, target, g, *, name, tm, n_real):
    t_all, d = h.shape

    def body(h_ref, t_ref, g_ref, loss_ref, dx_ref, dxb_ref, dg_ref):
        i = pl.program_id(0)
        xv = h_ref[...]
        gv = g_ref[...]
        rstd = lax.rsqrt(jnp.mean(xv * xv, axis=-1, keepdims=True) + RMS_EPS)
        y = xv * rstd * gv
        row = i * tm + lax.broadcasted_iota(jnp.int32, (tm, 1), 0)
        valid = (row >= N_META) & (row < n_real)
        err = jnp.where(valid, y - t_ref[...], 0.0)
        part = 0.5 * jnp.sum(jnp.mean(err * err, axis=-1, keepdims=True), axis=0, keepdims=True)
        dx, dg = _rms_bwd_math(err * (1.0 / d), xv, gv)
        dx_ref[...] = dx
        dxb_ref[...] = dx.astype(BF16)

        @pl.when(i == 0)
        def _():
            dg_ref[...] = dg
            loss_ref[...] = jnp.broadcast_to(part, loss_ref.shape)

        @pl.when(i > 0)
        def _():
            dg_ref[...] += dg
            loss_ref[...] += jnp.broadcast_to(part, loss_ref.shape)

    row = pl.BlockSpec((tm, d), lambda i: (i, 0))
    vec = pl.BlockSpec((1, d), lambda i: (0, 0))
    return pl.pallas_call(
        body, name=name, grid=(t_all // tm,),
        in_specs=[row, row, vec],
        out_specs=[pl.BlockSpec((1, LANES), lambda i: (0, 0)), row, row, vec],
        out_shape=[jax.ShapeDtypeStruct((1, LANES), F32), jax.ShapeDtypeStruct((t_all, d), F32),
                   jax.ShapeDtypeStruct((t_all, d), BF16), jax.ShapeDtypeStruct((1, d), F32)],
        compiler_params=_params(("arbitrary",)),
    )(h, target, g)


def _ffn_up(x, w_gu, *, name, tm):
    t_all, d = x.shape
    g_all, kb, nb = w_gu.shape
    half = g_all // 2
    f = half * nb
    assert kb == d and t_all % tm == 0

    def body(x_ref, wg_ref, wu_ref, gu_ref, act_ref):
        xv = x_ref[...]
        gv = jnp.dot(xv, wg_ref[...], preferred_element_type=F32)
        uv = jnp.dot(xv, wu_ref[...], preferred_element_type=F32)
        gu_ref[0] = gv
        gu_ref[1] = uv
        act_ref[...] = (gv * _sigmoid(gv) * uv).astype(act_ref.dtype)

    return pl.pallas_call(
        body, name=name, grid=(half, t_all // tm),
        in_specs=[pl.BlockSpec((tm, d), lambda j, i: (i, 0)), pl.BlockSpec((None, d, nb), lambda j, i: (j, 0, 0)),
                  pl.BlockSpec((None, d, nb), lambda j, i: (j + half, 0, 0))],
        out_specs=[pl.BlockSpec((2, tm, nb), lambda j, i: (0, i, j)), pl.BlockSpec((tm, nb), lambda j, i: (i, j))],
        out_shape=[jax.ShapeDtypeStruct((2, t_all, f), F32), jax.ShapeDtypeStruct((t_all, f), BF16)],
        compiler_params=_params(("parallel", "parallel")),
    )(x, w_gu, w_gu)


def _ffn_dact(dy, w_down, gu, *, name, tm, tn):
    t_all, d = dy.shape
    f = w_down.shape[0]
    assert t_all % tm == 0 and f % tn == 0

    def body(dy_ref, w_ref, gu_ref, o_ref):
        dact = lax.dot_general(dy_ref[...], w_ref[...], _NT, preferred_element_type=F32)
        gv, uv = gu_ref[0], gu_ref[1]
        sg = _sigmoid(gv)
        o_ref[0] = (dact * uv * (sg * (1.0 + gv * (1.0 - sg)))).astype(o_ref.dtype)
        o_ref[1] = (dact * gv * sg).astype(o_ref.dtype)

    return pl.pallas_call(
        body, name=name, grid=(f // tn, t_all // tm),
        in_specs=[pl.BlockSpec((tm, d), lambda j, i: (i, 0)), pl.BlockSpec((tn, d), lambda j, i: (j, 0)),
                  pl.BlockSpec((2, tm, tn), lambda j, i: (0, i, j))],
        out_specs=pl.BlockSpec((2, tm, tn), lambda j, i: (0, i, j)),
        out_shape=jax.ShapeDtypeStruct((2, t_all, f), BF16),
        compiler_params=_params(("parallel", "parallel")),
    )(dy, w_down, gu)


def _mla_prep_fwd(proj, qn, kvn, cos, sin, *, name, tm, lq, lkv):
    t_all, w = proj.shape

    def body(p_ref, qn_ref, kvn_ref, cos_ref, sin_ref, cq_ref, ckv_ref, kr_ref):
        pv = p_ref[...]
        xq = pv[:, :lq]
        xkv = pv[:, lq:lq + lkv]
        cq_ref[...] = (xq * lax.rsqrt(jnp.mean(xq * xq, axis=-1, keepdims=True) + RMS_EPS) * qn_ref[...]).astype(BF16)
        ckv_ref[...] = (xkv * lax.rsqrt(jnp.mean(xkv * xkv, axis=-1, keepdims=True) + RMS_EPS) * kvn_ref[...]).astype(BF16)
        kr_ref[...] = _rope(pv[:, lq + lkv:], cos_ref[...], sin_ref[...]).astype(BF16)

    def row(width):
        return pl.BlockSpec((tm, width), lambda i: (i, 0))

    def vec(width):
        return pl.BlockSpec((1, width), lambda i: (0, 0))

    return pl.pallas_call(
        body, name=name, grid=(t_all // tm,),
        in_specs=[row(w), vec(lq), vec(lkv), row(LANES), row(LANES)],
        out_specs=[row(lq), row(lkv), row(LANES)],
        out_shape=[jax.ShapeDtypeStruct((t_all, lq), BF16), jax.ShapeDtypeStruct((t_all, lkv), BF16),
                   jax.ShapeDtypeStruct((t_all, LANES), BF16)],
        compiler_params=_params(("parallel",)),
    )(proj, qn, kvn, cos, sin)


def _mla_prep_bwd(dcq, dckv, dkr_h, proj, qn, kvn, cos, sin, *, name, tm, lq, lkv):
    t_all, w = proj.shape
    n_heads = dkr_h.shape[0]

    def body(dcq_ref, dckv_ref, dkr_ref, p_ref, qn_ref, kvn_ref, cos_ref, sin_ref, dp_ref, dqn_ref, dkvn_ref):
        pv = p_ref[...]
        dxq, dqn = _rms_bwd_math(dcq_ref[...], pv[:, :lq], qn_ref[...])
        dxkv, dkvn = _rms_bwd_math(dckv_ref[...], pv[:, lq:lq + lkv], kvn_ref[...])
        dkr = dkr_ref[0]
        for hh in range(1, n_heads):
            dkr = dkr + dkr_ref[hh]
        dkr = _unrope(dkr, cos_ref[...], sin_ref[...])
        dp_ref[...] = jnp.concatenate([dxq, dxkv, dkr], axis=1).astype(BF16)

        @pl.when(pl.program_id(0) == 0)
        def _():
            dqn_ref[...] = dqn
            dkvn_ref[...] = dkvn

        @pl.when(pl.program_id(0) > 0)
        def _():
            dqn_ref[...] += dqn
            dkvn_ref[...] += dkvn

    def row(width):
        return pl.BlockSpec((tm, width), lambda i: (i, 0))

    def vec(width):
        return pl.BlockSpec((1, width), lambda i: (0, 0))

    return pl.pallas_call(
        body, name=name, grid=(t_all // tm,),
        in_specs=[row(lq), row(lkv), pl.BlockSpec((n_heads, tm, LANES), lambda i: (0, i, 0)), row(w),
                  vec(lq), vec(lkv), row(LANES), row(LANES)],
        out_specs=[row(w), vec(lq), vec(lkv)],
        out_shape=[jax.ShapeDtypeStruct((t_all, w), BF16), jax.ShapeDtypeStruct((1, lq), F32),
                   jax.ShapeDtypeStruct((1, lkv), F32)],
        compiler_params=_params(("arbitrary",)),
    )(dcq, dckv, dkr_h, proj, qn, kvn, cos, sin)


def _rope_q_epilogue(acc, cos_ref, sin_ref):
    parts = []
    for g in range(acc.shape[1] // LANES):
        blk = acc[:, g * LANES:(g + 1) * LANES]
        parts.append(_rope(blk, cos_ref[...], sin_ref[...]) if g % 2 == 1 else blk)
    return jnp.concatenate(parts, axis=1)


def _chunk_causal(rows, cols, row0=0):
    r = row0 + lax.broadcasted_iota(jnp.int32, (rows, cols), 0)
    c = lax.broadcasted_iota(jnp.int32, (rows, cols), 1)
    return (c >> 6) <= (r >> 6)


def _meta_keys(rows, cols):
    return lax.broadcasted_iota(jnp.int32, (rows, cols), 1) < N_META


def _attn_fwd(q, kv, kr, *, name, n_heads, tq, n_real, scale):
    t_all = q.shape[0]
    nq = (n_real - N_META) // tq
    assert N_META + nq * tq == n_real and tq % CHUNK == 0 and t_all >= LANES
    n_pad = t_all - n_real
    sub = tq // 2 if (tq // 2) % CHUNK == 0 else tq

    def body(q_ref, kv_ref, kr_ref, o_ref, lse_ref, k_scr, m_scr, l_scr, acc_scr):
        k_scr[:, :QK_NOPE] = kv_ref[:, :QK_NOPE]
        k_scr[:, QK_NOPE:] = kr_ref[...]
        if n_pad:
            o_ref[pl.ds(n_real, n_pad), :] = jnp.zeros((n_pad, V_HEAD), o_ref.dtype)
            lse_ref[pl.ds(n_real, n_pad), :] = jnp.zeros((n_pad, LANES), F32)

        def scores(qt, c0, width):
            return lax.dot_general(qt, k_scr[pl.ds(c0, width), :], _NT, preferred_element_type=F32) * scale

        def values(c0, width):
            return kv_ref[pl.ds(c0, width), QK_NOPE:]

        s = jnp.where(_meta_keys(LANES, LANES), scores(q_ref[pl.ds(0, LANES), :], 0, LANES), NEG_BIG)
        m = jnp.max(s, axis=-1, keepdims=True)
        p = jnp.exp(s - m)
        l = jnp.sum(p, axis=-1, keepdims=True)
        o_meta = jnp.dot(p.astype(BF16), values(0, LANES), preferred_element_type=F32) / l
        o_ref[pl.ds(0, N_META), :] = o_meta[:N_META].astype(o_ref.dtype)
        lse_ref[pl.ds(0, N_META), :] = jnp.broadcast_to((m + jnp.log(l))[:N_META], (N_META, LANES))

        parts = [(u * sub, sub) for u in range(tq // sub)]

        def accumulate(u0, s, vals):
            rows = pl.ds(u0, s.shape[0])
            m_prev = m_scr[rows, :]
            m_new = jnp.maximum(m_prev, jnp.max(s, axis=-1, keepdims=True))
            alpha = jnp.exp(m_prev - m_new)
            p = jnp.exp(s - m_new)
            l_scr[rows, :] = alpha * l_scr[rows, :] + jnp.sum(p, axis=-1, keepdims=True)
            acc_scr[rows, :] = alpha * acc_scr[rows, :] + jnp.dot(p.astype(BF16), vals, preferred_element_type=F32)
            m_scr[rows, :] = m_new

        def q_tile(i, carry):
            r0 = pl.multiple_of(N_META + i * tq, N_META)
            qts = [q_ref[pl.ds(r0 + u0, rows), :] for u0, rows in parts]
            m_scr[...] = jnp.full(m_scr.shape, NEG_BIG, F32)
            l_scr[...] = jnp.zeros(l_scr.shape, F32)
            acc_scr[...] = jnp.zeros(acc_scr.shape, F32)

            def full_blocks(j, width):
                c0 = pl.multiple_of(N_META + j * tq, N_META)
                for (u0, _), qt in zip(parts, qts):
                    accumulate(u0, scores(qt, c0, width), values(c0, width))

            def two_blocks(jj, c):
                full_blocks(2 * jj, 2 * tq)
                return c

            lax.fori_loop(0, i // 2, two_blocks, 0)

            @pl.when(i % 2 == 1)
            def _():
                full_blocks(i - 1, tq)

            for (u0, rows), qt in zip(parts, qts):
                width = u0 + rows
                s = jnp.concatenate([jnp.where(_meta_keys(rows, LANES), scores(qt, 0, LANES), NEG_BIG),
                                     jnp.where(_chunk_causal(rows, width, u0), scores(qt, r0, width), NEG_BIG)], axis=1)
                accumulate(u0, s, jnp.concatenate([values(0, LANES), values(r0, width)], axis=0))
            o_ref[pl.ds(r0, tq), :] = (acc_scr[...] / l_scr[...]).astype(o_ref.dtype)
            lse_ref[pl.ds(r0, tq), :] = jnp.broadcast_to(m_scr[...] + jnp.log(l_scr[...]), (tq, LANES))
            return carry

        lax.fori_loop(0, nq, q_tile, 0)

    def head(width):
        return pl.BlockSpec((t_all, width), lambda h: (0, h))

    return pl.pallas_call(
        body, name=name, grid=(n_heads,),
        in_specs=[head(HEAD_W), head(HEAD_W), pl.BlockSpec((t_all, LANES), lambda h: (0, 0))],
        out_specs=[head(V_HEAD), pl.BlockSpec((None, t_all, LANES), lambda h: (h, 0, 0))],
        out_shape=[jax.ShapeDtypeStruct((t_all, n_heads * V_HEAD), BF16),
                   jax.ShapeDtypeStruct((n_heads, t_all, LANES), F32)],
        scratch_shapes=[pltpu.VMEM((t_all, HEAD_W), BF16), pltpu.VMEM((tq, 1), F32), pltpu.VMEM((tq, 1), F32),
                        pltpu.VMEM((tq, V_HEAD), F32)],
        compiler_params=_params(("parallel",)),
    )(q, kv, kr)


def _attn_bwd(q, kv, kr, o, lse, do, cos, sin, *, name, n_heads, tq, n_real, scale):
    t_all = q.shape[0]
    nq = (n_real - N_META) // tq
    assert N_META + nq * tq == n_real and tq % CHUNK == 0 and t_all >= LANES
    n_pad = t_all - n_real

    def body(q_ref, kv_ref, kr_ref, o_ref, lse_ref, do_ref, cos_ref, sin_ref, dq_ref, dkv_ref, dkr_ref,
             k_scr, dk_scr, dv_scr, dq_scr):
        k_scr[:, :QK_NOPE] = kv_ref[:, :QK_NOPE]
        k_scr[:, QK_NOPE:] = kr_ref[...]
        dk_scr[...] = jnp.zeros(dk_scr.shape, F32)
        dv_scr[...] = jnp.zeros(dv_scr.shape, F32)
        if n_pad:
            dq_ref[pl.ds(n_real, n_pad), :] = jnp.zeros((n_pad, HEAD_W), dq_ref.dtype)

        def blocks(qt, dot, lse_t, delta, segments):
            kb = jnp.concatenate([k_scr[pl.ds(c0, w), :] for c0, w, _ in segments], axis=0)
            vb = jnp.concatenate([kv_ref[pl.ds(c0, w), QK_NOPE:] for c0, w, _ in segments], axis=0)
            s = lax.dot_general(qt, kb, _NT, preferred_element_type=F32) * scale
            p = jnp.exp(s - lse_t)
            if any(m is not None for _, _, m in segments):
                rows = qt.shape[0]
                mask = jnp.concatenate([jnp.ones((rows, w), jnp.bool_) if m is None else m for _, w, m in segments], axis=1)
                p = jnp.where(mask, p, 0.0)
            dp = lax.dot_general(dot, vb, _NT, preferred_element_type=F32)
            ds = (p * (dp - delta) * scale).astype(BF16)
            dv = lax.dot_general(p.astype(BF16), dot, _TN, preferred_element_type=F32)
            dk = lax.dot_general(ds, qt, _TN, preferred_element_type=F32)
            at = 0
            for c0, w, _ in segments:
                dv_scr[pl.ds(c0, w), :] += dv[at:at + w]
                dk_scr[pl.ds(c0, w), :] += dk[at:at + w]
                at += w
            return jnp.dot(ds, kb, preferred_element_type=F32)

        def block(qt, dot, lse_t, delta, c0, width, mask):
            return blocks(qt, dot, lse_t, delta, [(c0, width, mask)])

        def write_dq(r0, rows, dq):
            cs, sn = cos_ref[pl.ds(r0, rows), :], sin_ref[pl.ds(r0, rows), :]
            dq_ref[pl.ds(r0, rows), :] = jnp.concatenate(
                [dq[:, :QK_NOPE], _unrope(dq[:, QK_NOPE:], cs, sn)], axis=1).astype(dq_ref.dtype)

        rows_m = lax.broadcasted_iota(jnp.int32, (LANES, LANES), 0) < N_META
        dot = do_ref[pl.ds(0, LANES), :]
        delta = jnp.sum(dot.astype(F32) * o_ref[pl.ds(0, LANES), :].astype(F32), axis=-1, keepdims=True)
        dq = block(q_ref[pl.ds(0, LANES), :], dot, lse_ref[pl.ds(0, LANES), :1], delta, 0, LANES,
                   _meta_keys(LANES, LANES) & rows_m)
        write_dq(0, N_META, dq[:N_META])

        def q_tile(i, carry):
            r0 = pl.multiple_of(N_META + i * tq, N_META)
            qt = q_ref[pl.ds(r0, tq), :]
            dot = do_ref[pl.ds(r0, tq), :]
            lse_t = lse_ref[pl.ds(r0, tq), :1]
            delta = jnp.sum(dot.astype(F32) * o_ref[pl.ds(r0, tq), :].astype(F32), axis=-1, keepdims=True)
            dq_scr[...] = blocks(qt, dot, lse_t, delta, [(0, LANES, _meta_keys(tq, LANES)), (r0, tq, _chunk_causal(tq, tq))])

            def two_blocks(jj, c):
                c0 = pl.multiple_of(N_META + 2 * jj * tq, N_META)
                dq_scr[...] += block(qt, dot, lse_t, delta, c0, 2 * tq, None)
                return c

            lax.fori_loop(0, i // 2, two_blocks, 0)

            @pl.when(i % 2 == 1)
            def _():
                c0 = pl.multiple_of(N_META + (i - 1) * tq, N_META)
                dq_scr[...] += block(qt, dot, lse_t, delta, c0, tq, None)

            write_dq(r0, tq, dq_scr[...])
            return carry

        lax.fori_loop(0, nq, q_tile, 0)
        dk = dk_scr[...]
        dkv_ref[...] = jnp.concatenate([dk[:, :QK_NOPE], dv_scr[...]], axis=1).astype(dkv_ref.dtype)
        dkr_ref[...] = dk[:, QK_NOPE:]

    def head(width):
        return pl.BlockSpec((t_all, width), lambda h: (0, h))

    table = pl.BlockSpec((t_all, LANES), lambda h: (0, 0))
    per_head = pl.BlockSpec((None, t_all, LANES), lambda h: (h, 0, 0))
    return pl.pallas_call(
        body, name=name, grid=(n_heads,),
        in_specs=[head(HEAD_W), head(HEAD_W), table, head(V_HEAD), per_head, head(V_HEAD), table, table],
        out_specs=[head(HEAD_W), head(HEAD_W), per_head],
        out_shape=[jax.ShapeDtypeStruct((t_all, n_heads * HEAD_W), BF16), jax.ShapeDtypeStruct((t_all, n_heads * HEAD_W), BF16),
                   jax.ShapeDtypeStruct((n_heads, t_all, LANES), F32)],
        scratch_shapes=[pltpu.VMEM((t_all, HEAD_W), BF16), pltpu.VMEM((t_all, HEAD_W), F32), pltpu.VMEM((t_all, V_HEAD), F32),
                        pltpu.VMEM((tq, HEAD_W), F32)],
        compiler_params=_params(("parallel",)),
    )(q, kv, kr, o, lse, do, cos, sin)


LRU_ROWS = 128


def _shifted_back(ref, t0, rows, shift_max):
    main = ref[pl.ds(t0, rows), :]
    prev = ref[pl.ds(pl.multiple_of(jnp.maximum(t0 - SUBLANES, 0), SUBLANES), SUBLANES), :]
    prev = jnp.where(t0 > 0, prev, 0.0)
    ext = jnp.concatenate([prev, main], axis=0)
    return [main] + [pltpu.roll(ext, s, 0)[SUBLANES:, :] for s in range(1, shift_max + 1)]


def _shifted_ahead(ref, t0, rows, t_all, shift_max):
    main = ref[pl.ds(t0, rows), :]
    nxt = ref[pl.ds(pl.multiple_of(jnp.minimum(t0 + rows, t_all - SUBLANES), SUBLANES), SUBLANES), :]
    nxt = jnp.where(t0 + rows < t_all, nxt, 0.0)
    ext = jnp.concatenate([main, nxt], axis=0)
    return [main] + [pltpu.roll(ext, rows + SUBLANES - s, 0)[:rows, :] for s in range(1, shift_max + 1)]


def _conv_fwd(xp_ref, t0, rows, cw, cb):
    sh = _shifted_back(xp_ref, t0, rows, 3)
    out = cb + cw[3:4, :] * sh[0]
    for k in range(3):
        out = out + cw[k:k + 1, :] * sh[3 - k]
    return out, sh


def _lru_gates(xb, wga, bga, wgx, bgx, sp):
    xbb = xb.astype(BF16)
    r = _sigmoid(jnp.dot(xbb, wga, preferred_element_type=F32) + bga)
    ig = _sigmoid(jnp.dot(xbb, wgx, preferred_element_type=F32) + bgx)
    la = -LRU_C * r * sp
    a = jnp.exp(la)
    s = jnp.sqrt(_neg_expm1(2.0 * la))
    return xbb, r, ig, a, s


def _scan_tile(a, b, reverse):
    rows = a.shape[0]
    ridx = lax.broadcasted_iota(jnp.int32, a.shape, 0)
    s = 1
    while s < rows:
        if reverse:
            keep = ridx < rows - s
            a_sh, b_sh = pltpu.roll(a, rows - s, 0), pltpu.roll(b, rows - s, 0)
        else:
            keep = ridx >= s
            a_sh, b_sh = pltpu.roll(a, s, 0), pltpu.roll(b, s, 0)
        b = jnp.where(keep, a * b_sh + b, b)
        a = jnp.where(keep, a * a_sh, a)
        s *= 2
    return a, b


def _lru_fwd(xy, conv_w, conv_b, wga, bga, wgx, bgx, lam, *, name):
    t_all = xy.shape[0]
    dr = xy.shape[1] // 2
    c = LANES
    nblk = dr // c
    rows = LRU_ROWS
    nt = t_all // rows

    def body(xp_ref, yp_ref, cw_ref, cb_ref, wga_ref, bga_ref, wgx_ref, bgx_ref, lam_ref, hs_ref, hsy_ref):
        cw, cb = cw_ref[...], cb_ref[...]
        sp = _softplus_neg(lam_ref[...])

        def tile(t, h_in):
            t0 = pl.multiple_of(t * rows, rows)
            xb, _ = _conv_fwd(xp_ref, t0, rows, cw, cb)
            _, _, ig, a, s = _lru_gates(xb, wga_ref[0], bga_ref[...], wgx_ref[0], bgx_ref[...], sp)
            cum_a, h0 = _scan_tile(a, s * (ig * xb), reverse=False)
            hs = cum_a * h_in + h0
            hs_ref[pl.ds(t0, rows), :] = hs
            hsy_ref[pl.ds(t0, rows), :] = (hs * _gelu(yp_ref[pl.ds(t0, rows), :])).astype(BF16)
            return hs[rows - 1:, :]

        lax.fori_loop(0, nt, tile, jnp.zeros((1, c), F32))

    col = pl.BlockSpec((t_all, c), lambda b: (0, b))
    vec = pl.BlockSpec((1, c), lambda b: (0, b))
    wsp = pl.BlockSpec((1, c, c), lambda b: (b, 0, 0))
    return pl.pallas_call(
        body, name=name, grid=(nblk,),
        in_specs=[col, pl.BlockSpec((t_all, c), lambda b: (0, nblk + b)), pl.BlockSpec((4, c), lambda b: (0, b)), vec,
                  wsp, vec, wsp, vec, vec],
        out_specs=[col, col],
        out_shape=[jax.ShapeDtypeStruct((t_all, dr), F32), jax.ShapeDtypeStruct((t_all, dr), BF16)],
        compiler_params=_params(("parallel",)),
    )(xy, xy, conv_w, conv_b, wga, bga, wgx, bgx, lam)


def _lru_bwd(xy, hs, dhsy, conv_w, conv_b, wga, bga, wgx, bgx, lam, *, name):
    t_all = xy.shape[0]
    dr = xy.shape[1] // 2
    c = LANES
    nblk = dr // c
    rows = LRU_ROWS
    nt = t_all // rows

    def body(xp_ref, yp_ref, hs_ref, dh_ref, cw_ref, cb_ref, wga_ref, bga_ref, wgx_ref, bgx_ref, lam_ref,
             dxp_ref, dyp_ref, dcw_ref, dcb_ref, dwga_ref, dbga_ref, dwgx_ref, dbgx_ref, dlam_ref,
             xb_scr, r_scr, i_scr, a_scr):
        cw, cb = cw_ref[...], cb_ref[...]
        lamv = lam_ref[...]
        sp = _softplus_neg(lamv)
        sig_neg = 1.0 / (1.0 + jnp.exp(lamv))
        wga_v, wgx_v = wga_ref[0], wgx_ref[0]

        def recompute(t, carry):
            t0 = pl.multiple_of(t * rows, rows)
            xb, _ = _conv_fwd(xp_ref, t0, rows, cw, cb)
            _, r, ig, a, _ = _lru_gates(xb, wga_v, bga_ref[...], wgx_v, bgx_ref[...], sp)
            xb_scr[pl.ds(t0, rows), :] = xb
            r_scr[pl.ds(t0, rows), :] = r
            i_scr[pl.ds(t0, rows), :] = ig
            a_scr[pl.ds(t0, rows), :] = a
            return carry

        lax.fori_loop(0, nt, recompute, 0)
        dwga_ref[...] = jnp.zeros(dwga_ref.shape, F32)
        dwgx_ref[...] = jnp.zeros(dwgx_ref.shape, F32)

        def tile(ti, carry):
            lam_in, dbga, dbgx, dlam, dcw, dcb = carry
            t = nt - 1 - ti
            t0 = pl.multiple_of(t * rows, rows)
            a_now, a_next = _shifted_ahead(a_scr, t0, rows, t_all, 1)
            yp = yp_ref[pl.ds(t0, rows), :]
            dhy = dh_ref[pl.ds(t0, rows), :]
            cum_a, lam0 = _scan_tile(a_next, dhy * _gelu(yp), reverse=True)
            lam_t = cum_a * lam_in + lam0
            hs_now, hs_prev = _shifted_back(hs_ref, t0, rows, 1)
            da = lam_t * hs_prev
            xb = xb_scr[pl.ds(t0, rows), :]
            r = r_scr[pl.ds(t0, rows), :]
            ig = i_scr[pl.ds(t0, rows), :]
            la = -LRU_C * r * sp
            s = jnp.sqrt(_neg_expm1(2.0 * la))
            d_ixb = lam_t * s
            dla = da * a_now - (lam_t * ig * xb) * (a_now * a_now / s)
            dzr = dla * (-LRU_C * sp) * r * (1.0 - r)
            dzi = d_ixb * xb * ig * (1.0 - ig)
            dzr_b, dzi_b = dzr.astype(BF16), dzi.astype(BF16)
            xbb = xb.astype(BF16)
            dwga_ref[0] += lax.dot_general(xbb, dzr_b, _TN, preferred_element_type=F32)
            dwgx_ref[0] += lax.dot_general(xbb, dzi_b, _TN, preferred_element_type=F32)
            dxb = (d_ixb * ig + lax.dot_general(dzr_b, wga_v, _NT, preferred_element_type=F32)
                   + lax.dot_general(dzi_b, wgx_v, _NT, preferred_element_type=F32))
            xb_scr[pl.ds(t0, rows), :] = dxb
            dyp_ref[pl.ds(t0, rows), :] = (dhy * hs_now * _gelu_grad(yp)).astype(BF16)
            ahead = _shifted_ahead(xb_scr, t0, rows, t_all, 3)
            dxp = cw[3:4, :] * ahead[0]
            for k in range(3):
                dxp = dxp + cw[k:k + 1, :] * ahead[3 - k]
            dxp_ref[pl.ds(t0, rows), :] = dxp.astype(BF16)
            back = _shifted_back(xp_ref, t0, rows, 3)
            dcw_t = jnp.concatenate([jnp.sum(dxb * back[3 - k], axis=0, keepdims=True) for k in range(4)], axis=0)
            return (lam_t[:1, :], dbga + jnp.sum(dzr, axis=0, keepdims=True), dbgx + jnp.sum(dzi, axis=0, keepdims=True),
                    dlam + jnp.sum(dla * r, axis=0, keepdims=True), dcw + dcw_t, dcb + jnp.sum(dxb, axis=0, keepdims=True))

        zero = jnp.zeros((1, c), F32)
        _, dbga, dbgx, dlam, dcw, dcb = lax.fori_loop(0, nt, tile, (zero, zero, zero, zero, jnp.zeros((4, c), F32), zero))
        dbga_ref[...] = dbga
        dbgx_ref[...] = dbgx
        dlam_ref[...] = dlam * (LRU_C * sig_neg)
        dcw_ref[...] = dcw
        dcb_ref[...] = dcb

    col = pl.BlockSpec((t_all, c), lambda b: (0, b))
    col2 = pl.BlockSpec((t_all, c), lambda b: (0, nblk + b))
    vec = pl.BlockSpec((1, c), lambda b: (0, b))
    tap = pl.BlockSpec((4, c), lambda b: (0, b))
    wsp = pl.BlockSpec((1, c, c), lambda b: (b, 0, 0))
    vshape = jax.ShapeDtypeStruct((1, dr), F32)
    wshape = jax.ShapeDtypeStruct((nblk, c, c), F32)
    def planes_body(*refs):
        dxy_ref = refs[11]
        body(*refs[:11], dxy_ref.at[0], dxy_ref.at[1], *refs[12:])

    return pl.pallas_call(
        planes_body, name=name, grid=(nblk,),
        in_specs=[col, col2, col, col, tap, vec, wsp, vec, wsp, vec, vec],
        out_specs=[pl.BlockSpec((2, t_all, c), lambda b: (0, 0, b)), tap, vec, wsp, vec, wsp, vec, vec],
        out_shape=[jax.ShapeDtypeStruct((2, t_all, dr), BF16),
                   jax.ShapeDtypeStruct((4, dr), F32), vshape, wshape, vshape, wshape, vshape, vshape],
        scratch_shapes=[pltpu.VMEM((t_all, c), F32)] * 4,
        compiler_params=_params(("parallel",)),
    )(xy, xy, hs, dhsy, conv_w, conv_b, wga, bga, wgx, bgx, lam)


def _mesh_pos():
    return lax.axis_index("x"), lax.axis_index("y"), lax.axis_index("c")


def _all_gather(shards, *, name):
    n = len(shards)

    def body(*refs):
        ins, outs, token = refs[:n], refs[n:2 * n], refs[2 * n]
        send_sems, recv_sems, local_sems = refs[2 * n + 1:]
        token[...] = jnp.zeros(token.shape, token.dtype)
        x, y, c = _mesh_pos()
        me, sibling = (x, y, c), (x, y, 1 - c)
        chips = [(1 - x, y), (x, 1 - y), (1 - x, 1 - y)]
        slot = _slot

        def copy(a, k, block, to, src=None):
            dst = outs[a].at[slot(block)]
            return pltpu.make_async_remote_copy(
                src_ref=dst if src is None else src, dst_ref=dst, send_sem=send_sems.at[a, k],
                recv_sem=recv_sems.at[a, k], device_id=to, device_id_type=MESH)

        mine = [pltpu.make_async_copy(ins[a], outs[a].at[slot(me)], local_sems.at[a]) for a in range(n)]
        for cp in mine:
            cp.start()
        first = []
        for a in range(n):
            first.append(copy(a, 0, me, sibling, src=ins[a]))
            first += [copy(a, 1 + j, me, (*chip, c), src=ins[a]) for j, chip in enumerate(chips)]
        for cp in first:
            cp.start()
        passed = []
        for a in range(n):
            for j, chip in enumerate(chips):
                copy(a, 1 + j, (*chip, c), me).wait_recv()
                fwd = copy(a, 4 + j, (*chip, c), sibling)
                fwd.start()
                passed.append(fwd)
        for a in range(n):
            copy(a, 0, sibling, me).wait_recv()
            for j, chip in enumerate(chips):
                copy(a, 4 + j, (*chip, 1 - c), me).wait_recv()
        for cp in first + passed:
            cp.wait_send()
        for cp in mine:
            cp.wait()

    any_spec = pl.BlockSpec(memory_space=pl.ANY)
    outs = pl.pallas_call(
        body, name=name,
        in_specs=[any_spec] * n, out_specs=[any_spec] * n + [pl.BlockSpec(memory_space=pltpu.VMEM)],
        out_shape=[jax.ShapeDtypeStruct((N_DEV,) + s.shape, s.dtype) for s in shards]
        + [jax.ShapeDtypeStruct((SUBLANES, LANES), F32)],
        scratch_shapes=[pltpu.SemaphoreType.DMA((n, 7)), pltpu.SemaphoreType.DMA((n, 7)), pltpu.SemaphoreType.DMA((n,))],
    )(*shards)
    return list(outs[:n]), outs[n][0, 0]


_HBM = pl.BlockSpec(memory_space=pltpu.HBM)
_SEM = pl.BlockSpec(memory_space=pltpu.SEMAPHORE)
_ANY = pl.BlockSpec(memory_space=pl.ANY)
_EFFECT = pltpu.SideEffectType.DATAFLOW_SIDE_EFFECTING


def _slot(p):
    return 4 * p[0] + 2 * p[1] + p[2]


def _remote(src, dst, send, recv, idx, to):
    return pltpu.make_async_remote_copy(src_ref=src, dst_ref=dst, send_sem=send.at[idx], recv_sem=recv.at[idx],
                                        device_id=to, device_id_type=MESH)


def _ag_plan_own(a, src, land, send, recv):
    x, y, c = _mesh_pos()
    dst = land.at[_slot((x, y, c))]
    targets = [(x, y, 1 - c), (1 - x, y, c), (x, 1 - y, c), (1 - x, 1 - y, c)]
    return [_remote(src, dst, send, recv, 4 * a + k, to) for k, to in enumerate(targets)]


def _ag_plan_pass(a, src, land, send, recv):
    x, y, c = _mesh_pos()
    blocks = [land.at[_slot((px, py, c))] for px, py in ((1 - x, y), (x, 1 - y), (1 - x, 1 - y))]
    return [_remote(blk, blk, send, recv, 3 * a + k, (x, y, 1 - c)) for k, blk in enumerate(blocks)]


def _rs_plan_sibling(a, src, land, send, recv):
    x, y, c = _mesh_pos()
    return [_remote(src.at[2 * j + (1 - c)], land.at[j], send, recv, 4 * a + j, (x, y, 1 - c)) for j in range(4)]


def _rs_plan_chips(a, src, land, send, recv):
    x, y, c = _mesh_pos()
    out = []
    for k in (1, 2, 3):
        px = 1 - x if k & 2 else x
        py = 1 - y if k & 1 else y
        out.append(_remote(src.at[2 * px + py], land.at[k - 1], send, recv, 3 * a + k - 1, (px, py, c)))
    return out


def _in_hbm(a):
    return pltpu.with_memory_space_constraint(a, pltpu.HBM)


def _exchange_start(srcs, lands, plan, n_k, *, name):
    ns, n = len(srcs), len(lands)

    def body(*refs):
        src_refs, land_refs = refs[:ns], refs[ns:ns + n]
        send, recv = refs[ns + n], refs[ns + n + 1]
        token = refs[-1]
        for a in range(n):
            for cp in plan(a, src_refs[a] if ns else None, land_refs[a], send, recv):
                cp.start()
        token[...] = jnp.zeros(token.shape, token.dtype)

    bufs = list(srcs) + list(lands)
    outs = pl.pallas_call(
        body, name=name,
        out_shape=(pltpu.SemaphoreType.DMA((n * n_k,)), pltpu.SemaphoreType.DMA((n * n_k,)),
                   *[pltpu.HBM(b.shape, b.dtype) for b in bufs], jax.ShapeDtypeStruct((SUBLANES, LANES), F32)),
        in_specs=[_HBM] * (ns + n),
        out_specs=(_SEM, _SEM, *[_HBM] * (ns + n), pl.BlockSpec(memory_space=pltpu.VMEM)),
        input_output_aliases={i: 2 + i for i in range(ns + n)},
        compiler_params=pltpu.CompilerParams(has_side_effects=_EFFECT),
    )(*[_in_hbm(b) for b in bufs])
    return outs[0], outs[1], list(outs[2:2 + ns]), list(outs[2 + ns:2 + ns + n]), outs[-1]


def _exchange_wait(started, plan, after, *, name):
    send, recv, srcs, lands, _ = started
    ns, n = len(srcs), len(lands)

    def body(*refs):
        src_refs, land_refs = refs[:ns], refs[ns:ns + n]
        send_ref, recv_ref = refs[ns + n], refs[ns + n + 1]
        for a in range(n):
            for cp in plan(a, src_refs[a] if ns else None, land_refs[a], send_ref, recv_ref):
                cp.wait_send()
                cp.wait_recv()

    bufs = list(srcs) + list(lands)
    outs = pl.pallas_call(
        body, name=name,
        out_shape=tuple(pltpu.HBM(b.shape, b.dtype) for b in bufs),
        in_specs=[_HBM] * (ns + n) + [_SEM, _SEM, _ANY],
        out_specs=tuple([_HBM] * (ns + n)),
        input_output_aliases={i: i for i in range(ns + n)},
        compiler_params=pltpu.CompilerParams(has_side_effects=_EFFECT),
    )(*bufs, send, recv, after)
    return list(outs[:ns]), list(outs[ns:])


def _pair_add(grads, landed, core, *, name, tr):
    _, r_all, c_all = grads.shape

    def body(core_ref, g_ref, l_ref, o_ref):
        o_ref[...] = (g_ref[...].astype(F32) + l_ref[...].astype(F32)).astype(o_ref.dtype)

    return pl.pallas_call(
        body, name=name,
        grid_spec=pltpu.PrefetchScalarGridSpec(
            num_scalar_prefetch=1, grid=(4, r_all // tr),
            in_specs=[pl.BlockSpec((None, tr, c_all), lambda j, i, core_ref: (2 * j + core_ref[0], i, 0)),
                      pl.BlockSpec((None, tr, c_all), lambda j, i, core_ref: (j, i, 0))],
            out_specs=pl.BlockSpec((None, tr, c_all), lambda j, i, core_ref: (j, i, 0))),
        out_shape=jax.ShapeDtypeStruct((4, r_all, c_all), grads.dtype),
        compiler_params=_params(("parallel", "parallel")),
    )(core, grads, landed)


def _adamw_math(w, g, m, v):
    m2 = ADAM_B1 * m + (1.0 - ADAM_B1) * g
    v2 = ADAM_B2 * v + (1.0 - ADAM_B2) * (g * g)
    m_hat = m2 / (1.0 - ADAM_B1 ** ADAM_STEP)
    v_hat = v2 / (1.0 - ADAM_B2 ** ADAM_STEP)
    delta = -ADAM_LR * (m_hat / (jnp.sqrt(v_hat) + ADAM_EPS) + ADAM_WD * w)
    return delta, m2, v2


def _adamw(w, m, v, terms, order, *, name, tr, col_block=None, own=None, stack=None):
    r_all, c_all = w.shape
    n_slots = terms.shape[0]

    def body(*refs):
        if col_block is not None or own is not None:
            refs = refs[1:]
        own_ref = None
        if own is not None:
            own_ref, refs = refs[0], refs[1:]
        w_ref, m_ref, v_ref, t_ref, g_ref, d_ref, m2_ref, v2_ref = refs
        if own_ref is not None:
            g = own_ref[...].astype(F32) + t_ref[order[0]].astype(F32)
        else:
            g = t_ref[order[0]].astype(F32)
        for s in order[1:]:
            g = g + t_ref[s].astype(F32)
        delta, m2, v2 = _adamw_math(w_ref[...], g, m_ref[...], v_ref[...])
        g_ref[...] = g
        d_ref[...] = delta
        m2_ref[...] = m2
        v2_ref[...] = v2

    shape = jax.ShapeDtypeStruct((r_all, c_all), F32)
    if own is not None:
        layer, n_layers, prev = stack
        row = pl.BlockSpec((tr, c_all), lambda i, idx: (i, 0))
        slab = pl.BlockSpec((None, tr, c_all), lambda i, idx: (layer, i, 0))
        carried = [] if prev is None else list(prev)

        def stacked_body(*refs):
            body(*refs[:6], *refs[6 + len(carried):])

        return pl.pallas_call(
            stacked_body, name=name,
            grid_spec=pltpu.PrefetchScalarGridSpec(
                num_scalar_prefetch=1, grid=(r_all // tr,),
                in_specs=[pl.BlockSpec((None, tr, c_all), lambda i, idx: (idx[0], i, 0)), row, row, row,
                          pl.BlockSpec((n_slots, tr, c_all), lambda i, idx: (0, i, 0))] + [_ANY] * len(carried),
                out_specs=[slab] * 4),
            out_shape=[jax.ShapeDtypeStruct((n_layers, r_all, c_all), F32)] * 4,
            input_output_aliases={6 + k: k for k in range(len(carried))},
            compiler_params=_params(("parallel",)),
        )(own[1], own[0], w, m, v, terms, *carried)
    if col_block is None:
        row = pl.BlockSpec((tr, c_all), lambda i: (i, 0))
        return pl.pallas_call(
            body, name=name, grid=(r_all // tr,),
            in_specs=[row, row, row, pl.BlockSpec((n_slots, tr, c_all), lambda i: (0, i, 0))],
            out_specs=[row] * 4, out_shape=[shape] * 4, compiler_params=_params(("parallel",)),
        )(w, m, v, terms)
    row = pl.BlockSpec((tr, c_all), lambda i, blk: (i, 0))
    return pl.pallas_call(
        body, name=name,
        grid_spec=pltpu.PrefetchScalarGridSpec(
            num_scalar_prefetch=1, grid=(r_all // tr,),
            in_specs=[row, row, row, pl.BlockSpec((n_slots, tr, c_all), lambda i, blk: (0, i, blk[0]))],
            out_specs=[row] * 4),
        out_shape=[shape] * 4, compiler_params=_params(("parallel",)),
    )(col_block, w, m, v, terms)


def _rope_tables(t_all):
    pos = jnp.arange(t_all, dtype=F32)
    inv_freq = ROPE_THETA ** (-jnp.arange(0, QK_ROPE, 2, dtype=F32) / QK_ROPE)
    ang = pos[:, None] * inv_freq[None, :]
    cos, sin = jnp.cos(ang), jnp.sin(ang)
    return jnp.tile(cos, (1, LANES // (QK_ROPE // 2))), jnp.tile(sin, (1, LANES // (QK_ROPE // 2)))


def _adam_row_tile(r_all, c_all, block_bytes=512 * 1024):
    target = max(SUBLANES, block_bytes // (4 * c_all))
    return _pick(r_all, [t for t in (1024, 704, 512, 352, 256, 176, 128, 64, 32, 16, 8) if t <= target])


def _rows_natural(wg):
    return wg.reshape(wg.shape[0] * wg.shape[1], wg.shape[2])


def _mla_layer_fwd(tag, h, g_mix, ws, qn, kvn, cos, sin, *, tm, tq, n_heads, scale, n_real):
    w_in, w_uq, w_ukv, w_o = _rows_natural(ws[0]), ws[1], ws[2], _rows_natural(ws[3])
    t_all, d = h.shape
    lq, lkv = qn.shape[1], kvn.shape[1]
    tmb = _pick(t_all, _ROW_TILES)
    hn = _rmsnorm_fwd(h, g_mix, name=f"norm_mix{tag}", tm=tm)
    proj = _mm_nn(hn, w_in, name=f"mla_in{tag}", out_dtype=F32, tm=tmb, tn=w_in.shape[1], tk=_pick(d, _DIVS))
    cq, ckv, kr = _mla_prep_fwd(proj, qn, kvn, cos, sin, name=f"mla_prep{tag}", tm=tm, lq=lq, lkv=lkv)
    q = _mm_nn(cq, w_uq, name=f"mla_q{tag}", out_dtype=BF16, tm=tmb, tn=w_uq.shape[2], tk=lq, b_blocked=True,
               epilogue=_rope_q_epilogue, extras=(cos, sin))
    kv = _mm_nn(ckv, w_ukv, name=f"mla_kv{tag}", out_dtype=BF16, tm=tmb, tn=w_ukv.shape[2], tk=lkv, b_blocked=True)
    o, lse = _attn_fwd(q, kv, kr, name=f"attn_fwd{tag}", n_heads=n_heads, tq=tq, n_real=n_real, scale=scale)
    h_mid = _mm_nn(o, w_o, name=f"mla_o{tag}", out_dtype=F32, tm=tm, tn=d, tk=o.shape[1], res=h)
    return h_mid, (hn, proj, cq, ckv, kr, q, kv, o, lse)


def _mla_layer_bwd(tag, dh, dh_b, h_in, saved, g_mix, ws, qn, kvn, cos, sin, *, tm, tq, n_heads, scale, n_real):
    hn, proj, cq, ckv, kr, q, kv, o, lse = saved
    w_in, w_uq, w_ukv, w_o = _rows_natural(ws[0]), ws[1], ws[2], _rows_natural(ws[3])
    t_all, d = h_in.shape
    lq, lkv = qn.shape[1], kvn.shape[1]
    ov = o.shape[1]
    tmb = _pick(t_all, _ROW_TILES)
    tn_d, tk_d = _pick(d, _DIVS[1:]), _pick(d, _DIVS)
    do = _mm_nt(dh_b, w_o, name=f"mla_do{tag}", out_dtype=BF16, tm=tmb, tn=_pick(ov, _DIVS[1:]), tk=tk_d)
    dw_o = _mm_tn(o, dh_b, name=f"mla_dwo{tag}", out_dtype=BF16, tm=_pick(ov, _DIVS[2:]), tn=tn_d, tk=t_all)
    dq, dkv, dkr_h = _attn_bwd(q, kv, kr, o, lse, do, cos, sin, name=f"attn_bwd{tag}", n_heads=n_heads, tq=tq, n_real=n_real,
                               scale=scale)
    hw, kw = w_uq.shape[2], w_ukv.shape[2]
    dw_uq = _mm_tn(cq, dq, name=f"mla_dwuq{tag}", out_dtype=BF16, tm=lq, tn=hw, tk=t_all, out_block=hw)
    dcq = _mm_nt(dq, w_uq, name=f"mla_dcq{tag}", out_dtype=F32, tm=tm, tn=lq, tk=dq.shape[1], b_blocked=True)
    dw_ukv = _mm_tn(ckv, dkv, name=f"mla_dwukv{tag}", out_dtype=BF16, tm=lkv, tn=kw, tk=t_all, out_block=kw)
    dckv = _mm_nt(dkv, w_ukv, name=f"mla_dckv{tag}", out_dtype=F32, tm=tm, tn=lkv, tk=dkv.shape[1], b_blocked=True)
    dproj, dqn, dkvn = _mla_prep_bwd(dcq, dckv, dkr_h, proj, qn, kvn, cos, sin, name=f"mla_prep_bwd{tag}", tm=tm, lq=lq, lkv=lkv)
    wc = w_in.shape[1]
    dw_in = _mm_tn(hn, dproj, name=f"mla_dwin{tag}", out_dtype=BF16, tm=_pick(d, _DIVS[2:]), tn=wc, tk=t_all)
    dhn = _mm_nt(dproj, w_in, name=f"mla_dhn{tag}", out_dtype=F32, tm=tmb, tn=tn_d, tk=wc)
    dh, dh_b, dg = _rmsnorm_bwd(dhn, h_in, g_mix, dh, name=f"norm_mix_bwd{tag}", tm=tm)
    return dh, dh_b, dg, dqn, dkvn, [dw_in.reshape(N_DEV, -1, wc), dw_uq, dw_ukv, dw_o.reshape(N_DEV, -1, d)]


def _lru_layer_fwd(tag, h, g_mix, ws, small, *, tm):
    w_lin, w_lo = ws[0], _rows_natural(ws[1])
    t_all, d = h.shape
    dr = w_lo.shape[0]
    tmb = _pick(t_all, _ROW_TILES)
    hn = _rmsnorm_fwd(h, g_mix, name=f"norm_mix{tag}", tm=tm)
    xy = _mm_nn(hn, w_lin, name=f"lru_in{tag}", out_dtype=F32, tm=tmb, tn=w_lin.shape[2], tk=_pick(d, _DIVS), b_blocked=True,
                rows_outer=True)
    hs, hsy = _lru_fwd(xy, *small, name=f"lru_fwd{tag}")
    h_mid = _mm_nn(hsy, w_lo, name=f"lru_o{tag}", out_dtype=F32, tm=tm, tn=d, tk=dr, res=h)
    return h_mid, (hn, xy, hs, hsy)


def _lru_layer_bwd(tag, dh, dh_b, h_in, saved, g_mix, ws, small, *, tm):
    hn, xy, hs, hsy = saved
    w_lin, w_lo = ws[0], _rows_natural(ws[1])
    t_all, d = h_in.shape
    dr = w_lo.shape[0]
    tmb = _pick(t_all, _ROW_TILES)
    tn_d, tk_d = _pick(d, _DIVS[1:]), _pick(d, _DIVS)
    dhsy = _mm_nt(dh_b, w_lo, name=f"lru_dhsy{tag}", out_dtype=F32, tm=tmb, tn=_pick(dr, _DIVS[1:]), tk=tk_d)
    dw_lo = _mm_tn(hsy, dh_b, name=f"lru_dwo{tag}", out_dtype=BF16, tm=_pick(dr, _DIVS[2:]), tn=tn_d, tk=t_all)
    dxy, *dsmall = _lru_bwd(xy, hs, dhsy, *small, name=f"lru_bwd{tag}")
    lw = w_lin.shape[2]
    dw_lin = _mm_tn(hn, dxy, name=f"lru_dwin{tag}", out_dtype=BF16, tm=tn_d, tn=lw, tk=t_all, out_block=lw)
    dhn = _mm_nt(dxy, w_lin, name=f"lru_dhn{tag}", out_dtype=F32, tm=tm, tn=tn_d, tk=2 * dr, b_blocked=True)
    dh, dh_b, dg = _rmsnorm_bwd(dhn, h_in, g_mix, dh, name=f"norm_mix_bwd{tag}", tm=tm)
    return dh, dh_b, dg, tuple(dsmall), [dw_lin, dw_lo.reshape(N_DEV, -1, d)]


def _ffn_layer_fwd(tag, h_mid, g_ffn, ws, *, tm):
    w_gu, w_down = ws[0], _rows_natural(ws[1])
    t_all, d = h_mid.shape
    f_all = w_down.shape[0]
    tmb = _pick(t_all, _ROW_TILES)
    fk = _pick(f_all, (1408,) + _DIVS[1:])
    hn2 = _rmsnorm_fwd(h_mid, g_ffn, name=f"norm_ffn{tag}", tm=tm)
    gu, act = _ffn_up(hn2, w_gu, name=f"ffn_up{tag}", tm=tm)
    h_out = _mm_nn(act, w_down, name=f"ffn_down{tag}", out_dtype=F32, tm=tm, tn=_pick(d, _DIVS[1:]), tk=f_all, res=h_mid)
    return h_out, (hn2, gu, act)


def _ffn_layer_bwd(tag, dh, dh_b, h_mid, saved, g_ffn, ws, *, tm):
    hn2, gu, act = saved
    w_gu, w_down = ws[0], _rows_natural(ws[1])
    t_all, d = h_mid.shape
    f_all = w_down.shape[0]
    f_local = w_gu.shape[2]
    tmb = _pick(t_all, _ROW_TILES)
    fk = _pick(f_all, (1408,) + _DIVS[1:])
    tn_d, tk_d = _pick(d, _DIVS[1:]), _pick(d, _DIVS)
    dgu = _ffn_dact(dh_b, w_down, gu, name=f"ffn_dact{tag}", tm=tm, tn=f_local)
    dw_down = _mm_tn(act, dh_b, name=f"ffn_dwdown{tag}", out_dtype=BF16, tm=fk, tn=_pick(d, _DIVS[2:]), tk=t_all)
    dhn2 = _mm_nt(dgu, w_gu, name=f"ffn_dhn{tag}", out_dtype=F32, tm=tm, tn=_pick(d, _DIVS[2:]), tk=2 * f_all, b_blocked=True)
    dw_gu = _mm_tn(hn2, dgu, name=f"ffn_dwgu{tag}", out_dtype=BF16, tm=_pick(d, _DIVS[2:]), tn=f_local, tk=t_all, out_block=f_local,
                   cols_outer=True)
    dh, dh_b, dg = _rmsnorm_bwd(dhn2, h_mid, g_ffn, dh, name=f"norm_ffn_bwd{tag}", tm=tm)
    return dh, dh_b, dg, [dw_gu, dw_down.reshape(N_DEV, -1, d)]


def kernel(x, meta_tokens, norm_mix, norm_ffn, norm_final, mla_w_in, mla_q_norm, mla_kv_norm, mla_w_uq, mla_w_ukv, mla_w_o, lru_w_in, lru_conv_w, lru_conv_b, lru_w_gate_a, lru_b_gate_a, lru_w_gate_x, lru_b_gate_x, lru_lambda, lru_w_o, ffn_w_gu, ffn_w_down, loss_target, m_meta_tokens, m_norm_mix, m_norm_ffn, m_norm_final, m_mla_w_in, m_mla_q_norm, m_mla_kv_norm, m_mla_w_uq, m_mla_w_ukv, m_mla_w_o, m_lru_w_in, m_lru_conv_w, m_lru_conv_b, m_lru_w_gate_a, m_lru_b_gate_a, m_lru_w_gate_x, m_lru_b_gate_x, m_lru_lambda, m_lru_w_o, m_ffn_w_gu, m_ffn_w_down, v_meta_tokens, v_norm_mix, v_norm_ffn, v_norm_final, v_mla_w_in, v_mla_q_norm, v_mla_kv_norm, v_mla_w_uq, v_mla_w_ukv, v_mla_w_o, v_lru_w_in, v_lru_conv_w, v_lru_conv_b, v_lru_w_gate_a, v_lru_b_gate_a, v_lru_w_gate_x, v_lru_b_gate_x, v_lru_lambda, v_lru_w_o, v_ffn_w_gu, v_ffn_w_down):
    seq, d = x.shape[1], x.shape[2]
    assert seq % CHUNK == 0
    n_real = N_META + seq
    t_all = -(-n_real // LANES) * LANES
    tm = _pick(t_all, (384, 256, 128))
    tq = _pick(seq, (512, 256, 128, 64))
    depth = norm_mix.shape[0]
    n_mla, n_lru = mla_w_in.shape[0], lru_w_in.shape[0]
    lq, lkv = mla_q_norm.shape[1], mla_kv_norm.shape[1]
    w_in_cols = lq + lkv + LANES
    heads_local = mla_w_uq.shape[2] // (QK_NOPE + QK_ROPE)
    n_heads = heads_local * N_DEV
    dr = lru_w_gate_a.shape[1] * lru_w_gate_a.shape[2]
    scale = (QK_NOPE + QK_ROPE) ** -0.5
    cx, cy, cc = _mesh_pos()
    core = jnp.reshape(cc, (1,)).astype(jnp.int32)
    my_slot = jnp.reshape(4 * cx + 2 * cy + cc, (1,)).astype(jnp.int32)

    def pad_cols(w, cols):
        return jnp.pad(w, ((0, 0), (0, cols - w.shape[1])))

    def pad_heads(w):
        k_all = w.shape[0]
        w3 = w.reshape(k_all, heads_local, QK_NOPE + QK_ROPE)
        return jnp.pad(w3, ((0, 0), (0, 0), (0, HEAD_W - QK_NOPE - QK_ROPE))).reshape(k_all, heads_local * HEAD_W)

    def unpad_heads(w):
        k_all = w.shape[0]
        return w.reshape(k_all, heads_local, HEAD_W)[:, :, :QK_NOPE + QK_ROPE].reshape(k_all, -1)

    small_rows = N_META + n_lru * 4 + 2 * n_lru
    small_pad = -(-small_rows // SUBLANES) * SUBLANES

    def pack_small(meta, conv_w, conv_b, lam):
        rows = jnp.concatenate([meta, conv_w.reshape(n_lru * 4, -1), conv_b, lam], axis=0)
        return jnp.pad(rows, ((0, small_pad - small_rows), (0, 0)))

    def unpack_small(p):
        o1 = N_META + n_lru * 4
        return (p[:N_META], p[N_META:o1].reshape(n_lru, 4, -1), p[o1:o1 + n_lru], p[o1 + n_lru:o1 + 2 * n_lru])

    (small_full,), small_done = _all_gather([pack_small(meta_tokens, lru_conv_w, lru_conv_b, lru_lambda)], name="ag_small")
    small_full = jnp.transpose(small_full, (1, 0, 2)).reshape(small_pad, -1)
    meta_full, conv_w_full, conv_b_full, lam_full = unpack_small(small_full)

    def wire(w):
        return (w + small_done).astype(BF16)

    mla_shards, lru_shards, ffn_shards = [], [], []
    for j in range(n_mla):
        mla_shards.append([wire(pad_cols(mla_w_in[j], w_in_cols)), wire(pad_heads(mla_w_uq[j])), wire(mla_w_ukv[j]),
                           wire(mla_w_o[j])])
    for j in range(n_lru):
        lru_shards.append([wire(lru_w_in[j]), wire(lru_w_o[j])])
    for layer in range(depth):
        ffn_shards.append([wire(ffn_w_gu[layer]), wire(ffn_w_down[layer])])

    n_sub = 2 * depth
    groups = []
    for layer in range(depth):
        groups += [mla_shards[layer // 2] if layer % 2 == 0 else lru_shards[layer // 2], ffn_shards[layer]]
    slot_idx = 4 * cx + 2 * cy + cc
    ag_own = []
    for gi, shards in enumerate(groups):
        lands = [lax.dynamic_update_slice(lax.empty((N_DEV,) + s.shape, s.dtype), s[None], (slot_idx, 0, 0)) for s in shards]
        ag_own.append(_exchange_start(shards, lands, _ag_plan_own, 4, name=f"ag{gi}_start"))
    ag_pass = [None] * n_sub
    weights = [None] * n_sub

    def ag_landed(gi, after):
        _, lands = _exchange_wait(ag_own[gi], _ag_plan_own, after, name=f"ag{gi}_wait")
        ag_pass[gi] = _exchange_start([], lands, _ag_plan_pass, 3, name=f"ag{gi}_pass")
        return ag_pass[gi][4][0, 0]

    def ag_done(gi, after):
        _, weights[gi] = _exchange_wait(ag_pass[gi], _ag_plan_pass, after, name=f"ag{gi}_pass_wait")

    cos, sin = _rope_tables(t_all)
    zeros_tail = jnp.zeros((t_all - n_real, d), F32)
    started = ag_own[0][4][0, 0]
    for st in ag_own[1:]:
        started = started + st[4][0, 0]
    h = jnp.concatenate([meta_full + started, x[0], zeros_tail], axis=0)
    target = jnp.concatenate([jnp.zeros((N_META, d), F32), loss_target[0], zeros_tail], axis=0)

    attn_kw = dict(tm=tm, tq=tq, n_heads=n_heads, scale=scale, n_real=n_real)

    def lru_small(j):
        return (conv_w_full[j], conv_b_full[j][None, :], lru_w_gate_a[j].astype(BF16), lru_b_gate_a[j].reshape(1, dr),
                lru_w_gate_x[j].astype(BF16), lru_b_gate_x[j].reshape(1, dr), lam_full[j][None, :])

    def before_sublayer(k, act):
        tok = ag_landed(k, act) if k <= 1 else 0.0
        ag_done(k, act)
        if 1 <= k < n_sub - 1:
            tok = tok + ag_landed(k + 1, act)
        return tok

    saved = []
    for layer in range(depth):
        j = layer // 2
        g_mix = norm_mix[layer][None, :] + before_sublayer(2 * layer, h)
        if layer % 2 == 0:
            h_mid, mix_saved = _mla_layer_fwd(layer, h, g_mix, weights[2 * layer], mla_q_norm[j][None, :],
                                              mla_kv_norm[j][None, :], cos, sin, **attn_kw)
        else:
            h_mid, mix_saved = _lru_layer_fwd(layer, h, g_mix, weights[2 * layer], lru_small(j), tm=tm)
        g_ffn = norm_ffn[layer][None, :] + before_sublayer(2 * layer + 1, h_mid)
        h_out, ffn_saved = _ffn_layer_fwd(layer, h_mid, g_ffn, weights[2 * layer + 1], tm=tm)
        saved.append((h, h_mid, mix_saved, ffn_saved))
        h = h_out

    loss_part, dh, dh_b, dg_final = _loss_head(h, target, norm_final[None, :], name="loss_head", tm=tm, n_real=n_real)
    loss = lax.psum(loss_part[0, 0], ("x", "y", "c"))

    rs_sib, rs_chip, reduced = [None] * n_sub, [None] * n_sub, [None] * n_sub
    chip_idx = jnp.reshape(2 * cx + cy, (1,)).astype(jnp.int32)

    def rs_begin(k, grads):
        lands = [lax.empty((4,) + g.shape[1:], g.dtype) for g in grads]
        rs_sib[k] = _exchange_start(grads, lands, _rs_plan_sibling, 4, name=f"rs{k}_start")
        return rs_sib[k][4][0, 0]

    def rs_middle(k, after):
        grads, landed = _exchange_wait(rs_sib[k], _rs_plan_sibling, after, name=f"rs{k}_wait")
        parts = [_pair_add(g, l, core, name=f"rs{k}_add{a}", tr=_adam_row_tile(g.shape[1], g.shape[2], 4 * 1024 * 1024))
                 for a, (g, l) in enumerate(zip(grads, landed))]
        lands = [lax.empty((3,) + p.shape[1:], p.dtype) for p in parts]
        rs_chip[k] = _exchange_start(parts, lands, _rs_plan_chips, 3, name=f"rs{k}_chips")
        return rs_chip[k][4][0, 0]

    def rs_end(k, after):
        reduced[k] = _exchange_wait(rs_chip[k], _rs_plan_chips, after, name=f"rs{k}_chips_wait")

    d_norm_mix, d_norm_ffn = [None] * depth, [None] * depth
    d_qn, d_kvn = [None] * n_mla, [None] * n_mla
    d_small = {k: [None] * n_lru for k in ("cw", "cb", "wga", "bga", "wgx", "bgx", "lam")}
    tok, waiting = 0.0, None
    for layer in reversed(range(depth)):
        j = layer // 2
        h_in, h_mid, mix_saved, ffn_saved = saved[layer]
        dh, dh_b, d_norm_ffn[layer], ffn_g = _ffn_layer_bwd(layer, dh, dh_b, h_mid, ffn_saved, norm_ffn[layer][None, :] + tok,
                                                            weights[2 * layer + 1], tm=tm)
        tok = rs_begin(2 * layer + 1, ffn_g)
        if waiting is not None:
            tok = tok + rs_middle(waiting, dh)
        waiting = 2 * layer + 1
        if layer == 0:
            tok = tok + rs_middle(waiting, dh)
            waiting = None
        g_mix = norm_mix[layer][None, :] + tok
        if layer % 2 == 0:
            dh, dh_b, d_norm_mix[layer], d_qn[j], d_kvn[j], mix_g = _mla_layer_bwd(
                layer, dh, dh_b, h_in, mix_saved, g_mix, weights[2 * layer], mla_q_norm[j][None, :], mla_kv_norm[j][None, :],
                cos, sin, **attn_kw)
        else:
            dh, dh_b, d_norm_mix[layer], dsmall, mix_g = _lru_layer_bwd(layer, dh, dh_b, h_in, mix_saved, g_mix,
                                                                        weights[2 * layer], lru_small(j), tm=tm)
            for key, val in zip(("cw", "cb", "wga", "bga", "wgx", "bgx", "lam"), dsmall):
                d_small[key][j] = val
        tok = rs_begin(2 * layer, mix_g)
        if waiting is not None:
            tok = tok + rs_middle(waiting, dh)
        waiting = 2 * layer
    rs_middle(waiting, dh)

    grad_x = dh[N_META:n_real][None]

    d_meta = dh[:N_META]
    small_grad = pack_small(d_meta, jnp.stack(d_small["cw"], axis=0), jnp.concatenate(d_small["cb"], axis=0),
                            jnp.concatenate(d_small["lam"], axis=0))
    rep_grads = [
        jnp.concatenate(d_norm_mix, axis=0), jnp.concatenate(d_norm_ffn, axis=0), dg_final,
        jnp.concatenate(d_qn, axis=0), jnp.concatenate(d_kvn, axis=0),
        jnp.stack(d_small["wga"], axis=0).reshape(-1, LANES), jnp.concatenate(d_small["bga"], axis=0),
        jnp.stack(d_small["wgx"], axis=0).reshape(-1, LANES), jnp.concatenate(d_small["bgx"], axis=0),
    ]
    small_srcs = [small_grad] + [jnp.pad(g, ((0, -g.shape[0] % SUBLANES), (0, 0))) for g in rep_grads]
    small_lands = [lax.dynamic_update_slice(lax.empty((N_DEV,) + s.shape, s.dtype), s[None], (slot_idx, 0, 0))
                   for s in small_srcs]
    small_own = _exchange_start(small_srcs, small_lands, _ag_plan_own, 4, name="ag_grads_start")

    res = {}

    def adam_sharded(nm, k, a, idx, n_layers, w, m, v):
        parts, landed = reduced[k]
        r_all, c_all = landed[a].shape[1], landed[a].shape[2]
        res[nm] = _adamw(w.reshape(r_all, c_all), m.reshape(r_all, c_all), v.reshape(r_all, c_all), landed[a], (0, 1, 2),
                         name=f"adamw_{nm}{idx}", tr=_adam_row_tile(r_all, c_all, 2 * 1024 * 1024), own=(parts[a], chip_idx),
                         stack=(idx, n_layers, res.get(nm)))

    after = small_own[4]
    for k in reversed(range(n_sub)):
        rs_end(k, after)
        layer, j = k // 2, k // 4
        if k % 2 == 1:
            adam_sharded("ffn_w_gu", k, 0, layer, depth, ffn_w_gu[layer], m_ffn_w_gu[layer], v_ffn_w_gu[layer])
            adam_sharded("ffn_w_down", k, 1, layer, depth, ffn_w_down[layer], m_ffn_w_down[layer], v_ffn_w_down[layer])
            after = res["ffn_w_down"][0]
        elif layer % 2 == 0:
            adam_sharded("mla_w_in", k, 0, j, n_mla, pad_cols(mla_w_in[j], w_in_cols), pad_cols(m_mla_w_in[j], w_in_cols),
                         pad_cols(v_mla_w_in[j], w_in_cols))
            adam_sharded("mla_w_uq", k, 1, j, n_mla, pad_heads(mla_w_uq[j]), pad_heads(m_mla_w_uq[j]), pad_heads(v_mla_w_uq[j]))
            adam_sharded("mla_w_ukv", k, 2, j, n_mla, mla_w_ukv[j], m_mla_w_ukv[j], v_mla_w_ukv[j])
            adam_sharded("mla_w_o", k, 3, j, n_mla, mla_w_o[j], m_mla_w_o[j], v_mla_w_o[j])
            after = res["mla_w_o"][0]
        else:
            adam_sharded("lru_w_in", k, 0, j, n_lru, lru_w_in[j], m_lru_w_in[j], v_lru_w_in[j])
            adam_sharded("lru_w_o", k, 1, j, n_lru, lru_w_o[j], m_lru_w_o[j], v_lru_w_o[j])
            after = res["lru_w_o"][0]
    res["mla_w_in"] = [t[:, :, :lq + lkv + QK_ROPE] for t in res["mla_w_in"]]
    res["mla_w_uq"] = [t.reshape(n_mla, lq, heads_local, HEAD_W)[:, :, :, :QK_NOPE + QK_ROPE].reshape(n_mla, lq, -1)
                       for t in res["mla_w_uq"]]

    _, small_lands = _exchange_wait(small_own, _ag_plan_own, after, name="ag_grads_wait")
    small_pass = _exchange_start([], small_lands, _ag_plan_pass, 3, name="ag_grads_pass")
    _, all_small = _exchange_wait(small_pass, _ag_plan_pass, after, name="ag_grads_pass_wait")
    slot_order = tuple(range(N_DEV))

    def adam_rep(terms, w, m, v, tag):
        r_pad, c_all = terms.shape[1], terms.shape[2]

        def prep(t):
            t2 = t.reshape(-1, c_all)
            return jnp.pad(t2, ((0, r_pad - t2.shape[0]), (0, 0)))

        outs = _adamw(prep(w), prep(m), prep(v), terms, slot_order, name=f"adamw_{tag}", tr=_adam_row_tile(r_pad, c_all))
        n_rows = w.size // c_all
        return [o[:n_rows].reshape(w.shape) for o in outs]

    small_w = pack_small(meta_tokens, lru_conv_w, lru_conv_b, lru_lambda)
    small_m = pack_small(m_meta_tokens, m_lru_conv_w, m_lru_conv_b, m_lru_lambda)
    small_v = pack_small(v_meta_tokens, v_lru_conv_w, v_lru_conv_b, v_lru_lambda)
    small_out = _adamw(small_w, small_m, small_v, all_small[0], slot_order, name="adamw_small", tr=small_pad, col_block=my_slot)
    small_out = [unpack_small(o) for o in small_out]
    for idx, key in enumerate(("meta_tokens", "lru_conv_w", "lru_conv_b", "lru_lambda")):
        res[key] = [small_out[k][idx] for k in range(4)]

    res["norm_mix"] = adam_rep(all_small[1], norm_mix, m_norm_mix, v_norm_mix, "norm_mix")
    res["norm_ffn"] = adam_rep(all_small[2], norm_ffn, m_norm_ffn, v_norm_ffn, "norm_ffn")
    res["norm_final"] = adam_rep(all_small[3], norm_final, m_norm_final, v_norm_final, "norm_final")
    res["mla_q_norm"] = adam_rep(all_small[4], mla_q_norm, m_mla_q_norm, v_mla_q_norm, "mla_q_norm")
    res["mla_kv_norm"] = adam_rep(all_small[5], mla_kv_norm, m_mla_kv_norm, v_mla_kv_norm, "mla_kv_norm")
    res["lru_w_gate_a"] = adam_rep(all_small[6], lru_w_gate_a, m_lru_w_gate_a, v_lru_w_gate_a, "lru_w_gate_a")
    res["lru_b_gate_a"] = adam_rep(all_small[7], lru_b_gate_a, m_lru_b_gate_a, v_lru_b_gate_a, "lru_b_gate_a")
    res["lru_w_gate_x"] = adam_rep(all_small[8], lru_w_gate_x, m_lru_w_gate_x, v_lru_w_gate_x, "lru_w_gate_x")
    res["lru_b_gate_x"] = adam_rep(all_small[9], lru_b_gate_x, m_lru_b_gate_x, v_lru_b_gate_x, "lru_b_gate_x")

    names = ["meta_tokens", "norm_mix", "norm_ffn", "norm_final", "mla_w_in", "mla_q_norm", "mla_kv_norm", "mla_w_uq",
             "mla_w_ukv", "mla_w_o", "lru_w_in", "lru_conv_w", "lru_conv_b", "lru_w_gate_a", "lru_b_gate_a", "lru_w_gate_x",
             "lru_b_gate_x", "lru_lambda", "lru_w_o", "ffn_w_gu", "ffn_w_down"]
    shapes = dict(meta_tokens=meta_tokens, norm_mix=norm_mix, norm_ffn=norm_ffn, norm_final=norm_final, mla_w_in=mla_w_in,
                  mla_q_norm=mla_q_norm, mla_kv_norm=mla_kv_norm, mla_w_uq=mla_w_uq, mla_w_ukv=mla_w_ukv, mla_w_o=mla_w_o,
                  lru_w_in=lru_w_in, lru_conv_w=lru_conv_w, lru_conv_b=lru_conv_b, lru_w_gate_a=lru_w_gate_a,
                  lru_b_gate_a=lru_b_gate_a, lru_w_gate_x=lru_w_gate_x, lru_b_gate_x=lru_b_gate_x, lru_lambda=lru_lambda,
                  lru_w_o=lru_w_o, ffn_w_gu=ffn_w_gu, ffn_w_down=ffn_w_down)
    outs = [loss, grad_x]
    for k in range(4):
        outs += [res[nm][k].reshape(shapes[nm].shape) for nm in names]
    return tuple(outs)
```

```python
import math

import jax
import jax.numpy as jnp
from jax import lax
from jax.experimental import pallas as pl
from jax.experimental.pallas import tpu as pltpu

F32 = jnp.float32
BF16 = jnp.bfloat16
MESH = pl.DeviceIdType.MESH

N_META = 16
CHUNK = 64
QK_NOPE = 128
QK_ROPE = 64
V_HEAD = 128
HEAD_W = 256
ROPE_THETA = 10000.0
LRU_C = 8.0
RMS_EPS = 1e-6
NEG_BIG = -1e30
ADAM_LR, ADAM_B1, ADAM_B2, ADAM_EPS, ADAM_WD, ADAM_STEP = 0.001, 0.9, 0.999, 1e-08, 0.01, 10

LANES = 128
SUBLANES = 8
VMEM_LIMIT_BYTES = 52 * 1024 * 1024
N_DEV = 8

_NT = (((1,), (1,)), ((), ()))
_TN = (((0,), (0,)), ((), ()))
_DIVS = (2048, 1024, 512, 256, 128)
_ROW_TILES = (1408, 1024, 512, 256, 128)


def _params(dims):
    return pltpu.CompilerParams(dimension_semantics=dims, vmem_limit_bytes=VMEM_LIMIT_BYTES)


def _pick(n, candidates):
    for c in candidates:
        if c <= n and n % c == 0:
            return c
    return n


def _sigmoid(z):
    return 0.5 + 0.5 * jnp.tanh(0.5 * z)


def _gelu(x):
    c = math.sqrt(2.0 / math.pi)
    return 0.5 * x * (1.0 + jnp.tanh(c * (x + 0.044715 * x * x * x)))


def _gelu_grad(x):
    c = math.sqrt(2.0 / math.pi)
    th = jnp.tanh(c * (x + 0.044715 * x * x * x))
    return 0.5 * (1.0 + th) + 0.5 * x * (1.0 - th * th) * c * (1.0 + 3.0 * 0.044715 * x * x)


def _neg_expm1(x):
    poly = -x * (1.0 + x * (1.0 / 2.0) * (1.0 + x * (1.0 / 3.0) * (1.0 + x * (1.0 / 4.0) * (
        1.0 + x * (1.0 / 5.0) * (1.0 + x * (1.0 / 6.0) * (1.0 + x * (1.0 / 7.0)))))))
    return jnp.where(x > -0.25, poly, 1.0 - jnp.exp(x))


def _softplus_neg(lam):
    e = jnp.exp(-jnp.abs(lam))
    log1p = jnp.where(e > 1e-4, jnp.log(1.0 + e), e * (1.0 - e * (0.5 - e * (1.0 / 3.0))))
    return jnp.maximum(-lam, 0.0) + log1p


def _rot_half(x):
    lane = lax.broadcasted_iota(jnp.int32, x.shape, 1)
    first = (lane % QK_ROPE) < (QK_ROPE // 2)
    return jnp.where(first, -pltpu.roll(x, LANES - QK_ROPE // 2, 1), pltpu.roll(x, QK_ROPE // 2, 1))


def _rope(x, cos, sin):
    return x * cos + _rot_half(x) * sin


def _unrope(g, cos, sin):
    return g * cos - _rot_half(g) * sin


def _grid_order(rows_outer):
    if not rows_outer:
        return lambda f: f
    return lambda f: (lambda i, j, k: f(j, i, k))


def _mm_nn(a, b, *, name, out_dtype, tm, tn, tk, b_blocked=False, res=None, epilogue=None, extras=(), rows_outer=False):
    m_all, k_all = a.shape
    om = _grid_order(rows_outer)
    if b_blocked:
        g_all, kb, nb = b.shape
        n_all = g_all * nb
        assert nb % tn == 0
        r = nb // tn
        b_spec = pl.BlockSpec((None, tk, tn), om(lambda j, i, k: (j // r, k, j % r)))
    else:
        kb, n_all = b.shape
        b_spec = pl.BlockSpec((tk, tn), om(lambda j, i, k: (k, j)))
    assert kb == k_all and m_all % tm == 0 and n_all % tn == 0 and k_all % tk == 0
    nm, nn, nk = m_all // tm, n_all // tn, k_all // tk
    in_specs = [pl.BlockSpec((tm, tk), om(lambda j, i, k: (i, k))), b_spec]
    operands = [a, b]
    has_res = res is not None
    if has_res:
        in_specs.append(pl.BlockSpec((tm, tn), om(lambda j, i, k: (i, j))))
        operands.append(res)
    for e in extras:
        in_specs.append(pl.BlockSpec((tm, e.shape[1]), om(lambda j, i, k: (i, 0))))
        operands.append(e)
    n_ex = len(extras)

    def body(*refs):
        a_ref, b_ref = refs[0], refs[1]
        pos = 2
        res_ref = None
        if has_res:
            res_ref = refs[pos]
            pos += 1
        ex_refs = refs[pos:pos + n_ex]
        pos += n_ex
        o_ref = refs[pos]
        acc_ref = refs[pos + 1] if nk > 1 else None

        def finish(acc):
            if has_res:
                acc = acc + res_ref[...]
            if epilogue is not None:
                acc = epilogue(acc, *ex_refs)
            o_ref[...] = acc.astype(o_ref.dtype)

        prod = jnp.dot(a_ref[...], b_ref[...], preferred_element_type=F32)
        if nk == 1:
            finish(prod)
        else:
            k = pl.program_id(2)

            @pl.when(k == 0)
            def _():
                acc_ref[...] = prod

            @pl.when(k > 0)
            def _():
                acc_ref[...] += prod

            @pl.when(k == nk - 1)
            def _():
                finish(acc_ref[...])

    return pl.pallas_call(
        body, name=name, grid=(nm, nn, nk) if rows_outer else (nn, nm, nk), in_specs=in_specs,
        out_specs=pl.BlockSpec((tm, tn), om(lambda j, i, k: (i, j))),
        out_shape=jax.ShapeDtypeStruct((m_all, n_all), out_dtype),
        scratch_shapes=[pltpu.VMEM((tm, tn), F32)] if nk > 1 else [],
        compiler_params=_params(("parallel", "parallel", "arbitrary")),
    )(*operands)


def _mm_nt(a, b, *, name, out_dtype, tm, tn, tk, b_blocked=False):
    if a.ndim == 3:
        n_planes, m_all, kp = a.shape
        k_all = n_planes * kp
    else:
        n_planes, (m_all, k_all) = 0, a.shape
    if b_blocked and tk == k_all and b.shape[0] > 1:
        g_all, n_all, nb = b.shape
        assert g_all * nb == k_all and m_all % tm == 0 and n_all % tn == 0
        per_plane = kp // nb if n_planes else 0

        def whole_body(a_ref, b_ref, o_ref):
            acc = None
            for g in range(g_all):
                a_g = a_ref[g // per_plane, :, (g % per_plane) * nb:(g % per_plane + 1) * nb] if n_planes else a_ref[:, g * nb:(g + 1) * nb]
                prod = lax.dot_general(a_g, b_ref[g], _NT, preferred_element_type=F32)
                acc = prod if acc is None else acc + prod
            o_ref[...] = acc.astype(o_ref.dtype)

        a_whole = (pl.BlockSpec((n_planes, tm, kp), lambda j, i: (0, i, 0)) if n_planes
                   else pl.BlockSpec((tm, k_all), lambda j, i: (i, 0)))
        return pl.pallas_call(
            whole_body, name=name, grid=(n_all // tn, m_all // tm),
            in_specs=[a_whole, pl.BlockSpec((g_all, tn, nb), lambda j, i: (0, j, 0))],
            out_specs=pl.BlockSpec((tm, tn), lambda j, i: (i, j)),
            out_shape=jax.ShapeDtypeStruct((m_all, n_all), out_dtype),
            compiler_params=_params(("parallel", "parallel")),
        )(a, b)
    if n_planes:
        assert kp % tk == 0
        rp = kp // tk
        a_spec = pl.BlockSpec((None, tm, tk), lambda j, i, k: (k // rp, i, k % rp))
    else:
        a_spec = pl.BlockSpec((tm, tk), lambda j, i, k: (i, k))
    if b_blocked:
        g_all, n_all, nb = b.shape
        assert g_all * nb == k_all and nb % tk == 0
        r = nb // tk
        b_spec = pl.BlockSpec((None, tn, tk), lambda j, i, k: (k // r, j, k % r))
    else:
        n_all, kb = b.shape
        assert kb == k_all
        b_spec = pl.BlockSpec((tn, tk), lambda j, i, k: (j, k))
    assert m_all % tm == 0 and n_all % tn == 0 and k_all % tk == 0
    nm, nn, nk = m_all // tm, n_all // tn, k_all // tk

    def body(a_ref, b_ref, o_ref, *scratch):
        prod = lax.dot_general(a_ref[...], b_ref[...], _NT, preferred_element_type=F32)
        if nk == 1:
            o_ref[...] = prod.astype(o_ref.dtype)
        else:
            acc_ref = scratch[0]
            k = pl.program_id(2)

            @pl.when(k == 0)
            def _():
                acc_ref[...] = prod

            @pl.when(k > 0)
            def _():
                acc_ref[...] += prod

            @pl.when(k == nk - 1)
            def _():
                o_ref[...] = acc_ref[...].astype(o_ref.dtype)

    return pl.pallas_call(
        body, name=name, grid=(nn, nm, nk),
        in_specs=[a_spec, b_spec],
        out_specs=pl.BlockSpec((tm, tn), lambda j, i, k: (i, j)),
        out_shape=jax.ShapeDtypeStruct((m_all, n_all), out_dtype),
        scratch_shapes=[pltpu.VMEM((tm, tn), F32)] if nk > 1 else [],
        compiler_params=_params(("parallel", "parallel", "arbitrary")),
    )(a, b)


def _mm_tn(a, b, *, name, out_dtype, tm, tn, tk, out_block=None, cols_outer=False):
    t_all, m_all = a.shape
    om = _grid_order(cols_outer)
    if b.ndim == 3:
        n_planes, tb, n_p = b.shape
        assert n_p % tn == 0
        rq = n_p // tn
        n_all = n_planes * n_p
        b_spec = pl.BlockSpec((None, tk, tn), om(lambda i, j, k: (j // rq, k, j % rq)))
    else:
        tb, n_all = b.shape
        b_spec = pl.BlockSpec((tk, tn), om(lambda i, j, k: (k, j)))
    assert tb == t_all and m_all % tm == 0 and n_all % tn == 0 and t_all % tk == 0
    nm, nn, nk = m_all // tm, n_all // tn, t_all // tk
    if out_block is None:
        out_shape = jax.ShapeDtypeStruct((m_all, n_all), out_dtype)
        out_spec = pl.BlockSpec((tm, tn), om(lambda i, j, k: (i, j)))
    else:
        assert out_block % tn == 0 and n_all % out_block == 0
        r = out_block // tn
        out_shape = jax.ShapeDtypeStruct((n_all // out_block, m_all, out_block), out_dtype)
        out_spec = pl.BlockSpec((None, tm, tn), om(lambda i, j, k: (j // r, i, j % r)))

    def body(a_ref, b_ref, o_ref, *scratch):
        prod = lax.dot_general(a_ref[...], b_ref[...], _TN, preferred_element_type=F32)
        if nk == 1:
            o_ref[...] = prod.astype(o_ref.dtype)
        else:
            acc_ref = scratch[0]
            k = pl.program_id(2)

            @pl.when(k == 0)
            def _():
                acc_ref[...] = prod

            @pl.when(k > 0)
            def _():
                acc_ref[...] += prod

            @pl.when(k == nk - 1)
            def _():
                o_ref[...] = acc_ref[...].astype(o_ref.dtype)

    return pl.pallas_call(
        body, name=name, grid=(nn, nm, nk) if cols_outer else (nm, nn, nk),
        in_specs=[pl.BlockSpec((tk, tm), om(lambda i, j, k: (k, i))), b_spec],
        out_specs=out_spec, out_shape=out_shape,
        scratch_shapes=[pltpu.VMEM((tm, tn), F32)] if nk > 1 else [],
        compiler_params=_params(("parallel", "parallel", "arbitrary")),
    )(a, b)


def _rmsnorm_fwd(x, g, *, name, tm):
    t_all, d = x.shape

    def body(x_ref, g_ref, o_ref):
        xv = x_ref[...]
        rstd = lax.rsqrt(jnp.mean(xv * xv, axis=-1, keepdims=True) + RMS_EPS)
        o_ref[...] = (xv * rstd * g_ref[...]).astype(o_ref.dtype)

    return pl.pallas_call(
        body, name=name, grid=(t_all // tm,),
        in_specs=[pl.BlockSpec((tm, d), lambda i: (i, 0)), pl.BlockSpec((1, d), lambda i: (0, 0))],
        out_specs=pl.BlockSpec((tm, d), lambda i: (i, 0)),
        out_shape=jax.ShapeDtypeStruct((t_all, d), BF16),
        compiler_params=_params(("parallel",)),
    )(x, g)


def _rms_bwd_math(dy, xv, g):
    rstd = lax.rsqrt(jnp.mean(xv * xv, axis=-1, keepdims=True) + RMS_EPS)
    xhat = xv * rstd
    dxh = dy * g
    dx = rstd * (dxh - xhat * jnp.mean(dxh * xhat, axis=-1, keepdims=True))
    return dx, jnp.sum(dy * xhat, axis=0, keepdims=True)


def _rmsnorm_bwd(dy, x, g, res, *, name, tm):
    t_all, d = x.shape

    def body(dy_ref, x_ref, g_ref, res_ref, dx_ref, dxb_ref, dg_ref):
        dx, dg = _rms_bwd_math(dy_ref[...], x_ref[...], g_ref[...])
        tot = res_ref[...] + dx
        dx_ref[...] = tot
        dxb_ref[...] = tot.astype(BF16)

        @pl.when(pl.program_id(0) == 0)
        def _():
            dg_ref[...] = dg

        @pl.when(pl.program_id(0) > 0)
        def _():
            dg_ref[...] += dg

    row = pl.BlockSpec((tm, d), lambda i: (i, 0))
    vec = pl.BlockSpec((1, d), lambda i: (0, 0))
    return pl.pallas_call(
        body, name=name, grid=(t_all // tm,),
        in_specs=[row, row, vec, row], out_specs=[row, row, vec],
        out_shape=[jax.ShapeDtypeStruct((t_all, d), F32), jax.ShapeDtypeStruct((t_all, d), BF16),
                   jax.ShapeDtypeStruct((1, d), F32)],
        compiler_params=_params(("arbitrary",)),
    )(dy, x, g, res)


def _loss_head(h, target, g, *, name, tm, n_real):
    t_all, d = h.shape

    def body(h_ref, t_ref, g_ref, loss_ref, dx_ref, dxb_ref, dg_ref):
        i = pl.program_id(0)
        xv = h_ref[...]
        gv = g_ref[...]
        rstd = lax.rsqrt(jnp.mean(xv * xv, axis=-1, keepdims=True) + RMS_EPS)
        y = xv * rstd * gv
        row = i * tm + lax.broadcasted_iota(jnp.int32, (tm, 1), 0)
        valid = (row >= N_META) & (row < n_real)
        err = jnp.where(valid, y - t_ref[...], 0.0)
        part = 0.5 * jnp.sum(jnp.mean(err * err, axis=-1, keepdims=True), axis=0, keepdims=True)
        dx, dg = _rms_bwd_math(err * (1.0 / d), xv, gv)
        dx_ref[...] = dx
        dxb_ref[...] = dx.astype(BF16)

        @pl.when(i == 0)
        def _():
            dg_ref[...] = dg
            loss_ref[...] = jnp.broadcast_to(part, loss_ref.shape)

        @pl.when(i > 0)
        def _():
            dg_ref[...] += dg
            loss_ref[...] += jnp.broadcast_to(part, loss_ref.shape)

    row = pl.BlockSpec((tm, d), lambda i: (i, 0))
    vec = pl.BlockSpec((1, d), lambda i: (0, 0))
    return pl.pallas_call(
        body, name=name, grid=(t_all // tm,),
        in_specs=[row, row, vec],
        out_specs=[pl.BlockSpec((1, LANES), lambda i: (0, 0)), row, row, vec],
        out_shape=[jax.ShapeDtypeStruct((1, LANES), F32), jax.ShapeDtypeStruct((t_all, d), F32),
                   jax.ShapeDtypeStruct((t_all, d), BF16), jax.ShapeDtypeStruct((1, d), F32)],
        compiler_params=_params(("arbitrary",)),
    )(h, target, g)


def _ffn_up(x, w_gu, *, name, tm):
    t_all, d = x.shape
    g_all, kb, nb = w_gu.shape
    half = g_all // 2
    f = half * nb
    assert kb == d and t_all % tm == 0

    def body(x_ref, wg_ref, wu_ref, gu_ref, act_ref):
        xv = x_ref[...]
        gv = jnp.dot(xv, wg_ref[...], preferred_element_type=F32)
        uv = jnp.dot(xv, wu_ref[...], preferred_element_type=F32)
        gu_ref[0] = gv.astype(gu_ref.dtype)
        gu_ref[1] = uv.astype(gu_ref.dtype)
        act_ref[...] = (gv * _sigmoid(gv) * uv).astype(act_ref.dtype)

    return pl.pallas_call(
        body, name=name, grid=(half, t_all // tm),
        in_specs=[pl.BlockSpec((tm, d), lambda j, i: (i, 0)), pl.BlockSpec((None, d, nb), lambda j, i: (j, 0, 0)),
                  pl.BlockSpec((None, d, nb), lambda j, i: (j + half, 0, 0))],
        out_specs=[pl.BlockSpec((2, tm, nb), lambda j, i: (0, i, j)), pl.BlockSpec((tm, nb), lambda j, i: (i, j))],
        out_shape=[jax.ShapeDtypeStruct((2, t_all, f), BF16), jax.ShapeDtypeStruct((t_all, f), BF16)],
        compiler_params=_params(("parallel", "parallel")),
    )(x, w_gu, w_gu)


def _ffn_dact(dy, w_down, gu, *, name, tm, tn):
    t_all, d = dy.shape
    f = w_down.shape[0]
    assert t_all % tm == 0 and f % tn == 0

    def body(dy_ref, w_ref, gu_ref, o_ref):
        dact = lax.dot_general(dy_ref[...], w_ref[...], _NT, preferred_element_type=F32)
        gv, uv = gu_ref[0].astype(F32), gu_ref[1].astype(F32)
        sg = _sigmoid(gv)
        o_ref[0] = (dact * uv * (sg * (1.0 + gv * (1.0 - sg)))).astype(o_ref.dtype)
        o_ref[1] = (dact * gv * sg).astype(o_ref.dtype)

    return pl.pallas_call(
        body, name=name, grid=(f // tn, t_all // tm),
        in_specs=[pl.BlockSpec((tm, d), lambda j, i: (i, 0)), pl.BlockSpec((tn, d), lambda j, i: (j, 0)),
                  pl.BlockSpec((2, tm, tn), lambda j, i: (0, i, j))],
        out_specs=pl.BlockSpec((2, tm, tn), lambda j, i: (0, i, j)),
        out_shape=jax.ShapeDtypeStruct((2, t_all, f), BF16),
        compiler_params=_params(("parallel", "parallel")),
    )(dy, w_down, gu)


def _mla_prep_fwd(proj, qn, kvn, cos, sin, *, name, tm, lq, lkv):
    t_all, w = proj.shape

    def body(p_ref, qn_ref, kvn_ref, cos_ref, sin_ref, cq_ref, ckv_ref, kr_ref):
        pv = p_ref[...]
        xq = pv[:, :lq]
        xkv = pv[:, lq:lq + lkv]
        cq_ref[...] = (xq * lax.rsqrt(jnp.mean(xq * xq, axis=-1, keepdims=True) + RMS_EPS) * qn_ref[...]).astype(BF16)
        ckv_ref[...] = (xkv * lax.rsqrt(jnp.mean(xkv * xkv, axis=-1, keepdims=True) + RMS_EPS) * kvn_ref[...]).astype(BF16)
        kr_ref[...] = _rope(pv[:, lq + lkv:], cos_ref[...], sin_ref[...]).astype(BF16)

    def row(width):
        return pl.BlockSpec((tm, width), lambda i: (i, 0))

    def vec(width):
        return pl.BlockSpec((1, width), lambda i: (0, 0))

    return pl.pallas_call(
        body, name=name, grid=(t_all // tm,),
        in_specs=[row(w), vec(lq), vec(lkv), row(LANES), row(LANES)],
        out_specs=[row(lq), row(lkv), row(LANES)],
        out_shape=[jax.ShapeDtypeStruct((t_all, lq), BF16), jax.ShapeDtypeStruct((t_all, lkv), BF16),
                   jax.ShapeDtypeStruct((t_all, LANES), BF16)],
        compiler_params=_params(("parallel",)),
    )(proj, qn, kvn, cos, sin)


def _mla_prep_bwd(dcq, dckv, dkr_h, proj, qn, kvn, cos, sin, *, name, tm, lq, lkv):
    t_all, w = proj.shape
    n_heads = dkr_h.shape[0]

    def body(dcq_ref, dckv_ref, dkr_ref, p_ref, qn_ref, kvn_ref, cos_ref, sin_ref, dp_ref, dqn_ref, dkvn_ref):
        pv = p_ref[...]
        dxq, dqn = _rms_bwd_math(dcq_ref[...], pv[:, :lq], qn_ref[...])
        dxkv, dkvn = _rms_bwd_math(dckv_ref[...], pv[:, lq:lq + lkv], kvn_ref[...])
        dkr = dkr_ref[0]
        for hh in range(1, n_heads):
            dkr = dkr + dkr_ref[hh]
        dkr = _unrope(dkr, cos_ref[...], sin_ref[...])
        dp_ref[...] = jnp.concatenate([dxq, dxkv, dkr], axis=1).astype(BF16)

        @pl.when(pl.program_id(0) == 0)
        def _():
            dqn_ref[...] = dqn
            dkvn_ref[...] = dkvn

        @pl.when(pl.program_id(0) > 0)
        def _():
            dqn_ref[...] += dqn
            dkvn_ref[...] += dkvn

    def row(width):
        return pl.BlockSpec((tm, width), lambda i: (i, 0))

    def vec(width):
        return pl.BlockSpec((1, width), lambda i: (0, 0))

    return pl.pallas_call(
        body, name=name, grid=(t_all // tm,),
        in_specs=[row(lq), row(lkv), pl.BlockSpec((n_heads, tm, LANES), lambda i: (0, i, 0)), row(w),
                  vec(lq), vec(lkv), row(LANES), row(LANES)],
        out_specs=[row(w), vec(lq), vec(lkv)],
        out_shape=[jax.ShapeDtypeStruct((t_all, w), BF16), jax.ShapeDtypeStruct((1, lq), F32),
                   jax.ShapeDtypeStruct((1, lkv), F32)],
        compiler_params=_params(("arbitrary",)),
    )(dcq, dckv, dkr_h, proj, qn, kvn, cos, sin)


def _rope_q_epilogue(acc, cos_ref, sin_ref):
    parts = []
    for g in range(acc.shape[1] // LANES):
        blk = acc[:, g * LANES:(g + 1) * LANES]
        parts.append(_rope(blk, cos_ref[...], sin_ref[...]) if g % 2 == 1 else blk)
    return jnp.concatenate(parts, axis=1)


def _chunk_causal(rows, cols, row0=0):
    r = row0 + lax.broadcasted_iota(jnp.int32, (rows, cols), 0)
    c = lax.broadcasted_iota(jnp.int32, (rows, cols), 1)
    return (c >> 6) <= (r >> 6)


def _meta_keys(rows, cols):
    return lax.broadcasted_iota(jnp.int32, (rows, cols), 1) < N_META


def _attn_fwd(q, kv, kr, *, name, n_heads, tq, n_real, scale):
    t_all = q.shape[0]
    nq = (n_real - N_META) // tq
    assert N_META + nq * tq == n_real and tq % CHUNK == 0 and t_all >= LANES
    n_pad = t_all - n_real
    sub = tq // 2 if (tq // 2) % CHUNK == 0 else tq

    def body(q_ref, kv_ref, kr_ref, o_ref, lse_ref, k_scr, m_scr, l_scr, acc_scr):
        k_scr[:, :QK_NOPE] = kv_ref[:, :QK_NOPE]
        k_scr[:, QK_NOPE:] = kr_ref[...]
        if n_pad:
            o_ref[pl.ds(n_real, n_pad), :] = jnp.zeros((n_pad, V_HEAD), o_ref.dtype)
            lse_ref[pl.ds(n_real, n_pad), :] = jnp.zeros((n_pad, LANES), F32)

        def scores(qt, c0, width):
            return lax.dot_general(qt, k_scr[pl.ds(c0, width), :], _NT, preferred_element_type=F32) * scale

        def values(c0, width):
            return kv_ref[pl.ds(c0, width), QK_NOPE:]

        s = jnp.where(_meta_keys(LANES, LANES), scores(q_ref[pl.ds(0, LANES), :], 0, LANES), NEG_BIG)
        m = jnp.max(s, axis=-1, keepdims=True)
        p = jnp.exp(s - m)
        l = jnp.sum(p, axis=-1, keepdims=True)
        o_meta = jnp.dot(p.astype(BF16), values(0, LANES), preferred_element_type=F32) / l
        o_ref[pl.ds(0, N_META), :] = o_meta[:N_META].astype(o_ref.dtype)
        lse_ref[pl.ds(0, N_META), :] = jnp.broadcast_to((m + jnp.log(l))[:N_META], (N_META, LANES))

        parts = [(u * sub, sub) for u in range(tq // sub)]

        def accumulate(u0, s, vals):
            rows = pl.ds(u0, s.shape[0])
            m_prev = m_scr[rows, :]
            m_new = jnp.maximum(m_prev, jnp.max(s, axis=-1, keepdims=True))
            alpha = jnp.exp(m_prev - m_new)
            p = jnp.exp(s - m_new)
            l_scr[rows, :] = alpha * l_scr[rows, :] + jnp.sum(p, axis=-1, keepdims=True)
            acc_scr[rows, :] = alpha * acc_scr[rows, :] + jnp.dot(p.astype(BF16), vals, preferred_element_type=F32)
            m_scr[rows, :] = m_new

        def q_tile(i, carry):
            r0 = pl.multiple_of(N_META + i * tq, N_META)
            qts = [q_ref[pl.ds(r0 + u0, rows), :] for u0, rows in parts]
            m_scr[...] = jnp.full(m_scr.shape, NEG_BIG, F32)
            l_scr[...] = jnp.zeros(l_scr.shape, F32)
            acc_scr[...] = jnp.zeros(acc_scr.shape, F32)

            def full_blocks(j, width):
                c0 = pl.multiple_of(N_META + j * tq, N_META)
                for (u0, _), qt in zip(parts, qts):
                    accumulate(u0, scores(qt, c0, width), values(c0, width))

            def two_blocks(jj, c):
                full_blocks(2 * jj, 2 * tq)
                return c

            lax.fori_loop(0, i // 2, two_blocks, 0)

            @pl.when(i % 2 == 1)
            def _():
                full_blocks(i - 1, tq)

            for (u0, rows), qt in zip(parts, qts):
                width = u0 + rows
                s = jnp.concatenate([jnp.where(_meta_keys(rows, LANES), scores(qt, 0, LANES), NEG_BIG),
                                     jnp.where(_chunk_causal(rows, width, u0), scores(qt, r0, width), NEG_BIG)], axis=1)
                accumulate(u0, s, jnp.concatenate([values(0, LANES), values(r0, width)], axis=0))
            o_ref[pl.ds(r0, tq), :] = (acc_scr[...] / l_scr[...]).astype(o_ref.dtype)
            lse_ref[pl.ds(r0, tq), :] = jnp.broadcast_to(m_scr[...] + jnp.log(l_scr[...]), (tq, LANES))
            return carry

        lax.fori_loop(0, nq, q_tile, 0)

    def head(width):
        return pl.BlockSpec((t_all, width), lambda h: (0, h))

    return pl.pallas_call(
        body, name=name, grid=(n_heads,),
        in_specs=[head(HEAD_W), head(HEAD_W), pl.BlockSpec((t_all, LANES), lambda h: (0, 0))],
        out_specs=[head(V_HEAD), pl.BlockSpec((None, t_all, LANES), lambda h: (h, 0, 0))],
        out_shape=[jax.ShapeDtypeStruct((t_all, n_heads * V_HEAD), BF16),
                   jax.ShapeDtypeStruct((n_heads, t_all, LANES), F32)],
        scratch_shapes=[pltpu.VMEM((t_all, HEAD_W), BF16), pltpu.VMEM((tq, 1), F32), pltpu.VMEM((tq, 1), F32),
                        pltpu.VMEM((tq, V_HEAD), F32)],
        compiler_params=_params(("parallel",)),
    )(q, kv, kr)


def _attn_bwd(q, kv, kr, o, lse, do, cos, sin, *, name, n_heads, tq, n_real, scale):
    t_all = q.shape[0]
    nq = (n_real - N_META) // tq
    assert N_META + nq * tq == n_real and tq % CHUNK == 0 and t_all >= LANES
    n_pad = t_all - n_real

    def body(q_ref, kv_ref, kr_ref, o_ref, lse_ref, do_ref, cos_ref, sin_ref, dq_ref, dkv_ref, dkr_ref,
             k_scr, dk_scr, dv_scr, dq_scr):
        k_scr[:, :QK_NOPE] = kv_ref[:, :QK_NOPE]
        k_scr[:, QK_NOPE:] = kr_ref[...]
        dk_scr[...] = jnp.zeros(dk_scr.shape, F32)
        dv_scr[...] = jnp.zeros(dv_scr.shape, F32)
        if n_pad:
            dq_ref[pl.ds(n_real, n_pad), :] = jnp.zeros((n_pad, HEAD_W), dq_ref.dtype)

        def blocks(qt, dot, lse_t, delta, segments):
            kb = jnp.concatenate([k_scr[pl.ds(c0, w), :] for c0, w, _ in segments], axis=0)
            vb = jnp.concatenate([kv_ref[pl.ds(c0, w), QK_NOPE:] for c0, w, _ in segments], axis=0)
            s = lax.dot_general(qt, kb, _NT, preferred_element_type=F32) * scale
            p = jnp.exp(s - lse_t)
            if any(m is not None for _, _, m in segments):
                rows = qt.shape[0]
                mask = jnp.concatenate([jnp.ones((rows, w), jnp.bool_) if m is None else m for _, w, m in segments], axis=1)
                p = jnp.where(mask, p, 0.0)
            dp = lax.dot_general(dot, vb, _NT, preferred_element_type=F32)
            ds = (p * (dp - delta) * scale).astype(BF16)
            dv = lax.dot_general(p.astype(BF16), dot, _TN, preferred_element_type=F32)
            dk = lax.dot_general(ds, qt, _TN, preferred_element_type=F32)
            at = 0
            for c0, w, _ in segments:
                dv_scr[pl.ds(c0, w), :] += dv[at:at + w]
                dk_scr[pl.ds(c0, w), :] += dk[at:at + w]
                at += w
            return jnp.dot(ds, kb, preferred_element_type=F32)

        def block(qt, dot, lse_t, delta, c0, width, mask):
            return blocks(qt, dot, lse_t, delta, [(c0, width, mask)])

        def write_dq(r0, rows, dq):
            cs, sn = cos_ref[pl.ds(r0, rows), :], sin_ref[pl.ds(r0, rows), :]
            dq_ref[pl.ds(r0, rows), :] = jnp.concatenate(
                [dq[:, :QK_NOPE], _unrope(dq[:, QK_NOPE:], cs, sn)], axis=1).astype(dq_ref.dtype)

        rows_m = lax.broadcasted_iota(jnp.int32, (LANES, LANES), 0) < N_META
        dot = do_ref[pl.ds(0, LANES), :]
        delta = jnp.sum(dot.astype(F32) * o_ref[pl.ds(0, LANES), :].astype(F32), axis=-1, keepdims=True)
        dq = block(q_ref[pl.ds(0, LANES), :], dot, lse_ref[pl.ds(0, LANES), :1], delta, 0, LANES,
                   _meta_keys(LANES, LANES) & rows_m)
        write_dq(0, N_META, dq[:N_META])

        def q_tile(i, carry):
            r0 = pl.multiple_of(N_META + i * tq, N_META)
            qt = q_ref[pl.ds(r0, tq), :]
            dot = do_ref[pl.ds(r0, tq), :]
            lse_t = lse_ref[pl.ds(r0, tq), :1]
            delta = jnp.sum(dot.astype(F32) * o_ref[pl.ds(r0, tq), :].astype(F32), axis=-1, keepdims=True)
            dq_scr[...] = blocks(qt, dot, lse_t, delta, [(0, LANES, _meta_keys(tq, LANES)), (r0, tq, _chunk_causal(tq, tq))])

            def two_blocks(jj, c):
                c0 = pl.multiple_of(N_META + 2 * jj * tq, N_META)
                dq_scr[...] += block(qt, dot, lse_t, delta, c0, 2 * tq, None)
                return c

            lax.fori_loop(0, i // 2, two_blocks, 0)

            @pl.when(i % 2 == 1)
            def _():
                c0 = pl.multiple_of(N_META + (i - 1) * tq, N_META)
                dq_scr[...] += block(qt, dot, lse_t, delta, c0, tq, None)

            write_dq(r0, tq, dq_scr[...])
            return carry

        lax.fori_loop(0, nq, q_tile, 0)
        dk = dk_scr[...]
        dkv_ref[...] = jnp.concatenate([dk[:, :QK_NOPE], dv_scr[...]], axis=1).astype(dkv_ref.dtype)
        dkr_ref[...] = dk[:, QK_NOPE:]

    def head(width):
        return pl.BlockSpec((t_all, width), lambda h: (0, h))

    table = pl.BlockSpec((t_all, LANES), lambda h: (0, 0))
    per_head = pl.BlockSpec((None, t_all, LANES), lambda h: (h, 0, 0))
    return pl.pallas_call(
        body, name=name, grid=(n_heads,),
        in_specs=[head(HEAD_W), head(HEAD_W), table, head(V_HEAD), per_head, head(V_HEAD), table, table],
        out_specs=[head(HEAD_W), head(HEAD_W), per_head],
        out_shape=[jax.ShapeDtypeStruct((t_all, n_heads * HEAD_W), BF16), jax.ShapeDtypeStruct((t_all, n_heads * HEAD_W), BF16),
                   jax.ShapeDtypeStruct((n_heads, t_all, LANES), F32)],
        scratch_shapes=[pltpu.VMEM((t_all, HEAD_W), BF16), pltpu.VMEM((t_all, HEAD_W), F32), pltpu.VMEM((t_all, V_HEAD), F32),
                        pltpu.VMEM((tq, HEAD_W), F32)],
        compiler_params=_params(("parallel",)),
    )(q, kv, kr, o, lse, do, cos, sin)


LRU_ROWS = 128


def _shifted_back(ref, t0, rows, shift_max):
    main = ref[pl.ds(t0, rows), :]
    prev = ref[pl.ds(pl.multiple_of(jnp.maximum(t0 - SUBLANES, 0), SUBLANES), SUBLANES), :]
    prev = jnp.where(t0 > 0, prev, 0.0)
    ext = jnp.concatenate([prev, main], axis=0)
    return [main] + [pltpu.roll(ext, s, 0)[SUBLANES:, :] for s in range(1, shift_max + 1)]


def _shifted_ahead(ref, t0, rows, t_all, shift_max):
    main = ref[pl.ds(t0, rows), :]
    nxt = ref[pl.ds(pl.multiple_of(jnp.minimum(t0 + rows, t_all - SUBLANES), SUBLANES), SUBLANES), :]
    nxt = jnp.where(t0 + rows < t_all, nxt, 0.0)
    ext = jnp.concatenate([main, nxt], axis=0)
    return [main] + [pltpu.roll(ext, rows + SUBLANES - s, 0)[:rows, :] for s in range(1, shift_max + 1)]


def _conv_fwd(xp_ref, t0, rows, cw, cb):
    sh = _shifted_back(xp_ref, t0, rows, 3)
    out = cb + cw[3:4, :] * sh[0]
    for k in range(3):
        out = out + cw[k:k + 1, :] * sh[3 - k]
    return out, sh


def _lru_gates(xb, wga, bga, wgx, bgx, sp):
    xbb = xb.astype(BF16)
    r = _sigmoid(jnp.dot(xbb, wga, preferred_element_type=F32) + bga)
    ig = _sigmoid(jnp.dot(xbb, wgx, preferred_element_type=F32) + bgx)
    la = -LRU_C * r * sp
    a = jnp.exp(la)
    s = jnp.sqrt(_neg_expm1(2.0 * la))
    return xbb, r, ig, a, s


def _scan_tile(a, b, reverse):
    rows = a.shape[0]
    ridx = lax.broadcasted_iota(jnp.int32, a.shape, 0)
    s = 1
    while s < rows:
        if reverse:
            keep = ridx < rows - s
            a_sh, b_sh = pltpu.roll(a, rows - s, 0), pltpu.roll(b, rows - s, 0)
        else:
            keep = ridx >= s
            a_sh, b_sh = pltpu.roll(a, s, 0), pltpu.roll(b, s, 0)
        b = jnp.where(keep, a * b_sh + b, b)
        a = jnp.where(keep, a * a_sh, a)
        s *= 2
    return a, b


def _lru_fwd(xy, conv_w, conv_b, wga, bga, wgx, bgx, lam, *, name):
    t_all = xy.shape[0]
    dr = xy.shape[1] // 2
    c = LANES
    nblk = dr // c
    rows = LRU_ROWS
    nt = t_all // rows

    def body(xp_ref, yp_ref, cw_ref, cb_ref, wga_ref, bga_ref, wgx_ref, bgx_ref, lam_ref, hs_ref, hsy_ref):
        cw, cb = cw_ref[...], cb_ref[...]
        sp = _softplus_neg(lam_ref[...])

        def tile(t, h_in):
            t0 = pl.multiple_of(t * rows, rows)
            xb, _ = _conv_fwd(xp_ref, t0, rows, cw, cb)
            _, _, ig, a, s = _lru_gates(xb, wga_ref[0], bga_ref[...], wgx_ref[0], bgx_ref[...], sp)
            cum_a, h0 = _scan_tile(a, s * (ig * xb), reverse=False)
            hs = cum_a * h_in + h0
            hs_ref[pl.ds(t0, rows), :] = hs
            hsy_ref[pl.ds(t0, rows), :] = (hs * _gelu(yp_ref[pl.ds(t0, rows), :])).astype(BF16)
            return hs[rows - 1:, :]

        lax.fori_loop(0, nt, tile, jnp.zeros((1, c), F32))

    col = pl.BlockSpec((t_all, c), lambda b: (0, b))
    vec = pl.BlockSpec((1, c), lambda b: (0, b))
    wsp = pl.BlockSpec((1, c, c), lambda b: (b, 0, 0))
    return pl.pallas_call(
        body, name=name, grid=(nblk,),
        in_specs=[col, pl.BlockSpec((t_all, c), lambda b: (0, nblk + b)), pl.BlockSpec((4, c), lambda b: (0, b)), vec,
                  wsp, vec, wsp, vec, vec],
        out_specs=[col, col],
        out_shape=[jax.ShapeDtypeStruct((t_all, dr), F32), jax.ShapeDtypeStruct((t_all, dr), BF16)],
        compiler_params=_params(("parallel",)),
    )(xy, xy, conv_w, conv_b, wga, bga, wgx, bgx, lam)


def _lru_bwd(xy, hs, dhsy, conv_w, conv_b, wga, bga, wgx, bgx, lam, *, name):
    t_all = xy.shape[0]
    dr = xy.shape[1] // 2
    c = LANES
    nblk = dr // c
    rows = LRU_ROWS
    nt = t_all // rows

    def body(xp_ref, yp_ref, hs_ref, dh_ref, cw_ref, cb_ref, wga_ref, bga_ref, wgx_ref, bgx_ref, lam_ref,
             dxp_ref, dyp_ref, dcw_ref, dcb_ref, dwga_ref, dbga_ref, dwgx_ref, dbgx_ref, dlam_ref,
             xb_scr, r_scr, i_scr, a_scr):
        cw, cb = cw_ref[...], cb_ref[...]
        lamv = lam_ref[...]
        sp = _softplus_neg(lamv)
        sig_neg = 1.0 / (1.0 + jnp.exp(lamv))
        wga_v, wgx_v = wga_ref[0], wgx_ref[0]

        def recompute(t, carry):
            t0 = pl.multiple_of(t * rows, rows)
            xb, _ = _conv_fwd(xp_ref, t0, rows, cw, cb)
            _, r, ig, a, _ = _lru_gates(xb, wga_v, bga_ref[...], wgx_v, bgx_ref[...], sp)
            xb_scr[pl.ds(t0, rows), :] = xb
            r_scr[pl.ds(t0, rows), :] = r
            i_scr[pl.ds(t0, rows), :] = ig
            a_scr[pl.ds(t0, rows), :] = a
            return carry

        lax.fori_loop(0, nt, recompute, 0)
        dwga_ref[...] = jnp.zeros(dwga_ref.shape, F32)
        dwgx_ref[...] = jnp.zeros(dwgx_ref.shape, F32)

        def tile(ti, carry):
            lam_in, dbga, dbgx, dlam, dcw, dcb = carry
            t = nt - 1 - ti
            t0 = pl.multiple_of(t * rows, rows)
            a_now, a_next = _shifted_ahead(a_scr, t0, rows, t_all, 1)
            yp = yp_ref[pl.ds(t0, rows), :]
            dhy = dh_ref[pl.ds(t0, rows), :]
            cum_a, lam0 = _scan_tile(a_next, dhy * _gelu(yp), reverse=True)
            lam_t = cum_a * lam_in + lam0
            hs_now, hs_prev = _shifted_back(hs_ref, t0, rows, 1)
            da = lam_t * hs_prev
            xb = xb_scr[pl.ds(t0, rows), :]
            r = r_scr[pl.ds(t0, rows), :]
            ig = i_scr[pl.ds(t0, rows), :]
            la = -LRU_C * r * sp
            s = jnp.sqrt(_neg_expm1(2.0 * la))
            d_ixb = lam_t * s
            dla = da * a_now - (lam_t * ig * xb) * (a_now * a_now / s)
            dzr = dla * (-LRU_C * sp) * r * (1.0 - r)
            dzi = d_ixb * xb * ig * (1.0 - ig)
            dzr_b, dzi_b = dzr.astype(BF16), dzi.astype(BF16)
            xbb = xb.astype(BF16)
            dwga_ref[0] += lax.dot_general(xbb, dzr_b, _TN, preferred_element_type=F32)
            dwgx_ref[0] += lax.dot_general(xbb, dzi_b, _TN, preferred_element_type=F32)
            dxb = (d_ixb * ig + lax.dot_general(dzr_b, wga_v, _NT, preferred_element_type=F32)
                   + lax.dot_general(dzi_b, wgx_v, _NT, preferred_element_type=F32))
            xb_scr[pl.ds(t0, rows), :] = dxb
            dyp_ref[pl.ds(t0, rows), :] = (dhy * hs_now * _gelu_grad(yp)).astype(BF16)
            ahead = _shifted_ahead(xb_scr, t0, rows, t_all, 3)
            dxp = cw[3:4, :] * ahead[0]
            for k in range(3):
                dxp = dxp + cw[k:k + 1, :] * ahead[3 - k]
            dxp_ref[pl.ds(t0, rows), :] = dxp.astype(BF16)
            back = _shifted_back(xp_ref, t0, rows, 3)
            dcw_t = jnp.concatenate([jnp.sum(dxb * back[3 - k], axis=0, keepdims=True) for k in range(4)], axis=0)
            return (lam_t[:1, :], dbga + jnp.sum(dzr, axis=0, keepdims=True), dbgx + jnp.sum(dzi, axis=0, keepdims=True),
                    dlam + jnp.sum(dla * r, axis=0, keepdims=True), dcw + dcw_t, dcb + jnp.sum(dxb, axis=0, keepdims=True))

        zero = jnp.zeros((1, c), F32)
        _, dbga, dbgx, dlam, dcw, dcb = lax.fori_loop(0, nt, tile, (zero, zero, zero, zero, jnp.zeros((4, c), F32), zero))
        dbga_ref[...] = dbga
        dbgx_ref[...] = dbgx
        dlam_ref[...] = dlam * (LRU_C * sig_neg)
        dcw_ref[...] = dcw
        dcb_ref[...] = dcb

    col = pl.BlockSpec((t_all, c), lambda b: (0, b))
    col2 = pl.BlockSpec((t_all, c), lambda b: (0, nblk + b))
    vec = pl.BlockSpec((1, c), lambda b: (0, b))
    tap = pl.BlockSpec((4, c), lambda b: (0, b))
    wsp = pl.BlockSpec((1, c, c), lambda b: (b, 0, 0))
    vshape = jax.ShapeDtypeStruct((1, dr), F32)
    wshape = jax.ShapeDtypeStruct((nblk, c, c), F32)
    def planes_body(*refs):
        dxy_ref = refs[11]
        body(*refs[:11], dxy_ref.at[0], dxy_ref.at[1], *refs[12:])

    return pl.pallas_call(
        planes_body, name=name, grid=(nblk,),
        in_specs=[col, col2, col, col, tap, vec, wsp, vec, wsp, vec, vec],
        out_specs=[pl.BlockSpec((2, t_all, c), lambda b: (0, 0, b)), tap, vec, wsp, vec, wsp, vec, vec],
        out_shape=[jax.ShapeDtypeStruct((2, t_all, dr), BF16),
                   jax.ShapeDtypeStruct((4, dr), F32), vshape, wshape, vshape, wshape, vshape, vshape],
        scratch_shapes=[pltpu.VMEM((t_all, c), F32)] * 4,
        compiler_params=_params(("parallel",)),
    )(xy, xy, hs, dhsy, conv_w, conv_b, wga, bga, wgx, bgx, lam)


def _mesh_pos():
    return lax.axis_index("x"), lax.axis_index("y"), lax.axis_index("c")


def _all_gather(shards, *, name):
    n = len(shards)

    def body(*refs):
        ins, outs, token = refs[:n], refs[n:2 * n], refs[2 * n]
        send_sems, recv_sems, local_sems = refs[2 * n + 1:]
        token[...] = jnp.zeros(token.shape, token.dtype)
        x, y, c = _mesh_pos()
        me, sibling = (x, y, c), (x, y, 1 - c)
        chips = [(1 - x, y), (x, 1 - y), (1 - x, 1 - y)]
        slot = _slot

        def copy(a, k, block, to, src=None):
            dst = outs[a].at[slot(block)]
            return pltpu.make_async_remote_copy(
                src_ref=dst if src is None else src, dst_ref=dst, send_sem=send_sems.at[a, k],
                recv_sem=recv_sems.at[a, k], device_id=to, device_id_type=MESH)

        mine = [pltpu.make_async_copy(ins[a], outs[a].at[slot(me)], local_sems.at[a]) for a in range(n)]
        for cp in mine:
            cp.start()
        first = []
        for a in range(n):
            first.append(copy(a, 0, me, sibling, src=ins[a]))
            first += [copy(a, 1 + j, me, (*chip, c), src=ins[a]) for j, chip in enumerate(chips)]
        for cp in first:
            cp.start()
        passed = []
        for a in range(n):
            for j, chip in enumerate(chips):
                copy(a, 1 + j, (*chip, c), me).wait_recv()
                fwd = copy(a, 4 + j, (*chip, c), sibling)
                fwd.start()
                passed.append(fwd)
        for a in range(n):
            copy(a, 0, sibling, me).wait_recv()
            for j, chip in enumerate(chips):
                copy(a, 4 + j, (*chip, 1 - c), me).wait_recv()
        for cp in first + passed:
            cp.wait_send()
        for cp in mine:
            cp.wait()

    any_spec = pl.BlockSpec(memory_space=pl.ANY)
    outs = pl.pallas_call(
        body, name=name,
        in_specs=[any_spec] * n, out_specs=[any_spec] * n + [pl.BlockSpec(memory_space=pltpu.VMEM)],
        out_shape=[jax.ShapeDtypeStruct((N_DEV,) + s.shape, s.dtype) for s in shards]
        + [jax.ShapeDtypeStruct((SUBLANES, LANES), F32)],
        scratch_shapes=[pltpu.SemaphoreType.DMA((n, 7)), pltpu.SemaphoreType.DMA((n, 7)), pltpu.SemaphoreType.DMA((n,))],
    )(*shards)
    return list(outs[:n]), outs[n][0, 0]


_HBM = pl.BlockSpec(memory_space=pltpu.HBM)
_SEM = pl.BlockSpec(memory_space=pltpu.SEMAPHORE)
_ANY = pl.BlockSpec(memory_space=pl.ANY)
_EFFECT = pltpu.SideEffectType.DATAFLOW_SIDE_EFFECTING


def _slot(p):
    return 4 * p[0] + 2 * p[1] + p[2]


def _remote(src, dst, send, recv, idx, to):
    return pltpu.make_async_remote_copy(src_ref=src, dst_ref=dst, send_sem=send.at[idx], recv_sem=recv.at[idx],
                                        device_id=to, device_id_type=MESH)


def _ag_plan_own(a, src, land, send, recv):
    x, y, c = _mesh_pos()
    dst = land.at[_slot((x, y, c))]
    targets = [(x, y, 1 - c), (1 - x, y, c), (x, 1 - y, c), (1 - x, 1 - y, c)]
    return [_remote(src, dst, send, recv, 4 * a + k, to) for k, to in enumerate(targets)]


def _ag_plan_pass(a, src, land, send, recv):
    x, y, c = _mesh_pos()
    blocks = [land.at[_slot((px, py, c))] for px, py in ((1 - x, y), (x, 1 - y), (1 - x, 1 - y))]
    return [_remote(blk, blk, send, recv, 3 * a + k, (x, y, 1 - c)) for k, blk in enumerate(blocks)]


def _rs_plan_sibling(a, src, land, send, recv):
    x, y, c = _mesh_pos()
    return [_remote(src.at[2 * j + (1 - c)], land.at[j], send, recv, 4 * a + j, (x, y, 1 - c)) for j in range(4)]


def _rs_plan_chips(a, src, land, send, recv):
    x, y, c = _mesh_pos()
    out = []
    for k in (1, 2, 3):
        px = 1 - x if k & 2 else x
        py = 1 - y if k & 1 else y
        out.append(_remote(src.at[2 * px + py], land.at[k - 1], send, recv, 3 * a + k - 1, (px, py, c)))
    return out


def _in_hbm(a):
    return pltpu.with_memory_space_constraint(a, pltpu.HBM)


def _exchange_start(srcs, lands, plan, n_k, *, name):
    ns, n = len(srcs), len(lands)

    def body(*refs):
        src_refs, land_refs = refs[:ns], refs[ns:ns + n]
        send, recv = refs[ns + n], refs[ns + n + 1]
        token = refs[-1]
        for a in range(n):
            for cp in plan(a, src_refs[a] if ns else None, land_refs[a], send, recv):
                cp.start()
        token[...] = jnp.zeros(token.shape, token.dtype)

    bufs = list(srcs) + list(lands)
    outs = pl.pallas_call(
        body, name=name,
        out_shape=(pltpu.SemaphoreType.DMA((n * n_k,)), pltpu.SemaphoreType.DMA((n * n_k,)),
                   *[pltpu.HBM(b.shape, b.dtype) for b in bufs], jax.ShapeDtypeStruct((SUBLANES, LANES), F32)),
        in_specs=[_HBM] * (ns + n),
        out_specs=(_SEM, _SEM, *[_HBM] * (ns + n), pl.BlockSpec(memory_space=pltpu.VMEM)),
        input_output_aliases={i: 2 + i for i in range(ns + n)},
        compiler_params=pltpu.CompilerParams(has_side_effects=_EFFECT),
    )(*[_in_hbm(b) for b in bufs])
    return outs[0], outs[1], list(outs[2:2 + ns]), list(outs[2 + ns:2 + ns + n]), outs[-1]


def _exchange_wait(started, plan, after, *, name):
    send, recv, srcs, lands, _ = started
    ns, n = len(srcs), len(lands)

    def body(*refs):
        src_refs, land_refs = refs[:ns], refs[ns:ns + n]
        send_ref, recv_ref = refs[ns + n], refs[ns + n + 1]
        for a in range(n):
            for cp in plan(a, src_refs[a] if ns else None, land_refs[a], send_ref, recv_ref):
                cp.wait_send()
                cp.wait_recv()

    bufs = list(srcs) + list(lands)
    outs = pl.pallas_call(
        body, name=name,
        out_shape=tuple(pltpu.HBM(b.shape, b.dtype) for b in bufs),
        in_specs=[_HBM] * (ns + n) + [_SEM, _SEM, _ANY],
        out_specs=tuple([_HBM] * (ns + n)),
        input_output_aliases={i: i for i in range(ns + n)},
        compiler_params=pltpu.CompilerParams(has_side_effects=_EFFECT),
    )(*bufs, send, recv, after)
    return list(outs[:ns]), list(outs[ns:])


def _pair_add(grads, landed, core, *, name, tr):
    _, r_all, c_all = grads.shape

    def body(core_ref, g_ref, l_ref, o_ref):
        o_ref[...] = (g_ref[...].astype(F32) + l_ref[...].astype(F32)).astype(o_ref.dtype)

    return pl.pallas_call(
        body, name=name,
        grid_spec=pltpu.PrefetchScalarGridSpec(
            num_scalar_prefetch=1, grid=(4, r_all // tr),
            in_specs=[pl.BlockSpec((None, tr, c_all), lambda j, i, core_ref: (2 * j + core_ref[0], i, 0)),
                      pl.BlockSpec((None, tr, c_all), lambda j, i, core_ref: (j, i, 0))],
            out_specs=pl.BlockSpec((None, tr, c_all), lambda j, i, core_ref: (j, i, 0))),
        out_shape=jax.ShapeDtypeStruct((4, r_all, c_all), grads.dtype),
        compiler_params=_params(("parallel", "parallel")),
    )(core, grads, landed)


def _adamw_math(w, g, m, v):
    m2 = ADAM_B1 * m + (1.0 - ADAM_B1) * g
    v2 = ADAM_B2 * v + (1.0 - ADAM_B2) * (g * g)
    m_hat = m2 / (1.0 - ADAM_B1 ** ADAM_STEP)
    v_hat = v2 / (1.0 - ADAM_B2 ** ADAM_STEP)
    delta = -ADAM_LR * (m_hat / (jnp.sqrt(v_hat) + ADAM_EPS) + ADAM_WD * w)
    return delta, m2, v2


def _adamw(w, m, v, terms, order, *, name, tr, col_block=None, own=None, stack=None):
    r_all, c_all = w.shape
    n_slots = terms.shape[0]

    def body(*refs):
        if col_block is not None or own is not None:
            refs = refs[1:]
        own_ref = None
        if own is not None:
            own_ref, refs = refs[0], refs[1:]
        w_ref, m_ref, v_ref, t_ref, g_ref, d_ref, m2_ref, v2_ref = refs
        if own_ref is not None:
            g = own_ref[...].astype(F32) + t_ref[order[0]].astype(F32)
        else:
            g = t_ref[order[0]].astype(F32)
        for s in order[1:]:
            g = g + t_ref[s].astype(F32)
        delta, m2, v2 = _adamw_math(w_ref[...], g, m_ref[...], v_ref[...])
        g_ref[...] = g
        d_ref[...] = delta
        m2_ref[...] = m2
        v2_ref[...] = v2

    shape = jax.ShapeDtypeStruct((r_all, c_all), F32)
    if own is not None:
        layer, n_layers, prev = stack
        row = pl.BlockSpec((tr, c_all), lambda i, idx: (i, 0))
        slab = pl.BlockSpec((None, tr, c_all), lambda i, idx: (layer, i, 0))
        carried = [] if prev is None else list(prev)

        def stacked_body(*refs):
            body(*refs[:6], *refs[6 + len(carried):])

        return pl.pallas_call(
            stacked_body, name=name,
            grid_spec=pltpu.PrefetchScalarGridSpec(
                num_scalar_prefetch=1, grid=(r_all // tr,),
                in_specs=[pl.BlockSpec((None, tr, c_all), lambda i, idx: (idx[0], i, 0)), row, row, row,
                          pl.BlockSpec((n_slots, tr, c_all), lambda i, idx: (0, i, 0))] + [_ANY] * len(carried),
                out_specs=[slab] * 4),
            out_shape=[jax.ShapeDtypeStruct((n_layers, r_all, c_all), F32)] * 4,
            input_output_aliases={6 + k: k for k in range(len(carried))},
            compiler_params=_params(("parallel",)),
        )(own[1], own[0], w, m, v, terms, *carried)
    if col_block is None:
        row = pl.BlockSpec((tr, c_all), lambda i: (i, 0))
        return pl.pallas_call(
            body, name=name, grid=(r_all // tr,),
            in_specs=[row, row, row, pl.BlockSpec((n_slots, tr, c_all), lambda i: (0, i, 0))],
            out_specs=[row] * 4, out_shape=[shape] * 4, compiler_params=_params(("parallel",)),
        )(w, m, v, terms)
    row = pl.BlockSpec((tr, c_all), lambda i, blk: (i, 0))
    return pl.pallas_call(
        body, name=name,
        grid_spec=pltpu.PrefetchScalarGridSpec(
            num_scalar_prefetch=1, grid=(r_all // tr,),
            in_specs=[row, row, row, pl.BlockSpec((n_slots, tr, c_all), lambda i, blk: (0, i, blk[0]))],
            out_specs=[row] * 4),
        out_shape=[shape] * 4, compiler_params=_params(("parallel",)),
    )(col_block, w, m, v, terms)


def _rope_tables(t_all):
    pos = jnp.arange(t_all, dtype=F32)
    inv_freq = ROPE_THETA ** (-jnp.arange(0, QK_ROPE, 2, dtype=F32) / QK_ROPE)
    ang = pos[:, None] * inv_freq[None, :]
    cos, sin = jnp.cos(ang), jnp.sin(ang)
    return jnp.tile(cos, (1, LANES // (QK_ROPE // 2))), jnp.tile(sin, (1, LANES // (QK_ROPE // 2)))


def _adam_row_tile(r_all, c_all, block_bytes=512 * 1024):
    target = max(SUBLANES, block_bytes // (4 * c_all))
    return _pick(r_all, [t for t in (1024, 704, 512, 352, 256, 176, 128, 64, 32, 16, 8) if t <= target])


def _rows_natural(wg):
    return wg.reshape(wg.shape[0] * wg.shape[1], wg.shape[2])


def _mla_layer_fwd(tag, h, g_mix, ws, qn, kvn, cos, sin, *, tm, tq, n_heads, scale, n_real):
    w_in, w_uq, w_ukv, w_o = _rows_natural(ws[0]), ws[1], ws[2], _rows_natural(ws[3])
    t_all, d = h.shape
    lq, lkv = qn.shape[1], kvn.shape[1]
    tmb = _pick(t_all, _ROW_TILES)
    hn = _rmsnorm_fwd(h, g_mix, name=f"norm_mix{tag}", tm=tm)
    proj = _mm_nn(hn, w_in, name=f"mla_in{tag}", out_dtype=F32, tm=tmb, tn=w_in.shape[1], tk=_pick(d, _DIVS))
    cq, ckv, kr = _mla_prep_fwd(proj, qn, kvn, cos, sin, name=f"mla_prep{tag}", tm=tm, lq=lq, lkv=lkv)
    q = _mm_nn(cq, w_uq, name=f"mla_q{tag}", out_dtype=BF16, tm=tmb, tn=w_uq.shape[2], tk=lq, b_blocked=True,
               epilogue=_rope_q_epilogue, extras=(cos, sin))
    kv = _mm_nn(ckv, w_ukv, name=f"mla_kv{tag}", out_dtype=BF16, tm=tmb, tn=w_ukv.shape[2], tk=lkv, b_blocked=True)
    o, lse = _attn_fwd(q, kv, kr, name=f"attn_fwd{tag}", n_heads=n_heads, tq=tq, n_real=n_real, scale=scale)
    h_mid = _mm_nn(o, w_o, name=f"mla_o{tag}", out_dtype=F32, tm=tm, tn=d, tk=o.shape[1], res=h)
    return h_mid, (hn, proj, cq, ckv, kr, q, kv, o, lse)


def _mla_layer_bwd(tag, dh, dh_b, h_in, saved, g_mix, ws, qn, kvn, cos, sin, *, tm, tq, n_heads, scale, n_real):
    hn, proj, cq, ckv, kr, q, kv, o, lse = saved
    w_in, w_uq, w_ukv, w_o = _rows_natural(ws[0]), ws[1], ws[2], _rows_natural(ws[3])
    t_all, d = h_in.shape
    lq, lkv = qn.shape[1], kvn.shape[1]
    ov = o.shape[1]
    tmb = _pick(t_all, _ROW_TILES)
    tn_d, tk_d = _pick(d, _DIVS[1:]), _pick(d, _DIVS)
    do = _mm_nt(dh_b, w_o, name=f"mla_do{tag}", out_dtype=BF16, tm=tmb, tn=_pick(ov, _DIVS[1:]), tk=tk_d)
    dw_o = _mm_tn(o, dh_b, name=f"mla_dwo{tag}", out_dtype=BF16, tm=_pick(ov, _DIVS[2:]), tn=tn_d, tk=t_all)
    dq, dkv, dkr_h = _attn_bwd(q, kv, kr, o, lse, do, cos, sin, name=f"attn_bwd{tag}", n_heads=n_heads, tq=tq, n_real=n_real,
                               scale=scale)
    hw, kw = w_uq.shape[2], w_ukv.shape[2]
    dw_uq = _mm_tn(cq, dq, name=f"mla_dwuq{tag}", out_dtype=BF16, tm=lq, tn=hw, tk=t_all, out_block=hw)
    dcq = _mm_nt(dq, w_uq, name=f"mla_dcq{tag}", out_dtype=F32, tm=tm, tn=lq, tk=dq.shape[1], b_blocked=True)
    dw_ukv = _mm_tn(ckv, dkv, name=f"mla_dwukv{tag}", out_dtype=BF16, tm=lkv, tn=kw, tk=t_all, out_block=kw)
    dckv = _mm_nt(dkv, w_ukv, name=f"mla_dckv{tag}", out_dtype=F32, tm=tm, tn=lkv, tk=dkv.shape[1], b_blocked=True)
    dproj, dqn, dkvn = _mla_prep_bwd(dcq, dckv, dkr_h, proj, qn, kvn, cos, sin, name=f"mla_prep_bwd{tag}", tm=tm, lq=lq, lkv=lkv)
    wc = w_in.shape[1]
    dw_in = _mm_tn(hn, dproj, name=f"mla_dwin{tag}", out_dtype=BF16, tm=_pick(d, _DIVS[2:]), tn=wc, tk=t_all)
    dhn = _mm_nt(dproj, w_in, name=f"mla_dhn{tag}", out_dtype=F32, tm=tmb, tn=tn_d, tk=wc)
    dh, dh_b, dg = _rmsnorm_bwd(dhn, h_in, g_mix, dh, name=f"norm_mix_bwd{tag}", tm=tm)
    return dh, dh_b, dg, dqn, dkvn, [dw_in.reshape(N_DEV, -1, wc), dw_uq, dw_ukv, dw_o.reshape(N_DEV, -1, d)]


def _lru_layer_fwd(tag, h, g_mix, ws, small, *, tm):
    w_lin, w_lo = ws[0], _rows_natural(ws[1])
    t_all, d = h.shape
    dr = w_lo.shape[0]
    tmb = _pick(t_all, _ROW_TILES)
    hn = _rmsnorm_fwd(h, g_mix, name=f"norm_mix{tag}", tm=tm)
    xy = _mm_nn(hn, w_lin, name=f"lru_in{tag}", out_dtype=F32, tm=tmb, tn=w_lin.shape[2], tk=_pick(d, _DIVS), b_blocked=True,
                rows_outer=True)
    hs, hsy = _lru_fwd(xy, *small, name=f"lru_fwd{tag}")
    h_mid = _mm_nn(hsy, w_lo, name=f"lru_o{tag}", out_dtype=F32, tm=tm, tn=d, tk=dr, res=h)
    return h_mid, (hn, xy, hs, hsy)


def _lru_layer_bwd(tag, dh, dh_b, h_in, saved, g_mix, ws, small, *, tm):
    hn, xy, hs, hsy = saved
    w_lin, w_lo = ws[0], _rows_natural(ws[1])
    t_all, d = h_in.shape
    dr = w_lo.shape[0]
    tmb = _pick(t_all, _ROW_TILES)
    tn_d, tk_d = _pick(d, _DIVS[1:]), _pick(d, _DIVS)
    dhsy = _mm_nt(dh_b, w_lo, name=f"lru_dhsy{tag}", out_dtype=F32, tm=tmb, tn=_pick(dr, _DIVS[1:]), tk=tk_d)
    dw_lo = _mm_tn(hsy, dh_b, name=f"lru_dwo{tag}", out_dtype=BF16, tm=_pick(dr, _DIVS[2:]), tn=tn_d, tk=t_all)
    dxy, *dsmall = _lru_bwd(xy, hs, dhsy, *small, name=f"lru_bwd{tag}")
    lw = w_lin.shape[2]
    dw_lin = _mm_tn(hn, dxy, name=f"lru_dwin{tag}", out_dtype=BF16, tm=tn_d, tn=lw, tk=t_all, out_block=lw)
    dhn = _mm_nt(dxy, w_lin, name=f"lru_dhn{tag}", out_dtype=F32, tm=tm, tn=tn_d, tk=2 * dr, b_blocked=True)
    dh, dh_b, dg = _rmsnorm_bwd(dhn, h_in, g_mix, dh, name=f"norm_mix_bwd{tag}", tm=tm)
    return dh, dh_b, dg, tuple(dsmall), [dw_lin, dw_lo.reshape(N_DEV, -1, d)]


def _ffn_layer_fwd(tag, h_mid, g_ffn, ws, *, tm):
    w_gu, w_down = ws[0], _rows_natural(ws[1])
    t_all, d = h_mid.shape
    f_all = w_down.shape[0]
    tmb = _pick(t_all, _ROW_TILES)
    fk = _pick(f_all, (1408,) + _DIVS[1:])
    hn2 = _rmsnorm_fwd(h_mid, g_ffn, name=f"norm_ffn{tag}", tm=tm)
    gu, act = _ffn_up(hn2, w_gu, name=f"ffn_up{tag}", tm=tm)
    h_out = _mm_nn(act, w_down, name=f"ffn_down{tag}", out_dtype=F32, tm=tm, tn=_pick(d, _DIVS[1:]), tk=f_all, res=h_mid)
    return h_out, (hn2, gu, act)


def _ffn_layer_bwd(tag, dh, dh_b, h_mid, saved, g_ffn, ws, *, tm):
    hn2, gu, act = saved
    w_gu, w_down = ws[0], _rows_natural(ws[1])
    t_all, d = h_mid.shape
    f_all = w_down.shape[0]
    f_local = w_gu.shape[2]
    tmb = _pick(t_all, _ROW_TILES)
    fk = _pick(f_all, (1408,) + _DIVS[1:])
    tn_d, tk_d = _pick(d, _DIVS[1:]), _pick(d, _DIVS)
    dgu = _ffn_dact(dh_b, w_down, gu, name=f"ffn_dact{tag}", tm=tm, tn=f_local)
    dw_down = _mm_tn(act, dh_b, name=f"ffn_dwdown{tag}", out_dtype=BF16, tm=fk, tn=_pick(d, _DIVS[2:]), tk=t_all)
    dhn2 = _mm_nt(dgu, w_gu, name=f"ffn_dhn{tag}", out_dtype=F32, tm=tm, tn=_pick(d, _DIVS[2:]), tk=2 * f_all, b_blocked=True)
    dw_gu = _mm_tn(hn2, dgu, name=f"ffn_dwgu{tag}", out_dtype=BF16, tm=_pick(d, _DIVS[2:]), tn=f_local, tk=t_all, out_block=f_local,
                   cols_outer=True)
    dh, dh_b, dg = _rmsnorm_bwd(dhn2, h_mid, g_ffn, dh, name=f"norm_ffn_bwd{tag}", tm=tm)
    return dh, dh_b, dg, [dw_gu, dw_down.reshape(N_DEV, -1, d)]


def kernel(x, meta_tokens, norm_mix, norm_ffn, norm_final, mla_w_in, mla_q_norm, mla_kv_norm, mla_w_uq, mla_w_ukv, mla_w_o, lru_w_in, lru_conv_w, lru_conv_b, lru_w_gate_a, lru_b_gate_a, lru_w_gate_x, lru_b_gate_x, lru_lambda, lru_w_o, ffn_w_gu, ffn_w_down, loss_target, m_meta_tokens, m_norm_mix, m_norm_ffn, m_norm_final, m_mla_w_in, m_mla_q_norm, m_mla_kv_norm, m_mla_w_uq, m_mla_w_ukv, m_mla_w_o, m_lru_w_in, m_lru_conv_w, m_lru_conv_b, m_lru_w_gate_a, m_lru_b_gate_a, m_lru_w_gate_x, m_lru_b_gate_x, m_lru_lambda, m_lru_w_o, m_ffn_w_gu, m_ffn_w_down, v_meta_tokens, v_norm_mix, v_norm_ffn, v_norm_final, v_mla_w_in, v_mla_q_norm, v_mla_kv_norm, v_mla_w_uq, v_mla_w_ukv, v_mla_w_o, v_lru_w_in, v_lru_conv_w, v_lru_conv_b, v_lru_w_gate_a, v_lru_b_gate_a, v_lru_w_gate_x, v_lru_b_gate_x, v_lru_lambda, v_lru_w_o, v_ffn_w_gu, v_ffn_w_down):
    seq, d = x.shape[1], x.shape[2]
    assert seq % CHUNK == 0
    n_real = N_META + seq
    t_all = -(-n_real // LANES) * LANES
    tm = _pick(t_all, (384, 256, 128))
    tq = _pick(seq, (512, 256, 128, 64))
    depth = norm_mix.shape[0]
    n_mla, n_lru = mla_w_in.shape[0], lru_w_in.shape[0]
    lq, lkv = mla_q_norm.shape[1], mla_kv_norm.shape[1]
    w_in_cols = lq + lkv + LANES
    heads_local = mla_w_uq.shape[2] // (QK_NOPE + QK_ROPE)
    n_heads = heads_local * N_DEV
    dr = lru_w_gate_a.shape[1] * lru_w_gate_a.shape[2]
    scale = (QK_NOPE + QK_ROPE) ** -0.5
    cx, cy, cc = _mesh_pos()
    core = jnp.reshape(cc, (1,)).astype(jnp.int32)
    my_slot = jnp.reshape(4 * cx + 2 * cy + cc, (1,)).astype(jnp.int32)

    def pad_cols(w, cols):
        return jnp.pad(w, ((0, 0), (0, cols - w.shape[1])))

    def pad_heads(w):
        k_all = w.shape[0]
        w3 = w.reshape(k_all, heads_local, QK_NOPE + QK_ROPE)
        return jnp.pad(w3, ((0, 0), (0, 0), (0, HEAD_W - QK_NOPE - QK_ROPE))).reshape(k_all, heads_local * HEAD_W)

    def unpad_heads(w):
        k_all = w.shape[0]
        return w.reshape(k_all, heads_local, HEAD_W)[:, :, :QK_NOPE + QK_ROPE].reshape(k_all, -1)

    small_rows = N_META + n_lru * 4 + 2 * n_lru
    small_pad = -(-small_rows // SUBLANES) * SUBLANES

    def pack_small(meta, conv_w, conv_b, lam):
        rows = jnp.concatenate([meta, conv_w.reshape(n_lru * 4, -1), conv_b, lam], axis=0)
        return jnp.pad(rows, ((0, small_pad - small_rows), (0, 0)))

    def unpack_small(p):
        o1 = N_META + n_lru * 4
        return (p[:N_META], p[N_META:o1].reshape(n_lru, 4, -1), p[o1:o1 + n_lru], p[o1 + n_lru:o1 + 2 * n_lru])

    (small_full,), small_done = _all_gather([pack_small(meta_tokens, lru_conv_w, lru_conv_b, lru_lambda)], name="ag_small")
    small_full = jnp.transpose(small_full, (1, 0, 2)).reshape(small_pad, -1)
    meta_full, conv_w_full, conv_b_full, lam_full = unpack_small(small_full)

    def wire(w):
        return (w + small_done).astype(BF16)

    mla_shards, lru_shards, ffn_shards = [], [], []
    for j in range(n_mla):
        mla_shards.append([wire(pad_cols(mla_w_in[j], w_in_cols)), wire(pad_heads(mla_w_uq[j])), wire(mla_w_ukv[j]),
                           wire(mla_w_o[j])])
    for j in range(n_lru):
        lru_shards.append([wire(lru_w_in[j]), wire(lru_w_o[j])])
    for layer in range(depth):
        ffn_shards.append([wire(ffn_w_gu[layer]), wire(ffn_w_down[layer])])

    n_sub = 2 * depth
    groups = []
    for layer in range(depth):
        groups += [mla_shards[layer // 2] if layer % 2 == 0 else lru_shards[layer // 2], ffn_shards[layer]]
    slot_idx = 4 * cx + 2 * cy + cc
    ag_own = []
    for gi, shards in enumerate(groups):
        lands = [lax.dynamic_update_slice(lax.empty((N_DEV,) + s.shape, s.dtype), s[None], (slot_idx, 0, 0)) for s in shards]
        ag_own.append(_exchange_start(shards, lands, _ag_plan_own, 4, name=f"ag{gi}_start"))
    ag_pass = [None] * n_sub
    weights = [None] * n_sub

    def ag_landed(gi, after):
        _, lands = _exchange_wait(ag_own[gi], _ag_plan_own, after, name=f"ag{gi}_wait")
        ag_pass[gi] = _exchange_start([], lands, _ag_plan_pass, 3, name=f"ag{gi}_pass")
        return ag_pass[gi][4][0, 0]

    def ag_done(gi, after):
        _, weights[gi] = _exchange_wait(ag_pass[gi], _ag_plan_pass, after, name=f"ag{gi}_pass_wait")

    cos, sin = _rope_tables(t_all)
    zeros_tail = jnp.zeros((t_all - n_real, d), F32)
    started = ag_own[0][4][0, 0]
    for st in ag_own[1:]:
        started = started + st[4][0, 0]
    h = jnp.concatenate([meta_full + started, x[0], zeros_tail], axis=0)
    target = jnp.concatenate([jnp.zeros((N_META, d), F32), loss_target[0], zeros_tail], axis=0)

    attn_kw = dict(tm=tm, tq=tq, n_heads=n_heads, scale=scale, n_real=n_real)

    def lru_small(j):
        return (conv_w_full[j], conv_b_full[j][None, :], lru_w_gate_a[j].astype(BF16), lru_b_gate_a[j].reshape(1, dr),
                lru_w_gate_x[j].astype(BF16), lru_b_gate_x[j].reshape(1, dr), lam_full[j][None, :])

    def before_sublayer(k, act):
        tok = ag_landed(k, act) if k <= 1 else 0.0
        ag_done(k, act)
        if 1 <= k < n_sub - 1:
            tok = tok + ag_landed(k + 1, act)
        return tok

    saved = []
    for layer in range(depth):
        j = layer // 2
        g_mix = norm_mix[layer][None, :] + before_sublayer(2 * layer, h)
        if layer % 2 == 0:
            h_mid, mix_saved = _mla_layer_fwd(layer, h, g_mix, weights[2 * layer], mla_q_norm[j][None, :],
                                              mla_kv_norm[j][None, :], cos, sin, **attn_kw)
        else:
            h_mid, mix_saved = _lru_layer_fwd(layer, h, g_mix, weights[2 * layer], lru_small(j), tm=tm)
        g_ffn = norm_ffn[layer][None, :] + before_sublayer(2 * layer + 1, h_mid)
        h_out, ffn_saved = _ffn_layer_fwd(layer, h_mid, g_ffn, weights[2 * layer + 1], tm=tm)
        saved.append((h, h_mid, mix_saved, ffn_saved))
        h = h_out

    loss_part, dh, dh_b, dg_final = _loss_head(h, target, norm_final[None, :], name="loss_head", tm=tm, n_real=n_real)
    loss = lax.psum(loss_part[0, 0], ("x", "y", "c"))

    rs_sib, rs_chip, reduced = [None] * n_sub, [None] * n_sub, [None] * n_sub
    chip_idx = jnp.reshape(2 * cx + cy, (1,)).astype(jnp.int32)

    def rs_begin(k, grads):
        lands = [lax.empty((4,) + g.shape[1:], g.dtype) for g in grads]
        rs_sib[k] = _exchange_start(grads, lands, _rs_plan_sibling, 4, name=f"rs{k}_start")
        return rs_sib[k][4][0, 0]

    def rs_middle(k, after):
        grads, landed = _exchange_wait(rs_sib[k], _rs_plan_sibling, after, name=f"rs{k}_wait")
        parts = [_pair_add(g, l, core, name=f"rs{k}_add{a}", tr=_adam_row_tile(g.shape[1], g.shape[2], 4 * 1024 * 1024))
                 for a, (g, l) in enumerate(zip(grads, landed))]
        lands = [lax.empty((3,) + p.shape[1:], p.dtype) for p in parts]
        rs_chip[k] = _exchange_start(parts, lands, _rs_plan_chips, 3, name=f"rs{k}_chips")
        return rs_chip[k][4][0, 0]

    def rs_end(k, after):
        reduced[k] = _exchange_wait(rs_chip[k], _rs_plan_chips, after, name=f"rs{k}_chips_wait")

    d_norm_mix, d_norm_ffn = [None] * depth, [None] * depth
    d_qn, d_kvn = [None] * n_mla, [None] * n_mla
    d_small = {k: [None] * n_lru for k in ("cw", "cb", "wga", "bga", "wgx", "bgx", "lam")}
    tok, waiting = 0.0, None
    for layer in reversed(range(depth)):
        j = layer // 2
        h_in, h_mid, mix_saved, ffn_saved = saved[layer]
        dh, dh_b, d_norm_ffn[layer], ffn_g = _ffn_layer_bwd(layer, dh, dh_b, h_mid, ffn_saved, norm_ffn[layer][None, :] + tok,
                                                            weights[2 * layer + 1], tm=tm)
        tok = rs_begin(2 * layer + 1, ffn_g)
        if waiting is not None:
            tok = tok + rs_middle(waiting, dh)
        waiting = 2 * layer + 1
        if layer == 0:
            tok = tok + rs_middle(waiting, dh)
            waiting = None
        g_mix = norm_mix[layer][None, :] + tok
        if layer % 2 == 0:
            dh, dh_b, d_norm_mix[layer], d_qn[j], d_kvn[j], mix_g = _mla_layer_bwd(
                layer, dh, dh_b, h_in, mix_saved, g_mix, weights[2 * layer], mla_q_norm[j][None, :], mla_kv_norm[j][None, :],
                cos, sin, **attn_kw)
        else:
            dh, dh_b, d_norm_mix[layer], dsmall, mix_g = _lru_layer_bwd(layer, dh, dh_b, h_in, mix_saved, g_mix,
                                                                        weights[2 * layer], lru_small(j), tm=tm)
            for key, val in zip(("cw", "cb", "wga", "bga", "wgx", "bgx", "lam"), dsmall):
                d_small[key][j] = val
        tok = rs_begin(2 * layer, mix_g)
        if waiting is not None:
            tok = tok + rs_middle(waiting, dh)
        waiting = 2 * layer
    rs_middle(waiting, dh)

    grad_x = dh[N_META:n_real][None]

    d_meta = dh[:N_META]
    small_grad = pack_small(d_meta, jnp.stack(d_small["cw"], axis=0), jnp.concatenate(d_small["cb"], axis=0),
                            jnp.concatenate(d_small["lam"], axis=0))
    rep_grads = [
        jnp.concatenate(d_norm_mix, axis=0), jnp.concatenate(d_norm_ffn, axis=0), dg_final,
        jnp.concatenate(d_qn, axis=0), jnp.concatenate(d_kvn, axis=0),
        jnp.stack(d_small["wga"], axis=0).reshape(-1, LANES), jnp.concatenate(d_small["bga"], axis=0),
        jnp.stack(d_small["wgx"], axis=0).reshape(-1, LANES), jnp.concatenate(d_small["bgx"], axis=0),
    ]
    small_srcs = [small_grad] + [jnp.pad(g, ((0, -g.shape[0] % SUBLANES), (0, 0))) for g in rep_grads]
    small_lands = [lax.dynamic_update_slice(lax.empty((N_DEV,) + s.shape, s.dtype), s[None], (slot_idx, 0, 0))
                   for s in small_srcs]
    small_own = _exchange_start(small_srcs, small_lands, _ag_plan_own, 4, name="ag_grads_start")

    res = {}

    def adam_sharded(nm, k, a, idx, n_layers, w, m, v):
        parts, landed = reduced[k]
        r_all, c_all = landed[a].shape[1], landed[a].shape[2]
        res[nm] = _adamw(w.reshape(r_all, c_all), m.reshape(r_all, c_all), v.reshape(r_all, c_all), landed[a], (0, 1, 2),
                         name=f"adamw_{nm}{idx}", tr=_adam_row_tile(r_all, c_all, 2 * 1024 * 1024), own=(parts[a], chip_idx),
                         stack=(idx, n_layers, res.get(nm)))

    after = small_own[4]
    for k in reversed(range(n_sub)):
        rs_end(k, after)
        layer, j = k // 2, k // 4
        if k % 2 == 1:
            adam_sharded("ffn_w_gu", k, 0, layer, depth, ffn_w_gu[layer], m_ffn_w_gu[layer], v_ffn_w_gu[layer])
            adam_sharded("ffn_w_down", k, 1, layer, depth, ffn_w_down[layer], m_ffn_w_down[layer], v_ffn_w_down[layer])
            after = res["ffn_w_down"][0]
        elif layer % 2 == 0:
            adam_sharded("mla_w_in", k, 0, j, n_mla, pad_cols(mla_w_in[j], w_in_cols), pad_cols(m_mla_w_in[j], w_in_cols),
                         pad_cols(v_mla_w_in[j], w_in_cols))
            adam_sharded("mla_w_uq", k, 1, j, n_mla, pad_heads(mla_w_uq[j]), pad_heads(m_mla_w_uq[j]), pad_heads(v_mla_w_uq[j]))
            adam_sharded("mla_w_ukv", k, 2, j, n_mla, mla_w_ukv[j], m_mla_w_ukv[j], v_mla_w_ukv[j])
            adam_sharded("mla_w_o", k, 3, j, n_mla, mla_w_o[j], m_mla_w_o[j], v_mla_w_o[j])
            after = res["mla_w_o"][0]
        else:
            adam_sharded("lru_w_in", k, 0, j, n_lru, lru_w_in[j], m_lru_w_in[j], v_lru_w_in[j])
            adam_sharded("lru_w_o", k, 1, j, n_lru, lru_w_o[j], m_lru_w_o[j], v_lru_w_o[j])
            after = res["lru_w_o"][0]
    res["mla_w_in"] = [t[:, :, :lq + lkv + QK_ROPE] for t in res["mla_w_in"]]
    res["mla_w_uq"] = [t.reshape(n_mla, lq, heads_local, HEAD_W)[:, :, :, :QK_NOPE + QK_ROPE].reshape(n_mla, lq, -1)
                       for t in res["mla_w_uq"]]

    _, small_lands = _exchange_wait(small_own, _ag_plan_own, after, name="ag_grads_wait")
    small_pass = _exchange_start([], small_lands, _ag_plan_pass, 3, name="ag_grads_pass")
    _, all_small = _exchange_wait(small_pass, _ag_plan_pass, after, name="ag_grads_pass_wait")
    slot_order = tuple(range(N_DEV))

    def adam_rep(terms, w, m, v, tag):
        r_pad, c_all = terms.shape[1], terms.shape[2]

        def prep(t):
            t2 = t.reshape(-1, c_all)
            return jnp.pad(t2, ((0, r_pad - t2.shape[0]), (0, 0)))

        outs = _adamw(prep(w), prep(m), prep(v), terms, slot_order, name=f"adamw_{tag}", tr=_adam_row_tile(r_pad, c_all))
        n_rows = w.size // c_all
        return [o[:n_rows].reshape(w.shape) for o in outs]

    small_w = pack_small(meta_tokens, lru_conv_w, lru_conv_b, lru_lambda)
    small_m = pack_small(m_meta_tokens, m_lru_conv_w, m_lru_conv_b, m_lru_lambda)
    small_v = pack_small(v_meta_tokens, v_lru_conv_w, v_lru_conv_b, v_lru_lambda)
    small_out = _adamw(small_w, small_m, small_v, all_small[0], slot_order, name="adamw_small", tr=small_pad, col_block=my_slot)
    small_out = [unpack_small(o) for o in small_out]
    for idx, key in enumerate(("meta_tokens", "lru_conv_w", "lru_conv_b", "lru_lambda")):
        res[key] = [small_out[k][idx] for k in range(4)]

    res["norm_mix"] = adam_rep(all_small[1], norm_mix, m_norm_mix, v_norm_mix, "norm_mix")
    res["norm_ffn"] = adam_rep(all_small[2], norm_ffn, m_norm_ffn, v_norm_ffn, "norm_ffn")
    res["norm_final"] = adam_rep(all_small[3], norm_final, m_norm_final, v_norm_final, "norm_final")
    res["mla_q_norm"] = adam_rep(all_small[4], mla_q_norm, m_mla_q_norm, v_mla_q_norm, "mla_q_norm")
    res["mla_kv_norm"] = adam_rep(all_small[5], mla_kv_norm, m_mla_kv_norm, v_mla_kv_norm, "mla_kv_norm")
    res["lru_w_gate_a"] = adam_rep(all_small[6], lru_w_gate_a, m_lru_w_gate_a, v_lru_w_gate_a, "lru_w_gate_a")
    res["lru_b_gate_a"] = adam_rep(all_small[7], lru_b_gate_a, m_lru_b_gate_a, v_lru_b_gate_a, "lru_b_gate_a")
    res["lru_w_gate_x"] = adam_rep(all_small[8], lru_w_gate_x, m_lru_w_gate_x, v_lru_w_gate_x, "lru_w_gate_x")
    res["lru_b_gate_x"] = adam_rep(all_small[9], lru_b_gate_x, m_lru_b_gate_x, v_lru_b_gate_x, "lru_b_gate_x")

    names = ["meta_tokens", "norm_mix", "norm_ffn", "norm_final", "mla_w_in", "mla_q_norm", "mla_kv_norm", "mla_w_uq",
             "mla_w_ukv", "mla_w_o", "lru_w_in", "lru_conv_w", "lru_conv_b", "lru_w_gate_a", "lru_b_gate_a", "lru_w_gate_x",
             "lru_b_gate_x", "lru_lambda", "lru_w_o", "ffn_w_gu", "ffn_w_down"]
    shapes = dict(meta_tokens=meta_tokens, norm_mix=norm_mix, norm_ffn=norm_ffn, norm_final=norm_final, mla_w_in=mla_w_in,
                  mla_q_norm=mla_q_norm, mla_kv_norm=mla_kv_norm, mla_w_uq=mla_w_uq, mla_w_ukv=mla_w_ukv, mla_w_o=mla_w_o,
                  lru_w_in=lru_w_in, lru_conv_w=lru_conv_w, lru_conv_b=lru_conv_b, lru_w_gate_a=lru_w_gate_a,
                  lru_b_gate_a=lru_b_gate_a, lru_w_gate_x=lru_w_gate_x, lru_b_gate_x=lru_b_gate_x, lru_lambda=lru_lambda,
                  lru_w_o=lru_w_o, ffn_w_gu=ffn_w_gu, ffn_w_down=ffn_w_down)
    outs = [loss, grad_x]
    for k in range(4):
        outs += [res[nm][k].reshape(shapes[nm].shape) for nm in names]
    return tuple(outs)
```

```python
import math

import jax
import jax.numpy as jnp
from jax import lax
from jax.experimental import pallas as pl
from jax.experimental.pallas import tpu as pltpu

F32 = jnp.float32
BF16 = jnp.bfloat16
MESH = pl.DeviceIdType.MESH

N_META = 16
CHUNK = 64
QK_NOPE = 128
QK_ROPE = 64
V_HEAD = 128
HEAD_W = 256
ROPE_THETA = 10000.0
LRU_C = 8.0
RMS_EPS = 1e-6
NEG_BIG = -1e30
ADAM_LR, ADAM_B1, ADAM_B2, ADAM_EPS, ADAM_WD, ADAM_STEP = 0.001, 0.9, 0.999, 1e-08, 0.01, 10

LANES = 128
SUBLANES = 8
VMEM_LIMIT_BYTES = 52 * 1024 * 1024
N_DEV = 8

_NT = (((1,), (1,)), ((), ()))
_TN = (((0,), (0,)), ((), ()))
_DIVS = (2048, 1024, 512, 256, 128)
_ROW_TILES = (1408, 1024, 512, 256, 128)


def _params(dims):
    return pltpu.CompilerParams(dimension_semantics=dims, vmem_limit_bytes=VMEM_LIMIT_BYTES)


def _pick(n, candidates):
    for c in candidates:
        if c <= n and n % c == 0:
            return c
    return n


def _sigmoid(z):
    return 0.5 + 0.5 * jnp.tanh(0.5 * z)


def _gelu(x):
    c = math.sqrt(2.0 / math.pi)
    return 0.5 * x * (1.0 + jnp.tanh(c * (x + 0.044715 * x * x * x)))


def _gelu_grad(x):
    c = math.sqrt(2.0 / math.pi)
    th = jnp.tanh(c * (x + 0.044715 * x * x * x))
    return 0.5 * (1.0 + th) + 0.5 * x * (1.0 - th * th) * c * (1.0 + 3.0 * 0.044715 * x * x)


def _neg_expm1(x):
    poly = -x * (1.0 + x * (1.0 / 2.0) * (1.0 + x * (1.0 / 3.0) * (1.0 + x * (1.0 / 4.0) * (
        1.0 + x * (1.0 / 5.0) * (1.0 + x * (1.0 / 6.0) * (1.0 + x * (1.0 / 7.0)))))))
    return jnp.where(x > -0.25, poly, 1.0 - jnp.exp(x))


def _softplus_neg(lam):
    e = jnp.exp(-jnp.abs(lam))
    log1p = jnp.where(e > 1e-4, jnp.log(1.0 + e), e * (1.0 - e * (0.5 - e * (1.0 / 3.0))))
    return jnp.maximum(-lam, 0.0) + log1p


def _rot_half(x):
    lane = lax.broadcasted_iota(jnp.int32, x.shape, 1)
    first = (lane % QK_ROPE) < (QK_ROPE // 2)
    return jnp.where(first, -pltpu.roll(x, LANES - QK_ROPE // 2, 1), pltpu.roll(x, QK_ROPE // 2, 1))


def _rope(x, cos, sin):
    return x * cos + _rot_half(x) * sin


def _unrope(g, cos, sin):
    return g * cos - _rot_half(g) * sin


def _grid_order(rows_outer):
    if not rows_outer:
        return lambda f: f
    return lambda f: (lambda i, j, k: f(j, i, k))


def _mm_nn(a, b, *, name, out_dtype, tm, tn, tk, b_blocked=False, res=None, epilogue=None, extras=(), rows_outer=False):
    m_all, k_all = a.shape
    om = _grid_order(rows_outer)
    if b_blocked:
        g_all, kb, nb = b.shape
        n_all = g_all * nb
        assert nb % tn == 0
        r = nb // tn
        b_spec = pl.BlockSpec((None, tk, tn), om(lambda j, i, k: (j // r, k, j % r)))
    else:
        kb, n_all = b.shape
        b_spec = pl.BlockSpec((tk, tn), om(lambda j, i, k: (k, j)))
    assert kb == k_all and m_all % tm == 0 and n_all % tn == 0 and k_all % tk == 0
    nm, nn, nk = m_all // tm, n_all // tn, k_all // tk
    in_specs = [pl.BlockSpec((tm, tk), om(lambda j, i, k: (i, k))), b_spec]
    operands = [a, b]
    has_res = res is not None
    if has_res:
        in_specs.append(pl.BlockSpec((tm, tn), om(lambda j, i, k: (i, j))))
        operands.append(res)
    for e in extras:
        in_specs.append(pl.BlockSpec((tm, e.shape[1]), om(lambda j, i, k: (i, 0))))
        operands.append(e)
    n_ex = len(extras)

    def body(*refs):
        a_ref, b_ref = refs[0], refs[1]
        pos = 2
        res_ref = None
        if has_res:
            res_ref = refs[pos]
            pos += 1
        ex_refs = refs[pos:pos + n_ex]
        pos += n_ex
        o_ref = refs[pos]
        acc_ref = refs[pos + 1] if nk > 1 else None

        def finish(acc):
            if has_res:
                acc = acc + res_ref[...]
            if epilogue is not None:
                acc = epilogue(acc, *ex_refs)
            o_ref[...] = acc.astype(o_ref.dtype)

        prod = jnp.dot(a_ref[...], b_ref[...], preferred_element_type=F32)
        if nk == 1:
            finish(prod)
        else:
            k = pl.program_id(2)

            @pl.when(k == 0)
            def _():
                acc_ref[...] = prod

            @pl.when(k > 0)
            def _():
                acc_ref[...] += prod

            @pl.when(k == nk - 1)
            def _():
                finish(acc_ref[...])

    return pl.pallas_call(
        body, name=name, grid=(nm, nn, nk) if rows_outer else (nn, nm, nk), in_specs=in_specs,
        out_specs=pl.BlockSpec((tm, tn), om(lambda j, i, k: (i, j))),
        out_shape=jax.ShapeDtypeStruct((m_all, n_all), out_dtype),
        scratch_shapes=[pltpu.VMEM((tm, tn), F32)] if nk > 1 else [],
        compiler_params=_params(("parallel", "parallel", "arbitrary")),
    )(*operands)


def _mm_nt(a, b, *, name, out_dtype, tm, tn, tk, b_blocked=False):
    if a.ndim == 3:
        n_planes, m_all, kp = a.shape
        k_all = n_planes * kp
    else:
        n_planes, (m_all, k_all) = 0, a.shape
    if b_blocked and tk == k_all and b.shape[0] > 1:
        g_all, n_all, nb = b.shape
        assert g_all * nb == k_all and m_all % tm == 0 and n_all % tn == 0
        per_plane = kp // nb if n_planes else 0

        def whole_body(a_ref, b_ref, o_ref):
            acc = None
            for g in range(g_all):
                a_g = a_ref[g // per_plane, :, (g % per_plane) * nb:(g % per_plane + 1) * nb] if n_planes else a_ref[:, g * nb:(g + 1) * nb]
                prod = lax.dot_general(a_g, b_ref[g], _NT, preferred_element_type=F32)
                acc = prod if acc is None else acc + prod
            o_ref[...] = acc.astype(o_ref.dtype)

        a_whole = (pl.BlockSpec((n_planes, tm, kp), lambda j, i: (0, i, 0)) if n_planes
                   else pl.BlockSpec((tm, k_all), lambda j, i: (i, 0)))
        return pl.pallas_call(
            whole_body, name=name, grid=(n_all // tn, m_all // tm),
            in_specs=[a_whole, pl.BlockSpec((g_all, tn, nb), lambda j, i: (0, j, 0))],
            out_specs=pl.BlockSpec((tm, tn), lambda j, i: (i, j)),
            out_shape=jax.ShapeDtypeStruct((m_all, n_all), out_dtype),
            compiler_params=_params(("parallel", "parallel")),
        )(a, b)
    if n_planes:
        assert kp % tk == 0
        rp = kp // tk
        a_spec = pl.BlockSpec((None, tm, tk), lambda j, i, k: (k // rp, i, k % rp))
    else:
        a_spec = pl.BlockSpec((tm, tk), lambda j, i, k: (i, k))
    if b_blocked:
        g_all, n_all, nb = b.shape
        assert g_all * nb == k_all and nb % tk == 0
        r = nb // tk
        b_spec = pl.BlockSpec((None, tn, tk), lambda j, i, k: (k // r, j, k % r))
    else:
        n_all, kb = b.shape
        assert kb == k_all
        b_spec = pl.BlockSpec((tn, tk), lambda j, i, k: (j, k))
    assert m_all % tm == 0 and n_all % tn == 0 and k_all % tk == 0
    nm, nn, nk = m_all // tm, n_all // tn, k_all // tk

    def body(a_ref, b_ref, o_ref, *scratch):
        prod = lax.dot_general(a_ref[...], b_ref[...], _NT, preferred_element_type=F32)
        if nk == 1:
            o_ref[...] = prod.astype(o_ref.dtype)
        else:
            acc_ref = scratch[0]
            k = pl.program_id(2)

            @pl.when(k == 0)
            def _():
                acc_ref[...] = prod

            @pl.when(k > 0)
            def _():
                acc_ref[...] += prod

            @pl.when(k == nk - 1)
            def _():
                o_ref[...] = acc_ref[...].astype(o_ref.dtype)

    return pl.pallas_call(
        body, name=name, grid=(nn, nm, nk),
        in_specs=[a_spec, b_spec],
        out_specs=pl.BlockSpec((tm, tn), lambda j, i, k: (i, j)),
        out_shape=jax.ShapeDtypeStruct((m_all, n_all), out_dtype),
        scratch_shapes=[pltpu.VMEM((tm, tn), F32)] if nk > 1 else [],
        compiler_params=_params(("parallel", "parallel", "arbitrary")),
    )(a, b)


def _mm_tn(a, b, *, name, out_dtype, tm, tn, tk, out_block=None, cols_outer=False):
    t_all, m_all = a.shape
    om = _grid_order(cols_outer)
    if b.ndim == 3:
        n_planes, tb, n_p = b.shape
        assert n_p % tn == 0
        rq = n_p // tn
        n_all = n_planes * n_p
        b_spec = pl.BlockSpec((None, tk, tn), om(lambda i, j, k: (j // rq, k, j % rq)))
    else:
        tb, n_all = b.shape
        b_spec = pl.BlockSpec((tk, tn), om(lambda i, j, k: (k, j)))
    assert tb == t_all and m_all % tm == 0 and n_all % tn == 0 and t_all % tk == 0
    nm, nn, nk = m_all // tm, n_all // tn, t_all // tk
    if out_block is None:
        out_shape = jax.ShapeDtypeStruct((m_all, n_all), out_dtype)
        out_spec = pl.BlockSpec((tm, tn), om(lambda i, j, k: (i, j)))
    else:
        assert out_block % tn == 0 and n_all % out_block == 0
        r = out_block // tn
        out_shape = jax.ShapeDtypeStruct((n_all // out_block, m_all, out_block), out_dtype)
        out_spec = pl.BlockSpec((None, tm, tn), om(lambda i, j, k: (j // r, i, j % r)))

    def body(a_ref, b_ref, o_ref, *scratch):
        prod = lax.dot_general(a_ref[...], b_ref[...], _TN, preferred_element_type=F32)
        if nk == 1:
            o_ref[...] = prod.astype(o_ref.dtype)
        else:
            acc_ref = scratch[0]
            k = pl.program_id(2)

            @pl.when(k == 0)
            def _():
                acc_ref[...] = prod

            @pl.when(k > 0)
            def _():
                acc_ref[...] += prod

            @pl.when(k == nk - 1)
            def _():
                o_ref[...] = acc_ref[...].astype(o_ref.dtype)

    return pl.pallas_call(
        body, name=name, grid=(nn, nm, nk) if cols_outer else (nm, nn, nk),
        in_specs=[pl.BlockSpec((tk, tm), om(lambda i, j, k: (k, i))), b_spec],
        out_specs=out_spec, out_shape=out_shape,
        scratch_shapes=[pltpu.VMEM((tm, tn), F32)] if nk > 1 else [],
        compiler_params=_params(("parallel", "parallel", "arbitrary")),
    )(a, b)


def _rmsnorm_fwd(x, g, *, name, tm):
    t_all, d = x.shape

    def body(x_ref, g_ref, o_ref):
        xv = x_ref[...]
        rstd = lax.rsqrt(jnp.mean(xv * xv, axis=-1, keepdims=True) + RMS_EPS)
        o_ref[...] = (xv * rstd * g_ref[...]).astype(o_ref.dtype)

    return pl.pallas_call(
        body, name=name, grid=(t_all // tm,),
        in_specs=[pl.BlockSpec((tm, d), lambda i: (i, 0)), pl.BlockSpec((1, d), lambda i: (0, 0))],
        out_specs=pl.BlockSpec((tm, d), lambda i: (i, 0)),
        out_shape=jax.ShapeDtypeStruct((t_all, d), BF16),
        compiler_params=_params(("parallel",)),
    )(x, g)


def _rms_bwd_math(dy, xv, g):
    rstd = lax.rsqrt(jnp.mean(xv * xv, axis=-1, keepdims=True) + RMS_EPS)
    xhat = xv * rstd
    dxh = dy * g
    dx = rstd * (dxh - xhat * jnp.mean(dxh * xhat, axis=-1, keepdims=True))
    return dx, jnp.sum(dy * xhat, axis=0, keepdims=True)


def _rmsnorm_bwd(dy, x, g, res, *, name, tm):
    t_all, d = x.shape

    def body(dy_ref, x_ref, g_ref, res_ref, dx_ref, dxb_ref, dg_ref):
        dx, dg = _rms_bwd_math(dy_ref[...], x_ref[...], g_ref[...])
        tot = res_ref[...] + dx
        dx_ref[...] = tot
        dxb_ref[...] = tot.astype(BF16)

        @pl.when(pl.program_id(0) == 0)
        def _():
            dg_ref[...] = dg

        @pl.when(pl.program_id(0) > 0)
        def _():
            dg_ref[...] += dg

    row = pl.BlockSpec((tm, d), lambda i: (i, 0))
    vec = pl.BlockSpec((1, d), lambda i: (0, 0))
    return pl.pallas_call(
        body, name=name, grid=(t_all // tm,),
        in_specs=[row, row, vec, row], out_specs=[row, row, vec],
        out_shape=[jax.ShapeDtypeStruct((t_all, d), F32), jax.ShapeDtypeStruct((t_all, d), BF16),
                   jax.ShapeDtypeStruct((1, d), F32)],
        compiler_params=_params(("arbitrary",)),
    )(dy, x, g, res)


def _loss_head(h, target, g, *, name, tm, n_real):
    t_all, d = h.shape

    def body(h_ref, t_ref, g_ref, loss_ref, dx_ref, dxb_ref, dg_ref):
        i = pl.program_id(0)
        xv = h_ref[...]
        gv = g_ref[...]
        rstd = lax.rsqrt(jnp.mean(xv * xv, axis=-1, keepdims=True) + RMS_EPS)
        y = xv * rstd * gv
        row = i * tm + lax.broadcasted_iota(jnp.int32, (tm, 1), 0)
        valid = (row >= N_META) & (row < n_real)
        err = jnp.where(valid, y - t_ref[...], 0.0)
        part = 0.5 * jnp.sum(jnp.mean(err * err, axis=-1, keepdims=True), axis=0, keepdims=True)
        dx, dg = _rms_bwd_math(err * (1.0 / d), xv, gv)
        dx_ref[...] = dx
        dxb_ref[...] = dx.astype(BF16)

        @pl.when(i == 0)
        def _():
            dg_ref[...] = dg
            loss_ref[...] = jnp.broadcast_to(part, loss_ref.shape)

        @pl.when(i > 0)
        def _():
            dg_ref[...] += dg
            loss_ref[...] += jnp.broadcast_to(part, loss_ref.shape)

    row = pl.BlockSpec((tm, d), lambda i: (i, 0))
    vec = pl.BlockSpec((1, d), lambda i: (0, 0))
    return pl.pallas_call(
        body, name=name, grid=(t_all // tm,),
        in_specs=[row, row, vec],
        out_specs=[pl.BlockSpec((1, LANES), lambda i: (0, 0)), row, row, vec],
        out_shape=[jax.ShapeDtypeStruct((1, LANES), F32), jax.ShapeDtypeStruct((t_all, d), F32),
                   jax.ShapeDtypeStruct((t_all, d), BF16), jax.ShapeDtypeStruct((1, d), F32)],
        compiler_params=_params(("arbitrary",)),
    )(h, target, g)


def _ffn_up(x, w_gu, *, name, tm):
    t_all, d = x.shape
    g_all, kb, nb = w_gu.shape
    half = g_all // 2
    f = half * nb
    assert kb == d and t_all % tm == 0

    def body(x_ref, wg_ref, wu_ref, gu_ref, act_ref):
        xv = x_ref[...]
        gv = jnp.dot(xv, wg_ref[...], preferred_element_type=F32)
        uv = jnp.dot(xv, wu_ref[...], preferred_element_type=F32)
        gu_ref[0] = gv.astype(gu_ref.dtype)
        gu_ref[1] = uv.astype(gu_ref.dtype)
        act_ref[...] = (gv * _sigmoid(gv) * uv).astype(act_ref.dtype)

    return pl.pallas_call(
        body, name=name, grid=(half, t_all // tm),
        in_specs=[pl.BlockSpec((tm, d), lambda j, i: (i, 0)), pl.BlockSpec((None, d, nb), lambda j, i: (j, 0, 0)),
                  pl.BlockSpec((None, d, nb), lambda j, i: (j + half, 0, 0))],
        out_specs=[pl.BlockSpec((2, tm, nb), lambda j, i: (0, i, j)), pl.BlockSpec((tm, nb), lambda j, i: (i, j))],
        out_shape=[jax.ShapeDtypeStruct((2, t_all, f), BF16), jax.ShapeDtypeStruct((t_all, f), BF16)],
        compiler_params=_params(("parallel", "parallel")),
    )(x, w_gu, w_gu)


def _ffn_dact(dy, w_down, gu, *, name, tm, tn):
    t_all, d = dy.shape
    f = w_down.shape[0]
    assert t_all % tm == 0 and f % tn == 0

    def body(dy_ref, w_ref, gu_ref, o_ref):
        dact = lax.dot_general(dy_ref[...], w_ref[...], _NT, preferred_element_type=F32)
        gv, uv = gu_ref[0].astype(F32), gu_ref[1].astype(F32)
        sg = _sigmoid(gv)
        o_ref[0] = (dact * uv * (sg * (1.0 + gv * (1.0 - sg)))).astype(o_ref.dtype)
        o_ref[1] = (dact * gv * sg).astype(o_ref.dtype)

    return pl.pallas_call(
        body, name=name, grid=(f // tn, t_all // tm),
        in_specs=[pl.BlockSpec((tm, d), lambda j, i: (i, 0)), pl.BlockSpec((tn, d), lambda j, i: (j, 0)),
                  pl.BlockSpec((2, tm, tn), lambda j, i: (0, i, j))],
        out_specs=pl.BlockSpec((2, tm, tn), lambda j, i: (0, i, j)),
        out_shape=jax.ShapeDtypeStruct((2, t_all, f), BF16),
        compiler_params=_params(("parallel", "parallel")),
    )(dy, w_down, gu)


def _mla_prep_fwd(proj, qn, kvn, cos, sin, *, name, tm, lq, lkv):
    t_all, w = proj.shape

    def body(p_ref, qn_ref, kvn_ref, cos_ref, sin_ref, cq_ref, ckv_ref, kr_ref):
        pv = p_ref[...]
        xq = pv[:, :lq]
        xkv = pv[:, lq:lq + lkv]
        cq_ref[...] = (xq * lax.rsqrt(jnp.mean(xq * xq, axis=-1, keepdims=True) + RMS_EPS) * qn_ref[...]).astype(BF16)
        ckv_ref[...] = (xkv * lax.rsqrt(jnp.mean(xkv * xkv, axis=-1, keepdims=True) + RMS_EPS) * kvn_ref[...]).astype(BF16)
        kr_ref[...] = _rope(pv[:, lq + lkv:], cos_ref[...], sin_ref[...]).astype(BF16)

    def row(width):
        return pl.BlockSpec((tm, width), lambda i: (i, 0))

    def vec(width):
        return pl.BlockSpec((1, width), lambda i: (0, 0))

    return pl.pallas_call(
        body, name=name, grid=(t_all // tm,),
        in_specs=[row(w), vec(lq), vec(lkv), row(LANES), row(LANES)],
        out_specs=[row(lq), row(lkv), row(LANES)],
        out_shape=[jax.ShapeDtypeStruct((t_all, lq), BF16), jax.ShapeDtypeStruct((t_all, lkv), BF16),
                   jax.ShapeDtypeStruct((t_all, LANES), BF16)],
        compiler_params=_params(("parallel",)),
    )(proj, qn, kvn, cos, sin)


def _mla_prep_bwd(dcq, dckv, dkr_h, proj, qn, kvn, cos, sin, *, name, tm, lq, lkv):
    t_all, w = proj.shape
    n_heads = dkr_h.shape[0]

    def body(dcq_ref, dckv_ref, dkr_ref, p_ref, qn_ref, kvn_ref, cos_ref, sin_ref, dp_ref, dqn_ref, dkvn_ref):
        pv = p_ref[...]
        dxq, dqn = _rms_bwd_math(dcq_ref[...], pv[:, :lq], qn_ref[...])
        dxkv, dkvn = _rms_bwd_math(dckv_ref[...], pv[:, lq:lq + lkv], kvn_ref[...])
        dkr = dkr_ref[0]
        for hh in range(1, n_heads):
            dkr = dkr + dkr_ref[hh]
        dkr = _unrope(dkr, cos_ref[...], sin_ref[...])
        dp_ref[...] = jnp.concatenate([dxq, dxkv, dkr], axis=1).astype(BF16)

        @pl.when(pl.program_id(0) == 0)
        def _():
            dqn_ref[...] = dqn
            dkvn_ref[...] = dkvn

        @pl.when(pl.program_id(0) > 0)
        def _():
            dqn_ref[...] += dqn
            dkvn_ref[...] += dkvn

    def row(width):
        return pl.BlockSpec((tm, width), lambda i: (i, 0))

    def vec(width):
        return pl.BlockSpec((1, width), lambda i: (0, 0))

    return pl.pallas_call(
        body, name=name, grid=(t_all // tm,),
        in_specs=[row(lq), row(lkv), pl.BlockSpec((n_heads, tm, LANES), lambda i: (0, i, 0)), row(w),
                  vec(lq), vec(lkv), row(LANES), row(LANES)],
        out_specs=[row(w), vec(lq), vec(lkv)],
        out_shape=[jax.ShapeDtypeStruct((t_all, w), BF16), jax.ShapeDtypeStruct((1, lq), F32),
                   jax.ShapeDtypeStruct((1, lkv), F32)],
        compiler_params=_params(("arbitrary",)),
    )(dcq, dckv, dkr_h, proj, qn, kvn, cos, sin)


def _rope_q_epilogue(acc, cos_ref, sin_ref):
    parts = []
    for g in range(acc.shape[1] // LANES):
        blk = acc[:, g * LANES:(g + 1) * LANES]
        parts.append(_rope(blk, cos_ref[...], sin_ref[...]) if g % 2 == 1 else blk)
    return jnp.concatenate(parts, axis=1)


def _chunk_causal(rows, cols, row0=0):
    r = row0 + lax.broadcasted_iota(jnp.int32, (rows, cols), 0)
    c = lax.broadcasted_iota(jnp.int32, (rows, cols), 1)
    return (c >> 6) <= (r >> 6)


def _meta_keys(rows, cols):
    return lax.broadcasted_iota(jnp.int32, (rows, cols), 1) < N_META


def _attn_fwd(q, kv, kr, *, name, n_heads, tq, n_real, scale):
    t_all = q.shape[0]
    nq = (n_real - N_META) // tq
    assert N_META + nq * tq == n_real and tq % CHUNK == 0 and t_all >= LANES
    n_pad = t_all - n_real
    sub = tq // 2 if (tq // 2) % CHUNK == 0 else tq

    def body(q_ref, kv_ref, kr_ref, o_ref, lse_ref, k_scr, m_scr, l_scr, acc_scr):
        k_scr[:, :QK_NOPE] = kv_ref[:, :QK_NOPE]
        k_scr[:, QK_NOPE:] = kr_ref[...]
        if n_pad:
            o_ref[pl.ds(n_real, n_pad), :] = jnp.zeros((n_pad, V_HEAD), o_ref.dtype)
            lse_ref[pl.ds(n_real, n_pad), :] = jnp.zeros((n_pad, LANES), F32)

        def scores(qt, c0, width):
            return lax.dot_general(qt, k_scr[pl.ds(c0, width), :], _NT, preferred_element_type=F32) * scale

        def values(c0, width):
            return kv_ref[pl.ds(c0, width), QK_NOPE:]

        s = jnp.where(_meta_keys(LANES, LANES), scores(q_ref[pl.ds(0, LANES), :], 0, LANES), NEG_BIG)
        m = jnp.max(s, axis=-1, keepdims=True)
        p = jnp.exp(s - m)
        l = jnp.sum(p, axis=-1, keepdims=True)
        o_meta = jnp.dot(p.astype(BF16), values(0, LANES), preferred_element_type=F32) / l
        o_ref[pl.ds(0, N_META), :] = o_meta[:N_META].astype(o_ref.dtype)
        lse_ref[pl.ds(0, N_META), :] = jnp.broadcast_to((m + jnp.log(l))[:N_META], (N_META, LANES))

        parts = [(u * sub, sub) for u in range(tq // sub)]

        def accumulate(u0, s, vals):
            rows = pl.ds(u0, s.shape[0])
            m_prev = m_scr[rows, :]
            m_new = jnp.maximum(m_prev, jnp.max(s, axis=-1, keepdims=True))
            alpha = jnp.exp(m_prev - m_new)
            p = jnp.exp(s - m_new)
            l_scr[rows, :] = alpha * l_scr[rows, :] + jnp.sum(p, axis=-1, keepdims=True)
            acc_scr[rows, :] = alpha * acc_scr[rows, :] + jnp.dot(p.astype(BF16), vals, preferred_element_type=F32)
            m_scr[rows, :] = m_new

        def q_tile(i, carry):
            r0 = pl.multiple_of(N_META + i * tq, N_META)
            qts = [q_ref[pl.ds(r0 + u0, rows), :] for u0, rows in parts]
            m_scr[...] = jnp.full(m_scr.shape, NEG_BIG, F32)
            l_scr[...] = jnp.zeros(l_scr.shape, F32)
            acc_scr[...] = jnp.zeros(acc_scr.shape, F32)

            def full_blocks(j, width):
                c0 = pl.multiple_of(N_META + j * tq, N_META)
                for (u0, _), qt in zip(parts, qts):
                    accumulate(u0, scores(qt, c0, width), values(c0, width))

            def two_blocks(jj, c):
                full_blocks(2 * jj, 2 * tq)
                return c

            lax.fori_loop(0, i // 2, two_blocks, 0)

            @pl.when(i % 2 == 1)
            def _():
                full_blocks(i - 1, tq)

            for (u0, rows), qt in zip(parts, qts):
                width = u0 + rows
                s = jnp.concatenate([jnp.where(_meta_keys(rows, LANES), scores(qt, 0, LANES), NEG_BIG),
                                     jnp.where(_chunk_causal(rows, width, u0), scores(qt, r0, width), NEG_BIG)], axis=1)
                accumulate(u0, s, jnp.concatenate([values(0, LANES), values(r0, width)], axis=0))
            o_ref[pl.ds(r0, tq), :] = (acc_scr[...] / l_scr[...]).astype(o_ref.dtype)
            lse_ref[pl.ds(r0, tq), :] = jnp.broadcast_to(m_scr[...] + jnp.log(l_scr[...]), (tq, LANES))
            return carry

        lax.fori_loop(0, nq, q_tile, 0)

    def head(width):
        return pl.BlockSpec((t_all, width), lambda h: (0, h))

    return pl.pallas_call(
        body, name=name, grid=(n_heads,),
        in_specs=[head(HEAD_W), head(HEAD_W), pl.BlockSpec((t_all, LANES), lambda h: (0, 0))],
        out_specs=[head(V_HEAD), pl.BlockSpec((None, t_all, LANES), lambda h: (h, 0, 0))],
        out_shape=[jax.ShapeDtypeStruct((t_all, n_heads * V_HEAD), BF16),
                   jax.ShapeDtypeStruct((n_heads, t_all, LANES), F32)],
        scratch_shapes=[pltpu.VMEM((t_all, HEAD_W), BF16), pltpu.VMEM((tq, 1), F32), pltpu.VMEM((tq, 1), F32),
                        pltpu.VMEM((tq, V_HEAD), F32)],
        compiler_params=_params(("parallel",)),
    )(q, kv, kr)


def _attn_bwd(q, kv, kr, o, lse, do, cos, sin, *, name, n_heads, tq, n_real, scale):
    t_all = q.shape[0]
    nq = (n_real - N_META) // tq
    assert N_META + nq * tq == n_real and tq % CHUNK == 0 and t_all >= LANES
    n_pad = t_all - n_real

    def body(q_ref, kv_ref, kr_ref, o_ref, lse_ref, do_ref, cos_ref, sin_ref, dq_ref, dkv_ref, dkr_ref,
             k_scr, dk_scr, dv_scr, dq_scr):
        k_scr[:, :QK_NOPE] = kv_ref[:, :QK_NOPE]
        k_scr[:, QK_NOPE:] = kr_ref[...]
        dk_scr[...] = jnp.zeros(dk_scr.shape, F32)
        dv_scr[...] = jnp.zeros(dv_scr.shape, F32)
        if n_pad:
            dq_ref[pl.ds(n_real, n_pad), :] = jnp.zeros((n_pad, HEAD_W), dq_ref.dtype)

        def blocks(qt, dot, lse_t, delta, segments):
            kb = jnp.concatenate([k_scr[pl.ds(c0, w), :] for c0, w, _ in segments], axis=0)
            vb = jnp.concatenate([kv_ref[pl.ds(c0, w), QK_NOPE:] for c0, w, _ in segments], axis=0)
            s = lax.dot_general(qt, kb, _NT, preferred_element_type=F32) * scale
            p = jnp.exp(s - lse_t)
            if any(m is not None for _, _, m in segments):
                rows = qt.shape[0]
                mask = jnp.concatenate([jnp.ones((rows, w), jnp.bool_) if m is None else m for _, w, m in segments], axis=1)
                p = jnp.where(mask, p, 0.0)
            dp = lax.dot_general(dot, vb, _NT, preferred_element_type=F32)
            ds = (p * (dp - delta) * scale).astype(BF16)
            dv = lax.dot_general(p.astype(BF16), dot, _TN, preferred_element_type=F32)
            dk = lax.dot_general(ds, qt, _TN, preferred_element_type=F32)
            at = 0
            for c0, w, _ in segments:
                dv_scr[pl.ds(c0, w), :] += dv[at:at + w]
                dk_scr[pl.ds(c0, w), :] += dk[at:at + w]
                at += w
            return jnp.dot(ds, kb, preferred_element_type=F32)

        def block(qt, dot, lse_t, delta, c0, width, mask):
            return blocks(qt, dot, lse_t, delta, [(c0, width, mask)])

        def write_dq(r0, rows, dq):
            cs, sn = cos_ref[pl.ds(r0, rows), :], sin_ref[pl.ds(r0, rows), :]
            dq_ref[pl.ds(r0, rows), :] = jnp.concatenate(
                [dq[:, :QK_NOPE], _unrope(dq[:, QK_NOPE:], cs, sn)], axis=1).astype(dq_ref.dtype)

        rows_m = lax.broadcasted_iota(jnp.int32, (LANES, LANES), 0) < N_META
        dot = do_ref[pl.ds(0, LANES), :]
        delta = jnp.sum(dot.astype(F32) * o_ref[pl.ds(0, LANES), :].astype(F32), axis=-1, keepdims=True)
        dq = block(q_ref[pl.ds(0, LANES), :], dot, lse_ref[pl.ds(0, LANES), :1], delta, 0, LANES,
                   _meta_keys(LANES, LANES) & rows_m)
        write_dq(0, N_META, dq[:N_META])

        def q_tile(i, carry):
            r0 = pl.multiple_of(N_META + i * tq, N_META)
            qt = q_ref[pl.ds(r0, tq), :]
            dot = do_ref[pl.ds(r0, tq), :]
            lse_t = lse_ref[pl.ds(r0, tq), :1]
            delta = jnp.sum(dot.astype(F32) * o_ref[pl.ds(r0, tq), :].astype(F32), axis=-1, keepdims=True)
            dq_scr[...] = blocks(qt, dot, lse_t, delta, [(0, LANES, _meta_keys(tq, LANES)), (r0, tq, _chunk_causal(tq, tq))])

            def two_blocks(jj, c):
                c0 = pl.multiple_of(N_META + 2 * jj * tq, N_META)
                dq_scr[...] += block(qt, dot, lse_t, delta, c0, 2 * tq, None)
                return c

            lax.fori_loop(0, i // 2, two_blocks, 0)

            @pl.when(i % 2 == 1)
            def _():
                c0 = pl.multiple_of(N_META + (i - 1) * tq, N_META)
                dq_scr[...] += block(qt, dot, lse_t, delta, c0, tq, None)

            write_dq(r0, tq, dq_scr[...])
            return carry

        lax.fori_loop(0, nq, q_tile, 0)
        dk = dk_scr[...]
        dkv_ref[...] = jnp.concatenate([dk[:, :QK_NOPE], dv_scr[...]], axis=1).astype(dkv_ref.dtype)
        dkr_ref[...] = dk[:, QK_NOPE:]

    def head(width):
        return pl.BlockSpec((t_all, width), lambda h: (0, h))

    table = pl.BlockSpec((t_all, LANES), lambda h: (0, 0))
    per_head = pl.BlockSpec((None, t_all, LANES), lambda h: (h, 0, 0))
    return pl.pallas_call(
        body, name=name, grid=(n_heads,),
        in_specs=[head(HEAD_W), head(HEAD_W), table, head(V_HEAD), per_head, head(V_HEAD), table, table],
        out_specs=[head(HEAD_W), head(HEAD_W), per_head],
        out_shape=[jax.ShapeDtypeStruct((t_all, n_heads * HEAD_W), BF16), jax.ShapeDtypeStruct((t_all, n_heads * HEAD_W), BF16),
                   jax.ShapeDtypeStruct((n_heads, t_all, LANES), F32)],
        scratch_shapes=[pltpu.VMEM((t_all, HEAD_W), BF16), pltpu.VMEM((t_all, HEAD_W), F32), pltpu.VMEM((t_all, V_HEAD), F32),
                        pltpu.VMEM((tq, HEAD_W), F32)],
        compiler_params=_params(("parallel",)),
    )(q, kv, kr, o, lse, do, cos, sin)


LRU_ROWS = 128


def _shifted_back(ref, t0, rows, shift_max):
    main = ref[pl.ds(t0, rows), :]
    prev = ref[pl.ds(pl.multiple_of(jnp.maximum(t0 - SUBLANES, 0), SUBLANES), SUBLANES), :]
    prev = jnp.where(t0 > 0, prev, 0.0)
    ext = jnp.concatenate([prev, main], axis=0)
    return [main] + [pltpu.roll(ext, s, 0)[SUBLANES:, :] for s in range(1, shift_max + 1)]


def _shifted_ahead(ref, t0, rows, t_all, shift_max):
    main = ref[pl.ds(t0, rows), :]
    nxt = ref[pl.ds(pl.multiple_of(jnp.minimum(t0 + rows, t_all - SUBLANES), SUBLANES), SUBLANES), :]
    nxt = jnp.where(t0 + rows < t_all, nxt, 0.0)
    ext = jnp.concatenate([main, nxt], axis=0)
    return [main] + [pltpu.roll(ext, rows + SUBLANES - s, 0)[:rows, :] for s in range(1, shift_max + 1)]


def _conv_fwd(xp_ref, t0, rows, cw, cb):
    sh = _shifted_back(xp_ref, t0, rows, 3)
    out = cb + cw[3:4, :] * sh[0]
    for k in range(3):
        out = out + cw[k:k + 1, :] * sh[3 - k]
    return out, sh


def _lru_gates(xb, wga, bga, wgx, bgx, sp):
    xbb = xb.astype(BF16)
    r = _sigmoid(jnp.dot(xbb, wga, preferred_element_type=F32) + bga)
    ig = _sigmoid(jnp.dot(xbb, wgx, preferred_element_type=F32) + bgx)
    la = -LRU_C * r * sp
    a = jnp.exp(la)
    s = jnp.sqrt(_neg_expm1(2.0 * la))
    return xbb, r, ig, a, s


def _scan_tile(a, b, reverse):
    rows = a.shape[0]
    ridx = lax.broadcasted_iota(jnp.int32, a.shape, 0)
    s = 1
    while s < rows:
        if reverse:
            keep = ridx < rows - s
            a_sh, b_sh = pltpu.roll(a, rows - s, 0), pltpu.roll(b, rows - s, 0)
        else:
            keep = ridx >= s
            a_sh, b_sh = pltpu.roll(a, s, 0), pltpu.roll(b, s, 0)
        b = jnp.where(keep, a * b_sh + b, b)
        a = jnp.where(keep, a * a_sh, a)
        s *= 2
    return a, b


def _lru_fwd(xy, conv_w, conv_b, wga, bga, wgx, bgx, lam, *, name):
    t_all = xy.shape[0]
    dr = xy.shape[1] // 2
    c = LANES
    nblk = dr // c
    rows = LRU_ROWS
    nt = t_all // rows

    def body(xp_ref, yp_ref, cw_ref, cb_ref, wga_ref, bga_ref, wgx_ref, bgx_ref, lam_ref, hs_ref, hsy_ref):
        cw, cb = cw_ref[...], cb_ref[...]
        sp = _softplus_neg(lam_ref[...])

        def tile(t, h_in):
            t0 = pl.multiple_of(t * rows, rows)
            xb, _ = _conv_fwd(xp_ref, t0, rows, cw, cb)
            _, _, ig, a, s = _lru_gates(xb, wga_ref[0], bga_ref[...], wgx_ref[0], bgx_ref[...], sp)
            cum_a, h0 = _scan_tile(a, s * (ig * xb), reverse=False)
            hs = cum_a * h_in + h0
            hs_ref[pl.ds(t0, rows), :] = hs
            hsy_ref[pl.ds(t0, rows), :] = (hs * _gelu(yp_ref[pl.ds(t0, rows), :])).astype(BF16)
            return hs[rows - 1:, :]

        lax.fori_loop(0, nt, tile, jnp.zeros((1, c), F32))

    col = pl.BlockSpec((t_all, c), lambda b: (0, b))
    vec = pl.BlockSpec((1, c), lambda b: (0, b))
    wsp = pl.BlockSpec((1, c, c), lambda b: (b, 0, 0))
    return pl.pallas_call(
        body, name=name, grid=(nblk,),
        in_specs=[col, pl.BlockSpec((t_all, c), lambda b: (0, nblk + b)), pl.BlockSpec((4, c), lambda b: (0, b)), vec,
                  wsp, vec, wsp, vec, vec],
        out_specs=[col, col],
        out_shape=[jax.ShapeDtypeStruct((t_all, dr), F32), jax.ShapeDtypeStruct((t_all, dr), BF16)],
        compiler_params=_params(("parallel",)),
    )(xy, xy, conv_w, conv_b, wga, bga, wgx, bgx, lam)


def _lru_bwd(xy, hs, dhsy, conv_w, conv_b, wga, bga, wgx, bgx, lam, *, name):
    t_all = xy.shape[0]
    dr = xy.shape[1] // 2
    c = LANES
    nblk = dr // c
    rows = LRU_ROWS
    nt = t_all // rows

    def body(xp_ref, yp_ref, hs_ref, dh_ref, cw_ref, cb_ref, wga_ref, bga_ref, wgx_ref, bgx_ref, lam_ref,
             dxp_ref, dyp_ref, dcw_ref, dcb_ref, dwga_ref, dbga_ref, dwgx_ref, dbgx_ref, dlam_ref,
             xb_scr, r_scr, i_scr, a_scr):
        cw, cb = cw_ref[...], cb_ref[...]
        lamv = lam_ref[...]
        sp = _softplus_neg(lamv)
        sig_neg = 1.0 / (1.0 + jnp.exp(lamv))
        wga_v, wgx_v = wga_ref[0], wgx_ref[0]

        def recompute(t, carry):
            t0 = pl.multiple_of(t * rows, rows)
            xb, _ = _conv_fwd(xp_ref, t0, rows, cw, cb)
            _, r, ig, a, _ = _lru_gates(xb, wga_v, bga_ref[...], wgx_v, bgx_ref[...], sp)
            xb_scr[pl.ds(t0, rows), :] = xb
            r_scr[pl.ds(t0, rows), :] = r
            i_scr[pl.ds(t0, rows), :] = ig
            a_scr[pl.ds(t0, rows), :] = a
            return carry

        lax.fori_loop(0, nt, recompute, 0)
        dwga_ref[...] = jnp.zeros(dwga_ref.shape, F32)
        dwgx_ref[...] = jnp.zeros(dwgx_ref.shape, F32)

        def tile(ti, carry):
            lam_in, dbga, dbgx, dlam, dcw, dcb = carry
            t = nt - 1 - ti
            t0 = pl.multiple_of(t * rows, rows)
            a_now, a_next = _shifted_ahead(a_scr, t0, rows, t_all, 1)
            yp = yp_ref[pl.ds(t0, rows), :]
            dhy = dh_ref[pl.ds(t0, rows), :]
            cum_a, lam0 = _scan_tile(a_next, dhy * _gelu(yp), reverse=True)
            lam_t = cum_a * lam_in + lam0
            hs_now, hs_prev = _shifted_back(hs_ref, t0, rows, 1)
            da = lam_t * hs_prev
            xb = xb_scr[pl.ds(t0, rows), :]
            r = r_scr[pl.ds(t0, rows), :]
            ig = i_scr[pl.ds(t0, rows), :]
            la = -LRU_C * r * sp
            s = jnp.sqrt(_neg_expm1(2.0 * la))
            d_ixb = lam_t * s
            dla = da * a_now - (lam_t * ig * xb) * (a_now * a_now / s)
            dzr = dla * (-LRU_C * sp) * r * (1.0 - r)
            dzi = d_ixb * xb * ig * (1.0 - ig)
            dzr_b, dzi_b = dzr.astype(BF16), dzi.astype(BF16)
            xbb = xb.astype(BF16)
            dwga_ref[0] += lax.dot_general(xbb, dzr_b, _TN, preferred_element_type=F32)
            dwgx_ref[0] += lax.dot_general(xbb, dzi_b, _TN, preferred_element_type=F32)
            dxb = (d_ixb * ig + lax.dot_general(dzr_b, wga_v, _NT, preferred_element_type=F32)
                   + lax.dot_general(dzi_b, wgx_v, _NT, preferred_element_type=F32))
            xb_scr[pl.ds(t0, rows), :] = dxb
            dyp_ref[pl.ds(t0, rows), :] = (dhy * hs_now * _gelu_grad(yp)).astype(BF16)
            ahead = _shifted_ahead(xb_scr, t0, rows, t_all, 3)
            dxp = cw[3:4, :] * ahead[0]
            for k in range(3):
                dxp = dxp + cw[k:k + 1, :] * ahead[3 - k]
            dxp_ref[pl.ds(t0, rows), :] = dxp.astype(BF16)
            back = _shifted_back(xp_ref, t0, rows, 3)
            dcw_t = jnp.concatenate([jnp.sum(dxb * back[3 - k], axis=0, keepdims=True) for k in range(4)], axis=0)
            return (lam_t[:1, :], dbga + jnp.sum(dzr, axis=0, keepdims=True), dbgx + jnp.sum(dzi, axis=0, keepdims=True),
                    dlam + jnp.sum(dla * r, axis=0, keepdims=True), dcw + dcw_t, dcb + jnp.sum(dxb, axis=0, keepdims=True))

        zero = jnp.zeros((1, c), F32)
        _, dbga, dbgx, dlam, dcw, dcb = lax.fori_loop(0, nt, tile, (zero, zero, zero, zero, jnp.zeros((4, c), F32), zero))
        dbga_ref[...] = dbga
        dbgx_ref[...] = dbgx
        dlam_ref[...] = dlam * (LRU_C * sig_neg)
        dcw_ref[...] = dcw
        dcb_ref[...] = dcb

    col = pl.BlockSpec((t_all, c), lambda b: (0, b))
    col2 = pl.BlockSpec((t_all, c), lambda b: (0, nblk + b))
    vec = pl.BlockSpec((1, c), lambda b: (0, b))
    tap = pl.BlockSpec((4, c), lambda b: (0, b))
    wsp = pl.BlockSpec((1, c, c), lambda b: (b, 0, 0))
    vshape = jax.ShapeDtypeStruct((1, dr), F32)
    wshape = jax.ShapeDtypeStruct((nblk, c, c), F32)
    def planes_body(*refs):
        dxy_ref = refs[11]
        body(*refs[:11], dxy_ref.at[0], dxy_ref.at[1], *refs[12:])

    return pl.pallas_call(
        planes_body, name=name, grid=(nblk,),
        in_specs=[col, col2, col, col, tap, vec, wsp, vec, wsp, vec, vec],
        out_specs=[pl.BlockSpec((2, t_all, c), lambda b: (0, 0, b)), tap, vec, wsp, vec, wsp, vec, vec],
        out_shape=[jax.ShapeDtypeStruct((2, t_all, dr), BF16),
                   jax.ShapeDtypeStruct((4, dr), F32), vshape, wshape, vshape, wshape, vshape, vshape],
        scratch_shapes=[pltpu.VMEM((t_all, c), F32)] * 4,
        compiler_params=_params(("parallel",)),
    )(xy, xy, hs, dhsy, conv_w, conv_b, wga, bga, wgx, bgx, lam)


def _mesh_pos():
    return lax.axis_index("x"), lax.axis_index("y"), lax.axis_index("c")


def _all_gather(shards, *, name):
    n = len(shards)

    def body(*refs):
        ins, outs, token = refs[:n], refs[n:2 * n], refs[2 * n]
        send_sems, recv_sems, local_sems = refs[2 * n + 1:]
        token[...] = jnp.zeros(token.shape, token.dtype)
        x, y, c = _mesh_pos()
        me, sibling = (x, y, c), (x, y, 1 - c)
        chips = [(1 - x, y), (x, 1 - y), (1 - x, 1 - y)]
        slot = _slot

        def copy(a, k, block, to, src=None):
            dst = outs[a].at[slot(block)]
            return pltpu.make_async_remote_copy(
                src_ref=dst if src is None else src, dst_ref=dst, send_sem=send_sems.at[a, k],
                recv_sem=recv_sems.at[a, k], device_id=to, device_id_type=MESH)

        mine = [pltpu.make_async_copy(ins[a], outs[a].at[slot(me)], local_sems.at[a]) for a in range(n)]
        for cp in mine:
            cp.start()
        first = []
        for a in range(n):
            first.append(copy(a, 0, me, sibling, src=ins[a]))
            first += [copy(a, 1 + j, me, (*chip, c), src=ins[a]) for j, chip in enumerate(chips)]
        for cp in first:
            cp.start()
        passed = []
        for a in range(n):
            for j, chip in enumerate(chips):
                copy(a, 1 + j, (*chip, c), me).wait_recv()
                fwd = copy(a, 4 + j, (*chip, c), sibling)
                fwd.start()
                passed.append(fwd)
        for a in range(n):
            copy(a, 0, sibling, me).wait_recv()
            for j, chip in enumerate(chips):
                copy(a, 4 + j, (*chip, 1 - c), me).wait_recv()
        for cp in first + passed:
            cp.wait_send()
        for cp in mine:
            cp.wait()

    any_spec = pl.BlockSpec(memory_space=pl.ANY)
    outs = pl.pallas_call(
        body, name=name,
        in_specs=[any_spec] * n, out_specs=[any_spec] * n + [pl.BlockSpec(memory_space=pltpu.VMEM)],
        out_shape=[jax.ShapeDtypeStruct((N_DEV,) + s.shape, s.dtype) for s in shards]
        + [jax.ShapeDtypeStruct((SUBLANES, LANES), F32)],
        scratch_shapes=[pltpu.SemaphoreType.DMA((n, 7)), pltpu.SemaphoreType.DMA((n, 7)), pltpu.SemaphoreType.DMA((n,))],
    )(*shards)
    return list(outs[:n]), outs[n][0, 0]


def _copy_rows(pieces, n_rows, *, name):
    d, dtype = pieces[0][0].shape[1], pieces[0][0].dtype
    assert sorted((p[2], p[2] + p[3]) for p in pieces)[0][0] == 0 and sum(p[3] for p in pieces) == n_rows
    n = len(pieces)

    def body(*refs):
        out, sems = refs[n], refs[n + 1]
        copies = [pltpu.make_async_copy(refs[a].at[pl.ds(src, rows), :], out.at[pl.ds(dst, rows), :], sems.at[a])
                  for a, (_, src, dst, rows) in enumerate(pieces)]
        for cp in copies:
            cp.start()
        for cp in copies:
            cp.wait()

    any_spec = pl.BlockSpec(memory_space=pl.ANY)
    return pl.pallas_call(
        body, name=name, in_specs=[any_spec] * n, out_specs=any_spec,
        out_shape=jax.ShapeDtypeStruct((n_rows, d), dtype),
        scratch_shapes=[pltpu.SemaphoreType.DMA((n,))],
    )(*[p[0] for p in pieces])


_HBM = pl.BlockSpec(memory_space=pltpu.HBM)
_SEM = pl.BlockSpec(memory_space=pltpu.SEMAPHORE)
_ANY = pl.BlockSpec(memory_space=pl.ANY)
_EFFECT = pltpu.SideEffectType.DATAFLOW_SIDE_EFFECTING


def _slot(p):
    return 4 * p[0] + 2 * p[1] + p[2]


def _remote(src, dst, send, recv, idx, to):
    return pltpu.make_async_remote_copy(src_ref=src, dst_ref=dst, send_sem=send.at[idx], recv_sem=recv.at[idx],
                                        device_id=to, device_id_type=MESH)


def _ag_plan_own(a, src, land, send, recv):
    x, y, c = _mesh_pos()
    dst = land.at[_slot((x, y, c))]
    targets = [(x, y, 1 - c), (1 - x, y, c), (x, 1 - y, c), (1 - x, 1 - y, c)]
    return [_remote(src, dst, send, recv, 4 * a + k, to) for k, to in enumerate(targets)]


def _ag_plan_pass(a, src, land, send, recv):
    x, y, c = _mesh_pos()
    blocks = [land.at[_slot((px, py, c))] for px, py in ((1 - x, y), (x, 1 - y), (1 - x, 1 - y))]
    return [_remote(blk, blk, send, recv, 3 * a + k, (x, y, 1 - c)) for k, blk in enumerate(blocks)]


def _rs_plan_sibling(a, src, land, send, recv):
    x, y, c = _mesh_pos()
    return [_remote(src.at[2 * j + (1 - c)], land.at[j], send, recv, 4 * a + j, (x, y, 1 - c)) for j in range(4)]


def _rs_plan_chips(a, src, land, send, recv):
    x, y, c = _mesh_pos()
    out = []
    for k in (1, 2, 3):
        px = 1 - x if k & 2 else x
        py = 1 - y if k & 1 else y
        out.append(_remote(src.at[2 * px + py], land.at[k - 1], send, recv, 3 * a + k - 1, (px, py, c)))
    return out


def _in_hbm(a):
    return pltpu.with_memory_space_constraint(a, pltpu.HBM)


def _exchange_start(srcs, lands, plan, n_k, *, name):
    ns, n = len(srcs), len(lands)

    def body(*refs):
        src_refs, land_refs = refs[:ns], refs[ns:ns + n]
        send, recv = refs[ns + n], refs[ns + n + 1]
        token = refs[-1]
        for a in range(n):
            for cp in plan(a, src_refs[a] if ns else None, land_refs[a], send, recv):
                cp.start()
        token[...] = jnp.zeros(token.shape, token.dtype)

    bufs = list(srcs) + list(lands)
    outs = pl.pallas_call(
        body, name=name,
        out_shape=(pltpu.SemaphoreType.DMA((n * n_k,)), pltpu.SemaphoreType.DMA((n * n_k,)),
                   *[pltpu.HBM(b.shape, b.dtype) for b in bufs], jax.ShapeDtypeStruct((SUBLANES, LANES), F32)),
        in_specs=[_HBM] * (ns + n),
        out_specs=(_SEM, _SEM, *[_HBM] * (ns + n), pl.BlockSpec(memory_space=pltpu.VMEM)),
        input_output_aliases={i: 2 + i for i in range(ns + n)},
        compiler_params=pltpu.CompilerParams(has_side_effects=_EFFECT),
    )(*[_in_hbm(b) for b in bufs])
    return outs[0], outs[1], list(outs[2:2 + ns]), list(outs[2 + ns:2 + ns + n]), outs[-1]


def _exchange_wait(started, plan, after, *, name):
    send, recv, srcs, lands, _ = started
    ns, n = len(srcs), len(lands)

    def body(*refs):
        src_refs, land_refs = refs[:ns], refs[ns:ns + n]
        send_ref, recv_ref = refs[ns + n], refs[ns + n + 1]
        for a in range(n):
            for cp in plan(a, src_refs[a] if ns else None, land_refs[a], send_ref, recv_ref):
                cp.wait_send()
                cp.wait_recv()

    bufs = list(srcs) + list(lands)
    outs = pl.pallas_call(
        body, name=name,
        out_shape=tuple(pltpu.HBM(b.shape, b.dtype) for b in bufs),
        in_specs=[_HBM] * (ns + n) + [_SEM, _SEM, _ANY],
        out_specs=tuple([_HBM] * (ns + n)),
        input_output_aliases={i: i for i in range(ns + n)},
        compiler_params=pltpu.CompilerParams(has_side_effects=_EFFECT),
    )(*bufs, send, recv, after)
    return list(outs[:ns]), list(outs[ns:])


def _pair_add(grads, landed, core, *, name, tr):
    _, r_all, c_all = grads.shape

    def body(core_ref, g_ref, l_ref, o_ref):
        o_ref[...] = (g_ref[...].astype(F32) + l_ref[...].astype(F32)).astype(o_ref.dtype)

    return pl.pallas_call(
        body, name=name,
        grid_spec=pltpu.PrefetchScalarGridSpec(
            num_scalar_prefetch=1, grid=(4, r_all // tr),
            in_specs=[pl.BlockSpec((None, tr, c_all), lambda j, i, core_ref: (2 * j + core_ref[0], i, 0)),
                      pl.BlockSpec((None, tr, c_all), lambda j, i, core_ref: (j, i, 0))],
            out_specs=pl.BlockSpec((None, tr, c_all), lambda j, i, core_ref: (j, i, 0))),
        out_shape=jax.ShapeDtypeStruct((4, r_all, c_all), grads.dtype),
        compiler_params=_params(("parallel", "parallel")),
    )(core, grads, landed)


def _adamw_math(w, g, m, v):
    m2 = ADAM_B1 * m + (1.0 - ADAM_B1) * g
    v2 = ADAM_B2 * v + (1.0 - ADAM_B2) * (g * g)
    m_hat = m2 / (1.0 - ADAM_B1 ** ADAM_STEP)
    v_hat = v2 / (1.0 - ADAM_B2 ** ADAM_STEP)
    delta = -ADAM_LR * (m_hat / (jnp.sqrt(v_hat) + ADAM_EPS) + ADAM_WD * w)
    return delta, m2, v2


def _adamw(w, m, v, terms, order, *, name, tr, col_block=None, own=None, stack=None):
    r_all, c_all = w.shape
    n_slots = terms.shape[0]

    def body(*refs):
        if col_block is not None or own is not None:
            refs = refs[1:]
        own_ref = None
        if own is not None:
            own_ref, refs = refs[0], refs[1:]
        w_ref, m_ref, v_ref, t_ref, g_ref, d_ref, m2_ref, v2_ref = refs
        if own_ref is not None:
            g = own_ref[...].astype(F32) + t_ref[order[0]].astype(F32)
        else:
            g = t_ref[order[0]].astype(F32)
        for s in order[1:]:
            g = g + t_ref[s].astype(F32)
        delta, m2, v2 = _adamw_math(w_ref[...], g, m_ref[...], v_ref[...])
        g_ref[...] = g
        d_ref[...] = delta
        m2_ref[...] = m2
        v2_ref[...] = v2

    shape = jax.ShapeDtypeStruct((r_all, c_all), F32)
    if own is not None:
        layer, n_layers, prev = stack
        row = pl.BlockSpec((tr, c_all), lambda i, idx: (i, 0))
        slab = pl.BlockSpec((None, tr, c_all), lambda i, idx: (layer, i, 0))
        carried = [] if prev is None else list(prev)

        def stacked_body(*refs):
            body(*refs[:6], *refs[6 + len(carried):])

        return pl.pallas_call(
            stacked_body, name=name,
            grid_spec=pltpu.PrefetchScalarGridSpec(
                num_scalar_prefetch=1, grid=(r_all // tr,),
                in_specs=[pl.BlockSpec((None, tr, c_all), lambda i, idx: (idx[0], i, 0)), row, row, row,
                          pl.BlockSpec((n_slots, tr, c_all), lambda i, idx: (0, i, 0))] + [_ANY] * len(carried),
                out_specs=[slab] * 4),
            out_shape=[jax.ShapeDtypeStruct((n_layers, r_all, c_all), F32)] * 4,
            input_output_aliases={6 + k: k for k in range(len(carried))},
            compiler_params=_params(("parallel",)),
        )(own[1], own[0], w, m, v, terms, *carried)
    if col_block is None:
        row = pl.BlockSpec((tr, c_all), lambda i: (i, 0))
        return pl.pallas_call(
            body, name=name, grid=(r_all // tr,),
            in_specs=[row, row, row, pl.BlockSpec((n_slots, tr, c_all), lambda i: (0, i, 0))],
            out_specs=[row] * 4, out_shape=[shape] * 4, compiler_params=_params(("parallel",)),
        )(w, m, v, terms)
    row = pl.BlockSpec((tr, c_all), lambda i, blk: (i, 0))
    return pl.pallas_call(
        body, name=name,
        grid_spec=pltpu.PrefetchScalarGridSpec(
            num_scalar_prefetch=1, grid=(r_all // tr,),
            in_specs=[row, row, row, pl.BlockSpec((n_slots, tr, c_all), lambda i, blk: (0, i, blk[0]))],
            out_specs=[row] * 4),
        out_shape=[shape] * 4, compiler_params=_params(("parallel",)),
    )(col_block, w, m, v, terms)


def _rope_tables(t_all):
    pos = jnp.arange(t_all, dtype=F32)
    inv_freq = ROPE_THETA ** (-jnp.arange(0, QK_ROPE, 2, dtype=F32) / QK_ROPE)
    ang = pos[:, None] * inv_freq[None, :]
    cos, sin = jnp.cos(ang), jnp.sin(ang)
    return jnp.tile(cos, (1, LANES // (QK_ROPE // 2))), jnp.tile(sin, (1, LANES // (QK_ROPE // 2)))


def _adam_row_tile(r_all, c_all, block_bytes=512 * 1024):
    target = max(SUBLANES, block_bytes // (4 * c_all))
    return _pick(r_all, [t for t in (1024, 704, 512, 352, 256, 176, 128, 64, 32, 16, 8) if t <= target])


def _rows_natural(wg):
    return wg.reshape(wg.shape[0] * wg.shape[1], wg.shape[2])


def _mla_layer_fwd(tag, h, g_mix, ws, qn, kvn, cos, sin, *, tm, tq, n_heads, scale, n_real):
    w_in, w_uq, w_ukv, w_o = _rows_natural(ws[0]), ws[1], ws[2], _rows_natural(ws[3])
    t_all, d = h.shape
    lq, lkv = qn.shape[1], kvn.shape[1]
    tmb = _pick(t_all, _ROW_TILES)
    hn = _rmsnorm_fwd(h, g_mix, name=f"norm_mix{tag}", tm=tm)
    proj = _mm_nn(hn, w_in, name=f"mla_in{tag}", out_dtype=F32, tm=tmb, tn=w_in.shape[1], tk=_pick(d, _DIVS))
    cq, ckv, kr = _mla_prep_fwd(proj, qn, kvn, cos, sin, name=f"mla_prep{tag}", tm=tm, lq=lq, lkv=lkv)
    q = _mm_nn(cq, w_uq, name=f"mla_q{tag}", out_dtype=BF16, tm=tmb, tn=w_uq.shape[2], tk=lq, b_blocked=True,
               epilogue=_rope_q_epilogue, extras=(cos, sin))
    kv = _mm_nn(ckv, w_ukv, name=f"mla_kv{tag}", out_dtype=BF16, tm=tmb, tn=w_ukv.shape[2], tk=lkv, b_blocked=True)
    o, lse = _attn_fwd(q, kv, kr, name=f"attn_fwd{tag}", n_heads=n_heads, tq=tq, n_real=n_real, scale=scale)
    h_mid = _mm_nn(o, w_o, name=f"mla_o{tag}", out_dtype=F32, tm=tm, tn=d, tk=o.shape[1], res=h)
    return h_mid, (hn, proj, cq, ckv, kr, q, kv, o, lse)


def _mla_layer_bwd(tag, dh, dh_b, h_in, saved, g_mix, ws, qn, kvn, cos, sin, *, tm, tq, n_heads, scale, n_real, early=None):
    hn, proj, cq, ckv, kr, q, kv, o, lse = saved
    w_in, w_uq, w_ukv, w_o = _rows_natural(ws[0]), ws[1], ws[2], _rows_natural(ws[3])
    t_all, d = h_in.shape
    lq, lkv = qn.shape[1], kvn.shape[1]
    ov = o.shape[1]
    tmb = _pick(t_all, _ROW_TILES)
    tn_d, tk_d = _pick(d, _DIVS[1:]), _pick(d, _DIVS)
    do = _mm_nt(dh_b, w_o, name=f"mla_do{tag}", out_dtype=BF16, tm=tmb, tn=_pick(ov, _DIVS[1:]), tk=tk_d)
    dw_o = _mm_tn(o, dh_b, name=f"mla_dwo{tag}", out_dtype=BF16, tm=_pick(ov, _DIVS[2:]), tn=tn_d, tk=t_all)
    dq, dkv, dkr_h = _attn_bwd(q, kv, kr, o, lse, do, cos, sin, name=f"attn_bwd{tag}", n_heads=n_heads, tq=tq, n_real=n_real,
                               scale=scale)
    hw, kw = w_uq.shape[2], w_ukv.shape[2]
    dw_uq = _mm_tn(cq, dq, name=f"mla_dwuq{tag}", out_dtype=BF16, tm=lq, tn=hw, tk=t_all, out_block=hw)
    dcq = _mm_nt(dq, w_uq, name=f"mla_dcq{tag}", out_dtype=F32, tm=tm, tn=lq, tk=dq.shape[1], b_blocked=True)
    dw_ukv = _mm_tn(ckv, dkv, name=f"mla_dwukv{tag}", out_dtype=BF16, tm=lkv, tn=kw, tk=t_all, out_block=kw)
    dckv = _mm_nt(dkv, w_ukv, name=f"mla_dckv{tag}", out_dtype=F32, tm=tm, tn=lkv, tk=dkv.shape[1], b_blocked=True)
    first = [dw_uq, dw_ukv, dw_o.reshape(N_DEV, -1, d)]
    if early is not None:
        qn = qn + early(first)
    dproj, dqn, dkvn = _mla_prep_bwd(dcq, dckv, dkr_h, proj, qn, kvn, cos, sin, name=f"mla_prep_bwd{tag}", tm=tm, lq=lq, lkv=lkv)
    wc = w_in.shape[1]
    dw_in = _mm_tn(hn, dproj, name=f"mla_dwin{tag}", out_dtype=BF16, tm=_pick(d, _DIVS[2:]), tn=wc, tk=t_all)
    dhn = _mm_nt(dproj, w_in, name=f"mla_dhn{tag}", out_dtype=F32, tm=tmb, tn=tn_d, tk=wc)
    dh, dh_b, dg = _rmsnorm_bwd(dhn, h_in, g_mix, dh, name=f"norm_mix_bwd{tag}", tm=tm)
    return dh, dh_b, dg, dqn, dkvn, [dw_in.reshape(N_DEV, -1, wc)] + ([] if early is not None else first)


def _lru_layer_fwd(tag, h, g_mix, ws, small, *, tm):
    w_lin, w_lo = ws[0], _rows_natural(ws[1])
    t_all, d = h.shape
    dr = w_lo.shape[0]
    tmb = _pick(t_all, _ROW_TILES)
    hn = _rmsnorm_fwd(h, g_mix, name=f"norm_mix{tag}", tm=tm)
    xy = _mm_nn(hn, w_lin, name=f"lru_in{tag}", out_dtype=F32, tm=tmb, tn=w_lin.shape[2], tk=_pick(d, _DIVS), b_blocked=True,
                rows_outer=True)
    hs, hsy = _lru_fwd(xy, *small, name=f"lru_fwd{tag}")
    h_mid = _mm_nn(hsy, w_lo, name=f"lru_o{tag}", out_dtype=F32, tm=tm, tn=d, tk=dr, res=h)
    return h_mid, (hn, xy, hs, hsy)


def _lru_layer_bwd(tag, dh, dh_b, h_in, saved, g_mix, ws, small, *, tm):
    hn, xy, hs, hsy = saved
    w_lin, w_lo = ws[0], _rows_natural(ws[1])
    t_all, d = h_in.shape
    dr = w_lo.shape[0]
    tmb = _pick(t_all, _ROW_TILES)
    tn_d, tk_d = _pick(d, _DIVS[1:]), _pick(d, _DIVS)
    dhsy = _mm_nt(dh_b, w_lo, name=f"lru_dhsy{tag}", out_dtype=F32, tm=tmb, tn=_pick(dr, _DIVS[1:]), tk=tk_d)
    dw_lo = _mm_tn(hsy, dh_b, name=f"lru_dwo{tag}", out_dtype=BF16, tm=_pick(dr, _DIVS[2:]), tn=tn_d, tk=t_all)
    dxy, *dsmall = _lru_bwd(xy, hs, dhsy, *small, name=f"lru_bwd{tag}")
    lw = w_lin.shape[2]
    dw_lin = _mm_tn(hn, dxy, name=f"lru_dwin{tag}", out_dtype=BF16, tm=tn_d, tn=lw, tk=t_all, out_block=lw)
    dhn = _mm_nt(dxy, w_lin, name=f"lru_dhn{tag}", out_dtype=F32, tm=tm, tn=tn_d, tk=2 * dr, b_blocked=True)
    dh, dh_b, dg = _rmsnorm_bwd(dhn, h_in, g_mix, dh, name=f"norm_mix_bwd{tag}", tm=tm)
    return dh, dh_b, dg, tuple(dsmall), [dw_lin, dw_lo.reshape(N_DEV, -1, d)]


def _ffn_layer_fwd(tag, h_mid, g_ffn, ws, *, tm):
    w_gu, w_down = ws[0], _rows_natural(ws[1])
    t_all, d = h_mid.shape
    f_all = w_down.shape[0]
    tmb = _pick(t_all, _ROW_TILES)
    fk = _pick(f_all, (1408,) + _DIVS[1:])
    hn2 = _rmsnorm_fwd(h_mid, g_ffn, name=f"norm_ffn{tag}", tm=tm)
    gu, act = _ffn_up(hn2, w_gu, name=f"ffn_up{tag}", tm=tm)
    h_out = _mm_nn(act, w_down, name=f"ffn_down{tag}", out_dtype=F32, tm=tm, tn=_pick(d, _DIVS[1:]), tk=f_all, res=h_mid)
    return h_out, (hn2, gu, act)


def _ffn_layer_bwd(tag, dh, dh_b, h_mid, saved, g_ffn, ws, *, tm):
    hn2, gu, act = saved
    w_gu, w_down = ws[0], _rows_natural(ws[1])
    t_all, d = h_mid.shape
    f_all = w_down.shape[0]
    f_local = w_gu.shape[2]
    tmb = _pick(t_all, _ROW_TILES)
    fk = _pick(f_all, (1408,) + _DIVS[1:])
    tn_d, tk_d = _pick(d, _DIVS[1:]), _pick(d, _DIVS)
    dgu = _ffn_dact(dh_b, w_down, gu, name=f"ffn_dact{tag}", tm=tm, tn=f_local)
    dw_down = _mm_tn(act, dh_b, name=f"ffn_dwdown{tag}", out_dtype=BF16, tm=fk, tn=_pick(d, _DIVS[2:]), tk=t_all)
    dhn2 = _mm_nt(dgu, w_gu, name=f"ffn_dhn{tag}", out_dtype=F32, tm=tm, tn=_pick(d, _DIVS[2:]), tk=2 * f_all, b_blocked=True)
    dw_gu = _mm_tn(hn2, dgu, name=f"ffn_dwgu{tag}", out_dtype=BF16, tm=_pick(d, _DIVS[2:]), tn=f_local, tk=t_all, out_block=f_local,
                   cols_outer=True)
    dh, dh_b, dg = _rmsnorm_bwd(dhn2, h_mid, g_ffn, dh, name=f"norm_ffn_bwd{tag}", tm=tm)
    return dh, dh_b, dg, [dw_gu, dw_down.reshape(N_DEV, -1, d)]


def kernel(x, meta_tokens, norm_mix, norm_ffn, norm_final, mla_w_in, mla_q_norm, mla_kv_norm, mla_w_uq, mla_w_ukv, mla_w_o, lru_w_in, lru_conv_w, lru_conv_b, lru_w_gate_a, lru_b_gate_a, lru_w_gate_x, lru_b_gate_x, lru_lambda, lru_w_o, ffn_w_gu, ffn_w_down, loss_target, m_meta_tokens, m_norm_mix, m_norm_ffn, m_norm_final, m_mla_w_in, m_mla_q_norm, m_mla_kv_norm, m_mla_w_uq, m_mla_w_ukv, m_mla_w_o, m_lru_w_in, m_lru_conv_w, m_lru_conv_b, m_lru_w_gate_a, m_lru_b_gate_a, m_lru_w_gate_x, m_lru_b_gate_x, m_lru_lambda, m_lru_w_o, m_ffn_w_gu, m_ffn_w_down, v_meta_tokens, v_norm_mix, v_norm_ffn, v_norm_final, v_mla_w_in, v_mla_q_norm, v_mla_kv_norm, v_mla_w_uq, v_mla_w_ukv, v_mla_w_o, v_lru_w_in, v_lru_conv_w, v_lru_conv_b, v_lru_w_gate_a, v_lru_b_gate_a, v_lru_w_gate_x, v_lru_b_gate_x, v_lru_lambda, v_lru_w_o, v_ffn_w_gu, v_ffn_w_down):
    seq, d = x.shape[1], x.shape[2]
    assert seq % CHUNK == 0
    n_real = N_META + seq
    t_all = -(-n_real // LANES) * LANES
    tm = _pick(t_all, (384, 256, 128))
    tq = _pick(seq, (512, 256, 128, 64))
    depth = norm_mix.shape[0]
    n_mla, n_lru = mla_w_in.shape[0], lru_w_in.shape[0]
    lq, lkv = mla_q_norm.shape[1], mla_kv_norm.shape[1]
    w_in_cols = lq + lkv + LANES
    heads_local = mla_w_uq.shape[2] // (QK_NOPE + QK_ROPE)
    n_heads = heads_local * N_DEV
    dr = lru_w_gate_a.shape[1] * lru_w_gate_a.shape[2]
    scale = (QK_NOPE + QK_ROPE) ** -0.5
    cx, cy, cc = _mesh_pos()
    core = jnp.reshape(cc, (1,)).astype(jnp.int32)
    my_slot = jnp.reshape(4 * cx + 2 * cy + cc, (1,)).astype(jnp.int32)

    def pad_cols(w, cols):
        return jnp.pad(w, ((0, 0), (0, cols - w.shape[1])))

    def pad_heads(w):
        k_all = w.shape[0]
        w3 = w.reshape(k_all, heads_local, QK_NOPE + QK_ROPE)
        return jnp.pad(w3, ((0, 0), (0, 0), (0, HEAD_W - QK_NOPE - QK_ROPE))).reshape(k_all, heads_local * HEAD_W)

    def unpad_heads(w):
        k_all = w.shape[0]
        return w.reshape(k_all, heads_local, HEAD_W)[:, :, :QK_NOPE + QK_ROPE].reshape(k_all, -1)

    small_rows = N_META + n_lru * 4 + 2 * n_lru
    small_pad = -(-small_rows // SUBLANES) * SUBLANES

    def pack_small(meta, conv_w, conv_b, lam):
        rows = jnp.concatenate([meta, conv_w.reshape(n_lru * 4, -1), conv_b, lam], axis=0)
        return jnp.pad(rows, ((0, small_pad - small_rows), (0, 0)))

    def unpack_small(p):
        o1 = N_META + n_lru * 4
        return (p[:N_META], p[N_META:o1].reshape(n_lru, 4, -1), p[o1:o1 + n_lru], p[o1 + n_lru:o1 + 2 * n_lru])

    (small_full,), small_done = _all_gather([pack_small(meta_tokens, lru_conv_w, lru_conv_b, lru_lambda)], name="ag_small")
    small_full = jnp.transpose(small_full, (1, 0, 2)).reshape(small_pad, -1)
    meta_full, conv_w_full, conv_b_full, lam_full = unpack_small(small_full)

    def wire(w):
        return (w + small_done).astype(BF16)

    mla_shards, lru_shards, ffn_shards = [], [], []
    for j in range(n_mla):
        mla_shards.append([wire(pad_cols(mla_w_in[j], w_in_cols)), wire(pad_heads(mla_w_uq[j])), wire(mla_w_ukv[j]),
                           wire(mla_w_o[j])])
    for j in range(n_lru):
        lru_shards.append([wire(lru_w_in[j]), wire(lru_w_o[j])])
    for layer in range(depth):
        ffn_shards.append([wire(ffn_w_gu[layer]), wire(ffn_w_down[layer])])

    n_sub = 2 * depth
    groups = []
    for layer in range(depth):
        groups += [mla_shards[layer // 2] if layer % 2 == 0 else lru_shards[layer // 2], ffn_shards[layer]]
    slot_idx = 4 * cx + 2 * cy + cc
    ag_own = []
    for gi, shards in enumerate(groups):
        lands = [lax.dynamic_update_slice(lax.empty((N_DEV,) + s.shape, s.dtype), s[None], (slot_idx, 0, 0)) for s in shards]
        ag_own.append(_exchange_start(shards, lands, _ag_plan_own, 4, name=f"ag{gi}_start"))
    ag_pass = [None] * n_sub
    weights = [None] * n_sub

    def ag_landed(gi, after):
        _, lands = _exchange_wait(ag_own[gi], _ag_plan_own, after, name=f"ag{gi}_wait")
        ag_pass[gi] = _exchange_start([], lands, _ag_plan_pass, 3, name=f"ag{gi}_pass")
        return ag_pass[gi][4][0, 0]

    def ag_done(gi, after):
        _, weights[gi] = _exchange_wait(ag_pass[gi], _ag_plan_pass, after, name=f"ag{gi}_pass_wait")

    cos, sin = _rope_tables(t_all)
    zeros_tail = jnp.zeros((t_all - n_real, d), F32)
    started = ag_own[0][4][0, 0]
    for st in ag_own[1:]:
        started = started + st[4][0, 0]
    pad_rows = t_all - n_real
    h = _copy_rows([(meta_full + started, 0, 0, N_META), (x[0], 0, N_META, seq), (zeros_tail, 0, n_real, pad_rows)], t_all,
                   name="assemble_h")
    target = _copy_rows([(zeros_tail, 0, 0, N_META), (loss_target[0], 0, N_META, seq), (zeros_tail, 0, n_real, pad_rows)],
                        t_all, name="assemble_target")

    attn_kw = dict(tm=tm, tq=tq, n_heads=n_heads, scale=scale, n_real=n_real)

    def lru_small(j):
        return (conv_w_full[j], conv_b_full[j][None, :], lru_w_gate_a[j].astype(BF16), lru_b_gate_a[j].reshape(1, dr),
                lru_w_gate_x[j].astype(BF16), lru_b_gate_x[j].reshape(1, dr), lam_full[j][None, :])

    def before_sublayer(k, act):
        tok = ag_landed(k, act) if k <= 1 else 0.0
        ag_done(k, act)
        if 1 <= k < n_sub - 1:
            tok = tok + ag_landed(k + 1, act)
        return tok

    saved = []
    for layer in range(depth):
        j = layer // 2
        g_mix = norm_mix[layer][None, :] + before_sublayer(2 * layer, h)
        if layer % 2 == 0:
            h_mid, mix_saved = _mla_layer_fwd(layer, h, g_mix, weights[2 * layer], mla_q_norm[j][None, :],
                                              mla_kv_norm[j][None, :], cos, sin, **attn_kw)
        else:
            h_mid, mix_saved = _lru_layer_fwd(layer, h, g_mix, weights[2 * layer], lru_small(j), tm=tm)
        g_ffn = norm_ffn[layer][None, :] + before_sublayer(2 * layer + 1, h_mid)
        h_out, ffn_saved = _ffn_layer_fwd(layer, h_mid, g_ffn, weights[2 * layer + 1], tm=tm)
        saved.append((h, h_mid, mix_saved, ffn_saved))
        h = h_out

    loss_part, dh, dh_b, dg_final = _loss_head(h, target, norm_final[None, :], name="loss_head", tm=tm, n_real=n_real)
    loss = lax.psum(loss_part[0, 0], ("x", "y", "c"))

    rs_sib, rs_chip, reduced = [None] * (n_sub + 1), [None] * (n_sub + 1), [None] * (n_sub + 1)
    chip_idx = jnp.reshape(2 * cx + cy, (1,)).astype(jnp.int32)

    def rs_begin(k, grads):
        lands = [lax.empty((4,) + g.shape[1:], g.dtype) for g in grads]
        rs_sib[k] = _exchange_start(grads, lands, _rs_plan_sibling, 4, name=f"rs{k}_start")
        return rs_sib[k][4][0, 0]

    def rs_middle(k, after):
        grads, landed = _exchange_wait(rs_sib[k], _rs_plan_sibling, after, name=f"rs{k}_wait")
        parts = [_pair_add(g, l, core, name=f"rs{k}_add{a}", tr=_adam_row_tile(g.shape[1], g.shape[2], 4 * 1024 * 1024))
                 for a, (g, l) in enumerate(zip(grads, landed))]
        lands = [lax.empty((3,) + p.shape[1:], p.dtype) for p in parts]
        rs_chip[k] = _exchange_start(parts, lands, _rs_plan_chips, 3, name=f"rs{k}_chips")
        return rs_chip[k][4][0, 0]

    def rs_end(k, after):
        reduced[k] = _exchange_wait(rs_chip[k], _rs_plan_chips, after, name=f"rs{k}_chips_wait")

    d_norm_mix, d_norm_ffn = [None] * depth, [None] * depth
    d_qn, d_kvn = [None] * n_mla, [None] * n_mla
    d_small = {k: [None] * n_lru for k in ("cw", "cb", "wga", "bga", "wgx", "bgx", "lam")}
    tok, waiting = 0.0, None
    for layer in reversed(range(depth)):
        j = layer // 2
        h_in, h_mid, mix_saved, ffn_saved = saved[layer]
        dh, dh_b, d_norm_ffn[layer], ffn_g = _ffn_layer_bwd(layer, dh, dh_b, h_mid, ffn_saved, norm_ffn[layer][None, :] + tok,
                                                            weights[2 * layer + 1], tm=tm)
        tok = rs_begin(2 * layer + 1, ffn_g)
        if waiting is not None:
            tok = tok + rs_middle(waiting, dh)
        waiting = 2 * layer + 1
        if layer == 0:
            tok = tok + rs_middle(waiting, dh)
            waiting = None
        g_mix = norm_mix[layer][None, :] + tok
        if layer % 2 == 0:
            early = (lambda g: rs_begin(n_sub, g) + rs_middle(n_sub, g[0])) if layer == 0 else None
            dh, dh_b, d_norm_mix[layer], d_qn[j], d_kvn[j], mix_g = _mla_layer_bwd(
                layer, dh, dh_b, h_in, mix_saved, g_mix, weights[2 * layer], mla_q_norm[j][None, :], mla_kv_norm[j][None, :],
                cos, sin, early=early, **attn_kw)
        else:
            dh, dh_b, d_norm_mix[layer], dsmall, mix_g = _lru_layer_bwd(layer, dh, dh_b, h_in, mix_saved, g_mix,
                                                                        weights[2 * layer], lru_small(j), tm=tm)
            for key, val in zip(("cw", "cb", "wga", "bga", "wgx", "bgx", "lam"), dsmall):
                d_small[key][j] = val
        tok = rs_begin(2 * layer, mix_g)
        if waiting is not None:
            tok = tok + rs_middle(waiting, dh)
        waiting = 2 * layer
    rs_middle(waiting, dh)

    grad_x = _copy_rows([(dh, N_META, 0, seq)], seq, name="slice_grad_x")[None]

    d_meta = dh[:N_META]
    small_grad = pack_small(d_meta, jnp.stack(d_small["cw"], axis=0), jnp.concatenate(d_small["cb"], axis=0),
                            jnp.concatenate(d_small["lam"], axis=0))
    rep_grads = [
        jnp.concatenate(d_norm_mix, axis=0), jnp.concatenate(d_norm_ffn, axis=0), dg_final,
        jnp.concatenate(d_qn, axis=0), jnp.concatenate(d_kvn, axis=0),
        jnp.stack(d_small["wga"], axis=0).reshape(-1, LANES), jnp.concatenate(d_small["bga"], axis=0),
        jnp.stack(d_small["wgx"], axis=0).reshape(-1, LANES), jnp.concatenate(d_small["bgx"], axis=0),
    ]
    small_srcs = [small_grad] + [jnp.pad(g, ((0, -g.shape[0] % SUBLANES), (0, 0))) for g in rep_grads]
    small_lands = [lax.dynamic_update_slice(lax.empty((N_DEV,) + s.shape, s.dtype), s[None], (slot_idx, 0, 0))
                   for s in small_srcs]
    small_own = _exchange_start(small_srcs, small_lands, _ag_plan_own, 4, name="ag_grads_start")

    res = {}

    def adam_sharded(nm, k, a, idx, n_layers, w, m, v):
        parts, landed = reduced[k]
        r_all, c_all = landed[a].shape[1], landed[a].shape[2]
        res[nm] = _adamw(w.reshape(r_all, c_all), m.reshape(r_all, c_all), v.reshape(r_all, c_all), landed[a], (0, 1, 2),
                         name=f"adamw_{nm}{idx}", tr=_adam_row_tile(r_all, c_all, 2 * 1024 * 1024), own=(parts[a], chip_idx),
                         stack=(idx, n_layers, res.get(nm)))

    after = small_own[4]
    for k in reversed(range(n_sub)):
        rs_end(k, after)
        if k == 0:
            rs_end(n_sub, after)
            reduced[0] = tuple(first + rest for first, rest in zip(reduced[0], reduced[n_sub]))
        layer, j = k // 2, k // 4
        if k % 2 == 1:
            adam_sharded("ffn_w_gu", k, 0, layer, depth, ffn_w_gu[layer], m_ffn_w_gu[layer], v_ffn_w_gu[layer])
            adam_sharded("ffn_w_down", k, 1, layer, depth, ffn_w_down[layer], m_ffn_w_down[layer], v_ffn_w_down[layer])
            after = res["ffn_w_down"][0]
        elif layer % 2 == 0:
            adam_sharded("mla_w_in", k, 0, j, n_mla, pad_cols(mla_w_in[j], w_in_cols), pad_cols(m_mla_w_in[j], w_in_cols),
                         pad_cols(v_mla_w_in[j], w_in_cols))
            adam_sharded("mla_w_uq", k, 1, j, n_mla, pad_heads(mla_w_uq[j]), pad_heads(m_mla_w_uq[j]), pad_heads(v_mla_w_uq[j]))
            adam_sharded("mla_w_ukv", k, 2, j, n_mla, mla_w_ukv[j], m_mla_w_ukv[j], v_mla_w_ukv[j])
            adam_sharded("mla_w_o", k, 3, j, n_mla, mla_w_o[j], m_mla_w_o[j], v_mla_w_o[j])
            after = res["mla_w_o"][0]
        else:
            adam_sharded("lru_w_in", k, 0, j, n_lru, lru_w_in[j], m_lru_w_in[j], v_lru_w_in[j])
            adam_sharded("lru_w_o", k, 1, j, n_lru, lru_w_o[j], m_lru_w_o[j], v_lru_w_o[j])
            after = res["lru_w_o"][0]
    res["mla_w_in"] = [t[:, :, :lq + lkv + QK_ROPE] for t in res["mla_w_in"]]
    res["mla_w_uq"] = [t.reshape(n_mla, lq, heads_local, HEAD_W)[:, :, :, :QK_NOPE + QK_ROPE].reshape(n_mla, lq, -1)
                       for t in res["mla_w_uq"]]

    _, small_lands = _exchange_wait(small_own, _ag_plan_own, after, name="ag_grads_wait")
    small_pass = _exchange_start([], small_lands, _ag_plan_pass, 3, name="ag_grads_pass")
    _, all_small = _exchange_wait(small_pass, _ag_plan_pass, after, name="ag_grads_pass_wait")
    slot_order = tuple(range(N_DEV))

    def adam_rep(terms, w, m, v, tag):
        r_pad, c_all = terms.shape[1], terms.shape[2]

        def prep(t):
            t2 = t.reshape(-1, c_all)
            return jnp.pad(t2, ((0, r_pad - t2.shape[0]), (0, 0)))

        outs = _adamw(prep(w), prep(m), prep(v), terms, slot_order, name=f"adamw_{tag}", tr=_adam_row_tile(r_pad, c_all))
        n_rows = w.size // c_all
        return [o[:n_rows].reshape(w.shape) for o in outs]

    small_w = pack_small(meta_tokens, lru_conv_w, lru_conv_b, lru_lambda)
    small_m = pack_small(m_meta_tokens, m_lru_conv_w, m_lru_conv_b, m_lru_lambda)
    small_v = pack_small(v_meta_tokens, v_lru_conv_w, v_lru_conv_b, v_lru_lambda)
    small_out = _adamw(small_w, small_m, small_v, all_small[0], slot_order, name="adamw_small", tr=small_pad, col_block=my_slot)
    small_out = [unpack_small(o) for o in small_out]
    for idx, key in enumerate(("meta_tokens", "lru_conv_w", "lru_conv_b", "lru_lambda")):
        res[key] = [small_out[k][idx] for k in range(4)]

    res["norm_mix"] = adam_rep(all_small[1], norm_mix, m_norm_mix, v_norm_mix, "norm_mix")
    res["norm_ffn"] = adam_rep(all_small[2], norm_ffn, m_norm_ffn, v_norm_ffn, "norm_ffn")
    res["norm_final"] = adam_rep(all_small[3], norm_final, m_norm_final, v_norm_final, "norm_final")
    res["mla_q_norm"] = adam_rep(all_small[4], mla_q_norm, m_mla_q_norm, v_mla_q_norm, "mla_q_norm")
    res["mla_kv_norm"] = adam_rep(all_small[5], mla_kv_norm, m_mla_kv_norm, v_mla_kv_norm, "mla_kv_norm")
    res["lru_w_gate_a"] = adam_rep(all_small[6], lru_w_gate_a, m_lru_w_gate_a, v_lru_w_gate_a, "lru_w_gate_a")
    res["lru_b_gate_a"] = adam_rep(all_small[7], lru_b_gate_a, m_lru_b_gate_a, v_lru_b_gate_a, "lru_b_gate_a")
    res["lru_w_gate_x"] = adam_rep(all_small[8], lru_w_gate_x, m_lru_w_gate_x, v_lru_w_gate_x, "lru_w_gate_x")
    res["lru_b_gate_x"] = adam_rep(all_small[9], lru_b_gate_x, m_lru_b_gate_x, v_lru_b_gate_x, "lru_b_gate_x")

    names = ["meta_tokens", "norm_mix", "norm_ffn", "norm_final", "mla_w_in", "mla_q_norm", "mla_kv_norm", "mla_w_uq",
             "mla_w_ukv", "mla_w_o", "lru_w_in", "lru_conv_w", "lru_conv_b", "lru_w_gate_a", "lru_b_gate_a", "lru_w_gate_x",
             "lru_b_gate_x", "lru_lambda", "lru_w_o", "ffn_w_gu", "ffn_w_down"]
    shapes = dict(meta_tokens=meta_tokens, norm_mix=norm_mix, norm_ffn=norm_ffn, norm_final=norm_final, mla_w_in=mla_w_in,
                  mla_q_norm=mla_q_norm, mla_kv_norm=mla_kv_norm, mla_w_uq=mla_w_uq, mla_w_ukv=mla_w_ukv, mla_w_o=mla_w_o,
                  lru_w_in=lru_w_in, lru_conv_w=lru_conv_w, lru_conv_b=lru_conv_b, lru_w_gate_a=lru_w_gate_a,
                  lru_b_gate_a=lru_b_gate_a, lru_w_gate_x=lru_w_gate_x, lru_b_gate_x=lru_b_gate_x, lru_lambda=lru_lambda,
                  lru_w_o=lru_w_o, ffn_w_gu=ffn_w_gu, ffn_w_down=ffn_w_down)
    outs = [loss, grad_x]
    for k in range(4):
        outs += [res[nm][k].reshape(shapes[nm].shape) for nm in names]
    return tuple(outs)
```

```python
import math

import jax
import jax.numpy as jnp
from jax import lax
from jax.experimental import pallas as pl
from jax.experimental.pallas import tpu as pltpu

F32 = jnp.float32
BF16 = jnp.bfloat16
MESH = pl.DeviceIdType.MESH

N_META = 16
CHUNK = 64
QK_NOPE = 128
QK_ROPE = 64
V_HEAD = 128
HEAD_W = 256
ROPE_THETA = 10000.0
LRU_C = 8.0
RMS_EPS = 1e-6
NEG_BIG = -1e30
ADAM_LR, ADAM_B1, ADAM_B2, ADAM_EPS, ADAM_WD, ADAM_STEP = 0.001, 0.9, 0.999, 1e-08, 0.01, 10

LANES = 128
SUBLANES = 8
VMEM_LIMIT_BYTES = 52 * 1024 * 1024
N_DEV = 8

_NT = (((1,), (1,)), ((), ()))
_TN = (((0,), (0,)), ((), ()))
_DIVS = (2048, 1024, 512, 256, 128)
_ROW_TILES = (1408, 1024, 512, 256, 128)


def _params(dims):
    return pltpu.CompilerParams(dimension_semantics=dims, vmem_limit_bytes=VMEM_LIMIT_BYTES)


def _pick(n, candidates):
    for c in candidates:
        if c <= n and n % c == 0:
            return c
    return n


def _sigmoid(z):
    return 0.5 + 0.5 * jnp.tanh(0.5 * z)


def _gelu(x):
    c = math.sqrt(2.0 / math.pi)
    return 0.5 * x * (1.0 + jnp.tanh(c * (x + 0.044715 * x * x * x)))


def _gelu_grad(x):
    c = math.sqrt(2.0 / math.pi)
    th = jnp.tanh(c * (x + 0.044715 * x * x * x))
    return 0.5 * (1.0 + th) + 0.5 * x * (1.0 - th * th) * c * (1.0 + 3.0 * 0.044715 * x * x)


def _neg_expm1(x):
    poly = -x * (1.0 + x * (1.0 / 2.0) * (1.0 + x * (1.0 / 3.0) * (1.0 + x * (1.0 / 4.0) * (
        1.0 + x * (1.0 / 5.0) * (1.0 + x * (1.0 / 6.0) * (1.0 + x * (1.0 / 7.0)))))))
    return jnp.where(x > -0.25, poly, 1.0 - jnp.exp(x))


def _softplus_neg(lam):
    e = jnp.exp(-jnp.abs(lam))
    log1p = jnp.where(e > 1e-4, jnp.log(1.0 + e), e * (1.0 - e * (0.5 - e * (1.0 / 3.0))))
    return jnp.maximum(-lam, 0.0) + log1p


def _rot_half(x):
    lane = lax.broadcasted_iota(jnp.int32, x.shape, 1)
    first = (lane % QK_ROPE) < (QK_ROPE // 2)
    return jnp.where(first, -pltpu.roll(x, LANES - QK_ROPE // 2, 1), pltpu.roll(x, QK_ROPE // 2, 1))


def _rope(x, cos, sin):
    return x * cos + _rot_half(x) * sin


def _unrope(g, cos, sin):
    return g * cos - _rot_half(g) * sin


def _grid_order(rows_outer):
    if not rows_outer:
        return lambda f: f
    return lambda f: (lambda i, j, k: f(j, i, k))


def _mm_nn(a, b, *, name, out_dtype, tm, tn, tk, b_blocked=False, res=None, epilogue=None, extras=(), rows_outer=False):
    m_all, k_all = a.shape
    om = _grid_order(rows_outer)
    if b_blocked:
        g_all, kb, nb = b.shape
        n_all = g_all * nb
        assert nb % tn == 0
        r = nb // tn
        b_spec = pl.BlockSpec((None, tk, tn), om(lambda j, i, k: (j // r, k, j % r)))
    else:
        kb, n_all = b.shape
        b_spec = pl.BlockSpec((tk, tn), om(lambda j, i, k: (k, j)))
    assert kb == k_all and m_all % tm == 0 and n_all % tn == 0 and k_all % tk == 0
    nm, nn, nk = m_all // tm, n_all // tn, k_all // tk
    in_specs = [pl.BlockSpec((tm, tk), om(lambda j, i, k: (i, k))), b_spec]
    operands = [a, b]
    has_res = res is not None
    if has_res:
        in_specs.append(pl.BlockSpec((tm, tn), om(lambda j, i, k: (i, j))))
        operands.append(res)
    for e in extras:
        in_specs.append(pl.BlockSpec((tm, e.shape[1]), om(lambda j, i, k: (i, 0))))
        operands.append(e)
    n_ex = len(extras)

    def body(*refs):
        a_ref, b_ref = refs[0], refs[1]
        pos = 2
        res_ref = None
        if has_res:
            res_ref = refs[pos]
            pos += 1
        ex_refs = refs[pos:pos + n_ex]
        pos += n_ex
        o_ref = refs[pos]
        acc_ref = refs[pos + 1] if nk > 1 else None

        def finish(acc):
            if has_res:
                acc = acc + res_ref[...]
            if epilogue is not None:
                acc = epilogue(acc, *ex_refs)
            o_ref[...] = acc.astype(o_ref.dtype)

        prod = jnp.dot(a_ref[...], b_ref[...], preferred_element_type=F32)
        if nk == 1:
            finish(prod)
        else:
            k = pl.program_id(2)

            @pl.when(k == 0)
            def _():
                acc_ref[...] = prod

            @pl.when(k > 0)
            def _():
                acc_ref[...] += prod

            @pl.when(k == nk - 1)
            def _():
                finish(acc_ref[...])

    return pl.pallas_call(
        body, name=name, grid=(nm, nn, nk) if rows_outer else (nn, nm, nk), in_specs=in_specs,
        out_specs=pl.BlockSpec((tm, tn), om(lambda j, i, k: (i, j))),
        out_shape=jax.ShapeDtypeStruct((m_all, n_all), out_dtype),
        scratch_shapes=[pltpu.VMEM((tm, tn), F32)] if nk > 1 else [],
        compiler_params=_params(("parallel", "parallel", "arbitrary")),
    )(*operands)


def _mm_nt(a, b, *, name, out_dtype, tm, tn, tk, b_blocked=False):
    if a.ndim == 3:
        n_planes, m_all, kp = a.shape
        k_all = n_planes * kp
    else:
        n_planes, (m_all, k_all) = 0, a.shape
    if b_blocked and tk == k_all and b.shape[0] > 1:
        g_all, n_all, nb = b.shape
        assert g_all * nb == k_all and m_all % tm == 0 and n_all % tn == 0
        per_plane = kp // nb if n_planes else 0

        def whole_body(a_ref, b_ref, o_ref):
            acc = None
            for g in range(g_all):
                a_g = a_ref[g // per_plane, :, (g % per_plane) * nb:(g % per_plane + 1) * nb] if n_planes else a_ref[:, g * nb:(g + 1) * nb]
                prod = lax.dot_general(a_g, b_ref[g], _NT, preferred_element_type=F32)
                acc = prod if acc is None else acc + prod
            o_ref[...] = acc.astype(o_ref.dtype)

        a_whole = (pl.BlockSpec((n_planes, tm, kp), lambda j, i: (0, i, 0)) if n_planes
                   else pl.BlockSpec((tm, k_all), lambda j, i: (i, 0)))
        return pl.pallas_call(
            whole_body, name=name, grid=(n_all // tn, m_all // tm),
            in_specs=[a_whole, pl.BlockSpec((g_all, tn, nb), lambda j, i: (0, j, 0))],
            out_specs=pl.BlockSpec((tm, tn), lambda j, i: (i, j)),
            out_shape=jax.ShapeDtypeStruct((m_all, n_all), out_dtype),
            compiler_params=_params(("parallel", "parallel")),
        )(a, b)
    if n_planes:
        assert kp % tk == 0
        rp = kp // tk
        a_spec = pl.BlockSpec((None, tm, tk), lambda j, i, k: (k // rp, i, k % rp))
    else:
        a_spec = pl.BlockSpec((tm, tk), lambda j, i, k: (i, k))
    if b_blocked:
        g_all, n_all, nb = b.shape
        assert g_all * nb == k_all and nb % tk == 0
        r = nb // tk
        b_spec = pl.BlockSpec((None, tn, tk), lambda j, i, k: (k // r, j, k % r))
    else:
        n_all, kb = b.shape
        assert kb == k_all
        b_spec = pl.BlockSpec((tn, tk), lambda j, i, k: (j, k))
    assert m_all % tm == 0 and n_all % tn == 0 and k_all % tk == 0
    nm, nn, nk = m_all // tm, n_all // tn, k_all // tk

    def body(a_ref, b_ref, o_ref, *scratch):
        prod = lax.dot_general(a_ref[...], b_ref[...], _NT, preferred_element_type=F32)
        if nk == 1:
            o_ref[...] = prod.astype(o_ref.dtype)
        else:
            acc_ref = scratch[0]
            k = pl.program_id(2)

            @pl.when(k == 0)
            def _():
                acc_ref[...] = prod

            @pl.when(k > 0)
            def _():
                acc_ref[...] += prod

            @pl.when(k == nk - 1)
            def _():
                o_ref[...] = acc_ref[...].astype(o_ref.dtype)

    return pl.pallas_call(
        body, name=name, grid=(nn, nm, nk),
        in_specs=[a_spec, b_spec],
        out_specs=pl.BlockSpec((tm, tn), lambda j, i, k: (i, j)),
        out_shape=jax.ShapeDtypeStruct((m_all, n_all), out_dtype),
        scratch_shapes=[pltpu.VMEM((tm, tn), F32)] if nk > 1 else [],
        compiler_params=_params(("parallel", "parallel", "arbitrary")),
    )(a, b)


def _mm_tn(a, b, *, name, out_dtype, tm, tn, tk, out_block=None, cols_outer=False):
    t_all, m_all = a.shape
    om = _grid_order(cols_outer)
    if b.ndim == 3:
        n_planes, tb, n_p = b.shape
        assert n_p % tn == 0
        rq = n_p // tn
        n_all = n_planes * n_p
        b_spec = pl.BlockSpec((None, tk, tn), om(lambda i, j, k: (j // rq, k, j % rq)))
    else:
        tb, n_all = b.shape
        b_spec = pl.BlockSpec((tk, tn), om(lambda i, j, k: (k, j)))
    assert tb == t_all and m_all % tm == 0 and n_all % tn == 0 and t_all % tk == 0
    nm, nn, nk = m_all // tm, n_all // tn, t_all // tk
    if out_block is None:
        out_shape = jax.ShapeDtypeStruct((m_all, n_all), out_dtype)
        out_spec = pl.BlockSpec((tm, tn), om(lambda i, j, k: (i, j)))
    else:
        assert out_block % tn == 0 and n_all % out_block == 0
        r = out_block // tn
        out_shape = jax.ShapeDtypeStruct((n_all // out_block, m_all, out_block), out_dtype)
        out_spec = pl.BlockSpec((None, tm, tn), om(lambda i, j, k: (j // r, i, j % r)))

    def body(a_ref, b_ref, o_ref, *scratch):
        prod = lax.dot_general(a_ref[...], b_ref[...], _TN, preferred_element_type=F32)
        if nk == 1:
            o_ref[...] = prod.astype(o_ref.dtype)
        else:
            acc_ref = scratch[0]
            k = pl.program_id(2)

            @pl.when(k == 0)
            def _():
                acc_ref[...] = prod

            @pl.when(k > 0)
            def _():
                acc_ref[...] += prod

            @pl.when(k == nk - 1)
            def _():
                o_ref[...] = acc_ref[...].astype(o_ref.dtype)

    return pl.pallas_call(
        body, name=name, grid=(nn, nm, nk) if cols_outer else (nm, nn, nk),
        in_specs=[pl.BlockSpec((tk, tm), om(lambda i, j, k: (k, i))), b_spec],
        out_specs=out_spec, out_shape=out_shape,
        scratch_shapes=[pltpu.VMEM((tm, tn), F32)] if nk > 1 else [],
        compiler_params=_params(("parallel", "parallel", "arbitrary")),
    )(a, b)


def _rmsnorm_fwd(x, g, *, name, tm):
    t_all, d = x.shape

    def body(x_ref, g_ref, o_ref):
        xv = x_ref[...]
        rstd = lax.rsqrt(jnp.mean(xv * xv, axis=-1, keepdims=True) + RMS_EPS)
        o_ref[...] = (xv * rstd * g_ref[...]).astype(o_ref.dtype)

    return pl.pallas_call(
        body, name=name, grid=(t_all // tm,),
        in_specs=[pl.BlockSpec((tm, d), lambda i: (i, 0)), pl.BlockSpec((1, d), lambda i: (0, 0))],
        out_specs=pl.BlockSpec((tm, d), lambda i: (i, 0)),
        out_shape=jax.ShapeDtypeStruct((t_all, d), BF16),
        compiler_params=_params(("parallel",)),
    )(x, g)


def _rms_bwd_math(dy, xv, g):
    rstd = lax.rsqrt(jnp.mean(xv * xv, axis=-1, keepdims=True) + RMS_EPS)
    xhat = xv * rstd
    dxh = dy * g
    dx = rstd * (dxh - xhat * jnp.mean(dxh * xhat, axis=-1, keepdims=True))
    return dx, jnp.sum(dy * xhat, axis=0, keepdims=True)


def _rmsnorm_bwd(dy, x, g, res, *, name, tm):
    t_all, d = x.shape

    def body(dy_ref, x_ref, g_ref, res_ref, dx_ref, dxb_ref, dg_ref):
        dx, dg = _rms_bwd_math(dy_ref[...].astype(F32), x_ref[...], g_ref[...])
        tot = res_ref[...] + dx
        dx_ref[...] = tot
        dxb_ref[...] = tot.astype(BF16)

        @pl.when(pl.program_id(0) == 0)
        def _():
            dg_ref[...] = dg

        @pl.when(pl.program_id(0) > 0)
        def _():
            dg_ref[...] += dg

    row = pl.BlockSpec((tm, d), lambda i: (i, 0))
    vec = pl.BlockSpec((1, d), lambda i: (0, 0))
    return pl.pallas_call(
        body, name=name, grid=(t_all // tm,),
        in_specs=[row, row, vec, row], out_specs=[row, row, vec],
        out_shape=[jax.ShapeDtypeStruct((t_all, d), F32), jax.ShapeDtypeStruct((t_all, d), BF16),
                   jax.ShapeDtypeStruct((1, d), F32)],
        compiler_params=_params(("arbitrary",)),
    )(dy, x, g, res)


def _loss_head(h, target, g, *, name, tm, n_real):
    t_all, d = h.shape

    def body(h_ref, t_ref, g_ref, loss_ref, dx_ref, dxb_ref, dg_ref):
        i = pl.program_id(0)
        xv = h_ref[...]
        gv = g_ref[...]
        rstd = lax.rsqrt(jnp.mean(xv * xv, axis=-1, keepdims=True) + RMS_EPS)
        y = xv * rstd * gv
        row = i * tm + lax.broadcasted_iota(jnp.int32, (tm, 1), 0)
        valid = (row >= N_META) & (row < n_real)
        err = jnp.where(valid, y - t_ref[...], 0.0)
        part = 0.5 * jnp.sum(jnp.mean(err * err, axis=-1, keepdims=True), axis=0, keepdims=True)
        dx, dg = _rms_bwd_math(err * (1.0 / d), xv, gv)
        dx_ref[...] = dx
        dxb_ref[...] = dx.astype(BF16)

        @pl.when(i == 0)
        def _():
            dg_ref[...] = dg
            loss_ref[...] = jnp.broadcast_to(part, loss_ref.shape)

        @pl.when(i > 0)
        def _():
            dg_ref[...] += dg
            loss_ref[...] += jnp.broadcast_to(part, loss_ref.shape)

    row = pl.BlockSpec((tm, d), lambda i: (i, 0))
    vec = pl.BlockSpec((1, d), lambda i: (0, 0))
    return pl.pallas_call(
        body, name=name, grid=(t_all // tm,),
        in_specs=[row, row, vec],
        out_specs=[pl.BlockSpec((1, LANES), lambda i: (0, 0)), row, row, vec],
        out_shape=[jax.ShapeDtypeStruct((1, LANES), F32), jax.ShapeDtypeStruct((t_all, d), F32),
                   jax.ShapeDtypeStruct((t_all, d), BF16), jax.ShapeDtypeStruct((1, d), F32)],
        compiler_params=_params(("arbitrary",)),
    )(h, target, g)


def _ffn_up(x, w_gu, *, name, tm):
    t_all, d = x.shape
    g_all, kb, nb = w_gu.shape
    half = g_all // 2
    f = half * nb
    assert kb == d and t_all % tm == 0

    def body(x_ref, wg_ref, wu_ref, gu_ref, act_ref):
        xv = x_ref[...]
        gv = jnp.dot(xv, wg_ref[...], preferred_element_type=F32)
        uv = jnp.dot(xv, wu_ref[...], preferred_element_type=F32)
        gu_ref[0] = gv.astype(gu_ref.dtype)
        gu_ref[1] = uv.astype(gu_ref.dtype)
        act_ref[...] = (gv * _sigmoid(gv) * uv).astype(act_ref.dtype)

    return pl.pallas_call(
        body, name=name, grid=(half, t_all // tm),
        in_specs=[pl.BlockSpec((tm, d), lambda j, i: (i, 0)), pl.BlockSpec((None, d, nb), lambda j, i: (j, 0, 0)),
                  pl.BlockSpec((None, d, nb), lambda j, i: (j + half, 0, 0))],
        out_specs=[pl.BlockSpec((2, tm, nb), lambda j, i: (0, i, j)), pl.BlockSpec((tm, nb), lambda j, i: (i, j))],
        out_shape=[jax.ShapeDtypeStruct((2, t_all, f), BF16), jax.ShapeDtypeStruct((t_all, f), BF16)],
        compiler_params=_params(("parallel", "parallel")),
    )(x, w_gu, w_gu)


def _ffn_dact(dy, w_down, gu, *, name, tm, tn):
    t_all, d = dy.shape
    f = w_down.shape[0]
    assert t_all % tm == 0 and f % tn == 0

    def body(dy_ref, w_ref, gu_ref, o_ref):
        dact = lax.dot_general(dy_ref[...], w_ref[...], _NT, preferred_element_type=F32)
        gv, uv = gu_ref[0].astype(F32), gu_ref[1].astype(F32)
        sg = _sigmoid(gv)
        o_ref[0] = (dact * uv * (sg * (1.0 + gv * (1.0 - sg)))).astype(o_ref.dtype)
        o_ref[1] = (dact * gv * sg).astype(o_ref.dtype)

    return pl.pallas_call(
        body, name=name, grid=(f // tn, t_all // tm),
        in_specs=[pl.BlockSpec((tm, d), lambda j, i: (i, 0)), pl.BlockSpec((tn, d), lambda j, i: (j, 0)),
                  pl.BlockSpec((2, tm, tn), lambda j, i: (0, i, j))],
        out_specs=pl.BlockSpec((2, tm, tn), lambda j, i: (0, i, j)),
        out_shape=jax.ShapeDtypeStruct((2, t_all, f), BF16),
        compiler_params=_params(("parallel", "parallel")),
    )(dy, w_down, gu)


def _mla_prep_fwd(proj, qn, kvn, cos, sin, *, name, tm, lq, lkv):
    t_all, w = proj.shape

    def body(p_ref, qn_ref, kvn_ref, cos_ref, sin_ref, cq_ref, ckv_ref, kr_ref):
        pv = p_ref[...]
        xq = pv[:, :lq]
        xkv = pv[:, lq:lq + lkv]
        cq_ref[...] = (xq * lax.rsqrt(jnp.mean(xq * xq, axis=-1, keepdims=True) + RMS_EPS) * qn_ref[...]).astype(BF16)
        ckv_ref[...] = (xkv * lax.rsqrt(jnp.mean(xkv * xkv, axis=-1, keepdims=True) + RMS_EPS) * kvn_ref[...]).astype(BF16)
        kr_ref[...] = _rope(pv[:, lq + lkv:], cos_ref[...], sin_ref[...]).astype(BF16)

    def row(width):
        return pl.BlockSpec((tm, width), lambda i: (i, 0))

    def vec(width):
        return pl.BlockSpec((1, width), lambda i: (0, 0))

    return pl.pallas_call(
        body, name=name, grid=(t_all // tm,),
        in_specs=[row(w), vec(lq), vec(lkv), row(LANES), row(LANES)],
        out_specs=[row(lq), row(lkv), row(LANES)],
        out_shape=[jax.ShapeDtypeStruct((t_all, lq), BF16), jax.ShapeDtypeStruct((t_all, lkv), BF16),
                   jax.ShapeDtypeStruct((t_all, LANES), BF16)],
        compiler_params=_params(("parallel",)),
    )(proj, qn, kvn, cos, sin)


def _mla_prep_bwd(dcq, dckv, dkr_h, proj, qn, kvn, cos, sin, *, name, tm, lq, lkv):
    t_all, w = proj.shape
    n_heads = dkr_h.shape[0]

    def body(dcq_ref, dckv_ref, dkr_ref, p_ref, qn_ref, kvn_ref, cos_ref, sin_ref, dp_ref, dqn_ref, dkvn_ref):
        pv = p_ref[...]
        dxq, dqn = _rms_bwd_math(dcq_ref[...], pv[:, :lq], qn_ref[...])
        dxkv, dkvn = _rms_bwd_math(dckv_ref[...], pv[:, lq:lq + lkv], kvn_ref[...])
        dkr = dkr_ref[0]
        for hh in range(1, n_heads):
            dkr = dkr + dkr_ref[hh]
        dkr = _unrope(dkr, cos_ref[...], sin_ref[...])
        dp_ref[...] = jnp.concatenate([dxq, dxkv, dkr], axis=1).astype(BF16)

        @pl.when(pl.program_id(0) == 0)
        def _():
            dqn_ref[...] = dqn
            dkvn_ref[...] = dkvn

        @pl.when(pl.program_id(0) > 0)
        def _():
            dqn_ref[...] += dqn
            dkvn_ref[...] += dkvn

    def row(width):
        return pl.BlockSpec((tm, width), lambda i: (i, 0))

    def vec(width):
        return pl.BlockSpec((1, width), lambda i: (0, 0))

    return pl.pallas_call(
        body, name=name, grid=(t_all // tm,),
        in_specs=[row(lq), row(lkv), pl.BlockSpec((n_heads, tm, LANES), lambda i: (0, i, 0)), row(w),
                  vec(lq), vec(lkv), row(LANES), row(LANES)],
        out_specs=[row(w), vec(lq), vec(lkv)],
        out_shape=[jax.ShapeDtypeStruct((t_all, w), BF16), jax.ShapeDtypeStruct((1, lq), F32),
                   jax.ShapeDtypeStruct((1, lkv), F32)],
        compiler_params=_params(("arbitrary",)),
    )(dcq, dckv, dkr_h, proj, qn, kvn, cos, sin)


def _rope_q_epilogue(acc, cos_ref, sin_ref):
    parts = []
    for g in range(acc.shape[1] // LANES):
        blk = acc[:, g * LANES:(g + 1) * LANES]
        parts.append(_rope(blk, cos_ref[...], sin_ref[...]) if g % 2 == 1 else blk)
    return jnp.concatenate(parts, axis=1)


def _chunk_causal(rows, cols, row0=0):
    r = row0 + lax.broadcasted_iota(jnp.int32, (rows, cols), 0)
    c = lax.broadcasted_iota(jnp.int32, (rows, cols), 1)
    return (c >> 6) <= (r >> 6)


def _meta_keys(rows, cols):
    return lax.broadcasted_iota(jnp.int32, (rows, cols), 1) < N_META


def _attn_fwd(q, kv, kr, *, name, n_heads, tq, n_real, scale):
    t_all = q.shape[0]
    nq = (n_real - N_META) // tq
    assert N_META + nq * tq == n_real and tq % CHUNK == 0 and t_all >= LANES
    n_pad = t_all - n_real
    sub = tq // 2 if (tq // 2) % CHUNK == 0 else tq

    def body(q_ref, kv_ref, kr_ref, o_ref, lse_ref, k_scr, m_scr, l_scr, acc_scr):
        k_scr[:, :QK_NOPE] = kv_ref[:, :QK_NOPE]
        k_scr[:, QK_NOPE:] = kr_ref[...]
        if n_pad:
            o_ref[pl.ds(n_real, n_pad), :] = jnp.zeros((n_pad, V_HEAD), o_ref.dtype)
            lse_ref[pl.ds(n_real, n_pad), :] = jnp.zeros((n_pad, LANES), F32)

        def scores(qt, c0, width):
            return lax.dot_general(qt, k_scr[pl.ds(c0, width), :], _NT, preferred_element_type=F32) * scale

        def values(c0, width):
            return kv_ref[pl.ds(c0, width), QK_NOPE:]

        s = jnp.where(_meta_keys(LANES, LANES), scores(q_ref[pl.ds(0, LANES), :], 0, LANES), NEG_BIG)
        m = jnp.max(s, axis=-1, keepdims=True)
        p = jnp.exp(s - m)
        l = jnp.sum(p, axis=-1, keepdims=True)
        o_meta = jnp.dot(p.astype(BF16), values(0, LANES), preferred_element_type=F32) / l
        o_ref[pl.ds(0, N_META), :] = o_meta[:N_META].astype(o_ref.dtype)
        lse_ref[pl.ds(0, N_META), :] = jnp.broadcast_to((m + jnp.log(l))[:N_META], (N_META, LANES))

        parts = [(u * sub, sub) for u in range(tq // sub)]

        def accumulate(u0, s, vals):
            rows = pl.ds(u0, s.shape[0])
            m_prev = m_scr[rows, :]
            m_new = jnp.maximum(m_prev, jnp.max(s, axis=-1, keepdims=True))
            alpha = jnp.exp(m_prev - m_new)
            p = jnp.exp(s - m_new)
            l_scr[rows, :] = alpha * l_scr[rows, :] + jnp.sum(p, axis=-1, keepdims=True)
            acc_scr[rows, :] = alpha * acc_scr[rows, :] + jnp.dot(p.astype(BF16), vals, preferred_element_type=F32)
            m_scr[rows, :] = m_new

        def q_tile(i, carry):
            r0 = pl.multiple_of(N_META + i * tq, N_META)
            qts = [q_ref[pl.ds(r0 + u0, rows), :] for u0, rows in parts]
            m_scr[...] = jnp.full(m_scr.shape, NEG_BIG, F32)
            l_scr[...] = jnp.zeros(l_scr.shape, F32)
            acc_scr[...] = jnp.zeros(acc_scr.shape, F32)

            def full_blocks(j, width):
                c0 = pl.multiple_of(N_META + j * tq, N_META)
                for (u0, _), qt in zip(parts, qts):
                    accumulate(u0, scores(qt, c0, width), values(c0, width))

            def two_blocks(jj, c):
                full_blocks(2 * jj, 2 * tq)
                return c

            lax.fori_loop(0, i // 2, two_blocks, 0)

            @pl.when(i % 2 == 1)
            def _():
                full_blocks(i - 1, tq)

            for (u0, rows), qt in zip(parts, qts):
                width = u0 + rows
                s = jnp.concatenate([jnp.where(_meta_keys(rows, LANES), scores(qt, 0, LANES), NEG_BIG),
                                     jnp.where(_chunk_causal(rows, width, u0), scores(qt, r0, width), NEG_BIG)], axis=1)
                accumulate(u0, s, jnp.concatenate([values(0, LANES), values(r0, width)], axis=0))
            o_ref[pl.ds(r0, tq), :] = (acc_scr[...] / l_scr[...]).astype(o_ref.dtype)
            lse_ref[pl.ds(r0, tq), :] = jnp.broadcast_to(m_scr[...] + jnp.log(l_scr[...]), (tq, LANES))
            return carry

        lax.fori_loop(0, nq, q_tile, 0)

    def head(width):
        return pl.BlockSpec((t_all, width), lambda h: (0, h))

    return pl.pallas_call(
        body, name=name, grid=(n_heads,),
        in_specs=[head(HEAD_W), head(HEAD_W), pl.BlockSpec((t_all, LANES), lambda h: (0, 0))],
        out_specs=[head(V_HEAD), pl.BlockSpec((None, t_all, LANES), lambda h: (h, 0, 0))],
        out_shape=[jax.ShapeDtypeStruct((t_all, n_heads * V_HEAD), BF16),
                   jax.ShapeDtypeStruct((n_heads, t_all, LANES), F32)],
        scratch_shapes=[pltpu.VMEM((t_all, HEAD_W), BF16), pltpu.VMEM((tq, 1), F32), pltpu.VMEM((tq, 1), F32),
                        pltpu.VMEM((tq, V_HEAD), F32)],
        compiler_params=_params(("parallel",)),
    )(q, kv, kr)


def _attn_bwd(q, kv, kr, o, lse, do, cos, sin, *, name, n_heads, tq, n_real, scale):
    t_all = q.shape[0]
    nq = (n_real - N_META) // tq
    assert N_META + nq * tq == n_real and tq % CHUNK == 0 and t_all >= LANES
    n_pad = t_all - n_real

    def body(q_ref, kv_ref, kr_ref, o_ref, lse_ref, do_ref, cos_ref, sin_ref, dq_ref, dkv_ref, dkr_ref,
             k_scr, dk_scr, dv_scr, dq_scr):
        k_scr[:, :QK_NOPE] = kv_ref[:, :QK_NOPE]
        k_scr[:, QK_NOPE:] = kr_ref[...]
        dk_scr[...] = jnp.zeros(dk_scr.shape, F32)
        dv_scr[...] = jnp.zeros(dv_scr.shape, F32)
        if n_pad:
            dq_ref[pl.ds(n_real, n_pad), :] = jnp.zeros((n_pad, HEAD_W), dq_ref.dtype)

        def blocks(qt, dot, lse_t, delta, segments):
            kb = jnp.concatenate([k_scr[pl.ds(c0, w), :] for c0, w, _ in segments], axis=0)
            vb = jnp.concatenate([kv_ref[pl.ds(c0, w), QK_NOPE:] for c0, w, _ in segments], axis=0)
            s = lax.dot_general(qt, kb, _NT, preferred_element_type=F32) * scale
            p = jnp.exp(s - lse_t)
            if any(m is not None for _, _, m in segments):
                rows = qt.shape[0]
                mask = jnp.concatenate([jnp.ones((rows, w), jnp.bool_) if m is None else m for _, w, m in segments], axis=1)
                p = jnp.where(mask, p, 0.0)
            dp = lax.dot_general(dot, vb, _NT, preferred_element_type=F32)
            ds = (p * (dp - delta) * scale).astype(BF16)
            dv = lax.dot_general(p.astype(BF16), dot, _TN, preferred_element_type=F32)
            dk = lax.dot_general(ds, qt, _TN, preferred_element_type=F32)
            at = 0
            for c0, w, _ in segments:
                dv_scr[pl.ds(c0, w), :] += dv[at:at + w]
                dk_scr[pl.ds(c0, w), :] += dk[at:at + w]
                at += w
            return jnp.dot(ds, kb, preferred_element_type=F32)

        def block(qt, dot, lse_t, delta, c0, width, mask):
            return blocks(qt, dot, lse_t, delta, [(c0, width, mask)])

        def write_dq(r0, rows, dq):
            cs, sn = cos_ref[pl.ds(r0, rows), :], sin_ref[pl.ds(r0, rows), :]
            dq_ref[pl.ds(r0, rows), :] = jnp.concatenate(
                [dq[:, :QK_NOPE], _unrope(dq[:, QK_NOPE:], cs, sn)], axis=1).astype(dq_ref.dtype)

        rows_m = lax.broadcasted_iota(jnp.int32, (LANES, LANES), 0) < N_META
        dot = do_ref[pl.ds(0, LANES), :]
        delta = jnp.sum(dot.astype(F32) * o_ref[pl.ds(0, LANES), :].astype(F32), axis=-1, keepdims=True)
        dq = block(q_ref[pl.ds(0, LANES), :], dot, lse_ref[pl.ds(0, LANES), :1], delta, 0, LANES,
                   _meta_keys(LANES, LANES) & rows_m)
        write_dq(0, N_META, dq[:N_META])

        def q_tile(i, carry):
            r0 = pl.multiple_of(N_META + i * tq, N_META)
            qt = q_ref[pl.ds(r0, tq), :]
            dot = do_ref[pl.ds(r0, tq), :]
            lse_t = lse_ref[pl.ds(r0, tq), :1]
            delta = jnp.sum(dot.astype(F32) * o_ref[pl.ds(r0, tq), :].astype(F32), axis=-1, keepdims=True)
            dq_scr[...] = blocks(qt, dot, lse_t, delta, [(0, LANES, _meta_keys(tq, LANES)), (r0, tq, _chunk_causal(tq, tq))])

            def two_blocks(jj, c):
                c0 = pl.multiple_of(N_META + 2 * jj * tq, N_META)
                dq_scr[...] += block(qt, dot, lse_t, delta, c0, 2 * tq, None)
                return c

            lax.fori_loop(0, i // 2, two_blocks, 0)

            @pl.when(i % 2 == 1)
            def _():
                c0 = pl.multiple_of(N_META + (i - 1) * tq, N_META)
                dq_scr[...] += block(qt, dot, lse_t, delta, c0, tq, None)

            write_dq(r0, tq, dq_scr[...])
            return carry

        lax.fori_loop(0, nq, q_tile, 0)
        dk = dk_scr[...]
        dkv_ref[...] = jnp.concatenate([dk[:, :QK_NOPE], dv_scr[...]], axis=1).astype(dkv_ref.dtype)
        dkr_ref[...] = dk[:, QK_NOPE:]

    def head(width):
        return pl.BlockSpec((t_all, width), lambda h: (0, h))

    table = pl.BlockSpec((t_all, LANES), lambda h: (0, 0))
    per_head = pl.BlockSpec((None, t_all, LANES), lambda h: (h, 0, 0))
    return pl.pallas_call(
        body, name=name, grid=(n_heads,),
        in_specs=[head(HEAD_W), head(HEAD_W), table, head(V_HEAD), per_head, head(V_HEAD), table, table],
        out_specs=[head(HEAD_W), head(HEAD_W), per_head],
        out_shape=[jax.ShapeDtypeStruct((t_all, n_heads * HEAD_W), BF16), jax.ShapeDtypeStruct((t_all, n_heads * HEAD_W), BF16),
                   jax.ShapeDtypeStruct((n_heads, t_all, LANES), F32)],
        scratch_shapes=[pltpu.VMEM((t_all, HEAD_W), BF16), pltpu.VMEM((t_all, HEAD_W), F32), pltpu.VMEM((t_all, V_HEAD), F32),
                        pltpu.VMEM((tq, HEAD_W), F32)],
        compiler_params=_params(("parallel",)),
    )(q, kv, kr, o, lse, do, cos, sin)


LRU_ROWS = 128


def _shifted_back(ref, t0, rows, shift_max):
    main = ref[pl.ds(t0, rows), :]
    prev = ref[pl.ds(pl.multiple_of(jnp.maximum(t0 - SUBLANES, 0), SUBLANES), SUBLANES), :]
    prev = jnp.where(t0 > 0, prev, 0.0)
    ext = jnp.concatenate([prev, main], axis=0)
    return [main] + [pltpu.roll(ext, s, 0)[SUBLANES:, :] for s in range(1, shift_max + 1)]


def _shifted_ahead(ref, t0, rows, t_all, shift_max):
    main = ref[pl.ds(t0, rows), :]
    nxt = ref[pl.ds(pl.multiple_of(jnp.minimum(t0 + rows, t_all - SUBLANES), SUBLANES), SUBLANES), :]
    nxt = jnp.where(t0 + rows < t_all, nxt, 0.0)
    ext = jnp.concatenate([main, nxt], axis=0)
    return [main] + [pltpu.roll(ext, rows + SUBLANES - s, 0)[:rows, :] for s in range(1, shift_max + 1)]


def _conv_fwd(xp_ref, t0, rows, cw, cb):
    sh = _shifted_back(xp_ref, t0, rows, 3)
    out = cb + cw[3:4, :] * sh[0]
    for k in range(3):
        out = out + cw[k:k + 1, :] * sh[3 - k]
    return out, sh


def _lru_gates(xb, wga, bga, wgx, bgx, sp):
    xbb = xb.astype(BF16)
    r = _sigmoid(jnp.dot(xbb, wga, preferred_element_type=F32) + bga)
    ig = _sigmoid(jnp.dot(xbb, wgx, preferred_element_type=F32) + bgx)
    la = -LRU_C * r * sp
    a = jnp.exp(la)
    s = jnp.sqrt(_neg_expm1(2.0 * la))
    return xbb, r, ig, a, s


def _scan_tile(a, b, reverse):
    rows = a.shape[0]
    ridx = lax.broadcasted_iota(jnp.int32, a.shape, 0)
    s = 1
    while s < rows:
        if reverse:
            keep = ridx < rows - s
            a_sh, b_sh = pltpu.roll(a, rows - s, 0), pltpu.roll(b, rows - s, 0)
        else:
            keep = ridx >= s
            a_sh, b_sh = pltpu.roll(a, s, 0), pltpu.roll(b, s, 0)
        b = jnp.where(keep, a * b_sh + b, b)
        a = jnp.where(keep, a * a_sh, a)
        s *= 2
    return a, b


def _lru_fwd(xy, conv_w, conv_b, wga, bga, wgx, bgx, lam, *, name):
    t_all = xy.shape[0]
    dr = xy.shape[1] // 2
    c = LANES
    nblk = dr // c
    rows = LRU_ROWS
    nt = t_all // rows

    def body(xp_ref, yp_ref, cw_ref, cb_ref, wga_ref, bga_ref, wgx_ref, bgx_ref, lam_ref, hs_ref, hsy_ref):
        cw, cb = cw_ref[...], cb_ref[...]
        sp = _softplus_neg(lam_ref[...])

        def tile(t, h_in):
            t0 = pl.multiple_of(t * rows, rows)
            xb, _ = _conv_fwd(xp_ref, t0, rows, cw, cb)
            _, _, ig, a, s = _lru_gates(xb, wga_ref[0], bga_ref[...], wgx_ref[0], bgx_ref[...], sp)
            cum_a, h0 = _scan_tile(a, s * (ig * xb), reverse=False)
            hs = cum_a * h_in + h0
            hs_ref[pl.ds(t0, rows), :] = hs
            hsy_ref[pl.ds(t0, rows), :] = (hs * _gelu(yp_ref[pl.ds(t0, rows), :])).astype(BF16)
            return hs[rows - 1:, :]

        lax.fori_loop(0, nt, tile, jnp.zeros((1, c), F32))

    col = pl.BlockSpec((t_all, c), lambda b: (0, b))
    vec = pl.BlockSpec((1, c), lambda b: (0, b))
    wsp = pl.BlockSpec((1, c, c), lambda b: (b, 0, 0))
    return pl.pallas_call(
        body, name=name, grid=(nblk,),
        in_specs=[col, pl.BlockSpec((t_all, c), lambda b: (0, nblk + b)), pl.BlockSpec((4, c), lambda b: (0, b)), vec,
                  wsp, vec, wsp, vec, vec],
        out_specs=[col, col],
        out_shape=[jax.ShapeDtypeStruct((t_all, dr), F32), jax.ShapeDtypeStruct((t_all, dr), BF16)],
        compiler_params=_params(("parallel",)),
    )(xy, xy, conv_w, conv_b, wga, bga, wgx, bgx, lam)


def _lru_bwd(xy, hs, dhsy, conv_w, conv_b, wga, bga, wgx, bgx, lam, *, name):
    t_all = xy.shape[0]
    dr = xy.shape[1] // 2
    c = LANES
    nblk = dr // c
    rows = LRU_ROWS
    nt = t_all // rows

    def body(xp_ref, yp_ref, hs_ref, dh_ref, cw_ref, cb_ref, wga_ref, bga_ref, wgx_ref, bgx_ref, lam_ref,
             dxp_ref, dyp_ref, dcw_ref, dcb_ref, dwga_ref, dbga_ref, dwgx_ref, dbgx_ref, dlam_ref,
             xb_scr, r_scr, i_scr, a_scr):
        cw, cb = cw_ref[...], cb_ref[...]
        lamv = lam_ref[...]
        sp = _softplus_neg(lamv)
        sig_neg = 1.0 / (1.0 + jnp.exp(lamv))
        wga_v, wgx_v = wga_ref[0], wgx_ref[0]

        def recompute(t, carry):
            t0 = pl.multiple_of(t * rows, rows)
            xb, _ = _conv_fwd(xp_ref, t0, rows, cw, cb)
            _, r, ig, a, _ = _lru_gates(xb, wga_v, bga_ref[...], wgx_v, bgx_ref[...], sp)
            xb_scr[pl.ds(t0, rows), :] = xb
            r_scr[pl.ds(t0, rows), :] = r
            i_scr[pl.ds(t0, rows), :] = ig
            a_scr[pl.ds(t0, rows), :] = a
            return carry

        lax.fori_loop(0, nt, recompute, 0)
        dwga_ref[...] = jnp.zeros(dwga_ref.shape, F32)
        dwgx_ref[...] = jnp.zeros(dwgx_ref.shape, F32)

        def tile(ti, carry):
            lam_in, dbga, dbgx, dlam, dcw, dcb = carry
            t = nt - 1 - ti
            t0 = pl.multiple_of(t * rows, rows)
            a_now, a_next = _shifted_ahead(a_scr, t0, rows, t_all, 1)
            yp = yp_ref[pl.ds(t0, rows), :]
            dhy = dh_ref[pl.ds(t0, rows), :]
            cum_a, lam0 = _scan_tile(a_next, dhy * _gelu(yp), reverse=True)
            lam_t = cum_a * lam_in + lam0
            hs_now, hs_prev = _shifted_back(hs_ref, t0, rows, 1)
            da = lam_t * hs_prev
            xb = xb_scr[pl.ds(t0, rows), :]
            r = r_scr[pl.ds(t0, rows), :]
            ig = i_scr[pl.ds(t0, rows), :]
            la = -LRU_C * r * sp
            s = jnp.sqrt(_neg_expm1(2.0 * la))
            d_ixb = lam_t * s
            dla = da * a_now - (lam_t * ig * xb) * (a_now * a_now / s)
            dzr = dla * (-LRU_C * sp) * r * (1.0 - r)
            dzi = d_ixb * xb * ig * (1.0 - ig)
            dzr_b, dzi_b = dzr.astype(BF16), dzi.astype(BF16)
            xbb = xb.astype(BF16)
            dwga_ref[0] += lax.dot_general(xbb, dzr_b, _TN, preferred_element_type=F32)
            dwgx_ref[0] += lax.dot_general(xbb, dzi_b, _TN, preferred_element_type=F32)
            dxb = (d_ixb * ig + lax.dot_general(dzr_b, wga_v, _NT, preferred_element_type=F32)
                   + lax.dot_general(dzi_b, wgx_v, _NT, preferred_element_type=F32))
            xb_scr[pl.ds(t0, rows), :] = dxb
            dyp_ref[pl.ds(t0, rows), :] = (dhy * hs_now * _gelu_grad(yp)).astype(BF16)
            ahead = _shifted_ahead(xb_scr, t0, rows, t_all, 3)
            dxp = cw[3:4, :] * ahead[0]
            for k in range(3):
                dxp = dxp + cw[k:k + 1, :] * ahead[3 - k]
            dxp_ref[pl.ds(t0, rows), :] = dxp.astype(BF16)
            back = _shifted_back(xp_ref, t0, rows, 3)
            dcw_t = jnp.concatenate([jnp.sum(dxb * back[3 - k], axis=0, keepdims=True) for k in range(4)], axis=0)
            return (lam_t[:1, :], dbga + jnp.sum(dzr, axis=0, keepdims=True), dbgx + jnp.sum(dzi, axis=0, keepdims=True),
                    dlam + jnp.sum(dla * r, axis=0, keepdims=True), dcw + dcw_t, dcb + jnp.sum(dxb, axis=0, keepdims=True))

        zero = jnp.zeros((1, c), F32)
        _, dbga, dbgx, dlam, dcw, dcb = lax.fori_loop(0, nt, tile, (zero, zero, zero, zero, jnp.zeros((4, c), F32), zero))
        dbga_ref[...] = dbga
        dbgx_ref[...] = dbgx
        dlam_ref[...] = dlam * (LRU_C * sig_neg)
        dcw_ref[...] = dcw
        dcb_ref[...] = dcb

    col = pl.BlockSpec((t_all, c), lambda b: (0, b))
    col2 = pl.BlockSpec((t_all, c), lambda b: (0, nblk + b))
    vec = pl.BlockSpec((1, c), lambda b: (0, b))
    tap = pl.BlockSpec((4, c), lambda b: (0, b))
    wsp = pl.BlockSpec((1, c, c), lambda b: (b, 0, 0))
    vshape = jax.ShapeDtypeStruct((1, dr), F32)
    wshape = jax.ShapeDtypeStruct((nblk, c, c), F32)
    def planes_body(*refs):
        dxy_ref = refs[11]
        body(*refs[:11], dxy_ref.at[0], dxy_ref.at[1], *refs[12:])

    return pl.pallas_call(
        planes_body, name=name, grid=(nblk,),
        in_specs=[col, col2, col, col, tap, vec, wsp, vec, wsp, vec, vec],
        out_specs=[pl.BlockSpec((2, t_all, c), lambda b: (0, 0, b)), tap, vec, wsp, vec, wsp, vec, vec],
        out_shape=[jax.ShapeDtypeStruct((2, t_all, dr), BF16),
                   jax.ShapeDtypeStruct((4, dr), F32), vshape, wshape, vshape, wshape, vshape, vshape],
        scratch_shapes=[pltpu.VMEM((t_all, c), F32)] * 4,
        compiler_params=_params(("parallel",)),
    )(xy, xy, hs, dhsy, conv_w, conv_b, wga, bga, wgx, bgx, lam)


def _mesh_pos():
    return lax.axis_index("x"), lax.axis_index("y"), lax.axis_index("c")


def _all_gather(shards, *, name):
    n = len(shards)

    def body(*refs):
        ins, outs, token = refs[:n], refs[n:2 * n], refs[2 * n]
        send_sems, recv_sems, local_sems = refs[2 * n + 1:]
        token[...] = jnp.zeros(token.shape, token.dtype)
        x, y, c = _mesh_pos()
        me, sibling = (x, y, c), (x, y, 1 - c)
        chips = [(1 - x, y), (x, 1 - y), (1 - x, 1 - y)]
        slot = _slot

        def copy(a, k, block, to, src=None):
            dst = outs[a].at[slot(block)]
            return pltpu.make_async_remote_copy(
                src_ref=dst if src is None else src, dst_ref=dst, send_sem=send_sems.at[a, k],
                recv_sem=recv_sems.at[a, k], device_id=to, device_id_type=MESH)

        mine = [pltpu.make_async_copy(ins[a], outs[a].at[slot(me)], local_sems.at[a]) for a in range(n)]
        for cp in mine:
            cp.start()
        first = []
        for a in range(n):
            first.append(copy(a, 0, me, sibling, src=ins[a]))
            first += [copy(a, 1 + j, me, (*chip, c), src=ins[a]) for j, chip in enumerate(chips)]
        for cp in first:
            cp.start()
        passed = []
        for a in range(n):
            for j, chip in enumerate(chips):
                copy(a, 1 + j, (*chip, c), me).wait_recv()
                fwd = copy(a, 4 + j, (*chip, c), sibling)
                fwd.start()
                passed.append(fwd)
        for a in range(n):
            copy(a, 0, sibling, me).wait_recv()
            for j, chip in enumerate(chips):
                copy(a, 4 + j, (*chip, 1 - c), me).wait_recv()
        for cp in first + passed:
            cp.wait_send()
        for cp in mine:
            cp.wait()

    any_spec = pl.BlockSpec(memory_space=pl.ANY)
    outs = pl.pallas_call(
        body, name=name,
        in_specs=[any_spec] * n, out_specs=[any_spec] * n + [pl.BlockSpec(memory_space=pltpu.VMEM)],
        out_shape=[jax.ShapeDtypeStruct((N_DEV,) + s.shape, s.dtype) for s in shards]
        + [jax.ShapeDtypeStruct((SUBLANES, LANES), F32)],
        scratch_shapes=[pltpu.SemaphoreType.DMA((n, 7)), pltpu.SemaphoreType.DMA((n, 7)), pltpu.SemaphoreType.DMA((n,))],
    )(*shards)
    return list(outs[:n]), outs[n][0, 0]


_HBM = pl.BlockSpec(memory_space=pltpu.HBM)
_SEM = pl.BlockSpec(memory_space=pltpu.SEMAPHORE)
_ANY = pl.BlockSpec(memory_space=pl.ANY)
_EFFECT = pltpu.SideEffectType.DATAFLOW_SIDE_EFFECTING


def _slot(p):
    return 4 * p[0] + 2 * p[1] + p[2]


def _remote(src, dst, send, recv, idx, to):
    return pltpu.make_async_remote_copy(src_ref=src, dst_ref=dst, send_sem=send.at[idx], recv_sem=recv.at[idx],
                                        device_id=to, device_id_type=MESH)


def _ag_plan_own(a, src, land, send, recv):
    x, y, c = _mesh_pos()
    dst = land.at[_slot((x, y, c))]
    targets = [(x, y, 1 - c), (1 - x, y, c), (x, 1 - y, c), (1 - x, 1 - y, c)]
    return [_remote(src, dst, send, recv, 4 * a + k, to) for k, to in enumerate(targets)]


def _ag_plan_pass(a, src, land, send, recv):
    x, y, c = _mesh_pos()
    blocks = [land.at[_slot((px, py, c))] for px, py in ((1 - x, y), (x, 1 - y), (1 - x, 1 - y))]
    return [_remote(blk, blk, send, recv, 3 * a + k, (x, y, 1 - c)) for k, blk in enumerate(blocks)]


def _rs_plan_sibling(a, src, land, send, recv):
    x, y, c = _mesh_pos()
    return [_remote(src.at[2 * j + (1 - c)], land.at[j], send, recv, 4 * a + j, (x, y, 1 - c)) for j in range(4)]


def _rs_plan_chips(a, src, land, send, recv):
    x, y, c = _mesh_pos()
    out = []
    for k in (1, 2, 3):
        px = 1 - x if k & 2 else x
        py = 1 - y if k & 1 else y
        out.append(_remote(src.at[2 * px + py], land.at[k - 1], send, recv, 3 * a + k - 1, (px, py, c)))
    return out


def _in_hbm(a):
    return pltpu.with_memory_space_constraint(a, pltpu.HBM)


def _exchange_start(srcs, lands, plan, n_k, *, name):
    ns, n = len(srcs), len(lands)

    def body(*refs):
        src_refs, land_refs = refs[:ns], refs[ns:ns + n]
        send, recv = refs[ns + n], refs[ns + n + 1]
        token = refs[-1]
        for a in range(n):
            for cp in plan(a, src_refs[a] if ns else None, land_refs[a], send, recv):
                cp.start()
        token[...] = jnp.zeros(token.shape, token.dtype)

    bufs = list(srcs) + list(lands)
    outs = pl.pallas_call(
        body, name=name,
        out_shape=(pltpu.SemaphoreType.DMA((n * n_k,)), pltpu.SemaphoreType.DMA((n * n_k,)),
                   *[pltpu.HBM(b.shape, b.dtype) for b in bufs], jax.ShapeDtypeStruct((SUBLANES, LANES), F32)),
        in_specs=[_HBM] * (ns + n),
        out_specs=(_SEM, _SEM, *[_HBM] * (ns + n), pl.BlockSpec(memory_space=pltpu.VMEM)),
        input_output_aliases={i: 2 + i for i in range(ns + n)},
        compiler_params=pltpu.CompilerParams(has_side_effects=_EFFECT),
    )(*[_in_hbm(b) for b in bufs])
    return outs[0], outs[1], list(outs[2:2 + ns]), list(outs[2 + ns:2 + ns + n]), outs[-1]


def _exchange_wait(started, plan, after, *, name):
    send, recv, srcs, lands, _ = started
    ns, n = len(srcs), len(lands)

    def body(*refs):
        src_refs, land_refs = refs[:ns], refs[ns:ns + n]
        send_ref, recv_ref = refs[ns + n], refs[ns + n + 1]
        for a in range(n):
            for cp in plan(a, src_refs[a] if ns else None, land_refs[a], send_ref, recv_ref):
                cp.wait_send()
                cp.wait_recv()

    bufs = list(srcs) + list(lands)
    outs = pl.pallas_call(
        body, name=name,
        out_shape=tuple(pltpu.HBM(b.shape, b.dtype) for b in bufs),
        in_specs=[_HBM] * (ns + n) + [_SEM, _SEM, _ANY],
        out_specs=tuple([_HBM] * (ns + n)),
        input_output_aliases={i: i for i in range(ns + n)},
        compiler_params=pltpu.CompilerParams(has_side_effects=_EFFECT),
    )(*bufs, send, recv, after)
    return list(outs[:ns]), list(outs[ns:])


def _pair_add(grads, landed, core, *, name, tr):
    _, r_all, c_all = grads.shape

    def body(core_ref, g_ref, l_ref, o_ref):
        o_ref[...] = (g_ref[...].astype(F32) + l_ref[...].astype(F32)).astype(o_ref.dtype)

    return pl.pallas_call(
        body, name=name,
        grid_spec=pltpu.PrefetchScalarGridSpec(
            num_scalar_prefetch=1, grid=(4, r_all // tr),
            in_specs=[pl.BlockSpec((None, tr, c_all), lambda j, i, core_ref: (2 * j + core_ref[0], i, 0)),
                      pl.BlockSpec((None, tr, c_all), lambda j, i, core_ref: (j, i, 0))],
            out_specs=pl.BlockSpec((None, tr, c_all), lambda j, i, core_ref: (j, i, 0))),
        out_shape=jax.ShapeDtypeStruct((4, r_all, c_all), grads.dtype),
        compiler_params=_params(("parallel", "parallel")),
    )(core, grads, landed)


def _adamw_math(w, g, m, v):
    m2 = ADAM_B1 * m + (1.0 - ADAM_B1) * g
    v2 = ADAM_B2 * v + (1.0 - ADAM_B2) * (g * g)
    m_hat = m2 / (1.0 - ADAM_B1 ** ADAM_STEP)
    v_hat = v2 / (1.0 - ADAM_B2 ** ADAM_STEP)
    delta = -ADAM_LR * (m_hat / (jnp.sqrt(v_hat) + ADAM_EPS) + ADAM_WD * w)
    return delta, m2, v2


def _adamw(w, m, v, terms, order, *, name, tr, col_block=None, own=None, stack=None):
    r_all, c_all = w.shape
    n_slots = terms.shape[0]

    def body(*refs):
        if col_block is not None or own is not None:
            refs = refs[1:]
        own_ref = None
        if own is not None:
            own_ref, refs = refs[0], refs[1:]
        w_ref, m_ref, v_ref, t_ref, g_ref, d_ref, m2_ref, v2_ref = refs
        if own_ref is not None:
            g = own_ref[...].astype(F32) + t_ref[order[0]].astype(F32)
        else:
            g = t_ref[order[0]].astype(F32)
        for s in order[1:]:
            g = g + t_ref[s].astype(F32)
        delta, m2, v2 = _adamw_math(w_ref[...], g, m_ref[...], v_ref[...])
        g_ref[...] = g
        d_ref[...] = delta
        m2_ref[...] = m2
        v2_ref[...] = v2

    shape = jax.ShapeDtypeStruct((r_all, c_all), F32)
    if own is not None:
        layer, n_layers, prev = stack
        row = pl.BlockSpec((tr, c_all), lambda i, idx: (i, 0))
        slab = pl.BlockSpec((None, tr, c_all), lambda i, idx: (layer, i, 0))
        carried = [] if prev is None else list(prev)

        def stacked_body(*refs):
            body(*refs[:6], *refs[6 + len(carried):])

        return pl.pallas_call(
            stacked_body, name=name,
            grid_spec=pltpu.PrefetchScalarGridSpec(
                num_scalar_prefetch=1, grid=(r_all // tr,),
                in_specs=[pl.BlockSpec((None, tr, c_all), lambda i, idx: (idx[0], i, 0)), row, row, row,
                          pl.BlockSpec((n_slots, tr, c_all), lambda i, idx: (0, i, 0))] + [_ANY] * len(carried),
                out_specs=[slab] * 4),
            out_shape=[jax.ShapeDtypeStruct((n_layers, r_all, c_all), F32)] * 4,
            input_output_aliases={6 + k: k for k in range(len(carried))},
            compiler_params=_params(("parallel",)),
        )(own[1], own[0], w, m, v, terms, *carried)
    if col_block is None:
        row = pl.BlockSpec((tr, c_all), lambda i: (i, 0))
        return pl.pallas_call(
            body, name=name, grid=(r_all // tr,),
            in_specs=[row, row, row, pl.BlockSpec((n_slots, tr, c_all), lambda i: (0, i, 0))],
            out_specs=[row] * 4, out_shape=[shape] * 4, compiler_params=_params(("parallel",)),
        )(w, m, v, terms)
    row = pl.BlockSpec((tr, c_all), lambda i, blk: (i, 0))
    return pl.pallas_call(
        body, name=name,
        grid_spec=pltpu.PrefetchScalarGridSpec(
            num_scalar_prefetch=1, grid=(r_all // tr,),
            in_specs=[row, row, row, pl.BlockSpec((n_slots, tr, c_all), lambda i, blk: (0, i, blk[0]))],
            out_specs=[row] * 4),
        out_shape=[shape] * 4, compiler_params=_params(("parallel",)),
    )(col_block, w, m, v, terms)


def _rope_tables(t_all):
    pos = jnp.arange(t_all, dtype=F32)
    inv_freq = ROPE_THETA ** (-jnp.arange(0, QK_ROPE, 2, dtype=F32) / QK_ROPE)
    ang = pos[:, None] * inv_freq[None, :]
    cos, sin = jnp.cos(ang), jnp.sin(ang)
    return jnp.tile(cos, (1, LANES // (QK_ROPE // 2))), jnp.tile(sin, (1, LANES // (QK_ROPE // 2)))


def _adam_row_tile(r_all, c_all, block_bytes=512 * 1024):
    target = max(SUBLANES, block_bytes // (4 * c_all))
    return _pick(r_all, [t for t in (1024, 704, 512, 352, 256, 176, 128, 64, 32, 16, 8) if t <= target])


def _rows_natural(wg):
    return wg.reshape(wg.shape[0] * wg.shape[1], wg.shape[2])


def _mla_layer_fwd(tag, h, g_mix, ws, qn, kvn, cos, sin, *, tm, tq, n_heads, scale, n_real):
    w_in, w_uq, w_ukv, w_o = _rows_natural(ws[0]), ws[1], ws[2], _rows_natural(ws[3])
    t_all, d = h.shape
    lq, lkv = qn.shape[1], kvn.shape[1]
    tmb = _pick(t_all, _ROW_TILES)
    hn = _rmsnorm_fwd(h, g_mix, name=f"norm_mix{tag}", tm=tm)
    proj = _mm_nn(hn, w_in, name=f"mla_in{tag}", out_dtype=F32, tm=tmb, tn=w_in.shape[1], tk=_pick(d, _DIVS))
    cq, ckv, kr = _mla_prep_fwd(proj, qn, kvn, cos, sin, name=f"mla_prep{tag}", tm=tm, lq=lq, lkv=lkv)
    q = _mm_nn(cq, w_uq, name=f"mla_q{tag}", out_dtype=BF16, tm=tmb, tn=w_uq.shape[2], tk=lq, b_blocked=True,
               epilogue=_rope_q_epilogue, extras=(cos, sin))
    kv = _mm_nn(ckv, w_ukv, name=f"mla_kv{tag}", out_dtype=BF16, tm=tmb, tn=w_ukv.shape[2], tk=lkv, b_blocked=True)
    o, lse = _attn_fwd(q, kv, kr, name=f"attn_fwd{tag}", n_heads=n_heads, tq=tq, n_real=n_real, scale=scale)
    h_mid = _mm_nn(o, w_o, name=f"mla_o{tag}", out_dtype=F32, tm=tm, tn=d, tk=o.shape[1], res=h)
    return h_mid, (hn, proj, cq, ckv, kr, q, kv, o, lse)


def _mla_layer_bwd(tag, dh, dh_b, h_in, saved, g_mix, ws, qn, kvn, cos, sin, *, tm, tq, n_heads, scale, n_real, early=None):
    hn, proj, cq, ckv, kr, q, kv, o, lse = saved
    w_in, w_uq, w_ukv, w_o = _rows_natural(ws[0]), ws[1], ws[2], _rows_natural(ws[3])
    t_all, d = h_in.shape
    lq, lkv = qn.shape[1], kvn.shape[1]
    ov = o.shape[1]
    tmb = _pick(t_all, _ROW_TILES)
    tn_d, tk_d = _pick(d, _DIVS[1:]), _pick(d, _DIVS)
    do = _mm_nt(dh_b, w_o, name=f"mla_do{tag}", out_dtype=BF16, tm=tmb, tn=_pick(ov, _DIVS[1:]), tk=tk_d)
    dw_o = _mm_tn(o, dh_b, name=f"mla_dwo{tag}", out_dtype=BF16, tm=_pick(ov, _DIVS[2:]), tn=tn_d, tk=t_all)
    dq, dkv, dkr_h = _attn_bwd(q, kv, kr, o, lse, do, cos, sin, name=f"attn_bwd{tag}", n_heads=n_heads, tq=tq, n_real=n_real,
                               scale=scale)
    hw, kw = w_uq.shape[2], w_ukv.shape[2]
    dw_uq = _mm_tn(cq, dq, name=f"mla_dwuq{tag}", out_dtype=BF16, tm=lq, tn=hw, tk=t_all, out_block=hw)
    dcq = _mm_nt(dq, w_uq, name=f"mla_dcq{tag}", out_dtype=F32, tm=tm, tn=lq, tk=dq.shape[1], b_blocked=True)
    dw_ukv = _mm_tn(ckv, dkv, name=f"mla_dwukv{tag}", out_dtype=BF16, tm=lkv, tn=kw, tk=t_all, out_block=kw)
    dckv = _mm_nt(dkv, w_ukv, name=f"mla_dckv{tag}", out_dtype=F32, tm=tm, tn=lkv, tk=dkv.shape[1], b_blocked=True)
    first = [dw_uq, dw_ukv, dw_o.reshape(N_DEV, -1, d)]
    if early is not None:
        qn = qn + early(first)
    dproj, dqn, dkvn = _mla_prep_bwd(dcq, dckv, dkr_h, proj, qn, kvn, cos, sin, name=f"mla_prep_bwd{tag}", tm=tm, lq=lq, lkv=lkv)
    wc = w_in.shape[1]
    dw_in = _mm_tn(hn, dproj, name=f"mla_dwin{tag}", out_dtype=BF16, tm=_pick(d, _DIVS[2:]), tn=wc, tk=t_all)
    dhn = _mm_nt(dproj, w_in, name=f"mla_dhn{tag}", out_dtype=BF16, tm=tmb, tn=tn_d, tk=wc)
    dh, dh_b, dg = _rmsnorm_bwd(dhn, h_in, g_mix, dh, name=f"norm_mix_bwd{tag}", tm=tm)
    return dh, dh_b, dg, dqn, dkvn, [dw_in.reshape(N_DEV, -1, wc)] + ([] if early is not None else first)


def _lru_layer_fwd(tag, h, g_mix, ws, small, *, tm):
    w_lin, w_lo = ws[0], _rows_natural(ws[1])
    t_all, d = h.shape
    dr = w_lo.shape[0]
    tmb = _pick(t_all, _ROW_TILES)
    hn = _rmsnorm_fwd(h, g_mix, name=f"norm_mix{tag}", tm=tm)
    xy = _mm_nn(hn, w_lin, name=f"lru_in{tag}", out_dtype=F32, tm=tmb, tn=w_lin.shape[2], tk=_pick(d, _DIVS), b_blocked=True,
                rows_outer=True)
    hs, hsy = _lru_fwd(xy, *small, name=f"lru_fwd{tag}")
    h_mid = _mm_nn(hsy, w_lo, name=f"lru_o{tag}", out_dtype=F32, tm=tm, tn=d, tk=dr, res=h)
    return h_mid, (hn, xy, hs, hsy)


def _lru_layer_bwd(tag, dh, dh_b, h_in, saved, g_mix, ws, small, *, tm):
    hn, xy, hs, hsy = saved
    w_lin, w_lo = ws[0], _rows_natural(ws[1])
    t_all, d = h_in.shape
    dr = w_lo.shape[0]
    tmb = _pick(t_all, _ROW_TILES)
    tn_d, tk_d = _pick(d, _DIVS[1:]), _pick(d, _DIVS)
    dhsy = _mm_nt(dh_b, w_lo, name=f"lru_dhsy{tag}", out_dtype=F32, tm=tmb, tn=_pick(dr, _DIVS[1:]), tk=tk_d)
    dw_lo = _mm_tn(hsy, dh_b, name=f"lru_dwo{tag}", out_dtype=BF16, tm=_pick(dr, _DIVS[2:]), tn=tn_d, tk=t_all)
    dxy, *dsmall = _lru_bwd(xy, hs, dhsy, *small, name=f"lru_bwd{tag}")
    lw = w_lin.shape[2]
    dw_lin = _mm_tn(hn, dxy, name=f"lru_dwin{tag}", out_dtype=BF16, tm=tn_d, tn=lw, tk=t_all, out_block=lw)
    dhn = _mm_nt(dxy, w_lin, name=f"lru_dhn{tag}", out_dtype=BF16, tm=tm, tn=tn_d, tk=2 * dr, b_blocked=True)
    dh, dh_b, dg = _rmsnorm_bwd(dhn, h_in, g_mix, dh, name=f"norm_mix_bwd{tag}", tm=tm)
    return dh, dh_b, dg, tuple(dsmall), [dw_lin, dw_lo.reshape(N_DEV, -1, d)]


def _ffn_layer_fwd(tag, h_mid, g_ffn, ws, *, tm):
    w_gu, w_down = ws[0], _rows_natural(ws[1])
    t_all, d = h_mid.shape
    f_all = w_down.shape[0]
    tmb = _pick(t_all, _ROW_TILES)
    fk = _pick(f_all, (1408,) + _DIVS[1:])
    hn2 = _rmsnorm_fwd(h_mid, g_ffn, name=f"norm_ffn{tag}", tm=tm)
    gu, act = _ffn_up(hn2, w_gu, name=f"ffn_up{tag}", tm=tm)
    h_out = _mm_nn(act, w_down, name=f"ffn_down{tag}", out_dtype=F32, tm=tm, tn=_pick(d, _DIVS[1:]), tk=f_all, res=h_mid)
    return h_out, (hn2, gu, act)


def _ffn_layer_bwd(tag, dh, dh_b, h_mid, saved, g_ffn, ws, *, tm):
    hn2, gu, act = saved
    w_gu, w_down = ws[0], _rows_natural(ws[1])
    t_all, d = h_mid.shape
    f_all = w_down.shape[0]
    f_local = w_gu.shape[2]
    tmb = _pick(t_all, _ROW_TILES)
    fk = _pick(f_all, (1408,) + _DIVS[1:])
    tn_d, tk_d = _pick(d, _DIVS[1:]), _pick(d, _DIVS)
    dgu = _ffn_dact(dh_b, w_down, gu, name=f"ffn_dact{tag}", tm=tm, tn=f_local)
    dw_down = _mm_tn(act, dh_b, name=f"ffn_dwdown{tag}", out_dtype=BF16, tm=fk, tn=_pick(d, _DIVS[2:]), tk=t_all)
    dhn2 = _mm_nt(dgu, w_gu, name=f"ffn_dhn{tag}", out_dtype=BF16, tm=tm, tn=_pick(d, _DIVS[2:]), tk=2 * f_all, b_blocked=True)
    dw_gu = _mm_tn(hn2, dgu, name=f"ffn_dwgu{tag}", out_dtype=BF16, tm=_pick(d, _DIVS[2:]), tn=f_local, tk=t_all, out_block=f_local,
                   cols_outer=True)
    dh, dh_b, dg = _rmsnorm_bwd(dhn2, h_mid, g_ffn, dh, name=f"norm_ffn_bwd{tag}", tm=tm)
    return dh, dh_b, dg, [dw_gu, dw_down.reshape(N_DEV, -1, d)]


def kernel(x, meta_tokens, norm_mix, norm_ffn, norm_final, mla_w_in, mla_q_norm, mla_kv_norm, mla_w_uq, mla_w_ukv, mla_w_o, lru_w_in, lru_conv_w, lru_conv_b, lru_w_gate_a, lru_b_gate_a, lru_w_gate_x, lru_b_gate_x, lru_lambda, lru_w_o, ffn_w_gu, ffn_w_down, loss_target, m_meta_tokens, m_norm_mix, m_norm_ffn, m_norm_final, m_mla_w_in, m_mla_q_norm, m_mla_kv_norm, m_mla_w_uq, m_mla_w_ukv, m_mla_w_o, m_lru_w_in, m_lru_conv_w, m_lru_conv_b, m_lru_w_gate_a, m_lru_b_gate_a, m_lru_w_gate_x, m_lru_b_gate_x, m_lru_lambda, m_lru_w_o, m_ffn_w_gu, m_ffn_w_down, v_meta_tokens, v_norm_mix, v_norm_ffn, v_norm_final, v_mla_w_in, v_mla_q_norm, v_mla_kv_norm, v_mla_w_uq, v_mla_w_ukv, v_mla_w_o, v_lru_w_in, v_lru_conv_w, v_lru_conv_b, v_lru_w_gate_a, v_lru_b_gate_a, v_lru_w_gate_x, v_lru_b_gate_x, v_lru_lambda, v_lru_w_o, v_ffn_w_gu, v_ffn_w_down):
    seq, d = x.shape[1], x.shape[2]
    assert seq % CHUNK == 0
    n_real = N_META + seq
    t_all = -(-n_real // LANES) * LANES
    tm = _pick(t_all, (384, 256, 128))
    tq = _pick(seq, (512, 256, 128, 64))
    depth = norm_mix.shape[0]
    n_mla, n_lru = mla_w_in.shape[0], lru_w_in.shape[0]
    lq, lkv = mla_q_norm.shape[1], mla_kv_norm.shape[1]
    w_in_cols = lq + lkv + LANES
    heads_local = mla_w_uq.shape[2] // (QK_NOPE + QK_ROPE)
    n_heads = heads_local * N_DEV
    dr = lru_w_gate_a.shape[1] * lru_w_gate_a.shape[2]
    scale = (QK_NOPE + QK_ROPE) ** -0.5
    cx, cy, cc = _mesh_pos()
    core = jnp.reshape(cc, (1,)).astype(jnp.int32)
    my_slot = jnp.reshape(4 * cx + 2 * cy + cc, (1,)).astype(jnp.int32)

    def pad_cols(w, cols):
        return jnp.pad(w, ((0, 0), (0, cols - w.shape[1])))

    def pad_heads(w):
        k_all = w.shape[0]
        w3 = w.reshape(k_all, heads_local, QK_NOPE + QK_ROPE)
        return jnp.pad(w3, ((0, 0), (0, 0), (0, HEAD_W - QK_NOPE - QK_ROPE))).reshape(k_all, heads_local * HEAD_W)

    def unpad_heads(w):
        k_all = w.shape[0]
        return w.reshape(k_all, heads_local, HEAD_W)[:, :, :QK_NOPE + QK_ROPE].reshape(k_all, -1)

    small_rows = N_META + n_lru * 4 + 2 * n_lru
    small_pad = -(-small_rows // SUBLANES) * SUBLANES

    def pack_small(meta, conv_w, conv_b, lam):
        rows = jnp.concatenate([meta, conv_w.reshape(n_lru * 4, -1), conv_b, lam], axis=0)
        return jnp.pad(rows, ((0, small_pad - small_rows), (0, 0)))

    def unpack_small(p):
        o1 = N_META + n_lru * 4
        return (p[:N_META], p[N_META:o1].reshape(n_lru, 4, -1), p[o1:o1 + n_lru], p[o1 + n_lru:o1 + 2 * n_lru])

    (small_full,), small_done = _all_gather([pack_small(meta_tokens, lru_conv_w, lru_conv_b, lru_lambda)], name="ag_small")
    small_full = jnp.transpose(small_full, (1, 0, 2)).reshape(small_pad, -1)
    meta_full, conv_w_full, conv_b_full, lam_full = unpack_small(small_full)

    def wire(w):
        return (w + small_done).astype(BF16)

    mla_shards, lru_shards, ffn_shards = [], [], []
    for j in range(n_mla):
        mla_shards.append([wire(pad_cols(mla_w_in[j], w_in_cols)), wire(pad_heads(mla_w_uq[j])), wire(mla_w_ukv[j]),
                           wire(mla_w_o[j])])
    for j in range(n_lru):
        lru_shards.append([wire(lru_w_in[j]), wire(lru_w_o[j])])
    for layer in range(depth):
        ffn_shards.append([wire(ffn_w_gu[layer]), wire(ffn_w_down[layer])])

    n_sub = 2 * depth
    groups = []
    for layer in range(depth):
        groups += [mla_shards[layer // 2] if layer % 2 == 0 else lru_shards[layer // 2], ffn_shards[layer]]
    slot_idx = 4 * cx + 2 * cy + cc
    ag_own = []
    for gi, shards in enumerate(groups):
        lands = [lax.dynamic_update_slice(lax.empty((N_DEV,) + s.shape, s.dtype), s[None], (slot_idx, 0, 0)) for s in shards]
        ag_own.append(_exchange_start(shards, lands, _ag_plan_own, 4, name=f"ag{gi}_start"))
    ag_pass = [None] * n_sub
    weights = [None] * n_sub

    def ag_landed(gi, after):
        _, lands = _exchange_wait(ag_own[gi], _ag_plan_own, after, name=f"ag{gi}_wait")
        ag_pass[gi] = _exchange_start([], lands, _ag_plan_pass, 3, name=f"ag{gi}_pass")
        return ag_pass[gi][4][0, 0]

    def ag_done(gi, after):
        _, weights[gi] = _exchange_wait(ag_pass[gi], _ag_plan_pass, after, name=f"ag{gi}_pass_wait")

    cos, sin = _rope_tables(t_all)
    zeros_tail = jnp.zeros((t_all - n_real, d), F32)
    started = ag_own[0][4][0, 0]
    for st in ag_own[1:]:
        started = started + st[4][0, 0]
    h = jnp.concatenate([meta_full + started, x[0], zeros_tail], axis=0)
    target = jnp.concatenate([jnp.zeros((N_META, d), F32), loss_target[0], zeros_tail], axis=0)

    attn_kw = dict(tm=tm, tq=tq, n_heads=n_heads, scale=scale, n_real=n_real)

    def lru_small(j):
        return (conv_w_full[j], conv_b_full[j][None, :], lru_w_gate_a[j].astype(BF16), lru_b_gate_a[j].reshape(1, dr),
                lru_w_gate_x[j].astype(BF16), lru_b_gate_x[j].reshape(1, dr), lam_full[j][None, :])

    def before_sublayer(k, act):
        tok = ag_landed(k, act) if k <= 1 else 0.0
        ag_done(k, act)
        if 1 <= k < n_sub - 1:
            tok = tok + ag_landed(k + 1, act)
        return tok

    saved = []
    for layer in range(depth):
        j = layer // 2
        g_mix = norm_mix[layer][None, :] + before_sublayer(2 * layer, h)
        if layer % 2 == 0:
            h_mid, mix_saved = _mla_layer_fwd(layer, h, g_mix, weights[2 * layer], mla_q_norm[j][None, :],
                                              mla_kv_norm[j][None, :], cos, sin, **attn_kw)
        else:
            h_mid, mix_saved = _lru_layer_fwd(layer, h, g_mix, weights[2 * layer], lru_small(j), tm=tm)
        g_ffn = norm_ffn[layer][None, :] + before_sublayer(2 * layer + 1, h_mid)
        h_out, ffn_saved = _ffn_layer_fwd(layer, h_mid, g_ffn, weights[2 * layer + 1], tm=tm)
        saved.append((h, h_mid, mix_saved, ffn_saved))
        h = h_out

    loss_part, dh, dh_b, dg_final = _loss_head(h, target, norm_final[None, :], name="loss_head", tm=tm, n_real=n_real)
    loss = lax.psum(loss_part[0, 0], ("x", "y", "c"))

    rs_sib, rs_chip, reduced = [None] * (n_sub + 1), [None] * (n_sub + 1), [None] * (n_sub + 1)
    chip_idx = jnp.reshape(2 * cx + cy, (1,)).astype(jnp.int32)

    def rs_begin(k, grads):
        lands = [lax.empty((4,) + g.shape[1:], g.dtype) for g in grads]
        rs_sib[k] = _exchange_start(grads, lands, _rs_plan_sibling, 4, name=f"rs{k}_start")
        return rs_sib[k][4][0, 0]

    def rs_middle(k, after):
        grads, landed = _exchange_wait(rs_sib[k], _rs_plan_sibling, after, name=f"rs{k}_wait")
        parts = [_pair_add(g, l, core, name=f"rs{k}_add{a}", tr=_adam_row_tile(g.shape[1], g.shape[2], 4 * 1024 * 1024))
                 for a, (g, l) in enumerate(zip(grads, landed))]
        lands = [lax.empty((3,) + p.shape[1:], p.dtype) for p in parts]
        rs_chip[k] = _exchange_start(parts, lands, _rs_plan_chips, 3, name=f"rs{k}_chips")
        return rs_chip[k][4][0, 0]

    def rs_end(k, after):
        reduced[k] = _exchange_wait(rs_chip[k], _rs_plan_chips, after, name=f"rs{k}_chips_wait")

    d_norm_mix, d_norm_ffn = [None] * depth, [None] * depth
    d_qn, d_kvn = [None] * n_mla, [None] * n_mla
    d_small = {k: [None] * n_lru for k in ("cw", "cb", "wga", "bga", "wgx", "bgx", "lam")}
    tok, waiting = 0.0, None
    for layer in reversed(range(depth)):
        j = layer // 2
        h_in, h_mid, mix_saved, ffn_saved = saved[layer]
        dh, dh_b, d_norm_ffn[layer], ffn_g = _ffn_layer_bwd(layer, dh, dh_b, h_mid, ffn_saved, norm_ffn[layer][None, :] + tok,
                                                            weights[2 * layer + 1], tm=tm)
        tok = rs_begin(2 * layer + 1, ffn_g)
        if waiting is not None:
            tok = tok + rs_middle(waiting, dh)
        waiting = 2 * layer + 1
        if layer == 0:
            tok = tok + rs_middle(waiting, dh)
            waiting = None
        g_mix = norm_mix[layer][None, :] + tok
        if layer % 2 == 0:
            early = (lambda g: rs_begin(n_sub, g) + rs_middle(n_sub, g[0])) if layer == 0 else None
            dh, dh_b, d_norm_mix[layer], d_qn[j], d_kvn[j], mix_g = _mla_layer_bwd(
                layer, dh, dh_b, h_in, mix_saved, g_mix, weights[2 * layer], mla_q_norm[j][None, :], mla_kv_norm[j][None, :],
                cos, sin, early=early, **attn_kw)
        else:
            dh, dh_b, d_norm_mix[layer], dsmall, mix_g = _lru_layer_bwd(layer, dh, dh_b, h_in, mix_saved, g_mix,
                                                                        weights[2 * layer], lru_small(j), tm=tm)
            for key, val in zip(("cw", "cb", "wga", "bga", "wgx", "bgx", "lam"), dsmall):
                d_small[key][j] = val
        tok = rs_begin(2 * layer, mix_g)
        if waiting is not None:
            tok = tok + rs_middle(waiting, dh)
        waiting = 2 * layer
    rs_middle(waiting, dh)

    grad_x = dh[N_META:n_real][None]

    d_meta = dh[:N_META]
    small_grad = pack_small(d_meta, jnp.stack(d_small["cw"], axis=0), jnp.concatenate(d_small["cb"], axis=0),
                            jnp.concatenate(d_small["lam"], axis=0))
    rep_grads = [
        jnp.concatenate(d_norm_mix, axis=0), jnp.concatenate(d_norm_ffn, axis=0), dg_final,
        jnp.concatenate(d_qn, axis=0), jnp.concatenate(d_kvn, axis=0),
        jnp.stack(d_small["wga"], axis=0).reshape(-1, LANES), jnp.concatenate(d_small["bga"], axis=0),
        jnp.stack(d_small["wgx"], axis=0).reshape(-1, LANES), jnp.concatenate(d_small["bgx"], axis=0),
    ]
    small_srcs = [small_grad] + [jnp.pad(g, ((0, -g.shape[0] % SUBLANES), (0, 0))) for g in rep_grads]
    small_lands = [lax.dynamic_update_slice(lax.empty((N_DEV,) + s.shape, s.dtype), s[None], (slot_idx, 0, 0))
                   for s in small_srcs]
    small_own = _exchange_start(small_srcs, small_lands, _ag_plan_own, 4, name="ag_grads_start")

    res = {}

    def adam_sharded(nm, k, a, idx, n_layers, w, m, v):
        parts, landed = reduced[k]
        r_all, c_all = landed[a].shape[1], landed[a].shape[2]
        res[nm] = _adamw(w.reshape(r_all, c_all), m.reshape(r_all, c_all), v.reshape(r_all, c_all), landed[a], (0, 1, 2),
                         name=f"adamw_{nm}{idx}", tr=_adam_row_tile(r_all, c_all, 2 * 1024 * 1024), own=(parts[a], chip_idx),
                         stack=(idx, n_layers, res.get(nm)))

    after = small_own[4]
    for k in reversed(range(n_sub)):
        rs_end(k, after)
        if k == 0:
            rs_end(n_sub, after)
            reduced[0] = tuple(first + rest for first, rest in zip(reduced[0], reduced[n_sub]))
        layer, j = k // 2, k // 4
        if k % 2 == 1:
            adam_sharded("ffn_w_gu", k, 0, layer, depth, ffn_w_gu[layer], m_ffn_w_gu[layer], v_ffn_w_gu[layer])
            adam_sharded("ffn_w_down", k, 1, layer, depth, ffn_w_down[layer], m_ffn_w_down[layer], v_ffn_w_down[layer])
            after = res["ffn_w_down"][0]
        elif layer % 2 == 0:
            adam_sharded("mla_w_in", k, 0, j, n_mla, pad_cols(mla_w_in[j], w_in_cols), pad_cols(m_mla_w_in[j], w_in_cols),
                         pad_cols(v_mla_w_in[j], w_in_cols))
            adam_sharded("mla_w_uq", k, 1, j, n_mla, pad_heads(mla_w_uq[j]), pad_heads(m_mla_w_uq[j]), pad_heads(v_mla_w_uq[j]))
            adam_sharded("mla_w_ukv", k, 2, j, n_mla, mla_w_ukv[j], m_mla_w_ukv[j], v_mla_w_ukv[j])
            adam_sharded("mla_w_o", k, 3, j, n_mla, mla_w_o[j], m_mla_w_o[j], v_mla_w_o[j])
            after = res["mla_w_o"][0]
        else:
            adam_sharded("lru_w_in", k, 0, j, n_lru, lru_w_in[j], m_lru_w_in[j], v_lru_w_in[j])
            adam_sharded("lru_w_o", k, 1, j, n_lru, lru_w_o[j], m_lru_w_o[j], v_lru_w_o[j])
            after = res["lru_w_o"][0]
    res["mla_w_in"] = [t[:, :, :lq + lkv + QK_ROPE] for t in res["mla_w_in"]]
    res["mla_w_uq"] = [t.reshape(n_mla, lq, heads_local, HEAD_W)[:, :, :, :QK_NOPE + QK_ROPE].reshape(n_mla, lq, -1)
                       for t in res["mla_w_uq"]]

    _, small_lands = _exchange_wait(small_own, _ag_plan_own, after, name="ag_grads_wait")
    small_pass = _exchange_start([], small_lands, _ag_plan_pass, 3, name="ag_grads_pass")
    _, all_small = _exchange_wait(small_pass, _ag_plan_pass, after, name="ag_grads_pass_wait")
    slot_order = tuple(range(N_DEV))

    def adam_rep(terms, w, m, v, tag):
        r_pad, c_all = terms.shape[1], terms.shape[2]

        def prep(t):
            t2 = t.reshape(-1, c_all)
            return jnp.pad(t2, ((0, r_pad - t2.shape[0]), (0, 0)))

        outs = _adamw(prep(w), prep(m), prep(v), terms, slot_order, name=f"adamw_{tag}", tr=_adam_row_tile(r_pad, c_all))
        n_rows = w.size // c_all
        return [o[:n_rows].reshape(w.shape) for o in outs]

    small_w = pack_small(meta_tokens, lru_conv_w, lru_conv_b, lru_lambda)
    small_m = pack_small(m_meta_tokens, m_lru_conv_w, m_lru_conv_b, m_lru_lambda)
    small_v = pack_small(v_meta_tokens, v_lru_conv_w, v_lru_conv_b, v_lru_lambda)
    small_out = _adamw(small_w, small_m, small_v, all_small[0], slot_order, name="adamw_small", tr=small_pad, col_block=my_slot)
    small_out = [unpack_small(o) for o in small_out]
    for idx, key in enumerate(("meta_tokens", "lru_conv_w", "lru_conv_b", "lru_lambda")):
        res[key] = [small_out[k][idx] for k in range(4)]

    res["norm_mix"] = adam_rep(all_small[1], norm_mix, m_norm_mix, v_norm_mix, "norm_mix")
    res["norm_ffn"] = adam_rep(all_small[2], norm_ffn, m_norm_ffn, v_norm_ffn, "norm_ffn")
    res["norm_final"] = adam_rep(all_small[3], norm_final, m_norm_final, v_norm_final, "norm_final")
    res["mla_q_norm"] = adam_rep(all_small[4], mla_q_norm, m_mla_q_norm, v_mla_q_norm, "mla_q_norm")
    res["mla_kv_norm"] = adam_rep(all_small[5], mla_kv_norm, m_mla_kv_norm, v_mla_kv_norm, "mla_kv_norm")
    res["lru_w_gate_a"] = adam_rep(all_small[6], lru_w_gate_a, m_lru_w_gate_a, v_lru_w_gate_a, "lru_w_gate_a")
    res["lru_b_gate_a"] = adam_rep(all_small[7], lru_b_gate_a, m_lru_b_gate_a, v_lru_b_gate_a, "lru_b_gate_a")
    res["lru_w_gate_x"] = adam_rep(all_small[8], lru_w_gate_x, m_lru_w_gate_x, v_lru_w_gate_x, "lru_w_gate_x")
    res["lru_b_gate_x"] = adam_rep(all_small[9], lru_b_gate_x, m_lru_b_gate_x, v_lru_b_gate_x, "lru_b_gate_x")

    names = ["meta_tokens", "norm_mix", "norm_ffn", "norm_final", "mla_w_in", "mla_q_norm", "mla_kv_norm", "mla_w_uq",
             "mla_w_ukv", "mla_w_o", "lru_w_in", "lru_conv_w", "lru_conv_b", "lru_w_gate_a", "lru_b_gate_a", "lru_w_gate_x",
             "lru_b_gate_x", "lru_lambda", "lru_w_o", "ffn_w_gu", "ffn_w_down"]
    shapes = dict(meta_tokens=meta_tokens, norm_mix=norm_mix, norm_ffn=norm_ffn, norm_final=norm_final, mla_w_in=mla_w_in,
                  mla_q_norm=mla_q_norm, mla_kv_norm=mla_kv_norm, mla_w_uq=mla_w_uq, mla_w_ukv=mla_w_ukv, mla_w_o=mla_w_o,
                  lru_w_in=lru_w_in, lru_conv_w=lru_conv_w, lru_conv_b=lru_conv_b, lru_w_gate_a=lru_w_gate_a,
                  lru_b_gate_a=lru_b_gate_a, lru_w_gate_x=lru_w_gate_x, lru_b_gate_x=lru_b_gate_x, lru_lambda=lru_lambda,
                  lru_w_o=lru_w_o, ffn_w_gu=ffn_w_gu, ffn_w_down=ffn_w_down)
    outs = [loss, grad_x]
    for k in range(4):
        outs += [res[nm][k].reshape(shapes[nm].shape) for nm in names]
    return tuple(outs)
```

```python
import math

import jax
import jax.numpy as jnp
from jax import lax
from jax.experimental import pallas as pl
from jax.experimental.pallas import tpu as pltpu

F32 = jnp.float32
BF16 = jnp.bfloat16
MESH = pl.DeviceIdType.MESH

N_META = 16
CHUNK = 64
QK_NOPE = 128
QK_ROPE = 64
V_HEAD = 128
HEAD_W = 256
ROPE_THETA = 10000.0
LRU_C = 8.0
RMS_EPS = 1e-6
NEG_BIG = -1e30
ADAM_LR, ADAM_B1, ADAM_B2, ADAM_EPS, ADAM_WD, ADAM_STEP = 0.001, 0.9, 0.999, 1e-08, 0.01, 10

LANES = 128
SUBLANES = 8
VMEM_LIMIT_BYTES = 52 * 1024 * 1024
N_DEV = 8

_NT = (((1,), (1,)), ((), ()))
_TN = (((0,), (0,)), ((), ()))
_DIVS = (2048, 1024, 512, 256, 128)
_ROW_TILES = (1408, 1024, 512, 256, 128)


def _params(dims):
    return pltpu.CompilerParams(dimension_semantics=dims, vmem_limit_bytes=VMEM_LIMIT_BYTES)


def _pick(n, candidates):
    for c in candidates:
        if c <= n and n % c == 0:
            return c
    return n


def _sigmoid(z):
    return 0.5 + 0.5 * jnp.tanh(0.5 * z)


def _gelu(x):
    c = math.sqrt(2.0 / math.pi)
    return 0.5 * x * (1.0 + jnp.tanh(c * (x + 0.044715 * x * x * x)))


def _gelu_grad(x):
    c = math.sqrt(2.0 / math.pi)
    th = jnp.tanh(c * (x + 0.044715 * x * x * x))
    return 0.5 * (1.0 + th) + 0.5 * x * (1.0 - th * th) * c * (1.0 + 3.0 * 0.044715 * x * x)


def _neg_expm1(x):
    poly = -x * (1.0 + x * (1.0 / 2.0) * (1.0 + x * (1.0 / 3.0) * (1.0 + x * (1.0 / 4.0) * (
        1.0 + x * (1.0 / 5.0) * (1.0 + x * (1.0 / 6.0) * (1.0 + x * (1.0 / 7.0)))))))
    return jnp.where(x > -0.25, poly, 1.0 - jnp.exp(x))


def _softplus_neg(lam):
    e = jnp.exp(-jnp.abs(lam))
    log1p = jnp.where(e > 1e-4, jnp.log(1.0 + e), e * (1.0 - e * (0.5 - e * (1.0 / 3.0))))
    return jnp.maximum(-lam, 0.0) + log1p


def _rot_half(x):
    lane = lax.broadcasted_iota(jnp.int32, x.shape, 1)
    first = (lane % QK_ROPE) < (QK_ROPE // 2)
    return jnp.where(first, -pltpu.roll(x, LANES - QK_ROPE // 2, 1), pltpu.roll(x, QK_ROPE // 2, 1))


def _rope(x, cos, sin):
    return x * cos + _rot_half(x) * sin


def _unrope(g, cos, sin):
    return g * cos - _rot_half(g) * sin


def _grid_order(rows_outer):
    if not rows_outer:
        return lambda f: f
    return lambda f: (lambda i, j, k: f(j, i, k))


def _mm_nn(a, b, *, name, out_dtype, tm, tn, tk, b_blocked=False, res=None, epilogue=None, extras=(), rows_outer=False):
    m_all, k_all = a.shape
    om = _grid_order(rows_outer)
    if b_blocked:
        g_all, kb, nb = b.shape
        n_all = g_all * nb
        assert nb % tn == 0
        r = nb // tn
        b_spec = pl.BlockSpec((None, tk, tn), om(lambda j, i, k: (j // r, k, j % r)))
    else:
        kb, n_all = b.shape
        b_spec = pl.BlockSpec((tk, tn), om(lambda j, i, k: (k, j)))
    assert kb == k_all and m_all % tm == 0 and n_all % tn == 0 and k_all % tk == 0
    nm, nn, nk = m_all // tm, n_all // tn, k_all // tk
    in_specs = [pl.BlockSpec((tm, tk), om(lambda j, i, k: (i, k))), b_spec]
    operands = [a, b]
    has_res = res is not None
    if has_res:
        in_specs.append(pl.BlockSpec((tm, tn), om(lambda j, i, k: (i, j))))
        operands.append(res)
    for e in extras:
        in_specs.append(pl.BlockSpec((tm, e.shape[1]), om(lambda j, i, k: (i, 0))))
        operands.append(e)
    n_ex = len(extras)

    def body(*refs):
        a_ref, b_ref = refs[0], refs[1]
        pos = 2
        res_ref = None
        if has_res:
            res_ref = refs[pos]
            pos += 1
        ex_refs = refs[pos:pos + n_ex]
        pos += n_ex
        o_ref = refs[pos]
        acc_ref = refs[pos + 1] if nk > 1 else None

        def finish(acc):
            if has_res:
                acc = acc + res_ref[...]
            if epilogue is not None:
                acc = epilogue(acc, *ex_refs)
            o_ref[...] = acc.astype(o_ref.dtype)

        prod = jnp.dot(a_ref[...], b_ref[...], preferred_element_type=F32)
        if nk == 1:
            finish(prod)
        else:
            k = pl.program_id(2)

            @pl.when(k == 0)
            def _():
                acc_ref[...] = prod

            @pl.when(k > 0)
            def _():
                acc_ref[...] += prod

            @pl.when(k == nk - 1)
            def _():
                finish(acc_ref[...])

    return pl.pallas_call(
        body, name=name, grid=(nm, nn, nk) if rows_outer else (nn, nm, nk), in_specs=in_specs,
        out_specs=pl.BlockSpec((tm, tn), om(lambda j, i, k: (i, j))),
        out_shape=jax.ShapeDtypeStruct((m_all, n_all), out_dtype),
        scratch_shapes=[pltpu.VMEM((tm, tn), F32)] if nk > 1 else [],
        compiler_params=_params(("parallel", "parallel", "arbitrary")),
    )(*operands)


def _mm_nt(a, b, *, name, out_dtype, tm, tn, tk, b_blocked=False, after=None):
    if a.ndim == 3:
        n_planes, m_all, kp = a.shape
        k_all = n_planes * kp
    else:
        n_planes, (m_all, k_all) = 0, a.shape
    if b_blocked and tk == k_all and b.shape[0] > 1:
        g_all, n_all, nb = b.shape
        assert g_all * nb == k_all and m_all % tm == 0 and n_all % tn == 0
        per_plane = kp // nb if n_planes else 0

        def whole_body(a_ref, b_ref, o_ref):
            acc = None
            for g in range(g_all):
                a_g = a_ref[g // per_plane, :, (g % per_plane) * nb:(g % per_plane + 1) * nb] if n_planes else a_ref[:, g * nb:(g + 1) * nb]
                prod = lax.dot_general(a_g, b_ref[g], _NT, preferred_element_type=F32)
                acc = prod if acc is None else acc + prod
            o_ref[...] = acc.astype(o_ref.dtype)

        a_whole = (pl.BlockSpec((n_planes, tm, kp), lambda j, i: (0, i, 0)) if n_planes
                   else pl.BlockSpec((tm, k_all), lambda j, i: (i, 0)))
        return pl.pallas_call(
            whole_body, name=name, grid=(n_all // tn, m_all // tm),
            in_specs=[a_whole, pl.BlockSpec((g_all, tn, nb), lambda j, i: (0, j, 0))],
            out_specs=pl.BlockSpec((tm, tn), lambda j, i: (i, j)),
            out_shape=jax.ShapeDtypeStruct((m_all, n_all), out_dtype),
            compiler_params=_params(("parallel", "parallel")),
        )(a, b)
    if n_planes:
        assert kp % tk == 0
        rp = kp // tk
        a_spec = pl.BlockSpec((None, tm, tk), lambda j, i, k: (k // rp, i, k % rp))
    else:
        a_spec = pl.BlockSpec((tm, tk), lambda j, i, k: (i, k))
    if b_blocked:
        g_all, n_all, nb = b.shape
        assert g_all * nb == k_all and nb % tk == 0
        r = nb // tk
        b_spec = pl.BlockSpec((None, tn, tk), lambda j, i, k: (k // r, j, k % r))
    else:
        n_all, kb = b.shape
        assert kb == k_all
        b_spec = pl.BlockSpec((tn, tk), lambda j, i, k: (j, k))
    assert m_all % tm == 0 and n_all % tn == 0 and k_all % tk == 0
    nm, nn, nk = m_all // tm, n_all // tn, k_all // tk

    def body(a_ref, b_ref, o_ref, *scratch):
        prod = lax.dot_general(a_ref[...], b_ref[...], _NT, preferred_element_type=F32)
        if nk == 1:
            o_ref[...] = prod.astype(o_ref.dtype)
        else:
            acc_ref = scratch[0]
            k = pl.program_id(2)

            @pl.when(k == 0)
            def _():
                acc_ref[...] = prod

            @pl.when(k > 0)
            def _():
                acc_ref[...] += prod

            @pl.when(k == nk - 1)
            def _():
                o_ref[...] = acc_ref[...].astype(o_ref.dtype)

    ordered = [] if after is None else [after]

    def ordered_body(a_ref, b_ref, *rest):
        body(a_ref, b_ref, *rest[len(ordered):])

    return pl.pallas_call(
        ordered_body, name=name, grid=(nn, nm, nk),
        in_specs=[a_spec, b_spec] + [pl.BlockSpec(memory_space=pl.ANY)] * len(ordered),
        out_specs=pl.BlockSpec((tm, tn), lambda j, i, k: (i, j)),
        out_shape=jax.ShapeDtypeStruct((m_all, n_all), out_dtype),
        scratch_shapes=[pltpu.VMEM((tm, tn), F32)] if nk > 1 else [],
        compiler_params=_params(("parallel", "parallel", "arbitrary")),
    )(a, b, *ordered)


def _mm_tn(a, b, *, name, out_dtype, tm, tn, tk, out_block=None, cols_outer=False):
    t_all, m_all = a.shape
    om = _grid_order(cols_outer)
    if b.ndim == 3:
        n_planes, tb, n_p = b.shape
        assert n_p % tn == 0
        rq = n_p // tn
        n_all = n_planes * n_p
        b_spec = pl.BlockSpec((None, tk, tn), om(lambda i, j, k: (j // rq, k, j % rq)))
    else:
        tb, n_all = b.shape
        b_spec = pl.BlockSpec((tk, tn), om(lambda i, j, k: (k, j)))
    assert tb == t_all and m_all % tm == 0 and n_all % tn == 0 and t_all % tk == 0
    nm, nn, nk = m_all // tm, n_all // tn, t_all // tk
    if out_block is None:
        out_shape = jax.ShapeDtypeStruct((m_all, n_all), out_dtype)
        out_spec = pl.BlockSpec((tm, tn), om(lambda i, j, k: (i, j)))
    else:
        assert out_block % tn == 0 and n_all % out_block == 0
        r = out_block // tn
        out_shape = jax.ShapeDtypeStruct((n_all // out_block, m_all, out_block), out_dtype)
        out_spec = pl.BlockSpec((None, tm, tn), om(lambda i, j, k: (j // r, i, j % r)))

    def body(a_ref, b_ref, o_ref, *scratch):
        prod = lax.dot_general(a_ref[...], b_ref[...], _TN, preferred_element_type=F32)
        if nk == 1:
            o_ref[...] = prod.astype(o_ref.dtype)
        else:
            acc_ref = scratch[0]
            k = pl.program_id(2)

            @pl.when(k == 0)
            def _():
                acc_ref[...] = prod

            @pl.when(k > 0)
            def _():
                acc_ref[...] += prod

            @pl.when(k == nk - 1)
            def _():
                o_ref[...] = acc_ref[...].astype(o_ref.dtype)

    return pl.pallas_call(
        body, name=name, grid=(nn, nm, nk) if cols_outer else (nm, nn, nk),
        in_specs=[pl.BlockSpec((tk, tm), om(lambda i, j, k: (k, i))), b_spec],
        out_specs=out_spec, out_shape=out_shape,
        scratch_shapes=[pltpu.VMEM((tm, tn), F32)] if nk > 1 else [],
        compiler_params=_params(("parallel", "parallel", "arbitrary")),
    )(a, b)


def _rmsnorm_fwd(x, g, *, name, tm):
    t_all, d = x.shape

    def body(x_ref, g_ref, o_ref):
        xv = x_ref[...]
        rstd = lax.rsqrt(jnp.mean(xv * xv, axis=-1, keepdims=True) + RMS_EPS)
        o_ref[...] = (xv * rstd * g_ref[...]).astype(o_ref.dtype)

    return pl.pallas_call(
        body, name=name, grid=(t_all // tm,),
        in_specs=[pl.BlockSpec((tm, d), lambda i: (i, 0)), pl.BlockSpec((1, d), lambda i: (0, 0))],
        out_specs=pl.BlockSpec((tm, d), lambda i: (i, 0)),
        out_shape=jax.ShapeDtypeStruct((t_all, d), BF16),
        compiler_params=_params(("parallel",)),
    )(x, g)


def _rms_bwd_math(dy, xv, g):
    rstd = lax.rsqrt(jnp.mean(xv * xv, axis=-1, keepdims=True) + RMS_EPS)
    xhat = xv * rstd
    dxh = dy * g
    dx = rstd * (dxh - xhat * jnp.mean(dxh * xhat, axis=-1, keepdims=True))
    return dx, jnp.sum(dy * xhat, axis=0, keepdims=True)


def _rmsnorm_bwd(dy, x, g, res, *, name, tm):
    t_all, d = x.shape

    def body(dy_ref, x_ref, g_ref, res_ref, dx_ref, dxb_ref, dg_ref):
        dx, dg = _rms_bwd_math(dy_ref[...].astype(F32), x_ref[...], g_ref[...])
        tot = res_ref[...] + dx
        dx_ref[...] = tot
        dxb_ref[...] = tot.astype(BF16)

        @pl.when(pl.program_id(0) == 0)
        def _():
            dg_ref[...] = dg

        @pl.when(pl.program_id(0) > 0)
        def _():
            dg_ref[...] += dg

    row = pl.BlockSpec((tm, d), lambda i: (i, 0))
    vec = pl.BlockSpec((1, d), lambda i: (0, 0))
    return pl.pallas_call(
        body, name=name, grid=(t_all // tm,),
        in_specs=[row, row, vec, row], out_specs=[row, row, vec],
        out_shape=[jax.ShapeDtypeStruct((t_all, d), F32), jax.ShapeDtypeStruct((t_all, d), BF16),
                   jax.ShapeDtypeStruct((1, d), F32)],
        compiler_params=_params(("arbitrary",)),
    )(dy, x, g, res)


def _loss_head(h, target, g, *, name, tm, n_real):
    t_all, d = h.shape

    def body(h_ref, t_ref, g_ref, loss_ref, dx_ref, dxb_ref, dg_ref):
        i = pl.program_id(0)
        xv = h_ref[...]
        gv = g_ref[...]
        rstd = lax.rsqrt(jnp.mean(xv * xv, axis=-1, keepdims=True) + RMS_EPS)
        y = xv * rstd * gv
        row = i * tm + lax.broadcasted_iota(jnp.int32, (tm, 1), 0)
        valid = (row >= N_META) & (row < n_real)
        err = jnp.where(valid, y - t_ref[...], 0.0)
        part = 0.5 * jnp.sum(jnp.mean(err * err, axis=-1, keepdims=True), axis=0, keepdims=True)
        dx, dg = _rms_bwd_math(err * (1.0 / d), xv, gv)
        dx_ref[...] = dx
        dxb_ref[...] = dx.astype(BF16)

        @pl.when(i == 0)
        def _():
            dg_ref[...] = dg
            loss_ref[...] = jnp.broadcast_to(part, loss_ref.shape)

        @pl.when(i > 0)
        def _():
            dg_ref[...] += dg
            loss_ref[...] += jnp.broadcast_to(part, loss_ref.shape)

    row = pl.BlockSpec((tm, d), lambda i: (i, 0))
    vec = pl.BlockSpec((1, d), lambda i: (0, 0))
    return pl.pallas_call(
        body, name=name, grid=(t_all // tm,),
        in_specs=[row, row, vec],
        out_specs=[pl.BlockSpec((1, LANES), lambda i: (0, 0)), row, row, vec],
        out_shape=[jax.ShapeDtypeStruct((1, LANES), F32), jax.ShapeDtypeStruct((t_all, d), F32),
                   jax.ShapeDtypeStruct((t_all, d), BF16), jax.ShapeDtypeStruct((1, d), F32)],
        compiler_params=_params(("arbitrary",)),
    )(h, target, g)


def _ffn_up(x, w_gu, *, name, tm):
    t_all, d = x.shape
    g_all, kb, nb = w_gu.shape
    half = g_all // 2
    f = half * nb
    assert kb == d and t_all % tm == 0

    def body(x_ref, wg_ref, wu_ref, gu_ref, act_ref):
        xv = x_ref[...]
        gv = jnp.dot(xv, wg_ref[...], preferred_element_type=F32)
        uv = jnp.dot(xv, wu_ref[...], preferred_element_type=F32)
        gu_ref[0] = gv.astype(gu_ref.dtype)
        gu_ref[1] = uv.astype(gu_ref.dtype)
        act_ref[...] = (gv * _sigmoid(gv) * uv).astype(act_ref.dtype)

    return pl.pallas_call(
        body, name=name, grid=(half, t_all // tm),
        in_specs=[pl.BlockSpec((tm, d), lambda j, i: (i, 0)), pl.BlockSpec((None, d, nb), lambda j, i: (j, 0, 0)),
                  pl.BlockSpec((None, d, nb), lambda j, i: (j + half, 0, 0))],
        out_specs=[pl.BlockSpec((2, tm, nb), lambda j, i: (0, i, j)), pl.BlockSpec((tm, nb), lambda j, i: (i, j))],
        out_shape=[jax.ShapeDtypeStruct((2, t_all, f), BF16), jax.ShapeDtypeStruct((t_all, f), BF16)],
        compiler_params=_params(("parallel", "parallel")),
    )(x, w_gu, w_gu)


def _ffn_dact(dy, w_down, gu, *, name, tm, tn, after=None):
    t_all, d = dy.shape
    f = w_down.shape[0]
    assert t_all % tm == 0 and f % tn == 0
    ordered = [] if after is None else [after]

    def body(dy_ref, w_ref, gu_ref, *rest):
        o_ref = rest[-1]
        dact = lax.dot_general(dy_ref[...], w_ref[...], _NT, preferred_element_type=F32)
        gv, uv = gu_ref[0].astype(F32), gu_ref[1].astype(F32)
        sg = _sigmoid(gv)
        o_ref[0] = (dact * uv * (sg * (1.0 + gv * (1.0 - sg)))).astype(o_ref.dtype)
        o_ref[1] = (dact * gv * sg).astype(o_ref.dtype)

    return pl.pallas_call(
        body, name=name, grid=(f // tn, t_all // tm),
        in_specs=[pl.BlockSpec((tm, d), lambda j, i: (i, 0)), pl.BlockSpec((tn, d), lambda j, i: (j, 0)),
                  pl.BlockSpec((2, tm, tn), lambda j, i: (0, i, j))] + [pl.BlockSpec(memory_space=pl.ANY)] * len(ordered),
        out_specs=pl.BlockSpec((2, tm, tn), lambda j, i: (0, i, j)),
        out_shape=jax.ShapeDtypeStruct((2, t_all, f), BF16),
        compiler_params=_params(("parallel", "parallel")),
    )(dy, w_down, gu, *ordered)


def _mla_prep_fwd(proj, qn, kvn, cos, sin, *, name, tm, lq, lkv):
    t_all, w = proj.shape

    def body(p_ref, qn_ref, kvn_ref, cos_ref, sin_ref, cq_ref, ckv_ref, kr_ref):
        pv = p_ref[...]
        xq = pv[:, :lq]
        xkv = pv[:, lq:lq + lkv]
        cq_ref[...] = (xq * lax.rsqrt(jnp.mean(xq * xq, axis=-1, keepdims=True) + RMS_EPS) * qn_ref[...]).astype(BF16)
        ckv_ref[...] = (xkv * lax.rsqrt(jnp.mean(xkv * xkv, axis=-1, keepdims=True) + RMS_EPS) * kvn_ref[...]).astype(BF16)
        kr_ref[...] = _rope(pv[:, lq + lkv:], cos_ref[...], sin_ref[...]).astype(BF16)

    def row(width):
        return pl.BlockSpec((tm, width), lambda i: (i, 0))

    def vec(width):
        return pl.BlockSpec((1, width), lambda i: (0, 0))

    return pl.pallas_call(
        body, name=name, grid=(t_all // tm,),
        in_specs=[row(w), vec(lq), vec(lkv), row(LANES), row(LANES)],
        out_specs=[row(lq), row(lkv), row(LANES)],
        out_shape=[jax.ShapeDtypeStruct((t_all, lq), BF16), jax.ShapeDtypeStruct((t_all, lkv), BF16),
                   jax.ShapeDtypeStruct((t_all, LANES), BF16)],
        compiler_params=_params(("parallel",)),
    )(proj, qn, kvn, cos, sin)


def _mla_prep_bwd(dcq, dckv, dkr_h, proj, qn, kvn, cos, sin, *, name, tm, lq, lkv):
    t_all, w = proj.shape
    n_heads = dkr_h.shape[0]

    def body(dcq_ref, dckv_ref, dkr_ref, p_ref, qn_ref, kvn_ref, cos_ref, sin_ref, dp_ref, dqn_ref, dkvn_ref):
        pv = p_ref[...]
        dxq, dqn = _rms_bwd_math(dcq_ref[...], pv[:, :lq], qn_ref[...])
        dxkv, dkvn = _rms_bwd_math(dckv_ref[...], pv[:, lq:lq + lkv], kvn_ref[...])
        dkr = dkr_ref[0]
        for hh in range(1, n_heads):
            dkr = dkr + dkr_ref[hh]
        dkr = _unrope(dkr, cos_ref[...], sin_ref[...])
        dp_ref[...] = jnp.concatenate([dxq, dxkv, dkr], axis=1).astype(BF16)

        @pl.when(pl.program_id(0) == 0)
        def _():
            dqn_ref[...] = dqn
            dkvn_ref[...] = dkvn

        @pl.when(pl.program_id(0) > 0)
        def _():
            dqn_ref[...] += dqn
            dkvn_ref[...] += dkvn

    def row(width):
        return pl.BlockSpec((tm, width), lambda i: (i, 0))

    def vec(width):
        return pl.BlockSpec((1, width), lambda i: (0, 0))

    return pl.pallas_call(
        body, name=name, grid=(t_all // tm,),
        in_specs=[row(lq), row(lkv), pl.BlockSpec((n_heads, tm, LANES), lambda i: (0, i, 0)), row(w),
                  vec(lq), vec(lkv), row(LANES), row(LANES)],
        out_specs=[row(w), vec(lq), vec(lkv)],
        out_shape=[jax.ShapeDtypeStruct((t_all, w), BF16), jax.ShapeDtypeStruct((1, lq), F32),
                   jax.ShapeDtypeStruct((1, lkv), F32)],
        compiler_params=_params(("arbitrary",)),
    )(dcq, dckv, dkr_h, proj, qn, kvn, cos, sin)


def _rope_q_epilogue(acc, cos_ref, sin_ref):
    parts = []
    for g in range(acc.shape[1] // LANES):
        blk = acc[:, g * LANES:(g + 1) * LANES]
        parts.append(_rope(blk, cos_ref[...], sin_ref[...]) if g % 2 == 1 else blk)
    return jnp.concatenate(parts, axis=1)


def _chunk_causal(rows, cols, row0=0):
    r = row0 + lax.broadcasted_iota(jnp.int32, (rows, cols), 0)
    c = lax.broadcasted_iota(jnp.int32, (rows, cols), 1)
    return (c >> 6) <= (r >> 6)


def _meta_keys(rows, cols):
    return lax.broadcasted_iota(jnp.int32, (rows, cols), 1) < N_META


def _attn_fwd(q, kv, kr, *, name, n_heads, tq, n_real, scale):
    t_all = q.shape[0]
    nq = (n_real - N_META) // tq
    assert N_META + nq * tq == n_real and tq % CHUNK == 0 and t_all >= LANES
    n_pad = t_all - n_real
    sub = tq // 2 if (tq // 2) % CHUNK == 0 else tq

    def body(q_ref, kv_ref, kr_ref, o_ref, lse_ref, k_scr, m_scr, l_scr, acc_scr):
        k_scr[:, :QK_NOPE] = kv_ref[:, :QK_NOPE]
        k_scr[:, QK_NOPE:] = kr_ref[...]
        if n_pad:
            o_ref[pl.ds(n_real, n_pad), :] = jnp.zeros((n_pad, V_HEAD), o_ref.dtype)
            lse_ref[pl.ds(n_real, n_pad), :] = jnp.zeros((n_pad, LANES), F32)

        def scores(qt, c0, width):
            return lax.dot_general(qt, k_scr[pl.ds(c0, width), :], _NT, preferred_element_type=F32) * scale

        def values(c0, width):
            return kv_ref[pl.ds(c0, width), QK_NOPE:]

        s = jnp.where(_meta_keys(LANES, LANES), scores(q_ref[pl.ds(0, LANES), :], 0, LANES), NEG_BIG)
        m = jnp.max(s, axis=-1, keepdims=True)
        p = jnp.exp(s - m)
        l = jnp.sum(p, axis=-1, keepdims=True)
        o_meta = jnp.dot(p.astype(BF16), values(0, LANES), preferred_element_type=F32) / l
        o_ref[pl.ds(0, N_META), :] = o_meta[:N_META].astype(o_ref.dtype)
        lse_ref[pl.ds(0, N_META), :] = jnp.broadcast_to((m + jnp.log(l))[:N_META], (N_META, LANES))

        parts = [(u * sub, sub) for u in range(tq // sub)]

        def accumulate(u0, s, vals):
            rows = pl.ds(u0, s.shape[0])
            m_prev = m_scr[rows, :]
            m_new = jnp.maximum(m_prev, jnp.max(s, axis=-1, keepdims=True))
            alpha = jnp.exp(m_prev - m_new)
            p = jnp.exp(s - m_new)
            l_scr[rows, :] = alpha * l_scr[rows, :] + jnp.sum(p, axis=-1, keepdims=True)
            acc_scr[rows, :] = alpha * acc_scr[rows, :] + jnp.dot(p.astype(BF16), vals, preferred_element_type=F32)
            m_scr[rows, :] = m_new

        def q_tile(i, carry):
            r0 = pl.multiple_of(N_META + i * tq, N_META)
            qts = [q_ref[pl.ds(r0 + u0, rows), :] for u0, rows in parts]
            m_scr[...] = jnp.full(m_scr.shape, NEG_BIG, F32)
            l_scr[...] = jnp.zeros(l_scr.shape, F32)
            acc_scr[...] = jnp.zeros(acc_scr.shape, F32)

            def full_blocks(j, width):
                c0 = pl.multiple_of(N_META + j * tq, N_META)
                for (u0, _), qt in zip(parts, qts):
                    accumulate(u0, scores(qt, c0, width), values(c0, width))

            def two_blocks(jj, c):
                full_blocks(2 * jj, 2 * tq)
                return c

            lax.fori_loop(0, i // 2, two_blocks, 0)

            @pl.when(i % 2 == 1)
            def _():
                full_blocks(i - 1, tq)

            for (u0, rows), qt in zip(parts, qts):
                width = u0 + rows
                s = jnp.concatenate([jnp.where(_meta_keys(rows, LANES), scores(qt, 0, LANES), NEG_BIG),
                                     jnp.where(_chunk_causal(rows, width, u0), scores(qt, r0, width), NEG_BIG)], axis=1)
                accumulate(u0, s, jnp.concatenate([values(0, LANES), values(r0, width)], axis=0))
            o_ref[pl.ds(r0, tq), :] = (acc_scr[...] / l_scr[...]).astype(o_ref.dtype)
            lse_ref[pl.ds(r0, tq), :] = jnp.broadcast_to(m_scr[...] + jnp.log(l_scr[...]), (tq, LANES))
            return carry

        lax.fori_loop(0, nq, q_tile, 0)

    def head(width):
        return pl.BlockSpec((t_all, width), lambda h: (0, h))

    return pl.pallas_call(
        body, name=name, grid=(n_heads,),
        in_specs=[head(HEAD_W), head(HEAD_W), pl.BlockSpec((t_all, LANES), lambda h: (0, 0))],
        out_specs=[head(V_HEAD), pl.BlockSpec((None, t_all, LANES), lambda h: (h, 0, 0))],
        out_shape=[jax.ShapeDtypeStruct((t_all, n_heads * V_HEAD), BF16),
                   jax.ShapeDtypeStruct((n_heads, t_all, LANES), F32)],
        scratch_shapes=[pltpu.VMEM((t_all, HEAD_W), BF16), pltpu.VMEM((tq, 1), F32), pltpu.VMEM((tq, 1), F32),
                        pltpu.VMEM((tq, V_HEAD), F32)],
        compiler_params=_params(("parallel",)),
    )(q, kv, kr)


def _attn_bwd(q, kv, kr, o, lse, do, cos, sin, *, name, n_heads, tq, n_real, scale):
    t_all = q.shape[0]
    nq = (n_real - N_META) // tq
    assert N_META + nq * tq == n_real and tq % CHUNK == 0 and t_all >= LANES
    n_pad = t_all - n_real

    def body(q_ref, kv_ref, kr_ref, o_ref, lse_ref, do_ref, cos_ref, sin_ref, dq_ref, dkv_ref, dkr_ref,
             k_scr, dk_scr, dv_scr, dq_scr):
        k_scr[:, :QK_NOPE] = kv_ref[:, :QK_NOPE]
        k_scr[:, QK_NOPE:] = kr_ref[...]
        dk_scr[...] = jnp.zeros(dk_scr.shape, F32)
        dv_scr[...] = jnp.zeros(dv_scr.shape, F32)
        if n_pad:
            dq_ref[pl.ds(n_real, n_pad), :] = jnp.zeros((n_pad, HEAD_W), dq_ref.dtype)

        def blocks(qt, dot, lse_t, delta, segments):
            kb = jnp.concatenate([k_scr[pl.ds(c0, w), :] for c0, w, _ in segments], axis=0)
            vb = jnp.concatenate([kv_ref[pl.ds(c0, w), QK_NOPE:] for c0, w, _ in segments], axis=0)
            s = lax.dot_general(qt, kb, _NT, preferred_element_type=F32) * scale
            p = jnp.exp(s - lse_t)
            if any(m is not None for _, _, m in segments):
                rows = qt.shape[0]
                mask = jnp.concatenate([jnp.ones((rows, w), jnp.bool_) if m is None else m for _, w, m in segments], axis=1)
                p = jnp.where(mask, p, 0.0)
            dp = lax.dot_general(dot, vb, _NT, preferred_element_type=F32)
            ds = (p * (dp - delta) * scale).astype(BF16)
            dv = lax.dot_general(p.astype(BF16), dot, _TN, preferred_element_type=F32)
            dk = lax.dot_general(ds, qt, _TN, preferred_element_type=F32)
            at = 0
            for c0, w, _ in segments:
                dv_scr[pl.ds(c0, w), :] += dv[at:at + w]
                dk_scr[pl.ds(c0, w), :] += dk[at:at + w]
                at += w
            return jnp.dot(ds, kb, preferred_element_type=F32)

        def block(qt, dot, lse_t, delta, c0, width, mask):
            return blocks(qt, dot, lse_t, delta, [(c0, width, mask)])

        def write_dq(r0, rows, dq):
            cs, sn = cos_ref[pl.ds(r0, rows), :], sin_ref[pl.ds(r0, rows), :]
            dq_ref[pl.ds(r0, rows), :] = jnp.concatenate(
                [dq[:, :QK_NOPE], _unrope(dq[:, QK_NOPE:], cs, sn)], axis=1).astype(dq_ref.dtype)

        rows_m = lax.broadcasted_iota(jnp.int32, (LANES, LANES), 0) < N_META
        dot = do_ref[pl.ds(0, LANES), :]
        delta = jnp.sum(dot.astype(F32) * o_ref[pl.ds(0, LANES), :].astype(F32), axis=-1, keepdims=True)
        dq = block(q_ref[pl.ds(0, LANES), :], dot, lse_ref[pl.ds(0, LANES), :1], delta, 0, LANES,
                   _meta_keys(LANES, LANES) & rows_m)
        write_dq(0, N_META, dq[:N_META])

        def q_tile(i, carry):
            r0 = pl.multiple_of(N_META + i * tq, N_META)
            qt = q_ref[pl.ds(r0, tq), :]
            dot = do_ref[pl.ds(r0, tq), :]
            lse_t = lse_ref[pl.ds(r0, tq), :1]
            delta = jnp.sum(dot.astype(F32) * o_ref[pl.ds(r0, tq), :].astype(F32), axis=-1, keepdims=True)
            dq_scr[...] = blocks(qt, dot, lse_t, delta, [(0, LANES, _meta_keys(tq, LANES)), (r0, tq, _chunk_causal(tq, tq))])

            def two_blocks(jj, c):
                c0 = pl.multiple_of(N_META + 2 * jj * tq, N_META)
                dq_scr[...] += block(qt, dot, lse_t, delta, c0, 2 * tq, None)
                return c

            lax.fori_loop(0, i // 2, two_blocks, 0)

            @pl.when(i % 2 == 1)
            def _():
                c0 = pl.multiple_of(N_META + (i - 1) * tq, N_META)
                dq_scr[...] += block(qt, dot, lse_t, delta, c0, tq, None)

            write_dq(r0, tq, dq_scr[...])
            return carry

        lax.fori_loop(0, nq, q_tile, 0)
        dk = dk_scr[...]
        dkv_ref[...] = jnp.concatenate([dk[:, :QK_NOPE], dv_scr[...]], axis=1).astype(dkv_ref.dtype)
        dkr_ref[...] = dk[:, QK_NOPE:]

    def head(width):
        return pl.BlockSpec((t_all, width), lambda h: (0, h))

    table = pl.BlockSpec((t_all, LANES), lambda h: (0, 0))
    per_head = pl.BlockSpec((None, t_all, LANES), lambda h: (h, 0, 0))
    return pl.pallas_call(
        body, name=name, grid=(n_heads,),
        in_specs=[head(HEAD_W), head(HEAD_W), table, head(V_HEAD), per_head, head(V_HEAD), table, table],
        out_specs=[head(HEAD_W), head(HEAD_W), per_head],
        out_shape=[jax.ShapeDtypeStruct((t_all, n_heads * HEAD_W), BF16), jax.ShapeDtypeStruct((t_all, n_heads * HEAD_W), BF16),
                   jax.ShapeDtypeStruct((n_heads, t_all, LANES), F32)],
        scratch_shapes=[pltpu.VMEM((t_all, HEAD_W), BF16), pltpu.VMEM((t_all, HEAD_W), F32), pltpu.VMEM((t_all, V_HEAD), F32),
                        pltpu.VMEM((tq, HEAD_W), F32)],
        compiler_params=_params(("parallel",)),
    )(q, kv, kr, o, lse, do, cos, sin)


LRU_ROWS = 128


def _shifted_back(ref, t0, rows, shift_max):
    main = ref[pl.ds(t0, rows), :]
    prev = ref[pl.ds(pl.multiple_of(jnp.maximum(t0 - SUBLANES, 0), SUBLANES), SUBLANES), :]
    prev = jnp.where(t0 > 0, prev, 0.0)
    ext = jnp.concatenate([prev, main], axis=0)
    return [main] + [pltpu.roll(ext, s, 0)[SUBLANES:, :] for s in range(1, shift_max + 1)]


def _shifted_ahead(ref, t0, rows, t_all, shift_max):
    main = ref[pl.ds(t0, rows), :]
    nxt = ref[pl.ds(pl.multiple_of(jnp.minimum(t0 + rows, t_all - SUBLANES), SUBLANES), SUBLANES), :]
    nxt = jnp.where(t0 + rows < t_all, nxt, 0.0)
    ext = jnp.concatenate([main, nxt], axis=0)
    return [main] + [pltpu.roll(ext, rows + SUBLANES - s, 0)[:rows, :] for s in range(1, shift_max + 1)]


def _conv_fwd(xp_ref, t0, rows, cw, cb):
    sh = _shifted_back(xp_ref, t0, rows, 3)
    out = cb + cw[3:4, :] * sh[0]
    for k in range(3):
        out = out + cw[k:k + 1, :] * sh[3 - k]
    return out, sh


def _lru_gates(xb, wga, bga, wgx, bgx, sp):
    xbb = xb.astype(BF16)
    r = _sigmoid(jnp.dot(xbb, wga, preferred_element_type=F32) + bga)
    ig = _sigmoid(jnp.dot(xbb, wgx, preferred_element_type=F32) + bgx)
    la = -LRU_C * r * sp
    a = jnp.exp(la)
    s = jnp.sqrt(_neg_expm1(2.0 * la))
    return xbb, r, ig, a, s


def _scan_tile(a, b, reverse):
    rows = a.shape[0]
    ridx = lax.broadcasted_iota(jnp.int32, a.shape, 0)
    s = 1
    while s < rows:
        if reverse:
            keep = ridx < rows - s
            a_sh, b_sh = pltpu.roll(a, rows - s, 0), pltpu.roll(b, rows - s, 0)
        else:
            keep = ridx >= s
            a_sh, b_sh = pltpu.roll(a, s, 0), pltpu.roll(b, s, 0)
        b = jnp.where(keep, a * b_sh + b, b)
        a = jnp.where(keep, a * a_sh, a)
        s *= 2
    return a, b


def _lru_fwd(xy, conv_w, conv_b, wga, bga, wgx, bgx, lam, *, name):
    t_all = xy.shape[0]
    dr = xy.shape[1] // 2
    c = LANES
    nblk = dr // c
    rows = LRU_ROWS
    nt = t_all // rows

    def body(xp_ref, yp_ref, cw_ref, cb_ref, wga_ref, bga_ref, wgx_ref, bgx_ref, lam_ref, hs_ref, hsy_ref):
        cw, cb = cw_ref[...], cb_ref[...]
        sp = _softplus_neg(lam_ref[...])

        def tile(t, h_in):
            t0 = pl.multiple_of(t * rows, rows)
            xb, _ = _conv_fwd(xp_ref, t0, rows, cw, cb)
            _, _, ig, a, s = _lru_gates(xb, wga_ref[0], bga_ref[...], wgx_ref[0], bgx_ref[...], sp)
            cum_a, h0 = _scan_tile(a, s * (ig * xb), reverse=False)
            hs = cum_a * h_in + h0
            hs_ref[pl.ds(t0, rows), :] = hs
            hsy_ref[pl.ds(t0, rows), :] = (hs * _gelu(yp_ref[pl.ds(t0, rows), :])).astype(BF16)
            return hs[rows - 1:, :]

        lax.fori_loop(0, nt, tile, jnp.zeros((1, c), F32))

    col = pl.BlockSpec((t_all, c), lambda b: (0, b))
    vec = pl.BlockSpec((1, c), lambda b: (0, b))
    wsp = pl.BlockSpec((1, c, c), lambda b: (b, 0, 0))
    return pl.pallas_call(
        body, name=name, grid=(nblk,),
        in_specs=[col, pl.BlockSpec((t_all, c), lambda b: (0, nblk + b)), pl.BlockSpec((4, c), lambda b: (0, b)), vec,
                  wsp, vec, wsp, vec, vec],
        out_specs=[col, col],
        out_shape=[jax.ShapeDtypeStruct((t_all, dr), F32), jax.ShapeDtypeStruct((t_all, dr), BF16)],
        compiler_params=_params(("parallel",)),
    )(xy, xy, conv_w, conv_b, wga, bga, wgx, bgx, lam)


def _lru_bwd(xy, hs, dhsy, conv_w, conv_b, wga, bga, wgx, bgx, lam, *, name):
    t_all = xy.shape[0]
    dr = xy.shape[1] // 2
    c = LANES
    nblk = dr // c
    rows = LRU_ROWS
    nt = t_all // rows

    def body(xp_ref, yp_ref, hs_ref, dh_ref, cw_ref, cb_ref, wga_ref, bga_ref, wgx_ref, bgx_ref, lam_ref,
             dxp_ref, dyp_ref, dcw_ref, dcb_ref, dwga_ref, dbga_ref, dwgx_ref, dbgx_ref, dlam_ref,
             xb_scr, r_scr, i_scr, a_scr):
        cw, cb = cw_ref[...], cb_ref[...]
        lamv = lam_ref[...]
        sp = _softplus_neg(lamv)
        sig_neg = 1.0 / (1.0 + jnp.exp(lamv))
        wga_v, wgx_v = wga_ref[0], wgx_ref[0]

        def recompute(t, carry):
            t0 = pl.multiple_of(t * rows, rows)
            xb, _ = _conv_fwd(xp_ref, t0, rows, cw, cb)
            _, r, ig, a, _ = _lru_gates(xb, wga_v, bga_ref[...], wgx_v, bgx_ref[...], sp)
            xb_scr[pl.ds(t0, rows), :] = xb
            r_scr[pl.ds(t0, rows), :] = r
            i_scr[pl.ds(t0, rows), :] = ig
            a_scr[pl.ds(t0, rows), :] = a
            return carry

        lax.fori_loop(0, nt, recompute, 0)
        dwga_ref[...] = jnp.zeros(dwga_ref.shape, F32)
        dwgx_ref[...] = jnp.zeros(dwgx_ref.shape, F32)

        def tile(ti, carry):
            lam_in, dbga, dbgx, dlam, dcw, dcb = carry
            t = nt - 1 - ti
            t0 = pl.multiple_of(t * rows, rows)
            a_now, a_next = _shifted_ahead(a_scr, t0, rows, t_all, 1)
            yp = yp_ref[pl.ds(t0, rows), :]
            dhy = dh_ref[pl.ds(t0, rows), :]
            cum_a, lam0 = _scan_tile(a_next, dhy * _gelu(yp), reverse=True)
            lam_t = cum_a * lam_in + lam0
            hs_now, hs_prev = _shifted_back(hs_ref, t0, rows, 1)
            da = lam_t * hs_prev
            xb = xb_scr[pl.ds(t0, rows), :]
            r = r_scr[pl.ds(t0, rows), :]
            ig = i_scr[pl.ds(t0, rows), :]
            la = -LRU_C * r * sp
            s = jnp.sqrt(_neg_expm1(2.0 * la))
            d_ixb = lam_t * s
            dla = da * a_now - (lam_t * ig * xb) * (a_now * a_now / s)
            dzr = dla * (-LRU_C * sp) * r * (1.0 - r)
            dzi = d_ixb * xb * ig * (1.0 - ig)
            dzr_b, dzi_b = dzr.astype(BF16), dzi.astype(BF16)
            xbb = xb.astype(BF16)
            dwga_ref[0] += lax.dot_general(xbb, dzr_b, _TN, preferred_element_type=F32)
            dwgx_ref[0] += lax.dot_general(xbb, dzi_b, _TN, preferred_element_type=F32)
            dxb = (d_ixb * ig + lax.dot_general(dzr_b, wga_v, _NT, preferred_element_type=F32)
                   + lax.dot_general(dzi_b, wgx_v, _NT, preferred_element_type=F32))
            xb_scr[pl.ds(t0, rows), :] = dxb
            dyp_ref[pl.ds(t0, rows), :] = (dhy * hs_now * _gelu_grad(yp)).astype(BF16)
            ahead = _shifted_ahead(xb_scr, t0, rows, t_all, 3)
            dxp = cw[3:4, :] * ahead[0]
            for k in range(3):
                dxp = dxp + cw[k:k + 1, :] * ahead[3 - k]
            dxp_ref[pl.ds(t0, rows), :] = dxp.astype(BF16)
            back = _shifted_back(xp_ref, t0, rows, 3)
            dcw_t = jnp.concatenate([jnp.sum(dxb * back[3 - k], axis=0, keepdims=True) for k in range(4)], axis=0)
            return (lam_t[:1, :], dbga + jnp.sum(dzr, axis=0, keepdims=True), dbgx + jnp.sum(dzi, axis=0, keepdims=True),
                    dlam + jnp.sum(dla * r, axis=0, keepdims=True), dcw + dcw_t, dcb + jnp.sum(dxb, axis=0, keepdims=True))

        zero = jnp.zeros((1, c), F32)
        _, dbga, dbgx, dlam, dcw, dcb = lax.fori_loop(0, nt, tile, (zero, zero, zero, zero, jnp.zeros((4, c), F32), zero))
        dbga_ref[...] = dbga
        dbgx_ref[...] = dbgx
        dlam_ref[...] = dlam * (LRU_C * sig_neg)
        dcw_ref[...] = dcw
        dcb_ref[...] = dcb

    col = pl.BlockSpec((t_all, c), lambda b: (0, b))
    col2 = pl.BlockSpec((t_all, c), lambda b: (0, nblk + b))
    vec = pl.BlockSpec((1, c), lambda b: (0, b))
    tap = pl.BlockSpec((4, c), lambda b: (0, b))
    wsp = pl.BlockSpec((1, c, c), lambda b: (b, 0, 0))
    vshape = jax.ShapeDtypeStruct((1, dr), F32)
    wshape = jax.ShapeDtypeStruct((nblk, c, c), F32)
    def planes_body(*refs):
        dxy_ref = refs[11]
        body(*refs[:11], dxy_ref.at[0], dxy_ref.at[1], *refs[12:])

    return pl.pallas_call(
        planes_body, name=name, grid=(nblk,),
        in_specs=[col, col2, col, col, tap, vec, wsp, vec, wsp, vec, vec],
        out_specs=[pl.BlockSpec((2, t_all, c), lambda b: (0, 0, b)), tap, vec, wsp, vec, wsp, vec, vec],
        out_shape=[jax.ShapeDtypeStruct((2, t_all, dr), BF16),
                   jax.ShapeDtypeStruct((4, dr), F32), vshape, wshape, vshape, wshape, vshape, vshape],
        scratch_shapes=[pltpu.VMEM((t_all, c), F32)] * 4,
        compiler_params=_params(("parallel",)),
    )(xy, xy, hs, dhsy, conv_w, conv_b, wga, bga, wgx, bgx, lam)


def _mesh_pos():
    return lax.axis_index("x"), lax.axis_index("y"), lax.axis_index("c")


def _all_gather(shards, *, name):
    n = len(shards)

    def body(*refs):
        ins, outs, token = refs[:n], refs[n:2 * n], refs[2 * n]
        send_sems, recv_sems, local_sems = refs[2 * n + 1:]
        token[...] = jnp.zeros(token.shape, token.dtype)
        x, y, c = _mesh_pos()
        me, sibling = (x, y, c), (x, y, 1 - c)
        chips = [(1 - x, y), (x, 1 - y), (1 - x, 1 - y)]
        slot = _slot

        def copy(a, k, block, to, src=None):
            dst = outs[a].at[slot(block)]
            return pltpu.make_async_remote_copy(
                src_ref=dst if src is None else src, dst_ref=dst, send_sem=send_sems.at[a, k],
                recv_sem=recv_sems.at[a, k], device_id=to, device_id_type=MESH)

        mine = [pltpu.make_async_copy(ins[a], outs[a].at[slot(me)], local_sems.at[a]) for a in range(n)]
        for cp in mine:
            cp.start()
        first = []
        for a in range(n):
            first.append(copy(a, 0, me, sibling, src=ins[a]))
            first += [copy(a, 1 + j, me, (*chip, c), src=ins[a]) for j, chip in enumerate(chips)]
        for cp in first:
            cp.start()
        passed = []
        for a in range(n):
            for j, chip in enumerate(chips):
                copy(a, 1 + j, (*chip, c), me).wait_recv()
                fwd = copy(a, 4 + j, (*chip, c), sibling)
                fwd.start()
                passed.append(fwd)
        for a in range(n):
            copy(a, 0, sibling, me).wait_recv()
            for j, chip in enumerate(chips):
                copy(a, 4 + j, (*chip, 1 - c), me).wait_recv()
        for cp in first + passed:
            cp.wait_send()
        for cp in mine:
            cp.wait()

    any_spec = pl.BlockSpec(memory_space=pl.ANY)
    outs = pl.pallas_call(
        body, name=name,
        in_specs=[any_spec] * n, out_specs=[any_spec] * n + [pl.BlockSpec(memory_space=pltpu.VMEM)],
        out_shape=[jax.ShapeDtypeStruct((N_DEV,) + s.shape, s.dtype) for s in shards]
        + [jax.ShapeDtypeStruct((SUBLANES, LANES), F32)],
        scratch_shapes=[pltpu.SemaphoreType.DMA((n, 7)), pltpu.SemaphoreType.DMA((n, 7)), pltpu.SemaphoreType.DMA((n,))],
    )(*shards)
    return list(outs[:n]), outs[n][0, 0]


_HBM = pl.BlockSpec(memory_space=pltpu.HBM)
_SEM = pl.BlockSpec(memory_space=pltpu.SEMAPHORE)
_ANY = pl.BlockSpec(memory_space=pl.ANY)
_EFFECT = pltpu.SideEffectType.DATAFLOW_SIDE_EFFECTING


def _slot(p):
    return 4 * p[0] + 2 * p[1] + p[2]


def _remote(src, dst, send, recv, idx, to):
    return pltpu.make_async_remote_copy(src_ref=src, dst_ref=dst, send_sem=send.at[idx], recv_sem=recv.at[idx],
                                        device_id=to, device_id_type=MESH)


def _ag_plan_own(a, src, land, send, recv):
    x, y, c = _mesh_pos()
    dst = land.at[_slot((x, y, c))]
    targets = [(x, y, 1 - c), (1 - x, y, c), (x, 1 - y, c), (1 - x, 1 - y, c)]
    return [_remote(src, dst, send, recv, 4 * a + k, to) for k, to in enumerate(targets)]


def _ag_plan_pass(a, src, land, send, recv):
    x, y, c = _mesh_pos()
    blocks = [land.at[_slot((px, py, c))] for px, py in ((1 - x, y), (x, 1 - y), (1 - x, 1 - y))]
    return [_remote(blk, blk, send, recv, 3 * a + k, (x, y, 1 - c)) for k, blk in enumerate(blocks)]


def _rs_plan_sibling(a, src, land, send, recv):
    x, y, c = _mesh_pos()
    return [_remote(src.at[2 * j + (1 - c)], land.at[j], send, recv, 4 * a + j, (x, y, 1 - c)) for j in range(4)]


def _rs_plan_chips(a, src, land, send, recv):
    x, y, c = _mesh_pos()
    out = []
    for k in (1, 2, 3):
        px = 1 - x if k & 2 else x
        py = 1 - y if k & 1 else y
        out.append(_remote(src.at[2 * px + py], land.at[k - 1], send, recv, 3 * a + k - 1, (px, py, c)))
    return out


def _in_hbm(a):
    return pltpu.with_memory_space_constraint(a, pltpu.HBM)


def _exchange_start(srcs, lands, plan, n_k, *, name):
    ns, n = len(srcs), len(lands)

    def body(*refs):
        src_refs, land_refs = refs[:ns], refs[ns:ns + n]
        send, recv = refs[ns + n], refs[ns + n + 1]
        token = refs[-1]
        for a in range(n):
            for cp in plan(a, src_refs[a] if ns else None, land_refs[a], send, recv):
                cp.start()
        token[...] = jnp.zeros(token.shape, token.dtype)

    bufs = list(srcs) + list(lands)
    outs = pl.pallas_call(
        body, name=name,
        out_shape=(pltpu.SemaphoreType.DMA((n * n_k,)), pltpu.SemaphoreType.DMA((n * n_k,)),
                   *[pltpu.HBM(b.shape, b.dtype) for b in bufs], jax.ShapeDtypeStruct((SUBLANES, LANES), F32)),
        in_specs=[_HBM] * (ns + n),
        out_specs=(_SEM, _SEM, *[_HBM] * (ns + n), pl.BlockSpec(memory_space=pltpu.VMEM)),
        input_output_aliases={i: 2 + i for i in range(ns + n)},
        compiler_params=pltpu.CompilerParams(has_side_effects=_EFFECT),
    )(*[_in_hbm(b) for b in bufs])
    return outs[0], outs[1], list(outs[2:2 + ns]), list(outs[2 + ns:2 + ns + n]), outs[-1]


def _exchange_wait(started, plan, after, *, name):
    send, recv, srcs, lands, _ = started
    ns, n = len(srcs), len(lands)

    def body(*refs):
        src_refs, land_refs = refs[:ns], refs[ns:ns + n]
        send_ref, recv_ref = refs[ns + n], refs[ns + n + 1]
        for a in range(n):
            for cp in plan(a, src_refs[a] if ns else None, land_refs[a], send_ref, recv_ref):
                cp.wait_send()
                cp.wait_recv()

    bufs = list(srcs) + list(lands)
    outs = pl.pallas_call(
        body, name=name,
        out_shape=tuple(pltpu.HBM(b.shape, b.dtype) for b in bufs),
        in_specs=[_HBM] * (ns + n) + [_SEM, _SEM, _ANY],
        out_specs=tuple([_HBM] * (ns + n)),
        input_output_aliases={i: i for i in range(ns + n)},
        compiler_params=pltpu.CompilerParams(has_side_effects=_EFFECT),
    )(*bufs, send, recv, after)
    return list(outs[:ns]), list(outs[ns:])


def _pair_add(grads, landed, core, *, name, tr):
    _, r_all, c_all = grads.shape

    def body(core_ref, g_ref, l_ref, o_ref):
        o_ref[...] = (g_ref[...].astype(F32) + l_ref[...].astype(F32)).astype(o_ref.dtype)

    return pl.pallas_call(
        body, name=name,
        grid_spec=pltpu.PrefetchScalarGridSpec(
            num_scalar_prefetch=1, grid=(4, r_all // tr),
            in_specs=[pl.BlockSpec((None, tr, c_all), lambda j, i, core_ref: (2 * j + core_ref[0], i, 0)),
                      pl.BlockSpec((None, tr, c_all), lambda j, i, core_ref: (j, i, 0))],
            out_specs=pl.BlockSpec((None, tr, c_all), lambda j, i, core_ref: (j, i, 0))),
        out_shape=jax.ShapeDtypeStruct((4, r_all, c_all), grads.dtype),
        compiler_params=_params(("parallel", "parallel")),
    )(core, grads, landed)


def _adamw_math(w, g, m, v):
    m2 = ADAM_B1 * m + (1.0 - ADAM_B1) * g
    v2 = ADAM_B2 * v + (1.0 - ADAM_B2) * (g * g)
    m_hat = m2 / (1.0 - ADAM_B1 ** ADAM_STEP)
    v_hat = v2 / (1.0 - ADAM_B2 ** ADAM_STEP)
    delta = -ADAM_LR * (m_hat / (jnp.sqrt(v_hat) + ADAM_EPS) + ADAM_WD * w)
    return delta, m2, v2


def _adamw(w, m, v, terms, order, *, name, tr, col_block=None, own=None, stack=None):
    r_all, c_all = w.shape
    n_slots = terms.shape[0]

    def body(*refs):
        if col_block is not None or own is not None:
            refs = refs[1:]
        own_ref = None
        if own is not None:
            own_ref, refs = refs[0], refs[1:]
        w_ref, m_ref, v_ref, t_ref, g_ref, d_ref, m2_ref, v2_ref = refs
        if own_ref is not None:
            g = own_ref[...].astype(F32) + t_ref[order[0]].astype(F32)
        else:
            g = t_ref[order[0]].astype(F32)
        for s in order[1:]:
            g = g + t_ref[s].astype(F32)
        delta, m2, v2 = _adamw_math(w_ref[...], g, m_ref[...], v_ref[...])
        g_ref[...] = g
        d_ref[...] = delta
        m2_ref[...] = m2
        v2_ref[...] = v2

    shape = jax.ShapeDtypeStruct((r_all, c_all), F32)
    if own is not None:
        layer, n_layers, prev = stack
        row = pl.BlockSpec((tr, c_all), lambda i, idx: (i, 0))
        slab = pl.BlockSpec((None, tr, c_all), lambda i, idx: (layer, i, 0))
        carried = [] if prev is None else list(prev)

        def stacked_body(*refs):
            body(*refs[:6], *refs[6 + len(carried):])

        return pl.pallas_call(
            stacked_body, name=name,
            grid_spec=pltpu.PrefetchScalarGridSpec(
                num_scalar_prefetch=1, grid=(r_all // tr,),
                in_specs=[pl.BlockSpec((None, tr, c_all), lambda i, idx: (idx[0], i, 0)), row, row, row,
                          pl.BlockSpec((n_slots, tr, c_all), lambda i, idx: (0, i, 0))] + [_ANY] * len(carried),
                out_specs=[slab] * 4),
            out_shape=[jax.ShapeDtypeStruct((n_layers, r_all, c_all), F32)] * 4,
            input_output_aliases={6 + k: k for k in range(len(carried))},
            compiler_params=_params(("parallel",)),
        )(own[1], own[0], w, m, v, terms, *carried)
    if col_block is None:
        row = pl.BlockSpec((tr, c_all), lambda i: (i, 0))
        return pl.pallas_call(
            body, name=name, grid=(r_all // tr,),
            in_specs=[row, row, row, pl.BlockSpec((n_slots, tr, c_all), lambda i: (0, i, 0))],
            out_specs=[row] * 4, out_shape=[shape] * 4, compiler_params=_params(("parallel",)),
        )(w, m, v, terms)
    row = pl.BlockSpec((tr, c_all), lambda i, blk: (i, 0))
    return pl.pallas_call(
        body, name=name,
        grid_spec=pltpu.PrefetchScalarGridSpec(
            num_scalar_prefetch=1, grid=(r_all // tr,),
            in_specs=[row, row, row, pl.BlockSpec((n_slots, tr, c_all), lambda i, blk: (0, i, blk[0]))],
            out_specs=[row] * 4),
        out_shape=[shape] * 4, compiler_params=_params(("parallel",)),
    )(col_block, w, m, v, terms)


def _rope_tables(t_all):
    pos = jnp.arange(t_all, dtype=F32)
    inv_freq = ROPE_THETA ** (-jnp.arange(0, QK_ROPE, 2, dtype=F32) / QK_ROPE)
    ang = pos[:, None] * inv_freq[None, :]
    cos, sin = jnp.cos(ang), jnp.sin(ang)
    return jnp.tile(cos, (1, LANES // (QK_ROPE // 2))), jnp.tile(sin, (1, LANES // (QK_ROPE // 2)))


def _adam_row_tile(r_all, c_all, block_bytes=512 * 1024):
    target = max(SUBLANES, block_bytes // (4 * c_all))
    return _pick(r_all, [t for t in (1024, 704, 512, 352, 256, 176, 128, 64, 32, 16, 8) if t <= target])


def _rows_natural(wg):
    return wg.reshape(wg.shape[0] * wg.shape[1], wg.shape[2])


def _mla_layer_fwd(tag, h, g_mix, ws, qn, kvn, cos, sin, *, tm, tq, n_heads, scale, n_real):
    w_in, w_uq, w_ukv, w_o = _rows_natural(ws[0]), ws[1], ws[2], _rows_natural(ws[3])
    t_all, d = h.shape
    lq, lkv = qn.shape[1], kvn.shape[1]
    tmb = _pick(t_all, _ROW_TILES)
    hn = _rmsnorm_fwd(h, g_mix, name=f"norm_mix{tag}", tm=tm)
    proj = _mm_nn(hn, w_in, name=f"mla_in{tag}", out_dtype=F32, tm=tmb, tn=w_in.shape[1], tk=_pick(d, _DIVS))
    cq, ckv, kr = _mla_prep_fwd(proj, qn, kvn, cos, sin, name=f"mla_prep{tag}", tm=tm, lq=lq, lkv=lkv)
    q = _mm_nn(cq, w_uq, name=f"mla_q{tag}", out_dtype=BF16, tm=tmb, tn=w_uq.shape[2], tk=lq, b_blocked=True,
               epilogue=_rope_q_epilogue, extras=(cos, sin))
    kv = _mm_nn(ckv, w_ukv, name=f"mla_kv{tag}", out_dtype=BF16, tm=tmb, tn=w_ukv.shape[2], tk=lkv, b_blocked=True)
    o, lse = _attn_fwd(q, kv, kr, name=f"attn_fwd{tag}", n_heads=n_heads, tq=tq, n_real=n_real, scale=scale)
    h_mid = _mm_nn(o, w_o, name=f"mla_o{tag}", out_dtype=F32, tm=tm, tn=d, tk=o.shape[1], res=h)
    return h_mid, (hn, proj, cq, ckv, kr, q, kv, o, lse)


def _mla_layer_bwd(tag, dh, dh_b, h_in, saved, g_mix, ws, qn, kvn, cos, sin, *, tm, tq, n_heads, scale, n_real, early=None,
                   after=None):
    hn, proj, cq, ckv, kr, q, kv, o, lse = saved
    w_in, w_uq, w_ukv, w_o = _rows_natural(ws[0]), ws[1], ws[2], _rows_natural(ws[3])
    t_all, d = h_in.shape
    lq, lkv = qn.shape[1], kvn.shape[1]
    ov = o.shape[1]
    tmb = _pick(t_all, _ROW_TILES)
    tn_d, tk_d = _pick(d, _DIVS[1:]), _pick(d, _DIVS)
    do = _mm_nt(dh_b, w_o, name=f"mla_do{tag}", out_dtype=BF16, tm=tmb, tn=_pick(ov, _DIVS[1:]), tk=tk_d, after=after)
    dw_o = _mm_tn(o, dh_b, name=f"mla_dwo{tag}", out_dtype=BF16, tm=_pick(ov, _DIVS[2:]), tn=tn_d, tk=t_all)
    dq, dkv, dkr_h = _attn_bwd(q, kv, kr, o, lse, do, cos, sin, name=f"attn_bwd{tag}", n_heads=n_heads, tq=tq, n_real=n_real,
                               scale=scale)
    hw, kw = w_uq.shape[2], w_ukv.shape[2]
    dw_uq = _mm_tn(cq, dq, name=f"mla_dwuq{tag}", out_dtype=BF16, tm=lq, tn=hw, tk=t_all, out_block=hw)
    dcq = _mm_nt(dq, w_uq, name=f"mla_dcq{tag}", out_dtype=F32, tm=tm, tn=lq, tk=dq.shape[1], b_blocked=True)
    dw_ukv = _mm_tn(ckv, dkv, name=f"mla_dwukv{tag}", out_dtype=BF16, tm=lkv, tn=kw, tk=t_all, out_block=kw)
    dckv = _mm_nt(dkv, w_ukv, name=f"mla_dckv{tag}", out_dtype=F32, tm=tm, tn=lkv, tk=dkv.shape[1], b_blocked=True)
    first = [dw_uq, dw_ukv, dw_o.reshape(N_DEV, -1, d)]
    if early is not None:
        qn = qn + early(first)
    dproj, dqn, dkvn = _mla_prep_bwd(dcq, dckv, dkr_h, proj, qn, kvn, cos, sin, name=f"mla_prep_bwd{tag}", tm=tm, lq=lq, lkv=lkv)
    wc = w_in.shape[1]
    dw_in = _mm_tn(hn, dproj, name=f"mla_dwin{tag}", out_dtype=BF16, tm=_pick(d, _DIVS[2:]), tn=wc, tk=t_all)
    dhn = _mm_nt(dproj, w_in, name=f"mla_dhn{tag}", out_dtype=BF16, tm=tmb, tn=tn_d, tk=wc)
    dh, dh_b, dg = _rmsnorm_bwd(dhn, h_in, g_mix, dh, name=f"norm_mix_bwd{tag}", tm=tm)
    return dh, dh_b, dg, dqn, dkvn, [dw_in.reshape(N_DEV, -1, wc)] + ([] if early is not None else first)


def _lru_layer_fwd(tag, h, g_mix, ws, small, *, tm):
    w_lin, w_lo = ws[0], _rows_natural(ws[1])
    t_all, d = h.shape
    dr = w_lo.shape[0]
    tmb = _pick(t_all, _ROW_TILES)
    hn = _rmsnorm_fwd(h, g_mix, name=f"norm_mix{tag}", tm=tm)
    xy = _mm_nn(hn, w_lin, name=f"lru_in{tag}", out_dtype=F32, tm=tmb, tn=w_lin.shape[2], tk=_pick(d, _DIVS), b_blocked=True,
                rows_outer=True)
    hs, hsy = _lru_fwd(xy, *small, name=f"lru_fwd{tag}")
    h_mid = _mm_nn(hsy, w_lo, name=f"lru_o{tag}", out_dtype=F32, tm=tm, tn=d, tk=dr, res=h)
    return h_mid, (hn, xy, hs, hsy)


def _lru_layer_bwd(tag, dh, dh_b, h_in, saved, g_mix, ws, small, *, tm, after=None):
    hn, xy, hs, hsy = saved
    w_lin, w_lo = ws[0], _rows_natural(ws[1])
    t_all, d = h_in.shape
    dr = w_lo.shape[0]
    tmb = _pick(t_all, _ROW_TILES)
    tn_d, tk_d = _pick(d, _DIVS[1:]), _pick(d, _DIVS)
    dhsy = _mm_nt(dh_b, w_lo, name=f"lru_dhsy{tag}", out_dtype=F32, tm=tmb, tn=_pick(dr, _DIVS[1:]), tk=tk_d, after=after)
    dw_lo = _mm_tn(hsy, dh_b, name=f"lru_dwo{tag}", out_dtype=BF16, tm=_pick(dr, _DIVS[2:]), tn=tn_d, tk=t_all)
    dxy, *dsmall = _lru_bwd(xy, hs, dhsy, *small, name=f"lru_bwd{tag}")
    lw = w_lin.shape[2]
    dw_lin = _mm_tn(hn, dxy, name=f"lru_dwin{tag}", out_dtype=BF16, tm=tn_d, tn=lw, tk=t_all, out_block=lw)
    dhn = _mm_nt(dxy, w_lin, name=f"lru_dhn{tag}", out_dtype=BF16, tm=tm, tn=tn_d, tk=2 * dr, b_blocked=True)
    dh, dh_b, dg = _rmsnorm_bwd(dhn, h_in, g_mix, dh, name=f"norm_mix_bwd{tag}", tm=tm)
    return dh, dh_b, dg, tuple(dsmall), [dw_lin, dw_lo.reshape(N_DEV, -1, d)]


def _ffn_layer_fwd(tag, h_mid, g_ffn, ws, *, tm):
    w_gu, w_down = ws[0], _rows_natural(ws[1])
    t_all, d = h_mid.shape
    f_all = w_down.shape[0]
    tmb = _pick(t_all, _ROW_TILES)
    fk = _pick(f_all, (1408,) + _DIVS[1:])
    hn2 = _rmsnorm_fwd(h_mid, g_ffn, name=f"norm_ffn{tag}", tm=tm)
    gu, act = _ffn_up(hn2, w_gu, name=f"ffn_up{tag}", tm=tm)
    h_out = _mm_nn(act, w_down, name=f"ffn_down{tag}", out_dtype=F32, tm=tm, tn=_pick(d, _DIVS[1:]), tk=f_all, res=h_mid)
    return h_out, (hn2, gu, act)


def _ffn_layer_bwd(tag, dh, dh_b, h_mid, saved, g_ffn, ws, *, tm, after=None):
    hn2, gu, act = saved
    w_gu, w_down = ws[0], _rows_natural(ws[1])
    t_all, d = h_mid.shape
    f_all = w_down.shape[0]
    f_local = w_gu.shape[2]
    tmb = _pick(t_all, _ROW_TILES)
    fk = _pick(f_all, (1408,) + _DIVS[1:])
    tn_d, tk_d = _pick(d, _DIVS[1:]), _pick(d, _DIVS)
    dgu = _ffn_dact(dh_b, w_down, gu, name=f"ffn_dact{tag}", tm=tm, tn=f_local, after=after)
    dw_down = _mm_tn(act, dh_b, name=f"ffn_dwdown{tag}", out_dtype=BF16, tm=fk, tn=_pick(d, _DIVS[2:]), tk=t_all)
    dhn2 = _mm_nt(dgu, w_gu, name=f"ffn_dhn{tag}", out_dtype=BF16, tm=tm, tn=_pick(d, _DIVS[2:]), tk=2 * f_all, b_blocked=True)
    dw_gu = _mm_tn(hn2, dgu, name=f"ffn_dwgu{tag}", out_dtype=BF16, tm=_pick(d, _DIVS[2:]), tn=f_local, tk=t_all, out_block=f_local,
                   cols_outer=True)
    dh, dh_b, dg = _rmsnorm_bwd(dhn2, h_mid, g_ffn, dh, name=f"norm_ffn_bwd{tag}", tm=tm)
    return dh, dh_b, dg, [dw_gu, dw_down.reshape(N_DEV, -1, d)]


def kernel(x, meta_tokens, norm_mix, norm_ffn, norm_final, mla_w_in, mla_q_norm, mla_kv_norm, mla_w_uq, mla_w_ukv, mla_w_o, lru_w_in, lru_conv_w, lru_conv_b, lru_w_gate_a, lru_b_gate_a, lru_w_gate_x, lru_b_gate_x, lru_lambda, lru_w_o, ffn_w_gu, ffn_w_down, loss_target, m_meta_tokens, m_norm_mix, m_norm_ffn, m_norm_final, m_mla_w_in, m_mla_q_norm, m_mla_kv_norm, m_mla_w_uq, m_mla_w_ukv, m_mla_w_o, m_lru_w_in, m_lru_conv_w, m_lru_conv_b, m_lru_w_gate_a, m_lru_b_gate_a, m_lru_w_gate_x, m_lru_b_gate_x, m_lru_lambda, m_lru_w_o, m_ffn_w_gu, m_ffn_w_down, v_meta_tokens, v_norm_mix, v_norm_ffn, v_norm_final, v_mla_w_in, v_mla_q_norm, v_mla_kv_norm, v_mla_w_uq, v_mla_w_ukv, v_mla_w_o, v_lru_w_in, v_lru_conv_w, v_lru_conv_b, v_lru_w_gate_a, v_lru_b_gate_a, v_lru_w_gate_x, v_lru_b_gate_x, v_lru_lambda, v_lru_w_o, v_ffn_w_gu, v_ffn_w_down):
    seq, d = x.shape[1], x.shape[2]
    assert seq % CHUNK == 0
    n_real = N_META + seq
    t_all = -(-n_real // LANES) * LANES
    tm = _pick(t_all, (384, 256, 128))
    tq = _pick(seq, (512, 256, 128, 64))
    depth = norm_mix.shape[0]
    n_mla, n_lru = mla_w_in.shape[0], lru_w_in.shape[0]
    lq, lkv = mla_q_norm.shape[1], mla_kv_norm.shape[1]
    w_in_cols = lq + lkv + LANES
    heads_local = mla_w_uq.shape[2] // (QK_NOPE + QK_ROPE)
    n_heads = heads_local * N_DEV
    dr = lru_w_gate_a.shape[1] * lru_w_gate_a.shape[2]
    scale = (QK_NOPE + QK_ROPE) ** -0.5
    cx, cy, cc = _mesh_pos()
    core = jnp.reshape(cc, (1,)).astype(jnp.int32)
    my_slot = jnp.reshape(4 * cx + 2 * cy + cc, (1,)).astype(jnp.int32)

    def pad_cols(w, cols):
        return jnp.pad(w, ((0, 0), (0, cols - w.shape[1])))

    def pad_heads(w):
        k_all = w.shape[0]
        w3 = w.reshape(k_all, heads_local, QK_NOPE + QK_ROPE)
        return jnp.pad(w3, ((0, 0), (0, 0), (0, HEAD_W - QK_NOPE - QK_ROPE))).reshape(k_all, heads_local * HEAD_W)

    def unpad_heads(w):
        k_all = w.shape[0]
        return w.reshape(k_all, heads_local, HEAD_W)[:, :, :QK_NOPE + QK_ROPE].reshape(k_all, -1)

    small_rows = N_META + n_lru * 4 + 2 * n_lru
    small_pad = -(-small_rows // SUBLANES) * SUBLANES

    def pack_small(meta, conv_w, conv_b, lam):
        rows = jnp.concatenate([meta, conv_w.reshape(n_lru * 4, -1), conv_b, lam], axis=0)
        return jnp.pad(rows, ((0, small_pad - small_rows), (0, 0)))

    def unpack_small(p):
        o1 = N_META + n_lru * 4
        return (p[:N_META], p[N_META:o1].reshape(n_lru, 4, -1), p[o1:o1 + n_lru], p[o1 + n_lru:o1 + 2 * n_lru])

    (small_full,), small_done = _all_gather([pack_small(meta_tokens, lru_conv_w, lru_conv_b, lru_lambda)], name="ag_small")
    small_full = jnp.transpose(small_full, (1, 0, 2)).reshape(small_pad, -1)
    meta_full, conv_w_full, conv_b_full, lam_full = unpack_small(small_full)

    def wire(w):
        return (w + small_done).astype(BF16)

    mla_shards, lru_shards, ffn_shards = [], [], []
    for j in range(n_mla):
        mla_shards.append([wire(pad_cols(mla_w_in[j], w_in_cols)), wire(pad_heads(mla_w_uq[j])), wire(mla_w_ukv[j]),
                           wire(mla_w_o[j])])
    for j in range(n_lru):
        lru_shards.append([wire(lru_w_in[j]), wire(lru_w_o[j])])
    for layer in range(depth):
        ffn_shards.append([wire(ffn_w_gu[layer]), wire(ffn_w_down[layer])])

    n_sub = 2 * depth
    groups = []
    for layer in range(depth):
        groups += [mla_shards[layer // 2] if layer % 2 == 0 else lru_shards[layer // 2], ffn_shards[layer]]
    slot_idx = 4 * cx + 2 * cy + cc
    ag_own = []
    for gi, shards in enumerate(groups):
        lands = [lax.dynamic_update_slice(lax.empty((N_DEV,) + s.shape, s.dtype), s[None], (slot_idx, 0, 0)) for s in shards]
        ag_own.append(_exchange_start(shards, lands, _ag_plan_own, 4, name=f"ag{gi}_start"))
    ag_pass = [None] * n_sub
    weights = [None] * n_sub

    def ag_landed(gi, after):
        _, lands = _exchange_wait(ag_own[gi], _ag_plan_own, after, name=f"ag{gi}_wait")
        ag_pass[gi] = _exchange_start([], lands, _ag_plan_pass, 3, name=f"ag{gi}_pass")
        return ag_pass[gi][4][0, 0]

    def ag_done(gi, after):
        _, weights[gi] = _exchange_wait(ag_pass[gi], _ag_plan_pass, after, name=f"ag{gi}_pass_wait")

    cos, sin = _rope_tables(t_all)
    zeros_tail = jnp.zeros((t_all - n_real, d), F32)
    started = ag_own[0][4][0, 0]
    for st in ag_own[1:]:
        started = started + st[4][0, 0]
    h = jnp.concatenate([meta_full + started, x[0], zeros_tail], axis=0)
    target = jnp.concatenate([jnp.zeros((N_META, d), F32), loss_target[0], zeros_tail], axis=0)

    attn_kw = dict(tm=tm, tq=tq, n_heads=n_heads, scale=scale, n_real=n_real)

    def lru_small(j):
        return (conv_w_full[j], conv_b_full[j][None, :], lru_w_gate_a[j].astype(BF16), lru_b_gate_a[j].reshape(1, dr),
                lru_w_gate_x[j].astype(BF16), lru_b_gate_x[j].reshape(1, dr), lam_full[j][None, :])

    def before_sublayer(k, act):
        tok = ag_landed(k, act) if k <= 1 else 0.0
        ag_done(k, act)
        if 1 <= k < n_sub - 1:
            tok = tok + ag_landed(k + 1, act)
        return tok

    saved = []
    for layer in range(depth):
        j = layer // 2
        g_mix = norm_mix[layer][None, :] + before_sublayer(2 * layer, h)
        if layer % 2 == 0:
            h_mid, mix_saved = _mla_layer_fwd(layer, h, g_mix, weights[2 * layer], mla_q_norm[j][None, :],
                                              mla_kv_norm[j][None, :], cos, sin, **attn_kw)
        else:
            h_mid, mix_saved = _lru_layer_fwd(layer, h, g_mix, weights[2 * layer], lru_small(j), tm=tm)
        g_ffn = norm_ffn[layer][None, :] + before_sublayer(2 * layer + 1, h_mid)
        h_out, ffn_saved = _ffn_layer_fwd(layer, h_mid, g_ffn, weights[2 * layer + 1], tm=tm)
        saved.append((h, h_mid, mix_saved, ffn_saved))
        h = h_out

    loss_part, dh, dh_b, dg_final = _loss_head(h, target, norm_final[None, :], name="loss_head", tm=tm, n_real=n_real)
    loss = lax.psum(loss_part[0, 0], ("x", "y", "c"))

    rs_sib, rs_chip, reduced = [None] * (n_sub + 1), [None] * (n_sub + 1), [None] * (n_sub + 1)
    chip_idx = jnp.reshape(2 * cx + cy, (1,)).astype(jnp.int32)

    def rs_begin(k, grads):
        lands = [lax.empty((4,) + g.shape[1:], g.dtype) for g in grads]
        rs_sib[k] = _exchange_start(grads, lands, _rs_plan_sibling, 4, name=f"rs{k}_start")
        return rs_sib[k][4]

    def rs_middle(k, after):
        grads, landed = _exchange_wait(rs_sib[k], _rs_plan_sibling, after, name=f"rs{k}_wait")
        parts = [_pair_add(g, l, core, name=f"rs{k}_add{a}", tr=_adam_row_tile(g.shape[1], g.shape[2], 4 * 1024 * 1024))
                 for a, (g, l) in enumerate(zip(grads, landed))]
        lands = [lax.empty((3,) + p.shape[1:], p.dtype) for p in parts]
        rs_chip[k] = _exchange_start(parts, lands, _rs_plan_chips, 3, name=f"rs{k}_chips")
        return rs_chip[k][4]

    def rs_end(k, after):
        reduced[k] = _exchange_wait(rs_chip[k], _rs_plan_chips, after, name=f"rs{k}_chips_wait")

    d_norm_mix, d_norm_ffn = [None] * depth, [None] * depth
    d_qn, d_kvn = [None] * n_mla, [None] * n_mla
    d_small = {k: [None] * n_lru for k in ("cw", "cb", "wga", "bga", "wgx", "bgx", "lam")}
    tok, waiting = None, None
    gate_own = [None] * n_lru
    for layer in reversed(range(depth)):
        j = layer // 2
        h_in, h_mid, mix_saved, ffn_saved = saved[layer]
        dh, dh_b, d_norm_ffn[layer], ffn_g = _ffn_layer_bwd(layer, dh, dh_b, h_mid, ffn_saved, norm_ffn[layer][None, :],
                                                            weights[2 * layer + 1], tm=tm, after=tok)
        tok = rs_begin(2 * layer + 1, ffn_g)
        if waiting is not None:
            tok = tok + rs_middle(waiting, dh)
        waiting = 2 * layer + 1
        if layer == 0:
            tok = tok + rs_middle(waiting, dh)
            waiting = None
        g_mix = norm_mix[layer][None, :]
        if layer % 2 == 0:
            early = (lambda g: (rs_begin(n_sub, g) + rs_middle(n_sub, g[0]))[0, 0]) if layer == 0 else None
            dh, dh_b, d_norm_mix[layer], d_qn[j], d_kvn[j], mix_g = _mla_layer_bwd(
                layer, dh, dh_b, h_in, mix_saved, g_mix, weights[2 * layer], mla_q_norm[j][None, :], mla_kv_norm[j][None, :],
                cos, sin, early=early, after=tok, **attn_kw)
            tok = rs_begin(2 * layer, mix_g)
        else:
            dh, dh_b, d_norm_mix[layer], dsmall, mix_g = _lru_layer_bwd(layer, dh, dh_b, h_in, mix_saved, g_mix,
                                                                        weights[2 * layer], lru_small(j), tm=tm, after=tok)
            for key, val in zip(("cw", "cb", "wga", "bga", "wgx", "bgx", "lam"), dsmall):
                d_small[key][j] = val
            gates = [d_small["wga"][j].reshape(-1, LANES), d_small["wgx"][j].reshape(-1, LANES)]
            gate_lands = [lax.dynamic_update_slice(lax.empty((N_DEV,) + g.shape, g.dtype), g[None], (slot_idx, 0, 0)) for g in gates]
            gate_own[j] = _exchange_start(gates, gate_lands, _ag_plan_own, 4, name=f"ag_gates{j}_start")
            tok = rs_begin(2 * layer, mix_g) + gate_own[j][4]
        if waiting is not None:
            tok = tok + rs_middle(waiting, dh)
        waiting = 2 * layer
    rs_middle(waiting, dh)

    grad_x = dh[N_META:n_real][None]

    d_meta = dh[:N_META]
    small_grad = pack_small(d_meta, jnp.stack(d_small["cw"], axis=0), jnp.concatenate(d_small["cb"], axis=0),
                            jnp.concatenate(d_small["lam"], axis=0))
    rep_grads = [
        jnp.concatenate(d_norm_mix, axis=0), jnp.concatenate(d_norm_ffn, axis=0), dg_final,
        jnp.concatenate(d_qn, axis=0), jnp.concatenate(d_kvn, axis=0),
        jnp.concatenate(d_small["bga"], axis=0), jnp.concatenate(d_small["bgx"], axis=0),
    ]
    small_srcs = [small_grad] + [jnp.pad(g, ((0, -g.shape[0] % SUBLANES), (0, 0))) for g in rep_grads]
    small_lands = [lax.dynamic_update_slice(lax.empty((N_DEV,) + s.shape, s.dtype), s[None], (slot_idx, 0, 0))
                   for s in small_srcs]
    small_own = _exchange_start(small_srcs, small_lands, _ag_plan_own, 4, name="ag_grads_start")

    res = {}

    def adam_sharded(nm, k, a, idx, n_layers, w, m, v):
        parts, landed = reduced[k]
        r_all, c_all = landed[a].shape[1], landed[a].shape[2]
        res[nm] = _adamw(w.reshape(r_all, c_all), m.reshape(r_all, c_all), v.reshape(r_all, c_all), landed[a], (0, 1, 2),
                         name=f"adamw_{nm}{idx}", tr=_adam_row_tile(r_all, c_all, 2 * 1024 * 1024), own=(parts[a], chip_idx),
                         stack=(idx, n_layers, res.get(nm)))

    after = small_own[4]
    for k in reversed(range(n_sub)):
        rs_end(k, after)
        if k == 0:
            rs_end(n_sub, after)
            reduced[0] = tuple(first + rest for first, rest in zip(reduced[0], reduced[n_sub]))
        layer, j = k // 2, k // 4
        if k % 2 == 1:
            adam_sharded("ffn_w_gu", k, 0, layer, depth, ffn_w_gu[layer], m_ffn_w_gu[layer], v_ffn_w_gu[layer])
            adam_sharded("ffn_w_down", k, 1, layer, depth, ffn_w_down[layer], m_ffn_w_down[layer], v_ffn_w_down[layer])
            after = res["ffn_w_down"][0]
        elif layer % 2 == 0:
            adam_sharded("mla_w_in", k, 0, j, n_mla, pad_cols(mla_w_in[j], w_in_cols), pad_cols(m_mla_w_in[j], w_in_cols),
                         pad_cols(v_mla_w_in[j], w_in_cols))
            adam_sharded("mla_w_uq", k, 1, j, n_mla, pad_heads(mla_w_uq[j]), pad_heads(m_mla_w_uq[j]), pad_heads(v_mla_w_uq[j]))
            adam_sharded("mla_w_ukv", k, 2, j, n_mla, mla_w_ukv[j], m_mla_w_ukv[j], v_mla_w_ukv[j])
            adam_sharded("mla_w_o", k, 3, j, n_mla, mla_w_o[j], m_mla_w_o[j], v_mla_w_o[j])
            after = res["mla_w_o"][0]
        else:
            adam_sharded("lru_w_in", k, 0, j, n_lru, lru_w_in[j], m_lru_w_in[j], v_lru_w_in[j])
            adam_sharded("lru_w_o", k, 1, j, n_lru, lru_w_o[j], m_lru_w_o[j], v_lru_w_o[j])
            after = res["lru_w_o"][0]
    res["mla_w_in"] = [t[:, :, :lq + lkv + QK_ROPE] for t in res["mla_w_in"]]
    res["mla_w_uq"] = [t.reshape(n_mla, lq, heads_local, HEAD_W)[:, :, :, :QK_NOPE + QK_ROPE].reshape(n_mla, lq, -1)
                       for t in res["mla_w_uq"]]

    _, small_lands = _exchange_wait(small_own, _ag_plan_own, after, name="ag_grads_wait")
    small_pass = _exchange_start([], small_lands, _ag_plan_pass, 3, name="ag_grads_pass")
    _, all_small = _exchange_wait(small_pass, _ag_plan_pass, after, name="ag_grads_pass_wait")
    gate_terms = []
    for j in range(n_lru):
        _, lands = _exchange_wait(gate_own[j], _ag_plan_own, after, name=f"ag_gates{j}_wait")
        gate_pass = _exchange_start([], lands, _ag_plan_pass, 3, name=f"ag_gates{j}_pass")
        gate_terms.append(_exchange_wait(gate_pass, _ag_plan_pass, after, name=f"ag_gates{j}_pass_wait")[1])
    wga_terms = jnp.concatenate([t[0] for t in gate_terms], axis=1)
    wgx_terms = jnp.concatenate([t[1] for t in gate_terms], axis=1)
    slot_order = tuple(range(N_DEV))

    def adam_rep(terms, w, m, v, tag):
        r_pad, c_all = terms.shape[1], terms.shape[2]

        def prep(t):
            t2 = t.reshape(-1, c_all)
            return jnp.pad(t2, ((0, r_pad - t2.shape[0]), (0, 0)))

        outs = _adamw(prep(w), prep(m), prep(v), terms, slot_order, name=f"adamw_{tag}", tr=_adam_row_tile(r_pad, c_all))
        n_rows = w.size // c_all
        return [o[:n_rows].reshape(w.shape) for o in outs]

    small_w = pack_small(meta_tokens, lru_conv_w, lru_conv_b, lru_lambda)
    small_m = pack_small(m_meta_tokens, m_lru_conv_w, m_lru_conv_b, m_lru_lambda)
    small_v = pack_small(v_meta_tokens, v_lru_conv_w, v_lru_conv_b, v_lru_lambda)
    small_out = _adamw(small_w, small_m, small_v, all_small[0], slot_order, name="adamw_small", tr=small_pad, col_block=my_slot)
    small_out = [unpack_small(o) for o in small_out]
    for idx, key in enumerate(("meta_tokens", "lru_conv_w", "lru_conv_b", "lru_lambda")):
        res[key] = [small_out[k][idx] for k in range(4)]

    res["norm_mix"] = adam_rep(all_small[1], norm_mix, m_norm_mix, v_norm_mix, "norm_mix")
    res["norm_ffn"] = adam_rep(all_small[2], norm_ffn, m_norm_ffn, v_norm_ffn, "norm_ffn")
    res["norm_final"] = adam_rep(all_small[3], norm_final, m_norm_final, v_norm_final, "norm_final")
    res["mla_q_norm"] = adam_rep(all_small[4], mla_q_norm, m_mla_q_norm, v_mla_q_norm, "mla_q_norm")
    res["mla_kv_norm"] = adam_rep(all_small[5], mla_kv_norm, m_mla_kv_norm, v_mla_kv_norm, "mla_kv_norm")
    res["lru_w_gate_a"] = adam_rep(wga_terms, lru_w_gate_a, m_lru_w_gate_a, v_lru_w_gate_a, "lru_w_gate_a")
    res["lru_b_gate_a"] = adam_rep(all_small[6], lru_b_gate_a, m_lru_b_gate_a, v_lru_b_gate_a, "lru_b_gate_a")
    res["lru_w_gate_x"] = adam_rep(wgx_terms, lru_w_gate_x, m_lru_w_gate_x, v_lru_w_gate_x, "lru_w_gate_x")
    res["lru_b_gate_x"] = adam_rep(all_small[7], lru_b_gate_x, m_lru_b_gate_x, v_lru_b_gate_x, "lru_b_gate_x")

    names = ["meta_tokens", "norm_mix", "norm_ffn", "norm_final", "mla_w_in", "mla_q_norm", "mla_kv_norm", "mla_w_uq",
             "mla_w_ukv", "mla_w_o", "lru_w_in", "lru_conv_w", "lru_conv_b", "lru_w_gate_a", "lru_b_gate_a", "lru_w_gate_x",
             "lru_b_gate_x", "lru_lambda", "lru_w_o", "ffn_w_gu", "ffn_w_down"]
    shapes = dict(meta_tokens=meta_tokens, norm_mix=norm_mix, norm_ffn=norm_ffn, norm_final=norm_final, mla_w_in=mla_w_in,
                  mla_q_norm=mla_q_norm, mla_kv_norm=mla_kv_norm, mla_w_uq=mla_w_uq, mla_w_ukv=mla_w_ukv, mla_w_o=mla_w_o,
                  lru_w_in=lru_w_in, lru_conv_w=lru_conv_w, lru_conv_b=lru_conv_b, lru_w_gate_a=lru_w_gate_a,
                  lru_b_gate_a=lru_b_gate_a, lru_w_gate_x=lru_w_gate_x, lru_b_gate_x=lru_b_gate_x, lru_lambda=lru_lambda,
                  lru_w_o=lru_w_o, ffn_w_gu=ffn_w_gu, ffn_w_down=ffn_w_down)
    outs = [loss, grad_x]
    for k in range(4):
        outs += [res[nm][k].reshape(shapes[nm].shape) for nm in names]
    return tuple(outs)
```

```python
import math

import jax
import jax.numpy as jnp
from jax import lax
from jax.experimental import pallas as pl
from jax.experimental.pallas import tpu as pltpu

F32 = jnp.float32
BF16 = jnp.bfloat16
MESH = pl.DeviceIdType.MESH

N_META = 16
CHUNK = 64
QK_NOPE = 128
QK_ROPE = 64
V_HEAD = 128
HEAD_W = 256
ROPE_THETA = 10000.0
LRU_C = 8.0
RMS_EPS = 1e-6
NEG_BIG = -1e30
ADAM_LR, ADAM_B1, ADAM_B2, ADAM_EPS, ADAM_WD, ADAM_STEP = 0.001, 0.9, 0.999, 1e-08, 0.01, 10

LANES = 128
SUBLANES = 8
VMEM_LIMIT_BYTES = 52 * 1024 * 1024
N_DEV = 8

_NT = (((1,), (1,)), ((), ()))
_TN = (((0,), (0,)), ((), ()))
_DIVS = (2048, 1024, 512, 256, 128)
_ROW_TILES = (1408, 1024, 512, 256, 128)


def _params(dims):
    return pltpu.CompilerParams(dimension_semantics=dims, vmem_limit_bytes=VMEM_LIMIT_BYTES)


def _pick(n, candidates):
    for c in candidates:
        if c <= n and n % c == 0:
            return c
    return n


def _sigmoid(z):
    return 0.5 + 0.5 * jnp.tanh(0.5 * z)


def _gelu(x):
    c = math.sqrt(2.0 / math.pi)
    return 0.5 * x * (1.0 + jnp.tanh(c * (x + 0.044715 * x * x * x)))


def _gelu_grad(x):
    c = math.sqrt(2.0 / math.pi)
    th = jnp.tanh(c * (x + 0.044715 * x * x * x))
    return 0.5 * (1.0 + th) + 0.5 * x * (1.0 - th * th) * c * (1.0 + 3.0 * 0.044715 * x * x)


def _neg_expm1(x):
    poly = -x * (1.0 + x * (1.0 / 2.0) * (1.0 + x * (1.0 / 3.0) * (1.0 + x * (1.0 / 4.0) * (
        1.0 + x * (1.0 / 5.0) * (1.0 + x * (1.0 / 6.0) * (1.0 + x * (1.0 / 7.0)))))))
    return jnp.where(x > -0.25, poly, 1.0 - jnp.exp(x))


def _softplus_neg(lam):
    e = jnp.exp(-jnp.abs(lam))
    log1p = jnp.where(e > 1e-4, jnp.log(1.0 + e), e * (1.0 - e * (0.5 - e * (1.0 / 3.0))))
    return jnp.maximum(-lam, 0.0) + log1p


def _rot_half(x):
    lane = lax.broadcasted_iota(jnp.int32, x.shape, 1)
    first = (lane % QK_ROPE) < (QK_ROPE // 2)
    return jnp.where(first, -pltpu.roll(x, LANES - QK_ROPE // 2, 1), pltpu.roll(x, QK_ROPE // 2, 1))


def _rope(x, cos, sin):
    return x * cos + _rot_half(x) * sin


def _unrope(g, cos, sin):
    return g * cos - _rot_half(g) * sin


def _grid_order(rows_outer):
    if not rows_outer:
        return lambda f: f
    return lambda f: (lambda i, j, k: f(j, i, k))


def _mm_nn(a, b, *, name, out_dtype, tm, tn, tk, b_blocked=False, res=None, epilogue=None, extras=(), rows_outer=False):
    m_all, k_all = a.shape
    om = _grid_order(rows_outer)
    if b_blocked:
        g_all, kb, nb = b.shape
        n_all = g_all * nb
        assert nb % tn == 0
        r = nb // tn
        b_spec = pl.BlockSpec((None, tk, tn), om(lambda j, i, k: (j // r, k, j % r)))
    else:
        kb, n_all = b.shape
        b_spec = pl.BlockSpec((tk, tn), om(lambda j, i, k: (k, j)))
    assert kb == k_all and m_all % tm == 0 and n_all % tn == 0 and k_all % tk == 0
    nm, nn, nk = m_all // tm, n_all // tn, k_all // tk
    in_specs = [pl.BlockSpec((tm, tk), om(lambda j, i, k: (i, k))), b_spec]
    operands = [a, b]
    has_res = res is not None
    if has_res:
        in_specs.append(pl.BlockSpec((tm, tn), om(lambda j, i, k: (i, j))))
        operands.append(res)
    for e in extras:
        in_specs.append(pl.BlockSpec((tm, e.shape[1]), om(lambda j, i, k: (i, 0))))
        operands.append(e)
    n_ex = len(extras)

    def body(*refs):
        a_ref, b_ref = refs[0], refs[1]
        pos = 2
        res_ref = None
        if has_res:
            res_ref = refs[pos]
            pos += 1
        ex_refs = refs[pos:pos + n_ex]
        pos += n_ex
        o_ref = refs[pos]
        acc_ref = refs[pos + 1] if nk > 1 else None

        def finish(acc):
            if has_res:
                acc = acc + res_ref[...]
            if epilogue is not None:
                acc = epilogue(acc, *ex_refs)
            o_ref[...] = acc.astype(o_ref.dtype)

        prod = jnp.dot(a_ref[...], b_ref[...], preferred_element_type=F32)
        if nk == 1:
            finish(prod)
        else:
            k = pl.program_id(2)

            @pl.when(k == 0)
            def _():
                acc_ref[...] = prod

            @pl.when(k > 0)
            def _():
                acc_ref[...] += prod

            @pl.when(k == nk - 1)
            def _():
                finish(acc_ref[...])

    return pl.pallas_call(
        body, name=name, grid=(nm, nn, nk) if rows_outer else (nn, nm, nk), in_specs=in_specs,
        out_specs=pl.BlockSpec((tm, tn), om(lambda j, i, k: (i, j))),
        out_shape=jax.ShapeDtypeStruct((m_all, n_all), out_dtype),
        scratch_shapes=[pltpu.VMEM((tm, tn), F32)] if nk > 1 else [],
        compiler_params=_params(("parallel", "parallel", "arbitrary")),
    )(*operands)


def _mm_nt(a, b, *, name, out_dtype, tm, tn, tk, b_blocked=False, after=None):
    if a.ndim == 3:
        n_planes, m_all, kp = a.shape
        k_all = n_planes * kp
    else:
        n_planes, (m_all, k_all) = 0, a.shape
    if b_blocked and tk == k_all and b.shape[0] > 1:
        g_all, n_all, nb = b.shape
        assert g_all * nb == k_all and m_all % tm == 0 and n_all % tn == 0
        per_plane = kp // nb if n_planes else 0

        def whole_body(a_ref, b_ref, o_ref):
            acc = None
            for g in range(g_all):
                a_g = a_ref[g // per_plane, :, (g % per_plane) * nb:(g % per_plane + 1) * nb] if n_planes else a_ref[:, g * nb:(g + 1) * nb]
                prod = lax.dot_general(a_g, b_ref[g], _NT, preferred_element_type=F32)
                acc = prod if acc is None else acc + prod
            o_ref[...] = acc.astype(o_ref.dtype)

        a_whole = (pl.BlockSpec((n_planes, tm, kp), lambda j, i: (0, i, 0)) if n_planes
                   else pl.BlockSpec((tm, k_all), lambda j, i: (i, 0)))
        return pl.pallas_call(
            whole_body, name=name, grid=(n_all // tn, m_all // tm),
            in_specs=[a_whole, pl.BlockSpec((g_all, tn, nb), lambda j, i: (0, j, 0))],
            out_specs=pl.BlockSpec((tm, tn), lambda j, i: (i, j)),
            out_shape=jax.ShapeDtypeStruct((m_all, n_all), out_dtype),
            compiler_params=_params(("parallel", "parallel")),
        )(a, b)
    if n_planes:
        assert kp % tk == 0
        rp = kp // tk
        a_spec = pl.BlockSpec((None, tm, tk), lambda j, i, k: (k // rp, i, k % rp))
    else:
        a_spec = pl.BlockSpec((tm, tk), lambda j, i, k: (i, k))
    if b_blocked:
        g_all, n_all, nb = b.shape
        assert g_all * nb == k_all and nb % tk == 0
        r = nb // tk
        b_spec = pl.BlockSpec((None, tn, tk), lambda j, i, k: (k // r, j, k % r))
    else:
        n_all, kb = b.shape
        assert kb == k_all
        b_spec = pl.BlockSpec((tn, tk), lambda j, i, k: (j, k))
    assert m_all % tm == 0 and n_all % tn == 0 and k_all % tk == 0
    nm, nn, nk = m_all // tm, n_all // tn, k_all // tk

    def body(a_ref, b_ref, o_ref, *scratch):
        prod = lax.dot_general(a_ref[...], b_ref[...], _NT, preferred_element_type=F32)
        if nk == 1:
            o_ref[...] = prod.astype(o_ref.dtype)
        else:
            acc_ref = scratch[0]
            k = pl.program_id(2)

            @pl.when(k == 0)
            def _():
                acc_ref[...] = prod

            @pl.when(k > 0)
            def _():
                acc_ref[...] += prod

            @pl.when(k == nk - 1)
            def _():
                o_ref[...] = acc_ref[...].astype(o_ref.dtype)

    ordered = [] if after is None else [after]

    def ordered_body(a_ref, b_ref, *rest):
        body(a_ref, b_ref, *rest[len(ordered):])

    return pl.pallas_call(
        ordered_body, name=name, grid=(nn, nm, nk),
        in_specs=[a_spec, b_spec] + [pl.BlockSpec(memory_space=pl.ANY)] * len(ordered),
        out_specs=pl.BlockSpec((tm, tn), lambda j, i, k: (i, j)),
        out_shape=jax.ShapeDtypeStruct((m_all, n_all), out_dtype),
        scratch_shapes=[pltpu.VMEM((tm, tn), F32)] if nk > 1 else [],
        compiler_params=_params(("parallel", "parallel", "arbitrary")),
    )(a, b, *ordered)


def _mm_tn(a, b, *, name, out_dtype, tm, tn, tk, out_block=None, cols_outer=False):
    t_all, m_all = a.shape
    om = _grid_order(cols_outer)
    if b.ndim == 3:
        n_planes, tb, n_p = b.shape
        assert n_p % tn == 0
        rq = n_p // tn
        n_all = n_planes * n_p
        b_spec = pl.BlockSpec((None, tk, tn), om(lambda i, j, k: (j // rq, k, j % rq)))
    else:
        tb, n_all = b.shape
        b_spec = pl.BlockSpec((tk, tn), om(lambda i, j, k: (k, j)))
    assert tb == t_all and m_all % tm == 0 and n_all % tn == 0 and t_all % tk == 0
    nm, nn, nk = m_all // tm, n_all // tn, t_all // tk
    if out_block is None:
        out_shape = jax.ShapeDtypeStruct((m_all, n_all), out_dtype)
        out_spec = pl.BlockSpec((tm, tn), om(lambda i, j, k: (i, j)))
    else:
        assert out_block % tn == 0 and n_all % out_block == 0
        r = out_block // tn
        out_shape = jax.ShapeDtypeStruct((n_all // out_block, m_all, out_block), out_dtype)
        out_spec = pl.BlockSpec((None, tm, tn), om(lambda i, j, k: (j // r, i, j % r)))

    def body(a_ref, b_ref, o_ref, *scratch):
        prod = lax.dot_general(a_ref[...], b_ref[...], _TN, preferred_element_type=F32)
        if nk == 1:
            o_ref[...] = prod.astype(o_ref.dtype)
        else:
            acc_ref = scratch[0]
            k = pl.program_id(2)

            @pl.when(k == 0)
            def _():
                acc_ref[...] = prod

            @pl.when(k > 0)
            def _():
                acc_ref[...] += prod

            @pl.when(k == nk - 1)
            def _():
                o_ref[...] = acc_ref[...].astype(o_ref.dtype)

    return pl.pallas_call(
        body, name=name, grid=(nn, nm, nk) if cols_outer else (nm, nn, nk),
        in_specs=[pl.BlockSpec((tk, tm), om(lambda i, j, k: (k, i))), b_spec],
        out_specs=out_spec, out_shape=out_shape,
        scratch_shapes=[pltpu.VMEM((tm, tn), F32)] if nk > 1 else [],
        compiler_params=_params(("parallel", "parallel", "arbitrary")),
    )(a, b)


def _rmsnorm_fwd(x, g, *, name, tm):
    t_all, d = x.shape

    def body(x_ref, g_ref, o_ref):
        xv = x_ref[...]
        rstd = lax.rsqrt(jnp.mean(xv * xv, axis=-1, keepdims=True) + RMS_EPS)
        o_ref[...] = (xv * rstd * g_ref[...]).astype(o_ref.dtype)

    return pl.pallas_call(
        body, name=name, grid=(t_all // tm,),
        in_specs=[pl.BlockSpec((tm, d), lambda i: (i, 0)), pl.BlockSpec((1, d), lambda i: (0, 0))],
        out_specs=pl.BlockSpec((tm, d), lambda i: (i, 0)),
        out_shape=jax.ShapeDtypeStruct((t_all, d), BF16),
        compiler_params=_params(("parallel",)),
    )(x, g)


def _rms_bwd_math(dy, xv, g):
    rstd = lax.rsqrt(jnp.mean(xv * xv, axis=-1, keepdims=True) + RMS_EPS)
    xhat = xv * rstd
    dxh = dy * g
    dx = rstd * (dxh - xhat * jnp.mean(dxh * xhat, axis=-1, keepdims=True))
    return dx, jnp.sum(dy * xhat, axis=0, keepdims=True)


def _rmsnorm_bwd(dy, x, g, res, *, name, tm):
    t_all, d = x.shape

    def body(dy_ref, x_ref, g_ref, res_ref, dx_ref, dxb_ref, dg_ref):
        dx, dg = _rms_bwd_math(dy_ref[...].astype(F32), x_ref[...], g_ref[...])
        tot = res_ref[...] + dx
        dx_ref[...] = tot
        dxb_ref[...] = tot.astype(BF16)

        @pl.when(pl.program_id(0) == 0)
        def _():
            dg_ref[...] = dg

        @pl.when(pl.program_id(0) > 0)
        def _():
            dg_ref[...] += dg

    row = pl.BlockSpec((tm, d), lambda i: (i, 0))
    vec = pl.BlockSpec((1, d), lambda i: (0, 0))
    return pl.pallas_call(
        body, name=name, grid=(t_all // tm,),
        in_specs=[row, row, vec, row], out_specs=[row, row, vec],
        out_shape=[jax.ShapeDtypeStruct((t_all, d), F32), jax.ShapeDtypeStruct((t_all, d), BF16),
                   jax.ShapeDtypeStruct((1, d), F32)],
        compiler_params=_params(("arbitrary",)),
    )(dy, x, g, res)


def _loss_head(h, target, g, *, name, tm, n_real):
    t_all, d = h.shape

    def body(h_ref, t_ref, g_ref, loss_ref, dx_ref, dxb_ref, dg_ref):
        i = pl.program_id(0)
        xv = h_ref[...]
        gv = g_ref[...]
        rstd = lax.rsqrt(jnp.mean(xv * xv, axis=-1, keepdims=True) + RMS_EPS)
        y = xv * rstd * gv
        row = i * tm + lax.broadcasted_iota(jnp.int32, (tm, 1), 0)
        valid = (row >= N_META) & (row < n_real)
        err = jnp.where(valid, y - t_ref[...], 0.0)
        part = 0.5 * jnp.sum(jnp.mean(err * err, axis=-1, keepdims=True), axis=0, keepdims=True)
        dx, dg = _rms_bwd_math(err * (1.0 / d), xv, gv)
        dx_ref[...] = dx
        dxb_ref[...] = dx.astype(BF16)

        @pl.when(i == 0)
        def _():
            dg_ref[...] = dg
            loss_ref[...] = jnp.broadcast_to(part, loss_ref.shape)

        @pl.when(i > 0)
        def _():
            dg_ref[...] += dg
            loss_ref[...] += jnp.broadcast_to(part, loss_ref.shape)

    row = pl.BlockSpec((tm, d), lambda i: (i, 0))
    vec = pl.BlockSpec((1, d), lambda i: (0, 0))
    return pl.pallas_call(
        body, name=name, grid=(t_all // tm,),
        in_specs=[row, row, vec],
        out_specs=[pl.BlockSpec((1, LANES), lambda i: (0, 0)), row, row, vec],
        out_shape=[jax.ShapeDtypeStruct((1, LANES), F32), jax.ShapeDtypeStruct((t_all, d), F32),
                   jax.ShapeDtypeStruct((t_all, d), BF16), jax.ShapeDtypeStruct((1, d), F32)],
        compiler_params=_params(("arbitrary",)),
    )(h, target, g)


def _ffn_up(x, w_gu, *, name, tm):
    t_all, d = x.shape
    g_all, kb, nb = w_gu.shape
    half = g_all // 2
    f = half * nb
    assert kb == d and t_all % tm == 0

    def body(x_ref, wg_ref, wu_ref, gu_ref, act_ref):
        xv = x_ref[...]
        gv = jnp.dot(xv, wg_ref[...], preferred_element_type=F32)
        uv = jnp.dot(xv, wu_ref[...], preferred_element_type=F32)
        gu_ref[0] = gv.astype(gu_ref.dtype)
        gu_ref[1] = uv.astype(gu_ref.dtype)
        act_ref[...] = (gv * _sigmoid(gv) * uv).astype(act_ref.dtype)

    return pl.pallas_call(
        body, name=name, grid=(half, t_all // tm),
        in_specs=[pl.BlockSpec((tm, d), lambda j, i: (i, 0)), pl.BlockSpec((None, d, nb), lambda j, i: (j, 0, 0)),
                  pl.BlockSpec((None, d, nb), lambda j, i: (j + half, 0, 0))],
        out_specs=[pl.BlockSpec((2, tm, nb), lambda j, i: (0, i, j)), pl.BlockSpec((tm, nb), lambda j, i: (i, j))],
        out_shape=[jax.ShapeDtypeStruct((2, t_all, f), BF16), jax.ShapeDtypeStruct((t_all, f), BF16)],
        compiler_params=_params(("parallel", "parallel")),
    )(x, w_gu, w_gu)


def _ffn_dact(dy, w_down, gu, *, name, tm, tn, after=None):
    t_all, d = dy.shape
    f = w_down.shape[0]
    assert t_all % tm == 0 and f % tn == 0
    ordered = [] if after is None else [after]

    def body(dy_ref, w_ref, gu_ref, *rest):
        o_ref = rest[-1]
        dact = lax.dot_general(dy_ref[...], w_ref[...], _NT, preferred_element_type=F32)
        gv, uv = gu_ref[0].astype(F32), gu_ref[1].astype(F32)
        sg = _sigmoid(gv)
        o_ref[0] = (dact * uv * (sg * (1.0 + gv * (1.0 - sg)))).astype(o_ref.dtype)
        o_ref[1] = (dact * gv * sg).astype(o_ref.dtype)

    return pl.pallas_call(
        body, name=name, grid=(f // tn, t_all // tm),
        in_specs=[pl.BlockSpec((tm, d), lambda j, i: (i, 0)), pl.BlockSpec((tn, d), lambda j, i: (j, 0)),
                  pl.BlockSpec((2, tm, tn), lambda j, i: (0, i, j))] + [pl.BlockSpec(memory_space=pl.ANY)] * len(ordered),
        out_specs=pl.BlockSpec((2, tm, tn), lambda j, i: (0, i, j)),
        out_shape=jax.ShapeDtypeStruct((2, t_all, f), BF16),
        compiler_params=_params(("parallel", "parallel")),
    )(dy, w_down, gu, *ordered)


def _mla_prep_fwd(proj, qn, kvn, cos, sin, *, name, tm, lq, lkv):
    t_all, w = proj.shape

    def body(p_ref, qn_ref, kvn_ref, cos_ref, sin_ref, cq_ref, ckv_ref, kr_ref):
        pv = p_ref[...]
        xq = pv[:, :lq]
        xkv = pv[:, lq:lq + lkv]
        cq_ref[...] = (xq * lax.rsqrt(jnp.mean(xq * xq, axis=-1, keepdims=True) + RMS_EPS) * qn_ref[...]).astype(BF16)
        ckv_ref[...] = (xkv * lax.rsqrt(jnp.mean(xkv * xkv, axis=-1, keepdims=True) + RMS_EPS) * kvn_ref[...]).astype(BF16)
        kr_ref[...] = _rope(pv[:, lq + lkv:], cos_ref[...], sin_ref[...]).astype(BF16)

    def row(width):
        return pl.BlockSpec((tm, width), lambda i: (i, 0))

    def vec(width):
        return pl.BlockSpec((1, width), lambda i: (0, 0))

    return pl.pallas_call(
        body, name=name, grid=(t_all // tm,),
        in_specs=[row(w), vec(lq), vec(lkv), row(LANES), row(LANES)],
        out_specs=[row(lq), row(lkv), row(LANES)],
        out_shape=[jax.ShapeDtypeStruct((t_all, lq), BF16), jax.ShapeDtypeStruct((t_all, lkv), BF16),
                   jax.ShapeDtypeStruct((t_all, LANES), BF16)],
        compiler_params=_params(("parallel",)),
    )(proj, qn, kvn, cos, sin)


def _mla_prep_bwd(dcq, dckv, dkr_h, proj, qn, kvn, cos, sin, *, name, tm, lq, lkv):
    t_all, w = proj.shape
    n_heads = dkr_h.shape[0]

    def body(dcq_ref, dckv_ref, dkr_ref, p_ref, qn_ref, kvn_ref, cos_ref, sin_ref, dp_ref, dqn_ref, dkvn_ref):
        pv = p_ref[...]
        dxq, dqn = _rms_bwd_math(dcq_ref[...], pv[:, :lq], qn_ref[...])
        dxkv, dkvn = _rms_bwd_math(dckv_ref[...], pv[:, lq:lq + lkv], kvn_ref[...])
        dkr = dkr_ref[0]
        for hh in range(1, n_heads):
            dkr = dkr + dkr_ref[hh]
        dkr = _unrope(dkr, cos_ref[...], sin_ref[...])
        dp_ref[...] = jnp.concatenate([dxq, dxkv, dkr], axis=1).astype(BF16)

        @pl.when(pl.program_id(0) == 0)
        def _():
            dqn_ref[...] = dqn
            dkvn_ref[...] = dkvn

        @pl.when(pl.program_id(0) > 0)
        def _():
            dqn_ref[...] += dqn
            dkvn_ref[...] += dkvn

    def row(width):
        return pl.BlockSpec((tm, width), lambda i: (i, 0))

    def vec(width):
        return pl.BlockSpec((1, width), lambda i: (0, 0))

    return pl.pallas_call(
        body, name=name, grid=(t_all // tm,),
        in_specs=[row(lq), row(lkv), pl.BlockSpec((n_heads, tm, LANES), lambda i: (0, i, 0)), row(w),
                  vec(lq), vec(lkv), row(LANES), row(LANES)],
        out_specs=[row(w), vec(lq), vec(lkv)],
        out_shape=[jax.ShapeDtypeStruct((t_all, w), BF16), jax.ShapeDtypeStruct((1, lq), F32),
                   jax.ShapeDtypeStruct((1, lkv), F32)],
        compiler_params=_params(("arbitrary",)),
    )(dcq, dckv, dkr_h, proj, qn, kvn, cos, sin)


def _rope_q_epilogue(acc, cos_ref, sin_ref):
    parts = []
    for g in range(acc.shape[1] // LANES):
        blk = acc[:, g * LANES:(g + 1) * LANES]
        parts.append(_rope(blk, cos_ref[...], sin_ref[...]) if g % 2 == 1 else blk)
    return jnp.concatenate(parts, axis=1)


def _chunk_causal(rows, cols, row0=0):
    r = row0 + lax.broadcasted_iota(jnp.int32, (rows, cols), 0)
    c = lax.broadcasted_iota(jnp.int32, (rows, cols), 1)
    return (c >> 6) <= (r >> 6)


def _meta_keys(rows, cols):
    return lax.broadcasted_iota(jnp.int32, (rows, cols), 1) < N_META


def _attn_fwd(q, kv, kr, *, name, n_heads, tq, n_real, scale):
    t_all = q.shape[0]
    nq = (n_real - N_META) // tq
    assert N_META + nq * tq == n_real and tq % CHUNK == 0 and t_all >= LANES
    n_pad = t_all - n_real
    sub = tq // 2 if (tq // 2) % CHUNK == 0 else tq

    def body(q_ref, kv_ref, kr_ref, o_ref, lse_ref, k_scr, m_scr, l_scr, acc_scr):
        k_scr[:, :QK_NOPE] = kv_ref[:, :QK_NOPE]
        k_scr[:, QK_NOPE:] = kr_ref[...]
        if n_pad:
            o_ref[pl.ds(n_real, n_pad), :] = jnp.zeros((n_pad, V_HEAD), o_ref.dtype)
            lse_ref[pl.ds(n_real, n_pad), :] = jnp.zeros((n_pad, LANES), F32)

        def scores(qt, c0, width):
            return lax.dot_general(qt, k_scr[pl.ds(c0, width), :], _NT, preferred_element_type=F32) * scale

        def values(c0, width):
            return kv_ref[pl.ds(c0, width), QK_NOPE:]

        s = jnp.where(_meta_keys(LANES, LANES), scores(q_ref[pl.ds(0, LANES), :], 0, LANES), NEG_BIG)
        m = jnp.max(s, axis=-1, keepdims=True)
        p = jnp.exp(s - m)
        l = jnp.sum(p, axis=-1, keepdims=True)
        o_meta = jnp.dot(p.astype(BF16), values(0, LANES), preferred_element_type=F32) / l
        o_ref[pl.ds(0, N_META), :] = o_meta[:N_META].astype(o_ref.dtype)
        lse_ref[pl.ds(0, N_META), :] = jnp.broadcast_to((m + jnp.log(l))[:N_META], (N_META, LANES))

        parts = [(u * sub, sub) for u in range(tq // sub)]

        def accumulate(u0, s, vals):
            rows = pl.ds(u0, s.shape[0])
            m_prev = m_scr[rows, :]
            m_new = jnp.maximum(m_prev, jnp.max(s, axis=-1, keepdims=True))
            alpha = jnp.exp(m_prev - m_new)
            p = jnp.exp(s - m_new)
            l_scr[rows, :] = alpha * l_scr[rows, :] + jnp.sum(p, axis=-1, keepdims=True)
            acc_scr[rows, :] = alpha * acc_scr[rows, :] + jnp.dot(p.astype(BF16), vals, preferred_element_type=F32)
            m_scr[rows, :] = m_new

        def q_tile(i, carry):
            r0 = pl.multiple_of(N_META + i * tq, N_META)
            qts = [q_ref[pl.ds(r0 + u0, rows), :] for u0, rows in parts]
            m_scr[...] = jnp.full(m_scr.shape, NEG_BIG, F32)
            l_scr[...] = jnp.zeros(l_scr.shape, F32)
            acc_scr[...] = jnp.zeros(acc_scr.shape, F32)

            def full_blocks(j, width):
                c0 = pl.multiple_of(N_META + j * tq, N_META)
                for (u0, _), qt in zip(parts, qts):
                    accumulate(u0, scores(qt, c0, width), values(c0, width))

            def two_blocks(jj, c):
                full_blocks(2 * jj, 2 * tq)
                return c

            lax.fori_loop(0, i // 2, two_blocks, 0)

            @pl.when(i % 2 == 1)
            def _():
                full_blocks(i - 1, tq)

            for (u0, rows), qt in zip(parts, qts):
                width = u0 + rows
                s = jnp.concatenate([jnp.where(_meta_keys(rows, LANES), scores(qt, 0, LANES), NEG_BIG),
                                     jnp.where(_chunk_causal(rows, width, u0), scores(qt, r0, width), NEG_BIG)], axis=1)
                accumulate(u0, s, jnp.concatenate([values(0, LANES), values(r0, width)], axis=0))
            o_ref[pl.ds(r0, tq), :] = (acc_scr[...] / l_scr[...]).astype(o_ref.dtype)
            lse_ref[pl.ds(r0, tq), :] = jnp.broadcast_to(m_scr[...] + jnp.log(l_scr[...]), (tq, LANES))
            return carry

        lax.fori_loop(0, nq, q_tile, 0)

    def head(width):
        return pl.BlockSpec((t_all, width), lambda h: (0, h))

    return pl.pallas_call(
        body, name=name, grid=(n_heads,),
        in_specs=[head(HEAD_W), head(HEAD_W), pl.BlockSpec((t_all, LANES), lambda h: (0, 0))],
        out_specs=[head(V_HEAD), pl.BlockSpec((None, t_all, LANES), lambda h: (h, 0, 0))],
        out_shape=[jax.ShapeDtypeStruct((t_all, n_heads * V_HEAD), BF16),
                   jax.ShapeDtypeStruct((n_heads, t_all, LANES), F32)],
        scratch_shapes=[pltpu.VMEM((t_all, HEAD_W), BF16), pltpu.VMEM((tq, 1), F32), pltpu.VMEM((tq, 1), F32),
                        pltpu.VMEM((tq, V_HEAD), F32)],
        compiler_params=_params(("parallel",)),
    )(q, kv, kr)


def _attn_bwd(q, kv, kr, o, lse, do, cos, sin, *, name, n_heads, tq, n_real, scale):
    t_all = q.shape[0]
    nq = (n_real - N_META) // tq
    assert N_META + nq * tq == n_real and tq % CHUNK == 0 and t_all >= LANES
    n_pad = t_all - n_real

    def body(q_ref, kv_ref, kr_ref, o_ref, lse_ref, do_ref, cos_ref, sin_ref, dq_ref, dkv_ref, dkr_ref,
             k_scr, dk_scr, dv_scr, dq_scr):
        k_scr[:, :QK_NOPE] = kv_ref[:, :QK_NOPE]
        k_scr[:, QK_NOPE:] = kr_ref[...]
        dk_scr[...] = jnp.zeros(dk_scr.shape, F32)
        dv_scr[...] = jnp.zeros(dv_scr.shape, F32)
        if n_pad:
            dq_ref[pl.ds(n_real, n_pad), :] = jnp.zeros((n_pad, HEAD_W), dq_ref.dtype)

        def blocks(qt, dot, lse_t, delta, segments):
            kb = jnp.concatenate([k_scr[pl.ds(c0, w), :] for c0, w, _ in segments], axis=0)
            vb = jnp.concatenate([kv_ref[pl.ds(c0, w), QK_NOPE:] for c0, w, _ in segments], axis=0)
            s = lax.dot_general(qt, kb, _NT, preferred_element_type=F32) * scale
            p = jnp.exp(s - lse_t)
            if any(m is not None for _, _, m in segments):
                rows = qt.shape[0]
                mask = jnp.concatenate([jnp.ones((rows, w), jnp.bool_) if m is None else m for _, w, m in segments], axis=1)
                p = jnp.where(mask, p, 0.0)
            dp = lax.dot_general(dot, vb, _NT, preferred_element_type=F32)
            ds = (p * (dp - delta) * scale).astype(BF16)
            dv = lax.dot_general(p.astype(BF16), dot, _TN, preferred_element_type=F32)
            dk = lax.dot_general(ds, qt, _TN, preferred_element_type=F32)
            at = 0
            for c0, w, _ in segments:
                dv_scr[pl.ds(c0, w), :] += dv[at:at + w]
                dk_scr[pl.ds(c0, w), :] += dk[at:at + w]
                at += w
            return jnp.dot(ds, kb, preferred_element_type=F32)

        def block(qt, dot, lse_t, delta, c0, width, mask):
            return blocks(qt, dot, lse_t, delta, [(c0, width, mask)])

        def write_dq(r0, rows, dq):
            cs, sn = cos_ref[pl.ds(r0, rows), :], sin_ref[pl.ds(r0, rows), :]
            dq_ref[pl.ds(r0, rows), :] = jnp.concatenate(
                [dq[:, :QK_NOPE], _unrope(dq[:, QK_NOPE:], cs, sn)], axis=1).astype(dq_ref.dtype)

        rows_m = lax.broadcasted_iota(jnp.int32, (LANES, LANES), 0) < N_META
        dot = do_ref[pl.ds(0, LANES), :]
        delta = jnp.sum(dot.astype(F32) * o_ref[pl.ds(0, LANES), :].astype(F32), axis=-1, keepdims=True)
        dq = block(q_ref[pl.ds(0, LANES), :], dot, lse_ref[pl.ds(0, LANES), :1], delta, 0, LANES,
                   _meta_keys(LANES, LANES) & rows_m)
        write_dq(0, N_META, dq[:N_META])

        def q_tile(i, carry):
            r0 = pl.multiple_of(N_META + i * tq, N_META)
            qt = q_ref[pl.ds(r0, tq), :]
            dot = do_ref[pl.ds(r0, tq), :]
            lse_t = lse_ref[pl.ds(r0, tq), :1]
            delta = jnp.sum(dot.astype(F32) * o_ref[pl.ds(r0, tq), :].astype(F32), axis=-1, keepdims=True)
            dq_scr[...] = blocks(qt, dot, lse_t, delta, [(0, LANES, _meta_keys(tq, LANES)), (r0, tq, _chunk_causal(tq, tq))])

            def two_blocks(jj, c):
                c0 = pl.multiple_of(N_META + 2 * jj * tq, N_META)
                dq_scr[...] += block(qt, dot, lse_t, delta, c0, 2 * tq, None)
                return c

            lax.fori_loop(0, i // 2, two_blocks, 0)

            @pl.when(i % 2 == 1)
            def _():
                c0 = pl.multiple_of(N_META + (i - 1) * tq, N_META)
                dq_scr[...] += block(qt, dot, lse_t, delta, c0, tq, None)

            write_dq(r0, tq, dq_scr[...])
            return carry

        lax.fori_loop(0, nq, q_tile, 0)
        dk = dk_scr[...]
        dkv_ref[...] = jnp.concatenate([dk[:, :QK_NOPE], dv_scr[...]], axis=1).astype(dkv_ref.dtype)
        dkr_ref[...] = dk[:, QK_NOPE:]

    def head(width):
        return pl.BlockSpec((t_all, width), lambda h: (0, h))

    table = pl.BlockSpec((t_all, LANES), lambda h: (0, 0))
    per_head = pl.BlockSpec((None, t_all, LANES), lambda h: (h, 0, 0))
    return pl.pallas_call(
        body, name=name, grid=(n_heads,),
        in_specs=[head(HEAD_W), head(HEAD_W), table, head(V_HEAD), per_head, head(V_HEAD), table, table],
        out_specs=[head(HEAD_W), head(HEAD_W), per_head],
        out_shape=[jax.ShapeDtypeStruct((t_all, n_heads * HEAD_W), BF16), jax.ShapeDtypeStruct((t_all, n_heads * HEAD_W), BF16),
                   jax.ShapeDtypeStruct((n_heads, t_all, LANES), F32)],
        scratch_shapes=[pltpu.VMEM((t_all, HEAD_W), BF16), pltpu.VMEM((t_all, HEAD_W), F32), pltpu.VMEM((t_all, V_HEAD), F32),
                        pltpu.VMEM((tq, HEAD_W), F32)],
        compiler_params=_params(("parallel",)),
    )(q, kv, kr, o, lse, do, cos, sin)


LRU_ROWS = 128


def _shifted_back(ref, t0, rows, shift_max):
    main = ref[pl.ds(t0, rows), :]
    prev = ref[pl.ds(pl.multiple_of(jnp.maximum(t0 - SUBLANES, 0), SUBLANES), SUBLANES), :]
    prev = jnp.where(t0 > 0, prev, 0.0)
    ext = jnp.concatenate([prev, main], axis=0)
    return [main] + [pltpu.roll(ext, s, 0)[SUBLANES:, :] for s in range(1, shift_max + 1)]


def _shifted_ahead(ref, t0, rows, t_all, shift_max):
    main = ref[pl.ds(t0, rows), :]
    nxt = ref[pl.ds(pl.multiple_of(jnp.minimum(t0 + rows, t_all - SUBLANES), SUBLANES), SUBLANES), :]
    nxt = jnp.where(t0 + rows < t_all, nxt, 0.0)
    ext = jnp.concatenate([main, nxt], axis=0)
    return [main] + [pltpu.roll(ext, rows + SUBLANES - s, 0)[:rows, :] for s in range(1, shift_max + 1)]


def _conv_fwd(xp_ref, t0, rows, cw, cb):
    sh = _shifted_back(xp_ref, t0, rows, 3)
    out = cb + cw[3:4, :] * sh[0]
    for k in range(3):
        out = out + cw[k:k + 1, :] * sh[3 - k]
    return out, sh


def _lru_gates(xb, wga, bga, wgx, bgx, sp):
    xbb = xb.astype(BF16)
    r = _sigmoid(jnp.dot(xbb, wga, preferred_element_type=F32) + bga)
    ig = _sigmoid(jnp.dot(xbb, wgx, preferred_element_type=F32) + bgx)
    la = -LRU_C * r * sp
    a = jnp.exp(la)
    s = jnp.sqrt(_neg_expm1(2.0 * la))
    return xbb, r, ig, a, s


def _scan_tile(a, b, reverse):
    rows = a.shape[0]
    ridx = lax.broadcasted_iota(jnp.int32, a.shape, 0)
    s = 1
    while s < rows:
        if reverse:
            keep = ridx < rows - s
            a_sh, b_sh = pltpu.roll(a, rows - s, 0), pltpu.roll(b, rows - s, 0)
        else:
            keep = ridx >= s
            a_sh, b_sh = pltpu.roll(a, s, 0), pltpu.roll(b, s, 0)
        b = jnp.where(keep, a * b_sh + b, b)
        a = jnp.where(keep, a * a_sh, a)
        s *= 2
    return a, b


def _lru_fwd(xy, conv_w, conv_b, wga, bga, wgx, bgx, lam, *, name):
    t_all = xy.shape[0]
    dr = xy.shape[1] // 2
    c = LANES
    nblk = dr // c
    rows = LRU_ROWS
    nt = t_all // rows

    def body(xp_ref, yp_ref, cw_ref, cb_ref, wga_ref, bga_ref, wgx_ref, bgx_ref, lam_ref, hs_ref, hsy_ref):
        cw, cb = cw_ref[...], cb_ref[...]
        sp = _softplus_neg(lam_ref[...])

        def tile(t, h_in):
            t0 = pl.multiple_of(t * rows, rows)
            xb, _ = _conv_fwd(xp_ref, t0, rows, cw, cb)
            _, _, ig, a, s = _lru_gates(xb, wga_ref[0], bga_ref[...], wgx_ref[0], bgx_ref[...], sp)
            cum_a, h0 = _scan_tile(a, s * (ig * xb), reverse=False)
            hs = cum_a * h_in + h0
            hs_ref[pl.ds(t0, rows), :] = hs
            hsy_ref[pl.ds(t0, rows), :] = (hs * _gelu(yp_ref[pl.ds(t0, rows), :])).astype(BF16)
            return hs[rows - 1:, :]

        lax.fori_loop(0, nt, tile, jnp.zeros((1, c), F32))

    col = pl.BlockSpec((t_all, c), lambda b: (0, b))
    vec = pl.BlockSpec((1, c), lambda b: (0, b))
    wsp = pl.BlockSpec((1, c, c), lambda b: (b, 0, 0))
    return pl.pallas_call(
        body, name=name, grid=(nblk,),
        in_specs=[col, pl.BlockSpec((t_all, c), lambda b: (0, nblk + b)), pl.BlockSpec((4, c), lambda b: (0, b)), vec,
                  wsp, vec, wsp, vec, vec],
        out_specs=[col, col],
        out_shape=[jax.ShapeDtypeStruct((t_all, dr), F32), jax.ShapeDtypeStruct((t_all, dr), BF16)],
        compiler_params=_params(("parallel",)),
    )(xy, xy, conv_w, conv_b, wga, bga, wgx, bgx, lam)


def _lru_bwd(xy, hs, dhsy, conv_w, conv_b, wga, bga, wgx, bgx, lam, *, name):
    t_all = xy.shape[0]
    dr = xy.shape[1] // 2
    c = LANES
    nblk = dr // c
    rows = LRU_ROWS
    nt = t_all // rows

    def body(xp_ref, yp_ref, hs_ref, dh_ref, cw_ref, cb_ref, wga_ref, bga_ref, wgx_ref, bgx_ref, lam_ref,
             dxp_ref, dyp_ref, dcw_ref, dcb_ref, dwga_ref, dbga_ref, dwgx_ref, dbgx_ref, dlam_ref,
             xb_scr, r_scr, i_scr, a_scr):
        cw, cb = cw_ref[...], cb_ref[...]
        lamv = lam_ref[...]
        sp = _softplus_neg(lamv)
        sig_neg = 1.0 / (1.0 + jnp.exp(lamv))
        wga_v, wgx_v = wga_ref[0], wgx_ref[0]

        def recompute(t, carry):
            t0 = pl.multiple_of(t * rows, rows)
            xb, _ = _conv_fwd(xp_ref, t0, rows, cw, cb)
            _, r, ig, a, _ = _lru_gates(xb, wga_v, bga_ref[...], wgx_v, bgx_ref[...], sp)
            xb_scr[pl.ds(t0, rows), :] = xb
            r_scr[pl.ds(t0, rows), :] = r
            i_scr[pl.ds(t0, rows), :] = ig
            a_scr[pl.ds(t0, rows), :] = a
            return carry

        lax.fori_loop(0, nt, recompute, 0)
        dwga_ref[...] = jnp.zeros(dwga_ref.shape, F32)
        dwgx_ref[...] = jnp.zeros(dwgx_ref.shape, F32)

        def tile(ti, carry):
            lam_in, dbga, dbgx, dlam, dcw, dcb = carry
            t = nt - 1 - ti
            t0 = pl.multiple_of(t * rows, rows)
            a_now, a_next = _shifted_ahead(a_scr, t0, rows, t_all, 1)
            yp = yp_ref[pl.ds(t0, rows), :]
            dhy = dh_ref[pl.ds(t0, rows), :]
            cum_a, lam0 = _scan_tile(a_next, dhy * _gelu(yp), reverse=True)
            lam_t = cum_a * lam_in + lam0
            hs_now, hs_prev = _shifted_back(hs_ref, t0, rows, 1)
            da = lam_t * hs_prev
            xb = xb_scr[pl.ds(t0, rows), :]
            r = r_scr[pl.ds(t0, rows), :]
            ig = i_scr[pl.ds(t0, rows), :]
            la = -LRU_C * r * sp
            s = jnp.sqrt(_neg_expm1(2.0 * la))
            d_ixb = lam_t * s
            dla = da * a_now - (lam_t * ig * xb) * (a_now * a_now / s)
            dzr = dla * (-LRU_C * sp) * r * (1.0 - r)
            dzi = d_ixb * xb * ig * (1.0 - ig)
            dzr_b, dzi_b = dzr.astype(BF16), dzi.astype(BF16)
            xbb = xb.astype(BF16)
            dwga_ref[0] += lax.dot_general(xbb, dzr_b, _TN, preferred_element_type=F32)
            dwgx_ref[0] += lax.dot_general(xbb, dzi_b, _TN, preferred_element_type=F32)
            dxb = (d_ixb * ig + lax.dot_general(dzr_b, wga_v, _NT, preferred_element_type=F32)
                   + lax.dot_general(dzi_b, wgx_v, _NT, preferred_element_type=F32))
            xb_scr[pl.ds(t0, rows), :] = dxb
            dyp_ref[pl.ds(t0, rows), :] = (dhy * hs_now * _gelu_grad(yp)).astype(BF16)
            ahead = _shifted_ahead(xb_scr, t0, rows, t_all, 3)
            dxp = cw[3:4, :] * ahead[0]
            for k in range(3):
                dxp = dxp + cw[k:k + 1, :] * ahead[3 - k]
            dxp_ref[pl.ds(t0, rows), :] = dxp.astype(BF16)
            back = _shifted_back(xp_ref, t0, rows, 3)
            dcw_t = jnp.concatenate([jnp.sum(dxb * back[3 - k], axis=0, keepdims=True) for k in range(4)], axis=0)
            return (lam_t[:1, :], dbga + jnp.sum(dzr, axis=0, keepdims=True), dbgx + jnp.sum(dzi, axis=0, keepdims=True),
                    dlam + jnp.sum(dla * r, axis=0, keepdims=True), dcw + dcw_t, dcb + jnp.sum(dxb, axis=0, keepdims=True))

        zero = jnp.zeros((1, c), F32)
        _, dbga, dbgx, dlam, dcw, dcb = lax.fori_loop(0, nt, tile, (zero, zero, zero, zero, jnp.zeros((4, c), F32), zero))
        dbga_ref[...] = dbga
        dbgx_ref[...] = dbgx
        dlam_ref[...] = dlam * (LRU_C * sig_neg)
        dcw_ref[...] = dcw
        dcb_ref[...] = dcb

    col = pl.BlockSpec((t_all, c), lambda b: (0, b))
    col2 = pl.BlockSpec((t_all, c), lambda b: (0, nblk + b))
    vec = pl.BlockSpec((1, c), lambda b: (0, b))
    tap = pl.BlockSpec((4, c), lambda b: (0, b))
    wsp = pl.BlockSpec((1, c, c), lambda b: (b, 0, 0))
    vshape = jax.ShapeDtypeStruct((1, dr), F32)
    wshape = jax.ShapeDtypeStruct((nblk, c, c), F32)
    def planes_body(*refs):
        dxy_ref = refs[11]
        body(*refs[:11], dxy_ref.at[0], dxy_ref.at[1], *refs[12:])

    return pl.pallas_call(
        planes_body, name=name, grid=(nblk,),
        in_specs=[col, col2, col, col, tap, vec, wsp, vec, wsp, vec, vec],
        out_specs=[pl.BlockSpec((2, t_all, c), lambda b: (0, 0, b)), tap, vec, wsp, vec, wsp, vec, vec],
        out_shape=[jax.ShapeDtypeStruct((2, t_all, dr), BF16),
                   jax.ShapeDtypeStruct((4, dr), F32), vshape, wshape, vshape, wshape, vshape, vshape],
        scratch_shapes=[pltpu.VMEM((t_all, c), F32)] * 4,
        compiler_params=_params(("parallel",)),
    )(xy, xy, hs, dhsy, conv_w, conv_b, wga, bga, wgx, bgx, lam)


def _mesh_pos():
    return lax.axis_index("x"), lax.axis_index("y"), lax.axis_index("c")


def _all_gather(shards, *, name):
    n = len(shards)

    def body(*refs):
        ins, outs, token = refs[:n], refs[n:2 * n], refs[2 * n]
        send_sems, recv_sems, local_sems = refs[2 * n + 1:]
        token[...] = jnp.zeros(token.shape, token.dtype)
        x, y, c = _mesh_pos()
        me, sibling = (x, y, c), (x, y, 1 - c)
        chips = [(1 - x, y), (x, 1 - y), (1 - x, 1 - y)]
        slot = _slot

        def copy(a, k, block, to, src=None):
            dst = outs[a].at[slot(block)]
            return pltpu.make_async_remote_copy(
                src_ref=dst if src is None else src, dst_ref=dst, send_sem=send_sems.at[a, k],
                recv_sem=recv_sems.at[a, k], device_id=to, device_id_type=MESH)

        mine = [pltpu.make_async_copy(ins[a], outs[a].at[slot(me)], local_sems.at[a]) for a in range(n)]
        for cp in mine:
            cp.start()
        first = []
        for a in range(n):
            first.append(copy(a, 0, me, sibling, src=ins[a]))
            first += [copy(a, 1 + j, me, (*chip, c), src=ins[a]) for j, chip in enumerate(chips)]
        for cp in first:
            cp.start()
        passed = []
        for a in range(n):
            for j, chip in enumerate(chips):
                copy(a, 1 + j, (*chip, c), me).wait_recv()
                fwd = copy(a, 4 + j, (*chip, c), sibling)
                fwd.start()
                passed.append(fwd)
        for a in range(n):
            copy(a, 0, sibling, me).wait_recv()
            for j, chip in enumerate(chips):
                copy(a, 4 + j, (*chip, 1 - c), me).wait_recv()
        for cp in first + passed:
            cp.wait_send()
        for cp in mine:
            cp.wait()

    any_spec = pl.BlockSpec(memory_space=pl.ANY)
    outs = pl.pallas_call(
        body, name=name,
        in_specs=[any_spec] * n, out_specs=[any_spec] * n + [pl.BlockSpec(memory_space=pltpu.VMEM)],
        out_shape=[jax.ShapeDtypeStruct((N_DEV,) + s.shape, s.dtype) for s in shards]
        + [jax.ShapeDtypeStruct((SUBLANES, LANES), F32)],
        scratch_shapes=[pltpu.SemaphoreType.DMA((n, 7)), pltpu.SemaphoreType.DMA((n, 7)), pltpu.SemaphoreType.DMA((n,))],
    )(*shards)
    return list(outs[:n]), outs[n][0, 0]


_HBM = pl.BlockSpec(memory_space=pltpu.HBM)
_SEM = pl.BlockSpec(memory_space=pltpu.SEMAPHORE)
_ANY = pl.BlockSpec(memory_space=pl.ANY)
_EFFECT = pltpu.SideEffectType.DATAFLOW_SIDE_EFFECTING


def _slot(p):
    return 4 * p[0] + 2 * p[1] + p[2]


def _remote(src, dst, send, recv, idx, to):
    return pltpu.make_async_remote_copy(src_ref=src, dst_ref=dst, send_sem=send.at[idx], recv_sem=recv.at[idx],
                                        device_id=to, device_id_type=MESH)


def _ag_plan_own(a, src, land, send, recv):
    x, y, c = _mesh_pos()
    dst = land.at[_slot((x, y, c))]
    targets = [(x, y, 1 - c), (1 - x, y, c), (x, 1 - y, c), (1 - x, 1 - y, c)]
    return [_remote(src, dst, send, recv, 4 * a + k, to) for k, to in enumerate(targets)]


def _ag_plan_pass(a, src, land, send, recv):
    x, y, c = _mesh_pos()
    blocks = [land.at[_slot((px, py, c))] for px, py in ((1 - x, y), (x, 1 - y), (1 - x, 1 - y))]
    return [_remote(blk, blk, send, recv, 3 * a + k, (x, y, 1 - c)) for k, blk in enumerate(blocks)]


def _rs_plan_sibling(a, src, land, send, recv):
    x, y, c = _mesh_pos()
    return [_remote(src.at[2 * j + (1 - c)], land.at[j], send, recv, 4 * a + j, (x, y, 1 - c)) for j in range(4)]


def _rs_plan_chips(a, src, land, send, recv):
    x, y, c = _mesh_pos()
    out = []
    for k in (1, 2, 3):
        px = 1 - x if k & 2 else x
        py = 1 - y if k & 1 else y
        out.append(_remote(src.at[2 * px + py], land.at[k - 1], send, recv, 3 * a + k - 1, (px, py, c)))
    return out


def _in_hbm(a):
    return pltpu.with_memory_space_constraint(a, pltpu.HBM)


def _exchange_start(srcs, lands, plan, n_k, *, name):
    ns, n = len(srcs), len(lands)

    def body(*refs):
        src_refs, land_refs = refs[:ns], refs[ns:ns + n]
        send, recv = refs[ns + n], refs[ns + n + 1]
        token = refs[-1]
        for a in range(n):
            for cp in plan(a, src_refs[a] if ns else None, land_refs[a], send, recv):
                cp.start()
        token[...] = jnp.zeros(token.shape, token.dtype)

    bufs = list(srcs) + list(lands)
    outs = pl.pallas_call(
        body, name=name,
        out_shape=(pltpu.SemaphoreType.DMA((n * n_k,)), pltpu.SemaphoreType.DMA((n * n_k,)),
                   *[pltpu.HBM(b.shape, b.dtype) for b in bufs], jax.ShapeDtypeStruct((SUBLANES, LANES), F32)),
        in_specs=[_HBM] * (ns + n),
        out_specs=(_SEM, _SEM, *[_HBM] * (ns + n), pl.BlockSpec(memory_space=pltpu.VMEM)),
        input_output_aliases={i: 2 + i for i in range(ns + n)},
        compiler_params=pltpu.CompilerParams(has_side_effects=_EFFECT),
    )(*[_in_hbm(b) for b in bufs])
    return outs[0], outs[1], list(outs[2:2 + ns]), list(outs[2 + ns:2 + ns + n]), outs[-1]


def _exchange_wait(started, plan, after, *, name):
    send, recv, srcs, lands, _ = started
    ns, n = len(srcs), len(lands)

    def body(*refs):
        src_refs, land_refs = refs[:ns], refs[ns:ns + n]
        send_ref, recv_ref = refs[ns + n], refs[ns + n + 1]
        for a in range(n):
            for cp in plan(a, src_refs[a] if ns else None, land_refs[a], send_ref, recv_ref):
                cp.wait_send()
                cp.wait_recv()

    bufs = list(srcs) + list(lands)
    outs = pl.pallas_call(
        body, name=name,
        out_shape=tuple(pltpu.HBM(b.shape, b.dtype) for b in bufs),
        in_specs=[_HBM] * (ns + n) + [_SEM, _SEM, _ANY],
        out_specs=tuple([_HBM] * (ns + n)),
        input_output_aliases={i: i for i in range(ns + n)},
        compiler_params=pltpu.CompilerParams(has_side_effects=_EFFECT),
    )(*bufs, send, recv, after)
    return list(outs[:ns]), list(outs[ns:])


def _pair_add(grads, landed, core, *, name, tr):
    _, r_all, c_all = grads.shape

    def body(core_ref, g_ref, l_ref, o_ref):
        o_ref[...] = (g_ref[...].astype(F32) + l_ref[...].astype(F32)).astype(o_ref.dtype)

    return pl.pallas_call(
        body, name=name,
        grid_spec=pltpu.PrefetchScalarGridSpec(
            num_scalar_prefetch=1, grid=(4, r_all // tr),
            in_specs=[pl.BlockSpec((None, tr, c_all), lambda j, i, core_ref: (2 * j + core_ref[0], i, 0)),
                      pl.BlockSpec((None, tr, c_all), lambda j, i, core_ref: (j, i, 0))],
            out_specs=pl.BlockSpec((None, tr, c_all), lambda j, i, core_ref: (j, i, 0))),
        out_shape=jax.ShapeDtypeStruct((4, r_all, c_all), grads.dtype),
        compiler_params=_params(("parallel", "parallel")),
    )(core, grads, landed)


def _adamw_math(w, g, m, v):
    m2 = ADAM_B1 * m + (1.0 - ADAM_B1) * g
    v2 = ADAM_B2 * v + (1.0 - ADAM_B2) * (g * g)
    m_hat = m2 / (1.0 - ADAM_B1 ** ADAM_STEP)
    v_hat = v2 / (1.0 - ADAM_B2 ** ADAM_STEP)
    delta = -ADAM_LR * (m_hat / (jnp.sqrt(v_hat) + ADAM_EPS) + ADAM_WD * w)
    return delta, m2, v2


def _adamw(w, m, v, terms, order, *, name, tr, col_block=None, own=None, stack=None):
    r_all, c_all = w.shape
    n_slots = terms.shape[0]

    def body(*refs):
        if col_block is not None or own is not None:
            refs = refs[1:]
        own_ref = None
        if own is not None:
            own_ref, refs = refs[0], refs[1:]
        w_ref, m_ref, v_ref, t_ref, g_ref, d_ref, m2_ref, v2_ref = refs
        if own_ref is not None:
            g = own_ref[...].astype(F32) + t_ref[order[0]].astype(F32)
        else:
            g = t_ref[order[0]].astype(F32)
        for s in order[1:]:
            g = g + t_ref[s].astype(F32)
        delta, m2, v2 = _adamw_math(w_ref[...], g, m_ref[...], v_ref[...])
        g_ref[...] = g
        d_ref[...] = delta
        m2_ref[...] = m2
        v2_ref[...] = v2

    shape = jax.ShapeDtypeStruct((r_all, c_all), F32)
    if own is not None:
        layer, n_layers, prev = stack
        row = pl.BlockSpec((tr, c_all), lambda i, idx: (i, 0))
        slab = pl.BlockSpec((None, tr, c_all), lambda i, idx: (layer, i, 0))
        carried = [] if prev is None else list(prev)

        def stacked_body(*refs):
            body(*refs[:6], *refs[6 + len(carried):])

        return pl.pallas_call(
            stacked_body, name=name,
            grid_spec=pltpu.PrefetchScalarGridSpec(
                num_scalar_prefetch=1, grid=(r_all // tr,),
                in_specs=[pl.BlockSpec((None, tr, c_all), lambda i, idx: (idx[0], i, 0)), row, row, row,
                          pl.BlockSpec((n_slots, tr, c_all), lambda i, idx: (0, i, 0))] + [_ANY] * len(carried),
                out_specs=[slab] * 4),
            out_shape=[jax.ShapeDtypeStruct((n_layers, r_all, c_all), F32)] * 4,
            input_output_aliases={6 + k: k for k in range(len(carried))},
            compiler_params=_params(("parallel",)),
        )(own[1], own[0], w, m, v, terms, *carried)
    if col_block is None:
        row = pl.BlockSpec((tr, c_all), lambda i: (i, 0))
        return pl.pallas_call(
            body, name=name, grid=(r_all // tr,),
            in_specs=[row, row, row, pl.BlockSpec((n_slots, tr, c_all), lambda i: (0, i, 0))],
            out_specs=[row] * 4, out_shape=[shape] * 4, compiler_params=_params(("parallel",)),
        )(w, m, v, terms)
    row = pl.BlockSpec((tr, c_all), lambda i, blk: (i, 0))
    return pl.pallas_call(
        body, name=name,
        grid_spec=pltpu.PrefetchScalarGridSpec(
            num_scalar_prefetch=1, grid=(r_all // tr,),
            in_specs=[row, row, row, pl.BlockSpec((n_slots, tr, c_all), lambda i, blk: (0, i, blk[0]))],
            out_specs=[row] * 4),
        out_shape=[shape] * 4, compiler_params=_params(("parallel",)),
    )(col_block, w, m, v, terms)


def _rope_tables(t_all):
    pos = jnp.arange(t_all, dtype=F32)
    inv_freq = ROPE_THETA ** (-jnp.arange(0, QK_ROPE, 2, dtype=F32) / QK_ROPE)
    ang = pos[:, None] * inv_freq[None, :]
    cos, sin = jnp.cos(ang), jnp.sin(ang)
    return jnp.tile(cos, (1, LANES // (QK_ROPE // 2))), jnp.tile(sin, (1, LANES // (QK_ROPE // 2)))


def _adam_row_tile(r_all, c_all, block_bytes=512 * 1024):
    target = max(SUBLANES, block_bytes // (4 * c_all))
    return _pick(r_all, [t for t in (1024, 704, 512, 352, 256, 176, 128, 64, 32, 16, 8) if t <= target])


def _rows_natural(wg):
    return wg.reshape(wg.shape[0] * wg.shape[1], wg.shape[2])


def _mla_layer_fwd(tag, h, g_mix, ws, qn, kvn, cos, sin, *, tm, tq, n_heads, scale, n_real):
    w_in, w_uq, w_ukv, w_o = _rows_natural(ws[0]), ws[1], ws[2], _rows_natural(ws[3])
    t_all, d = h.shape
    lq, lkv = qn.shape[1], kvn.shape[1]
    tmb = _pick(t_all, _ROW_TILES)
    hn = _rmsnorm_fwd(h, g_mix, name=f"norm_mix{tag}", tm=tm)
    proj = _mm_nn(hn, w_in, name=f"mla_in{tag}", out_dtype=F32, tm=tmb, tn=w_in.shape[1], tk=_pick(d, _DIVS))
    cq, ckv, kr = _mla_prep_fwd(proj, qn, kvn, cos, sin, name=f"mla_prep{tag}", tm=tm, lq=lq, lkv=lkv)
    q = _mm_nn(cq, w_uq, name=f"mla_q{tag}", out_dtype=BF16, tm=tmb, tn=w_uq.shape[2], tk=lq, b_blocked=True,
               epilogue=_rope_q_epilogue, extras=(cos, sin))
    kv = _mm_nn(ckv, w_ukv, name=f"mla_kv{tag}", out_dtype=BF16, tm=tmb, tn=w_ukv.shape[2], tk=lkv, b_blocked=True)
    o, lse = _attn_fwd(q, kv, kr, name=f"attn_fwd{tag}", n_heads=n_heads, tq=tq, n_real=n_real, scale=scale)
    h_mid = _mm_nn(o, w_o, name=f"mla_o{tag}", out_dtype=F32, tm=tm, tn=d, tk=o.shape[1], res=h)
    return h_mid, (hn, proj, cq, ckv, kr, q, kv, o, lse)


def _mla_layer_bwd(tag, dh, dh_b, h_in, saved, g_mix, ws, qn, kvn, cos, sin, *, tm, tq, n_heads, scale, n_real, early=None,
                   after=None):
    hn, proj, cq, ckv, kr, q, kv, o, lse = saved
    w_in, w_uq, w_ukv, w_o = _rows_natural(ws[0]), ws[1], ws[2], _rows_natural(ws[3])
    t_all, d = h_in.shape
    lq, lkv = qn.shape[1], kvn.shape[1]
    ov = o.shape[1]
    tmb = _pick(t_all, _ROW_TILES)
    tn_d, tk_d = _pick(d, _DIVS[1:]), _pick(d, _DIVS)
    do = _mm_nt(dh_b, w_o, name=f"mla_do{tag}", out_dtype=BF16, tm=tmb, tn=_pick(ov, _DIVS[1:]), tk=tk_d, after=after)
    dw_o = _mm_tn(o, dh_b, name=f"mla_dwo{tag}", out_dtype=BF16, tm=_pick(ov, _DIVS[2:]), tn=tn_d, tk=t_all)
    dq, dkv, dkr_h = _attn_bwd(q, kv, kr, o, lse, do, cos, sin, name=f"attn_bwd{tag}", n_heads=n_heads, tq=tq, n_real=n_real,
                               scale=scale)
    hw, kw = w_uq.shape[2], w_ukv.shape[2]
    dw_uq = _mm_tn(cq, dq, name=f"mla_dwuq{tag}", out_dtype=BF16, tm=lq, tn=hw, tk=t_all, out_block=hw)
    dcq = _mm_nt(dq, w_uq, name=f"mla_dcq{tag}", out_dtype=F32, tm=tm, tn=lq, tk=dq.shape[1], b_blocked=True)
    dw_ukv = _mm_tn(ckv, dkv, name=f"mla_dwukv{tag}", out_dtype=BF16, tm=lkv, tn=kw, tk=t_all, out_block=kw)
    dckv = _mm_nt(dkv, w_ukv, name=f"mla_dckv{tag}", out_dtype=F32, tm=tm, tn=lkv, tk=dkv.shape[1], b_blocked=True)
    first = [dw_uq, dw_ukv, dw_o.reshape(N_DEV, -1, d)]
    if early is not None:
        qn = qn + early(first)
    dproj, dqn, dkvn = _mla_prep_bwd(dcq, dckv, dkr_h, proj, qn, kvn, cos, sin, name=f"mla_prep_bwd{tag}", tm=tm, lq=lq, lkv=lkv)
    wc = w_in.shape[1]
    dw_in = _mm_tn(hn, dproj, name=f"mla_dwin{tag}", out_dtype=BF16, tm=_pick(d, _DIVS[2:]), tn=wc, tk=t_all)
    dhn = _mm_nt(dproj, w_in, name=f"mla_dhn{tag}", out_dtype=BF16, tm=tmb, tn=tn_d, tk=wc)
    dh, dh_b, dg = _rmsnorm_bwd(dhn, h_in, g_mix, dh, name=f"norm_mix_bwd{tag}", tm=tm)
    return dh, dh_b, dg, dqn, dkvn, [dw_in.reshape(N_DEV, -1, wc)] + ([] if early is not None else first)


def _lru_layer_fwd(tag, h, g_mix, ws, small, *, tm):
    w_lin, w_lo = ws[0], _rows_natural(ws[1])
    t_all, d = h.shape
    dr = w_lo.shape[0]
    tmb = _pick(t_all, _ROW_TILES)
    hn = _rmsnorm_fwd(h, g_mix, name=f"norm_mix{tag}", tm=tm)
    xy = _mm_nn(hn, w_lin, name=f"lru_in{tag}", out_dtype=F32, tm=tmb, tn=w_lin.shape[2], tk=_pick(d, _DIVS), b_blocked=True,
                rows_outer=True)
    hs, hsy = _lru_fwd(xy, *small, name=f"lru_fwd{tag}")
    h_mid = _mm_nn(hsy, w_lo, name=f"lru_o{tag}", out_dtype=F32, tm=tm, tn=d, tk=dr, res=h)
    return h_mid, (hn, xy, hs, hsy)


def _lru_layer_bwd(tag, dh, dh_b, h_in, saved, g_mix, ws, small, *, tm, after=None):
    hn, xy, hs, hsy = saved
    w_lin, w_lo = ws[0], _rows_natural(ws[1])
    t_all, d = h_in.shape
    dr = w_lo.shape[0]
    tmb = _pick(t_all, _ROW_TILES)
    tn_d, tk_d = _pick(d, _DIVS[1:]), _pick(d, _DIVS)
    dhsy = _mm_nt(dh_b, w_lo, name=f"lru_dhsy{tag}", out_dtype=F32, tm=tmb, tn=_pick(dr, _DIVS[1:]), tk=tk_d, after=after)
    dw_lo = _mm_tn(hsy, dh_b, name=f"lru_dwo{tag}", out_dtype=BF16, tm=_pick(dr, _DIVS[2:]), tn=tn_d, tk=t_all)
    dxy, *dsmall = _lru_bwd(xy, hs, dhsy, *small, name=f"lru_bwd{tag}")
    lw = w_lin.shape[2]
    dw_lin = _mm_tn(hn, dxy, name=f"lru_dwin{tag}", out_dtype=BF16, tm=tn_d, tn=lw, tk=t_all, out_block=lw)
    dhn = _mm_nt(dxy, w_lin, name=f"lru_dhn{tag}", out_dtype=BF16, tm=tm, tn=tn_d, tk=2 * dr, b_blocked=True)
    dh, dh_b, dg = _rmsnorm_bwd(dhn, h_in, g_mix, dh, name=f"norm_mix_bwd{tag}", tm=tm)
    return dh, dh_b, dg, tuple(dsmall), [dw_lin, dw_lo.reshape(N_DEV, -1, d)]


def _ffn_layer_fwd(tag, h_mid, g_ffn, ws, *, tm):
    w_gu, w_down = ws[0], _rows_natural(ws[1])
    t_all, d = h_mid.shape
    f_all = w_down.shape[0]
    tmb = _pick(t_all, _ROW_TILES)
    fk = _pick(f_all, (1408,) + _DIVS[1:])
    hn2 = _rmsnorm_fwd(h_mid, g_ffn, name=f"norm_ffn{tag}", tm=tm)
    gu, act = _ffn_up(hn2, w_gu, name=f"ffn_up{tag}", tm=_pick(t_all, (704, 384, 256, 128)))
    h_out = _mm_nn(act, w_down, name=f"ffn_down{tag}", out_dtype=F32, tm=tm, tn=_pick(d, _DIVS[1:]), tk=f_all, res=h_mid)
    return h_out, (hn2, gu, act)


def _ffn_layer_bwd(tag, dh, dh_b, h_mid, saved, g_ffn, ws, *, tm, after=None):
    hn2, gu, act = saved
    w_gu, w_down = ws[0], _rows_natural(ws[1])
    t_all, d = h_mid.shape
    f_all = w_down.shape[0]
    f_local = w_gu.shape[2]
    tmb = _pick(t_all, _ROW_TILES)
    fk = _pick(f_all, (1408,) + _DIVS[1:])
    tn_d, tk_d = _pick(d, _DIVS[1:]), _pick(d, _DIVS)
    dgu = _ffn_dact(dh_b, w_down, gu, name=f"ffn_dact{tag}", tm=_pick(t_all, (704, 384, 256, 128)), tn=f_local, after=after)
    dw_down = _mm_tn(act, dh_b, name=f"ffn_dwdown{tag}", out_dtype=BF16, tm=fk, tn=_pick(d, _DIVS[2:]), tk=t_all)
    dhn2 = _mm_nt(dgu, w_gu, name=f"ffn_dhn{tag}", out_dtype=BF16, tm=tm, tn=_pick(d, _DIVS[2:]), tk=2 * f_all, b_blocked=True)
    dw_gu = _mm_tn(hn2, dgu, name=f"ffn_dwgu{tag}", out_dtype=BF16, tm=_pick(d, _DIVS[2:]), tn=f_local, tk=t_all, out_block=f_local,
                   cols_outer=True)
    dh, dh_b, dg = _rmsnorm_bwd(dhn2, h_mid, g_ffn, dh, name=f"norm_ffn_bwd{tag}", tm=tm)
    return dh, dh_b, dg, [dw_gu, dw_down.reshape(N_DEV, -1, d)]


def kernel(x, meta_tokens, norm_mix, norm_ffn, norm_final, mla_w_in, mla_q_norm, mla_kv_norm, mla_w_uq, mla_w_ukv, mla_w_o, lru_w_in, lru_conv_w, lru_conv_b, lru_w_gate_a, lru_b_gate_a, lru_w_gate_x, lru_b_gate_x, lru_lambda, lru_w_o, ffn_w_gu, ffn_w_down, loss_target, m_meta_tokens, m_norm_mix, m_norm_ffn, m_norm_final, m_mla_w_in, m_mla_q_norm, m_mla_kv_norm, m_mla_w_uq, m_mla_w_ukv, m_mla_w_o, m_lru_w_in, m_lru_conv_w, m_lru_conv_b, m_lru_w_gate_a, m_lru_b_gate_a, m_lru_w_gate_x, m_lru_b_gate_x, m_lru_lambda, m_lru_w_o, m_ffn_w_gu, m_ffn_w_down, v_meta_tokens, v_norm_mix, v_norm_ffn, v_norm_final, v_mla_w_in, v_mla_q_norm, v_mla_kv_norm, v_mla_w_uq, v_mla_w_ukv, v_mla_w_o, v_lru_w_in, v_lru_conv_w, v_lru_conv_b, v_lru_w_gate_a, v_lru_b_gate_a, v_lru_w_gate_x, v_lru_b_gate_x, v_lru_lambda, v_lru_w_o, v_ffn_w_gu, v_ffn_w_down):
    seq, d = x.shape[1], x.shape[2]
    assert seq % CHUNK == 0
    n_real = N_META + seq
    t_all = -(-n_real // LANES) * LANES
    tm = _pick(t_all, (384, 256, 128))
    tq = _pick(seq, (512, 256, 128, 64))
    depth = norm_mix.shape[0]
    n_mla, n_lru = mla_w_in.shape[0], lru_w_in.shape[0]
    lq, lkv = mla_q_norm.shape[1], mla_kv_norm.shape[1]
    w_in_cols = lq + lkv + LANES
    heads_local = mla_w_uq.shape[2] // (QK_NOPE + QK_ROPE)
    n_heads = heads_local * N_DEV
    dr = lru_w_gate_a.shape[1] * lru_w_gate_a.shape[2]
    scale = (QK_NOPE + QK_ROPE) ** -0.5
    cx, cy, cc = _mesh_pos()
    core = jnp.reshape(cc, (1,)).astype(jnp.int32)
    my_slot = jnp.reshape(4 * cx + 2 * cy + cc, (1,)).astype(jnp.int32)

    def pad_cols(w, cols):
        return jnp.pad(w, ((0, 0), (0, cols - w.shape[1])))

    def pad_heads(w):
        k_all = w.shape[0]
        w3 = w.reshape(k_all, heads_local, QK_NOPE + QK_ROPE)
        return jnp.pad(w3, ((0, 0), (0, 0), (0, HEAD_W - QK_NOPE - QK_ROPE))).reshape(k_all, heads_local * HEAD_W)

    def unpad_heads(w):
        k_all = w.shape[0]
        return w.reshape(k_all, heads_local, HEAD_W)[:, :, :QK_NOPE + QK_ROPE].reshape(k_all, -1)

    small_rows = N_META + n_lru * 4 + 2 * n_lru
    small_pad = -(-small_rows // SUBLANES) * SUBLANES

    def pack_small(meta, conv_w, conv_b, lam):
        rows = jnp.concatenate([meta, conv_w.reshape(n_lru * 4, -1), conv_b, lam], axis=0)
        return jnp.pad(rows, ((0, small_pad - small_rows), (0, 0)))

    def unpack_small(p):
        o1 = N_META + n_lru * 4
        return (p[:N_META], p[N_META:o1].reshape(n_lru, 4, -1), p[o1:o1 + n_lru], p[o1 + n_lru:o1 + 2 * n_lru])

    (small_full,), small_done = _all_gather([pack_small(meta_tokens, lru_conv_w, lru_conv_b, lru_lambda)], name="ag_small")
    small_full = jnp.transpose(small_full, (1, 0, 2)).reshape(small_pad, -1)
    meta_full, conv_w_full, conv_b_full, lam_full = unpack_small(small_full)

    def wire(w):
        return (w + small_done).astype(BF16)

    mla_shards, lru_shards, ffn_shards = [], [], []
    for j in range(n_mla):
        mla_shards.append([wire(pad_cols(mla_w_in[j], w_in_cols)), wire(pad_heads(mla_w_uq[j])), wire(mla_w_ukv[j]),
                           wire(mla_w_o[j])])
    for j in range(n_lru):
        lru_shards.append([wire(lru_w_in[j]), wire(lru_w_o[j])])
    for layer in range(depth):
        ffn_shards.append([wire(ffn_w_gu[layer]), wire(ffn_w_down[layer])])

    n_sub = 2 * depth
    groups = []
    for layer in range(depth):
        groups += [mla_shards[layer // 2] if layer % 2 == 0 else lru_shards[layer // 2], ffn_shards[layer]]
    slot_idx = 4 * cx + 2 * cy + cc
    ag_own = []
    for gi, shards in enumerate(groups):
        lands = [lax.dynamic_update_slice(lax.empty((N_DEV,) + s.shape, s.dtype), s[None], (slot_idx, 0, 0)) for s in shards]
        ag_own.append(_exchange_start(shards, lands, _ag_plan_own, 4, name=f"ag{gi}_start"))
    ag_pass = [None] * n_sub
    weights = [None] * n_sub

    def ag_landed(gi, after):
        _, lands = _exchange_wait(ag_own[gi], _ag_plan_own, after, name=f"ag{gi}_wait")
        ag_pass[gi] = _exchange_start([], lands, _ag_plan_pass, 3, name=f"ag{gi}_pass")
        return ag_pass[gi][4][0, 0]

    def ag_done(gi, after):
        _, weights[gi] = _exchange_wait(ag_pass[gi], _ag_plan_pass, after, name=f"ag{gi}_pass_wait")

    cos, sin = _rope_tables(t_all)
    zeros_tail = jnp.zeros((t_all - n_real, d), F32)
    started = ag_own[0][4][0, 0]
    for st in ag_own[1:]:
        started = started + st[4][0, 0]
    h = jnp.concatenate([meta_full + started, x[0], zeros_tail], axis=0)
    target = jnp.concatenate([jnp.zeros((N_META, d), F32), loss_target[0], zeros_tail], axis=0)

    attn_kw = dict(tm=tm, tq=tq, n_heads=n_heads, scale=scale, n_real=n_real)

    def lru_small(j):
        return (conv_w_full[j], conv_b_full[j][None, :], lru_w_gate_a[j].astype(BF16), lru_b_gate_a[j].reshape(1, dr),
                lru_w_gate_x[j].astype(BF16), lru_b_gate_x[j].reshape(1, dr), lam_full[j][None, :])

    def before_sublayer(k, act):
        tok = ag_landed(k, act) if k <= 1 else 0.0
        ag_done(k, act)
        if 1 <= k < n_sub - 1:
            tok = tok + ag_landed(k + 1, act)
        return tok

    saved = []
    for layer in range(depth):
        j = layer // 2
        g_mix = norm_mix[layer][None, :] + before_sublayer(2 * layer, h)
        if layer % 2 == 0:
            h_mid, mix_saved = _mla_layer_fwd(layer, h, g_mix, weights[2 * layer], mla_q_norm[j][None, :],
                                              mla_kv_norm[j][None, :], cos, sin, **attn_kw)
        else:
            h_mid, mix_saved = _lru_layer_fwd(layer, h, g_mix, weights[2 * layer], lru_small(j), tm=tm)
        g_ffn = norm_ffn[layer][None, :] + before_sublayer(2 * layer + 1, h_mid)
        h_out, ffn_saved = _ffn_layer_fwd(layer, h_mid, g_ffn, weights[2 * layer + 1], tm=tm)
        saved.append((h, h_mid, mix_saved, ffn_saved))
        h = h_out

    loss_part, dh, dh_b, dg_final = _loss_head(h, target, norm_final[None, :], name="loss_head", tm=tm, n_real=n_real)
    loss = lax.psum(loss_part[0, 0], ("x", "y", "c"))

    rs_sib, rs_chip, reduced = [None] * (n_sub + 1), [None] * (n_sub + 1), [None] * (n_sub + 1)
    chip_idx = jnp.reshape(2 * cx + cy, (1,)).astype(jnp.int32)

    def rs_begin(k, grads):
        lands = [lax.empty((4,) + g.shape[1:], g.dtype) for g in grads]
        rs_sib[k] = _exchange_start(grads, lands, _rs_plan_sibling, 4, name=f"rs{k}_start")
        return rs_sib[k][4]

    def rs_middle(k, after):
        grads, landed = _exchange_wait(rs_sib[k], _rs_plan_sibling, after, name=f"rs{k}_wait")
        parts = [_pair_add(g, l, core, name=f"rs{k}_add{a}", tr=_adam_row_tile(g.shape[1], g.shape[2], 4 * 1024 * 1024))
                 for a, (g, l) in enumerate(zip(grads, landed))]
        lands = [lax.empty((3,) + p.shape[1:], p.dtype) for p in parts]
        rs_chip[k] = _exchange_start(parts, lands, _rs_plan_chips, 3, name=f"rs{k}_chips")
        return rs_chip[k][4]

    def rs_end(k, after):
        reduced[k] = _exchange_wait(rs_chip[k], _rs_plan_chips, after, name=f"rs{k}_chips_wait")

    d_norm_mix, d_norm_ffn = [None] * depth, [None] * depth
    d_qn, d_kvn = [None] * n_mla, [None] * n_mla
    d_small = {k: [None] * n_lru for k in ("cw", "cb", "wga", "bga", "wgx", "bgx", "lam")}
    tok, waiting = None, None
    gate_own = [None] * n_lru
    for layer in reversed(range(depth)):
        j = layer // 2
        h_in, h_mid, mix_saved, ffn_saved = saved[layer]
        dh, dh_b, d_norm_ffn[layer], ffn_g = _ffn_layer_bwd(layer, dh, dh_b, h_mid, ffn_saved, norm_ffn[layer][None, :],
                                                            weights[2 * layer + 1], tm=tm, after=tok)
        tok = rs_begin(2 * layer + 1, ffn_g)
        if waiting is not None:
            tok = tok + rs_middle(waiting, dh)
        waiting = 2 * layer + 1
        if layer == 0:
            tok = tok + rs_middle(waiting, dh)
            waiting = None
        g_mix = norm_mix[layer][None, :]
        if layer % 2 == 0:
            early = (lambda g: (rs_begin(n_sub, g) + rs_middle(n_sub, g[0]))[0, 0]) if layer == 0 else None
            dh, dh_b, d_norm_mix[layer], d_qn[j], d_kvn[j], mix_g = _mla_layer_bwd(
                layer, dh, dh_b, h_in, mix_saved, g_mix, weights[2 * layer], mla_q_norm[j][None, :], mla_kv_norm[j][None, :],
                cos, sin, early=early, after=tok, **attn_kw)
            tok = rs_begin(2 * layer, mix_g)
        else:
            dh, dh_b, d_norm_mix[layer], dsmall, mix_g = _lru_layer_bwd(layer, dh, dh_b, h_in, mix_saved, g_mix,
                                                                        weights[2 * layer], lru_small(j), tm=tm, after=tok)
            for key, val in zip(("cw", "cb", "wga", "bga", "wgx", "bgx", "lam"), dsmall):
                d_small[key][j] = val
            gates = [d_small["wga"][j].reshape(-1, LANES), d_small["wgx"][j].reshape(-1, LANES)]
            gate_lands = [lax.dynamic_update_slice(lax.empty((N_DEV,) + g.shape, g.dtype), g[None], (slot_idx, 0, 0)) for g in gates]
            gate_own[j] = _exchange_start(gates, gate_lands, _ag_plan_own, 4, name=f"ag_gates{j}_start")
            tok = rs_begin(2 * layer, mix_g) + gate_own[j][4]
        if waiting is not None:
            tok = tok + rs_middle(waiting, dh)
        waiting = 2 * layer
    rs_middle(waiting, dh)

    grad_x = dh[N_META:n_real][None]

    d_meta = dh[:N_META]
    small_grad = pack_small(d_meta, jnp.stack(d_small["cw"], axis=0), jnp.concatenate(d_small["cb"], axis=0),
                            jnp.concatenate(d_small["lam"], axis=0))
    rep_grads = [
        jnp.concatenate(d_norm_mix, axis=0), jnp.concatenate(d_norm_ffn, axis=0), dg_final,
        jnp.concatenate(d_qn, axis=0), jnp.concatenate(d_kvn, axis=0),
        jnp.concatenate(d_small["bga"], axis=0), jnp.concatenate(d_small["bgx"], axis=0),
    ]
    small_srcs = [small_grad] + [jnp.pad(g, ((0, -g.shape[0] % SUBLANES), (0, 0))) for g in rep_grads]
    small_lands = [lax.dynamic_update_slice(lax.empty((N_DEV,) + s.shape, s.dtype), s[None], (slot_idx, 0, 0))
                   for s in small_srcs]
    small_own = _exchange_start(small_srcs, small_lands, _ag_plan_own, 4, name="ag_grads_start")

    res = {}

    def adam_sharded(nm, k, a, idx, n_layers, w, m, v):
        parts, landed = reduced[k]
        r_all, c_all = landed[a].shape[1], landed[a].shape[2]
        res[nm] = _adamw(w.reshape(r_all, c_all), m.reshape(r_all, c_all), v.reshape(r_all, c_all), landed[a], (0, 1, 2),
                         name=f"adamw_{nm}{idx}", tr=_adam_row_tile(r_all, c_all, 2 * 1024 * 1024), own=(parts[a], chip_idx),
                         stack=(idx, n_layers, res.get(nm)))

    after = small_own[4]
    for k in reversed(range(n_sub)):
        rs_end(k, after)
        if k == 0:
            rs_end(n_sub, after)
            reduced[0] = tuple(first + rest for first, rest in zip(reduced[0], reduced[n_sub]))
        layer, j = k // 2, k // 4
        if k % 2 == 1:
            adam_sharded("ffn_w_gu", k, 0, layer, depth, ffn_w_gu[layer], m_ffn_w_gu[layer], v_ffn_w_gu[layer])
            adam_sharded("ffn_w_down", k, 1, layer, depth, ffn_w_down[layer], m_ffn_w_down[layer], v_ffn_w_down[layer])
            after = res["ffn_w_down"][0]
        elif layer % 2 == 0:
            adam_sharded("mla_w_in", k, 0, j, n_mla, pad_cols(mla_w_in[j], w_in_cols), pad_cols(m_mla_w_in[j], w_in_cols),
                         pad_cols(v_mla_w_in[j], w_in_cols))
            adam_sharded("mla_w_uq", k, 1, j, n_mla, pad_heads(mla_w_uq[j]), pad_heads(m_mla_w_uq[j]), pad_heads(v_mla_w_uq[j]))
            adam_sharded("mla_w_ukv", k, 2, j, n_mla, mla_w_ukv[j], m_mla_w_ukv[j], v_mla_w_ukv[j])
            adam_sharded("mla_w_o", k, 3, j, n_mla, mla_w_o[j], m_mla_w_o[j], v_mla_w_o[j])
            after = res["mla_w_o"][0]
        else:
            adam_sharded("lru_w_in", k, 0, j, n_lru, lru_w_in[j], m_lru_w_in[j], v_lru_w_in[j])
            adam_sharded("lru_w_o", k, 1, j, n_lru, lru_w_o[j], m_lru_w_o[j], v_lru_w_o[j])
            after = res["lru_w_o"][0]
    res["mla_w_in"] = [t[:, :, :lq + lkv + QK_ROPE] for t in res["mla_w_in"]]
    res["mla_w_uq"] = [t.reshape(n_mla, lq, heads_local, HEAD_W)[:, :, :, :QK_NOPE + QK_ROPE].reshape(n_mla, lq, -1)
                       for t in res["mla_w_uq"]]

    _, small_lands = _exchange_wait(small_own, _ag_plan_own, after, name="ag_grads_wait")
    small_pass = _exchange_start([], small_lands, _ag_plan_pass, 3, name="ag_grads_pass")
    _, all_small = _exchange_wait(small_pass, _ag_plan_pass, after, name="ag_grads_pass_wait")
    gate_terms = []
    for j in range(n_lru):
        _, lands = _exchange_wait(gate_own[j], _ag_plan_own, after, name=f"ag_gates{j}_wait")
        gate_pass = _exchange_start([], lands, _ag_plan_pass, 3, name=f"ag_gates{j}_pass")
        gate_terms.append(_exchange_wait(gate_pass, _ag_plan_pass, after, name=f"ag_gates{j}_pass_wait")[1])
    wga_terms = jnp.concatenate([t[0] for t in gate_terms], axis=1)
    wgx_terms = jnp.concatenate([t[1] for t in gate_terms], axis=1)
    slot_order = tuple(range(N_DEV))

    def adam_rep(terms, w, m, v, tag):
        r_pad, c_all = terms.shape[1], terms.shape[2]

        def prep(t):
            t2 = t.reshape(-1, c_all)
            return jnp.pad(t2, ((0, r_pad - t2.shape[0]), (0, 0)))

        outs = _adamw(prep(w), prep(m), prep(v), terms, slot_order, name=f"adamw_{tag}", tr=_adam_row_tile(r_pad, c_all))
        n_rows = w.size // c_all
        return [o[:n_rows].reshape(w.shape) for o in outs]

    small_w = pack_small(meta_tokens, lru_conv_w, lru_conv_b, lru_lambda)
    small_m = pack_small(m_meta_tokens, m_lru_conv_w, m_lru_conv_b, m_lru_lambda)
    small_v = pack_small(v_meta_tokens, v_lru_conv_w, v_lru_conv_b, v_lru_lambda)
    small_out = _adamw(small_w, small_m, small_v, all_small[0], slot_order, name="adamw_small", tr=small_pad, col_block=my_slot)
    small_out = [unpack_small(o) for o in small_out]
    for idx, key in enumerate(("meta_tokens", "lru_conv_w", "lru_conv_b", "lru_lambda")):
        res[key] = [small_out[k][idx] for k in range(4)]

    res["norm_mix"] = adam_rep(all_small[1], norm_mix, m_norm_mix, v_norm_mix, "norm_mix")
    res["norm_ffn"] = adam_rep(all_small[2], norm_ffn, m_norm_ffn, v_norm_ffn, "norm_ffn")
    res["norm_final"] = adam_rep(all_small[3], norm_final, m_norm_final, v_norm_final, "norm_final")
    res["mla_q_norm"] = adam_rep(all_small[4], mla_q_norm, m_mla_q_norm, v_mla_q_norm, "mla_q_norm")
    res["mla_kv_norm"] = adam_rep(all_small[5], mla_kv_norm, m_mla_kv_norm, v_mla_kv_norm, "mla_kv_norm")
    res["lru_w_gate_a"] = adam_rep(wga_terms, lru_w_gate_a, m_lru_w_gate_a, v_lru_w_gate_a, "lru_w_gate_a")
    res["lru_b_gate_a"] = adam_rep(all_small[6], lru_b_gate_a, m_lru_b_gate_a, v_lru_b_gate_a, "lru_b_gate_a")
    res["lru_w_gate_x"] = adam_rep(wgx_terms, lru_w_gate_x, m_lru_w_gate_x, v_lru_w_gate_x, "lru_w_gate_x")
    res["lru_b_gate_x"] = adam_rep(all_small[7], lru_b_gate_x, m_lru_b_gate_x, v_lru_b_gate_x, "lru_b_gate_x")

    names = ["meta_tokens", "norm_mix", "norm_ffn", "norm_final", "mla_w_in", "mla_q_norm", "mla_kv_norm", "mla_w_uq",
             "mla_w_ukv", "mla_w_o", "lru_w_in", "lru_conv_w", "lru_conv_b", "lru_w_gate_a", "lru_b_gate_a", "lru_w_gate_x",
             "lru_b_gate_x", "lru_lambda", "lru_w_o", "ffn_w_gu", "ffn_w_down"]
    shapes = dict(meta_tokens=meta_tokens, norm_mix=norm_mix, norm_ffn=norm_ffn, norm_final=norm_final, mla_w_in=mla_w_in,
                  mla_q_norm=mla_q_norm, mla_kv_norm=mla_kv_norm, mla_w_uq=mla_w_uq, mla_w_ukv=mla_w_ukv, mla_w_o=mla_w_o,
                  lru_w_in=lru_w_in, lru_conv_w=lru_conv_w, lru_conv_b=lru_conv_b, lru_w_gate_a=lru_w_gate_a,
                  lru_b_gate_a=lru_b_gate_a, lru_w_gate_x=lru_w_gate_x, lru_b_gate_x=lru_b_gate_x, lru_lambda=lru_lambda,
                  lru_w_o=lru_w_o, ffn_w_gu=ffn_w_gu, ffn_w_down=ffn_w_down)
    outs = [loss, grad_x]
    for k in range(4):
        outs += [res[nm][k].reshape(shapes[nm].shape) for nm in names]
    return tuple(outs)
```

```python
import math

import jax
import jax.numpy as jnp
from jax import lax
from jax.experimental import pallas as pl
from jax.experimental.pallas import tpu as pltpu

F32 = jnp.float32
BF16 = jnp.bfloat16
MESH = pl.DeviceIdType.MESH

N_META = 16
CHUNK = 64
QK_NOPE = 128
QK_ROPE = 64
V_HEAD = 128
HEAD_W = 256
ROPE_THETA = 10000.0
LRU_C = 8.0
RMS_EPS = 1e-6
NEG_BIG = -1e30
ADAM_LR, ADAM_B1, ADAM_B2, ADAM_EPS, ADAM_WD, ADAM_STEP = 0.001, 0.9, 0.999, 1e-08, 0.01, 10

LANES = 128
SUBLANES = 8
VMEM_LIMIT_BYTES = 52 * 1024 * 1024
N_DEV = 8

_NT = (((1,), (1,)), ((), ()))
_TN = (((0,), (0,)), ((), ()))
_DIVS = (2048, 1024, 512, 256, 128)
_ROW_TILES = (1408, 1024, 512, 256, 128)


def _params(dims):
    return pltpu.CompilerParams(dimension_semantics=dims, vmem_limit_bytes=VMEM_LIMIT_BYTES)


def _pick(n, candidates):
    for c in candidates:
        if c <= n and n % c == 0:
            return c
    return n


def _sigmoid(z):
    return 0.5 + 0.5 * jnp.tanh(0.5 * z)


def _gelu(x):
    c = math.sqrt(2.0 / math.pi)
    return 0.5 * x * (1.0 + jnp.tanh(c * (x + 0.044715 * x * x * x)))


def _gelu_grad(x):
    c = math.sqrt(2.0 / math.pi)
    th = jnp.tanh(c * (x + 0.044715 * x * x * x))
    return 0.5 * (1.0 + th) + 0.5 * x * (1.0 - th * th) * c * (1.0 + 3.0 * 0.044715 * x * x)


def _neg_expm1(x):
    poly = -x * (1.0 + x * (1.0 / 2.0) * (1.0 + x * (1.0 / 3.0) * (1.0 + x * (1.0 / 4.0) * (
        1.0 + x * (1.0 / 5.0) * (1.0 + x * (1.0 / 6.0) * (1.0 + x * (1.0 / 7.0)))))))
    return jnp.where(x > -0.25, poly, 1.0 - jnp.exp(x))


def _softplus_neg(lam):
    e = jnp.exp(-jnp.abs(lam))
    log1p = jnp.where(e > 1e-4, jnp.log(1.0 + e), e * (1.0 - e * (0.5 - e * (1.0 / 3.0))))
    return jnp.maximum(-lam, 0.0) + log1p


def _rot_half(x):
    lane = lax.broadcasted_iota(jnp.int32, x.shape, 1)
    first = (lane % QK_ROPE) < (QK_ROPE // 2)
    return jnp.where(first, -pltpu.roll(x, LANES - QK_ROPE // 2, 1), pltpu.roll(x, QK_ROPE // 2, 1))


def _rope(x, cos, sin):
    return x * cos + _rot_half(x) * sin


def _unrope(g, cos, sin):
    return g * cos - _rot_half(g) * sin


def _grid_order(rows_outer):
    if not rows_outer:
        return lambda f: f
    return lambda f: (lambda i, j, k: f(j, i, k))


def _mm_nn(a, b, *, name, out_dtype, tm, tn, tk, b_blocked=False, res=None, epilogue=None, extras=(), rows_outer=False):
    m_all, k_all = a.shape
    om = _grid_order(rows_outer)
    if b_blocked:
        g_all, kb, nb = b.shape
        n_all = g_all * nb
        assert nb % tn == 0
        r = nb // tn
        b_spec = pl.BlockSpec((None, tk, tn), om(lambda j, i, k: (j // r, k, j % r)))
    else:
        kb, n_all = b.shape
        b_spec = pl.BlockSpec((tk, tn), om(lambda j, i, k: (k, j)))
    assert kb == k_all and m_all % tm == 0 and n_all % tn == 0 and k_all % tk == 0
    nm, nn, nk = m_all // tm, n_all // tn, k_all // tk
    in_specs = [pl.BlockSpec((tm, tk), om(lambda j, i, k: (i, k))), b_spec]
    operands = [a, b]
    has_res = res is not None
    if has_res:
        in_specs.append(pl.BlockSpec((tm, tn), om(lambda j, i, k: (i, j))))
        operands.append(res)
    for e in extras:
        in_specs.append(pl.BlockSpec((tm, e.shape[1]), om(lambda j, i, k: (i, 0))))
        operands.append(e)
    n_ex = len(extras)

    def body(*refs):
        a_ref, b_ref = refs[0], refs[1]
        pos = 2
        res_ref = None
        if has_res:
            res_ref = refs[pos]
            pos += 1
        ex_refs = refs[pos:pos + n_ex]
        pos += n_ex
        o_ref = refs[pos]
        acc_ref = refs[pos + 1] if nk > 1 else None

        def finish(acc):
            if has_res:
                acc = acc + res_ref[...]
            if epilogue is not None:
                acc = epilogue(acc, *ex_refs)
            o_ref[...] = acc.astype(o_ref.dtype)

        prod = jnp.dot(a_ref[...], b_ref[...], preferred_element_type=F32)
        if nk == 1:
            finish(prod)
        else:
            k = pl.program_id(2)

            @pl.when(k == 0)
            def _():
                acc_ref[...] = prod

            @pl.when(k > 0)
            def _():
                acc_ref[...] += prod

            @pl.when(k == nk - 1)
            def _():
                finish(acc_ref[...])

    return pl.pallas_call(
        body, name=name, grid=(nm, nn, nk) if rows_outer else (nn, nm, nk), in_specs=in_specs,
        out_specs=pl.BlockSpec((tm, tn), om(lambda j, i, k: (i, j))),
        out_shape=jax.ShapeDtypeStruct((m_all, n_all), out_dtype),
        scratch_shapes=[pltpu.VMEM((tm, tn), F32)] if nk > 1 else [],
        compiler_params=_params(("parallel", "parallel", "arbitrary")),
    )(*operands)


def _mm_nt(a, b, *, name, out_dtype, tm, tn, tk, b_blocked=False, after=None):
    if a.ndim == 3:
        n_planes, m_all, kp = a.shape
        k_all = n_planes * kp
    else:
        n_planes, (m_all, k_all) = 0, a.shape
    if b_blocked and tk == k_all and b.shape[0] > 1:
        g_all, n_all, nb = b.shape
        assert g_all * nb == k_all and m_all % tm == 0 and n_all % tn == 0
        per_plane = kp // nb if n_planes else 0

        def whole_body(a_ref, b_ref, o_ref):
            acc = None
            for g in range(g_all):
                a_g = a_ref[g // per_plane, :, (g % per_plane) * nb:(g % per_plane + 1) * nb] if n_planes else a_ref[:, g * nb:(g + 1) * nb]
                prod = lax.dot_general(a_g, b_ref[g], _NT, preferred_element_type=F32)
                acc = prod if acc is None else acc + prod
            o_ref[...] = acc.astype(o_ref.dtype)

        a_whole = (pl.BlockSpec((n_planes, tm, kp), lambda j, i: (0, i, 0)) if n_planes
                   else pl.BlockSpec((tm, k_all), lambda j, i: (i, 0)))
        return pl.pallas_call(
            whole_body, name=name, grid=(n_all // tn, m_all // tm),
            in_specs=[a_whole, pl.BlockSpec((g_all, tn, nb), lambda j, i: (0, j, 0))],
            out_specs=pl.BlockSpec((tm, tn), lambda j, i: (i, j)),
            out_shape=jax.ShapeDtypeStruct((m_all, n_all), out_dtype),
            compiler_params=_params(("parallel", "parallel")),
        )(a, b)
    if n_planes:
        assert kp % tk == 0
        rp = kp // tk
        a_spec = pl.BlockSpec((None, tm, tk), lambda j, i, k: (k // rp, i, k % rp))
    else:
        a_spec = pl.BlockSpec((tm, tk), lambda j, i, k: (i, k))
    if b_blocked:
        g_all, n_all, nb = b.shape
        assert g_all * nb == k_all and nb % tk == 0
        r = nb // tk
        b_spec = pl.BlockSpec((None, tn, tk), lambda j, i, k: (k // r, j, k % r))
    else:
        n_all, kb = b.shape
        assert kb == k_all
        b_spec = pl.BlockSpec((tn, tk), lambda j, i, k: (j, k))
    assert m_all % tm == 0 and n_all % tn == 0 and k_all % tk == 0
    nm, nn, nk = m_all // tm, n_all // tn, k_all // tk

    def body(a_ref, b_ref, o_ref, *scratch):
        prod = lax.dot_general(a_ref[...], b_ref[...], _NT, preferred_element_type=F32)
        if nk == 1:
            o_ref[...] = prod.astype(o_ref.dtype)
        else:
            acc_ref = scratch[0]
            k = pl.program_id(2)

            @pl.when(k == 0)
            def _():
                acc_ref[...] = prod

            @pl.when(k > 0)
            def _():
                acc_ref[...] += prod

            @pl.when(k == nk - 1)
            def _():
                o_ref[...] = acc_ref[...].astype(o_ref.dtype)

    ordered = [] if after is None else [after]

    def ordered_body(a_ref, b_ref, *rest):
        body(a_ref, b_ref, *rest[len(ordered):])

    return pl.pallas_call(
        ordered_body, name=name, grid=(nn, nm, nk),
        in_specs=[a_spec, b_spec] + [pl.BlockSpec(memory_space=pl.ANY)] * len(ordered),
        out_specs=pl.BlockSpec((tm, tn), lambda j, i, k: (i, j)),
        out_shape=jax.ShapeDtypeStruct((m_all, n_all), out_dtype),
        scratch_shapes=[pltpu.VMEM((tm, tn), F32)] if nk > 1 else [],
        compiler_params=_params(("parallel", "parallel", "arbitrary")),
    )(a, b, *ordered)


def _mm_tn(a, b, *, name, out_dtype, tm, tn, tk, out_block=None, cols_outer=False):
    t_all, m_all = a.shape
    om = _grid_order(cols_outer)
    if b.ndim == 3:
        n_planes, tb, n_p = b.shape
        assert n_p % tn == 0
        rq = n_p // tn
        n_all = n_planes * n_p
        b_spec = pl.BlockSpec((None, tk, tn), om(lambda i, j, k: (j // rq, k, j % rq)))
    else:
        tb, n_all = b.shape
        b_spec = pl.BlockSpec((tk, tn), om(lambda i, j, k: (k, j)))
    assert tb == t_all and m_all % tm == 0 and n_all % tn == 0 and t_all % tk == 0
    nm, nn, nk = m_all // tm, n_all // tn, t_all // tk
    if out_block is None:
        out_shape = jax.ShapeDtypeStruct((m_all, n_all), out_dtype)
        out_spec = pl.BlockSpec((tm, tn), om(lambda i, j, k: (i, j)))
    else:
        assert out_block % tn == 0 and n_all % out_block == 0
        r = out_block // tn
        out_shape = jax.ShapeDtypeStruct((n_all // out_block, m_all, out_block), out_dtype)
        out_spec = pl.BlockSpec((None, tm, tn), om(lambda i, j, k: (j // r, i, j % r)))

    def body(a_ref, b_ref, o_ref, *scratch):
        prod = lax.dot_general(a_ref[...], b_ref[...], _TN, preferred_element_type=F32)
        if nk == 1:
            o_ref[...] = prod.astype(o_ref.dtype)
        else:
            acc_ref = scratch[0]
            k = pl.program_id(2)

            @pl.when(k == 0)
            def _():
                acc_ref[...] = prod

            @pl.when(k > 0)
            def _():
                acc_ref[...] += prod

            @pl.when(k == nk - 1)
            def _():
                o_ref[...] = acc_ref[...].astype(o_ref.dtype)

    return pl.pallas_call(
        body, name=name, grid=(nn, nm, nk) if cols_outer else (nm, nn, nk),
        in_specs=[pl.BlockSpec((tk, tm), om(lambda i, j, k: (k, i))), b_spec],
        out_specs=out_spec, out_shape=out_shape,
        scratch_shapes=[pltpu.VMEM((tm, tn), F32)] if nk > 1 else [],
        compiler_params=_params(("parallel", "parallel", "arbitrary")),
    )(a, b)


def _rmsnorm_fwd(x, g, *, name, tm):
    t_all, d = x.shape

    def body(x_ref, g_ref, o_ref):
        xv = x_ref[...]
        rstd = lax.rsqrt(jnp.mean(xv * xv, axis=-1, keepdims=True) + RMS_EPS)
        o_ref[...] = (xv * rstd * g_ref[...]).astype(o_ref.dtype)

    return pl.pallas_call(
        body, name=name, grid=(t_all // tm,),
        in_specs=[pl.BlockSpec((tm, d), lambda i: (i, 0)), pl.BlockSpec((1, d), lambda i: (0, 0))],
        out_specs=pl.BlockSpec((tm, d), lambda i: (i, 0)),
        out_shape=jax.ShapeDtypeStruct((t_all, d), BF16),
        compiler_params=_params(("parallel",)),
    )(x, g)


def _rms_bwd_math(dy, xv, g):
    rstd = lax.rsqrt(jnp.mean(xv * xv, axis=-1, keepdims=True) + RMS_EPS)
    xhat = xv * rstd
    dxh = dy * g
    dx = rstd * (dxh - xhat * jnp.mean(dxh * xhat, axis=-1, keepdims=True))
    return dx, jnp.sum(dy * xhat, axis=0, keepdims=True)


def _rmsnorm_bwd(dy, x, g, res, *, name, tm):
    t_all, d = x.shape

    def body(dy_ref, x_ref, g_ref, res_ref, dx_ref, dxb_ref, dg_ref):
        dx, dg = _rms_bwd_math(dy_ref[...].astype(F32), x_ref[...], g_ref[...])
        tot = res_ref[...] + dx
        dx_ref[...] = tot
        dxb_ref[...] = tot.astype(BF16)

        @pl.when(pl.program_id(0) == 0)
        def _():
            dg_ref[...] = dg

        @pl.when(pl.program_id(0) > 0)
        def _():
            dg_ref[...] += dg

    row = pl.BlockSpec((tm, d), lambda i: (i, 0))
    vec = pl.BlockSpec((1, d), lambda i: (0, 0))
    return pl.pallas_call(
        body, name=name, grid=(t_all // tm,),
        in_specs=[row, row, vec, row], out_specs=[row, row, vec],
        out_shape=[jax.ShapeDtypeStruct((t_all, d), F32), jax.ShapeDtypeStruct((t_all, d), BF16),
                   jax.ShapeDtypeStruct((1, d), F32)],
        compiler_params=_params(("arbitrary",)),
    )(dy, x, g, res)


def _loss_head(h, target, g, *, name, tm, n_real):
    t_all, d = h.shape

    def body(h_ref, t_ref, g_ref, loss_ref, dx_ref, dxb_ref, dg_ref):
        i = pl.program_id(0)
        xv = h_ref[...]
        gv = g_ref[...]
        rstd = lax.rsqrt(jnp.mean(xv * xv, axis=-1, keepdims=True) + RMS_EPS)
        y = xv * rstd * gv
        row = i * tm + lax.broadcasted_iota(jnp.int32, (tm, 1), 0)
        valid = (row >= N_META) & (row < n_real)
        err = jnp.where(valid, y - t_ref[...], 0.0)
        part = 0.5 * jnp.sum(jnp.mean(err * err, axis=-1, keepdims=True), axis=0, keepdims=True)
        dx, dg = _rms_bwd_math(err * (1.0 / d), xv, gv)
        dx_ref[...] = dx
        dxb_ref[...] = dx.astype(BF16)

        @pl.when(i == 0)
        def _():
            dg_ref[...] = dg
            loss_ref[...] = jnp.broadcast_to(part, loss_ref.shape)

        @pl.when(i > 0)
        def _():
            dg_ref[...] += dg
            loss_ref[...] += jnp.broadcast_to(part, loss_ref.shape)

    row = pl.BlockSpec((tm, d), lambda i: (i, 0))
    vec = pl.BlockSpec((1, d), lambda i: (0, 0))
    return pl.pallas_call(
        body, name=name, grid=(t_all // tm,),
        in_specs=[row, row, vec],
        out_specs=[pl.BlockSpec((1, LANES), lambda i: (0, 0)), row, row, vec],
        out_shape=[jax.ShapeDtypeStruct((1, LANES), F32), jax.ShapeDtypeStruct((t_all, d), F32),
                   jax.ShapeDtypeStruct((t_all, d), BF16), jax.ShapeDtypeStruct((1, d), F32)],
        compiler_params=_params(("arbitrary",)),
    )(h, target, g)


def _ffn_up(x, w_gu, *, name, tm):
    t_all, d = x.shape
    g_all, kb, nb = w_gu.shape
    half = g_all // 2
    f = half * nb
    assert kb == d and t_all % tm == 0

    def body(x_ref, wg_ref, wu_ref, gu_ref, act_ref):
        xv = x_ref[...]
        gv = jnp.dot(xv, wg_ref[...], preferred_element_type=F32)
        uv = jnp.dot(xv, wu_ref[...], preferred_element_type=F32)
        gu_ref[0] = gv.astype(gu_ref.dtype)
        gu_ref[1] = uv.astype(gu_ref.dtype)
        act_ref[...] = (gv * _sigmoid(gv) * uv).astype(act_ref.dtype)

    return pl.pallas_call(
        body, name=name, grid=(half, t_all // tm),
        in_specs=[pl.BlockSpec((tm, d), lambda j, i: (i, 0)), pl.BlockSpec((None, d, nb), lambda j, i: (j, 0, 0)),
                  pl.BlockSpec((None, d, nb), lambda j, i: (j + half, 0, 0))],
        out_specs=[pl.BlockSpec((2, tm, nb), lambda j, i: (0, i, j)), pl.BlockSpec((tm, nb), lambda j, i: (i, j))],
        out_shape=[jax.ShapeDtypeStruct((2, t_all, f), BF16), jax.ShapeDtypeStruct((t_all, f), BF16)],
        compiler_params=_params(("parallel", "parallel")),
    )(x, w_gu, w_gu)


def _ffn_dact(dy, w_down, gu, *, name, tm, tn, after=None):
    t_all, d = dy.shape
    f = w_down.shape[0]
    assert t_all % tm == 0 and f % tn == 0
    ordered = [] if after is None else [after]

    def body(dy_ref, w_ref, gu_ref, *rest):
        o_ref = rest[-1]
        dact = lax.dot_general(dy_ref[...], w_ref[...], _NT, preferred_element_type=F32)
        gv, uv = gu_ref[0].astype(F32), gu_ref[1].astype(F32)
        sg = _sigmoid(gv)
        o_ref[0] = (dact * uv * (sg * (1.0 + gv * (1.0 - sg)))).astype(o_ref.dtype)
        o_ref[1] = (dact * gv * sg).astype(o_ref.dtype)

    return pl.pallas_call(
        body, name=name, grid=(f // tn, t_all // tm),
        in_specs=[pl.BlockSpec((tm, d), lambda j, i: (i, 0)), pl.BlockSpec((tn, d), lambda j, i: (j, 0)),
                  pl.BlockSpec((2, tm, tn), lambda j, i: (0, i, j))] + [pl.BlockSpec(memory_space=pl.ANY)] * len(ordered),
        out_specs=pl.BlockSpec((2, tm, tn), lambda j, i: (0, i, j)),
        out_shape=jax.ShapeDtypeStruct((2, t_all, f), BF16),
        compiler_params=_params(("parallel", "parallel")),
    )(dy, w_down, gu, *ordered)


def _mla_prep_fwd(proj, qn, kvn, cos, sin, *, name, tm, lq, lkv):
    t_all, w = proj.shape

    def body(p_ref, qn_ref, kvn_ref, cos_ref, sin_ref, cq_ref, ckv_ref, kr_ref):
        pv = p_ref[...]
        xq = pv[:, :lq]
        xkv = pv[:, lq:lq + lkv]
        cq_ref[...] = (xq * lax.rsqrt(jnp.mean(xq * xq, axis=-1, keepdims=True) + RMS_EPS) * qn_ref[...]).astype(BF16)
        ckv_ref[...] = (xkv * lax.rsqrt(jnp.mean(xkv * xkv, axis=-1, keepdims=True) + RMS_EPS) * kvn_ref[...]).astype(BF16)
        kr_ref[...] = _rope(pv[:, lq + lkv:], cos_ref[...], sin_ref[...]).astype(BF16)

    def row(width):
        return pl.BlockSpec((tm, width), lambda i: (i, 0))

    def vec(width):
        return pl.BlockSpec((1, width), lambda i: (0, 0))

    return pl.pallas_call(
        body, name=name, grid=(t_all // tm,),
        in_specs=[row(w), vec(lq), vec(lkv), row(LANES), row(LANES)],
        out_specs=[row(lq), row(lkv), row(LANES)],
        out_shape=[jax.ShapeDtypeStruct((t_all, lq), BF16), jax.ShapeDtypeStruct((t_all, lkv), BF16),
                   jax.ShapeDtypeStruct((t_all, LANES), BF16)],
        compiler_params=_params(("parallel",)),
    )(proj, qn, kvn, cos, sin)


def _mla_prep_bwd(dcq, dckv, dkr_h, proj, qn, kvn, cos, sin, *, name, tm, lq, lkv):
    t_all, w = proj.shape
    n_heads = dkr_h.shape[0]

    def body(dcq_ref, dckv_ref, dkr_ref, p_ref, qn_ref, kvn_ref, cos_ref, sin_ref, dp_ref, dqn_ref, dkvn_ref):
        pv = p_ref[...]
        dxq, dqn = _rms_bwd_math(dcq_ref[...], pv[:, :lq], qn_ref[...])
        dxkv, dkvn = _rms_bwd_math(dckv_ref[...], pv[:, lq:lq + lkv], kvn_ref[...])
        dkr = dkr_ref[0]
        for hh in range(1, n_heads):
            dkr = dkr + dkr_ref[hh]
        dkr = _unrope(dkr, cos_ref[...], sin_ref[...])
        dp_ref[...] = jnp.concatenate([dxq, dxkv, dkr], axis=1).astype(BF16)

        @pl.when(pl.program_id(0) == 0)
        def _():
            dqn_ref[...] = dqn
            dkvn_ref[...] = dkvn

        @pl.when(pl.program_id(0) > 0)
        def _():
            dqn_ref[...] += dqn
            dkvn_ref[...] += dkvn

    def row(width):
        return pl.BlockSpec((tm, width), lambda i: (i, 0))

    def vec(width):
        return pl.BlockSpec((1, width), lambda i: (0, 0))

    return pl.pallas_call(
        body, name=name, grid=(t_all // tm,),
        in_specs=[row(lq), row(lkv), pl.BlockSpec((n_heads, tm, LANES), lambda i: (0, i, 0)), row(w),
                  vec(lq), vec(lkv), row(LANES), row(LANES)],
        out_specs=[row(w), vec(lq), vec(lkv)],
        out_shape=[jax.ShapeDtypeStruct((t_all, w), BF16), jax.ShapeDtypeStruct((1, lq), F32),
                   jax.ShapeDtypeStruct((1, lkv), F32)],
        compiler_params=_params(("arbitrary",)),
    )(dcq, dckv, dkr_h, proj, qn, kvn, cos, sin)


def _rope_q_epilogue(acc, cos_ref, sin_ref):
    parts = []
    for g in range(acc.shape[1] // LANES):
        blk = acc[:, g * LANES:(g + 1) * LANES]
        parts.append(_rope(blk, cos_ref[...], sin_ref[...]) if g % 2 == 1 else blk)
    return jnp.concatenate(parts, axis=1)


def _chunk_causal(rows, cols, row0=0):
    r = row0 + lax.broadcasted_iota(jnp.int32, (rows, cols), 0)
    c = lax.broadcasted_iota(jnp.int32, (rows, cols), 1)
    return (c >> 6) <= (r >> 6)


def _meta_keys(rows, cols):
    return lax.broadcasted_iota(jnp.int32, (rows, cols), 1) < N_META


def _attn_fwd(q, kv, kr, *, name, n_heads, tq, n_real, scale):
    t_all = q.shape[0]
    nq = (n_real - N_META) // tq
    assert N_META + nq * tq == n_real and tq % CHUNK == 0 and t_all >= LANES
    n_pad = t_all - n_real
    sub = tq // 2 if (tq // 2) % CHUNK == 0 else tq

    def body(q_ref, kv_ref, kr_ref, o_ref, lse_ref, k_scr, m_scr, l_scr, acc_scr):
        k_scr[:, :QK_NOPE] = kv_ref[:, :QK_NOPE]
        k_scr[:, QK_NOPE:] = kr_ref[...]
        if n_pad:
            o_ref[pl.ds(n_real, n_pad), :] = jnp.zeros((n_pad, V_HEAD), o_ref.dtype)
            lse_ref[pl.ds(n_real, n_pad), :] = jnp.zeros((n_pad, LANES), F32)

        def scores(qt, c0, width):
            return lax.dot_general(qt, k_scr[pl.ds(c0, width), :], _NT, preferred_element_type=F32) * scale

        def values(c0, width):
            return kv_ref[pl.ds(c0, width), QK_NOPE:]

        s = jnp.where(_meta_keys(LANES, LANES), scores(q_ref[pl.ds(0, LANES), :], 0, LANES), NEG_BIG)
        m = jnp.max(s, axis=-1, keepdims=True)
        p = jnp.exp(s - m)
        l = jnp.sum(p, axis=-1, keepdims=True)
        o_meta = jnp.dot(p.astype(BF16), values(0, LANES), preferred_element_type=F32) / l
        o_ref[pl.ds(0, N_META), :] = o_meta[:N_META].astype(o_ref.dtype)
        lse_ref[pl.ds(0, N_META), :] = jnp.broadcast_to((m + jnp.log(l))[:N_META], (N_META, LANES))

        parts = [(u * sub, sub) for u in range(tq // sub)]

        def accumulate(u0, s, vals):
            rows = pl.ds(u0, s.shape[0])
            m_prev = m_scr[rows, :]
            m_new = jnp.maximum(m_prev, jnp.max(s, axis=-1, keepdims=True))
            alpha = jnp.exp(m_prev - m_new)
            p = jnp.exp(s - m_new)
            l_scr[rows, :] = alpha * l_scr[rows, :] + jnp.sum(p, axis=-1, keepdims=True)
            acc_scr[rows, :] = alpha * acc_scr[rows, :] + jnp.dot(p.astype(BF16), vals, preferred_element_type=F32)
            m_scr[rows, :] = m_new

        def q_tile(i, carry):
            r0 = pl.multiple_of(N_META + i * tq, N_META)
            qts = [q_ref[pl.ds(r0 + u0, rows), :] for u0, rows in parts]
            m_scr[...] = jnp.full(m_scr.shape, NEG_BIG, F32)
            l_scr[...] = jnp.zeros(l_scr.shape, F32)
            acc_scr[...] = jnp.zeros(acc_scr.shape, F32)

            def full_blocks(j, width):
                c0 = pl.multiple_of(N_META + j * tq, N_META)
                for (u0, _), qt in zip(parts, qts):
                    accumulate(u0, scores(qt, c0, width), values(c0, width))

            def two_blocks(jj, c):
                full_blocks(2 * jj, 2 * tq)
                return c

            lax.fori_loop(0, i // 2, two_blocks, 0)

            @pl.when(i % 2 == 1)
            def _():
                full_blocks(i - 1, tq)

            for (u0, rows), qt in zip(parts, qts):
                width = u0 + rows
                s = jnp.concatenate([jnp.where(_meta_keys(rows, LANES), scores(qt, 0, LANES), NEG_BIG),
                                     jnp.where(_chunk_causal(rows, width, u0), scores(qt, r0, width), NEG_BIG)], axis=1)
                accumulate(u0, s, jnp.concatenate([values(0, LANES), values(r0, width)], axis=0))
            o_ref[pl.ds(r0, tq), :] = (acc_scr[...] / l_scr[...]).astype(o_ref.dtype)
            lse_ref[pl.ds(r0, tq), :] = jnp.broadcast_to(m_scr[...] + jnp.log(l_scr[...]), (tq, LANES))
            return carry

        lax.fori_loop(0, nq, q_tile, 0)

    def head(width):
        return pl.BlockSpec((t_all, width), lambda h: (0, h))

    return pl.pallas_call(
        body, name=name, grid=(n_heads,),
        in_specs=[head(HEAD_W), head(HEAD_W), pl.BlockSpec((t_all, LANES), lambda h: (0, 0))],
        out_specs=[head(V_HEAD), pl.BlockSpec((None, t_all, LANES), lambda h: (h, 0, 0))],
        out_shape=[jax.ShapeDtypeStruct((t_all, n_heads * V_HEAD), BF16),
                   jax.ShapeDtypeStruct((n_heads, t_all, LANES), F32)],
        scratch_shapes=[pltpu.VMEM((t_all, HEAD_W), BF16), pltpu.VMEM((tq, 1), F32), pltpu.VMEM((tq, 1), F32),
                        pltpu.VMEM((tq, V_HEAD), F32)],
        compiler_params=_params(("parallel",)),
    )(q, kv, kr)


def _attn_bwd(q, kv, kr, o, lse, do, cos, sin, *, name, n_heads, tq, n_real, scale):
    t_all = q.shape[0]
    nq = (n_real - N_META) // tq
    assert N_META + nq * tq == n_real and tq % CHUNK == 0 and t_all >= LANES
    n_pad = t_all - n_real

    def body(q_ref, kv_ref, kr_ref, o_ref, lse_ref, do_ref, cos_ref, sin_ref, dq_ref, dkv_ref, dkr_ref,
             k_scr, dk_scr, dv_scr, dq_scr):
        k_scr[:, :QK_NOPE] = kv_ref[:, :QK_NOPE]
        k_scr[:, QK_NOPE:] = kr_ref[...]
        dk_scr[...] = jnp.zeros(dk_scr.shape, F32)
        dv_scr[...] = jnp.zeros(dv_scr.shape, F32)
        if n_pad:
            dq_ref[pl.ds(n_real, n_pad), :] = jnp.zeros((n_pad, HEAD_W), dq_ref.dtype)

        def blocks(qt, dot, lse_t, delta, segments):
            kb = jnp.concatenate([k_scr[pl.ds(c0, w), :] for c0, w, _ in segments], axis=0)
            vb = jnp.concatenate([kv_ref[pl.ds(c0, w), QK_NOPE:] for c0, w, _ in segments], axis=0)
            s = lax.dot_general(qt, kb, _NT, preferred_element_type=F32) * scale
            p = jnp.exp(s - lse_t)
            if any(m is not None for _, _, m in segments):
                rows = qt.shape[0]
                mask = jnp.concatenate([jnp.ones((rows, w), jnp.bool_) if m is None else m for _, w, m in segments], axis=1)
                p = jnp.where(mask, p, 0.0)
            dp = lax.dot_general(dot, vb, _NT, preferred_element_type=F32)
            ds = (p * (dp - delta) * scale).astype(BF16)
            dv = lax.dot_general(p.astype(BF16), dot, _TN, preferred_element_type=F32)
            dk = lax.dot_general(ds, qt, _TN, preferred_element_type=F32)
            at = 0
            for c0, w, _ in segments:
                dv_scr[pl.ds(c0, w), :] += dv[at:at + w]
                dk_scr[pl.ds(c0, w), :] += dk[at:at + w]
                at += w
            return jnp.dot(ds, kb, preferred_element_type=F32)

        def block(qt, dot, lse_t, delta, c0, width, mask):
            return blocks(qt, dot, lse_t, delta, [(c0, width, mask)])

        def write_dq(r0, rows, dq):
            cs, sn = cos_ref[pl.ds(r0, rows), :], sin_ref[pl.ds(r0, rows), :]
            dq_ref[pl.ds(r0, rows), :] = jnp.concatenate(
                [dq[:, :QK_NOPE], _unrope(dq[:, QK_NOPE:], cs, sn)], axis=1).astype(dq_ref.dtype)

        rows_m = lax.broadcasted_iota(jnp.int32, (LANES, LANES), 0) < N_META
        dot = do_ref[pl.ds(0, LANES), :]
        delta = jnp.sum(dot.astype(F32) * o_ref[pl.ds(0, LANES), :].astype(F32), axis=-1, keepdims=True)
        dq = block(q_ref[pl.ds(0, LANES), :], dot, lse_ref[pl.ds(0, LANES), :1], delta, 0, LANES,
                   _meta_keys(LANES, LANES) & rows_m)
        write_dq(0, N_META, dq[:N_META])

        def q_tile(i, carry):
            r0 = pl.multiple_of(N_META + i * tq, N_META)
            qt = q_ref[pl.ds(r0, tq), :]
            dot = do_ref[pl.ds(r0, tq), :]
            lse_t = lse_ref[pl.ds(r0, tq), :1]
            delta = jnp.sum(dot.astype(F32) * o_ref[pl.ds(r0, tq), :].astype(F32), axis=-1, keepdims=True)
            dq_scr[...] = blocks(qt, dot, lse_t, delta, [(0, LANES, _meta_keys(tq, LANES)), (r0, tq, _chunk_causal(tq, tq))])

            def two_blocks(jj, c):
                c0 = pl.multiple_of(N_META + 2 * jj * tq, N_META)
                dq_scr[...] += block(qt, dot, lse_t, delta, c0, 2 * tq, None)
                return c

            lax.fori_loop(0, i // 2, two_blocks, 0)

            @pl.when(i % 2 == 1)
            def _():
                c0 = pl.multiple_of(N_META + (i - 1) * tq, N_META)
                dq_scr[...] += block(qt, dot, lse_t, delta, c0, tq, None)

            write_dq(r0, tq, dq_scr[...])
            return carry

        lax.fori_loop(0, nq, q_tile, 0)
        dk = dk_scr[...]
        dkv_ref[...] = jnp.concatenate([dk[:, :QK_NOPE], dv_scr[...]], axis=1).astype(dkv_ref.dtype)
        dkr_ref[...] = dk[:, QK_NOPE:]

    def head(width):
        return pl.BlockSpec((t_all, width), lambda h: (0, h))

    table = pl.BlockSpec((t_all, LANES), lambda h: (0, 0))
    per_head = pl.BlockSpec((None, t_all, LANES), lambda h: (h, 0, 0))
    return pl.pallas_call(
        body, name=name, grid=(n_heads,),
        in_specs=[head(HEAD_W), head(HEAD_W), table, head(V_HEAD), per_head, head(V_HEAD), table, table],
        out_specs=[head(HEAD_W), head(HEAD_W), per_head],
        out_shape=[jax.ShapeDtypeStruct((t_all, n_heads * HEAD_W), BF16), jax.ShapeDtypeStruct((t_all, n_heads * HEAD_W), BF16),
                   jax.ShapeDtypeStruct((n_heads, t_all, LANES), F32)],
        scratch_shapes=[pltpu.VMEM((t_all, HEAD_W), BF16), pltpu.VMEM((t_all, HEAD_W), F32), pltpu.VMEM((t_all, V_HEAD), F32),
                        pltpu.VMEM((tq, HEAD_W), F32)],
        compiler_params=_params(("parallel",)),
    )(q, kv, kr, o, lse, do, cos, sin)


LRU_ROWS = 128


def _shifted_back(ref, t0, rows, shift_max):
    main = ref[pl.ds(t0, rows), :]
    prev = ref[pl.ds(pl.multiple_of(jnp.maximum(t0 - SUBLANES, 0), SUBLANES), SUBLANES), :]
    prev = jnp.where(t0 > 0, prev, 0.0)
    ext = jnp.concatenate([prev, main], axis=0)
    return [main] + [pltpu.roll(ext, s, 0)[SUBLANES:, :] for s in range(1, shift_max + 1)]


def _shifted_ahead(ref, t0, rows, t_all, shift_max):
    main = ref[pl.ds(t0, rows), :]
    nxt = ref[pl.ds(pl.multiple_of(jnp.minimum(t0 + rows, t_all - SUBLANES), SUBLANES), SUBLANES), :]
    nxt = jnp.where(t0 + rows < t_all, nxt, 0.0)
    ext = jnp.concatenate([main, nxt], axis=0)
    return [main] + [pltpu.roll(ext, rows + SUBLANES - s, 0)[:rows, :] for s in range(1, shift_max + 1)]


def _conv_fwd(xp_ref, t0, rows, cw, cb):
    sh = _shifted_back(xp_ref, t0, rows, 3)
    out = cb + cw[3:4, :] * sh[0]
    for k in range(3):
        out = out + cw[k:k + 1, :] * sh[3 - k]
    return out, sh


def _lru_gates(xb, wga, bga, wgx, bgx, sp):
    xbb = xb.astype(BF16)
    r = _sigmoid(jnp.dot(xbb, wga, preferred_element_type=F32) + bga)
    ig = _sigmoid(jnp.dot(xbb, wgx, preferred_element_type=F32) + bgx)
    la = -LRU_C * r * sp
    a = jnp.exp(la)
    s = jnp.sqrt(_neg_expm1(2.0 * la))
    return xbb, r, ig, a, s


def _scan_tile(a, b, reverse):
    rows = a.shape[0]
    ridx = lax.broadcasted_iota(jnp.int32, a.shape, 0)
    s = 1
    while s < rows:
        if reverse:
            keep = ridx < rows - s
            a_sh, b_sh = pltpu.roll(a, rows - s, 0), pltpu.roll(b, rows - s, 0)
        else:
            keep = ridx >= s
            a_sh, b_sh = pltpu.roll(a, s, 0), pltpu.roll(b, s, 0)
        b = jnp.where(keep, a * b_sh + b, b)
        a = jnp.where(keep, a * a_sh, a)
        s *= 2
    return a, b


def _lru_fwd(xy, conv_w, conv_b, wga, bga, wgx, bgx, lam, *, name):
    t_all = xy.shape[0]
    dr = xy.shape[1] // 2
    c = LANES
    nblk = dr // c
    rows = LRU_ROWS
    nt = t_all // rows

    def body(xp_ref, yp_ref, cw_ref, cb_ref, wga_ref, bga_ref, wgx_ref, bgx_ref, lam_ref, hs_ref, hsy_ref):
        cw, cb = cw_ref[...], cb_ref[...]
        sp = _softplus_neg(lam_ref[...])

        def tile(t, h_in):
            t0 = pl.multiple_of(t * rows, rows)
            xb, _ = _conv_fwd(xp_ref, t0, rows, cw, cb)
            _, _, ig, a, s = _lru_gates(xb, wga_ref[0], bga_ref[...], wgx_ref[0], bgx_ref[...], sp)
            cum_a, h0 = _scan_tile(a, s * (ig * xb), reverse=False)
            hs = cum_a * h_in + h0
            hs_ref[pl.ds(t0, rows), :] = hs
            hsy_ref[pl.ds(t0, rows), :] = (hs * _gelu(yp_ref[pl.ds(t0, rows), :])).astype(BF16)
            return hs[rows - 1:, :]

        lax.fori_loop(0, nt, tile, jnp.zeros((1, c), F32))

    col = pl.BlockSpec((t_all, c), lambda b: (0, b))
    vec = pl.BlockSpec((1, c), lambda b: (0, b))
    wsp = pl.BlockSpec((1, c, c), lambda b: (b, 0, 0))
    return pl.pallas_call(
        body, name=name, grid=(nblk,),
        in_specs=[col, pl.BlockSpec((t_all, c), lambda b: (0, nblk + b)), pl.BlockSpec((4, c), lambda b: (0, b)), vec,
                  wsp, vec, wsp, vec, vec],
        out_specs=[col, col],
        out_shape=[jax.ShapeDtypeStruct((t_all, dr), F32), jax.ShapeDtypeStruct((t_all, dr), BF16)],
        compiler_params=_params(("parallel",)),
    )(xy, xy, conv_w, conv_b, wga, bga, wgx, bgx, lam)


def _lru_bwd(xy, hs, dhsy, conv_w, conv_b, wga, bga, wgx, bgx, lam, *, name):
    t_all = xy.shape[0]
    dr = xy.shape[1] // 2
    c = LANES
    nblk = dr // c
    rows = LRU_ROWS
    nt = t_all // rows

    def body(xp_ref, yp_ref, hs_ref, dh_ref, cw_ref, cb_ref, wga_ref, bga_ref, wgx_ref, bgx_ref, lam_ref,
             dxp_ref, dyp_ref, dcw_ref, dcb_ref, dwga_ref, dbga_ref, dwgx_ref, dbgx_ref, dlam_ref,
             xb_scr, r_scr, i_scr, a_scr):
        cw, cb = cw_ref[...], cb_ref[...]
        lamv = lam_ref[...]
        sp = _softplus_neg(lamv)
        sig_neg = 1.0 / (1.0 + jnp.exp(lamv))
        wga_v, wgx_v = wga_ref[0], wgx_ref[0]

        def recompute(t, carry):
            t0 = pl.multiple_of(t * rows, rows)
            xb, _ = _conv_fwd(xp_ref, t0, rows, cw, cb)
            _, r, ig, a, _ = _lru_gates(xb, wga_v, bga_ref[...], wgx_v, bgx_ref[...], sp)
            xb_scr[pl.ds(t0, rows), :] = xb
            r_scr[pl.ds(t0, rows), :] = r
            i_scr[pl.ds(t0, rows), :] = ig
            a_scr[pl.ds(t0, rows), :] = a
            return carry

        lax.fori_loop(0, nt, recompute, 0)
        dwga_ref[...] = jnp.zeros(dwga_ref.shape, F32)
        dwgx_ref[...] = jnp.zeros(dwgx_ref.shape, F32)

        def tile(ti, carry):
            lam_in, dbga, dbgx, dlam, dcw, dcb = carry
            t = nt - 1 - ti
            t0 = pl.multiple_of(t * rows, rows)
            a_now, a_next = _shifted_ahead(a_scr, t0, rows, t_all, 1)
            yp = yp_ref[pl.ds(t0, rows), :]
            dhy = dh_ref[pl.ds(t0, rows), :]
            cum_a, lam0 = _scan_tile(a_next, dhy * _gelu(yp), reverse=True)
            lam_t = cum_a * lam_in + lam0
            hs_now, hs_prev = _shifted_back(hs_ref, t0, rows, 1)
            da = lam_t * hs_prev
            xb = xb_scr[pl.ds(t0, rows), :]
            r = r_scr[pl.ds(t0, rows), :]
            ig = i_scr[pl.ds(t0, rows), :]
            la = -LRU_C * r * sp
            s = jnp.sqrt(_neg_expm1(2.0 * la))
            d_ixb = lam_t * s
            dla = da * a_now - (lam_t * ig * xb) * (a_now * a_now / s)
            dzr = dla * (-LRU_C * sp) * r * (1.0 - r)
            dzi = d_ixb * xb * ig * (1.0 - ig)
            dzr_b, dzi_b = dzr.astype(BF16), dzi.astype(BF16)
            xbb = xb.astype(BF16)
            dwga_ref[0] += lax.dot_general(xbb, dzr_b, _TN, preferred_element_type=F32)
            dwgx_ref[0] += lax.dot_general(xbb, dzi_b, _TN, preferred_element_type=F32)
            dxb = (d_ixb * ig + lax.dot_general(dzr_b, wga_v, _NT, preferred_element_type=F32)
                   + lax.dot_general(dzi_b, wgx_v, _NT, preferred_element_type=F32))
            xb_scr[pl.ds(t0, rows), :] = dxb
            dyp_ref[pl.ds(t0, rows), :] = (dhy * hs_now * _gelu_grad(yp)).astype(BF16)
            ahead = _shifted_ahead(xb_scr, t0, rows, t_all, 3)
            dxp = cw[3:4, :] * ahead[0]
            for k in range(3):
                dxp = dxp + cw[k:k + 1, :] * ahead[3 - k]
            dxp_ref[pl.ds(t0, rows), :] = dxp.astype(BF16)
            back = _shifted_back(xp_ref, t0, rows, 3)
            dcw_t = jnp.concatenate([jnp.sum(dxb * back[3 - k], axis=0, keepdims=True) for k in range(4)], axis=0)
            return (lam_t[:1, :], dbga + jnp.sum(dzr, axis=0, keepdims=True), dbgx + jnp.sum(dzi, axis=0, keepdims=True),
                    dlam + jnp.sum(dla * r, axis=0, keepdims=True), dcw + dcw_t, dcb + jnp.sum(dxb, axis=0, keepdims=True))

        zero = jnp.zeros((1, c), F32)
        _, dbga, dbgx, dlam, dcw, dcb = lax.fori_loop(0, nt, tile, (zero, zero, zero, zero, jnp.zeros((4, c), F32), zero))
        dbga_ref[...] = dbga
        dbgx_ref[...] = dbgx
        dlam_ref[...] = dlam * (LRU_C * sig_neg)
        dcw_ref[...] = dcw
        dcb_ref[...] = dcb

    col = pl.BlockSpec((t_all, c), lambda b: (0, b))
    col2 = pl.BlockSpec((t_all, c), lambda b: (0, nblk + b))
    vec = pl.BlockSpec((1, c), lambda b: (0, b))
    tap = pl.BlockSpec((4, c), lambda b: (0, b))
    wsp = pl.BlockSpec((1, c, c), lambda b: (b, 0, 0))
    vshape = jax.ShapeDtypeStruct((1, dr), F32)
    wshape = jax.ShapeDtypeStruct((nblk, c, c), F32)
    def planes_body(*refs):
        dxy_ref = refs[11]
        body(*refs[:11], dxy_ref.at[0], dxy_ref.at[1], *refs[12:])

    return pl.pallas_call(
        planes_body, name=name, grid=(nblk,),
        in_specs=[col, col2, col, col, tap, vec, wsp, vec, wsp, vec, vec],
        out_specs=[pl.BlockSpec((2, t_all, c), lambda b: (0, 0, b)), tap, vec, wsp, vec, wsp, vec, vec],
        out_shape=[jax.ShapeDtypeStruct((2, t_all, dr), BF16),
                   jax.ShapeDtypeStruct((4, dr), F32), vshape, wshape, vshape, wshape, vshape, vshape],
        scratch_shapes=[pltpu.VMEM((t_all, c), F32)] * 4,
        compiler_params=_params(("parallel",)),
    )(xy, xy, hs, dhsy, conv_w, conv_b, wga, bga, wgx, bgx, lam)


def _mesh_pos():
    return lax.axis_index("x"), lax.axis_index("y"), lax.axis_index("c")


def _all_gather(shards, *, name):
    n = len(shards)

    def body(*refs):
        ins, outs, token = refs[:n], refs[n:2 * n], refs[2 * n]
        send_sems, recv_sems, local_sems = refs[2 * n + 1:]
        token[...] = jnp.zeros(token.shape, token.dtype)
        x, y, c = _mesh_pos()
        me, sibling = (x, y, c), (x, y, 1 - c)
        chips = [(1 - x, y), (x, 1 - y), (1 - x, 1 - y)]
        slot = _slot

        def copy(a, k, block, to, src=None):
            dst = outs[a].at[slot(block)]
            return pltpu.make_async_remote_copy(
                src_ref=dst if src is None else src, dst_ref=dst, send_sem=send_sems.at[a, k],
                recv_sem=recv_sems.at[a, k], device_id=to, device_id_type=MESH)

        mine = [pltpu.make_async_copy(ins[a], outs[a].at[slot(me)], local_sems.at[a]) for a in range(n)]
        for cp in mine:
            cp.start()
        first = []
        for a in range(n):
            first.append(copy(a, 0, me, sibling, src=ins[a]))
            first += [copy(a, 1 + j, me, (*chip, c), src=ins[a]) for j, chip in enumerate(chips)]
        for cp in first:
            cp.start()
        passed = []
        for a in range(n):
            for j, chip in enumerate(chips):
                copy(a, 1 + j, (*chip, c), me).wait_recv()
                fwd = copy(a, 4 + j, (*chip, c), sibling)
                fwd.start()
                passed.append(fwd)
        for a in range(n):
            copy(a, 0, sibling, me).wait_recv()
            for j, chip in enumerate(chips):
                copy(a, 4 + j, (*chip, 1 - c), me).wait_recv()
        for cp in first + passed:
            cp.wait_send()
        for cp in mine:
            cp.wait()

    any_spec = pl.BlockSpec(memory_space=pl.ANY)
    outs = pl.pallas_call(
        body, name=name,
        in_specs=[any_spec] * n, out_specs=[any_spec] * n + [pl.BlockSpec(memory_space=pltpu.VMEM)],
        out_shape=[jax.ShapeDtypeStruct((N_DEV,) + s.shape, s.dtype) for s in shards]
        + [jax.ShapeDtypeStruct((SUBLANES, LANES), F32)],
        scratch_shapes=[pltpu.SemaphoreType.DMA((n, 7)), pltpu.SemaphoreType.DMA((n, 7)), pltpu.SemaphoreType.DMA((n,))],
    )(*shards)
    return list(outs[:n]), outs[n][0, 0]


_HBM = pl.BlockSpec(memory_space=pltpu.HBM)
_SEM = pl.BlockSpec(memory_space=pltpu.SEMAPHORE)
_ANY = pl.BlockSpec(memory_space=pl.ANY)
_EFFECT = pltpu.SideEffectType.DATAFLOW_SIDE_EFFECTING


def _slot(p):
    return 4 * p[0] + 2 * p[1] + p[2]


def _remote(src, dst, send, recv, idx, to):
    return pltpu.make_async_remote_copy(src_ref=src, dst_ref=dst, send_sem=send.at[idx], recv_sem=recv.at[idx],
                                        device_id=to, device_id_type=MESH)


def _ag_plan_own(a, src, land, send, recv):
    x, y, c = _mesh_pos()
    dst = land.at[_slot((x, y, c))]
    targets = [(x, y, 1 - c), (1 - x, y, c), (x, 1 - y, c), (1 - x, 1 - y, c)]
    return [_remote(src, dst, send, recv, 4 * a + k, to) for k, to in enumerate(targets)]


def _ag_plan_pass(a, src, land, send, recv):
    x, y, c = _mesh_pos()
    blocks = [land.at[_slot((px, py, c))] for px, py in ((1 - x, y), (x, 1 - y), (1 - x, 1 - y))]
    return [_remote(blk, blk, send, recv, 3 * a + k, (x, y, 1 - c)) for k, blk in enumerate(blocks)]


def _rs_plan_sibling(a, src, land, send, recv):
    x, y, c = _mesh_pos()
    return [_remote(src.at[2 * j + (1 - c)], land.at[j], send, recv, 4 * a + j, (x, y, 1 - c)) for j in range(4)]


def _rs_plan_chips(a, src, land, send, recv):
    x, y, c = _mesh_pos()
    out = []
    for k in (1, 2, 3):
        px = 1 - x if k & 2 else x
        py = 1 - y if k & 1 else y
        out.append(_remote(src.at[2 * px + py], land.at[k - 1], send, recv, 3 * a + k - 1, (px, py, c)))
    return out


def _in_hbm(a):
    return pltpu.with_memory_space_constraint(a, pltpu.HBM)


def _exchange_start(srcs, lands, plan, n_k, *, name):
    ns, n = len(srcs), len(lands)

    def body(*refs):
        src_refs, land_refs = refs[:ns], refs[ns:ns + n]
        send, recv = refs[ns + n], refs[ns + n + 1]
        token = refs[-1]
        for a in range(n):
            for cp in plan(a, src_refs[a] if ns else None, land_refs[a], send, recv):
                cp.start()
        token[...] = jnp.zeros(token.shape, token.dtype)

    bufs = list(srcs) + list(lands)
    outs = pl.pallas_call(
        body, name=name,
        out_shape=(pltpu.SemaphoreType.DMA((n * n_k,)), pltpu.SemaphoreType.DMA((n * n_k,)),
                   *[pltpu.HBM(b.shape, b.dtype) for b in bufs], jax.ShapeDtypeStruct((SUBLANES, LANES), F32)),
        in_specs=[_HBM] * (ns + n),
        out_specs=(_SEM, _SEM, *[_HBM] * (ns + n), pl.BlockSpec(memory_space=pltpu.VMEM)),
        input_output_aliases={i: 2 + i for i in range(ns + n)},
        compiler_params=pltpu.CompilerParams(has_side_effects=_EFFECT),
    )(*[_in_hbm(b) for b in bufs])
    return outs[0], outs[1], list(outs[2:2 + ns]), list(outs[2 + ns:2 + ns + n]), outs[-1]


def _exchange_wait(started, plan, after, *, name):
    send, recv, srcs, lands, _ = started
    ns, n = len(srcs), len(lands)

    def body(*refs):
        src_refs, land_refs = refs[:ns], refs[ns:ns + n]
        send_ref, recv_ref = refs[ns + n], refs[ns + n + 1]
        for a in range(n):
            for cp in plan(a, src_refs[a] if ns else None, land_refs[a], send_ref, recv_ref):
                cp.wait_send()
                cp.wait_recv()

    bufs = list(srcs) + list(lands)
    outs = pl.pallas_call(
        body, name=name,
        out_shape=tuple(pltpu.HBM(b.shape, b.dtype) for b in bufs),
        in_specs=[_HBM] * (ns + n) + [_SEM, _SEM, _ANY],
        out_specs=tuple([_HBM] * (ns + n)),
        input_output_aliases={i: i for i in range(ns + n)},
        compiler_params=pltpu.CompilerParams(has_side_effects=_EFFECT),
    )(*bufs, send, recv, after)
    return list(outs[:ns]), list(outs[ns:])


def _pair_add(grads, landed, core, *, name, tr):
    _, r_all, c_all = grads.shape

    def body(core_ref, g_ref, l_ref, o_ref):
        o_ref[...] = (g_ref[...].astype(F32) + l_ref[...].astype(F32)).astype(o_ref.dtype)

    return pl.pallas_call(
        body, name=name,
        grid_spec=pltpu.PrefetchScalarGridSpec(
            num_scalar_prefetch=1, grid=(4, r_all // tr),
            in_specs=[pl.BlockSpec((None, tr, c_all), lambda j, i, core_ref: (2 * j + core_ref[0], i, 0)),
                      pl.BlockSpec((None, tr, c_all), lambda j, i, core_ref: (j, i, 0))],
            out_specs=pl.BlockSpec((None, tr, c_all), lambda j, i, core_ref: (j, i, 0))),
        out_shape=jax.ShapeDtypeStruct((4, r_all, c_all), grads.dtype),
        compiler_params=_params(("parallel", "parallel")),
    )(core, grads, landed)


def _adamw_math(w, g, m, v):
    m2 = ADAM_B1 * m + (1.0 - ADAM_B1) * g
    v2 = ADAM_B2 * v + (1.0 - ADAM_B2) * (g * g)
    m_hat = m2 / (1.0 - ADAM_B1 ** ADAM_STEP)
    v_hat = v2 / (1.0 - ADAM_B2 ** ADAM_STEP)
    delta = -ADAM_LR * (m_hat / (jnp.sqrt(v_hat) + ADAM_EPS) + ADAM_WD * w)
    return delta, m2, v2


def _adamw(w, m, v, terms, order, *, name, tr, col_block=None, own=None, stack=None):
    r_all, c_all = w.shape
    n_slots = terms.shape[0]

    def body(*refs):
        if col_block is not None or own is not None:
            refs = refs[1:]
        own_ref = None
        if own is not None:
            own_ref, refs = refs[0], refs[1:]
        w_ref, m_ref, v_ref, t_ref, g_ref, d_ref, m2_ref, v2_ref = refs
        if own_ref is not None:
            g = own_ref[...].astype(F32) + t_ref[order[0]].astype(F32)
        else:
            g = t_ref[order[0]].astype(F32)
        for s in order[1:]:
            g = g + t_ref[s].astype(F32)
        delta, m2, v2 = _adamw_math(w_ref[...], g, m_ref[...], v_ref[...])
        g_ref[...] = g
        d_ref[...] = delta
        m2_ref[...] = m2
        v2_ref[...] = v2

    shape = jax.ShapeDtypeStruct((r_all, c_all), F32)
    if own is not None:
        layer, n_layers, prev = stack
        row = pl.BlockSpec((tr, c_all), lambda i, idx: (i, 0))
        slab = pl.BlockSpec((None, tr, c_all), lambda i, idx: (layer, i, 0))
        carried = [] if prev is None else list(prev)

        def stacked_body(*refs):
            body(*refs[:6], *refs[6 + len(carried):])

        return pl.pallas_call(
            stacked_body, name=name,
            grid_spec=pltpu.PrefetchScalarGridSpec(
                num_scalar_prefetch=1, grid=(r_all // tr,),
                in_specs=[pl.BlockSpec((None, tr, c_all), lambda i, idx: (idx[0], i, 0)), row, row, row,
                          pl.BlockSpec((n_slots, tr, c_all), lambda i, idx: (0, i, 0))] + [_ANY] * len(carried),
                out_specs=[slab] * 4),
            out_shape=[jax.ShapeDtypeStruct((n_layers, r_all, c_all), F32)] * 4,
            input_output_aliases={6 + k: k for k in range(len(carried))},
            compiler_params=_params(("parallel",)),
        )(own[1], own[0], w, m, v, terms, *carried)
    if col_block is None:
        row = pl.BlockSpec((tr, c_all), lambda i: (i, 0))
        return pl.pallas_call(
            body, name=name, grid=(r_all // tr,),
            in_specs=[row, row, row, pl.BlockSpec((n_slots, tr, c_all), lambda i: (0, i, 0))],
            out_specs=[row] * 4, out_shape=[shape] * 4, compiler_params=_params(("parallel",)),
        )(w, m, v, terms)
    row = pl.BlockSpec((tr, c_all), lambda i, blk: (i, 0))
    return pl.pallas_call(
        body, name=name,
        grid_spec=pltpu.PrefetchScalarGridSpec(
            num_scalar_prefetch=1, grid=(r_all // tr,),
            in_specs=[row, row, row, pl.BlockSpec((n_slots, tr, c_all), lambda i, blk: (0, i, blk[0]))],
            out_specs=[row] * 4),
        out_shape=[shape] * 4, compiler_params=_params(("parallel",)),
    )(col_block, w, m, v, terms)


def _rope_tables(t_all):
    pos = jnp.arange(t_all, dtype=F32)
    inv_freq = ROPE_THETA ** (-jnp.arange(0, QK_ROPE, 2, dtype=F32) / QK_ROPE)
    ang = pos[:, None] * inv_freq[None, :]
    cos, sin = jnp.cos(ang), jnp.sin(ang)
    return jnp.tile(cos, (1, LANES // (QK_ROPE // 2))), jnp.tile(sin, (1, LANES // (QK_ROPE // 2)))


def _adam_row_tile(r_all, c_all, block_bytes=512 * 1024):
    target = max(SUBLANES, block_bytes // (4 * c_all))
    return _pick(r_all, [t for t in (1024, 704, 512, 352, 256, 176, 128, 64, 32, 16, 8) if t <= target])


def _rows_natural(wg):
    return wg.reshape(wg.shape[0] * wg.shape[1], wg.shape[2])


def _mla_layer_fwd(tag, h, g_mix, ws, qn, kvn, cos, sin, *, tm, tq, n_heads, scale, n_real):
    w_in, w_uq, w_ukv, w_o = _rows_natural(ws[0]), ws[1], ws[2], _rows_natural(ws[3])
    t_all, d = h.shape
    lq, lkv = qn.shape[1], kvn.shape[1]
    tmb = _pick(t_all, _ROW_TILES)
    hn = _rmsnorm_fwd(h, g_mix, name=f"norm_mix{tag}", tm=_pick(t_all, _ROW_TILES))
    proj = _mm_nn(hn, w_in, name=f"mla_in{tag}", out_dtype=F32, tm=tmb, tn=w_in.shape[1], tk=_pick(d, _DIVS))
    cq, ckv, kr = _mla_prep_fwd(proj, qn, kvn, cos, sin, name=f"mla_prep{tag}", tm=tm, lq=lq, lkv=lkv)
    q = _mm_nn(cq, w_uq, name=f"mla_q{tag}", out_dtype=BF16, tm=tmb, tn=w_uq.shape[2], tk=lq, b_blocked=True,
               epilogue=_rope_q_epilogue, extras=(cos, sin))
    kv = _mm_nn(ckv, w_ukv, name=f"mla_kv{tag}", out_dtype=BF16, tm=tmb, tn=w_ukv.shape[2], tk=lkv, b_blocked=True)
    o, lse = _attn_fwd(q, kv, kr, name=f"attn_fwd{tag}", n_heads=n_heads, tq=tq, n_real=n_real, scale=scale)
    h_mid = _mm_nn(o, w_o, name=f"mla_o{tag}", out_dtype=F32, tm=tm, tn=d, tk=o.shape[1], res=h)
    return h_mid, (hn, proj, cq, ckv, kr, q, kv, o, lse)


def _mla_layer_bwd(tag, dh, dh_b, h_in, saved, g_mix, ws, qn, kvn, cos, sin, *, tm, tq, n_heads, scale, n_real, early=None,
                   after=None):
    hn, proj, cq, ckv, kr, q, kv, o, lse = saved
    w_in, w_uq, w_ukv, w_o = _rows_natural(ws[0]), ws[1], ws[2], _rows_natural(ws[3])
    t_all, d = h_in.shape
    lq, lkv = qn.shape[1], kvn.shape[1]
    ov = o.shape[1]
    tmb = _pick(t_all, _ROW_TILES)
    tn_d, tk_d = _pick(d, _DIVS[1:]), _pick(d, _DIVS)
    do = _mm_nt(dh_b, w_o, name=f"mla_do{tag}", out_dtype=BF16, tm=tmb, tn=_pick(ov, _DIVS[1:]), tk=tk_d, after=after)
    dw_o = _mm_tn(o, dh_b, name=f"mla_dwo{tag}", out_dtype=BF16, tm=_pick(ov, _DIVS[2:]), tn=tn_d, tk=t_all)
    dq, dkv, dkr_h = _attn_bwd(q, kv, kr, o, lse, do, cos, sin, name=f"attn_bwd{tag}", n_heads=n_heads, tq=tq, n_real=n_real,
                               scale=scale)
    hw, kw = w_uq.shape[2], w_ukv.shape[2]
    dw_uq = _mm_tn(cq, dq, name=f"mla_dwuq{tag}", out_dtype=BF16, tm=lq, tn=hw, tk=t_all, out_block=hw)
    dcq = _mm_nt(dq, w_uq, name=f"mla_dcq{tag}", out_dtype=F32, tm=tm, tn=lq, tk=dq.shape[1], b_blocked=True)
    dw_ukv = _mm_tn(ckv, dkv, name=f"mla_dwukv{tag}", out_dtype=BF16, tm=lkv, tn=kw, tk=t_all, out_block=kw)
    dckv = _mm_nt(dkv, w_ukv, name=f"mla_dckv{tag}", out_dtype=F32, tm=tm, tn=lkv, tk=dkv.shape[1], b_blocked=True)
    first = [dw_uq, dw_ukv, dw_o.reshape(N_DEV, -1, d)]
    if early is not None:
        qn = qn + early(first)
    dproj, dqn, dkvn = _mla_prep_bwd(dcq, dckv, dkr_h, proj, qn, kvn, cos, sin, name=f"mla_prep_bwd{tag}", tm=tm, lq=lq, lkv=lkv)
    wc = w_in.shape[1]
    dw_in = _mm_tn(hn, dproj, name=f"mla_dwin{tag}", out_dtype=BF16, tm=_pick(d, _DIVS[2:]), tn=wc, tk=t_all)
    dhn = _mm_nt(dproj, w_in, name=f"mla_dhn{tag}", out_dtype=BF16, tm=tmb, tn=tn_d, tk=wc)
    dh, dh_b, dg = _rmsnorm_bwd(dhn, h_in, g_mix, dh, name=f"norm_mix_bwd{tag}", tm=tm)
    return dh, dh_b, dg, dqn, dkvn, [dw_in.reshape(N_DEV, -1, wc)] + ([] if early is not None else first)


def _lru_layer_fwd(tag, h, g_mix, ws, small, *, tm):
    w_lin, w_lo = ws[0], _rows_natural(ws[1])
    t_all, d = h.shape
    dr = w_lo.shape[0]
    tmb = _pick(t_all, _ROW_TILES)
    hn = _rmsnorm_fwd(h, g_mix, name=f"norm_mix{tag}", tm=_pick(t_all, _ROW_TILES))
    xy = _mm_nn(hn, w_lin, name=f"lru_in{tag}", out_dtype=F32, tm=tmb, tn=w_lin.shape[2], tk=_pick(d, _DIVS), b_blocked=True,
                rows_outer=True)
    hs, hsy = _lru_fwd(xy, *small, name=f"lru_fwd{tag}")
    h_mid = _mm_nn(hsy, w_lo, name=f"lru_o{tag}", out_dtype=F32, tm=tm, tn=d, tk=dr, res=h)
    return h_mid, (hn, xy, hs, hsy)


def _lru_layer_bwd(tag, dh, dh_b, h_in, saved, g_mix, ws, small, *, tm, after=None):
    hn, xy, hs, hsy = saved
    w_lin, w_lo = ws[0], _rows_natural(ws[1])
    t_all, d = h_in.shape
    dr = w_lo.shape[0]
    tmb = _pick(t_all, _ROW_TILES)
    tn_d, tk_d = _pick(d, _DIVS[1:]), _pick(d, _DIVS)
    dhsy = _mm_nt(dh_b, w_lo, name=f"lru_dhsy{tag}", out_dtype=F32, tm=tmb, tn=_pick(dr, _DIVS[1:]), tk=tk_d, after=after)
    dw_lo = _mm_tn(hsy, dh_b, name=f"lru_dwo{tag}", out_dtype=BF16, tm=_pick(dr, _DIVS[2:]), tn=tn_d, tk=t_all)
    dxy, *dsmall = _lru_bwd(xy, hs, dhsy, *small, name=f"lru_bwd{tag}")
    lw = w_lin.shape[2]
    dw_lin = _mm_tn(hn, dxy, name=f"lru_dwin{tag}", out_dtype=BF16, tm=tn_d, tn=lw, tk=t_all, out_block=lw)
    dhn = _mm_nt(dxy, w_lin, name=f"lru_dhn{tag}", out_dtype=BF16, tm=tm, tn=tn_d, tk=2 * dr, b_blocked=True)
    dh, dh_b, dg = _rmsnorm_bwd(dhn, h_in, g_mix, dh, name=f"norm_mix_bwd{tag}", tm=tm)
    return dh, dh_b, dg, tuple(dsmall), [dw_lin, dw_lo.reshape(N_DEV, -1, d)]


def _ffn_layer_fwd(tag, h_mid, g_ffn, ws, *, tm):
    w_gu, w_down = ws[0], _rows_natural(ws[1])
    t_all, d = h_mid.shape
    f_all = w_down.shape[0]
    tmb = _pick(t_all, _ROW_TILES)
    fk = _pick(f_all, (1408,) + _DIVS[1:])
    hn2 = _rmsnorm_fwd(h_mid, g_ffn, name=f"norm_ffn{tag}", tm=_pick(t_all, _ROW_TILES))
    gu, act = _ffn_up(hn2, w_gu, name=f"ffn_up{tag}", tm=_pick(t_all, (704, 384, 256, 128)))
    h_out = _mm_nn(act, w_down, name=f"ffn_down{tag}", out_dtype=F32, tm=tm, tn=_pick(d, _DIVS[1:]), tk=f_all, res=h_mid)
    return h_out, (hn2, gu, act)


def _ffn_layer_bwd(tag, dh, dh_b, h_mid, saved, g_ffn, ws, *, tm, after=None):
    hn2, gu, act = saved
    w_gu, w_down = ws[0], _rows_natural(ws[1])
    t_all, d = h_mid.shape
    f_all = w_down.shape[0]
    f_local = w_gu.shape[2]
    tmb = _pick(t_all, _ROW_TILES)
    fk = _pick(f_all, (1408,) + _DIVS[1:])
    tn_d, tk_d = _pick(d, _DIVS[1:]), _pick(d, _DIVS)
    dgu = _ffn_dact(dh_b, w_down, gu, name=f"ffn_dact{tag}", tm=_pick(t_all, (704, 384, 256, 128)), tn=f_local, after=after)
    dw_down = _mm_tn(act, dh_b, name=f"ffn_dwdown{tag}", out_dtype=BF16, tm=fk, tn=_pick(d, _DIVS[2:]), tk=t_all)
    dhn2 = _mm_nt(dgu, w_gu, name=f"ffn_dhn{tag}", out_dtype=BF16, tm=tm, tn=_pick(d, _DIVS[2:]), tk=2 * f_all, b_blocked=True)
    dw_gu = _mm_tn(hn2, dgu, name=f"ffn_dwgu{tag}", out_dtype=BF16, tm=_pick(d, _DIVS[2:]), tn=f_local, tk=t_all, out_block=f_local,
                   cols_outer=True)
    dh, dh_b, dg = _rmsnorm_bwd(dhn2, h_mid, g_ffn, dh, name=f"norm_ffn_bwd{tag}", tm=tm)
    return dh, dh_b, dg, [dw_gu, dw_down.reshape(N_DEV, -1, d)]


def kernel(x, meta_tokens, norm_mix, norm_ffn, norm_final, mla_w_in, mla_q_norm, mla_kv_norm, mla_w_uq, mla_w_ukv, mla_w_o, lru_w_in, lru_conv_w, lru_conv_b, lru_w_gate_a, lru_b_gate_a, lru_w_gate_x, lru_b_gate_x, lru_lambda, lru_w_o, ffn_w_gu, ffn_w_down, loss_target, m_meta_tokens, m_norm_mix, m_norm_ffn, m_norm_final, m_mla_w_in, m_mla_q_norm, m_mla_kv_norm, m_mla_w_uq, m_mla_w_ukv, m_mla_w_o, m_lru_w_in, m_lru_conv_w, m_lru_conv_b, m_lru_w_gate_a, m_lru_b_gate_a, m_lru_w_gate_x, m_lru_b_gate_x, m_lru_lambda, m_lru_w_o, m_ffn_w_gu, m_ffn_w_down, v_meta_tokens, v_norm_mix, v_norm_ffn, v_norm_final, v_mla_w_in, v_mla_q_norm, v_mla_kv_norm, v_mla_w_uq, v_mla_w_ukv, v_mla_w_o, v_lru_w_in, v_lru_conv_w, v_lru_conv_b, v_lru_w_gate_a, v_lru_b_gate_a, v_lru_w_gate_x, v_lru_b_gate_x, v_lru_lambda, v_lru_w_o, v_ffn_w_gu, v_ffn_w_down):
    seq, d = x.shape[1], x.shape[2]
    assert seq % CHUNK == 0
    n_real = N_META + seq
    t_all = -(-n_real // LANES) * LANES
    tm = _pick(t_all, (384, 256, 128))
    tq = _pick(seq, (512, 256, 128, 64))
    depth = norm_mix.shape[0]
    n_mla, n_lru = mla_w_in.shape[0], lru_w_in.shape[0]
    lq, lkv = mla_q_norm.shape[1], mla_kv_norm.shape[1]
    w_in_cols = lq + lkv + LANES
    heads_local = mla_w_uq.shape[2] // (QK_NOPE + QK_ROPE)
    n_heads = heads_local * N_DEV
    dr = lru_w_gate_a.shape[1] * lru_w_gate_a.shape[2]
    scale = (QK_NOPE + QK_ROPE) ** -0.5
    cx, cy, cc = _mesh_pos()
    core = jnp.reshape(cc, (1,)).astype(jnp.int32)
    my_slot = jnp.reshape(4 * cx + 2 * cy + cc, (1,)).astype(jnp.int32)

    def pad_cols(w, cols):
        return jnp.pad(w, ((0, 0), (0, cols - w.shape[1])))

    def pad_heads(w):
        k_all = w.shape[0]
        w3 = w.reshape(k_all, heads_local, QK_NOPE + QK_ROPE)
        return jnp.pad(w3, ((0, 0), (0, 0), (0, HEAD_W - QK_NOPE - QK_ROPE))).reshape(k_all, heads_local * HEAD_W)

    def unpad_heads(w):
        k_all = w.shape[0]
        return w.reshape(k_all, heads_local, HEAD_W)[:, :, :QK_NOPE + QK_ROPE].reshape(k_all, -1)

    small_rows = N_META + n_lru * 4 + 2 * n_lru
    small_pad = -(-small_rows // SUBLANES) * SUBLANES

    def pack_small(meta, conv_w, conv_b, lam):
        rows = jnp.concatenate([meta, conv_w.reshape(n_lru * 4, -1), conv_b, lam], axis=0)
        return jnp.pad(rows, ((0, small_pad - small_rows), (0, 0)))

    def unpack_small(p):
        o1 = N_META + n_lru * 4
        return (p[:N_META], p[N_META:o1].reshape(n_lru, 4, -1), p[o1:o1 + n_lru], p[o1 + n_lru:o1 + 2 * n_lru])

    (small_full,), small_done = _all_gather([pack_small(meta_tokens, lru_conv_w, lru_conv_b, lru_lambda)], name="ag_small")
    small_full = jnp.transpose(small_full, (1, 0, 2)).reshape(small_pad, -1)
    meta_full, conv_w_full, conv_b_full, lam_full = unpack_small(small_full)

    def wire(w):
        return (w + small_done).astype(BF16)

    mla_shards, lru_shards, ffn_shards = [], [], []
    for j in range(n_mla):
        mla_shards.append([wire(pad_cols(mla_w_in[j], w_in_cols)), wire(pad_heads(mla_w_uq[j])), wire(mla_w_ukv[j]),
                           wire(mla_w_o[j])])
    for j in range(n_lru):
        lru_shards.append([wire(lru_w_in[j]), wire(lru_w_o[j])])
    for layer in range(depth):
        ffn_shards.append([wire(ffn_w_gu[layer]), wire(ffn_w_down[layer])])

    n_sub = 2 * depth
    groups = []
    for layer in range(depth):
        groups += [mla_shards[layer // 2] if layer % 2 == 0 else lru_shards[layer // 2], ffn_shards[layer]]
    slot_idx = 4 * cx + 2 * cy + cc
    ag_own = []
    for gi, shards in enumerate(groups):
        lands = [lax.dynamic_update_slice(lax.empty((N_DEV,) + s.shape, s.dtype), s[None], (slot_idx, 0, 0)) for s in shards]
        ag_own.append(_exchange_start(shards, lands, _ag_plan_own, 4, name=f"ag{gi}_start"))
    ag_pass = [None] * n_sub
    weights = [None] * n_sub

    def ag_landed(gi, after):
        _, lands = _exchange_wait(ag_own[gi], _ag_plan_own, after, name=f"ag{gi}_wait")
        ag_pass[gi] = _exchange_start([], lands, _ag_plan_pass, 3, name=f"ag{gi}_pass")
        return ag_pass[gi][4][0, 0]

    def ag_done(gi, after):
        _, weights[gi] = _exchange_wait(ag_pass[gi], _ag_plan_pass, after, name=f"ag{gi}_pass_wait")

    cos, sin = _rope_tables(t_all)
    zeros_tail = jnp.zeros((t_all - n_real, d), F32)
    started = ag_own[0][4][0, 0]
    for st in ag_own[1:]:
        started = started + st[4][0, 0]
    h = jnp.concatenate([meta_full + started, x[0], zeros_tail], axis=0)
    target = jnp.concatenate([jnp.zeros((N_META, d), F32), loss_target[0], zeros_tail], axis=0)

    attn_kw = dict(tm=tm, tq=tq, n_heads=n_heads, scale=scale, n_real=n_real)

    def lru_small(j):
        return (conv_w_full[j], conv_b_full[j][None, :], lru_w_gate_a[j].astype(BF16), lru_b_gate_a[j].reshape(1, dr),
                lru_w_gate_x[j].astype(BF16), lru_b_gate_x[j].reshape(1, dr), lam_full[j][None, :])

    def before_sublayer(k, act):
        tok = ag_landed(k, act) if k <= 1 else 0.0
        ag_done(k, act)
        if 1 <= k < n_sub - 1:
            tok = tok + ag_landed(k + 1, act)
        return tok

    saved = []
    for layer in range(depth):
        j = layer // 2
        g_mix = norm_mix[layer][None, :] + before_sublayer(2 * layer, h)
        if layer % 2 == 0:
            h_mid, mix_saved = _mla_layer_fwd(layer, h, g_mix, weights[2 * layer], mla_q_norm[j][None, :],
                                              mla_kv_norm[j][None, :], cos, sin, **attn_kw)
        else:
            h_mid, mix_saved = _lru_layer_fwd(layer, h, g_mix, weights[2 * layer], lru_small(j), tm=tm)
        g_ffn = norm_ffn[layer][None, :] + before_sublayer(2 * layer + 1, h_mid)
        h_out, ffn_saved = _ffn_layer_fwd(layer, h_mid, g_ffn, weights[2 * layer + 1], tm=tm)
        saved.append((h, h_mid, mix_saved, ffn_saved))
        h = h_out

    loss_part, dh, dh_b, dg_final = _loss_head(h, target, norm_final[None, :], name="loss_head", tm=tm, n_real=n_real)
    loss = lax.psum(loss_part[0, 0], ("x", "y", "c"))

    rs_sib, rs_chip, reduced = [None] * (n_sub + 1), [None] * (n_sub + 1), [None] * (n_sub + 1)
    chip_idx = jnp.reshape(2 * cx + cy, (1,)).astype(jnp.int32)

    def rs_begin(k, grads):
        lands = [lax.empty((4,) + g.shape[1:], g.dtype) for g in grads]
        rs_sib[k] = _exchange_start(grads, lands, _rs_plan_sibling, 4, name=f"rs{k}_start")
        return rs_sib[k][4]

    def rs_middle(k, after):
        grads, landed = _exchange_wait(rs_sib[k], _rs_plan_sibling, after, name=f"rs{k}_wait")
        parts = [_pair_add(g, l, core, name=f"rs{k}_add{a}", tr=_adam_row_tile(g.shape[1], g.shape[2], 4 * 1024 * 1024))
                 for a, (g, l) in enumerate(zip(grads, landed))]
        lands = [lax.empty((3,) + p.shape[1:], p.dtype) for p in parts]
        rs_chip[k] = _exchange_start(parts, lands, _rs_plan_chips, 3, name=f"rs{k}_chips")
        return rs_chip[k][4]

    def rs_end(k, after):
        reduced[k] = _exchange_wait(rs_chip[k], _rs_plan_chips, after, name=f"rs{k}_chips_wait")

    d_norm_mix, d_norm_ffn = [None] * depth, [None] * depth
    d_qn, d_kvn = [None] * n_mla, [None] * n_mla
    d_small = {k: [None] * n_lru for k in ("cw", "cb", "wga", "bga", "wgx", "bgx", "lam")}
    tok, waiting = None, None
    gate_own = [None] * n_lru
    for layer in reversed(range(depth)):
        j = layer // 2
        h_in, h_mid, mix_saved, ffn_saved = saved[layer]
        dh, dh_b, d_norm_ffn[layer], ffn_g = _ffn_layer_bwd(layer, dh, dh_b, h_mid, ffn_saved, norm_ffn[layer][None, :],
                                                            weights[2 * layer + 1], tm=tm, after=tok)
        tok = rs_begin(2 * layer + 1, ffn_g)
        if waiting is not None:
            tok = tok + rs_middle(waiting, dh)
        waiting = 2 * layer + 1
        if layer == 0:
            tok = tok + rs_middle(waiting, dh)
            waiting = None
        g_mix = norm_mix[layer][None, :]
        if layer % 2 == 0:
            early = (lambda g: (rs_begin(n_sub, g) + rs_middle(n_sub, g[0]))[0, 0]) if layer == 0 else None
            dh, dh_b, d_norm_mix[layer], d_qn[j], d_kvn[j], mix_g = _mla_layer_bwd(
                layer, dh, dh_b, h_in, mix_saved, g_mix, weights[2 * layer], mla_q_norm[j][None, :], mla_kv_norm[j][None, :],
                cos, sin, early=early, after=tok, **attn_kw)
            tok = rs_begin(2 * layer, mix_g)
        else:
            dh, dh_b, d_norm_mix[layer], dsmall, mix_g = _lru_layer_bwd(layer, dh, dh_b, h_in, mix_saved, g_mix,
                                                                        weights[2 * layer], lru_small(j), tm=tm, after=tok)
            for key, val in zip(("cw", "cb", "wga", "bga", "wgx", "bgx", "lam"), dsmall):
                d_small[key][j] = val
            gates = [d_small["wga"][j].reshape(-1, LANES), d_small["wgx"][j].reshape(-1, LANES)]
            gate_lands = [lax.dynamic_update_slice(lax.empty((N_DEV,) + g.shape, g.dtype), g[None], (slot_idx, 0, 0)) for g in gates]
            gate_own[j] = _exchange_start(gates, gate_lands, _ag_plan_own, 4, name=f"ag_gates{j}_start")
            tok = rs_begin(2 * layer, mix_g) + gate_own[j][4]
        if waiting is not None:
            tok = tok + rs_middle(waiting, dh)
        waiting = 2 * layer
    rs_middle(waiting, dh)

    grad_x = dh[N_META:n_real][None]

    d_meta = dh[:N_META]
    small_grad = pack_small(d_meta, jnp.stack(d_small["cw"], axis=0), jnp.concatenate(d_small["cb"], axis=0),
                            jnp.concatenate(d_small["lam"], axis=0))
    rep_grads = [
        jnp.concatenate(d_norm_mix, axis=0), jnp.concatenate(d_norm_ffn, axis=0), dg_final,
        jnp.concatenate(d_qn, axis=0), jnp.concatenate(d_kvn, axis=0),
        jnp.concatenate(d_small["bga"], axis=0), jnp.concatenate(d_small["bgx"], axis=0),
    ]
    small_srcs = [small_grad] + [jnp.pad(g, ((0, -g.shape[0] % SUBLANES), (0, 0))) for g in rep_grads]
    small_lands = [lax.dynamic_update_slice(lax.empty((N_DEV,) + s.shape, s.dtype), s[None], (slot_idx, 0, 0))
                   for s in small_srcs]
    small_own = _exchange_start(small_srcs, small_lands, _ag_plan_own, 4, name="ag_grads_start")

    res = {}

    def adam_sharded(nm, k, a, idx, n_layers, w, m, v):
        parts, landed = reduced[k]
        r_all, c_all = landed[a].shape[1], landed[a].shape[2]
        res[nm] = _adamw(w.reshape(r_all, c_all), m.reshape(r_all, c_all), v.reshape(r_all, c_all), landed[a], (0, 1, 2),
                         name=f"adamw_{nm}{idx}", tr=_adam_row_tile(r_all, c_all, 2 * 1024 * 1024), own=(parts[a], chip_idx),
                         stack=(idx, n_layers, res.get(nm)))

    after = small_own[4]
    for k in reversed(range(n_sub)):
        rs_end(k, after)
        if k == 0:
            rs_end(n_sub, after)
            reduced[0] = tuple(first + rest for first, rest in zip(reduced[0], reduced[n_sub]))
        layer, j = k // 2, k // 4
        if k % 2 == 1:
            adam_sharded("ffn_w_gu", k, 0, layer, depth, ffn_w_gu[layer], m_ffn_w_gu[layer], v_ffn_w_gu[layer])
            adam_sharded("ffn_w_down", k, 1, layer, depth, ffn_w_down[layer], m_ffn_w_down[layer], v_ffn_w_down[layer])
            after = res["ffn_w_down"][0]
        elif layer % 2 == 0:
            adam_sharded("mla_w_in", k, 0, j, n_mla, pad_cols(mla_w_in[j], w_in_cols), pad_cols(m_mla_w_in[j], w_in_cols),
                         pad_cols(v_mla_w_in[j], w_in_cols))
            adam_sharded("mla_w_uq", k, 1, j, n_mla, pad_heads(mla_w_uq[j]), pad_heads(m_mla_w_uq[j]), pad_heads(v_mla_w_uq[j]))
            adam_sharded("mla_w_ukv", k, 2, j, n_mla, mla_w_ukv[j], m_mla_w_ukv[j], v_mla_w_ukv[j])
            adam_sharded("mla_w_o", k, 3, j, n_mla, mla_w_o[j], m_mla_w_o[j], v_mla_w_o[j])
            after = res["mla_w_o"][0]
        else:
            adam_sharded("lru_w_in", k, 0, j, n_lru, lru_w_in[j], m_lru_w_in[j], v_lru_w_in[j])
            adam_sharded("lru_w_o", k, 1, j, n_lru, lru_w_o[j], m_lru_w_o[j], v_lru_w_o[j])
            after = res["lru_w_o"][0]
    res["mla_w_in"] = [t[:, :, :lq + lkv + QK_ROPE] for t in res["mla_w_in"]]
    res["mla_w_uq"] = [t.reshape(n_mla, lq, heads_local, HEAD_W)[:, :, :, :QK_NOPE + QK_ROPE].reshape(n_mla, lq, -1)
                       for t in res["mla_w_uq"]]

    _, small_lands = _exchange_wait(small_own, _ag_plan_own, after, name="ag_grads_wait")
    small_pass = _exchange_start([], small_lands, _ag_plan_pass, 3, name="ag_grads_pass")
    _, all_small = _exchange_wait(small_pass, _ag_plan_pass, after, name="ag_grads_pass_wait")
    gate_terms = []
    for j in range(n_lru):
        _, lands = _exchange_wait(gate_own[j], _ag_plan_own, after, name=f"ag_gates{j}_wait")
        gate_pass = _exchange_start([], lands, _ag_plan_pass, 3, name=f"ag_gates{j}_pass")
        gate_terms.append(_exchange_wait(gate_pass, _ag_plan_pass, after, name=f"ag_gates{j}_pass_wait")[1])
    wga_terms = jnp.concatenate([t[0] for t in gate_terms], axis=1)
    wgx_terms = jnp.concatenate([t[1] for t in gate_terms], axis=1)
    slot_order = tuple(range(N_DEV))

    def adam_rep(terms, w, m, v, tag):
        r_pad, c_all = terms.shape[1], terms.shape[2]

        def prep(t):
            t2 = t.reshape(-1, c_all)
            return jnp.pad(t2, ((0, r_pad - t2.shape[0]), (0, 0)))

        outs = _adamw(prep(w), prep(m), prep(v), terms, slot_order, name=f"adamw_{tag}", tr=_adam_row_tile(r_pad, c_all))
        n_rows = w.size // c_all
        return [o[:n_rows].reshape(w.shape) for o in outs]

    small_w = pack_small(meta_tokens, lru_conv_w, lru_conv_b, lru_lambda)
    small_m = pack_small(m_meta_tokens, m_lru_conv_w, m_lru_conv_b, m_lru_lambda)
    small_v = pack_small(v_meta_tokens, v_lru_conv_w, v_lru_conv_b, v_lru_lambda)
    small_out = _adamw(small_w, small_m, small_v, all_small[0], slot_order, name="adamw_small", tr=small_pad, col_block=my_slot)
    small_out = [unpack_small(o) for o in small_out]
    for idx, key in enumerate(("meta_tokens", "lru_conv_w", "lru_conv_b", "lru_lambda")):
        res[key] = [small_out[k][idx] for k in range(4)]

    res["norm_mix"] = adam_rep(all_small[1], norm_mix, m_norm_mix, v_norm_mix, "norm_mix")
    res["norm_ffn"] = adam_rep(all_small[2], norm_ffn, m_norm_ffn, v_norm_ffn, "norm_ffn")
    res["norm_final"] = adam_rep(all_small[3], norm_final, m_norm_final, v_norm_final, "norm_final")
    res["mla_q_norm"] = adam_rep(all_small[4], mla_q_norm, m_mla_q_norm, v_mla_q_norm, "mla_q_norm")
    res["mla_kv_norm"] = adam_rep(all_small[5], mla_kv_norm, m_mla_kv_norm, v_mla_kv_norm, "mla_kv_norm")
    res["lru_w_gate_a"] = adam_rep(wga_terms, lru_w_gate_a, m_lru_w_gate_a, v_lru_w_gate_a, "lru_w_gate_a")
    res["lru_b_gate_a"] = adam_rep(all_small[6], lru_b_gate_a, m_lru_b_gate_a, v_lru_b_gate_a, "lru_b_gate_a")
    res["lru_w_gate_x"] = adam_rep(wgx_terms, lru_w_gate_x, m_lru_w_gate_x, v_lru_w_gate_x, "lru_w_gate_x")
    res["lru_b_gate_x"] = adam_rep(all_small[7], lru_b_gate_x, m_lru_b_gate_x, v_lru_b_gate_x, "lru_b_gate_x")

    names = ["meta_tokens", "norm_mix", "norm_ffn", "norm_final", "mla_w_in", "mla_q_norm", "mla_kv_norm", "mla_w_uq",
             "mla_w_ukv", "mla_w_o", "lru_w_in", "lru_conv_w", "lru_conv_b", "lru_w_gate_a", "lru_b_gate_a", "lru_w_gate_x",
             "lru_b_gate_x", "lru_lambda", "lru_w_o", "ffn_w_gu", "ffn_w_down"]
    shapes = dict(meta_tokens=meta_tokens, norm_mix=norm_mix, norm_ffn=norm_ffn, norm_final=norm_final, mla_w_in=mla_w_in,
                  mla_q_norm=mla_q_norm, mla_kv_norm=mla_kv_norm, mla_w_uq=mla_w_uq, mla_w_ukv=mla_w_ukv, mla_w_o=mla_w_o,
                  lru_w_in=lru_w_in, lru_conv_w=lru_conv_w, lru_conv_b=lru_conv_b, lru_w_gate_a=lru_w_gate_a,
                  lru_b_gate_a=lru_b_gate_a, lru_w_gate_x=lru_w_gate_x, lru_b_gate_x=lru_b_gate_x, lru_lambda=lru_lambda,
                  lru_w_o=lru_w_o, ffn_w_gu=ffn_w_gu, ffn_w_down=ffn_w_down)
    outs = [loss, grad_x]
    for k in range(4):
        outs += [res[nm][k].reshape(shapes[nm].shape) for nm in names]
    return tuple(outs)
```

```python
import math

import jax
import jax.numpy as jnp
from jax import lax
from jax.experimental import pallas as pl
from jax.experimental.pallas import tpu as pltpu

F32 = jnp.float32
BF16 = jnp.bfloat16
MESH = pl.DeviceIdType.MESH

N_META = 16
CHUNK = 64
QK_NOPE = 128
QK_ROPE = 64
V_HEAD = 128
HEAD_W = 256
ROPE_THETA = 10000.0
LRU_C = 8.0
RMS_EPS = 1e-6
NEG_BIG = -1e30
ADAM_LR, ADAM_B1, ADAM_B2, ADAM_EPS, ADAM_WD, ADAM_STEP = 0.001, 0.9, 0.999, 1e-08, 0.01, 10

LANES = 128
SUBLANES = 8
VMEM_LIMIT_BYTES = 52 * 1024 * 1024
N_DEV = 8

_NT = (((1,), (1,)), ((), ()))
_TN = (((0,), (0,)), ((), ()))
_DIVS = (2048, 1024, 512, 256, 128)
_ROW_TILES = (1408, 1024, 512, 256, 128)


def _params(dims):
    return pltpu.CompilerParams(dimension_semantics=dims, vmem_limit_bytes=VMEM_LIMIT_BYTES)


def _pick(n, candidates):
    for c in candidates:
        if c <= n and n % c == 0:
            return c
    return n


def _sigmoid(z):
    return 0.5 + 0.5 * jnp.tanh(0.5 * z)


def _gelu(x):
    c = math.sqrt(2.0 / math.pi)
    return 0.5 * x * (1.0 + jnp.tanh(c * (x + 0.044715 * x * x * x)))


def _gelu_grad(x):
    c = math.sqrt(2.0 / math.pi)
    th = jnp.tanh(c * (x + 0.044715 * x * x * x))
    return 0.5 * (1.0 + th) + 0.5 * x * (1.0 - th * th) * c * (1.0 + 3.0 * 0.044715 * x * x)


def _neg_expm1(x):
    poly = -x * (1.0 + x * (1.0 / 2.0) * (1.0 + x * (1.0 / 3.0) * (1.0 + x * (1.0 / 4.0) * (
        1.0 + x * (1.0 / 5.0) * (1.0 + x * (1.0 / 6.0) * (1.0 + x * (1.0 / 7.0)))))))
    return jnp.where(x > -0.25, poly, 1.0 - jnp.exp(x))


def _softplus_neg(lam):
    e = jnp.exp(-jnp.abs(lam))
    log1p = jnp.where(e > 1e-4, jnp.log(1.0 + e), e * (1.0 - e * (0.5 - e * (1.0 / 3.0))))
    return jnp.maximum(-lam, 0.0) + log1p


def _rot_half(x):
    lane = lax.broadcasted_iota(jnp.int32, x.shape, 1)
    first = (lane % QK_ROPE) < (QK_ROPE // 2)
    return jnp.where(first, -pltpu.roll(x, LANES - QK_ROPE // 2, 1), pltpu.roll(x, QK_ROPE // 2, 1))


def _rope(x, cos, sin):
    return x * cos + _rot_half(x) * sin


def _unrope(g, cos, sin):
    return g * cos - _rot_half(g) * sin


def _grid_order(rows_outer):
    if not rows_outer:
        return lambda f: f
    return lambda f: (lambda i, j, k: f(j, i, k))


def _mm_nn(a, b, *, name, out_dtype, tm, tn, tk, b_blocked=False, res=None, epilogue=None, extras=(), rows_outer=False):
    m_all, k_all = a.shape
    om = _grid_order(rows_outer)
    if b_blocked:
        g_all, kb, nb = b.shape
        n_all = g_all * nb
        assert nb % tn == 0
        r = nb // tn
        b_spec = pl.BlockSpec((None, tk, tn), om(lambda j, i, k: (j // r, k, j % r)))
    else:
        kb, n_all = b.shape
        b_spec = pl.BlockSpec((tk, tn), om(lambda j, i, k: (k, j)))
    assert kb == k_all and m_all % tm == 0 and n_all % tn == 0 and k_all % tk == 0
    nm, nn, nk = m_all // tm, n_all // tn, k_all // tk
    in_specs = [pl.BlockSpec((tm, tk), om(lambda j, i, k: (i, k))), b_spec]
    operands = [a, b]
    has_res = res is not None
    if has_res:
        in_specs.append(pl.BlockSpec((tm, tn), om(lambda j, i, k: (i, j))))
        operands.append(res)
    for e in extras:
        in_specs.append(pl.BlockSpec((tm, e.shape[1]), om(lambda j, i, k: (i, 0))))
        operands.append(e)
    n_ex = len(extras)

    def body(*refs):
        a_ref, b_ref = refs[0], refs[1]
        pos = 2
        res_ref = None
        if has_res:
            res_ref = refs[pos]
            pos += 1
        ex_refs = refs[pos:pos + n_ex]
        pos += n_ex
        o_ref = refs[pos]
        acc_ref = refs[pos + 1] if nk > 1 else None

        def finish(acc):
            if has_res:
                acc = acc + res_ref[...]
            if epilogue is not None:
                acc = epilogue(acc, *ex_refs)
            o_ref[...] = acc.astype(o_ref.dtype)

        prod = jnp.dot(a_ref[...], b_ref[...], preferred_element_type=F32)
        if nk == 1:
            finish(prod)
        else:
            k = pl.program_id(2)

            @pl.when(k == 0)
            def _():
                acc_ref[...] = prod

            @pl.when(k > 0)
            def _():
                acc_ref[...] += prod

            @pl.when(k == nk - 1)
            def _():
                finish(acc_ref[...])

    return pl.pallas_call(
        body, name=name, grid=(nm, nn, nk) if rows_outer else (nn, nm, nk), in_specs=in_specs,
        out_specs=pl.BlockSpec((tm, tn), om(lambda j, i, k: (i, j))),
        out_shape=jax.ShapeDtypeStruct((m_all, n_all), out_dtype),
        scratch_shapes=[pltpu.VMEM((tm, tn), F32)] if nk > 1 else [],
        compiler_params=_params(("parallel", "parallel", "arbitrary")),
    )(*operands)


def _mm_nt(a, b, *, name, out_dtype, tm, tn, tk, b_blocked=False, after=None):
    if a.ndim == 3:
        n_planes, m_all, kp = a.shape
        k_all = n_planes * kp
    else:
        n_planes, (m_all, k_all) = 0, a.shape
    if b_blocked and tk == k_all and b.shape[0] > 1:
        g_all, n_all, nb = b.shape
        assert g_all * nb == k_all and m_all % tm == 0 and n_all % tn == 0
        per_plane = kp // nb if n_planes else 0

        def whole_body(a_ref, b_ref, o_ref):
            acc = None
            for g in range(g_all):
                a_g = a_ref[g // per_plane, :, (g % per_plane) * nb:(g % per_plane + 1) * nb] if n_planes else a_ref[:, g * nb:(g + 1) * nb]
                prod = lax.dot_general(a_g, b_ref[g], _NT, preferred_element_type=F32)
                acc = prod if acc is None else acc + prod
            o_ref[...] = acc.astype(o_ref.dtype)

        a_whole = (pl.BlockSpec((n_planes, tm, kp), lambda j, i: (0, i, 0)) if n_planes
                   else pl.BlockSpec((tm, k_all), lambda j, i: (i, 0)))
        return pl.pallas_call(
            whole_body, name=name, grid=(n_all // tn, m_all // tm),
            in_specs=[a_whole, pl.BlockSpec((g_all, tn, nb), lambda j, i: (0, j, 0))],
            out_specs=pl.BlockSpec((tm, tn), lambda j, i: (i, j)),
            out_shape=jax.ShapeDtypeStruct((m_all, n_all), out_dtype),
            compiler_params=_params(("parallel", "parallel")),
        )(a, b)
    if n_planes:
        assert kp % tk == 0
        rp = kp // tk
        a_spec = pl.BlockSpec((None, tm, tk), lambda j, i, k: (k // rp, i, k % rp))
    else:
        a_spec = pl.BlockSpec((tm, tk), lambda j, i, k: (i, k))
    if b_blocked:
        g_all, n_all, nb = b.shape
        assert g_all * nb == k_all and nb % tk == 0
        r = nb // tk
        b_spec = pl.BlockSpec((None, tn, tk), lambda j, i, k: (k // r, j, k % r))
    else:
        n_all, kb = b.shape
        assert kb == k_all
        b_spec = pl.BlockSpec((tn, tk), lambda j, i, k: (j, k))
    assert m_all % tm == 0 and n_all % tn == 0 and k_all % tk == 0
    nm, nn, nk = m_all // tm, n_all // tn, k_all // tk

    def body(a_ref, b_ref, o_ref, *scratch):
        prod = lax.dot_general(a_ref[...], b_ref[...], _NT, preferred_element_type=F32)
        if nk == 1:
            o_ref[...] = prod.astype(o_ref.dtype)
        else:
            acc_ref = scratch[0]
            k = pl.program_id(2)

            @pl.when(k == 0)
            def _():
                acc_ref[...] = prod

            @pl.when(k > 0)
            def _():
                acc_ref[...] += prod

            @pl.when(k == nk - 1)
            def _():
                o_ref[...] = acc_ref[...].astype(o_ref.dtype)

    ordered = [] if after is None else [after]

    def ordered_body(a_ref, b_ref, *rest):
        body(a_ref, b_ref, *rest[len(ordered):])

    return pl.pallas_call(
        ordered_body, name=name, grid=(nn, nm, nk),
        in_specs=[a_spec, b_spec] + [pl.BlockSpec(memory_space=pl.ANY)] * len(ordered),
        out_specs=pl.BlockSpec((tm, tn), lambda j, i, k: (i, j)),
        out_shape=jax.ShapeDtypeStruct((m_all, n_all), out_dtype),
        scratch_shapes=[pltpu.VMEM((tm, tn), F32)] if nk > 1 else [],
        compiler_params=_params(("parallel", "parallel", "arbitrary")),
    )(a, b, *ordered)


def _mm_tn(a, b, *, name, out_dtype, tm, tn, tk, out_block=None, cols_outer=False):
    t_all, m_all = a.shape
    om = _grid_order(cols_outer)
    if b.ndim == 3:
        n_planes, tb, n_p = b.shape
        assert n_p % tn == 0
        rq = n_p // tn
        n_all = n_planes * n_p
        b_spec = pl.BlockSpec((None, tk, tn), om(lambda i, j, k: (j // rq, k, j % rq)))
    else:
        tb, n_all = b.shape
        b_spec = pl.BlockSpec((tk, tn), om(lambda i, j, k: (k, j)))
    assert tb == t_all and m_all % tm == 0 and n_all % tn == 0 and t_all % tk == 0
    nm, nn, nk = m_all // tm, n_all // tn, t_all // tk
    if out_block is None:
        out_shape = jax.ShapeDtypeStruct((m_all, n_all), out_dtype)
        out_spec = pl.BlockSpec((tm, tn), om(lambda i, j, k: (i, j)))
    else:
        assert out_block % tn == 0 and n_all % out_block == 0
        r = out_block // tn
        out_shape = jax.ShapeDtypeStruct((n_all // out_block, m_all, out_block), out_dtype)
        out_spec = pl.BlockSpec((None, tm, tn), om(lambda i, j, k: (j // r, i, j % r)))

    def body(a_ref, b_ref, o_ref, *scratch):
        prod = lax.dot_general(a_ref[...], b_ref[...], _TN, preferred_element_type=F32)
        if nk == 1:
            o_ref[...] = prod.astype(o_ref.dtype)
        else:
            acc_ref = scratch[0]
            k = pl.program_id(2)

            @pl.when(k == 0)
            def _():
                acc_ref[...] = prod

            @pl.when(k > 0)
            def _():
                acc_ref[...] += prod

            @pl.when(k == nk - 1)
            def _():
                o_ref[...] = acc_ref[...].astype(o_ref.dtype)

    return pl.pallas_call(
        body, name=name, grid=(nn, nm, nk) if cols_outer else (nm, nn, nk),
        in_specs=[pl.BlockSpec((tk, tm), om(lambda i, j, k: (k, i))), b_spec],
        out_specs=out_spec, out_shape=out_shape,
        scratch_shapes=[pltpu.VMEM((tm, tn), F32)] if nk > 1 else [],
        compiler_params=_params(("parallel", "parallel", "arbitrary")),
    )(a, b)


def _rmsnorm_fwd(x, g, *, name, tm):
    t_all, d = x.shape

    def body(x_ref, g_ref, o_ref):
        xv = x_ref[...]
        rstd = lax.rsqrt(jnp.mean(xv * xv, axis=-1, keepdims=True) + RMS_EPS)
        o_ref[...] = (xv * rstd * g_ref[...]).astype(o_ref.dtype)

    return pl.pallas_call(
        body, name=name, grid=(t_all // tm,),
        in_specs=[pl.BlockSpec((tm, d), lambda i: (i, 0)), pl.BlockSpec((1, d), lambda i: (0, 0))],
        out_specs=pl.BlockSpec((tm, d), lambda i: (i, 0)),
        out_shape=jax.ShapeDtypeStruct((t_all, d), BF16),
        compiler_params=_params(("parallel",)),
    )(x, g)


def _rms_bwd_math(dy, xv, g):
    rstd = lax.rsqrt(jnp.mean(xv * xv, axis=-1, keepdims=True) + RMS_EPS)
    xhat = xv * rstd
    dxh = dy * g
    dx = rstd * (dxh - xhat * jnp.mean(dxh * xhat, axis=-1, keepdims=True))
    return dx, jnp.sum(dy * xhat, axis=0, keepdims=True)


def _rmsnorm_bwd(dy, x, g, res, *, name, tm):
    t_all, d = x.shape

    def body(dy_ref, x_ref, g_ref, res_ref, dx_ref, dxb_ref, dg_ref):
        dx, dg = _rms_bwd_math(dy_ref[...].astype(F32), x_ref[...], g_ref[...])
        tot = res_ref[...] + dx
        dx_ref[...] = tot
        dxb_ref[...] = tot.astype(BF16)

        @pl.when(pl.program_id(0) == 0)
        def _():
            dg_ref[...] = dg

        @pl.when(pl.program_id(0) > 0)
        def _():
            dg_ref[...] += dg

    row = pl.BlockSpec((tm, d), lambda i: (i, 0))
    vec = pl.BlockSpec((1, d), lambda i: (0, 0))
    return pl.pallas_call(
        body, name=name, grid=(t_all // tm,),
        in_specs=[row, row, vec, row], out_specs=[row, row, vec],
        out_shape=[jax.ShapeDtypeStruct((t_all, d), F32), jax.ShapeDtypeStruct((t_all, d), BF16),
                   jax.ShapeDtypeStruct((1, d), F32)],
        compiler_params=_params(("arbitrary",)),
    )(dy, x, g, res)


def _loss_head(h, target, g, *, name, tm, n_real):
    t_all, d = h.shape

    def body(h_ref, t_ref, g_ref, loss_ref, dx_ref, dxb_ref, dg_ref):
        i = pl.program_id(0)
        xv = h_ref[...]
        gv = g_ref[...]
        rstd = lax.rsqrt(jnp.mean(xv * xv, axis=-1, keepdims=True) + RMS_EPS)
        y = xv * rstd * gv
        row = i * tm + lax.broadcasted_iota(jnp.int32, (tm, 1), 0)
        valid = (row >= N_META) & (row < n_real)
        err = jnp.where(valid, y - t_ref[...], 0.0)
        part = 0.5 * jnp.sum(jnp.mean(err * err, axis=-1, keepdims=True), axis=0, keepdims=True)
        dx, dg = _rms_bwd_math(err * (1.0 / d), xv, gv)
        dx_ref[...] = dx
        dxb_ref[...] = dx.astype(BF16)

        @pl.when(i == 0)
        def _():
            dg_ref[...] = dg
            loss_ref[...] = jnp.broadcast_to(part, loss_ref.shape)

        @pl.when(i > 0)
        def _():
            dg_ref[...] += dg
            loss_ref[...] += jnp.broadcast_to(part, loss_ref.shape)

    row = pl.BlockSpec((tm, d), lambda i: (i, 0))
    vec = pl.BlockSpec((1, d), lambda i: (0, 0))
    return pl.pallas_call(
        body, name=name, grid=(t_all // tm,),
        in_specs=[row, row, vec],
        out_specs=[pl.BlockSpec((1, LANES), lambda i: (0, 0)), row, row, vec],
        out_shape=[jax.ShapeDtypeStruct((1, LANES), F32), jax.ShapeDtypeStruct((t_all, d), F32),
                   jax.ShapeDtypeStruct((t_all, d), BF16), jax.ShapeDtypeStruct((1, d), F32)],
        compiler_params=_params(("arbitrary",)),
    )(h, target, g)


def _ffn_up(x, w_gu, *, name, tm):
    t_all, d = x.shape
    g_all, kb, nb = w_gu.shape
    half = g_all // 2
    f = half * nb
    assert kb == d and t_all % tm == 0

    def body(x_ref, wg_ref, wu_ref, gu_ref, act_ref):
        xv = x_ref[...]
        gv = jnp.dot(xv, wg_ref[...], preferred_element_type=F32)
        uv = jnp.dot(xv, wu_ref[...], preferred_element_type=F32)
        gu_ref[0] = gv.astype(gu_ref.dtype)
        gu_ref[1] = uv.astype(gu_ref.dtype)
        act_ref[...] = (gv * _sigmoid(gv) * uv).astype(act_ref.dtype)

    return pl.pallas_call(
        body, name=name, grid=(half, t_all // tm),
        in_specs=[pl.BlockSpec((tm, d), lambda j, i: (i, 0)), pl.BlockSpec((None, d, nb), lambda j, i: (j, 0, 0)),
                  pl.BlockSpec((None, d, nb), lambda j, i: (j + half, 0, 0))],
        out_specs=[pl.BlockSpec((2, tm, nb), lambda j, i: (0, i, j)), pl.BlockSpec((tm, nb), lambda j, i: (i, j))],
        out_shape=[jax.ShapeDtypeStruct((2, t_all, f), BF16), jax.ShapeDtypeStruct((t_all, f), BF16)],
        compiler_params=_params(("parallel", "parallel")),
    )(x, w_gu, w_gu)


def _ffn_dact(dy, w_down, gu, *, name, tm, tn, after=None):
    t_all, d = dy.shape
    f = w_down.shape[0]
    assert t_all % tm == 0 and f % tn == 0
    ordered = [] if after is None else [after]

    def body(dy_ref, w_ref, gu_ref, *rest):
        o_ref = rest[-1]
        dact = lax.dot_general(dy_ref[...], w_ref[...], _NT, preferred_element_type=F32)
        gv, uv = gu_ref[0].astype(F32), gu_ref[1].astype(F32)
        sg = _sigmoid(gv)
        o_ref[0] = (dact * uv * (sg * (1.0 + gv * (1.0 - sg)))).astype(o_ref.dtype)
        o_ref[1] = (dact * gv * sg).astype(o_ref.dtype)

    return pl.pallas_call(
        body, name=name, grid=(f // tn, t_all // tm),
        in_specs=[pl.BlockSpec((tm, d), lambda j, i: (i, 0)), pl.BlockSpec((tn, d), lambda j, i: (j, 0)),
                  pl.BlockSpec((2, tm, tn), lambda j, i: (0, i, j))] + [pl.BlockSpec(memory_space=pl.ANY)] * len(ordered),
        out_specs=pl.BlockSpec((2, tm, tn), lambda j, i: (0, i, j)),
        out_shape=jax.ShapeDtypeStruct((2, t_all, f), BF16),
        compiler_params=_params(("parallel", "parallel")),
    )(dy, w_down, gu, *ordered)


def _mla_prep_fwd(proj, qn, kvn, cos, sin, *, name, tm, lq, lkv):
    t_all, w = proj.shape

    def body(p_ref, qn_ref, kvn_ref, cos_ref, sin_ref, cq_ref, ckv_ref, kr_ref):
        pv = p_ref[...]
        xq = pv[:, :lq]
        xkv = pv[:, lq:lq + lkv]
        cq_ref[...] = (xq * lax.rsqrt(jnp.mean(xq * xq, axis=-1, keepdims=True) + RMS_EPS) * qn_ref[...]).astype(BF16)
        ckv_ref[...] = (xkv * lax.rsqrt(jnp.mean(xkv * xkv, axis=-1, keepdims=True) + RMS_EPS) * kvn_ref[...]).astype(BF16)
        kr_ref[...] = _rope(pv[:, lq + lkv:], cos_ref[...], sin_ref[...]).astype(BF16)

    def row(width):
        return pl.BlockSpec((tm, width), lambda i: (i, 0))

    def vec(width):
        return pl.BlockSpec((1, width), lambda i: (0, 0))

    return pl.pallas_call(
        body, name=name, grid=(t_all // tm,),
        in_specs=[row(w), vec(lq), vec(lkv), row(LANES), row(LANES)],
        out_specs=[row(lq), row(lkv), row(LANES)],
        out_shape=[jax.ShapeDtypeStruct((t_all, lq), BF16), jax.ShapeDtypeStruct((t_all, lkv), BF16),
                   jax.ShapeDtypeStruct((t_all, LANES), BF16)],
        compiler_params=_params(("parallel",)),
    )(proj, qn, kvn, cos, sin)


def _mla_prep_bwd(dcq, dckv, dkr_h, proj, qn, kvn, cos, sin, *, name, tm, lq, lkv):
    t_all, w = proj.shape
    n_heads = dkr_h.shape[0]

    def body(dcq_ref, dckv_ref, dkr_ref, p_ref, qn_ref, kvn_ref, cos_ref, sin_ref, dp_ref, dqn_ref, dkvn_ref):
        pv = p_ref[...]
        dxq, dqn = _rms_bwd_math(dcq_ref[...], pv[:, :lq], qn_ref[...])
        dxkv, dkvn = _rms_bwd_math(dckv_ref[...], pv[:, lq:lq + lkv], kvn_ref[...])
        dkr = dkr_ref[0]
        for hh in range(1, n_heads):
            dkr = dkr + dkr_ref[hh]
        dkr = _unrope(dkr, cos_ref[...], sin_ref[...])
        dp_ref[...] = jnp.concatenate([dxq, dxkv, dkr], axis=1).astype(BF16)

        @pl.when(pl.program_id(0) == 0)
        def _():
            dqn_ref[...] = dqn
            dkvn_ref[...] = dkvn

        @pl.when(pl.program_id(0) > 0)
        def _():
            dqn_ref[...] += dqn
            dkvn_ref[...] += dkvn

    def row(width):
        return pl.BlockSpec((tm, width), lambda i: (i, 0))

    def vec(width):
        return pl.BlockSpec((1, width), lambda i: (0, 0))

    return pl.pallas_call(
        body, name=name, grid=(t_all // tm,),
        in_specs=[row(lq), row(lkv), pl.BlockSpec((n_heads, tm, LANES), lambda i: (0, i, 0)), row(w),
                  vec(lq), vec(lkv), row(LANES), row(LANES)],
        out_specs=[row(w), vec(lq), vec(lkv)],
        out_shape=[jax.ShapeDtypeStruct((t_all, w), BF16), jax.ShapeDtypeStruct((1, lq), F32),
                   jax.ShapeDtypeStruct((1, lkv), F32)],
        compiler_params=_params(("arbitrary",)),
    )(dcq, dckv, dkr_h, proj, qn, kvn, cos, sin)


def _rope_q_epilogue(acc, cos_ref, sin_ref):
    parts = []
    for g in range(acc.shape[1] // LANES):
        blk = acc[:, g * LANES:(g + 1) * LANES]
        parts.append(_rope(blk, cos_ref[...], sin_ref[...]) if g % 2 == 1 else blk)
    return jnp.concatenate(parts, axis=1)


def _chunk_causal(rows, cols, row0=0):
    r = row0 + lax.broadcasted_iota(jnp.int32, (rows, cols), 0)
    c = lax.broadcasted_iota(jnp.int32, (rows, cols), 1)
    return (c >> 6) <= (r >> 6)


def _meta_keys(rows, cols):
    return lax.broadcasted_iota(jnp.int32, (rows, cols), 1) < N_META


def _attn_fwd(q, kv, kr, *, name, n_heads, tq, n_real, scale):
    t_all = q.shape[0]
    nq = (n_real - N_META) // tq
    assert N_META + nq * tq == n_real and tq % CHUNK == 0 and t_all >= LANES
    n_pad = t_all - n_real
    sub = tq // 2 if (tq // 2) % CHUNK == 0 else tq

    def body(q_ref, kv_ref, kr_ref, o_ref, lse_ref, k_scr, m_scr, l_scr, acc_scr):
        k_scr[:, :QK_NOPE] = kv_ref[:, :QK_NOPE]
        k_scr[:, QK_NOPE:] = kr_ref[...]
        if n_pad:
            o_ref[pl.ds(n_real, n_pad), :] = jnp.zeros((n_pad, V_HEAD), o_ref.dtype)
            lse_ref[pl.ds(n_real, n_pad), :] = jnp.zeros((n_pad, LANES), F32)

        def scores(qt, c0, width):
            return lax.dot_general(qt, k_scr[pl.ds(c0, width), :], _NT, preferred_element_type=F32) * scale

        def values(c0, width):
            return kv_ref[pl.ds(c0, width), QK_NOPE:]

        s = jnp.where(_meta_keys(LANES, LANES), scores(q_ref[pl.ds(0, LANES), :], 0, LANES), NEG_BIG)
        m = jnp.max(s, axis=-1, keepdims=True)
        p = jnp.exp(s - m)
        l = jnp.sum(p, axis=-1, keepdims=True)
        o_meta = jnp.dot(p.astype(BF16), values(0, LANES), preferred_element_type=F32) / l
        o_ref[pl.ds(0, N_META), :] = o_meta[:N_META].astype(o_ref.dtype)
        lse_ref[pl.ds(0, N_META), :] = jnp.broadcast_to((m + jnp.log(l))[:N_META], (N_META, LANES))

        parts = [(u * sub, sub) for u in range(tq // sub)]

        def accumulate(u0, s, vals):
            rows = pl.ds(u0, s.shape[0])
            m_prev = m_scr[rows, :]
            m_new = jnp.maximum(m_prev, jnp.max(s, axis=-1, keepdims=True))
            alpha = jnp.exp(m_prev - m_new)
            p = jnp.exp(s - m_new)
            l_scr[rows, :] = alpha * l_scr[rows, :] + jnp.sum(p, axis=-1, keepdims=True)
            acc_scr[rows, :] = alpha * acc_scr[rows, :] + jnp.dot(p.astype(BF16), vals, preferred_element_type=F32)
            m_scr[rows, :] = m_new

        def q_tile(i, carry):
            r0 = pl.multiple_of(N_META + i * tq, N_META)
            qts = [q_ref[pl.ds(r0 + u0, rows), :] for u0, rows in parts]
            m_scr[...] = jnp.full(m_scr.shape, NEG_BIG, F32)
            l_scr[...] = jnp.zeros(l_scr.shape, F32)
            acc_scr[...] = jnp.zeros(acc_scr.shape, F32)

            def full_blocks(j, width):
                c0 = pl.multiple_of(N_META + j * tq, N_META)
                for (u0, _), qt in zip(parts, qts):
                    accumulate(u0, scores(qt, c0, width), values(c0, width))

            def two_blocks(jj, c):
                full_blocks(2 * jj, 2 * tq)
                return c

            lax.fori_loop(0, i // 2, two_blocks, 0)

            @pl.when(i % 2 == 1)
            def _():
                full_blocks(i - 1, tq)

            for (u0, rows), qt in zip(parts, qts):
                width = u0 + rows
                s = jnp.concatenate([jnp.where(_meta_keys(rows, LANES), scores(qt, 0, LANES), NEG_BIG),
                                     jnp.where(_chunk_causal(rows, width, u0), scores(qt, r0, width), NEG_BIG)], axis=1)
                accumulate(u0, s, jnp.concatenate([values(0, LANES), values(r0, width)], axis=0))
            o_ref[pl.ds(r0, tq), :] = (acc_scr[...] / l_scr[...]).astype(o_ref.dtype)
            lse_ref[pl.ds(r0, tq), :] = jnp.broadcast_to(m_scr[...] + jnp.log(l_scr[...]), (tq, LANES))
            return carry

        lax.fori_loop(0, nq, q_tile, 0)

    def head(width):
        return pl.BlockSpec((t_all, width), lambda h: (0, h))

    return pl.pallas_call(
        body, name=name, grid=(n_heads,),
        in_specs=[head(HEAD_W), head(HEAD_W), pl.BlockSpec((t_all, LANES), lambda h: (0, 0))],
        out_specs=[head(V_HEAD), pl.BlockSpec((None, t_all, LANES), lambda h: (h, 0, 0))],
        out_shape=[jax.ShapeDtypeStruct((t_all, n_heads * V_HEAD), BF16),
                   jax.ShapeDtypeStruct((n_heads, t_all, LANES), F32)],
        scratch_shapes=[pltpu.VMEM((t_all, HEAD_W), BF16), pltpu.VMEM((tq, 1), F32), pltpu.VMEM((tq, 1), F32),
                        pltpu.VMEM((tq, V_HEAD), F32)],
        compiler_params=_params(("parallel",)),
    )(q, kv, kr)


def _attn_bwd(q, kv, kr, o, lse, do, cos, sin, *, name, n_heads, tq, n_real, scale):
    t_all = q.shape[0]
    nq = (n_real - N_META) // tq
    assert N_META + nq * tq == n_real and tq % CHUNK == 0 and t_all >= LANES
    n_pad = t_all - n_real

    def body(q_ref, kv_ref, kr_ref, o_ref, lse_ref, do_ref, cos_ref, sin_ref, dq_ref, dkv_ref, dkr_ref,
             k_scr, dk_scr, dv_scr, dq_scr):
        k_scr[:, :QK_NOPE] = kv_ref[:, :QK_NOPE]
        k_scr[:, QK_NOPE:] = kr_ref[...]
        dk_scr[...] = jnp.zeros(dk_scr.shape, F32)
        dv_scr[...] = jnp.zeros(dv_scr.shape, F32)
        if n_pad:
            dq_ref[pl.ds(n_real, n_pad), :] = jnp.zeros((n_pad, HEAD_W), dq_ref.dtype)

        def blocks(qt, dot, lse_t, delta, segments):
            kb = jnp.concatenate([k_scr[pl.ds(c0, w), :] for c0, w, _ in segments], axis=0)
            vb = jnp.concatenate([kv_ref[pl.ds(c0, w), QK_NOPE:] for c0, w, _ in segments], axis=0)
            s = lax.dot_general(qt, kb, _NT, preferred_element_type=F32) * scale
            p = jnp.exp(s - lse_t)
            if any(m is not None for _, _, m in segments):
                rows = qt.shape[0]
                mask = jnp.concatenate([jnp.ones((rows, w), jnp.bool_) if m is None else m for _, w, m in segments], axis=1)
                p = jnp.where(mask, p, 0.0)
            dp = lax.dot_general(dot, vb, _NT, preferred_element_type=F32)
            ds = (p * (dp - delta) * scale).astype(BF16)
            dv = lax.dot_general(p.astype(BF16), dot, _TN, preferred_element_type=F32)
            dk = lax.dot_general(ds, qt, _TN, preferred_element_type=F32)
            at = 0
            for c0, w, _ in segments:
                dv_scr[pl.ds(c0, w), :] += dv[at:at + w]
                dk_scr[pl.ds(c0, w), :] += dk[at:at + w]
                at += w
            return jnp.dot(ds, kb, preferred_element_type=F32)

        def block(qt, dot, lse_t, delta, c0, width, mask):
            return blocks(qt, dot, lse_t, delta, [(c0, width, mask)])

        def write_dq(r0, rows, dq):
            cs, sn = cos_ref[pl.ds(r0, rows), :], sin_ref[pl.ds(r0, rows), :]
            dq_ref[pl.ds(r0, rows), :] = jnp.concatenate(
                [dq[:, :QK_NOPE], _unrope(dq[:, QK_NOPE:], cs, sn)], axis=1).astype(dq_ref.dtype)

        rows_m = lax.broadcasted_iota(jnp.int32, (LANES, LANES), 0) < N_META
        dot = do_ref[pl.ds(0, LANES), :]
        delta = jnp.sum(dot.astype(F32) * o_ref[pl.ds(0, LANES), :].astype(F32), axis=-1, keepdims=True)
        dq = block(q_ref[pl.ds(0, LANES), :], dot, lse_ref[pl.ds(0, LANES), :1], delta, 0, LANES,
                   _meta_keys(LANES, LANES) & rows_m)
        write_dq(0, N_META, dq[:N_META])

        def q_tile(i, carry):
            r0 = pl.multiple_of(N_META + i * tq, N_META)
            qt = q_ref[pl.ds(r0, tq), :]
            dot = do_ref[pl.ds(r0, tq), :]
            lse_t = lse_ref[pl.ds(r0, tq), :1]
            delta = jnp.sum(dot.astype(F32) * o_ref[pl.ds(r0, tq), :].astype(F32), axis=-1, keepdims=True)
            dq_scr[...] = blocks(qt, dot, lse_t, delta, [(0, LANES, _meta_keys(tq, LANES)), (r0, tq, _chunk_causal(tq, tq))])

            def two_blocks(jj, c):
                c0 = pl.multiple_of(N_META + 2 * jj * tq, N_META)
                dq_scr[...] += block(qt, dot, lse_t, delta, c0, 2 * tq, None)
                return c

            lax.fori_loop(0, i // 2, two_blocks, 0)

            @pl.when(i % 2 == 1)
            def _():
                c0 = pl.multiple_of(N_META + (i - 1) * tq, N_META)
                dq_scr[...] += block(qt, dot, lse_t, delta, c0, tq, None)

            write_dq(r0, tq, dq_scr[...])
            return carry

        lax.fori_loop(0, nq, q_tile, 0)
        dk = dk_scr[...]
        dkv_ref[...] = jnp.concatenate([dk[:, :QK_NOPE], dv_scr[...]], axis=1).astype(dkv_ref.dtype)
        dkr_ref[...] = dk[:, QK_NOPE:]

    def head(width):
        return pl.BlockSpec((t_all, width), lambda h: (0, h))

    table = pl.BlockSpec((t_all, LANES), lambda h: (0, 0))
    per_head = pl.BlockSpec((None, t_all, LANES), lambda h: (h, 0, 0))
    return pl.pallas_call(
        body, name=name, grid=(n_heads,),
        in_specs=[head(HEAD_W), head(HEAD_W), table, head(V_HEAD), per_head, head(V_HEAD), table, table],
        out_specs=[head(HEAD_W), head(HEAD_W), per_head],
        out_shape=[jax.ShapeDtypeStruct((t_all, n_heads * HEAD_W), BF16), jax.ShapeDtypeStruct((t_all, n_heads * HEAD_W), BF16),
                   jax.ShapeDtypeStruct((n_heads, t_all, LANES), F32)],
        scratch_shapes=[pltpu.VMEM((t_all, HEAD_W), BF16), pltpu.VMEM((t_all, HEAD_W), F32), pltpu.VMEM((t_all, V_HEAD), F32),
                        pltpu.VMEM((tq, HEAD_W), F32)],
        compiler_params=_params(("parallel",)),
    )(q, kv, kr, o, lse, do, cos, sin)


LRU_ROWS = (384, 256, 128)


def _shifted_back(ref, t0, rows, shift_max):
    main = ref[pl.ds(t0, rows), :]
    prev = ref[pl.ds(pl.multiple_of(jnp.maximum(t0 - SUBLANES, 0), SUBLANES), SUBLANES), :]
    prev = jnp.where(t0 > 0, prev, 0.0)
    ext = jnp.concatenate([prev, main], axis=0)
    return [main] + [pltpu.roll(ext, s, 0)[SUBLANES:, :] for s in range(1, shift_max + 1)]


def _shifted_ahead(ref, t0, rows, t_all, shift_max):
    main = ref[pl.ds(t0, rows), :]
    nxt = ref[pl.ds(pl.multiple_of(jnp.minimum(t0 + rows, t_all - SUBLANES), SUBLANES), SUBLANES), :]
    nxt = jnp.where(t0 + rows < t_all, nxt, 0.0)
    ext = jnp.concatenate([main, nxt], axis=0)
    return [main] + [pltpu.roll(ext, rows + SUBLANES - s, 0)[:rows, :] for s in range(1, shift_max + 1)]


def _conv_fwd(xp_ref, t0, rows, cw, cb):
    sh = _shifted_back(xp_ref, t0, rows, 3)
    out = cb + cw[3:4, :] * sh[0]
    for k in range(3):
        out = out + cw[k:k + 1, :] * sh[3 - k]
    return out, sh


def _lru_gates(xb, wga, bga, wgx, bgx, sp):
    xbb = xb.astype(BF16)
    r = _sigmoid(jnp.dot(xbb, wga, preferred_element_type=F32) + bga)
    ig = _sigmoid(jnp.dot(xbb, wgx, preferred_element_type=F32) + bgx)
    la = -LRU_C * r * sp
    a = jnp.exp(la)
    s = jnp.sqrt(_neg_expm1(2.0 * la))
    return xbb, r, ig, a, s


def _scan_tile(a, b, reverse):
    rows = a.shape[0]
    ridx = lax.broadcasted_iota(jnp.int32, a.shape, 0)
    s = 1
    while s < rows:
        if reverse:
            keep = ridx < rows - s
            a_sh, b_sh = pltpu.roll(a, rows - s, 0), pltpu.roll(b, rows - s, 0)
        else:
            keep = ridx >= s
            a_sh, b_sh = pltpu.roll(a, s, 0), pltpu.roll(b, s, 0)
        b = jnp.where(keep, a * b_sh + b, b)
        a = jnp.where(keep, a * a_sh, a)
        s *= 2
    return a, b


def _lru_fwd(xy, conv_w, conv_b, wga, bga, wgx, bgx, lam, *, name):
    t_all = xy.shape[0]
    dr = xy.shape[1] // 2
    c = LANES
    nblk = dr // c
    rows = _pick(t_all, LRU_ROWS)
    nt = t_all // rows

    def body(xp_ref, yp_ref, cw_ref, cb_ref, wga_ref, bga_ref, wgx_ref, bgx_ref, lam_ref, hs_ref, hsy_ref):
        cw, cb = cw_ref[...], cb_ref[...]
        sp = _softplus_neg(lam_ref[...])

        def tile(t, h_in):
            t0 = pl.multiple_of(t * rows, rows)
            xb, _ = _conv_fwd(xp_ref, t0, rows, cw, cb)
            _, _, ig, a, s = _lru_gates(xb, wga_ref[0], bga_ref[...], wgx_ref[0], bgx_ref[...], sp)
            cum_a, h0 = _scan_tile(a, s * (ig * xb), reverse=False)
            hs = cum_a * h_in + h0
            hs_ref[pl.ds(t0, rows), :] = hs
            hsy_ref[pl.ds(t0, rows), :] = (hs * _gelu(yp_ref[pl.ds(t0, rows), :])).astype(BF16)
            return hs[rows - 1:, :]

        lax.fori_loop(0, nt, tile, jnp.zeros((1, c), F32))

    col = pl.BlockSpec((t_all, c), lambda b: (0, b))
    vec = pl.BlockSpec((1, c), lambda b: (0, b))
    wsp = pl.BlockSpec((1, c, c), lambda b: (b, 0, 0))
    return pl.pallas_call(
        body, name=name, grid=(nblk,),
        in_specs=[col, pl.BlockSpec((t_all, c), lambda b: (0, nblk + b)), pl.BlockSpec((4, c), lambda b: (0, b)), vec,
                  wsp, vec, wsp, vec, vec],
        out_specs=[col, col],
        out_shape=[jax.ShapeDtypeStruct((t_all, dr), F32), jax.ShapeDtypeStruct((t_all, dr), BF16)],
        compiler_params=_params(("parallel",)),
    )(xy, xy, conv_w, conv_b, wga, bga, wgx, bgx, lam)


def _lru_bwd(xy, hs, dhsy, conv_w, conv_b, wga, bga, wgx, bgx, lam, *, name):
    t_all = xy.shape[0]
    dr = xy.shape[1] // 2
    c = LANES
    nblk = dr // c
    rows = _pick(t_all, LRU_ROWS)
    nt = t_all // rows

    def body(xp_ref, yp_ref, hs_ref, dh_ref, cw_ref, cb_ref, wga_ref, bga_ref, wgx_ref, bgx_ref, lam_ref,
             dxp_ref, dyp_ref, dcw_ref, dcb_ref, dwga_ref, dbga_ref, dwgx_ref, dbgx_ref, dlam_ref,
             xb_scr, r_scr, i_scr, a_scr):
        cw, cb = cw_ref[...], cb_ref[...]
        lamv = lam_ref[...]
        sp = _softplus_neg(lamv)
        sig_neg = 1.0 / (1.0 + jnp.exp(lamv))
        wga_v, wgx_v = wga_ref[0], wgx_ref[0]

        def recompute(t, carry):
            t0 = pl.multiple_of(t * rows, rows)
            xb, _ = _conv_fwd(xp_ref, t0, rows, cw, cb)
            _, r, ig, a, _ = _lru_gates(xb, wga_v, bga_ref[...], wgx_v, bgx_ref[...], sp)
            xb_scr[pl.ds(t0, rows), :] = xb
            r_scr[pl.ds(t0, rows), :] = r
            i_scr[pl.ds(t0, rows), :] = ig
            a_scr[pl.ds(t0, rows), :] = a
            return carry

        lax.fori_loop(0, nt, recompute, 0)
        dwga_ref[...] = jnp.zeros(dwga_ref.shape, F32)
        dwgx_ref[...] = jnp.zeros(dwgx_ref.shape, F32)

        def tile(ti, carry):
            lam_in, dbga, dbgx, dlam, dcw, dcb = carry
            t = nt - 1 - ti
            t0 = pl.multiple_of(t * rows, rows)
            a_now, a_next = _shifted_ahead(a_scr, t0, rows, t_all, 1)
            yp = yp_ref[pl.ds(t0, rows), :]
            dhy = dh_ref[pl.ds(t0, rows), :]
            cum_a, lam0 = _scan_tile(a_next, dhy * _gelu(yp), reverse=True)
            lam_t = cum_a * lam_in + lam0
            hs_now, hs_prev = _shifted_back(hs_ref, t0, rows, 1)
            da = lam_t * hs_prev
            xb = xb_scr[pl.ds(t0, rows), :]
            r = r_scr[pl.ds(t0, rows), :]
            ig = i_scr[pl.ds(t0, rows), :]
            la = -LRU_C * r * sp
            s = jnp.sqrt(_neg_expm1(2.0 * la))
            d_ixb = lam_t * s
            dla = da * a_now - (lam_t * ig * xb) * (a_now * a_now / s)
            dzr = dla * (-LRU_C * sp) * r * (1.0 - r)
            dzi = d_ixb * xb * ig * (1.0 - ig)
            dzr_b, dzi_b = dzr.astype(BF16), dzi.astype(BF16)
            xbb = xb.astype(BF16)
            dwga_ref[0] += lax.dot_general(xbb, dzr_b, _TN, preferred_element_type=F32)
            dwgx_ref[0] += lax.dot_general(xbb, dzi_b, _TN, preferred_element_type=F32)
            dxb = (d_ixb * ig + lax.dot_general(dzr_b, wga_v, _NT, preferred_element_type=F32)
                   + lax.dot_general(dzi_b, wgx_v, _NT, preferred_element_type=F32))
            xb_scr[pl.ds(t0, rows), :] = dxb
            dyp_ref[pl.ds(t0, rows), :] = (dhy * hs_now * _gelu_grad(yp)).astype(BF16)
            ahead = _shifted_ahead(xb_scr, t0, rows, t_all, 3)
            dxp = cw[3:4, :] * ahead[0]
            for k in range(3):
                dxp = dxp + cw[k:k + 1, :] * ahead[3 - k]
            dxp_ref[pl.ds(t0, rows), :] = dxp.astype(BF16)
            back = _shifted_back(xp_ref, t0, rows, 3)
            dcw_t = jnp.concatenate([jnp.sum(dxb * back[3 - k], axis=0, keepdims=True) for k in range(4)], axis=0)
            return (lam_t[:1, :], dbga + jnp.sum(dzr, axis=0, keepdims=True), dbgx + jnp.sum(dzi, axis=0, keepdims=True),
                    dlam + jnp.sum(dla * r, axis=0, keepdims=True), dcw + dcw_t, dcb + jnp.sum(dxb, axis=0, keepdims=True))

        zero = jnp.zeros((1, c), F32)
        _, dbga, dbgx, dlam, dcw, dcb = lax.fori_loop(0, nt, tile, (zero, zero, zero, zero, jnp.zeros((4, c), F32), zero))
        dbga_ref[...] = dbga
        dbgx_ref[...] = dbgx
        dlam_ref[...] = dlam * (LRU_C * sig_neg)
        dcw_ref[...] = dcw
        dcb_ref[...] = dcb

    col = pl.BlockSpec((t_all, c), lambda b: (0, b))
    col2 = pl.BlockSpec((t_all, c), lambda b: (0, nblk + b))
    vec = pl.BlockSpec((1, c), lambda b: (0, b))
    tap = pl.BlockSpec((4, c), lambda b: (0, b))
    wsp = pl.BlockSpec((1, c, c), lambda b: (b, 0, 0))
    vshape = jax.ShapeDtypeStruct((1, dr), F32)
    wshape = jax.ShapeDtypeStruct((nblk, c, c), F32)
    def planes_body(*refs):
        dxy_ref = refs[11]
        body(*refs[:11], dxy_ref.at[0], dxy_ref.at[1], *refs[12:])

    return pl.pallas_call(
        planes_body, name=name, grid=(nblk,),
        in_specs=[col, col2, col, col, tap, vec, wsp, vec, wsp, vec, vec],
        out_specs=[pl.BlockSpec((2, t_all, c), lambda b: (0, 0, b)), tap, vec, wsp, vec, wsp, vec, vec],
        out_shape=[jax.ShapeDtypeStruct((2, t_all, dr), BF16),
                   jax.ShapeDtypeStruct((4, dr), F32), vshape, wshape, vshape, wshape, vshape, vshape],
        scratch_shapes=[pltpu.VMEM((t_all, c), F32)] * 4,
        compiler_params=_params(("parallel",)),
    )(xy, xy, hs, dhsy, conv_w, conv_b, wga, bga, wgx, bgx, lam)


def _mesh_pos():
    return lax.axis_index("x"), lax.axis_index("y"), lax.axis_index("c")


def _all_gather(shards, *, name):
    n = len(shards)

    def body(*refs):
        ins, outs, token = refs[:n], refs[n:2 * n], refs[2 * n]
        send_sems, recv_sems, local_sems = refs[2 * n + 1:]
        token[...] = jnp.zeros(token.shape, token.dtype)
        x, y, c = _mesh_pos()
        me, sibling = (x, y, c), (x, y, 1 - c)
        chips = [(1 - x, y), (x, 1 - y), (1 - x, 1 - y)]
        slot = _slot

        def copy(a, k, block, to, src=None):
            dst = outs[a].at[slot(block)]
            return pltpu.make_async_remote_copy(
                src_ref=dst if src is None else src, dst_ref=dst, send_sem=send_sems.at[a, k],
                recv_sem=recv_sems.at[a, k], device_id=to, device_id_type=MESH)

        mine = [pltpu.make_async_copy(ins[a], outs[a].at[slot(me)], local_sems.at[a]) for a in range(n)]
        for cp in mine:
            cp.start()
        first = []
        for a in range(n):
            first.append(copy(a, 0, me, sibling, src=ins[a]))
            first += [copy(a, 1 + j, me, (*chip, c), src=ins[a]) for j, chip in enumerate(chips)]
        for cp in first:
            cp.start()
        passed = []
        for a in range(n):
            for j, chip in enumerate(chips):
                copy(a, 1 + j, (*chip, c), me).wait_recv()
                fwd = copy(a, 4 + j, (*chip, c), sibling)
                fwd.start()
                passed.append(fwd)
        for a in range(n):
            copy(a, 0, sibling, me).wait_recv()
            for j, chip in enumerate(chips):
                copy(a, 4 + j, (*chip, 1 - c), me).wait_recv()
        for cp in first + passed:
            cp.wait_send()
        for cp in mine:
            cp.wait()

    any_spec = pl.BlockSpec(memory_space=pl.ANY)
    outs = pl.pallas_call(
        body, name=name,
        in_specs=[any_spec] * n, out_specs=[any_spec] * n + [pl.BlockSpec(memory_space=pltpu.VMEM)],
        out_shape=[jax.ShapeDtypeStruct((N_DEV,) + s.shape, s.dtype) for s in shards]
        + [jax.ShapeDtypeStruct((SUBLANES, LANES), F32)],
        scratch_shapes=[pltpu.SemaphoreType.DMA((n, 7)), pltpu.SemaphoreType.DMA((n, 7)), pltpu.SemaphoreType.DMA((n,))],
    )(*shards)
    return list(outs[:n]), outs[n][0, 0]


_HBM = pl.BlockSpec(memory_space=pltpu.HBM)
_SEM = pl.BlockSpec(memory_space=pltpu.SEMAPHORE)
_ANY = pl.BlockSpec(memory_space=pl.ANY)
_EFFECT = pltpu.SideEffectType.DATAFLOW_SIDE_EFFECTING


def _slot(p):
    return 4 * p[0] + 2 * p[1] + p[2]


def _remote(src, dst, send, recv, idx, to):
    return pltpu.make_async_remote_copy(src_ref=src, dst_ref=dst, send_sem=send.at[idx], recv_sem=recv.at[idx],
                                        device_id=to, device_id_type=MESH)


def _ag_plan_own(a, src, land, send, recv):
    x, y, c = _mesh_pos()
    dst = land.at[_slot((x, y, c))]
    targets = [(x, y, 1 - c), (1 - x, y, c), (x, 1 - y, c), (1 - x, 1 - y, c)]
    return [_remote(src, dst, send, recv, 4 * a + k, to) for k, to in enumerate(targets)]


def _ag_plan_pass(a, src, land, send, recv):
    x, y, c = _mesh_pos()
    blocks = [land.at[_slot((px, py, c))] for px, py in ((1 - x, y), (x, 1 - y), (1 - x, 1 - y))]
    return [_remote(blk, blk, send, recv, 3 * a + k, (x, y, 1 - c)) for k, blk in enumerate(blocks)]


def _rs_plan_sibling(a, src, land, send, recv):
    x, y, c = _mesh_pos()
    return [_remote(src.at[2 * j + (1 - c)], land.at[j], send, recv, 4 * a + j, (x, y, 1 - c)) for j in range(4)]


def _rs_plan_chips(a, src, land, send, recv):
    x, y, c = _mesh_pos()
    out = []
    for k in (1, 2, 3):
        px = 1 - x if k & 2 else x
        py = 1 - y if k & 1 else y
        out.append(_remote(src.at[2 * px + py], land.at[k - 1], send, recv, 3 * a + k - 1, (px, py, c)))
    return out


def _in_hbm(a):
    return pltpu.with_memory_space_constraint(a, pltpu.HBM)


def _exchange_start(srcs, lands, plan, n_k, *, name):
    ns, n = len(srcs), len(lands)

    def body(*refs):
        src_refs, land_refs = refs[:ns], refs[ns:ns + n]
        send, recv = refs[ns + n], refs[ns + n + 1]
        token = refs[-1]
        for a in range(n):
            for cp in plan(a, src_refs[a] if ns else None, land_refs[a], send, recv):
                cp.start()
        token[...] = jnp.zeros(token.shape, token.dtype)

    bufs = list(srcs) + list(lands)
    outs = pl.pallas_call(
        body, name=name,
        out_shape=(pltpu.SemaphoreType.DMA((n * n_k,)), pltpu.SemaphoreType.DMA((n * n_k,)),
                   *[pltpu.HBM(b.shape, b.dtype) for b in bufs], jax.ShapeDtypeStruct((SUBLANES, LANES), F32)),
        in_specs=[_HBM] * (ns + n),
        out_specs=(_SEM, _SEM, *[_HBM] * (ns + n), pl.BlockSpec(memory_space=pltpu.VMEM)),
        input_output_aliases={i: 2 + i for i in range(ns + n)},
        compiler_params=pltpu.CompilerParams(has_side_effects=_EFFECT),
    )(*[_in_hbm(b) for b in bufs])
    return outs[0], outs[1], list(outs[2:2 + ns]), list(outs[2 + ns:2 + ns + n]), outs[-1]


def _exchange_wait(started, plan, after, *, name):
    send, recv, srcs, lands, _ = started
    ns, n = len(srcs), len(lands)

    def body(*refs):
        src_refs, land_refs = refs[:ns], refs[ns:ns + n]
        send_ref, recv_ref = refs[ns + n], refs[ns + n + 1]
        for a in range(n):
            for cp in plan(a, src_refs[a] if ns else None, land_refs[a], send_ref, recv_ref):
                cp.wait_send()
                cp.wait_recv()

    bufs = list(srcs) + list(lands)
    outs = pl.pallas_call(
        body, name=name,
        out_shape=tuple(pltpu.HBM(b.shape, b.dtype) for b in bufs),
        in_specs=[_HBM] * (ns + n) + [_SEM, _SEM, _ANY],
        out_specs=tuple([_HBM] * (ns + n)),
        input_output_aliases={i: i for i in range(ns + n)},
        compiler_params=pltpu.CompilerParams(has_side_effects=_EFFECT),
    )(*bufs, send, recv, after)
    return list(outs[:ns]), list(outs[ns:])


def _pair_add(grads, landed, core, *, name, tr):
    _, r_all, c_all = grads.shape

    def body(core_ref, g_ref, l_ref, o_ref):
        o_ref[...] = (g_ref[...].astype(F32) + l_ref[...].astype(F32)).astype(o_ref.dtype)

    return pl.pallas_call(
        body, name=name,
        grid_spec=pltpu.PrefetchScalarGridSpec(
            num_scalar_prefetch=1, grid=(4, r_all // tr),
            in_specs=[pl.BlockSpec((None, tr, c_all), lambda j, i, core_ref: (2 * j + core_ref[0], i, 0)),
                      pl.BlockSpec((None, tr, c_all), lambda j, i, core_ref: (j, i, 0))],
            out_specs=pl.BlockSpec((None, tr, c_all), lambda j, i, core_ref: (j, i, 0))),
        out_shape=jax.ShapeDtypeStruct((4, r_all, c_all), grads.dtype),
        compiler_params=_params(("parallel", "parallel")),
    )(core, grads, landed)


def _adamw_math(w, g, m, v):
    m2 = ADAM_B1 * m + (1.0 - ADAM_B1) * g
    v2 = ADAM_B2 * v + (1.0 - ADAM_B2) * (g * g)
    m_hat = m2 / (1.0 - ADAM_B1 ** ADAM_STEP)
    v_hat = v2 / (1.0 - ADAM_B2 ** ADAM_STEP)
    delta = -ADAM_LR * (m_hat / (jnp.sqrt(v_hat) + ADAM_EPS) + ADAM_WD * w)
    return delta, m2, v2


def _adamw(w, m, v, terms, order, *, name, tr, col_block=None, own=None, stack=None):
    r_all, c_all = w.shape
    n_slots = terms.shape[0]

    def body(*refs):
        if col_block is not None or own is not None:
            refs = refs[1:]
        own_ref = None
        if own is not None:
            own_ref, refs = refs[0], refs[1:]
        w_ref, m_ref, v_ref, t_ref, g_ref, d_ref, m2_ref, v2_ref = refs
        if own_ref is not None:
            g = own_ref[...].astype(F32) + t_ref[order[0]].astype(F32)
        else:
            g = t_ref[order[0]].astype(F32)
        for s in order[1:]:
            g = g + t_ref[s].astype(F32)
        delta, m2, v2 = _adamw_math(w_ref[...], g, m_ref[...], v_ref[...])
        g_ref[...] = g
        d_ref[...] = delta
        m2_ref[...] = m2
        v2_ref[...] = v2

    shape = jax.ShapeDtypeStruct((r_all, c_all), F32)
    if own is not None:
        layer, n_layers, prev = stack
        row = pl.BlockSpec((tr, c_all), lambda i, idx: (i, 0))
        slab = pl.BlockSpec((None, tr, c_all), lambda i, idx: (layer, i, 0))
        carried = [] if prev is None else list(prev)

        def stacked_body(*refs):
            body(*refs[:6], *refs[6 + len(carried):])

        return pl.pallas_call(
            stacked_body, name=name,
            grid_spec=pltpu.PrefetchScalarGridSpec(
                num_scalar_prefetch=1, grid=(r_all // tr,),
                in_specs=[pl.BlockSpec((None, tr, c_all), lambda i, idx: (idx[0], i, 0)), row, row, row,
                          pl.BlockSpec((n_slots, tr, c_all), lambda i, idx: (0, i, 0))] + [_ANY] * len(carried),
                out_specs=[slab] * 4),
            out_shape=[jax.ShapeDtypeStruct((n_layers, r_all, c_all), F32)] * 4,
            input_output_aliases={6 + k: k for k in range(len(carried))},
            compiler_params=_params(("parallel",)),
        )(own[1], own[0], w, m, v, terms, *carried)
    if col_block is None:
        row = pl.BlockSpec((tr, c_all), lambda i: (i, 0))
        return pl.pallas_call(
            body, name=name, grid=(r_all // tr,),
            in_specs=[row, row, row, pl.BlockSpec((n_slots, tr, c_all), lambda i: (0, i, 0))],
            out_specs=[row] * 4, out_shape=[shape] * 4, compiler_params=_params(("parallel",)),
        )(w, m, v, terms)
    row = pl.BlockSpec((tr, c_all), lambda i, blk: (i, 0))
    return pl.pallas_call(
        body, name=name,
        grid_spec=pltpu.PrefetchScalarGridSpec(
            num_scalar_prefetch=1, grid=(r_all // tr,),
            in_specs=[row, row, row, pl.BlockSpec((n_slots, tr, c_all), lambda i, blk: (0, i, blk[0]))],
            out_specs=[row] * 4),
        out_shape=[shape] * 4, compiler_params=_params(("parallel",)),
    )(col_block, w, m, v, terms)


def _rope_tables(t_all):
    pos = jnp.arange(t_all, dtype=F32)
    inv_freq = ROPE_THETA ** (-jnp.arange(0, QK_ROPE, 2, dtype=F32) / QK_ROPE)
    ang = pos[:, None] * inv_freq[None, :]
    cos, sin = jnp.cos(ang), jnp.sin(ang)
    return jnp.tile(cos, (1, LANES // (QK_ROPE // 2))), jnp.tile(sin, (1, LANES // (QK_ROPE // 2)))


def _adam_row_tile(r_all, c_all, block_bytes=512 * 1024):
    target = max(SUBLANES, block_bytes // (4 * c_all))
    return _pick(r_all, [t for t in (1024, 704, 512, 352, 256, 176, 128, 64, 32, 16, 8) if t <= target])


def _rows_natural(wg):
    return wg.reshape(wg.shape[0] * wg.shape[1], wg.shape[2])


def _mla_layer_fwd(tag, h, g_mix, ws, qn, kvn, cos, sin, *, tm, tq, n_heads, scale, n_real):
    w_in, w_uq, w_ukv, w_o = _rows_natural(ws[0]), ws[1], ws[2], _rows_natural(ws[3])
    t_all, d = h.shape
    lq, lkv = qn.shape[1], kvn.shape[1]
    tmb = _pick(t_all, _ROW_TILES)
    hn = _rmsnorm_fwd(h, g_mix, name=f"norm_mix{tag}", tm=_pick(t_all, _ROW_TILES))
    proj = _mm_nn(hn, w_in, name=f"mla_in{tag}", out_dtype=F32, tm=tmb, tn=w_in.shape[1], tk=_pick(d, _DIVS))
    cq, ckv, kr = _mla_prep_fwd(proj, qn, kvn, cos, sin, name=f"mla_prep{tag}", tm=tm, lq=lq, lkv=lkv)
    q = _mm_nn(cq, w_uq, name=f"mla_q{tag}", out_dtype=BF16, tm=tmb, tn=w_uq.shape[2], tk=lq, b_blocked=True,
               epilogue=_rope_q_epilogue, extras=(cos, sin))
    kv = _mm_nn(ckv, w_ukv, name=f"mla_kv{tag}", out_dtype=BF16, tm=tmb, tn=w_ukv.shape[2], tk=lkv, b_blocked=True)
    o, lse = _attn_fwd(q, kv, kr, name=f"attn_fwd{tag}", n_heads=n_heads, tq=tq, n_real=n_real, scale=scale)
    h_mid = _mm_nn(o, w_o, name=f"mla_o{tag}", out_dtype=F32, tm=tm, tn=d, tk=o.shape[1], res=h)
    return h_mid, (hn, proj, cq, ckv, kr, q, kv, o, lse)


def _mla_layer_bwd(tag, dh, dh_b, h_in, saved, g_mix, ws, qn, kvn, cos, sin, *, tm, tq, n_heads, scale, n_real, early=None,
                   after=None):
    hn, proj, cq, ckv, kr, q, kv, o, lse = saved
    w_in, w_uq, w_ukv, w_o = _rows_natural(ws[0]), ws[1], ws[2], _rows_natural(ws[3])
    t_all, d = h_in.shape
    lq, lkv = qn.shape[1], kvn.shape[1]
    ov = o.shape[1]
    tmb = _pick(t_all, _ROW_TILES)
    tn_d, tk_d = _pick(d, _DIVS[1:]), _pick(d, _DIVS)
    do = _mm_nt(dh_b, w_o, name=f"mla_do{tag}", out_dtype=BF16, tm=tmb, tn=_pick(ov, _DIVS[1:]), tk=tk_d, after=after)
    dw_o = _mm_tn(o, dh_b, name=f"mla_dwo{tag}", out_dtype=BF16, tm=_pick(ov, _DIVS[2:]), tn=tn_d, tk=t_all)
    dq, dkv, dkr_h = _attn_bwd(q, kv, kr, o, lse, do, cos, sin, name=f"attn_bwd{tag}", n_heads=n_heads, tq=tq, n_real=n_real,
                               scale=scale)
    hw, kw = w_uq.shape[2], w_ukv.shape[2]
    dw_uq = _mm_tn(cq, dq, name=f"mla_dwuq{tag}", out_dtype=BF16, tm=lq, tn=hw, tk=t_all, out_block=hw)
    dcq = _mm_nt(dq, w_uq, name=f"mla_dcq{tag}", out_dtype=F32, tm=tm, tn=lq, tk=dq.shape[1], b_blocked=True)
    dw_ukv = _mm_tn(ckv, dkv, name=f"mla_dwukv{tag}", out_dtype=BF16, tm=lkv, tn=kw, tk=t_all, out_block=kw)
    dckv = _mm_nt(dkv, w_ukv, name=f"mla_dckv{tag}", out_dtype=F32, tm=tm, tn=lkv, tk=dkv.shape[1], b_blocked=True)
    first = [dw_uq, dw_ukv, dw_o.reshape(N_DEV, -1, d)]
    if early is not None:
        qn = qn + early(first)
    dproj, dqn, dkvn = _mla_prep_bwd(dcq, dckv, dkr_h, proj, qn, kvn, cos, sin, name=f"mla_prep_bwd{tag}", tm=tm, lq=lq, lkv=lkv)
    wc = w_in.shape[1]
    dw_in = _mm_tn(hn, dproj, name=f"mla_dwin{tag}", out_dtype=BF16, tm=_pick(d, _DIVS[2:]), tn=wc, tk=t_all)
    dhn = _mm_nt(dproj, w_in, name=f"mla_dhn{tag}", out_dtype=BF16, tm=tmb, tn=tn_d, tk=wc)
    dh, dh_b, dg = _rmsnorm_bwd(dhn, h_in, g_mix, dh, name=f"norm_mix_bwd{tag}", tm=tm)
    return dh, dh_b, dg, dqn, dkvn, [dw_in.reshape(N_DEV, -1, wc)] + ([] if early is not None else first)


def _lru_layer_fwd(tag, h, g_mix, ws, small, *, tm):
    w_lin, w_lo = ws[0], _rows_natural(ws[1])
    t_all, d = h.shape
    dr = w_lo.shape[0]
    tmb = _pick(t_all, _ROW_TILES)
    hn = _rmsnorm_fwd(h, g_mix, name=f"norm_mix{tag}", tm=_pick(t_all, _ROW_TILES))
    xy = _mm_nn(hn, w_lin, name=f"lru_in{tag}", out_dtype=F32, tm=tmb, tn=w_lin.shape[2], tk=_pick(d, _DIVS), b_blocked=True,
                rows_outer=True)
    hs, hsy = _lru_fwd(xy, *small, name=f"lru_fwd{tag}")
    h_mid = _mm_nn(hsy, w_lo, name=f"lru_o{tag}", out_dtype=F32, tm=tm, tn=d, tk=dr, res=h)
    return h_mid, (hn, xy, hs, hsy)


def _lru_layer_bwd(tag, dh, dh_b, h_in, saved, g_mix, ws, small, *, tm, after=None):
    hn, xy, hs, hsy = saved
    w_lin, w_lo = ws[0], _rows_natural(ws[1])
    t_all, d = h_in.shape
    dr = w_lo.shape[0]
    tmb = _pick(t_all, _ROW_TILES)
    tn_d, tk_d = _pick(d, _DIVS[1:]), _pick(d, _DIVS)
    dhsy = _mm_nt(dh_b, w_lo, name=f"lru_dhsy{tag}", out_dtype=F32, tm=tmb, tn=_pick(dr, _DIVS[1:]), tk=tk_d, after=after)
    dw_lo = _mm_tn(hsy, dh_b, name=f"lru_dwo{tag}", out_dtype=BF16, tm=_pick(dr, _DIVS[2:]), tn=tn_d, tk=t_all)
    dxy, *dsmall = _lru_bwd(xy, hs, dhsy, *small, name=f"lru_bwd{tag}")
    lw = w_lin.shape[2]
    dw_lin = _mm_tn(hn, dxy, name=f"lru_dwin{tag}", out_dtype=BF16, tm=tn_d, tn=lw, tk=t_all, out_block=lw)
    dhn = _mm_nt(dxy, w_lin, name=f"lru_dhn{tag}", out_dtype=BF16, tm=tm, tn=tn_d, tk=2 * dr, b_blocked=True)
    dh, dh_b, dg = _rmsnorm_bwd(dhn, h_in, g_mix, dh, name=f"norm_mix_bwd{tag}", tm=tm)
    return dh, dh_b, dg, tuple(dsmall), [dw_lin, dw_lo.reshape(N_DEV, -1, d)]


def _ffn_layer_fwd(tag, h_mid, g_ffn, ws, *, tm):
    w_gu, w_down = ws[0], _rows_natural(ws[1])
    t_all, d = h_mid.shape
    f_all = w_down.shape[0]
    tmb = _pick(t_all, _ROW_TILES)
    fk = _pick(f_all, (1408,) + _DIVS[1:])
    hn2 = _rmsnorm_fwd(h_mid, g_ffn, name=f"norm_ffn{tag}", tm=_pick(t_all, _ROW_TILES))
    gu, act = _ffn_up(hn2, w_gu, name=f"ffn_up{tag}", tm=_pick(t_all, (704, 384, 256, 128)))
    h_out = _mm_nn(act, w_down, name=f"ffn_down{tag}", out_dtype=F32, tm=tm, tn=_pick(d, _DIVS[1:]), tk=f_all, res=h_mid)
    return h_out, (hn2, gu, act)


def _ffn_layer_bwd(tag, dh, dh_b, h_mid, saved, g_ffn, ws, *, tm, after=None):
    hn2, gu, act = saved
    w_gu, w_down = ws[0], _rows_natural(ws[1])
    t_all, d = h_mid.shape
    f_all = w_down.shape[0]
    f_local = w_gu.shape[2]
    tmb = _pick(t_all, _ROW_TILES)
    fk = _pick(f_all, (1408,) + _DIVS[1:])
    tn_d, tk_d = _pick(d, _DIVS[1:]), _pick(d, _DIVS)
    dgu = _ffn_dact(dh_b, w_down, gu, name=f"ffn_dact{tag}", tm=_pick(t_all, (704, 384, 256, 128)), tn=f_local, after=after)
    dw_down = _mm_tn(act, dh_b, name=f"ffn_dwdown{tag}", out_dtype=BF16, tm=fk, tn=_pick(d, _DIVS[2:]), tk=t_all)
    dhn2 = _mm_nt(dgu, w_gu, name=f"ffn_dhn{tag}", out_dtype=BF16, tm=tm, tn=_pick(d, _DIVS[2:]), tk=2 * f_all, b_blocked=True)
    dw_gu = _mm_tn(hn2, dgu, name=f"ffn_dwgu{tag}", out_dtype=BF16, tm=_pick(d, _DIVS[2:]), tn=f_local, tk=t_all, out_block=f_local,
                   cols_outer=True)
    dh, dh_b, dg = _rmsnorm_bwd(dhn2, h_mid, g_ffn, dh, name=f"norm_ffn_bwd{tag}", tm=tm)
    return dh, dh_b, dg, [dw_gu, dw_down.reshape(N_DEV, -1, d)]


def kernel(x, meta_tokens, norm_mix, norm_ffn, norm_final, mla_w_in, mla_q_norm, mla_kv_norm, mla_w_uq, mla_w_ukv, mla_w_o, lru_w_in, lru_conv_w, lru_conv_b, lru_w_gate_a, lru_b_gate_a, lru_w_gate_x, lru_b_gate_x, lru_lambda, lru_w_o, ffn_w_gu, ffn_w_down, loss_target, m_meta_tokens, m_norm_mix, m_norm_ffn, m_norm_final, m_mla_w_in, m_mla_q_norm, m_mla_kv_norm, m_mla_w_uq, m_mla_w_ukv, m_mla_w_o, m_lru_w_in, m_lru_conv_w, m_lru_conv_b, m_lru_w_gate_a, m_lru_b_gate_a, m_lru_w_gate_x, m_lru_b_gate_x, m_lru_lambda, m_lru_w_o, m_ffn_w_gu, m_ffn_w_down, v_meta_tokens, v_norm_mix, v_norm_ffn, v_norm_final, v_mla_w_in, v_mla_q_norm, v_mla_kv_norm, v_mla_w_uq, v_mla_w_ukv, v_mla_w_o, v_lru_w_in, v_lru_conv_w, v_lru_conv_b, v_lru_w_gate_a, v_lru_b_gate_a, v_lru_w_gate_x, v_lru_b_gate_x, v_lru_lambda, v_lru_w_o, v_ffn_w_gu, v_ffn_w_down):
    seq, d = x.shape[1], x.shape[2]
    assert seq % CHUNK == 0
    n_real = N_META + seq
    t_all = -(-n_real // LANES) * LANES
    tm = _pick(t_all, (384, 256, 128))
    tq = _pick(seq, (512, 256, 128, 64))
    depth = norm_mix.shape[0]
    n_mla, n_lru = mla_w_in.shape[0], lru_w_in.shape[0]
    lq, lkv = mla_q_norm.shape[1], mla_kv_norm.shape[1]
    w_in_cols = lq + lkv + LANES
    heads_local = mla_w_uq.shape[2] // (QK_NOPE + QK_ROPE)
    n_heads = heads_local * N_DEV
    dr = lru_w_gate_a.shape[1] * lru_w_gate_a.shape[2]
    scale = (QK_NOPE + QK_ROPE) ** -0.5
    cx, cy, cc = _mesh_pos()
    core = jnp.reshape(cc, (1,)).astype(jnp.int32)
    my_slot = jnp.reshape(4 * cx + 2 * cy + cc, (1,)).astype(jnp.int32)

    def pad_cols(w, cols):
        return jnp.pad(w, ((0, 0), (0, cols - w.shape[1])))

    def pad_heads(w):
        k_all = w.shape[0]
        w3 = w.reshape(k_all, heads_local, QK_NOPE + QK_ROPE)
        return jnp.pad(w3, ((0, 0), (0, 0), (0, HEAD_W - QK_NOPE - QK_ROPE))).reshape(k_all, heads_local * HEAD_W)

    def unpad_heads(w):
        k_all = w.shape[0]
        return w.reshape(k_all, heads_local, HEAD_W)[:, :, :QK_NOPE + QK_ROPE].reshape(k_all, -1)

    small_rows = N_META + n_lru * 4 + 2 * n_lru
    small_pad = -(-small_rows // SUBLANES) * SUBLANES

    def pack_small(meta, conv_w, conv_b, lam):
        rows = jnp.concatenate([meta, conv_w.reshape(n_lru * 4, -1), conv_b, lam], axis=0)
        return jnp.pad(rows, ((0, small_pad - small_rows), (0, 0)))

    def unpack_small(p):
        o1 = N_META + n_lru * 4
        return (p[:N_META], p[N_META:o1].reshape(n_lru, 4, -1), p[o1:o1 + n_lru], p[o1 + n_lru:o1 + 2 * n_lru])

    (small_full,), small_done = _all_gather([pack_small(meta_tokens, lru_conv_w, lru_conv_b, lru_lambda)], name="ag_small")
    small_full = jnp.transpose(small_full, (1, 0, 2)).reshape(small_pad, -1)
    meta_full, conv_w_full, conv_b_full, lam_full = unpack_small(small_full)

    def wire(w):
        return (w + small_done).astype(BF16)

    mla_shards, lru_shards, ffn_shards = [], [], []
    for j in range(n_mla):
        mla_shards.append([wire(pad_cols(mla_w_in[j], w_in_cols)), wire(pad_heads(mla_w_uq[j])), wire(mla_w_ukv[j]),
                           wire(mla_w_o[j])])
    for j in range(n_lru):
        lru_shards.append([wire(lru_w_in[j]), wire(lru_w_o[j])])
    for layer in range(depth):
        ffn_shards.append([wire(ffn_w_gu[layer]), wire(ffn_w_down[layer])])

    n_sub = 2 * depth
    groups = []
    for layer in range(depth):
        groups += [mla_shards[layer // 2] if layer % 2 == 0 else lru_shards[layer // 2], ffn_shards[layer]]
    slot_idx = 4 * cx + 2 * cy + cc
    ag_own = []
    for gi, shards in enumerate(groups):
        lands = [lax.dynamic_update_slice(lax.empty((N_DEV,) + s.shape, s.dtype), s[None], (slot_idx, 0, 0)) for s in shards]
        ag_own.append(_exchange_start(shards, lands, _ag_plan_own, 4, name=f"ag{gi}_start"))
    ag_pass = [None] * n_sub
    weights = [None] * n_sub

    def ag_landed(gi, after):
        _, lands = _exchange_wait(ag_own[gi], _ag_plan_own, after, name=f"ag{gi}_wait")
        ag_pass[gi] = _exchange_start([], lands, _ag_plan_pass, 3, name=f"ag{gi}_pass")
        return ag_pass[gi][4][0, 0]

    def ag_done(gi, after):
        _, weights[gi] = _exchange_wait(ag_pass[gi], _ag_plan_pass, after, name=f"ag{gi}_pass_wait")

    cos, sin = _rope_tables(t_all)
    zeros_tail = jnp.zeros((t_all - n_real, d), F32)
    started = ag_own[0][4][0, 0]
    for st in ag_own[1:]:
        started = started + st[4][0, 0]
    h = jnp.concatenate([meta_full + started, x[0], zeros_tail], axis=0)
    target = jnp.concatenate([jnp.zeros((N_META, d), F32), loss_target[0], zeros_tail], axis=0)

    attn_kw = dict(tm=tm, tq=tq, n_heads=n_heads, scale=scale, n_real=n_real)

    def lru_small(j):
        return (conv_w_full[j], conv_b_full[j][None, :], lru_w_gate_a[j].astype(BF16), lru_b_gate_a[j].reshape(1, dr),
                lru_w_gate_x[j].astype(BF16), lru_b_gate_x[j].reshape(1, dr), lam_full[j][None, :])

    def before_sublayer(k, act):
        tok = ag_landed(k, act) if k <= 1 else 0.0
        ag_done(k, act)
        if 1 <= k < n_sub - 1:
            tok = tok + ag_landed(k + 1, act)
        return tok

    saved = []
    for layer in range(depth):
        j = layer // 2
        g_mix = norm_mix[layer][None, :] + before_sublayer(2 * layer, h)
        if layer % 2 == 0:
            h_mid, mix_saved = _mla_layer_fwd(layer, h, g_mix, weights[2 * layer], mla_q_norm[j][None, :],
                                              mla_kv_norm[j][None, :], cos, sin, **attn_kw)
        else:
            h_mid, mix_saved = _lru_layer_fwd(layer, h, g_mix, weights[2 * layer], lru_small(j), tm=tm)
        g_ffn = norm_ffn[layer][None, :] + before_sublayer(2 * layer + 1, h_mid)
        h_out, ffn_saved = _ffn_layer_fwd(layer, h_mid, g_ffn, weights[2 * layer + 1], tm=tm)
        saved.append((h, h_mid, mix_saved, ffn_saved))
        h = h_out

    loss_part, dh, dh_b, dg_final = _loss_head(h, target, norm_final[None, :], name="loss_head", tm=tm, n_real=n_real)
    loss = lax.psum(loss_part[0, 0], ("x", "y", "c"))

    rs_sib, rs_chip, reduced = [None] * (n_sub + 1), [None] * (n_sub + 1), [None] * (n_sub + 1)
    chip_idx = jnp.reshape(2 * cx + cy, (1,)).astype(jnp.int32)

    def rs_begin(k, grads):
        lands = [lax.empty((4,) + g.shape[1:], g.dtype) for g in grads]
        rs_sib[k] = _exchange_start(grads, lands, _rs_plan_sibling, 4, name=f"rs{k}_start")
        return rs_sib[k][4]

    def rs_middle(k, after):
        grads, landed = _exchange_wait(rs_sib[k], _rs_plan_sibling, after, name=f"rs{k}_wait")
        parts = [_pair_add(g, l, core, name=f"rs{k}_add{a}", tr=_adam_row_tile(g.shape[1], g.shape[2], 4 * 1024 * 1024))
                 for a, (g, l) in enumerate(zip(grads, landed))]
        lands = [lax.empty((3,) + p.shape[1:], p.dtype) for p in parts]
        rs_chip[k] = _exchange_start(parts, lands, _rs_plan_chips, 3, name=f"rs{k}_chips")
        return rs_chip[k][4]

    def rs_end(k, after):
        reduced[k] = _exchange_wait(rs_chip[k], _rs_plan_chips, after, name=f"rs{k}_chips_wait")

    d_norm_mix, d_norm_ffn = [None] * depth, [None] * depth
    d_qn, d_kvn = [None] * n_mla, [None] * n_mla
    d_small = {k: [None] * n_lru for k in ("cw", "cb", "wga", "bga", "wgx", "bgx", "lam")}
    tok, waiting = None, None
    gate_own = [None] * n_lru
    for layer in reversed(range(depth)):
        j = layer // 2
        h_in, h_mid, mix_saved, ffn_saved = saved[layer]
        dh, dh_b, d_norm_ffn[layer], ffn_g = _ffn_layer_bwd(layer, dh, dh_b, h_mid, ffn_saved, norm_ffn[layer][None, :],
                                                            weights[2 * layer + 1], tm=tm, after=tok)
        tok = rs_begin(2 * layer + 1, ffn_g)
        if waiting is not None:
            tok = tok + rs_middle(waiting, dh)
        waiting = 2 * layer + 1
        if layer == 0:
            tok = tok + rs_middle(waiting, dh)
            waiting = None
        g_mix = norm_mix[layer][None, :]
        if layer % 2 == 0:
            early = (lambda g: (rs_begin(n_sub, g) + rs_middle(n_sub, g[0]))[0, 0]) if layer == 0 else None
            dh, dh_b, d_norm_mix[layer], d_qn[j], d_kvn[j], mix_g = _mla_layer_bwd(
                layer, dh, dh_b, h_in, mix_saved, g_mix, weights[2 * layer], mla_q_norm[j][None, :], mla_kv_norm[j][None, :],
                cos, sin, early=early, after=tok, **attn_kw)
            tok = rs_begin(2 * layer, mix_g)
        else:
            dh, dh_b, d_norm_mix[layer], dsmall, mix_g = _lru_layer_bwd(layer, dh, dh_b, h_in, mix_saved, g_mix,
                                                                        weights[2 * layer], lru_small(j), tm=tm, after=tok)
            for key, val in zip(("cw", "cb", "wga", "bga", "wgx", "bgx", "lam"), dsmall):
                d_small[key][j] = val
            gates = [d_small["wga"][j].reshape(-1, LANES), d_small["wgx"][j].reshape(-1, LANES)]
            gate_lands = [lax.dynamic_update_slice(lax.empty((N_DEV,) + g.shape, g.dtype), g[None], (slot_idx, 0, 0)) for g in gates]
            gate_own[j] = _exchange_start(gates, gate_lands, _ag_plan_own, 4, name=f"ag_gates{j}_start")
            tok = rs_begin(2 * layer, mix_g) + gate_own[j][4]
        if waiting is not None:
            tok = tok + rs_middle(waiting, dh)
        waiting = 2 * layer
    rs_middle(waiting, dh)

    grad_x = dh[N_META:n_real][None]

    d_meta = dh[:N_META]
    small_grad = pack_small(d_meta, jnp.stack(d_small["cw"], axis=0), jnp.concatenate(d_small["cb"], axis=0),
                            jnp.concatenate(d_small["lam"], axis=0))
    rep_grads = [
        jnp.concatenate(d_norm_mix, axis=0), jnp.concatenate(d_norm_ffn, axis=0), dg_final,
        jnp.concatenate(d_qn, axis=0), jnp.concatenate(d_kvn, axis=0),
        jnp.concatenate(d_small["bga"], axis=0), jnp.concatenate(d_small["bgx"], axis=0),
    ]
    small_srcs = [small_grad] + [jnp.pad(g, ((0, -g.shape[0] % SUBLANES), (0, 0))) for g in rep_grads]
    small_lands = [lax.dynamic_update_slice(lax.empty((N_DEV,) + s.shape, s.dtype), s[None], (slot_idx, 0, 0))
                   for s in small_srcs]
    small_own = _exchange_start(small_srcs, small_lands, _ag_plan_own, 4, name="ag_grads_start")

    res = {}

    def adam_sharded(nm, k, a, idx, n_layers, w, m, v):
        parts, landed = reduced[k]
        r_all, c_all = landed[a].shape[1], landed[a].shape[2]
        res[nm] = _adamw(w.reshape(r_all, c_all), m.reshape(r_all, c_all), v.reshape(r_all, c_all), landed[a], (0, 1, 2),
                         name=f"adamw_{nm}{idx}", tr=_adam_row_tile(r_all, c_all, 2 * 1024 * 1024), own=(parts[a], chip_idx),
                         stack=(idx, n_layers, res.get(nm)))

    after = small_own[4]
    for k in reversed(range(n_sub)):
        rs_end(k, after)
        if k == 0:
            rs_end(n_sub, after)
            reduced[0] = tuple(first + rest for first, rest in zip(reduced[0], reduced[n_sub]))
        layer, j = k // 2, k // 4
        if k % 2 == 1:
            adam_sharded("ffn_w_gu", k, 0, layer, depth, ffn_w_gu[layer], m_ffn_w_gu[layer], v_ffn_w_gu[layer])
            adam_sharded("ffn_w_down", k, 1, layer, depth, ffn_w_down[layer], m_ffn_w_down[layer], v_ffn_w_down[layer])
            after = res["ffn_w_down"][0]
        elif layer % 2 == 0:
            adam_sharded("mla_w_in", k, 0, j, n_mla, pad_cols(mla_w_in[j], w_in_cols), pad_cols(m_mla_w_in[j], w_in_cols),
                         pad_cols(v_mla_w_in[j], w_in_cols))
            adam_sharded("mla_w_uq", k, 1, j, n_mla, pad_heads(mla_w_uq[j]), pad_heads(m_mla_w_uq[j]), pad_heads(v_mla_w_uq[j]))
            adam_sharded("mla_w_ukv", k, 2, j, n_mla, mla_w_ukv[j], m_mla_w_ukv[j], v_mla_w_ukv[j])
            adam_sharded("mla_w_o", k, 3, j, n_mla, mla_w_o[j], m_mla_w_o[j], v_mla_w_o[j])
            after = res["mla_w_o"][0]
        else:
            adam_sharded("lru_w_in", k, 0, j, n_lru, lru_w_in[j], m_lru_w_in[j], v_lru_w_in[j])
            adam_sharded("lru_w_o", k, 1, j, n_lru, lru_w_o[j], m_lru_w_o[j], v_lru_w_o[j])
            after = res["lru_w_o"][0]
    res["mla_w_in"] = [t[:, :, :lq + lkv + QK_ROPE] for t in res["mla_w_in"]]
    res["mla_w_uq"] = [t.reshape(n_mla, lq, heads_local, HEAD_W)[:, :, :, :QK_NOPE + QK_ROPE].reshape(n_mla, lq, -1)
                       for t in res["mla_w_uq"]]

    _, small_lands = _exchange_wait(small_own, _ag_plan_own, after, name="ag_grads_wait")
    small_pass = _exchange_start([], small_lands, _ag_plan_pass, 3, name="ag_grads_pass")
    _, all_small = _exchange_wait(small_pass, _ag_plan_pass, after, name="ag_grads_pass_wait")
    gate_terms = []
    for j in range(n_lru):
        _, lands = _exchange_wait(gate_own[j], _ag_plan_own, after, name=f"ag_gates{j}_wait")
        gate_pass = _exchange_start([], lands, _ag_plan_pass, 3, name=f"ag_gates{j}_pass")
        gate_terms.append(_exchange_wait(gate_pass, _ag_plan_pass, after, name=f"ag_gates{j}_pass_wait")[1])
    wga_terms = jnp.concatenate([t[0] for t in gate_terms], axis=1)
    wgx_terms = jnp.concatenate([t[1] for t in gate_terms], axis=1)
    slot_order = tuple(range(N_DEV))

    def adam_rep(terms, w, m, v, tag):
        r_pad, c_all = terms.shape[1], terms.shape[2]

        def prep(t):
            t2 = t.reshape(-1, c_all)
            return jnp.pad(t2, ((0, r_pad - t2.shape[0]), (0, 0)))

        outs = _adamw(prep(w), prep(m), prep(v), terms, slot_order, name=f"adamw_{tag}", tr=_adam_row_tile(r_pad, c_all))
        n_rows = w.size // c_all
        return [o[:n_rows].reshape(w.shape) for o in outs]

    small_w = pack_small(meta_tokens, lru_conv_w, lru_conv_b, lru_lambda)
    small_m = pack_small(m_meta_tokens, m_lru_conv_w, m_lru_conv_b, m_lru_lambda)
    small_v = pack_small(v_meta_tokens, v_lru_conv_w, v_lru_conv_b, v_lru_lambda)
    small_out = _adamw(small_w, small_m, small_v, all_small[0], slot_order, name="adamw_small", tr=small_pad, col_block=my_slot)
    small_out = [unpack_small(o) for o in small_out]
    for idx, key in enumerate(("meta_tokens", "lru_conv_w", "lru_conv_b", "lru_lambda")):
        res[key] = [small_out[k][idx] for k in range(4)]

    res["norm_mix"] = adam_rep(all_small[1], norm_mix, m_norm_mix, v_norm_mix, "norm_mix")
    res["norm_ffn"] = adam_rep(all_small[2], norm_ffn, m_norm_ffn, v_norm_ffn, "norm_ffn")
    res["norm_final"] = adam_rep(all_small[3], norm_final, m_norm_final, v_norm_final, "norm_final")
    res["mla_q_norm"] = adam_rep(all_small[4], mla_q_norm, m_mla_q_norm, v_mla_q_norm, "mla_q_norm")
    res["mla_kv_norm"] = adam_rep(all_small[5], mla_kv_norm, m_mla_kv_norm, v_mla_kv_norm, "mla_kv_norm")
    res["lru_w_gate_a"] = adam_rep(wga_terms, lru_w_gate_a, m_lru_w_gate_a, v_lru_w_gate_a, "lru_w_gate_a")
    res["lru_b_gate_a"] = adam_rep(all_small[6], lru_b_gate_a, m_lru_b_gate_a, v_lru_b_gate_a, "lru_b_gate_a")
    res["lru_w_gate_x"] = adam_rep(wgx_terms, lru_w_gate_x, m_lru_w_gate_x, v_lru_w_gate_x, "lru_w_gate_x")
    res["lru_b_gate_x"] = adam_rep(all_small[7], lru_b_gate_x, m_lru_b_gate_x, v_lru_b_gate_x, "lru_b_gate_x")

    names = ["meta_tokens", "norm_mix", "norm_ffn", "norm_final", "mla_w_in", "mla_q_norm", "mla_kv_norm", "mla_w_uq",
             "mla_w_ukv", "mla_w_o", "lru_w_in", "lru_conv_w", "lru_conv_b", "lru_w_gate_a", "lru_b_gate_a", "lru_w_gate_x",
             "lru_b_gate_x", "lru_lambda", "lru_w_o", "ffn_w_gu", "ffn_w_down"]
    shapes = dict(meta_tokens=meta_tokens, norm_mix=norm_mix, norm_ffn=norm_ffn, norm_final=norm_final, mla_w_in=mla_w_in,
                  mla_q_norm=mla_q_norm, mla_kv_norm=mla_kv_norm, mla_w_uq=mla_w_uq, mla_w_ukv=mla_w_ukv, mla_w_o=mla_w_o,
                  lru_w_in=lru_w_in, lru_conv_w=lru_conv_w, lru_conv_b=lru_conv_b, lru_w_gate_a=lru_w_gate_a,
                  lru_b_gate_a=lru_b_gate_a, lru_w_gate_x=lru_w_gate_x, lru_b_gate_x=lru_b_gate_x, lru_lambda=lru_lambda,
                  lru_w_o=lru_w_o, ffn_w_gu=ffn_w_gu, ffn_w_down=ffn_w_down)
    outs = [loss, grad_x]
    for k in range(4):
        outs += [res[nm][k].reshape(shapes[nm].shape) for nm in names]
    return tuple(outs)
```

```python
import math

import jax
import jax.numpy as jnp
from jax import lax
from jax.experimental import pallas as pl
from jax.experimental.pallas import tpu as pltpu

F32 = jnp.float32
BF16 = jnp.bfloat16
MESH = pl.DeviceIdType.MESH

N_META = 16
CHUNK = 64
QK_NOPE = 128
QK_ROPE = 64
V_HEAD = 128
HEAD_W = 256
ROPE_THETA = 10000.0
LRU_C = 8.0
RMS_EPS = 1e-6
NEG_BIG = -1e30
ADAM_LR, ADAM_B1, ADAM_B2, ADAM_EPS, ADAM_WD, ADAM_STEP = 0.001, 0.9, 0.999, 1e-08, 0.01, 10

LANES = 128
SUBLANES = 8
VMEM_LIMIT_BYTES = 52 * 1024 * 1024
N_DEV = 8

_NT = (((1,), (1,)), ((), ()))
_TN = (((0,), (0,)), ((), ()))
_DIVS = (2048, 1024, 512, 256, 128)
_ROW_TILES = (1408, 1024, 512, 256, 128)


def _params(dims):
    return pltpu.CompilerParams(dimension_semantics=dims, vmem_limit_bytes=VMEM_LIMIT_BYTES)


def _pick(n, candidates):
    for c in candidates:
        if c <= n and n % c == 0:
            return c
    return n


def _sigmoid(z):
    return 0.5 + 0.5 * jnp.tanh(0.5 * z)


def _gelu(x):
    c = math.sqrt(2.0 / math.pi)
    return 0.5 * x * (1.0 + jnp.tanh(c * (x + 0.044715 * x * x * x)))


def _gelu_grad(x):
    c = math.sqrt(2.0 / math.pi)
    th = jnp.tanh(c * (x + 0.044715 * x * x * x))
    return 0.5 * (1.0 + th) + 0.5 * x * (1.0 - th * th) * c * (1.0 + 3.0 * 0.044715 * x * x)


def _neg_expm1(x):
    poly = -x * (1.0 + x * (1.0 / 2.0) * (1.0 + x * (1.0 / 3.0) * (1.0 + x * (1.0 / 4.0) * (
        1.0 + x * (1.0 / 5.0) * (1.0 + x * (1.0 / 6.0) * (1.0 + x * (1.0 / 7.0)))))))
    return jnp.where(x > -0.25, poly, 1.0 - jnp.exp(x))


def _softplus_neg(lam):
    e = jnp.exp(-jnp.abs(lam))
    log1p = jnp.where(e > 1e-4, jnp.log(1.0 + e), e * (1.0 - e * (0.5 - e * (1.0 / 3.0))))
    return jnp.maximum(-lam, 0.0) + log1p


def _rot_half(x):
    lane = lax.broadcasted_iota(jnp.int32, x.shape, 1)
    first = (lane % QK_ROPE) < (QK_ROPE // 2)
    return jnp.where(first, -pltpu.roll(x, LANES - QK_ROPE // 2, 1), pltpu.roll(x, QK_ROPE // 2, 1))


def _rope(x, cos, sin):
    return x * cos + _rot_half(x) * sin


def _unrope(g, cos, sin):
    return g * cos - _rot_half(g) * sin


def _grid_order(rows_outer):
    if not rows_outer:
        return lambda f: f
    return lambda f: (lambda i, j, k: f(j, i, k))


def _mm_nn(a, b, *, name, out_dtype, tm, tn, tk, b_blocked=False, res=None, epilogue=None, extras=(), rows_outer=False):
    m_all, k_all = a.shape
    om = _grid_order(rows_outer)
    if b_blocked:
        g_all, kb, nb = b.shape
        n_all = g_all * nb
        assert nb % tn == 0
        r = nb // tn
        b_spec = pl.BlockSpec((None, tk, tn), om(lambda j, i, k: (j // r, k, j % r)))
    else:
        kb, n_all = b.shape
        b_spec = pl.BlockSpec((tk, tn), om(lambda j, i, k: (k, j)))
    assert kb == k_all and m_all % tm == 0 and n_all % tn == 0 and k_all % tk == 0
    nm, nn, nk = m_all // tm, n_all // tn, k_all // tk
    in_specs = [pl.BlockSpec((tm, tk), om(lambda j, i, k: (i, k))), b_spec]
    operands = [a, b]
    has_res = res is not None
    if has_res:
        in_specs.append(pl.BlockSpec((tm, tn), om(lambda j, i, k: (i, j))))
        operands.append(res)
    for e in extras:
        in_specs.append(pl.BlockSpec((tm, e.shape[1]), om(lambda j, i, k: (i, 0))))
        operands.append(e)
    n_ex = len(extras)

    def body(*refs):
        a_ref, b_ref = refs[0], refs[1]
        pos = 2
        res_ref = None
        if has_res:
            res_ref = refs[pos]
            pos += 1
        ex_refs = refs[pos:pos + n_ex]
        pos += n_ex
        o_ref = refs[pos]
        acc_ref = refs[pos + 1] if nk > 1 else None

        def finish(acc):
            if has_res:
                acc = acc + res_ref[...]
            if epilogue is not None:
                acc = epilogue(acc, *ex_refs)
            o_ref[...] = acc.astype(o_ref.dtype)

        prod = jnp.dot(a_ref[...], b_ref[...], preferred_element_type=F32)
        if nk == 1:
            finish(prod)
        else:
            k = pl.program_id(2)

            @pl.when(k == 0)
            def _():
                acc_ref[...] = prod

            @pl.when(k > 0)
            def _():
                acc_ref[...] += prod

            @pl.when(k == nk - 1)
            def _():
                finish(acc_ref[...])

    return pl.pallas_call(
        body, name=name, grid=(nm, nn, nk) if rows_outer else (nn, nm, nk), in_specs=in_specs,
        out_specs=pl.BlockSpec((tm, tn), om(lambda j, i, k: (i, j))),
        out_shape=jax.ShapeDtypeStruct((m_all, n_all), out_dtype),
        scratch_shapes=[pltpu.VMEM((tm, tn), F32)] if nk > 1 else [],
        compiler_params=_params(("parallel", "parallel", "arbitrary")),
    )(*operands)


def _mm_nt(a, b, *, name, out_dtype, tm, tn, tk, b_blocked=False, after=None):
    if a.ndim == 3:
        n_planes, m_all, kp = a.shape
        k_all = n_planes * kp
    else:
        n_planes, (m_all, k_all) = 0, a.shape
    if b_blocked and tk == k_all and b.shape[0] > 1:
        g_all, n_all, nb = b.shape
        assert g_all * nb == k_all and m_all % tm == 0 and n_all % tn == 0
        per_plane = kp // nb if n_planes else 0

        def whole_body(a_ref, b_ref, o_ref):
            acc = None
            for g in range(g_all):
                a_g = a_ref[g // per_plane, :, (g % per_plane) * nb:(g % per_plane + 1) * nb] if n_planes else a_ref[:, g * nb:(g + 1) * nb]
                prod = lax.dot_general(a_g, b_ref[g], _NT, preferred_element_type=F32)
                acc = prod if acc is None else acc + prod
            o_ref[...] = acc.astype(o_ref.dtype)

        a_whole = (pl.BlockSpec((n_planes, tm, kp), lambda j, i: (0, i, 0)) if n_planes
                   else pl.BlockSpec((tm, k_all), lambda j, i: (i, 0)))
        return pl.pallas_call(
            whole_body, name=name, grid=(n_all // tn, m_all // tm),
            in_specs=[a_whole, pl.BlockSpec((g_all, tn, nb), lambda j, i: (0, j, 0))],
            out_specs=pl.BlockSpec((tm, tn), lambda j, i: (i, j)),
            out_shape=jax.ShapeDtypeStruct((m_all, n_all), out_dtype),
            compiler_params=_params(("parallel", "parallel")),
        )(a, b)
    if n_planes:
        assert kp % tk == 0
        rp = kp // tk
        a_spec = pl.BlockSpec((None, tm, tk), lambda j, i, k: (k // rp, i, k % rp))
    else:
        a_spec = pl.BlockSpec((tm, tk), lambda j, i, k: (i, k))
    if b_blocked:
        g_all, n_all, nb = b.shape
        assert g_all * nb == k_all and nb % tk == 0
        r = nb // tk
        b_spec = pl.BlockSpec((None, tn, tk), lambda j, i, k: (k // r, j, k % r))
    else:
        n_all, kb = b.shape
        assert kb == k_all
        b_spec = pl.BlockSpec((tn, tk), lambda j, i, k: (j, k))
    assert m_all % tm == 0 and n_all % tn == 0 and k_all % tk == 0
    nm, nn, nk = m_all // tm, n_all // tn, k_all // tk

    def body(a_ref, b_ref, o_ref, *scratch):
        prod = lax.dot_general(a_ref[...], b_ref[...], _NT, preferred_element_type=F32)
        if nk == 1:
            o_ref[...] = prod.astype(o_ref.dtype)
        else:
            acc_ref = scratch[0]
            k = pl.program_id(2)

            @pl.when(k == 0)
            def _():
                acc_ref[...] = prod

            @pl.when(k > 0)
            def _():
                acc_ref[...] += prod

            @pl.when(k == nk - 1)
            def _():
                o_ref[...] = acc_ref[...].astype(o_ref.dtype)

    ordered = [] if after is None else [after]

    def ordered_body(a_ref, b_ref, *rest):
        body(a_ref, b_ref, *rest[len(ordered):])

    return pl.pallas_call(
        ordered_body, name=name, grid=(nn, nm, nk),
        in_specs=[a_spec, b_spec] + [pl.BlockSpec(memory_space=pl.ANY)] * len(ordered),
        out_specs=pl.BlockSpec((tm, tn), lambda j, i, k: (i, j)),
        out_shape=jax.ShapeDtypeStruct((m_all, n_all), out_dtype),
        scratch_shapes=[pltpu.VMEM((tm, tn), F32)] if nk > 1 else [],
        compiler_params=_params(("parallel", "parallel", "arbitrary")),
    )(a, b, *ordered)


def _mm_tn(a, b, *, name, out_dtype, tm, tn, tk, out_block=None, cols_outer=False):
    t_all, m_all = a.shape
    om = _grid_order(cols_outer)
    if b.ndim == 3:
        n_planes, tb, n_p = b.shape
        assert n_p % tn == 0
        rq = n_p // tn
        n_all = n_planes * n_p
        b_spec = pl.BlockSpec((None, tk, tn), om(lambda i, j, k: (j // rq, k, j % rq)))
    else:
        tb, n_all = b.shape
        b_spec = pl.BlockSpec((tk, tn), om(lambda i, j, k: (k, j)))
    assert tb == t_all and m_all % tm == 0 and n_all % tn == 0 and t_all % tk == 0
    nm, nn, nk = m_all // tm, n_all // tn, t_all // tk
    if out_block is None:
        out_shape = jax.ShapeDtypeStruct((m_all, n_all), out_dtype)
        out_spec = pl.BlockSpec((tm, tn), om(lambda i, j, k: (i, j)))
    else:
        assert out_block % tn == 0 and n_all % out_block == 0
        r = out_block // tn
        out_shape = jax.ShapeDtypeStruct((n_all // out_block, m_all, out_block), out_dtype)
        out_spec = pl.BlockSpec((None, tm, tn), om(lambda i, j, k: (j // r, i, j % r)))

    def body(a_ref, b_ref, o_ref, *scratch):
        prod = lax.dot_general(a_ref[...], b_ref[...], _TN, preferred_element_type=F32)
        if nk == 1:
            o_ref[...] = prod.astype(o_ref.dtype)
        else:
            acc_ref = scratch[0]
            k = pl.program_id(2)

            @pl.when(k == 0)
            def _():
                acc_ref[...] = prod

            @pl.when(k > 0)
            def _():
                acc_ref[...] += prod

            @pl.when(k == nk - 1)
            def _():
                o_ref[...] = acc_ref[...].astype(o_ref.dtype)

    return pl.pallas_call(
        body, name=name, grid=(nn, nm, nk) if cols_outer else (nm, nn, nk),
        in_specs=[pl.BlockSpec((tk, tm), om(lambda i, j, k: (k, i))), b_spec],
        out_specs=out_spec, out_shape=out_shape,
        scratch_shapes=[pltpu.VMEM((tm, tn), F32)] if nk > 1 else [],
        compiler_params=_params(("parallel", "parallel", "arbitrary")),
    )(a, b)


def _rmsnorm_fwd(x, g, *, name, tm):
    t_all, d = x.shape

    def body(x_ref, g_ref, o_ref):
        xv = x_ref[...]
        rstd = lax.rsqrt(jnp.mean(xv * xv, axis=-1, keepdims=True) + RMS_EPS)
        o_ref[...] = (xv * rstd * g_ref[...]).astype(o_ref.dtype)

    return pl.pallas_call(
        body, name=name, grid=(t_all // tm,),
        in_specs=[pl.BlockSpec((tm, d), lambda i: (i, 0)), pl.BlockSpec((1, d), lambda i: (0, 0))],
        out_specs=pl.BlockSpec((tm, d), lambda i: (i, 0)),
        out_shape=jax.ShapeDtypeStruct((t_all, d), BF16),
        compiler_params=_params(("parallel",)),
    )(x, g)


def _rms_bwd_math(dy, xv, g):
    rstd = lax.rsqrt(jnp.mean(xv * xv, axis=-1, keepdims=True) + RMS_EPS)
    xhat = xv * rstd
    dxh = dy * g
    dx = rstd * (dxh - xhat * jnp.mean(dxh * xhat, axis=-1, keepdims=True))
    return dx, jnp.sum(dy * xhat, axis=0, keepdims=True)


def _rmsnorm_bwd(dy, x, g, res, *, name, tm):
    t_all, d = x.shape

    def body(dy_ref, x_ref, g_ref, res_ref, dx_ref, dxb_ref, dg_ref):
        dx, dg = _rms_bwd_math(dy_ref[...].astype(F32), x_ref[...], g_ref[...])
        tot = res_ref[...] + dx
        dx_ref[...] = tot
        dxb_ref[...] = tot.astype(BF16)

        @pl.when(pl.program_id(0) == 0)
        def _():
            dg_ref[...] = dg

        @pl.when(pl.program_id(0) > 0)
        def _():
            dg_ref[...] += dg

    row = pl.BlockSpec((tm, d), lambda i: (i, 0))
    vec = pl.BlockSpec((1, d), lambda i: (0, 0))
    return pl.pallas_call(
        body, name=name, grid=(t_all // tm,),
        in_specs=[row, row, vec, row], out_specs=[row, row, vec],
        out_shape=[jax.ShapeDtypeStruct((t_all, d), F32), jax.ShapeDtypeStruct((t_all, d), BF16),
                   jax.ShapeDtypeStruct((1, d), F32)],
        compiler_params=_params(("arbitrary",)),
    )(dy, x, g, res)


def _loss_head(h, target, g, *, name, tm, n_real):
    t_all, d = h.shape

    def body(h_ref, t_ref, g_ref, loss_ref, dx_ref, dxb_ref, dg_ref):
        i = pl.program_id(0)
        xv = h_ref[...]
        gv = g_ref[...]
        rstd = lax.rsqrt(jnp.mean(xv * xv, axis=-1, keepdims=True) + RMS_EPS)
        y = xv * rstd * gv
        row = i * tm + lax.broadcasted_iota(jnp.int32, (tm, 1), 0)
        valid = (row >= N_META) & (row < n_real)
        err = jnp.where(valid, y - t_ref[...], 0.0)
        part = 0.5 * jnp.sum(jnp.mean(err * err, axis=-1, keepdims=True), axis=0, keepdims=True)
        dx, dg = _rms_bwd_math(err * (1.0 / d), xv, gv)
        dx_ref[...] = dx
        dxb_ref[...] = dx.astype(BF16)

        @pl.when(i == 0)
        def _():
            dg_ref[...] = dg
            loss_ref[...] = jnp.broadcast_to(part, loss_ref.shape)

        @pl.when(i > 0)
        def _():
            dg_ref[...] += dg
            loss_ref[...] += jnp.broadcast_to(part, loss_ref.shape)

    row = pl.BlockSpec((tm, d), lambda i: (i, 0))
    vec = pl.BlockSpec((1, d), lambda i: (0, 0))
    return pl.pallas_call(
        body, name=name, grid=(t_all // tm,),
        in_specs=[row, row, vec],
        out_specs=[pl.BlockSpec((1, LANES), lambda i: (0, 0)), row, row, vec],
        out_shape=[jax.ShapeDtypeStruct((1, LANES), F32), jax.ShapeDtypeStruct((t_all, d), F32),
                   jax.ShapeDtypeStruct((t_all, d), BF16), jax.ShapeDtypeStruct((1, d), F32)],
        compiler_params=_params(("arbitrary",)),
    )(h, target, g)


def _ffn_up(x, w_gu, *, name, tm):
    t_all, d = x.shape
    g_all, kb, nb = w_gu.shape
    half = g_all // 2
    f = half * nb
    assert kb == d and t_all % tm == 0

    def body(x_ref, wg_ref, wu_ref, gu_ref, act_ref):
        xv = x_ref[...]
        gv = jnp.dot(xv, wg_ref[...], preferred_element_type=F32)
        uv = jnp.dot(xv, wu_ref[...], preferred_element_type=F32)
        gu_ref[0] = gv.astype(gu_ref.dtype)
        gu_ref[1] = uv.astype(gu_ref.dtype)
        act_ref[...] = (gv * _sigmoid(gv) * uv).astype(act_ref.dtype)

    return pl.pallas_call(
        body, name=name, grid=(half, t_all // tm),
        in_specs=[pl.BlockSpec((tm, d), lambda j, i: (i, 0)), pl.BlockSpec((None, d, nb), lambda j, i: (j, 0, 0)),
                  pl.BlockSpec((None, d, nb), lambda j, i: (j + half, 0, 0))],
        out_specs=[pl.BlockSpec((2, tm, nb), lambda j, i: (0, i, j)), pl.BlockSpec((tm, nb), lambda j, i: (i, j))],
        out_shape=[jax.ShapeDtypeStruct((2, t_all, f), BF16), jax.ShapeDtypeStruct((t_all, f), BF16)],
        compiler_params=_params(("parallel", "parallel")),
    )(x, w_gu, w_gu)


def _ffn_dact(dy, w_down, gu, *, name, tm, tn, after=None):
    t_all, d = dy.shape
    f = w_down.shape[0]
    assert t_all % tm == 0 and f % tn == 0
    ordered = [] if after is None else [after]

    def body(dy_ref, w_ref, gu_ref, *rest):
        o_ref = rest[-1]
        dact = lax.dot_general(dy_ref[...], w_ref[...], _NT, preferred_element_type=F32)
        gv, uv = gu_ref[0].astype(F32), gu_ref[1].astype(F32)
        sg = _sigmoid(gv)
        o_ref[0] = (dact * uv * (sg * (1.0 + gv * (1.0 - sg)))).astype(o_ref.dtype)
        o_ref[1] = (dact * gv * sg).astype(o_ref.dtype)

    return pl.pallas_call(
        body, name=name, grid=(f // tn, t_all // tm),
        in_specs=[pl.BlockSpec((tm, d), lambda j, i: (i, 0)), pl.BlockSpec((tn, d), lambda j, i: (j, 0)),
                  pl.BlockSpec((2, tm, tn), lambda j, i: (0, i, j))] + [pl.BlockSpec(memory_space=pl.ANY)] * len(ordered),
        out_specs=pl.BlockSpec((2, tm, tn), lambda j, i: (0, i, j)),
        out_shape=jax.ShapeDtypeStruct((2, t_all, f), BF16),
        compiler_params=_params(("parallel", "parallel")),
    )(dy, w_down, gu, *ordered)


def _mla_prep_fwd(proj, qn, kvn, cos, sin, *, name, tm, lq, lkv):
    t_all, w = proj.shape

    def body(p_ref, qn_ref, kvn_ref, cos_ref, sin_ref, cq_ref, ckv_ref, kr_ref):
        pv = p_ref[...]
        xq = pv[:, :lq]
        xkv = pv[:, lq:lq + lkv]
        cq_ref[...] = (xq * lax.rsqrt(jnp.mean(xq * xq, axis=-1, keepdims=True) + RMS_EPS) * qn_ref[...]).astype(BF16)
        ckv_ref[...] = (xkv * lax.rsqrt(jnp.mean(xkv * xkv, axis=-1, keepdims=True) + RMS_EPS) * kvn_ref[...]).astype(BF16)
        kr_ref[...] = _rope(pv[:, lq + lkv:], cos_ref[...], sin_ref[...]).astype(BF16)

    def row(width):
        return pl.BlockSpec((tm, width), lambda i: (i, 0))

    def vec(width):
        return pl.BlockSpec((1, width), lambda i: (0, 0))

    return pl.pallas_call(
        body, name=name, grid=(t_all // tm,),
        in_specs=[row(w), vec(lq), vec(lkv), row(LANES), row(LANES)],
        out_specs=[row(lq), row(lkv), row(LANES)],
        out_shape=[jax.ShapeDtypeStruct((t_all, lq), BF16), jax.ShapeDtypeStruct((t_all, lkv), BF16),
                   jax.ShapeDtypeStruct((t_all, LANES), BF16)],
        compiler_params=_params(("parallel",)),
    )(proj, qn, kvn, cos, sin)


def _mla_prep_bwd(dcq, dckv, dkr_h, proj, qn, kvn, cos, sin, *, name, tm, lq, lkv):
    t_all, w = proj.shape
    n_heads = dkr_h.shape[0]

    def body(dcq_ref, dckv_ref, dkr_ref, p_ref, qn_ref, kvn_ref, cos_ref, sin_ref, dp_ref, dqn_ref, dkvn_ref):
        pv = p_ref[...]
        dxq, dqn = _rms_bwd_math(dcq_ref[...], pv[:, :lq], qn_ref[...])
        dxkv, dkvn = _rms_bwd_math(dckv_ref[...], pv[:, lq:lq + lkv], kvn_ref[...])
        dkr = dkr_ref[0]
        for hh in range(1, n_heads):
            dkr = dkr + dkr_ref[hh]
        dkr = _unrope(dkr, cos_ref[...], sin_ref[...])
        dp_ref[...] = jnp.concatenate([dxq, dxkv, dkr], axis=1).astype(BF16)

        @pl.when(pl.program_id(0) == 0)
        def _():
            dqn_ref[...] = dqn
            dkvn_ref[...] = dkvn

        @pl.when(pl.program_id(0) > 0)
        def _():
            dqn_ref[...] += dqn
            dkvn_ref[...] += dkvn

    def row(width):
        return pl.BlockSpec((tm, width), lambda i: (i, 0))

    def vec(width):
        return pl.BlockSpec((1, width), lambda i: (0, 0))

    return pl.pallas_call(
        body, name=name, grid=(t_all // tm,),
        in_specs=[row(lq), row(lkv), pl.BlockSpec((n_heads, tm, LANES), lambda i: (0, i, 0)), row(w),
                  vec(lq), vec(lkv), row(LANES), row(LANES)],
        out_specs=[row(w), vec(lq), vec(lkv)],
        out_shape=[jax.ShapeDtypeStruct((t_all, w), BF16), jax.ShapeDtypeStruct((1, lq), F32),
                   jax.ShapeDtypeStruct((1, lkv), F32)],
        compiler_params=_params(("arbitrary",)),
    )(dcq, dckv, dkr_h, proj, qn, kvn, cos, sin)


def _rope_q_epilogue(acc, cos_ref, sin_ref):
    parts = []
    for g in range(acc.shape[1] // LANES):
        blk = acc[:, g * LANES:(g + 1) * LANES]
        parts.append(_rope(blk, cos_ref[...], sin_ref[...]) if g % 2 == 1 else blk)
    return jnp.concatenate(parts, axis=1)


def _chunk_causal(rows, cols, row0=0):
    r = row0 + lax.broadcasted_iota(jnp.int32, (rows, cols), 0)
    c = lax.broadcasted_iota(jnp.int32, (rows, cols), 1)
    return (c >> 6) <= (r >> 6)


def _meta_keys(rows, cols):
    return lax.broadcasted_iota(jnp.int32, (rows, cols), 1) < N_META


def _attn_fwd(q, kv, kr, *, name, n_heads, tq, n_real, scale):
    t_all = q.shape[0]
    nq = (n_real - N_META) // tq
    assert N_META + nq * tq == n_real and tq % CHUNK == 0 and t_all >= LANES
    n_pad = t_all - n_real
    sub = tq // 2 if (tq // 2) % CHUNK == 0 else tq

    def body(q_ref, kv_ref, kr_ref, o_ref, lse_ref, k_scr, m_scr, l_scr, acc_scr):
        k_scr[:, :QK_NOPE] = kv_ref[:, :QK_NOPE]
        k_scr[:, QK_NOPE:] = kr_ref[...]
        if n_pad:
            o_ref[pl.ds(n_real, n_pad), :] = jnp.zeros((n_pad, V_HEAD), o_ref.dtype)
            lse_ref[pl.ds(n_real, n_pad), :] = jnp.zeros((n_pad, LANES), F32)

        def scores(qt, c0, width):
            return lax.dot_general(qt, k_scr[pl.ds(c0, width), :], _NT, preferred_element_type=F32) * scale

        def values(c0, width):
            return kv_ref[pl.ds(c0, width), QK_NOPE:]

        s = jnp.where(_meta_keys(LANES, LANES), scores(q_ref[pl.ds(0, LANES), :], 0, LANES), NEG_BIG)
        m = jnp.max(s, axis=-1, keepdims=True)
        p = jnp.exp(s - m)
        l = jnp.sum(p, axis=-1, keepdims=True)
        o_meta = jnp.dot(p.astype(BF16), values(0, LANES), preferred_element_type=F32) / l
        o_ref[pl.ds(0, N_META), :] = o_meta[:N_META].astype(o_ref.dtype)
        lse_ref[pl.ds(0, N_META), :] = jnp.broadcast_to((m + jnp.log(l))[:N_META], (N_META, LANES))

        parts = [(u * sub, sub) for u in range(tq // sub)]

        def accumulate(u0, s, vals):
            rows = pl.ds(u0, s.shape[0])
            m_prev = m_scr[rows, :]
            m_new = jnp.maximum(m_prev, jnp.max(s, axis=-1, keepdims=True))
            alpha = jnp.exp(m_prev - m_new)
            p = jnp.exp(s - m_new)
            l_scr[rows, :] = alpha * l_scr[rows, :] + jnp.sum(p, axis=-1, keepdims=True)
            acc_scr[rows, :] = alpha * acc_scr[rows, :] + jnp.dot(p.astype(BF16), vals, preferred_element_type=F32)
            m_scr[rows, :] = m_new

        def q_tile(i, carry):
            r0 = pl.multiple_of(N_META + i * tq, N_META)
            qts = [q_ref[pl.ds(r0 + u0, rows), :] for u0, rows in parts]
            m_scr[...] = jnp.full(m_scr.shape, NEG_BIG, F32)
            l_scr[...] = jnp.zeros(l_scr.shape, F32)
            acc_scr[...] = jnp.zeros(acc_scr.shape, F32)

            def full_blocks(j, width):
                c0 = pl.multiple_of(N_META + j * tq, N_META)
                for (u0, _), qt in zip(parts, qts):
                    accumulate(u0, scores(qt, c0, width), values(c0, width))

            def two_blocks(jj, c):
                full_blocks(2 * jj, 2 * tq)
                return c

            lax.fori_loop(0, i // 2, two_blocks, 0)

            @pl.when(i % 2 == 1)
            def _():
                full_blocks(i - 1, tq)

            for (u0, rows), qt in zip(parts, qts):
                width = u0 + rows
                s = jnp.concatenate([jnp.where(_meta_keys(rows, LANES), scores(qt, 0, LANES), NEG_BIG),
                                     jnp.where(_chunk_causal(rows, width, u0), scores(qt, r0, width), NEG_BIG)], axis=1)
                accumulate(u0, s, jnp.concatenate([values(0, LANES), values(r0, width)], axis=0))
            o_ref[pl.ds(r0, tq), :] = (acc_scr[...] / l_scr[...]).astype(o_ref.dtype)
            lse_ref[pl.ds(r0, tq), :] = jnp.broadcast_to(m_scr[...] + jnp.log(l_scr[...]), (tq, LANES))
            return carry

        lax.fori_loop(0, nq, q_tile, 0)

    def head(width):
        return pl.BlockSpec((t_all, width), lambda h: (0, h))

    return pl.pallas_call(
        body, name=name, grid=(n_heads,),
        in_specs=[head(HEAD_W), head(HEAD_W), pl.BlockSpec((t_all, LANES), lambda h: (0, 0))],
        out_specs=[head(V_HEAD), pl.BlockSpec((None, t_all, LANES), lambda h: (h, 0, 0))],
        out_shape=[jax.ShapeDtypeStruct((t_all, n_heads * V_HEAD), BF16),
                   jax.ShapeDtypeStruct((n_heads, t_all, LANES), F32)],
        scratch_shapes=[pltpu.VMEM((t_all, HEAD_W), BF16), pltpu.VMEM((tq, 1), F32), pltpu.VMEM((tq, 1), F32),
                        pltpu.VMEM((tq, V_HEAD), F32)],
        compiler_params=_params(("parallel",)),
    )(q, kv, kr)


def _attn_bwd(q, kv, kr, o, lse, do, cos, sin, *, name, n_heads, tq, n_real, scale):
    t_all = q.shape[0]
    nq = (n_real - N_META) // tq
    assert N_META + nq * tq == n_real and tq % CHUNK == 0 and t_all >= LANES
    n_pad = t_all - n_real

    def body(q_ref, kv_ref, kr_ref, o_ref, lse_ref, do_ref, cos_ref, sin_ref, dq_ref, dkv_ref, dkr_ref,
             k_scr, dk_scr, dv_scr, dq_scr):
        k_scr[:, :QK_NOPE] = kv_ref[:, :QK_NOPE]
        k_scr[:, QK_NOPE:] = kr_ref[...]
        dk_scr[...] = jnp.zeros(dk_scr.shape, F32)
        dv_scr[...] = jnp.zeros(dv_scr.shape, F32)
        if n_pad:
            dq_ref[pl.ds(n_real, n_pad), :] = jnp.zeros((n_pad, HEAD_W), dq_ref.dtype)

        def blocks(qt, dot, lse_t, delta, segments):
            kb = jnp.concatenate([k_scr[pl.ds(c0, w), :] for c0, w, _ in segments], axis=0)
            vb = jnp.concatenate([kv_ref[pl.ds(c0, w), QK_NOPE:] for c0, w, _ in segments], axis=0)
            s = lax.dot_general(qt, kb, _NT, preferred_element_type=F32) * scale
            p = jnp.exp(s - lse_t)
            if any(m is not None for _, _, m in segments):
                rows = qt.shape[0]
                mask = jnp.concatenate([jnp.ones((rows, w), jnp.bool_) if m is None else m for _, w, m in segments], axis=1)
                p = jnp.where(mask, p, 0.0)
            dp = lax.dot_general(dot, vb, _NT, preferred_element_type=F32)
            ds = (p * (dp - delta) * scale).astype(BF16)
            dv = lax.dot_general(p.astype(BF16), dot, _TN, preferred_element_type=F32)
            dk = lax.dot_general(ds, qt, _TN, preferred_element_type=F32)
            at = 0
            for c0, w, _ in segments:
                dv_scr[pl.ds(c0, w), :] += dv[at:at + w]
                dk_scr[pl.ds(c0, w), :] += dk[at:at + w]
                at += w
            return jnp.dot(ds, kb, preferred_element_type=F32)

        def block(qt, dot, lse_t, delta, c0, width, mask):
            return blocks(qt, dot, lse_t, delta, [(c0, width, mask)])

        def write_dq(r0, rows, dq):
            cs, sn = cos_ref[pl.ds(r0, rows), :], sin_ref[pl.ds(r0, rows), :]
            dq_ref[pl.ds(r0, rows), :] = jnp.concatenate(
                [dq[:, :QK_NOPE], _unrope(dq[:, QK_NOPE:], cs, sn)], axis=1).astype(dq_ref.dtype)

        rows_m = lax.broadcasted_iota(jnp.int32, (LANES, LANES), 0) < N_META
        dot = do_ref[pl.ds(0, LANES), :]
        delta = jnp.sum(dot.astype(F32) * o_ref[pl.ds(0, LANES), :].astype(F32), axis=-1, keepdims=True)
        dq = block(q_ref[pl.ds(0, LANES), :], dot, lse_ref[pl.ds(0, LANES), :1], delta, 0, LANES,
                   _meta_keys(LANES, LANES) & rows_m)
        write_dq(0, N_META, dq[:N_META])

        def q_tile(i, carry):
            r0 = pl.multiple_of(N_META + i * tq, N_META)
            qt = q_ref[pl.ds(r0, tq), :]
            dot = do_ref[pl.ds(r0, tq), :]
            lse_t = lse_ref[pl.ds(r0, tq), :1]
            delta = jnp.sum(dot.astype(F32) * o_ref[pl.ds(r0, tq), :].astype(F32), axis=-1, keepdims=True)
            dq_scr[...] = blocks(qt, dot, lse_t, delta, [(0, LANES, _meta_keys(tq, LANES)), (r0, tq, _chunk_causal(tq, tq))])

            def two_blocks(jj, c):
                c0 = pl.multiple_of(N_META + 2 * jj * tq, N_META)
                dq_scr[...] += block(qt, dot, lse_t, delta, c0, 2 * tq, None)
                return c

            lax.fori_loop(0, i // 2, two_blocks, 0)

            @pl.when(i % 2 == 1)
            def _():
                c0 = pl.multiple_of(N_META + (i - 1) * tq, N_META)
                dq_scr[...] += block(qt, dot, lse_t, delta, c0, tq, None)

            write_dq(r0, tq, dq_scr[...])
            return carry

        lax.fori_loop(0, nq, q_tile, 0)
        dk = dk_scr[...]
        dkv_ref[...] = jnp.concatenate([dk[:, :QK_NOPE], dv_scr[...]], axis=1).astype(dkv_ref.dtype)
        dkr_ref[...] = dk[:, QK_NOPE:]

    def head(width):
        return pl.BlockSpec((t_all, width), lambda h: (0, h))

    table = pl.BlockSpec((t_all, LANES), lambda h: (0, 0))
    per_head = pl.BlockSpec((None, t_all, LANES), lambda h: (h, 0, 0))
    return pl.pallas_call(
        body, name=name, grid=(n_heads,),
        in_specs=[head(HEAD_W), head(HEAD_W), table, head(V_HEAD), per_head, head(V_HEAD), table, table],
        out_specs=[head(HEAD_W), head(HEAD_W), per_head],
        out_shape=[jax.ShapeDtypeStruct((t_all, n_heads * HEAD_W), BF16), jax.ShapeDtypeStruct((t_all, n_heads * HEAD_W), BF16),
                   jax.ShapeDtypeStruct((n_heads, t_all, LANES), F32)],
        scratch_shapes=[pltpu.VMEM((t_all, HEAD_W), BF16), pltpu.VMEM((t_all, HEAD_W), F32), pltpu.VMEM((t_all, V_HEAD), F32),
                        pltpu.VMEM((tq, HEAD_W), F32)],
        compiler_params=_params(("parallel",)),
    )(q, kv, kr, o, lse, do, cos, sin)


LRU_ROWS = (384, 256, 128)


def _shifted_back(ref, t0, rows, shift_max):
    main = ref[pl.ds(t0, rows), :]
    prev = ref[pl.ds(pl.multiple_of(jnp.maximum(t0 - SUBLANES, 0), SUBLANES), SUBLANES), :]
    prev = jnp.where(t0 > 0, prev, 0.0)
    ext = jnp.concatenate([prev, main], axis=0)
    return [main] + [pltpu.roll(ext, s, 0)[SUBLANES:, :] for s in range(1, shift_max + 1)]


def _shifted_ahead(ref, t0, rows, t_all, shift_max):
    main = ref[pl.ds(t0, rows), :]
    nxt = ref[pl.ds(pl.multiple_of(jnp.minimum(t0 + rows, t_all - SUBLANES), SUBLANES), SUBLANES), :]
    nxt = jnp.where(t0 + rows < t_all, nxt, 0.0)
    ext = jnp.concatenate([main, nxt], axis=0)
    return [main] + [pltpu.roll(ext, rows + SUBLANES - s, 0)[:rows, :] for s in range(1, shift_max + 1)]


def _conv_fwd(xp_ref, t0, rows, cw, cb):
    sh = _shifted_back(xp_ref, t0, rows, 3)
    out = cb + cw[3:4, :] * sh[0]
    for k in range(3):
        out = out + cw[k:k + 1, :] * sh[3 - k]
    return out, sh


def _lru_gates(xb, wga, bga, wgx, bgx, sp):
    xbb = xb.astype(BF16)
    r = _sigmoid(jnp.dot(xbb, wga, preferred_element_type=F32) + bga)
    ig = _sigmoid(jnp.dot(xbb, wgx, preferred_element_type=F32) + bgx)
    la = -LRU_C * r * sp
    a = jnp.exp(la)
    s = jnp.sqrt(_neg_expm1(2.0 * la))
    return xbb, r, ig, a, s


SCAN_GROUP = 32


def _scan_tile(a, b, h_in, reverse):
    rows = a.shape[0]
    pos = lax.broadcasted_iota(jnp.int32, a.shape, 0) % SCAN_GROUP
    s = 1
    while s < SCAN_GROUP:
        keep = (pos < SCAN_GROUP - s) if reverse else (pos >= s)
        shift = rows - s if reverse else s
        a_sh, b_sh = pltpu.roll(a, shift, 0), pltpu.roll(b, shift, 0)
        b = jnp.where(keep, a * b_sh + b, b)
        a = jnp.where(keep, a * a_sh, a)
        s *= 2
    n_groups = rows // SCAN_GROUP
    out = [None] * n_groups
    h = h_in
    for g in (reversed(range(n_groups)) if reverse else range(n_groups)):
        rows_g = slice(g * SCAN_GROUP, (g + 1) * SCAN_GROUP)
        out[g] = a[rows_g] * h + b[rows_g]
        h = out[g][:1] if reverse else out[g][SCAN_GROUP - 1:]
    return jnp.concatenate(out, axis=0)


def _lru_fwd(xy, conv_w, conv_b, wga, bga, wgx, bgx, lam, *, name):
    t_all = xy.shape[0]
    dr = xy.shape[1] // 2
    c = LANES
    nblk = dr // c
    rows = _pick(t_all, LRU_ROWS)
    nt = t_all // rows

    def body(xp_ref, yp_ref, cw_ref, cb_ref, wga_ref, bga_ref, wgx_ref, bgx_ref, lam_ref, hs_ref, hsy_ref):
        cw, cb = cw_ref[...], cb_ref[...]
        sp = _softplus_neg(lam_ref[...])

        def tile(t, h_in):
            t0 = pl.multiple_of(t * rows, rows)
            xb, _ = _conv_fwd(xp_ref, t0, rows, cw, cb)
            _, _, ig, a, s = _lru_gates(xb, wga_ref[0], bga_ref[...], wgx_ref[0], bgx_ref[...], sp)
            hs = _scan_tile(a, s * (ig * xb), h_in, reverse=False)
            hs_ref[pl.ds(t0, rows), :] = hs
            hsy_ref[pl.ds(t0, rows), :] = (hs * _gelu(yp_ref[pl.ds(t0, rows), :])).astype(BF16)
            return hs[rows - 1:, :]

        lax.fori_loop(0, nt, tile, jnp.zeros((1, c), F32))

    col = pl.BlockSpec((t_all, c), lambda b: (0, b))
    vec = pl.BlockSpec((1, c), lambda b: (0, b))
    wsp = pl.BlockSpec((1, c, c), lambda b: (b, 0, 0))
    return pl.pallas_call(
        body, name=name, grid=(nblk,),
        in_specs=[col, pl.BlockSpec((t_all, c), lambda b: (0, nblk + b)), pl.BlockSpec((4, c), lambda b: (0, b)), vec,
                  wsp, vec, wsp, vec, vec],
        out_specs=[col, col],
        out_shape=[jax.ShapeDtypeStruct((t_all, dr), F32), jax.ShapeDtypeStruct((t_all, dr), BF16)],
        compiler_params=_params(("parallel",)),
    )(xy, xy, conv_w, conv_b, wga, bga, wgx, bgx, lam)


def _lru_bwd(xy, hs, dhsy, conv_w, conv_b, wga, bga, wgx, bgx, lam, *, name):
    t_all = xy.shape[0]
    dr = xy.shape[1] // 2
    c = LANES
    nblk = dr // c
    rows = _pick(t_all, LRU_ROWS)
    nt = t_all // rows

    def body(xp_ref, yp_ref, hs_ref, dh_ref, cw_ref, cb_ref, wga_ref, bga_ref, wgx_ref, bgx_ref, lam_ref,
             dxp_ref, dyp_ref, dcw_ref, dcb_ref, dwga_ref, dbga_ref, dwgx_ref, dbgx_ref, dlam_ref,
             xb_scr, r_scr, i_scr, a_scr):
        cw, cb = cw_ref[...], cb_ref[...]
        lamv = lam_ref[...]
        sp = _softplus_neg(lamv)
        sig_neg = 1.0 / (1.0 + jnp.exp(lamv))
        wga_v, wgx_v = wga_ref[0], wgx_ref[0]

        def recompute(t, carry):
            t0 = pl.multiple_of(t * rows, rows)
            xb, _ = _conv_fwd(xp_ref, t0, rows, cw, cb)
            _, r, ig, a, _ = _lru_gates(xb, wga_v, bga_ref[...], wgx_v, bgx_ref[...], sp)
            xb_scr[pl.ds(t0, rows), :] = xb
            r_scr[pl.ds(t0, rows), :] = r
            i_scr[pl.ds(t0, rows), :] = ig
            a_scr[pl.ds(t0, rows), :] = a
            return carry

        lax.fori_loop(0, nt, recompute, 0)
        dwga_ref[...] = jnp.zeros(dwga_ref.shape, F32)
        dwgx_ref[...] = jnp.zeros(dwgx_ref.shape, F32)

        def tile(ti, carry):
            lam_in, dbga, dbgx, dlam, dcw, dcb = carry
            t = nt - 1 - ti
            t0 = pl.multiple_of(t * rows, rows)
            a_now, a_next = _shifted_ahead(a_scr, t0, rows, t_all, 1)
            yp = yp_ref[pl.ds(t0, rows), :]
            dhy = dh_ref[pl.ds(t0, rows), :]
            lam_t = _scan_tile(a_next, dhy * _gelu(yp), lam_in, reverse=True)
            hs_now, hs_prev = _shifted_back(hs_ref, t0, rows, 1)
            da = lam_t * hs_prev
            xb = xb_scr[pl.ds(t0, rows), :]
            r = r_scr[pl.ds(t0, rows), :]
            ig = i_scr[pl.ds(t0, rows), :]
            la = -LRU_C * r * sp
            s = jnp.sqrt(_neg_expm1(2.0 * la))
            d_ixb = lam_t * s
            dla = da * a_now - (lam_t * ig * xb) * (a_now * a_now / s)
            dzr = dla * (-LRU_C * sp) * r * (1.0 - r)
            dzi = d_ixb * xb * ig * (1.0 - ig)
            dzr_b, dzi_b = dzr.astype(BF16), dzi.astype(BF16)
            xbb = xb.astype(BF16)
            dwga_ref[0] += lax.dot_general(xbb, dzr_b, _TN, preferred_element_type=F32)
            dwgx_ref[0] += lax.dot_general(xbb, dzi_b, _TN, preferred_element_type=F32)
            dxb = (d_ixb * ig + lax.dot_general(dzr_b, wga_v, _NT, preferred_element_type=F32)
                   + lax.dot_general(dzi_b, wgx_v, _NT, preferred_element_type=F32))
            xb_scr[pl.ds(t0, rows), :] = dxb
            dyp_ref[pl.ds(t0, rows), :] = (dhy * hs_now * _gelu_grad(yp)).astype(BF16)
            ahead = _shifted_ahead(xb_scr, t0, rows, t_all, 3)
            dxp = cw[3:4, :] * ahead[0]
            for k in range(3):
                dxp = dxp + cw[k:k + 1, :] * ahead[3 - k]
            dxp_ref[pl.ds(t0, rows), :] = dxp.astype(BF16)
            back = _shifted_back(xp_ref, t0, rows, 3)
            dcw_t = jnp.concatenate([jnp.sum(dxb * back[3 - k], axis=0, keepdims=True) for k in range(4)], axis=0)
            return (lam_t[:1, :], dbga + jnp.sum(dzr, axis=0, keepdims=True), dbgx + jnp.sum(dzi, axis=0, keepdims=True),
                    dlam + jnp.sum(dla * r, axis=0, keepdims=True), dcw + dcw_t, dcb + jnp.sum(dxb, axis=0, keepdims=True))

        zero = jnp.zeros((1, c), F32)
        _, dbga, dbgx, dlam, dcw, dcb = lax.fori_loop(0, nt, tile, (zero, zero, zero, zero, jnp.zeros((4, c), F32), zero))
        dbga_ref[...] = dbga
        dbgx_ref[...] = dbgx
        dlam_ref[...] = dlam * (LRU_C * sig_neg)
        dcw_ref[...] = dcw
        dcb_ref[...] = dcb

    col = pl.BlockSpec((t_all, c), lambda b: (0, b))
    col2 = pl.BlockSpec((t_all, c), lambda b: (0, nblk + b))
    vec = pl.BlockSpec((1, c), lambda b: (0, b))
    tap = pl.BlockSpec((4, c), lambda b: (0, b))
    wsp = pl.BlockSpec((1, c, c), lambda b: (b, 0, 0))
    vshape = jax.ShapeDtypeStruct((1, dr), F32)
    wshape = jax.ShapeDtypeStruct((nblk, c, c), F32)
    def planes_body(*refs):
        dxy_ref = refs[11]
        body(*refs[:11], dxy_ref.at[0], dxy_ref.at[1], *refs[12:])

    return pl.pallas_call(
        planes_body, name=name, grid=(nblk,),
        in_specs=[col, col2, col, col, tap, vec, wsp, vec, wsp, vec, vec],
        out_specs=[pl.BlockSpec((2, t_all, c), lambda b: (0, 0, b)), tap, vec, wsp, vec, wsp, vec, vec],
        out_shape=[jax.ShapeDtypeStruct((2, t_all, dr), BF16),
                   jax.ShapeDtypeStruct((4, dr), F32), vshape, wshape, vshape, wshape, vshape, vshape],
        scratch_shapes=[pltpu.VMEM((t_all, c), F32)] * 4,
        compiler_params=_params(("parallel",)),
    )(xy, xy, hs, dhsy, conv_w, conv_b, wga, bga, wgx, bgx, lam)


def _mesh_pos():
    return lax.axis_index("x"), lax.axis_index("y"), lax.axis_index("c")


def _all_gather(shards, *, name):
    n = len(shards)

    def body(*refs):
        ins, outs, token = refs[:n], refs[n:2 * n], refs[2 * n]
        send_sems, recv_sems, local_sems = refs[2 * n + 1:]
        token[...] = jnp.zeros(token.shape, token.dtype)
        x, y, c = _mesh_pos()
        me, sibling = (x, y, c), (x, y, 1 - c)
        chips = [(1 - x, y), (x, 1 - y), (1 - x, 1 - y)]
        slot = _slot

        def copy(a, k, block, to, src=None):
            dst = outs[a].at[slot(block)]
            return pltpu.make_async_remote_copy(
                src_ref=dst if src is None else src, dst_ref=dst, send_sem=send_sems.at[a, k],
                recv_sem=recv_sems.at[a, k], device_id=to, device_id_type=MESH)

        mine = [pltpu.make_async_copy(ins[a], outs[a].at[slot(me)], local_sems.at[a]) for a in range(n)]
        for cp in mine:
            cp.start()
        first = []
        for a in range(n):
            first.append(copy(a, 0, me, sibling, src=ins[a]))
            first += [copy(a, 1 + j, me, (*chip, c), src=ins[a]) for j, chip in enumerate(chips)]
        for cp in first:
            cp.start()
        passed = []
        for a in range(n):
            for j, chip in enumerate(chips):
                copy(a, 1 + j, (*chip, c), me).wait_recv()
                fwd = copy(a, 4 + j, (*chip, c), sibling)
                fwd.start()
                passed.append(fwd)
        for a in range(n):
            copy(a, 0, sibling, me).wait_recv()
            for j, chip in enumerate(chips):
                copy(a, 4 + j, (*chip, 1 - c), me).wait_recv()
        for cp in first + passed:
            cp.wait_send()
        for cp in mine:
            cp.wait()

    any_spec = pl.BlockSpec(memory_space=pl.ANY)
    outs = pl.pallas_call(
        body, name=name,
        in_specs=[any_spec] * n, out_specs=[any_spec] * n + [pl.BlockSpec(memory_space=pltpu.VMEM)],
        out_shape=[jax.ShapeDtypeStruct((N_DEV,) + s.shape, s.dtype) for s in shards]
        + [jax.ShapeDtypeStruct((SUBLANES, LANES), F32)],
        scratch_shapes=[pltpu.SemaphoreType.DMA((n, 7)), pltpu.SemaphoreType.DMA((n, 7)), pltpu.SemaphoreType.DMA((n,))],
    )(*shards)
    return list(outs[:n]), outs[n][0, 0]


_HBM = pl.BlockSpec(memory_space=pltpu.HBM)
_SEM = pl.BlockSpec(memory_space=pltpu.SEMAPHORE)
_ANY = pl.BlockSpec(memory_space=pl.ANY)
_EFFECT = pltpu.SideEffectType.DATAFLOW_SIDE_EFFECTING


def _slot(p):
    return 4 * p[0] + 2 * p[1] + p[2]


def _remote(src, dst, send, recv, idx, to):
    return pltpu.make_async_remote_copy(src_ref=src, dst_ref=dst, send_sem=send.at[idx], recv_sem=recv.at[idx],
                                        device_id=to, device_id_type=MESH)


def _ag_plan_own(a, src, land, send, recv):
    x, y, c = _mesh_pos()
    dst = land.at[_slot((x, y, c))]
    targets = [(x, y, 1 - c), (1 - x, y, c), (x, 1 - y, c), (1 - x, 1 - y, c)]
    return [_remote(src, dst, send, recv, 4 * a + k, to) for k, to in enumerate(targets)]


def _ag_plan_pass(a, src, land, send, recv):
    x, y, c = _mesh_pos()
    blocks = [land.at[_slot((px, py, c))] for px, py in ((1 - x, y), (x, 1 - y), (1 - x, 1 - y))]
    return [_remote(blk, blk, send, recv, 3 * a + k, (x, y, 1 - c)) for k, blk in enumerate(blocks)]


def _rs_plan_sibling(a, src, land, send, recv):
    x, y, c = _mesh_pos()
    return [_remote(src.at[2 * j + (1 - c)], land.at[j], send, recv, 4 * a + j, (x, y, 1 - c)) for j in range(4)]


def _rs_plan_chips(a, src, land, send, recv):
    x, y, c = _mesh_pos()
    out = []
    for k in (1, 2, 3):
        px = 1 - x if k & 2 else x
        py = 1 - y if k & 1 else y
        out.append(_remote(src.at[2 * px + py], land.at[k - 1], send, recv, 3 * a + k - 1, (px, py, c)))
    return out


def _in_hbm(a):
    return pltpu.with_memory_space_constraint(a, pltpu.HBM)


def _exchange_start(srcs, lands, plan, n_k, *, name):
    ns, n = len(srcs), len(lands)

    def body(*refs):
        src_refs, land_refs = refs[:ns], refs[ns:ns + n]
        send, recv = refs[ns + n], refs[ns + n + 1]
        token = refs[-1]
        for a in range(n):
            for cp in plan(a, src_refs[a] if ns else None, land_refs[a], send, recv):
                cp.start()
        token[...] = jnp.zeros(token.shape, token.dtype)

    bufs = list(srcs) + list(lands)
    outs = pl.pallas_call(
        body, name=name,
        out_shape=(pltpu.SemaphoreType.DMA((n * n_k,)), pltpu.SemaphoreType.DMA((n * n_k,)),
                   *[pltpu.HBM(b.shape, b.dtype) for b in bufs], jax.ShapeDtypeStruct((SUBLANES, LANES), F32)),
        in_specs=[_HBM] * (ns + n),
        out_specs=(_SEM, _SEM, *[_HBM] * (ns + n), pl.BlockSpec(memory_space=pltpu.VMEM)),
        input_output_aliases={i: 2 + i for i in range(ns + n)},
        compiler_params=pltpu.CompilerParams(has_side_effects=_EFFECT),
    )(*[_in_hbm(b) for b in bufs])
    return outs[0], outs[1], list(outs[2:2 + ns]), list(outs[2 + ns:2 + ns + n]), outs[-1]


def _exchange_wait(started, plan, after, *, name):
    send, recv, srcs, lands, _ = started
    ns, n = len(srcs), len(lands)

    def body(*refs):
        src_refs, land_refs = refs[:ns], refs[ns:ns + n]
        send_ref, recv_ref = refs[ns + n], refs[ns + n + 1]
        for a in range(n):
            for cp in plan(a, src_refs[a] if ns else None, land_refs[a], send_ref, recv_ref):
                cp.wait_send()
                cp.wait_recv()

    bufs = list(srcs) + list(lands)
    outs = pl.pallas_call(
        body, name=name,
        out_shape=tuple(pltpu.HBM(b.shape, b.dtype) for b in bufs),
        in_specs=[_HBM] * (ns + n) + [_SEM, _SEM, _ANY],
        out_specs=tuple([_HBM] * (ns + n)),
        input_output_aliases={i: i for i in range(ns + n)},
        compiler_params=pltpu.CompilerParams(has_side_effects=_EFFECT),
    )(*bufs, send, recv, after)
    return list(outs[:ns]), list(outs[ns:])


def _pair_add(grads, landed, core, *, name, tr):
    _, r_all, c_all = grads.shape

    def body(core_ref, g_ref, l_ref, o_ref):
        o_ref[...] = (g_ref[...].astype(F32) + l_ref[...].astype(F32)).astype(o_ref.dtype)

    return pl.pallas_call(
        body, name=name,
        grid_spec=pltpu.PrefetchScalarGridSpec(
            num_scalar_prefetch=1, grid=(4, r_all // tr),
            in_specs=[pl.BlockSpec((None, tr, c_all), lambda j, i, core_ref: (2 * j + core_ref[0], i, 0)),
                      pl.BlockSpec((None, tr, c_all), lambda j, i, core_ref: (j, i, 0))],
            out_specs=pl.BlockSpec((None, tr, c_all), lambda j, i, core_ref: (j, i, 0))),
        out_shape=jax.ShapeDtypeStruct((4, r_all, c_all), grads.dtype),
        compiler_params=_params(("parallel", "parallel")),
    )(core, grads, landed)


def _adamw_math(w, g, m, v):
    m2 = ADAM_B1 * m + (1.0 - ADAM_B1) * g
    v2 = ADAM_B2 * v + (1.0 - ADAM_B2) * (g * g)
    m_hat = m2 / (1.0 - ADAM_B1 ** ADAM_STEP)
    v_hat = v2 / (1.0 - ADAM_B2 ** ADAM_STEP)
    delta = -ADAM_LR * (m_hat / (jnp.sqrt(v_hat) + ADAM_EPS) + ADAM_WD * w)
    return delta, m2, v2


def _adamw(w, m, v, terms, order, *, name, tr, col_block=None, own=None, stack=None):
    r_all, c_all = w.shape
    n_slots = terms.shape[0]

    def body(*refs):
        if col_block is not None or own is not None:
            refs = refs[1:]
        own_ref = None
        if own is not None:
            own_ref, refs = refs[0], refs[1:]
        w_ref, m_ref, v_ref, t_ref, g_ref, d_ref, m2_ref, v2_ref = refs
        if own_ref is not None:
            g = own_ref[...].astype(F32) + t_ref[order[0]].astype(F32)
        else:
            g = t_ref[order[0]].astype(F32)
        for s in order[1:]:
            g = g + t_ref[s].astype(F32)
        delta, m2, v2 = _adamw_math(w_ref[...], g, m_ref[...], v_ref[...])
        g_ref[...] = g
        d_ref[...] = delta
        m2_ref[...] = m2
        v2_ref[...] = v2

    shape = jax.ShapeDtypeStruct((r_all, c_all), F32)
    if own is not None:
        layer, n_layers, prev = stack
        row = pl.BlockSpec((tr, c_all), lambda i, idx: (i, 0))
        slab = pl.BlockSpec((None, tr, c_all), lambda i, idx: (layer, i, 0))
        carried = [] if prev is None else list(prev)

        def stacked_body(*refs):
            body(*refs[:6], *refs[6 + len(carried):])

        return pl.pallas_call(
            stacked_body, name=name,
            grid_spec=pltpu.PrefetchScalarGridSpec(
                num_scalar_prefetch=1, grid=(r_all // tr,),
                in_specs=[pl.BlockSpec((None, tr, c_all), lambda i, idx: (idx[0], i, 0)), row, row, row,
                          pl.BlockSpec((n_slots, tr, c_all), lambda i, idx: (0, i, 0))] + [_ANY] * len(carried),
                out_specs=[slab] * 4),
            out_shape=[jax.ShapeDtypeStruct((n_layers, r_all, c_all), F32)] * 4,
            input_output_aliases={6 + k: k for k in range(len(carried))},
            compiler_params=_params(("parallel",)),
        )(own[1], own[0], w, m, v, terms, *carried)
    if col_block is None:
        row = pl.BlockSpec((tr, c_all), lambda i: (i, 0))
        return pl.pallas_call(
            body, name=name, grid=(r_all // tr,),
            in_specs=[row, row, row, pl.BlockSpec((n_slots, tr, c_all), lambda i: (0, i, 0))],
            out_specs=[row] * 4, out_shape=[shape] * 4, compiler_params=_params(("parallel",)),
        )(w, m, v, terms)
    row = pl.BlockSpec((tr, c_all), lambda i, blk: (i, 0))
    return pl.pallas_call(
        body, name=name,
        grid_spec=pltpu.PrefetchScalarGridSpec(
            num_scalar_prefetch=1, grid=(r_all // tr,),
            in_specs=[row, row, row, pl.BlockSpec((n_slots, tr, c_all), lambda i, blk: (0, i, blk[0]))],
            out_specs=[row] * 4),
        out_shape=[shape] * 4, compiler_params=_params(("parallel",)),
    )(col_block, w, m, v, terms)


def _rope_tables(t_all):
    pos = jnp.arange(t_all, dtype=F32)
    inv_freq = ROPE_THETA ** (-jnp.arange(0, QK_ROPE, 2, dtype=F32) / QK_ROPE)
    ang = pos[:, None] * inv_freq[None, :]
    cos, sin = jnp.cos(ang), jnp.sin(ang)
    return jnp.tile(cos, (1, LANES // (QK_ROPE // 2))), jnp.tile(sin, (1, LANES // (QK_ROPE // 2)))


def _adam_row_tile(r_all, c_all, block_bytes=512 * 1024):
    target = max(SUBLANES, block_bytes // (4 * c_all))
    return _pick(r_all, [t for t in (1024, 704, 512, 352, 256, 176, 128, 64, 32, 16, 8) if t <= target])


def _rows_natural(wg):
    return wg.reshape(wg.shape[0] * wg.shape[1], wg.shape[2])


def _mla_layer_fwd(tag, h, g_mix, ws, qn, kvn, cos, sin, *, tm, tq, n_heads, scale, n_real):
    w_in, w_uq, w_ukv, w_o = _rows_natural(ws[0]), ws[1], ws[2], _rows_natural(ws[3])
    t_all, d = h.shape
    lq, lkv = qn.shape[1], kvn.shape[1]
    tmb = _pick(t_all, _ROW_TILES)
    hn = _rmsnorm_fwd(h, g_mix, name=f"norm_mix{tag}", tm=_pick(t_all, _ROW_TILES))
    proj = _mm_nn(hn, w_in, name=f"mla_in{tag}", out_dtype=F32, tm=tmb, tn=w_in.shape[1], tk=_pick(d, _DIVS))
    cq, ckv, kr = _mla_prep_fwd(proj, qn, kvn, cos, sin, name=f"mla_prep{tag}", tm=tm, lq=lq, lkv=lkv)
    q = _mm_nn(cq, w_uq, name=f"mla_q{tag}", out_dtype=BF16, tm=tmb, tn=w_uq.shape[2], tk=lq, b_blocked=True,
               epilogue=_rope_q_epilogue, extras=(cos, sin))
    kv = _mm_nn(ckv, w_ukv, name=f"mla_kv{tag}", out_dtype=BF16, tm=tmb, tn=w_ukv.shape[2], tk=lkv, b_blocked=True)
    o, lse = _attn_fwd(q, kv, kr, name=f"attn_fwd{tag}", n_heads=n_heads, tq=tq, n_real=n_real, scale=scale)
    h_mid = _mm_nn(o, w_o, name=f"mla_o{tag}", out_dtype=F32, tm=tm, tn=d, tk=o.shape[1], res=h)
    return h_mid, (hn, proj, cq, ckv, kr, q, kv, o, lse)


def _mla_layer_bwd(tag, dh, dh_b, h_in, saved, g_mix, ws, qn, kvn, cos, sin, *, tm, tq, n_heads, scale, n_real, early=None,
                   after=None):
    hn, proj, cq, ckv, kr, q, kv, o, lse = saved
    w_in, w_uq, w_ukv, w_o = _rows_natural(ws[0]), ws[1], ws[2], _rows_natural(ws[3])
    t_all, d = h_in.shape
    lq, lkv = qn.shape[1], kvn.shape[1]
    ov = o.shape[1]
    tmb = _pick(t_all, _ROW_TILES)
    tn_d, tk_d = _pick(d, _DIVS[1:]), _pick(d, _DIVS)
    do = _mm_nt(dh_b, w_o, name=f"mla_do{tag}", out_dtype=BF16, tm=tmb, tn=_pick(ov, _DIVS[1:]), tk=tk_d, after=after)
    dw_o = _mm_tn(o, dh_b, name=f"mla_dwo{tag}", out_dtype=BF16, tm=_pick(ov, _DIVS[2:]), tn=tn_d, tk=t_all)
    dq, dkv, dkr_h = _attn_bwd(q, kv, kr, o, lse, do, cos, sin, name=f"attn_bwd{tag}", n_heads=n_heads, tq=tq, n_real=n_real,
                               scale=scale)
    hw, kw = w_uq.shape[2], w_ukv.shape[2]
    dw_uq = _mm_tn(cq, dq, name=f"mla_dwuq{tag}", out_dtype=BF16, tm=lq, tn=hw, tk=t_all, out_block=hw)
    dcq = _mm_nt(dq, w_uq, name=f"mla_dcq{tag}", out_dtype=F32, tm=tm, tn=lq, tk=dq.shape[1], b_blocked=True)
    dw_ukv = _mm_tn(ckv, dkv, name=f"mla_dwukv{tag}", out_dtype=BF16, tm=lkv, tn=kw, tk=t_all, out_block=kw)
    dckv = _mm_nt(dkv, w_ukv, name=f"mla_dckv{tag}", out_dtype=F32, tm=tm, tn=lkv, tk=dkv.shape[1], b_blocked=True)
    first = [dw_uq, dw_ukv, dw_o.reshape(N_DEV, -1, d)]
    if early is not None:
        qn = qn + early(first)
    dproj, dqn, dkvn = _mla_prep_bwd(dcq, dckv, dkr_h, proj, qn, kvn, cos, sin, name=f"mla_prep_bwd{tag}", tm=tm, lq=lq, lkv=lkv)
    wc = w_in.shape[1]
    dw_in = _mm_tn(hn, dproj, name=f"mla_dwin{tag}", out_dtype=BF16, tm=_pick(d, _DIVS[2:]), tn=wc, tk=t_all)
    dhn = _mm_nt(dproj, w_in, name=f"mla_dhn{tag}", out_dtype=BF16, tm=tmb, tn=tn_d, tk=wc)
    dh, dh_b, dg = _rmsnorm_bwd(dhn, h_in, g_mix, dh, name=f"norm_mix_bwd{tag}", tm=tm)
    return dh, dh_b, dg, dqn, dkvn, [dw_in.reshape(N_DEV, -1, wc)] + ([] if early is not None else first)


def _lru_layer_fwd(tag, h, g_mix, ws, small, *, tm):
    w_lin, w_lo = ws[0], _rows_natural(ws[1])
    t_all, d = h.shape
    dr = w_lo.shape[0]
    tmb = _pick(t_all, _ROW_TILES)
    hn = _rmsnorm_fwd(h, g_mix, name=f"norm_mix{tag}", tm=_pick(t_all, _ROW_TILES))
    xy = _mm_nn(hn, w_lin, name=f"lru_in{tag}", out_dtype=F32, tm=tmb, tn=w_lin.shape[2], tk=_pick(d, _DIVS), b_blocked=True,
                rows_outer=True)
    hs, hsy = _lru_fwd(xy, *small, name=f"lru_fwd{tag}")
    h_mid = _mm_nn(hsy, w_lo, name=f"lru_o{tag}", out_dtype=F32, tm=tm, tn=d, tk=dr, res=h)
    return h_mid, (hn, xy, hs, hsy)


def _lru_layer_bwd(tag, dh, dh_b, h_in, saved, g_mix, ws, small, *, tm, after=None):
    hn, xy, hs, hsy = saved
    w_lin, w_lo = ws[0], _rows_natural(ws[1])
    t_all, d = h_in.shape
    dr = w_lo.shape[0]
    tmb = _pick(t_all, _ROW_TILES)
    tn_d, tk_d = _pick(d, _DIVS[1:]), _pick(d, _DIVS)
    dhsy = _mm_nt(dh_b, w_lo, name=f"lru_dhsy{tag}", out_dtype=F32, tm=tmb, tn=_pick(dr, _DIVS[1:]), tk=tk_d, after=after)
    dw_lo = _mm_tn(hsy, dh_b, name=f"lru_dwo{tag}", out_dtype=BF16, tm=_pick(dr, _DIVS[2:]), tn=tn_d, tk=t_all)
    dxy, *dsmall = _lru_bwd(xy, hs, dhsy, *small, name=f"lru_bwd{tag}")
    lw = w_lin.shape[2]
    dw_lin = _mm_tn(hn, dxy, name=f"lru_dwin{tag}", out_dtype=BF16, tm=tn_d, tn=lw, tk=t_all, out_block=lw)
    dhn = _mm_nt(dxy, w_lin, name=f"lru_dhn{tag}", out_dtype=BF16, tm=tm, tn=tn_d, tk=2 * dr, b_blocked=True)
    dh, dh_b, dg = _rmsnorm_bwd(dhn, h_in, g_mix, dh, name=f"norm_mix_bwd{tag}", tm=tm)
    return dh, dh_b, dg, tuple(dsmall), [dw_lin, dw_lo.reshape(N_DEV, -1, d)]


def _ffn_layer_fwd(tag, h_mid, g_ffn, ws, *, tm):
    w_gu, w_down = ws[0], _rows_natural(ws[1])
    t_all, d = h_mid.shape
    f_all = w_down.shape[0]
    tmb = _pick(t_all, _ROW_TILES)
    fk = _pick(f_all, (1408,) + _DIVS[1:])
    hn2 = _rmsnorm_fwd(h_mid, g_ffn, name=f"norm_ffn{tag}", tm=_pick(t_all, _ROW_TILES))
    gu, act = _ffn_up(hn2, w_gu, name=f"ffn_up{tag}", tm=_pick(t_all, (704, 384, 256, 128)))
    h_out = _mm_nn(act, w_down, name=f"ffn_down{tag}", out_dtype=F32, tm=tm, tn=_pick(d, _DIVS[1:]), tk=f_all, res=h_mid)
    return h_out, (hn2, gu, act)


def _ffn_layer_bwd(tag, dh, dh_b, h_mid, saved, g_ffn, ws, *, tm, after=None):
    hn2, gu, act = saved
    w_gu, w_down = ws[0], _rows_natural(ws[1])
    t_all, d = h_mid.shape
    f_all = w_down.shape[0]
    f_local = w_gu.shape[2]
    tmb = _pick(t_all, _ROW_TILES)
    fk = _pick(f_all, (1408,) + _DIVS[1:])
    tn_d, tk_d = _pick(d, _DIVS[1:]), _pick(d, _DIVS)
    dgu = _ffn_dact(dh_b, w_down, gu, name=f"ffn_dact{tag}", tm=_pick(t_all, (704, 384, 256, 128)), tn=f_local, after=after)
    dw_down = _mm_tn(act, dh_b, name=f"ffn_dwdown{tag}", out_dtype=BF16, tm=fk, tn=_pick(d, _DIVS[2:]), tk=t_all)
    dhn2 = _mm_nt(dgu, w_gu, name=f"ffn_dhn{tag}", out_dtype=BF16, tm=tm, tn=_pick(d, _DIVS[2:]), tk=2 * f_all, b_blocked=True)
    dw_gu = _mm_tn(hn2, dgu, name=f"ffn_dwgu{tag}", out_dtype=BF16, tm=_pick(d, _DIVS[2:]), tn=f_local, tk=t_all, out_block=f_local,
                   cols_outer=True)
    dh, dh_b, dg = _rmsnorm_bwd(dhn2, h_mid, g_ffn, dh, name=f"norm_ffn_bwd{tag}", tm=tm)
    return dh, dh_b, dg, [dw_gu, dw_down.reshape(N_DEV, -1, d)]


def kernel(x, meta_tokens, norm_mix, norm_ffn, norm_final, mla_w_in, mla_q_norm, mla_kv_norm, mla_w_uq, mla_w_ukv, mla_w_o, lru_w_in, lru_conv_w, lru_conv_b, lru_w_gate_a, lru_b_gate_a, lru_w_gate_x, lru_b_gate_x, lru_lambda, lru_w_o, ffn_w_gu, ffn_w_down, loss_target, m_meta_tokens, m_norm_mix, m_norm_ffn, m_norm_final, m_mla_w_in, m_mla_q_norm, m_mla_kv_norm, m_mla_w_uq, m_mla_w_ukv, m_mla_w_o, m_lru_w_in, m_lru_conv_w, m_lru_conv_b, m_lru_w_gate_a, m_lru_b_gate_a, m_lru_w_gate_x, m_lru_b_gate_x, m_lru_lambda, m_lru_w_o, m_ffn_w_gu, m_ffn_w_down, v_meta_tokens, v_norm_mix, v_norm_ffn, v_norm_final, v_mla_w_in, v_mla_q_norm, v_mla_kv_norm, v_mla_w_uq, v_mla_w_ukv, v_mla_w_o, v_lru_w_in, v_lru_conv_w, v_lru_conv_b, v_lru_w_gate_a, v_lru_b_gate_a, v_lru_w_gate_x, v_lru_b_gate_x, v_lru_lambda, v_lru_w_o, v_ffn_w_gu, v_ffn_w_down):
    seq, d = x.shape[1], x.shape[2]
    assert seq % CHUNK == 0
    n_real = N_META + seq
    t_all = -(-n_real // LANES) * LANES
    tm = _pick(t_all, (384, 256, 128))
    tq = _pick(seq, (512, 256, 128, 64))
    depth = norm_mix.shape[0]
    n_mla, n_lru = mla_w_in.shape[0], lru_w_in.shape[0]
    lq, lkv = mla_q_norm.shape[1], mla_kv_norm.shape[1]
    w_in_cols = lq + lkv + LANES
    heads_local = mla_w_uq.shape[2] // (QK_NOPE + QK_ROPE)
    n_heads = heads_local * N_DEV
    dr = lru_w_gate_a.shape[1] * lru_w_gate_a.shape[2]
    scale = (QK_NOPE + QK_ROPE) ** -0.5
    cx, cy, cc = _mesh_pos()
    core = jnp.reshape(cc, (1,)).astype(jnp.int32)
    my_slot = jnp.reshape(4 * cx + 2 * cy + cc, (1,)).astype(jnp.int32)

    def pad_cols(w, cols):
        return jnp.pad(w, ((0, 0), (0, cols - w.shape[1])))

    def pad_heads(w):
        k_all = w.shape[0]
        w3 = w.reshape(k_all, heads_local, QK_NOPE + QK_ROPE)
        return jnp.pad(w3, ((0, 0), (0, 0), (0, HEAD_W - QK_NOPE - QK_ROPE))).reshape(k_all, heads_local * HEAD_W)

    def unpad_heads(w):
        k_all = w.shape[0]
        return w.reshape(k_all, heads_local, HEAD_W)[:, :, :QK_NOPE + QK_ROPE].reshape(k_all, -1)

    small_rows = N_META + n_lru * 4 + 2 * n_lru
    small_pad = -(-small_rows // SUBLANES) * SUBLANES

    def pack_small(meta, conv_w, conv_b, lam):
        rows = jnp.concatenate([meta, conv_w.reshape(n_lru * 4, -1), conv_b, lam], axis=0)
        return jnp.pad(rows, ((0, small_pad - small_rows), (0, 0)))

    def unpack_small(p):
        o1 = N_META + n_lru * 4
        return (p[:N_META], p[N_META:o1].reshape(n_lru, 4, -1), p[o1:o1 + n_lru], p[o1 + n_lru:o1 + 2 * n_lru])

    (small_full,), small_done = _all_gather([pack_small(meta_tokens, lru_conv_w, lru_conv_b, lru_lambda)], name="ag_small")
    small_full = jnp.transpose(small_full, (1, 0, 2)).reshape(small_pad, -1)
    meta_full, conv_w_full, conv_b_full, lam_full = unpack_small(small_full)

    def wire(w):
        return (w + small_done).astype(BF16)

    mla_shards, lru_shards, ffn_shards = [], [], []
    for j in range(n_mla):
        mla_shards.append([wire(pad_cols(mla_w_in[j], w_in_cols)), wire(pad_heads(mla_w_uq[j])), wire(mla_w_ukv[j]),
                           wire(mla_w_o[j])])
    for j in range(n_lru):
        lru_shards.append([wire(lru_w_in[j]), wire(lru_w_o[j])])
    for layer in range(depth):
        ffn_shards.append([wire(ffn_w_gu[layer]), wire(ffn_w_down[layer])])

    n_sub = 2 * depth
    groups = []
    for layer in range(depth):
        groups += [mla_shards[layer // 2] if layer % 2 == 0 else lru_shards[layer // 2], ffn_shards[layer]]
    slot_idx = 4 * cx + 2 * cy + cc
    ag_own = []
    for gi, shards in enumerate(groups):
        lands = [lax.dynamic_update_slice(lax.empty((N_DEV,) + s.shape, s.dtype), s[None], (slot_idx, 0, 0)) for s in shards]
        ag_own.append(_exchange_start(shards, lands, _ag_plan_own, 4, name=f"ag{gi}_start"))
    ag_pass = [None] * n_sub
    weights = [None] * n_sub

    def ag_landed(gi, after):
        _, lands = _exchange_wait(ag_own[gi], _ag_plan_own, after, name=f"ag{gi}_wait")
        ag_pass[gi] = _exchange_start([], lands, _ag_plan_pass, 3, name=f"ag{gi}_pass")
        return ag_pass[gi][4][0, 0]

    def ag_done(gi, after):
        _, weights[gi] = _exchange_wait(ag_pass[gi], _ag_plan_pass, after, name=f"ag{gi}_pass_wait")

    cos, sin = _rope_tables(t_all)
    zeros_tail = jnp.zeros((t_all - n_real, d), F32)
    started = ag_own[0][4][0, 0]
    for st in ag_own[1:]:
        started = started + st[4][0, 0]
    h = jnp.concatenate([meta_full + started, x[0], zeros_tail], axis=0)
    target = jnp.concatenate([jnp.zeros((N_META, d), F32), loss_target[0], zeros_tail], axis=0)

    attn_kw = dict(tm=tm, tq=tq, n_heads=n_heads, scale=scale, n_real=n_real)

    def lru_small(j):
        return (conv_w_full[j], conv_b_full[j][None, :], lru_w_gate_a[j].astype(BF16), lru_b_gate_a[j].reshape(1, dr),
                lru_w_gate_x[j].astype(BF16), lru_b_gate_x[j].reshape(1, dr), lam_full[j][None, :])

    def before_sublayer(k, act):
        tok = ag_landed(k, act) if k <= 1 else 0.0
        ag_done(k, act)
        if 1 <= k < n_sub - 1:
            tok = tok + ag_landed(k + 1, act)
        return tok

    saved = []
    for layer in range(depth):
        j = layer // 2
        g_mix = norm_mix[layer][None, :] + before_sublayer(2 * layer, h)
        if layer % 2 == 0:
            h_mid, mix_saved = _mla_layer_fwd(layer, h, g_mix, weights[2 * layer], mla_q_norm[j][None, :],
                                              mla_kv_norm[j][None, :], cos, sin, **attn_kw)
        else:
            h_mid, mix_saved = _lru_layer_fwd(layer, h, g_mix, weights[2 * layer], lru_small(j), tm=tm)
        g_ffn = norm_ffn[layer][None, :] + before_sublayer(2 * layer + 1, h_mid)
        h_out, ffn_saved = _ffn_layer_fwd(layer, h_mid, g_ffn, weights[2 * layer + 1], tm=tm)
        saved.append((h, h_mid, mix_saved, ffn_saved))
        h = h_out

    loss_part, dh, dh_b, dg_final = _loss_head(h, target, norm_final[None, :], name="loss_head", tm=tm, n_real=n_real)
    loss = lax.psum(loss_part[0, 0], ("x", "y", "c"))

    rs_sib, rs_chip, reduced = [None] * (n_sub + 1), [None] * (n_sub + 1), [None] * (n_sub + 1)
    chip_idx = jnp.reshape(2 * cx + cy, (1,)).astype(jnp.int32)

    def rs_begin(k, grads):
        lands = [lax.empty((4,) + g.shape[1:], g.dtype) for g in grads]
        rs_sib[k] = _exchange_start(grads, lands, _rs_plan_sibling, 4, name=f"rs{k}_start")
        return rs_sib[k][4]

    def rs_middle(k, after):
        grads, landed = _exchange_wait(rs_sib[k], _rs_plan_sibling, after, name=f"rs{k}_wait")
        parts = [_pair_add(g, l, core, name=f"rs{k}_add{a}", tr=_adam_row_tile(g.shape[1], g.shape[2], 4 * 1024 * 1024))
                 for a, (g, l) in enumerate(zip(grads, landed))]
        lands = [lax.empty((3,) + p.shape[1:], p.dtype) for p in parts]
        rs_chip[k] = _exchange_start(parts, lands, _rs_plan_chips, 3, name=f"rs{k}_chips")
        return rs_chip[k][4]

    def rs_end(k, after):
        reduced[k] = _exchange_wait(rs_chip[k], _rs_plan_chips, after, name=f"rs{k}_chips_wait")

    d_norm_mix, d_norm_ffn = [None] * depth, [None] * depth
    d_qn, d_kvn = [None] * n_mla, [None] * n_mla
    d_small = {k: [None] * n_lru for k in ("cw", "cb", "wga", "bga", "wgx", "bgx", "lam")}
    tok, waiting = None, None
    gate_own = [None] * n_lru
    for layer in reversed(range(depth)):
        j = layer // 2
        h_in, h_mid, mix_saved, ffn_saved = saved[layer]
        dh, dh_b, d_norm_ffn[layer], ffn_g = _ffn_layer_bwd(layer, dh, dh_b, h_mid, ffn_saved, norm_ffn[layer][None, :],
                                                            weights[2 * layer + 1], tm=tm, after=tok)
        tok = rs_begin(2 * layer + 1, ffn_g)
        if waiting is not None:
            tok = tok + rs_middle(waiting, dh)
        waiting = 2 * layer + 1
        if layer == 0:
            tok = tok + rs_middle(waiting, dh)
            waiting = None
        g_mix = norm_mix[layer][None, :]
        if layer % 2 == 0:
            early = (lambda g: (rs_begin(n_sub, g) + rs_middle(n_sub, g[0]))[0, 0]) if layer == 0 else None
            dh, dh_b, d_norm_mix[layer], d_qn[j], d_kvn[j], mix_g = _mla_layer_bwd(
                layer, dh, dh_b, h_in, mix_saved, g_mix, weights[2 * layer], mla_q_norm[j][None, :], mla_kv_norm[j][None, :],
                cos, sin, early=early, after=tok, **attn_kw)
            tok = rs_begin(2 * layer, mix_g)
        else:
            dh, dh_b, d_norm_mix[layer], dsmall, mix_g = _lru_layer_bwd(layer, dh, dh_b, h_in, mix_saved, g_mix,
                                                                        weights[2 * layer], lru_small(j), tm=tm, after=tok)
            for key, val in zip(("cw", "cb", "wga", "bga", "wgx", "bgx", "lam"), dsmall):
                d_small[key][j] = val
            gates = [d_small["wga"][j].reshape(-1, LANES), d_small["wgx"][j].reshape(-1, LANES)]
            gate_lands = [lax.dynamic_update_slice(lax.empty((N_DEV,) + g.shape, g.dtype), g[None], (slot_idx, 0, 0)) for g in gates]
            gate_own[j] = _exchange_start(gates, gate_lands, _ag_plan_own, 4, name=f"ag_gates{j}_start")
            tok = rs_begin(2 * layer, mix_g) + gate_own[j][4]
        if waiting is not None:
            tok = tok + rs_middle(waiting, dh)
        waiting = 2 * layer
    rs_middle(waiting, dh)

    grad_x = dh[N_META:n_real][None]

    d_meta = dh[:N_META]
    small_grad = pack_small(d_meta, jnp.stack(d_small["cw"], axis=0), jnp.concatenate(d_small["cb"], axis=0),
                            jnp.concatenate(d_small["lam"], axis=0))
    rep_grads = [
        jnp.concatenate(d_norm_mix, axis=0), jnp.concatenate(d_norm_ffn, axis=0), dg_final,
        jnp.concatenate(d_qn, axis=0), jnp.concatenate(d_kvn, axis=0),
        jnp.concatenate(d_small["bga"], axis=0), jnp.concatenate(d_small["bgx"], axis=0),
    ]
    small_srcs = [small_grad] + [jnp.pad(g, ((0, -g.shape[0] % SUBLANES), (0, 0))) for g in rep_grads]
    small_lands = [lax.dynamic_update_slice(lax.empty((N_DEV,) + s.shape, s.dtype), s[None], (slot_idx, 0, 0))
                   for s in small_srcs]
    small_own = _exchange_start(small_srcs, small_lands, _ag_plan_own, 4, name="ag_grads_start")

    res = {}

    def adam_sharded(nm, k, a, idx, n_layers, w, m, v):
        parts, landed = reduced[k]
        r_all, c_all = landed[a].shape[1], landed[a].shape[2]
        res[nm] = _adamw(w.reshape(r_all, c_all), m.reshape(r_all, c_all), v.reshape(r_all, c_all), landed[a], (0, 1, 2),
                         name=f"adamw_{nm}{idx}", tr=_adam_row_tile(r_all, c_all, 2 * 1024 * 1024), own=(parts[a], chip_idx),
                         stack=(idx, n_layers, res.get(nm)))

    after = small_own[4]
    for k in reversed(range(n_sub)):
        rs_end(k, after)
        if k == 0:
            rs_end(n_sub, after)
            reduced[0] = tuple(first + rest for first, rest in zip(reduced[0], reduced[n_sub]))
        layer, j = k // 2, k // 4
        if k % 2 == 1:
            adam_sharded("ffn_w_gu", k, 0, layer, depth, ffn_w_gu[layer], m_ffn_w_gu[layer], v_ffn_w_gu[layer])
            adam_sharded("ffn_w_down", k, 1, layer, depth, ffn_w_down[layer], m_ffn_w_down[layer], v_ffn_w_down[layer])
            after = res["ffn_w_down"][0]
        elif layer % 2 == 0:
            adam_sharded("mla_w_in", k, 0, j, n_mla, pad_cols(mla_w_in[j], w_in_cols), pad_cols(m_mla_w_in[j], w_in_cols),
                         pad_cols(v_mla_w_in[j], w_in_cols))
            adam_sharded("mla_w_uq", k, 1, j, n_mla, pad_heads(mla_w_uq[j]), pad_heads(m_mla_w_uq[j]), pad_heads(v_mla_w_uq[j]))
            adam_sharded("mla_w_ukv", k, 2, j, n_mla, mla_w_ukv[j], m_mla_w_ukv[j], v_mla_w_ukv[j])
            adam_sharded("mla_w_o", k, 3, j, n_mla, mla_w_o[j], m_mla_w_o[j], v_mla_w_o[j])
            after = res["mla_w_o"][0]
        else:
            adam_sharded("lru_w_in", k, 0, j, n_lru, lru_w_in[j], m_lru_w_in[j], v_lru_w_in[j])
            adam_sharded("lru_w_o", k, 1, j, n_lru, lru_w_o[j], m_lru_w_o[j], v_lru_w_o[j])
            after = res["lru_w_o"][0]
    res["mla_w_in"] = [t[:, :, :lq + lkv + QK_ROPE] for t in res["mla_w_in"]]
    res["mla_w_uq"] = [t.reshape(n_mla, lq, heads_local, HEAD_W)[:, :, :, :QK_NOPE + QK_ROPE].reshape(n_mla, lq, -1)
                       for t in res["mla_w_uq"]]

    _, small_lands = _exchange_wait(small_own, _ag_plan_own, after, name="ag_grads_wait")
    small_pass = _exchange_start([], small_lands, _ag_plan_pass, 3, name="ag_grads_pass")
    _, all_small = _exchange_wait(small_pass, _ag_plan_pass, after, name="ag_grads_pass_wait")
    gate_terms = []
    for j in range(n_lru):
        _, lands = _exchange_wait(gate_own[j], _ag_plan_own, after, name=f"ag_gates{j}_wait")
        gate_pass = _exchange_start([], lands, _ag_plan_pass, 3, name=f"ag_gates{j}_pass")
        gate_terms.append(_exchange_wait(gate_pass, _ag_plan_pass, after, name=f"ag_gates{j}_pass_wait")[1])
    wga_terms = jnp.concatenate([t[0] for t in gate_terms], axis=1)
    wgx_terms = jnp.concatenate([t[1] for t in gate_terms], axis=1)
    slot_order = tuple(range(N_DEV))

    def adam_rep(terms, w, m, v, tag):
        r_pad, c_all = terms.shape[1], terms.shape[2]

        def prep(t):
            t2 = t.reshape(-1, c_all)
            return jnp.pad(t2, ((0, r_pad - t2.shape[0]), (0, 0)))

        outs = _adamw(prep(w), prep(m), prep(v), terms, slot_order, name=f"adamw_{tag}", tr=_adam_row_tile(r_pad, c_all))
        n_rows = w.size // c_all
        return [o[:n_rows].reshape(w.shape) for o in outs]

    small_w = pack_small(meta_tokens, lru_conv_w, lru_conv_b, lru_lambda)
    small_m = pack_small(m_meta_tokens, m_lru_conv_w, m_lru_conv_b, m_lru_lambda)
    small_v = pack_small(v_meta_tokens, v_lru_conv_w, v_lru_conv_b, v_lru_lambda)
    small_out = _adamw(small_w, small_m, small_v, all_small[0], slot_order, name="adamw_small", tr=small_pad, col_block=my_slot)
    small_out = [unpack_small(o) for o in small_out]
    for idx, key in enumerate(("meta_tokens", "lru_conv_w", "lru_conv_b", "lru_lambda")):
        res[key] = [small_out[k][idx] for k in range(4)]

    res["norm_mix"] = adam_rep(all_small[1], norm_mix, m_norm_mix, v_norm_mix, "norm_mix")
    res["norm_ffn"] = adam_rep(all_small[2], norm_ffn, m_norm_ffn, v_norm_ffn, "norm_ffn")
    res["norm_final"] = adam_rep(all_small[3], norm_final, m_norm_final, v_norm_final, "norm_final")
    res["mla_q_norm"] = adam_rep(all_small[4], mla_q_norm, m_mla_q_norm, v_mla_q_norm, "mla_q_norm")
    res["mla_kv_norm"] = adam_rep(all_small[5], mla_kv_norm, m_mla_kv_norm, v_mla_kv_norm, "mla_kv_norm")
    res["lru_w_gate_a"] = adam_rep(wga_terms, lru_w_gate_a, m_lru_w_gate_a, v_lru_w_gate_a, "lru_w_gate_a")
    res["lru_b_gate_a"] = adam_rep(all_small[6], lru_b_gate_a, m_lru_b_gate_a, v_lru_b_gate_a, "lru_b_gate_a")
    res["lru_w_gate_x"] = adam_rep(wgx_terms, lru_w_gate_x, m_lru_w_gate_x, v_lru_w_gate_x, "lru_w_gate_x")
    res["lru_b_gate_x"] = adam_rep(all_small[7], lru_b_gate_x, m_lru_b_gate_x, v_lru_b_gate_x, "lru_b_gate_x")

    names = ["meta_tokens", "norm_mix", "norm_ffn", "norm_final", "mla_w_in", "mla_q_norm", "mla_kv_norm", "mla_w_uq",
             "mla_w_ukv", "mla_w_o", "lru_w_in", "lru_conv_w", "lru_conv_b", "lru_w_gate_a", "lru_b_gate_a", "lru_w_gate_x",
             "lru_b_gate_x", "lru_lambda", "lru_w_o", "ffn_w_gu", "ffn_w_down"]
    shapes = dict(meta_tokens=meta_tokens, norm_mix=norm_mix, norm_ffn=norm_ffn, norm_final=norm_final, mla_w_in=mla_w_in,
                  mla_q_norm=mla_q_norm, mla_kv_norm=mla_kv_norm, mla_w_uq=mla_w_uq, mla_w_ukv=mla_w_ukv, mla_w_o=mla_w_o,
                  lru_w_in=lru_w_in, lru_conv_w=lru_conv_w, lru_conv_b=lru_conv_b, lru_w_gate_a=lru_w_gate_a,
                  lru_b_gate_a=lru_b_gate_a, lru_w_gate_x=lru_w_gate_x, lru_b_gate_x=lru_b_gate_x, lru_lambda=lru_lambda,
                  lru_w_o=lru_w_o, ffn_w_gu=ffn_w_gu, ffn_w_down=ffn_w_down)
    outs = [loss, grad_x]
    for k in range(4):
        outs += [res[nm][k].reshape(shapes[nm].shape) for nm in names]
    return tuple(outs)
```
